```python
import math
import jax, jax.numpy as jnp
from jax import lax
import numpy as np


D_MODEL = 2048
BATCH = 8
SEQ = 4096
DEPTH = 1

N_META = 16
POOL_WINDOWS = (2, 4, 8, 16)
POOL_GROUP = 256
POOL_WIDTH = POOL_GROUP * len(POOL_WINDOWS)
MLA_HEADS = 16
Q_LORA = 512
KV_LORA = 512
QK_NOPE = 128
QK_ROPE = 64
V_DIM = 128
QK_DIM = QK_NOPE + QK_ROPE
MLA_WIDTH = MLA_HEADS * V_DIM
ROPE_THETA = 10000.0
SOFTMAX_SCALE = QK_DIM ** -0.5
D_FF = 5632
Q_BLOCK = 128
EPS = 1e-6
SPLITS = (POOL_WIDTH,
          POOL_WIDTH + Q_LORA,
          POOL_WIDTH + Q_LORA + KV_LORA,
          POOL_WIDTH + Q_LORA + KV_LORA + QK_ROPE,
          POOL_WIDTH + Q_LORA + KV_LORA + QK_ROPE + D_MODEL)
IN_COLS = POOL_WIDTH + Q_LORA + KV_LORA + QK_ROPE + 2 * D_MODEL

kernel_name = 'hybrid_pool_mla_macaron_block'


def _rmsnorm(x, gain):
    x32 = x.astype(jnp.float32)
    y = x32 * lax.rsqrt(jnp.mean(x32 * x32, axis=-1, keepdims=True) + EPS)
    return (y * gain.astype(jnp.float32)).astype(x.dtype)


def _swiglu(h, w_gu, w_down):
    g, u = jnp.split(h @ w_gu, 2, axis=-1)
    return (jax.nn.silu(g) * u) @ w_down


def _rope_tables(L, dtype):
    pos = jnp.arange(L, dtype=jnp.float32)
    inv = ROPE_THETA ** (-jnp.arange(0, QK_ROPE, 2, dtype=jnp.float32) / QK_ROPE)
    ang = pos[:, None] * inv[None, :]
    ang = jnp.concatenate([ang, ang], axis=-1)
    return jnp.cos(ang).astype(dtype), jnp.sin(ang).astype(dtype)


def _rotate(x, cos, sin):
    x1, x2 = jnp.split(x, 2, axis=-1)
    return x * cos + jnp.concatenate([-x2, x1], axis=-1) * sin


def _multiscale_pool(u, pool_w, pool_scale):
    B, L, _ = u.shape
    u32 = u.astype(jnp.float32)
    cs = jnp.concatenate([jnp.zeros_like(u32[:, :1]), jnp.cumsum(u32, axis=1)], axis=1)
    hi = jnp.arange(1, L + 1)
    outs = []
    for g, w in enumerate(POOL_WINDOWS):
        csg = cs[..., g * POOL_GROUP:(g + 1) * POOL_GROUP]
        lo = jnp.maximum(hi - w, 0)
        cnt = (hi - lo).astype(jnp.float32)[None, :, None]
        mean = (csg[:, hi] - csg[:, lo]) / cnt
        outs.append(mean - u32[..., g * POOL_GROUP:(g + 1) * POOL_GROUP])
    d = jnp.stack(outs, axis=2).astype(u.dtype)
    y = jnp.einsum('blgc,gcd->blgd', d, pool_w).reshape(B, L, POOL_WIDTH)
    return y * pool_scale


def _attend_block(q_blk, q_pos, k, v, k_pos):
    s = jnp.einsum('bqhd,bkhd->bhqk', q_blk, k, preferred_element_type=jnp.float32) * SOFTMAX_SCALE
    mask = k_pos[None, :] <= q_pos[:, None]
    s = jnp.where(mask[None, None], s, jnp.float32(-1e30))
    p = jax.nn.softmax(s, axis=-1).astype(v.dtype)
    return jnp.einsum('bhqk,bkhd->bqhd', p, v)


def _mla(c_q, c_kv, k_rope, q_a_norm, w_q_b, kv_a_norm, w_kv_b, cos, sin):
    B, L, _ = c_q.shape
    q = (_rmsnorm(c_q, q_a_norm) @ w_q_b).reshape(B, L, MLA_HEADS, QK_DIM)
    q_nope, q_pe = jnp.split(q, [QK_NOPE], axis=-1)
    q_pe = _rotate(q_pe, cos[:, None, :], sin[:, None, :])
    kv = (_rmsnorm(c_kv, kv_a_norm) @ w_kv_b).reshape(B, L, MLA_HEADS, QK_NOPE + V_DIM)
    k_nope, v = jnp.split(kv, [QK_NOPE], axis=-1)
    k_pe = _rotate(k_rope, cos, sin)
    q = jnp.concatenate([q_nope, q_pe], axis=-1)
    k = jnp.concatenate([k_nope, jnp.broadcast_to(k_pe[:, :, None, :], (B, L, MLA_HEADS, QK_ROPE))], axis=-1)
    pos = jnp.arange(L)
    o_meta = _attend_block(q[:, :N_META], pos[:N_META], k[:, :N_META], v[:, :N_META], pos[:N_META])
    n_blk = (L - N_META) // Q_BLOCK
    q_real = q[:, N_META:].reshape(B, n_blk, Q_BLOCK, MLA_HEADS, QK_DIM).transpose(1, 0, 2, 3, 4)
    pos_real = pos[N_META:].reshape(n_blk, Q_BLOCK)
    o_real = lax.map(lambda a: _attend_block(a[0], a[1], k, v, pos), (q_real, pos_real))
    o_real = o_real.transpose(1, 0, 2, 3, 4).reshape(B, L - N_META, MLA_HEADS, V_DIM)
    o = jnp.concatenate([o_meta, o_real], axis=1)
    return o.reshape(B, L, MLA_WIDTH)


def _hybrid_mixer(h, w_in, pool_w, pool_scale, w_pool_o, q_a_norm, w_q_b, kv_a_norm, w_kv_b,
                  w_mla_o, w_out, cos, sin):
    z = h @ w_in
    u_pool, c_q, c_kv, k_rope, g_pool, g_mla = jnp.split(z, SPLITS, axis=-1)
    y_pool = _multiscale_pool(u_pool, pool_w, pool_scale) @ w_pool_o
    y_mla = _mla(c_q, c_kv, k_rope, q_a_norm, w_q_b, kv_a_norm, w_kv_b, cos, sin) @ w_mla_o
    y = jax.nn.sigmoid(g_pool) * y_pool + jax.nn.sigmoid(g_mla) * y_mla
    return y @ w_out


def _fwd_setup_inputs(seed: int = 0) -> dict:
    key = jax.random.key(seed)
    ks = jax.random.split(key, 32)

    def dense(k, shape, fan_in):
        return jax.random.normal(k, shape, jnp.float32) * (fan_in ** -0.5)

    def gain(k, shape):
        return 1.0 + 0.1 * jax.random.normal(k, shape, jnp.float32)

    return {
        'x': jax.random.normal(ks[0], (BATCH, SEQ, D_MODEL), jnp.float32),
        'meta_tokens': jax.random.normal(ks[1], (N_META, D_MODEL), jnp.float32),
        'norm_ffn1_pre': gain(ks[2], (DEPTH, D_MODEL)),
        'norm_ffn1_post': gain(ks[3], (DEPTH, D_MODEL)),
        'ffn1_w_gu': dense(ks[4], (DEPTH, D_MODEL, 2 * D_FF), D_MODEL),
        'ffn1_w_down': dense(ks[5], (DEPTH, D_FF, D_MODEL), D_FF),
        'norm_mix_pre': gain(ks[6], (DEPTH, D_MODEL)),
        'norm_mix_post': gain(ks[7], (DEPTH, D_MODEL)),
        'w_in': dense(ks[8], (DEPTH, D_MODEL, IN_COLS), D_MODEL),
        'pool_w': dense(ks[9], (DEPTH, len(POOL_WINDOWS), POOL_GROUP, POOL_GROUP), POOL_GROUP),
        'pool_scale': gain(ks[10], (DEPTH, POOL_WIDTH)),
        'w_pool_o': dense(ks[11], (DEPTH, POOL_WIDTH, D_MODEL), POOL_WIDTH),
        'q_a_norm': gain(ks[12], (DEPTH, Q_LORA)),
        'w_q_b': dense(ks[13], (DEPTH, Q_LORA, MLA_HEADS * QK_DIM), Q_LORA),
        'kv_a_norm': gain(ks[14], (DEPTH, KV_LORA)),
        'w_kv_b': dense(ks[15], (DEPTH, KV_LORA, MLA_HEADS * (QK_NOPE + V_DIM)), KV_LORA),
        'w_mla_o': dense(ks[16], (DEPTH, MLA_WIDTH, D_MODEL), MLA_WIDTH),
        'w_out': dense(ks[17], (DEPTH, D_MODEL, D_MODEL), D_MODEL),
        'norm_ffn2_pre': gain(ks[18], (DEPTH, D_MODEL)),
        'norm_ffn2_post': gain(ks[19], (DEPTH, D_MODEL)),
        'ffn2_w_gu': dense(ks[20], (DEPTH, D_MODEL, 2 * D_FF), D_MODEL),
        'ffn2_w_down': dense(ks[21], (DEPTH, D_FF, D_MODEL), D_FF),
    }


def _fwd_reference(x, meta_tokens, norm_ffn1_pre, norm_ffn1_post, ffn1_w_gu, ffn1_w_down,
              norm_mix_pre, norm_mix_post, w_in, pool_w, pool_scale, w_pool_o,
              q_a_norm, w_q_b, kv_a_norm, w_kv_b, w_mla_o, w_out,
              norm_ffn2_pre, norm_ffn2_post, ffn2_w_gu, ffn2_w_down):
    B = x.shape[0]
    meta = jnp.broadcast_to(meta_tokens.astype(x.dtype)[None], (B, N_META, D_MODEL))
    h = jnp.concatenate([meta, x], axis=1)
    L = h.shape[1]
    cos, sin = _rope_tables(L, h.dtype)
    for i in range(DEPTH):
        h = h + 0.5 * _rmsnorm(_swiglu(_rmsnorm(h, norm_ffn1_pre[i]), ffn1_w_gu[i], ffn1_w_down[i]),
                               norm_ffn1_post[i])
        m = _hybrid_mixer(_rmsnorm(h, norm_mix_pre[i]), w_in[i], pool_w[i], pool_scale[i], w_pool_o[i],
                          q_a_norm[i], w_q_b[i], kv_a_norm[i], w_kv_b[i], w_mla_o[i], w_out[i], cos, sin)
        h = h + _rmsnorm(m, norm_mix_post[i])
        h = h + 0.5 * _rmsnorm(_swiglu(_rmsnorm(h, norm_ffn2_pre[i]), ffn2_w_gu[i], ffn2_w_down[i]),
                               norm_ffn2_post[i])
    return h[:, N_META:]


import jax as _jax
import jax.numpy as _jnp

TWIN_FORMAT = 'train_step'
FWD_PARAMS = ['x', 'meta_tokens', 'norm_ffn1_pre', 'norm_ffn1_post', 'ffn1_w_gu', 'ffn1_w_down', 'norm_mix_pre', 'norm_mix_post', 'w_in', 'pool_w', 'pool_scale', 'w_pool_o', 'q_a_norm', 'w_q_b', 'kv_a_norm', 'w_kv_b', 'w_mla_o', 'w_out', 'norm_ffn2_pre', 'norm_ffn2_post', 'ffn2_w_gu', 'ffn2_w_down']
TWIN_WEIGHTS = ['meta_tokens', 'norm_ffn1_pre', 'norm_ffn1_post', 'ffn1_w_gu', 'ffn1_w_down', 'norm_mix_pre', 'norm_mix_post', 'w_in', 'pool_w', 'pool_scale', 'w_pool_o', 'q_a_norm', 'w_q_b', 'kv_a_norm', 'w_kv_b', 'w_mla_o', 'w_out', 'norm_ffn2_pre', 'norm_ffn2_post', 'ffn2_w_gu', 'ffn2_w_down']
TWIN_DIFF_INPUT = 'x'
TWIN_INPUTS = ['x', 'meta_tokens', 'norm_ffn1_pre', 'norm_ffn1_post', 'ffn1_w_gu', 'ffn1_w_down', 'norm_mix_pre', 'norm_mix_post', 'w_in', 'pool_w', 'pool_scale', 'w_pool_o', 'q_a_norm', 'w_q_b', 'kv_a_norm', 'w_kv_b', 'w_mla_o', 'w_out', 'norm_ffn2_pre', 'norm_ffn2_post', 'ffn2_w_gu', 'ffn2_w_down', 'loss_target', 'm_meta_tokens', 'm_norm_ffn1_pre', 'm_norm_ffn1_post', 'm_ffn1_w_gu', 'm_ffn1_w_down', 'm_norm_mix_pre', 'm_norm_mix_post', 'm_w_in', 'm_pool_w', 'm_pool_scale', 'm_w_pool_o', 'm_q_a_norm', 'm_w_q_b', 'm_kv_a_norm', 'm_w_kv_b', 'm_w_mla_o', 'm_w_out', 'm_norm_ffn2_pre', 'm_norm_ffn2_post', 'm_ffn2_w_gu', 'm_ffn2_w_down', 'v_meta_tokens', 'v_norm_ffn1_pre', 'v_norm_ffn1_post', 'v_ffn1_w_gu', 'v_ffn1_w_down', 'v_norm_mix_pre', 'v_norm_mix_post', 'v_w_in', 'v_pool_w', 'v_pool_scale', 'v_w_pool_o', 'v_q_a_norm', 'v_w_q_b', 'v_kv_a_norm', 'v_w_kv_b', 'v_w_mla_o', 'v_w_out', 'v_norm_ffn2_pre', 'v_norm_ffn2_post', 'v_ffn2_w_gu', 'v_ffn2_w_down']
TWIN_OUTPUTS = ['loss', 'grad_x', 'grad_meta_tokens', 'grad_norm_ffn1_pre', 'grad_norm_ffn1_post', 'grad_ffn1_w_gu', 'grad_ffn1_w_down', 'grad_norm_mix_pre', 'grad_norm_mix_post', 'grad_w_in', 'grad_pool_w', 'grad_pool_scale', 'grad_w_pool_o', 'grad_q_a_norm', 'grad_w_q_b', 'grad_kv_a_norm', 'grad_w_kv_b', 'grad_w_mla_o', 'grad_w_out', 'grad_norm_ffn2_pre', 'grad_norm_ffn2_post', 'grad_ffn2_w_gu', 'grad_ffn2_w_down', 'delta_meta_tokens', 'delta_norm_ffn1_pre', 'delta_norm_ffn1_post', 'delta_ffn1_w_gu', 'delta_ffn1_w_down', 'delta_norm_mix_pre', 'delta_norm_mix_post', 'delta_w_in', 'delta_pool_w', 'delta_pool_scale', 'delta_w_pool_o', 'delta_q_a_norm', 'delta_w_q_b', 'delta_kv_a_norm', 'delta_w_kv_b', 'delta_w_mla_o', 'delta_w_out', 'delta_norm_ffn2_pre', 'delta_norm_ffn2_post', 'delta_ffn2_w_gu', 'delta_ffn2_w_down', 'new_m_meta_tokens', 'new_m_norm_ffn1_pre', 'new_m_norm_ffn1_post', 'new_m_ffn1_w_gu', 'new_m_ffn1_w_down', 'new_m_norm_mix_pre', 'new_m_norm_mix_post', 'new_m_w_in', 'new_m_pool_w', 'new_m_pool_scale', 'new_m_w_pool_o', 'new_m_q_a_norm', 'new_m_w_q_b', 'new_m_kv_a_norm', 'new_m_w_kv_b', 'new_m_w_mla_o', 'new_m_w_out', 'new_m_norm_ffn2_pre', 'new_m_norm_ffn2_post', 'new_m_ffn2_w_gu', 'new_m_ffn2_w_down', 'new_v_meta_tokens', 'new_v_norm_ffn1_pre', 'new_v_norm_ffn1_post', 'new_v_ffn1_w_gu', 'new_v_ffn1_w_down', 'new_v_norm_mix_pre', 'new_v_norm_mix_post', 'new_v_w_in', 'new_v_pool_w', 'new_v_pool_scale', 'new_v_w_pool_o', 'new_v_q_a_norm', 'new_v_w_q_b', 'new_v_kv_a_norm', 'new_v_w_kv_b', 'new_v_w_mla_o', 'new_v_w_out', 'new_v_norm_ffn2_pre', 'new_v_norm_ffn2_post', 'new_v_ffn2_w_gu', 'new_v_ffn2_w_down']
TWIN_LEAF_KINDS = {'loss': 'loss', 'grad_x': 'grad_x', 'grad_meta_tokens': 'grad_w', 'grad_norm_ffn1_pre': 'grad_w', 'grad_norm_ffn1_post': 'grad_w', 'grad_ffn1_w_gu': 'grad_w', 'grad_ffn1_w_down': 'grad_w', 'grad_norm_mix_pre': 'grad_w', 'grad_norm_mix_post': 'grad_w', 'grad_w_in': 'grad_w', 'grad_pool_w': 'grad_w', 'grad_pool_scale': 'grad_w', 'grad_w_pool_o': 'grad_w', 'grad_q_a_norm': 'grad_w', 'grad_w_q_b': 'grad_w', 'grad_kv_a_norm': 'grad_w', 'grad_w_kv_b': 'grad_w', 'grad_w_mla_o': 'grad_w', 'grad_w_out': 'grad_w', 'grad_norm_ffn2_pre': 'grad_w', 'grad_norm_ffn2_post': 'grad_w', 'grad_ffn2_w_gu': 'grad_w', 'grad_ffn2_w_down': 'grad_w', 'delta_meta_tokens': 'delta_w', 'delta_norm_ffn1_pre': 'delta_w', 'delta_norm_ffn1_post': 'delta_w', 'delta_ffn1_w_gu': 'delta_w', 'delta_ffn1_w_down': 'delta_w', 'delta_norm_mix_pre': 'delta_w', 'delta_norm_mix_post': 'delta_w', 'delta_w_in': 'delta_w', 'delta_pool_w': 'delta_w', 'delta_pool_scale': 'delta_w', 'delta_w_pool_o': 'delta_w', 'delta_q_a_norm': 'delta_w', 'delta_w_q_b': 'delta_w', 'delta_kv_a_norm': 'delta_w', 'delta_w_kv_b': 'delta_w', 'delta_w_mla_o': 'delta_w', 'delta_w_out': 'delta_w', 'delta_norm_ffn2_pre': 'delta_w', 'delta_norm_ffn2_post': 'delta_w', 'delta_ffn2_w_gu': 'delta_w', 'delta_ffn2_w_down': 'delta_w', 'new_m_meta_tokens': 'new_m', 'new_m_norm_ffn1_pre': 'new_m', 'new_m_norm_ffn1_post': 'new_m', 'new_m_ffn1_w_gu': 'new_m', 'new_m_ffn1_w_down': 'new_m', 'new_m_norm_mix_pre': 'new_m', 'new_m_norm_mix_post': 'new_m', 'new_m_w_in': 'new_m', 'new_m_pool_w': 'new_m', 'new_m_pool_scale': 'new_m', 'new_m_w_pool_o': 'new_m', 'new_m_q_a_norm': 'new_m', 'new_m_w_q_b': 'new_m', 'new_m_kv_a_norm': 'new_m', 'new_m_w_kv_b': 'new_m', 'new_m_w_mla_o': 'new_m', 'new_m_w_out': 'new_m', 'new_m_norm_ffn2_pre': 'new_m', 'new_m_norm_ffn2_post': 'new_m', 'new_m_ffn2_w_gu': 'new_m', 'new_m_ffn2_w_down': 'new_m', 'new_v_meta_tokens': 'new_v', 'new_v_norm_ffn1_pre': 'new_v', 'new_v_norm_ffn1_post': 'new_v', 'new_v_ffn1_w_gu': 'new_v', 'new_v_ffn1_w_down': 'new_v', 'new_v_norm_mix_pre': 'new_v', 'new_v_norm_mix_post': 'new_v', 'new_v_w_in': 'new_v', 'new_v_pool_w': 'new_v', 'new_v_pool_scale': 'new_v', 'new_v_w_pool_o': 'new_v', 'new_v_q_a_norm': 'new_v', 'new_v_w_q_b': 'new_v', 'new_v_kv_a_norm': 'new_v', 'new_v_w_kv_b': 'new_v', 'new_v_w_mla_o': 'new_v', 'new_v_w_out': 'new_v', 'new_v_norm_ffn2_pre': 'new_v', 'new_v_norm_ffn2_post': 'new_v', 'new_v_ffn2_w_gu': 'new_v', 'new_v_ffn2_w_down': 'new_v'}


def _forward(args):
    return _fwd_reference(*[args[k] for k in FWD_PARAMS])


def _output_shape():
    def fwd():
        inp = _fwd_setup_inputs(0)
        return _fwd_reference(*[inp[k] for k in FWD_PARAMS])
    out = _jax.eval_shape(fwd)
    return out.shape, out.dtype

N_MICROBATCH = 1
ADAM_LR = 0.001
ADAM_B1 = 0.9
ADAM_B2 = 0.999
ADAM_EPS = 1e-08
ADAM_WD = 0.01
ADAM_STEP = 10
PER_EXAMPLE_BATCH_AXIS = {'x': 0, 'loss_target': 0}
SHARED_INPUTS = []
_WEIGHT_DTYPES = {'meta_tokens': _jnp.float32, 'norm_ffn1_pre': _jnp.float32, 'norm_ffn1_post': _jnp.float32, 'ffn1_w_gu': _jnp.float32, 'ffn1_w_down': _jnp.float32, 'norm_mix_pre': _jnp.float32, 'norm_mix_post': _jnp.float32, 'w_in': _jnp.float32, 'pool_w': _jnp.float32, 'pool_scale': _jnp.float32, 'w_pool_o': _jnp.float32, 'q_a_norm': _jnp.float32, 'w_q_b': _jnp.float32, 'kv_a_norm': _jnp.float32, 'w_kv_b': _jnp.float32, 'w_mla_o': _jnp.float32, 'w_out': _jnp.float32, 'norm_ffn2_pre': _jnp.float32, 'norm_ffn2_post': _jnp.float32, 'ffn2_w_gu': _jnp.float32, 'ffn2_w_down': _jnp.float32}
MOMENT_SCALE = {'meta_tokens': 6.893040e-03, 'norm_ffn1_pre': 1.841782e-01, 'norm_ffn1_post': 3.989907e+00, 'ffn1_w_gu': 7.698513e-02, 'ffn1_w_down': 1.391898e-01, 'norm_mix_pre': 2.973112e-01, 'norm_mix_post': 1.610216e+01, 'w_in': 1.769822e-01, 'pool_w': 4.610129e-01, 'pool_scale': 4.773249e-01, 'w_pool_o': 3.374644e-01, 'q_a_norm': 6.290604e-02, 'w_q_b': 2.464564e-02, 'kv_a_norm': 8.580384e-02, 'w_kv_b': 2.925759e-02, 'w_mla_o': 3.276211e-02, 'w_out': 3.436916e-01, 'norm_ffn2_pre': 1.465885e-01, 'norm_ffn2_post': 4.020994e+00, 'ffn2_w_gu': 6.505383e-02, 'ffn2_w_down': 1.310647e-01}


def _to_microbatches(a, axis):
    t = _jnp.moveaxis(a, axis, 0)
    t = t.reshape((N_MICROBATCH, t.shape[0] // N_MICROBATCH) + t.shape[1:])
    return _jnp.moveaxis(t, 1, axis + 1)


def setup_inputs(seed: int = 0) -> dict:
    inp = _fwd_setup_inputs(seed)
    key = _jax.random.fold_in(_jax.random.key(seed), 7919)
    shape, _ = _output_shape()
    out = dict(inp)
    out["loss_target"] = _jax.random.normal(_jax.random.fold_in(key, 0), shape, _jnp.float32)
    for i, name in enumerate(TWIN_WEIGHTS):
        w = inp[name].astype(_jnp.float32)
        if MOMENT_SCALE is None:
            s = _jnp.sqrt(_jnp.mean(_jnp.square(w)) + 1e-30)
        else:
            s = MOMENT_SCALE[name]
        km, kv = _jax.random.split(_jax.random.fold_in(key, i + 1))
        out[name] = w
        out["m_" + name] = s * _jax.random.normal(km, w.shape, _jnp.float32)
        out["v_" + name] = (s * s) * _jax.random.uniform(kv, w.shape, _jnp.float32, 0.5, 1.5)
    if N_MICROBATCH > 1:
        for name, axis in PER_EXAMPLE_BATCH_AXIS.items():
            out[name] = _to_microbatches(out[name], axis)
    return {'x': out['x'], 'meta_tokens': out['meta_tokens'], 'norm_ffn1_pre': out['norm_ffn1_pre'], 'norm_ffn1_post': out['norm_ffn1_post'], 'ffn1_w_gu': out['ffn1_w_gu'], 'ffn1_w_down': out['ffn1_w_down'], 'norm_mix_pre': out['norm_mix_pre'], 'norm_mix_post': out['norm_mix_post'], 'w_in': out['w_in'], 'pool_w': out['pool_w'], 'pool_scale': out['pool_scale'], 'w_pool_o': out['w_pool_o'], 'q_a_norm': out['q_a_norm'], 'w_q_b': out['w_q_b'], 'kv_a_norm': out['kv_a_norm'], 'w_kv_b': out['w_kv_b'], 'w_mla_o': out['w_mla_o'], 'w_out': out['w_out'], 'norm_ffn2_pre': out['norm_ffn2_pre'], 'norm_ffn2_post': out['norm_ffn2_post'], 'ffn2_w_gu': out['ffn2_w_gu'], 'ffn2_w_down': out['ffn2_w_down'], 'loss_target': out['loss_target'], 'm_meta_tokens': out['m_meta_tokens'], 'm_norm_ffn1_pre': out['m_norm_ffn1_pre'], 'm_norm_ffn1_post': out['m_norm_ffn1_post'], 'm_ffn1_w_gu': out['m_ffn1_w_gu'], 'm_ffn1_w_down': out['m_ffn1_w_down'], 'm_norm_mix_pre': out['m_norm_mix_pre'], 'm_norm_mix_post': out['m_norm_mix_post'], 'm_w_in': out['m_w_in'], 'm_pool_w': out['m_pool_w'], 'm_pool_scale': out['m_pool_scale'], 'm_w_pool_o': out['m_w_pool_o'], 'm_q_a_norm': out['m_q_a_norm'], 'm_w_q_b': out['m_w_q_b'], 'm_kv_a_norm': out['m_kv_a_norm'], 'm_w_kv_b': out['m_w_kv_b'], 'm_w_mla_o': out['m_w_mla_o'], 'm_w_out': out['m_w_out'], 'm_norm_ffn2_pre': out['m_norm_ffn2_pre'], 'm_norm_ffn2_post': out['m_norm_ffn2_post'], 'm_ffn2_w_gu': out['m_ffn2_w_gu'], 'm_ffn2_w_down': out['m_ffn2_w_down'], 'v_meta_tokens': out['v_meta_tokens'], 'v_norm_ffn1_pre': out['v_norm_ffn1_pre'], 'v_norm_ffn1_post': out['v_norm_ffn1_post'], 'v_ffn1_w_gu': out['v_ffn1_w_gu'], 'v_ffn1_w_down': out['v_ffn1_w_down'], 'v_norm_mix_pre': out['v_norm_mix_pre'], 'v_norm_mix_post': out['v_norm_mix_post'], 'v_w_in': out['v_w_in'], 'v_pool_w': out['v_pool_w'], 'v_pool_scale': out['v_pool_scale'], 'v_w_pool_o': out['v_w_pool_o'], 'v_q_a_norm': out['v_q_a_norm'], 'v_w_q_b': out['v_w_q_b'], 'v_kv_a_norm': out['v_kv_a_norm'], 'v_w_kv_b': out['v_w_kv_b'], 'v_w_mla_o': out['v_w_mla_o'], 'v_w_out': out['v_w_out'], 'v_norm_ffn2_pre': out['v_norm_ffn2_pre'], 'v_norm_ffn2_post': out['v_norm_ffn2_post'], 'v_ffn2_w_gu': out['v_ffn2_w_gu'], 'v_ffn2_w_down': out['v_ffn2_w_down']}


def _loss(weights, diff, rest, loss_target):
    with _jax.named_scope("forward"):
        args = {**rest, TWIN_DIFF_INPUT: diff, **{k: w.astype(_WEIGHT_DTYPES[k]) for k, w in weights.items()}}
        y = _forward(args)
    with _jax.named_scope("loss_head"):
        err = _jnp.square(y.astype(_jnp.float32) - loss_target)
        return 0.5 * _jnp.sum(_jnp.mean(err, axis=-1)) if err.ndim else 0.5 * err


def _adamw(w, g, m, v):
    m = ADAM_B1 * m + (1.0 - ADAM_B1) * g
    v = ADAM_B2 * v + (1.0 - ADAM_B2) * _jnp.square(g)
    m_hat = m / (1.0 - ADAM_B1 ** ADAM_STEP)
    v_hat = v / (1.0 - ADAM_B2 ** ADAM_STEP)
    delta = -ADAM_LR * (m_hat / (_jnp.sqrt(v_hat) + ADAM_EPS) + ADAM_WD * w)
    return delta, m, v


def reference(x, meta_tokens, norm_ffn1_pre, norm_ffn1_post, ffn1_w_gu, ffn1_w_down, norm_mix_pre, norm_mix_post, w_in, pool_w, pool_scale, w_pool_o, q_a_norm, w_q_b, kv_a_norm, w_kv_b, w_mla_o, w_out, norm_ffn2_pre, norm_ffn2_post, ffn2_w_gu, ffn2_w_down, loss_target, m_meta_tokens, m_norm_ffn1_pre, m_norm_ffn1_post, m_ffn1_w_gu, m_ffn1_w_down, m_norm_mix_pre, m_norm_mix_post, m_w_in, m_pool_w, m_pool_scale, m_w_pool_o, m_q_a_norm, m_w_q_b, m_kv_a_norm, m_w_kv_b, m_w_mla_o, m_w_out, m_norm_ffn2_pre, m_norm_ffn2_post, m_ffn2_w_gu, m_ffn2_w_down, v_meta_tokens, v_norm_ffn1_pre, v_norm_ffn1_post, v_ffn1_w_gu, v_ffn1_w_down, v_norm_mix_pre, v_norm_mix_post, v_w_in, v_pool_w, v_pool_scale, v_w_pool_o, v_q_a_norm, v_w_q_b, v_kv_a_norm, v_w_kv_b, v_w_mla_o, v_w_out, v_norm_ffn2_pre, v_norm_ffn2_post, v_ffn2_w_gu, v_ffn2_w_down):
    given = dict(x=x, meta_tokens=meta_tokens, norm_ffn1_pre=norm_ffn1_pre, norm_ffn1_post=norm_ffn1_post, ffn1_w_gu=ffn1_w_gu, ffn1_w_down=ffn1_w_down, norm_mix_pre=norm_mix_pre, norm_mix_post=norm_mix_post, w_in=w_in, pool_w=pool_w, pool_scale=pool_scale, w_pool_o=w_pool_o, q_a_norm=q_a_norm, w_q_b=w_q_b, kv_a_norm=kv_a_norm, w_kv_b=w_kv_b, w_mla_o=w_mla_o, w_out=w_out, norm_ffn2_pre=norm_ffn2_pre, norm_ffn2_post=norm_ffn2_post, ffn2_w_gu=ffn2_w_gu, ffn2_w_down=ffn2_w_down, loss_target=loss_target, m_meta_tokens=m_meta_tokens, m_norm_ffn1_pre=m_norm_ffn1_pre, m_norm_ffn1_post=m_norm_ffn1_post, m_ffn1_w_gu=m_ffn1_w_gu, m_ffn1_w_down=m_ffn1_w_down, m_norm_mix_pre=m_norm_mix_pre, m_norm_mix_post=m_norm_mix_post, m_w_in=m_w_in, m_pool_w=m_pool_w, m_pool_scale=m_pool_scale, m_w_pool_o=m_w_pool_o, m_q_a_norm=m_q_a_norm, m_w_q_b=m_w_q_b, m_kv_a_norm=m_kv_a_norm, m_w_kv_b=m_w_kv_b, m_w_mla_o=m_w_mla_o, m_w_out=m_w_out, m_norm_ffn2_pre=m_norm_ffn2_pre, m_norm_ffn2_post=m_norm_ffn2_post, m_ffn2_w_gu=m_ffn2_w_gu, m_ffn2_w_down=m_ffn2_w_down, v_meta_tokens=v_meta_tokens, v_norm_ffn1_pre=v_norm_ffn1_pre, v_norm_ffn1_post=v_norm_ffn1_post, v_ffn1_w_gu=v_ffn1_w_gu, v_ffn1_w_down=v_ffn1_w_down, v_norm_mix_pre=v_norm_mix_pre, v_norm_mix_post=v_norm_mix_post, v_w_in=v_w_in, v_pool_w=v_pool_w, v_pool_scale=v_pool_scale, v_w_pool_o=v_w_pool_o, v_q_a_norm=v_q_a_norm, v_w_q_b=v_w_q_b, v_kv_a_norm=v_kv_a_norm, v_w_kv_b=v_w_kv_b, v_w_mla_o=v_w_mla_o, v_w_out=v_w_out, v_norm_ffn2_pre=v_norm_ffn2_pre, v_norm_ffn2_post=v_norm_ffn2_post, v_ffn2_w_gu=v_ffn2_w_gu, v_ffn2_w_down=v_ffn2_w_down)
    weights = {n: given[n] for n in TWIN_WEIGHTS}
    shared = {n: given[n] for n in SHARED_INPUTS}
    per_example = {n: given[n] for n in ['x']}
    grad_fn = _jax.value_and_grad(_loss, argnums=(0, 1))

    def one_microbatch(ex, loss_target):
        ex = dict(ex)
        diff = ex.pop(TWIN_DIFF_INPUT)
        return grad_fn(weights, diff, {**shared, **ex}, loss_target)

    if N_MICROBATCH == 1:
        loss, (grad_w, grad_x) = one_microbatch(per_example, given["loss_target"])
    else:
        def body(carry, xs):
            loss_sum, grad_sum = carry
            l_k, (gw_k, gx_k) = one_microbatch(xs[0], xs[1])
            with _jax.named_scope("update"):
                return (loss_sum + l_k, _jax.tree.map(_jnp.add, grad_sum, gw_k)), gx_k

        init = (_jnp.zeros((), _jnp.float32), _jax.tree.map(_jnp.zeros_like, weights))
        (loss, grad_w), grad_x = _jax.lax.scan(body, init, (per_example, given["loss_target"]))
    with _jax.named_scope("update"):
        delta_w, new_m, new_v = {}, {}, {}
        for n in TWIN_WEIGHTS:
            delta_w[n], new_m[n], new_v[n] = _adamw(weights[n], grad_w[n], given["m_" + n], given["v_" + n])
    return (loss, grad_x, *[grad_w[n] for n in TWIN_WEIGHTS], *[delta_w[n] for n in TWIN_WEIGHTS],
            *[new_m[n] for n in TWIN_WEIGHTS], *[new_v[n] for n in TWIN_WEIGHTS])
```

```python
import functools

import jax
import jax.numpy as jnp
import numpy as np
from jax import lax
from jax.experimental import pallas as pl
from jax.experimental.pallas import tpu as pltpu

F32 = jnp.float32
BF = jnp.bfloat16
MESH = pl.DeviceIdType.MESH

EPS = 1e-6
N_CHIPS = 4
POOL_WINDOWS = (2, 4, 8, 16)
POOL_GROUP = 256
POOL_WIDTH = POOL_GROUP * len(POOL_WINDOWS)
QK_NOPE = 128
QK_ROPE = 64
V_DIM = 128
QK_DIM = QK_NOPE + QK_ROPE
HEAD_PAD = 256
ROPE_THETA = 10000.0
SOFTMAX_SCALE = QK_DIM ** -0.5
ADAM_LR = 0.001
ADAM_B1 = 0.9
ADAM_B2 = 0.999
ADAM_EPS = 1e-08
ADAM_WD = 0.01
ADAM_STEP = 10
ROW_ALIGN = 256
VMEM_LIMIT = 48 * 1024 * 1024
NEG = -1e30

WEIGHTS = ['meta_tokens', 'norm_ffn1_pre', 'norm_ffn1_post', 'ffn1_w_gu', 'ffn1_w_down', 'norm_mix_pre',
           'norm_mix_post', 'w_in', 'pool_w', 'pool_scale', 'w_pool_o', 'q_a_norm', 'w_q_b', 'kv_a_norm', 'w_kv_b',
           'w_mla_o', 'w_out', 'norm_ffn2_pre', 'norm_ffn2_post', 'ffn2_w_gu', 'ffn2_w_down']
BIG = {'ffn1_w_gu': 'col', 'ffn1_w_down': 'row', 'w_in': 'lead', 'pool_w': 'pool', 'w_pool_o': 'col', 'w_q_b': 'col',
       'w_kv_b': 'col', 'w_mla_o': 'row', 'w_out': 'row', 'ffn2_w_gu': 'col', 'ffn2_w_down': 'row'}
SMALL_VEC = ['norm_ffn1_pre', 'norm_ffn1_post', 'norm_mix_pre', 'norm_mix_post', 'norm_ffn2_pre', 'norm_ffn2_post',
             'pool_scale', 'q_a_norm', 'kv_a_norm']


def _div_tile(n, target, align):
    best = None
    for t in range(align, min(n, target) + 1, align):
        if n % t == 0:
            best = t
    return best if best is not None else n


def _cp(sem=None):
    return pltpu.CompilerParams(dimension_semantics=sem, vmem_limit_bytes=VMEM_LIMIT)


def _full_shape(kind, slab):
    if kind == 'col':
        return (slab[0], slab[1] * N_CHIPS)
    if kind == 'row':
        return (slab[0] * N_CHIPS, slab[1])
    if kind == 'lead':
        return (N_CHIPS,) + tuple(slab)
    return (slab[0], slab[1] * N_CHIPS, slab[2])


def _half_shape(kind, slab):
    if kind == 'pool':
        return (slab[0], slab[1] // 2, slab[2])
    return (slab[0] // 2, slab[1])


def _half_of_slab(ref, kind, c):
    if kind == 'pool':
        n = ref.shape[1] // 2
        return ref.at[:, pl.ds(c * n, n), :]
    n = ref.shape[0] // 2
    return ref.at[pl.ds(c * n, n), :]


def _piece(ref, kind, k, c):
    if kind == 'col':
        r, w = ref.shape[0] // 2, ref.shape[1] // N_CHIPS
        return ref.at[pl.ds(c * r, r), pl.ds(k * w, w)]
    if kind == 'row':
        r = ref.shape[0] // (2 * N_CHIPS)
        return ref.at[pl.ds((2 * k + c) * r, r), :]
    if kind == 'lead':
        r = ref.shape[1] // 2
        return ref.at[k, pl.ds(c * r, r), :]
    r = ref.shape[1] // (2 * N_CHIPS)
    return ref.at[:, pl.ds((2 * k + c) * r, r), :]


def _slab(ref, kind, k):
    if kind == 'col':
        w = ref.shape[1] // N_CHIPS
        return ref.at[:, pl.ds(k * w, w)]
    if kind == 'row':
        r = ref.shape[0] // N_CHIPS
        return ref.at[pl.ds(k * r, r), :]
    if kind == 'lead':
        return ref.at[k]
    r = ref.shape[1] // N_CHIPS
    return ref.at[:, pl.ds(k * r, r), :]


def _place():
    x, y, c = lax.axis_index('x'), lax.axis_index('y'), lax.axis_index('c')
    return x, y, c


def _peer_chip(x, y, j):
    px = 1 - x if (j >> 1) else x
    py = 1 - y if (j & 1) else y
    return px, py


ANY = pl.BlockSpec(memory_space=pl.ANY)


def _all_gather(shards, kinds):
    n = len(shards)

    def body(*refs):
        ins, outs = refs[:n], refs[n:2 * n]
        ssem, rsem, fssem, frsem, lsem = refs[2 * n:]
        x, y, c = _place()
        me = 2 * x + y
        sib = (x, y, 1 - c)
        local = [pltpu.make_async_copy(ins[w], _slab(outs[w], kinds[w], me), lsem.at[w]) for w in range(n)]
        for cp in local:
            cp.start()
        sends = []
        for w in range(n):
            for j in (1, 2, 3):
                px, py = _peer_chip(x, y, j)
                sends.append(pltpu.make_async_remote_copy(
                    src_ref=_half_of_slab(ins[w], kinds[w], c), dst_ref=_piece(outs[w], kinds[w], me, c),
                    send_sem=ssem.at[w, j - 1], recv_sem=rsem.at[w, j - 1], device_id=(px, py, c), device_id_type=MESH))
        for cp in sends:
            cp.start()
        fwds = []
        for w in range(n):
            for j in (1, 2, 3):
                px, py = _peer_chip(x, y, j)
                got = _piece(outs[w], kinds[w], 2 * px + py, c)
                pltpu.make_async_remote_copy(src_ref=got, dst_ref=got, send_sem=ssem.at[w, j - 1],
                                             recv_sem=rsem.at[w, j - 1], device_id=(px, py, c),
                                             device_id_type=MESH).wait_recv()
                fwd = pltpu.make_async_remote_copy(src_ref=got, dst_ref=got, send_sem=fssem.at[w, j - 1],
                                                   recv_sem=frsem.at[w, j - 1], device_id=sib, device_id_type=MESH)
                fwd.start()
                fwds.append(fwd)
        for w in range(n):
            for j in (1, 2, 3):
                px, py = _peer_chip(x, y, j)
                other = _piece(outs[w], kinds[w], 2 * px + py, 1 - c)
                pltpu.make_async_remote_copy(src_ref=other, dst_ref=other, send_sem=fssem.at[w, j - 1],
                                             recv_sem=frsem.at[w, j - 1], device_id=sib,
                                             device_id_type=MESH).wait_recv()
        for cp in sends + fwds:
            cp.wait_send()
        for cp in local:
            cp.wait()

    return pl.pallas_call(
        body, name='all_gather_weights',
        out_shape=[jax.ShapeDtypeStruct(_full_shape(k, s.shape), s.dtype) for s, k in zip(shards, kinds)],
        in_specs=[ANY] * n, out_specs=[ANY] * n,
        scratch_shapes=[pltpu.SemaphoreType.DMA((n, 3)), pltpu.SemaphoreType.DMA((n, 3)),
                        pltpu.SemaphoreType.DMA((n, 3)), pltpu.SemaphoreType.DMA((n, 3)),
                        pltpu.SemaphoreType.DMA((n,))],
    )(*shards)


def _pair_exchange(grads, kinds, slabs):
    n = len(grads)

    def body(*refs):
        ins, own, got = refs[:n], refs[n:2 * n], refs[2 * n:3 * n]
        ssem, rsem, lsem = refs[3 * n:]
        x, y, c = _place()
        sib = (x, y, 1 - c)
        started, local = [], []
        for w in range(n):
            for k in range(N_CHIPS):
                cp = pltpu.make_async_remote_copy(
                    src_ref=_piece(ins[w], kinds[w], k, 1 - c), dst_ref=got[w].at[k], send_sem=ssem.at[w, k],
                    recv_sem=rsem.at[w, k], device_id=sib, device_id_type=MESH)
                cp.start()
                started.append(cp)
                lc = pltpu.make_async_copy(_piece(ins[w], kinds[w], k, c), own[w].at[k], lsem.at[w, k])
                lc.start()
                local.append(lc)
        for cp in started:
            cp.wait_recv()
        for cp in started:
            cp.wait_send()
        for lc in local:
            lc.wait()

    shapes = [jax.ShapeDtypeStruct((N_CHIPS,) + _half_shape(k, s), g.dtype) for g, k, s in zip(grads, kinds, slabs)]
    outs = pl.pallas_call(
        body, name='rs_pair_exchange', out_shape=shapes + shapes, in_specs=[ANY] * n, out_specs=[ANY] * (2 * n),
        scratch_shapes=[pltpu.SemaphoreType.DMA((n, N_CHIPS)), pltpu.SemaphoreType.DMA((n, N_CHIPS)),
                        pltpu.SemaphoreType.DMA((n, N_CHIPS))],
    )(*grads)
    return outs[:n], outs[n:]


def _chip_exchange(sums):
    n = len(sums)

    def body(*refs):
        ins, outs = refs[:n], refs[n:2 * n]
        ssem, rsem, lsem = refs[2 * n:]
        x, y, c = _place()
        me = 2 * x + y
        local = [pltpu.make_async_copy(ins[w].at[me], outs[w].at[me], lsem.at[w]) for w in range(n)]
        for lc in local:
            lc.start()
        started = []
        for w in range(n):
            for j in (1, 2, 3):
                px, py = _peer_chip(x, y, j)
                cp = pltpu.make_async_remote_copy(
                    src_ref=ins[w].at[2 * px + py], dst_ref=outs[w].at[me], send_sem=ssem.at[w, j - 1],
                    recv_sem=rsem.at[w, j - 1], device_id=(px, py, c), device_id_type=MESH)
                cp.start()
                started.append(cp)
        for w in range(n):
            for j in (1, 2, 3):
                px, py = _peer_chip(x, y, j)
                slot = outs[w].at[2 * px + py]
                pltpu.make_async_remote_copy(src_ref=slot, dst_ref=slot, send_sem=ssem.at[w, j - 1],
                                             recv_sem=rsem.at[w, j - 1], device_id=(px, py, c),
                                             device_id_type=MESH).wait_recv()
        for cp in started:
            cp.wait_send()
        for lc in local:
            lc.wait()

    return pl.pallas_call(
        body, name='rs_chip_exchange', out_shape=[jax.ShapeDtypeStruct(s.shape, s.dtype) for s in sums],
        in_specs=[ANY] * n, out_specs=[ANY] * n,
        scratch_shapes=[pltpu.SemaphoreType.DMA((n, 3)), pltpu.SemaphoreType.DMA((n, 3)),
                        pltpu.SemaphoreType.DMA((n,))],
    )(*sums)


def _final_exchange(halves, kinds, slabs):
    n = len(halves)

    def body(*refs):
        ins, outs = refs[:n], refs[n:2 * n]
        ssem, rsem, lsem = refs[2 * n:]
        x, y, c = _place()
        sib = (x, y, 1 - c)
        local, started = [], []
        for w in range(n):
            lc = pltpu.make_async_copy(ins[w], _half_of_slab(outs[w], kinds[w], c), lsem.at[w])
            lc.start()
            local.append(lc)
            cp = pltpu.make_async_remote_copy(
                src_ref=ins[w], dst_ref=_half_of_slab(outs[w], kinds[w], c), send_sem=ssem.at[w], recv_sem=rsem.at[w],
                device_id=sib, device_id_type=MESH)
            cp.start()
            started.append(cp)
        for w in range(n):
            other = _half_of_slab(outs[w], kinds[w], 1 - c)
            pltpu.make_async_remote_copy(src_ref=other, dst_ref=other, send_sem=ssem.at[w], recv_sem=rsem.at[w],
                                         device_id=sib, device_id_type=MESH).wait_recv()
        for cp in started:
            cp.wait_send()
        for lc in local:
            lc.wait()

    return pl.pallas_call(
        body, name='rs_final_exchange', out_shape=[jax.ShapeDtypeStruct(tuple(s), F32) for s in slabs],
        in_specs=[ANY] * n, out_specs=[ANY] * n,
        scratch_shapes=[pltpu.SemaphoreType.DMA((n,)), pltpu.SemaphoreType.DMA((n,)), pltpu.SemaphoreType.DMA((n,))],
    )(*halves)


def _all_reduce_small(buf):
    rows, cols = buf.shape

    def body(in_ref, out_ref, land, ssem, rsem):
        x, y, c = _place()
        me = 4 * x + 2 * y + c
        land[me] = in_ref[...]
        started = []
        for j in range(1, 8):
            px = 1 - x if (j >> 2) & 1 else x
            py = 1 - y if (j >> 1) & 1 else y
            pc = 1 - c if j & 1 else c
            cp = pltpu.make_async_remote_copy(src_ref=in_ref, dst_ref=land.at[me], send_sem=ssem.at[j - 1],
                                              recv_sem=rsem.at[j - 1], device_id=(px, py, pc), device_id_type=MESH)
            cp.start()
            started.append(cp)
        for j in range(1, 8):
            px = 1 - x if (j >> 2) & 1 else x
            py = 1 - y if (j >> 1) & 1 else y
            pc = 1 - c if j & 1 else c
            slot = land.at[4 * px + 2 * py + pc]
            pltpu.make_async_remote_copy(src_ref=slot, dst_ref=slot, send_sem=ssem.at[j - 1], recv_sem=rsem.at[j - 1],
                                         device_id=(px, py, pc), device_id_type=MESH).wait_recv()
        for cp in started:
            cp.wait_send()
        acc = land[0]
        for d in range(1, 8):
            acc = acc + land[d]
        out_ref[...] = acc

    return pl.pallas_call(
        body, name='all_reduce_small', out_shape=jax.ShapeDtypeStruct((rows, cols), F32),
        in_specs=[pl.BlockSpec(memory_space=pltpu.VMEM)], out_specs=pl.BlockSpec(memory_space=pltpu.VMEM),
        scratch_shapes=[pltpu.VMEM((8, rows, cols), F32), pltpu.SemaphoreType.DMA((7,)), pltpu.SemaphoreType.DMA((7,))],
    )(buf)


def _elementwise(fn, ins, lead_index, out_shape, out_dtypes, name):
    nd = len(out_shape)
    r, cdim = out_shape[-2], out_shape[-1]
    tr = _div_tile(r, max(16, (1 << 19) // cdim), 16)
    grid = tuple(out_shape[:-2]) + (r // tr,)
    block = (None,) * (nd - 2) + (tr, cdim)

    def spec(lead):
        if lead is None:
            return pl.BlockSpec(block, lambda *g: tuple(g) + (0,))
        return pl.BlockSpec((None,) + block, lambda *g, lead=lead: (lead,) + tuple(g) + (0,))

    n_in = len(ins)

    def body(*refs):
        res = fn(*[r_[...] for r_ in refs[:n_in]])
        for o_ref, v in zip(refs[n_in:], res):
            o_ref[...] = v.astype(o_ref.dtype)

    return pl.pallas_call(
        body, name=name, grid=grid, in_specs=[spec(l) for l in lead_index],
        out_specs=[spec(None) for _ in out_dtypes],
        out_shape=[jax.ShapeDtypeStruct(tuple(out_shape), dt) for dt in out_dtypes],
        compiler_params=_cp(('parallel',) * len(grid)),
    )(*ins)


def _adam_fn(w, g, m, v):
    m = ADAM_B1 * m + (1.0 - ADAM_B1) * g
    v = ADAM_B2 * v + (1.0 - ADAM_B2) * (g * g)
    m_hat = m / (1.0 - ADAM_B1 ** ADAM_STEP)
    v_hat = v / (1.0 - ADAM_B2 ** ADAM_STEP)
    delta = -ADAM_LR * (m_hat / (jnp.sqrt(v_hat) + ADAM_EPS) + ADAM_WD * w)
    return delta, m, v


def _adam(w, g, m, v, name):
    return _elementwise(_adam_fn, [w, g, m, v], [None] * 4, w.shape, [F32] * 3, name)


_DIMS = {'nn': (((1,), (0,)), ((), ())), 'nt': (((1,), (1,)), ((), ())), 'tn': (((0,), (0,)), ((), ()))}


def _mm(a, b, mode, out_dtype, name):
    if mode == 'nn':
        (M, K), N = a.shape, b.shape[1]
    elif mode == 'nt':
        (M, K), N = a.shape, b.shape[0]
    else:
        (K, M), N = a.shape, b.shape[1]
    if mode == 'tn':
        tm, tn, tk = _div_tile(M, 512, 128), _div_tile(N, 1024, 128), _div_tile(K, 2176, 16)
    else:
        tm, tn, tk = _div_tile(M, 1088, 16), _div_tile(N, 1024, 128), _div_tile(K, 2048, 128)
    nk = K // tk
    a_spec = {'nn': pl.BlockSpec((tm, tk), lambda i, j, k: (i, k)), 'nt': pl.BlockSpec((tm, tk), lambda i, j, k: (i, k)),
              'tn': pl.BlockSpec((tk, tm), lambda i, j, k: (k, i))}[mode]
    b_spec = {'nn': pl.BlockSpec((tk, tn), lambda i, j, k: (k, j)), 'nt': pl.BlockSpec((tn, tk), lambda i, j, k: (j, k)),
              'tn': pl.BlockSpec((tk, tn), lambda i, j, k: (k, j))}[mode]
    dims = _DIMS[mode]

    def body(a_ref, b_ref, o_ref, acc_ref):
        part = lax.dot_general(a_ref[...], b_ref[...], dims, preferred_element_type=F32)
        if nk == 1:
            o_ref[...] = part.astype(o_ref.dtype)
        else:
            k = pl.program_id(2)

            @pl.when(k == 0)
            def _():
                acc_ref[...] = part

            @pl.when(k > 0)
            def _():
                acc_ref[...] += part

            @pl.when(k == nk - 1)
            def _():
                o_ref[...] = acc_ref[...].astype(o_ref.dtype)

    return pl.pallas_call(
        body, name=name, grid=(M // tm, N // tn, nk), in_specs=[a_spec, b_spec],
        out_specs=pl.BlockSpec((tm, tn), lambda i, j, k: (i, j)), out_shape=jax.ShapeDtypeStruct((M, N), out_dtype),
        scratch_shapes=[pltpu.VMEM((tm, tn) if nk > 1 else (8, 128), F32)],
        compiler_params=_cp(('parallel', 'parallel', 'arbitrary')),
    )(a, b)


def _rb(tm, w, col=0):
    return pl.BlockSpec((tm, w), lambda i, col=col: (i, col))


def _fixed(shape):
    return pl.BlockSpec(shape, lambda i: (0,) * len(shape))


def _rms(x):
    return lax.rsqrt(jnp.mean(x * x, axis=-1, keepdims=True) + EPS)


def _norm_fwd(x, gain, name, width=None, col=0):
    Lp = x.shape[0]
    width = width or x.shape[1]
    tm = _div_tile(Lp, 256, 16)

    def body(x_ref, g_ref, o_ref):
        v = x_ref[...]
        o_ref[...] = (v * _rms(v) * g_ref[...]).astype(o_ref.dtype)

    return pl.pallas_call(
        body, name=name, grid=(Lp // tm,), in_specs=[_rb(tm, width, col), _fixed((1, width))], out_specs=_rb(tm, width),
        out_shape=jax.ShapeDtypeStruct((Lp, width), BF), compiler_params=_cp(('parallel',)),
    )(x, gain)


def _post_residual(h, f, gain, scale, name):
    Lp, D = h.shape
    tm = _div_tile(Lp, 256, 16)

    def body(h_ref, f_ref, g_ref, o_ref):
        v = f_ref[...]
        o_ref[...] = h_ref[...] + scale * (v * _rms(v) * g_ref[...])

    return pl.pallas_call(
        body, name=name, grid=(Lp // tm,), in_specs=[_rb(tm, D), _rb(tm, D), _fixed((1, D))], out_specs=_rb(tm, D),
        out_shape=jax.ShapeDtypeStruct((Lp, D), F32), compiler_params=_cp(('parallel',)),
    )(h, f, gain)


def _post_bwd(dh, f, gain, scale, name):
    Lp, D = dh.shape
    tm = _div_tile(Lp, 256, 16)

    def body(dh_ref, f_ref, g_ref, df_ref, dg_ref):
        v = f_ref[...]
        r = _rms(v)
        dy = scale * dh_ref[...]
        w = dy * g_ref[...]
        df_ref[...] = (r * w - v * (r * r * r) * jnp.mean(w * v, axis=-1, keepdims=True)).astype(df_ref.dtype)
        part = jnp.sum(dy * v * r, axis=0, keepdims=True)

        @pl.when(pl.program_id(0) == 0)
        def _():
            dg_ref[...] = part

        @pl.when(pl.program_id(0) > 0)
        def _():
            dg_ref[...] += part

    return pl.pallas_call(
        body, name=name, grid=(Lp // tm,), in_specs=[_rb(tm, D), _rb(tm, D), _fixed((1, D))],
        out_specs=[_rb(tm, D), _fixed((1, D))],
        out_shape=[jax.ShapeDtypeStruct((Lp, D), BF), jax.ShapeDtypeStruct((1, D), F32)],
        compiler_params=_cp(('arbitrary',)),
    )(dh, f, gain)


def _pre_bwd(dres, dn, h, gain, name):
    Lp, D = h.shape
    tm = _div_tile(Lp, 256, 16)

    def body(dres_ref, dn_ref, h_ref, g_ref, dh_ref, dg_ref):
        v = h_ref[...]
        r = _rms(v)
        dy = dn_ref[...]
        w = dy * g_ref[...]
        dh_ref[...] = dres_ref[...] + r * w - v * (r * r * r) * jnp.mean(w * v, axis=-1, keepdims=True)
        part = jnp.sum(dy * v * r, axis=0, keepdims=True)

        @pl.when(pl.program_id(0) == 0)
        def _():
            dg_ref[...] = part

        @pl.when(pl.program_id(0) > 0)
        def _():
            dg_ref[...] += part

    return pl.pallas_call(
        body, name=name, grid=(Lp // tm,), in_specs=[_rb(tm, D), _rb(tm, D), _rb(tm, D), _fixed((1, D))],
        out_specs=[_rb(tm, D), _fixed((1, D))],
        out_shape=[jax.ShapeDtypeStruct((Lp, D), F32), jax.ShapeDtypeStruct((1, D), F32)],
        compiler_params=_cp(('arbitrary',)),
    )(dres, dn, h, gain)


def _swiglu_fwd(gu, name):
    Lp, F2 = gu.shape
    F = F2 // 2
    tm = _div_tile(Lp, 256, 16)

    def body(gu_ref, a_ref):
        g = gu_ref[:, :F].astype(F32)
        u = gu_ref[:, F:].astype(F32)
        a_ref[...] = (g * jax.nn.sigmoid(g) * u).astype(a_ref.dtype)

    return pl.pallas_call(
        body, name=name, grid=(Lp // tm,), in_specs=[_rb(tm, F2)], out_specs=_rb(tm, F),
        out_shape=jax.ShapeDtypeStruct((Lp, F), BF), compiler_params=_cp(('parallel',)),
    )(gu)


def _swiglu_bwd(da, gu, name):
    Lp, F2 = gu.shape
    F = F2 // 2
    tm = _div_tile(Lp, 256, 16)

    def body(da_ref, gu_ref, o_ref):
        g = gu_ref[:, :F].astype(F32)
        u = gu_ref[:, F:].astype(F32)
        da_ = da_ref[...].astype(F32)
        s = jax.nn.sigmoid(g)
        o_ref[:, :F] = (da_ * u * (s * (1.0 + g * (1.0 - s)))).astype(o_ref.dtype)
        o_ref[:, F:] = (da_ * (g * s)).astype(o_ref.dtype)

    return pl.pallas_call(
        body, name=name, grid=(Lp // tm,), in_specs=[_rb(tm, F), _rb(tm, F2)], out_specs=_rb(tm, F2),
        out_shape=jax.ShapeDtypeStruct((Lp, F2), BF), compiler_params=_cp(('parallel',)),
    )(da, gu)


def _loss_bwd(h, tgt, n_meta, n_real, name):
    Lp, D = h.shape
    tm = _div_tile(Lp, 256, 16)

    def body(h_ref, t_ref, d_ref, l_ref):
        row = lax.broadcasted_iota(jnp.int32, (tm, 1), 0) + pl.program_id(0) * tm
        ok = (row >= n_meta) & (row < n_meta + n_real)
        err = jnp.where(ok, h_ref[...] - t_ref[...], 0.0)
        d_ref[...] = err / D
        part = jnp.full((1, 128), jnp.sum(err * err), F32)

        @pl.when(pl.program_id(0) == 0)
        def _():
            l_ref[...] = part

        @pl.when(pl.program_id(0) > 0)
        def _():
            l_ref[...] += part

    return pl.pallas_call(
        body, name=name, grid=(Lp // tm,), in_specs=[_rb(tm, D), _rb(tm, D)], out_specs=[_rb(tm, D), _fixed((1, 128))],
        out_shape=[jax.ShapeDtypeStruct((Lp, D), F32), jax.ShapeDtypeStruct((1, 128), F32)],
        compiler_params=_cp(('arbitrary',)),
    )(h, tgt)


def _split_bf16(v):
    hi = v.astype(BF)
    return hi, (v - hi.astype(F32)).astype(BF)


def _pool_fwd(z, name):
    Lp = z.shape[0]
    T = _div_tile(Lp, 256, 16)
    G = len(POOL_WINDOWS)

    def body(cur_ref, prev_ref, d_ref):
        i, g = pl.program_id(0), pl.program_id(1)
        w = jnp.left_shift(2, g)
        rr = lax.broadcasted_iota(jnp.int32, (T, T), 0)
        cc = lax.broadcasted_iota(jnp.int32, (T, T), 1)
        b_cur = jnp.where((cc <= rr) & (cc > rr - w), 1.0, 0.0).astype(BF)
        w_prev = jnp.where(i > 0, w, 0)
        b_prev = jnp.where(cc - T > rr - w_prev, 1.0, 0.0).astype(BF)
        u = cur_ref[...]
        s = jnp.zeros((T, POOL_GROUP), F32)
        for part in _split_bf16(u):
            s += jnp.dot(b_cur, part, preferred_element_type=F32)
        for part in _split_bf16(prev_ref[...]):
            s += jnp.dot(b_prev, part, preferred_element_type=F32)
        t = lax.broadcasted_iota(jnp.int32, (T, 1), 0) + i * T
        cnt = jnp.minimum(w, t + 1).astype(F32)
        d_ref[...] = (s / cnt - u).astype(d_ref.dtype)

    return pl.pallas_call(
        body, name=name, grid=(Lp // T, G),
        in_specs=[pl.BlockSpec((T, POOL_GROUP), lambda i, g: (i, g)),
                  pl.BlockSpec((T, POOL_GROUP), lambda i, g: (jnp.maximum(i - 1, 0), g))],
        out_specs=pl.BlockSpec((T, POOL_GROUP), lambda i, g: (i, g)),
        out_shape=jax.ShapeDtypeStruct((Lp, POOL_WIDTH), BF), compiler_params=_cp(('parallel', 'parallel')),
    )(z, z)


def _pool_bwd(dd, name):
    Lp = dd.shape[0]
    T = _div_tile(Lp, 256, 16)
    G = len(POOL_WINDOWS)
    n_t = Lp // T

    def body(cur_ref, next_ref, o_ref):
        i, g = pl.program_id(0), pl.program_id(1)
        w = jnp.left_shift(2, g)
        rr = lax.broadcasted_iota(jnp.int32, (T, T), 0)
        cc = lax.broadcasted_iota(jnp.int32, (T, T), 1)
        b_cur = jnp.where((cc >= rr) & (cc < rr + w), 1.0, 0.0).astype(BF)
        w_next = jnp.where(i < n_t - 1, w, 0)
        b_next = jnp.where(cc + T < rr + w_next, 1.0, 0.0).astype(BF)
        t = lax.broadcasted_iota(jnp.int32, (T, 1), 0) + i * T
        cur = cur_ref[...]
        e_cur = cur / jnp.minimum(w, t + 1).astype(F32)
        e_next = next_ref[...] / jnp.minimum(w, t + T + 1).astype(F32)
        s = jnp.zeros((T, POOL_GROUP), F32)
        for part in _split_bf16(e_cur):
            s += jnp.dot(b_cur, part, preferred_element_type=F32)
        for part in _split_bf16(e_next):
            s += jnp.dot(b_next, part, preferred_element_type=F32)
        o_ref[...] = (s - cur).astype(o_ref.dtype)

    return pl.pallas_call(
        body, name=name, grid=(n_t, G),
        in_specs=[pl.BlockSpec((T, POOL_GROUP), lambda i, g: (i, g)),
                  pl.BlockSpec((T, POOL_GROUP), lambda i, g: (jnp.minimum(i + 1, n_t - 1), g))],
        out_specs=pl.BlockSpec((T, POOL_GROUP), lambda i, g: (i, g)),
        out_shape=jax.ShapeDtypeStruct((Lp, POOL_WIDTH), BF), compiler_params=_cp(('parallel', 'parallel')),
    )(dd, dd)


def _pool_mix_fwd(d, pool_w, scale, name):
    Lp = d.shape[0]
    G = len(POOL_WINDOWS)
    tm = _div_tile(Lp, 1088, 16)

    def body(d_ref, w_ref, s_ref, e_ref, y_ref):
        e = jnp.dot(d_ref[...], w_ref[...], preferred_element_type=F32)
        e_ref[...] = e.astype(e_ref.dtype)
        y_ref[...] = (e * s_ref[...]).astype(y_ref.dtype)

    blk = pl.BlockSpec((tm, POOL_GROUP), lambda g, i: (i, g))
    return pl.pallas_call(
        body, name=name, grid=(G, Lp // tm),
        in_specs=[blk, pl.BlockSpec((None, POOL_GROUP, POOL_GROUP), lambda g, i: (g, 0, 0)),
                  pl.BlockSpec((1, POOL_GROUP), lambda g, i: (0, g))],
        out_specs=[blk, blk], out_shape=[jax.ShapeDtypeStruct((Lp, POOL_WIDTH), BF)] * 2,
        compiler_params=_cp(('parallel', 'parallel')),
    )(d, pool_w, scale)


def _pool_mix_bwd(dyp, e, d, pool_w, scale, name):
    Lp = d.shape[0]
    G = len(POOL_WINDOWS)
    tm = _div_tile(Lp, 1088, 16)

    def body(dy_ref, e_ref, d_ref, w_ref, s_ref, dd_ref, ds_ref, dw_ref):
        i = pl.program_id(1)
        dy = dy_ref[...]
        de = (dy * s_ref[...]).astype(BF)
        dd_ref[...] = lax.dot_general(de, w_ref[...], _DIMS['nt'], preferred_element_type=F32)
        ds_part = jnp.sum(dy * e_ref[...].astype(F32), axis=0, keepdims=True)
        dw_part = lax.dot_general(d_ref[...], de, _DIMS['tn'], preferred_element_type=F32)

        @pl.when(i == 0)
        def _():
            ds_ref[...] = ds_part
            dw_ref[...] = dw_part

        @pl.when(i > 0)
        def _():
            ds_ref[...] += ds_part
            dw_ref[...] += dw_part

    blk = pl.BlockSpec((tm, POOL_GROUP), lambda g, i: (i, g))
    wblk = pl.BlockSpec((None, POOL_GROUP, POOL_GROUP), lambda g, i: (g, 0, 0))
    sblk = pl.BlockSpec((1, POOL_GROUP), lambda g, i: (0, g))
    return pl.pallas_call(
        body, name=name, grid=(G, Lp // tm), in_specs=[blk, blk, blk, wblk, sblk], out_specs=[blk, sblk, wblk],
        out_shape=[jax.ShapeDtypeStruct((Lp, POOL_WIDTH), F32), jax.ShapeDtypeStruct((1, POOL_WIDTH), F32),
                   jax.ShapeDtypeStruct((G, POOL_GROUP, POOL_GROUP), F32)],
        compiler_params=_cp(('parallel', 'arbitrary')),
    )(dyp, e, d, pool_w, scale)


def _rot_half(t):
    lane = lax.broadcasted_iota(jnp.int32, t.shape, 1)
    half = QK_ROPE // 2
    return jnp.where(lane < half, -pltpu.roll(t, 128 - half, 1), pltpu.roll(t, half, 1))


def _lora_norms(z, q_gain, kv_gain, lay, name):
    Lp = z.shape[0]
    QL, KVL = lay['QL'], lay['KVL']
    tm = _div_tile(Lp, 256, 16)

    def body(q_ref, kv_ref, qg_ref, kg_ref, qo_ref, ko_ref):
        a = q_ref[...]
        qo_ref[...] = (a * _rms(a) * qg_ref[...]).astype(BF)
        b = kv_ref[...]
        ko_ref[...] = (b * _rms(b) * kg_ref[...]).astype(BF)

    return pl.pallas_call(
        body, name=name, grid=(Lp // tm,),
        in_specs=[_rb(tm, QL, lay['cq'] // QL), _rb(tm, KVL, lay['ckv'] // KVL), _fixed((1, QL)), _fixed((1, KVL))],
        out_specs=[_rb(tm, QL), _rb(tm, KVL)],
        out_shape=[jax.ShapeDtypeStruct((Lp, QL), BF), jax.ShapeDtypeStruct((Lp, KVL), BF)],
        compiler_params=_cp(('parallel',)),
    )(z, z, q_gain, kv_gain)


def _lora_norms_bwd(dqn, dkn, z, q_gain, kv_gain, lay, name):
    Lp = z.shape[0]
    QL, KVL = lay['QL'], lay['KVL']
    tm = _div_tile(Lp, 256, 16)

    def one(dy, v, gain):
        r = _rms(v)
        w = dy * gain
        return r * w - v * (r * r * r) * jnp.mean(w * v, axis=-1, keepdims=True), jnp.sum(dy * v * r, axis=0, keepdims=True)

    def body(dq_ref, dk_ref, q_ref, kv_ref, qg_ref, kg_ref, o_ref, dqg_ref, dkg_ref):
        da, ga = one(dq_ref[...], q_ref[...], qg_ref[...])
        db, gb = one(dk_ref[...], kv_ref[...], kg_ref[...])
        o_ref[:, :QL] = da.astype(BF)
        o_ref[:, QL:] = db.astype(BF)

        @pl.when(pl.program_id(0) == 0)
        def _():
            dqg_ref[...] = ga
            dkg_ref[...] = gb

        @pl.when(pl.program_id(0) > 0)
        def _():
            dqg_ref[...] += ga
            dkg_ref[...] += gb

    return pl.pallas_call(
        body, name=name, grid=(Lp // tm,),
        in_specs=[_rb(tm, QL), _rb(tm, KVL), _rb(tm, QL, lay['cq'] // QL), _rb(tm, KVL, lay['ckv'] // KVL),
                  _fixed((1, QL)), _fixed((1, KVL))],
        out_specs=[_rb(tm, QL + KVL), _fixed((1, QL)), _fixed((1, KVL))],
        out_shape=[jax.ShapeDtypeStruct((Lp, QL + KVL), BF), jax.ShapeDtypeStruct((1, QL), F32),
                   jax.ShapeDtypeStruct((1, KVL), F32)],
        compiler_params=_cp(('arbitrary',)),
    )(dqn, dkn, z, z, q_gain, kv_gain)


def _qk_prep(q_raw, kv, z, cos, sin, lay, H, name):
    Lp = q_raw.shape[0]
    W = H * HEAD_PAD
    tm = _div_tile(Lp, 256, 16)

    def body(q_ref, kv_ref, kr_ref, c_ref, s_ref, qo_ref, ko_ref):
        c, s = c_ref[...], s_ref[...]

        def rope(t):
            return t * c + _rot_half(t) * s

        kpe = rope(kr_ref[...]).astype(BF)
        for h in range(H):
            b = h * HEAD_PAD
            qo_ref[:, b:b + 128] = (q_ref[:, b:b + 128] * SOFTMAX_SCALE).astype(BF)
            qo_ref[:, b + 128:b + 256] = (rope(q_ref[:, b + 128:b + 256]) * SOFTMAX_SCALE).astype(BF)
            ko_ref[:, b:b + 128] = kv_ref[:, b:b + 128]
            ko_ref[:, b + 128:b + 256] = kpe

    return pl.pallas_call(
        body, name=name, grid=(Lp // tm,),
        in_specs=[_rb(tm, W), _rb(tm, W), _rb(tm, 128, lay['kr'] // 128), _rb(tm, 128), _rb(tm, 128)],
        out_specs=[_rb(tm, W), _rb(tm, W)], out_shape=[jax.ShapeDtypeStruct((Lp, W), BF)] * 2,
        compiler_params=_cp(('parallel',)),
    )(q_raw, kv, z, cos, sin)


def _qk_prep_bwd(dQ, dK, dV, cos, sin, H, name):
    Lp = dQ.shape[0]
    W = H * HEAD_PAD
    tm = _div_tile(Lp, 256, 16)

    def body(dq_ref, dk_ref, dv_ref, c_ref, s_ref, qo_ref, kvo_ref, kro_ref):
        c, s = c_ref[...], s_ref[...]

        def unrope(t):
            return t * c - _rot_half(t * s)

        acc = jnp.zeros((tm, 128), F32)
        for h in range(H):
            b = h * HEAD_PAD
            qo_ref[:, b:b + 128] = (dq_ref[:, b:b + 128] * SOFTMAX_SCALE).astype(BF)
            qo_ref[:, b + 128:b + 256] = (unrope(dq_ref[:, b + 128:b + 256]) * SOFTMAX_SCALE).astype(BF)
            kvo_ref[:, b:b + 128] = dk_ref[:, b:b + 128].astype(BF)
            kvo_ref[:, b + 128:b + 256] = dv_ref[:, h * V_DIM:(h + 1) * V_DIM]
            acc += dk_ref[:, b + 128:b + 256]
        kro_ref[...] = unrope(acc).astype(BF)

    return pl.pallas_call(
        body, name=name, grid=(Lp // tm,),
        in_specs=[_rb(tm, W), _rb(tm, W), _rb(tm, H * V_DIM), _rb(tm, 128), _rb(tm, 128)],
        out_specs=[_rb(tm, W), _rb(tm, W), _rb(tm, 128)],
        out_shape=[jax.ShapeDtypeStruct((Lp, W), BF), jax.ShapeDtypeStruct((Lp, W), BF),
                   jax.ShapeDtypeStruct((Lp, 128), BF)],
        compiler_params=_cp(('parallel',)),
    )(dQ, dK, dV, cos, sin)


def _flash_fwd(Q, K, kv, H, name):
    Lp = Q.shape[0]
    T = _div_tile(Lp, 256, 16)

    def body(q_ref, k_ref, v_ref, o_ref, lse_ref):
        i = pl.program_id(1)
        q = q_ref[...]

        def step(j, carry, masked):
            m, l, acc = carry
            rows = pl.ds(pl.multiple_of(j * T, T), T)
            s = lax.dot_general(q, k_ref[rows, :], _DIMS['nt'], preferred_element_type=F32)
            if masked:
                rr = lax.broadcasted_iota(jnp.int32, (T, T), 0)
                cc = lax.broadcasted_iota(jnp.int32, (T, T), 1)
                s = jnp.where(cc <= rr, s, NEG)
            m_new = jnp.maximum(m, jnp.max(s, axis=-1, keepdims=True))
            alpha = jnp.exp(m - m_new)
            p = jnp.exp(s - m_new)
            l = alpha * l + jnp.sum(p, axis=-1, keepdims=True)
            acc = alpha * acc + jnp.dot(p.astype(BF), v_ref[rows, :], preferred_element_type=F32)
            return m_new, l, acc

        init = (jnp.full((T, 1), NEG, F32), jnp.zeros((T, 1), F32), jnp.zeros((T, V_DIM), F32))
        carry = lax.fori_loop(0, i, lambda j, cr: step(j, cr, False), init)
        m, l, acc = step(i, carry, True)
        o_ref[...] = (acc / l).astype(o_ref.dtype)
        lse_ref[...] = jnp.broadcast_to(m + jnp.log(l), (T, 128))

    return pl.pallas_call(
        body, name=name, grid=(H, Lp // T),
        in_specs=[pl.BlockSpec((T, HEAD_PAD), lambda h, i: (i, h)), pl.BlockSpec((Lp, HEAD_PAD), lambda h, i: (0, h)),
                  pl.BlockSpec((Lp, V_DIM), lambda h, i: (0, 2 * h + 1))],
        out_specs=[pl.BlockSpec((T, V_DIM), lambda h, i: (i, h)), pl.BlockSpec((None, T, 128), lambda h, i: (h, i, 0))],
        out_shape=[jax.ShapeDtypeStruct((Lp, H * V_DIM), BF), jax.ShapeDtypeStruct((H, Lp, 128), F32)],
        compiler_params=_cp(('parallel', 'parallel')),
    )(Q, K, kv)


def _flash_bwd(Q, K, kv, O, dO, lse, H, name):
    Lp = Q.shape[0]
    T = _div_tile(Lp, 256, 16)
    n_t = Lp // T

    def body(q_ref, k_ref, v_ref, o_ref, do_ref, lse_ref, dq_ref, dk_ref, dv_ref, dk_acc, dv_acc):
        j = pl.program_id(1)

        @pl.when(j == 0)
        def _():
            dq_ref[...] = jnp.zeros_like(dq_ref)

        kj, vj = k_ref[...], v_ref[...]
        dk_acc[...] = jnp.zeros_like(dk_acc)
        dv_acc[...] = jnp.zeros_like(dv_acc)

        def step(i, masked):
            rows = pl.ds(pl.multiple_of(i * T, T), T)
            qi, doi = q_ref[rows, :], do_ref[rows, :]
            delta = jnp.sum(doi.astype(F32) * o_ref[rows, :].astype(F32), axis=-1, keepdims=True)
            s = lax.dot_general(qi, kj, _DIMS['nt'], preferred_element_type=F32)
            p = jnp.exp(s - lse_ref[rows, :][:, :1])
            if masked:
                rr = lax.broadcasted_iota(jnp.int32, (T, T), 0)
                cc = lax.broadcasted_iota(jnp.int32, (T, T), 1)
                p = jnp.where(cc <= rr, p, 0.0)
            dp = lax.dot_general(doi, vj, _DIMS['nt'], preferred_element_type=F32)
            ds = (p * (dp - delta)).astype(BF)
            dv_acc[...] += lax.dot_general(p.astype(BF), doi, _DIMS['tn'], preferred_element_type=F32)
            dk_acc[...] += lax.dot_general(ds, qi, _DIMS['tn'], preferred_element_type=F32)
            dq_ref[rows, :] += jnp.dot(ds, kj, preferred_element_type=F32)

        step(j, True)

        def loop(i, carry):
            step(i, False)
            return carry

        lax.fori_loop(j + 1, n_t, loop, 0)
        dk_ref[...] = dk_acc[...]
        dv_ref[...] = dv_acc[...].astype(dv_ref.dtype)

    head_q = pl.BlockSpec((Lp, HEAD_PAD), lambda h, j: (0, h))
    head_v = pl.BlockSpec((Lp, V_DIM), lambda h, j: (0, h))
    return pl.pallas_call(
        body, name=name, grid=(H, n_t),
        in_specs=[head_q, pl.BlockSpec((T, HEAD_PAD), lambda h, j: (j, h)),
                  pl.BlockSpec((T, V_DIM), lambda h, j: (j, 2 * h + 1)), head_v, head_v,
                  pl.BlockSpec((None, Lp, 128), lambda h, j: (h, 0, 0))],
        out_specs=[head_q, pl.BlockSpec((T, HEAD_PAD), lambda h, j: (j, h)),
                   pl.BlockSpec((T, V_DIM), lambda h, j: (j, h))],
        out_shape=[jax.ShapeDtypeStruct((Lp, H * HEAD_PAD), F32), jax.ShapeDtypeStruct((Lp, H * HEAD_PAD), F32),
                   jax.ShapeDtypeStruct((Lp, H * V_DIM), BF)],
        scratch_shapes=[pltpu.VMEM((T, HEAD_PAD), F32), pltpu.VMEM((T, V_DIM), F32)],
        compiler_params=_cp(('parallel', 'arbitrary')),
    )(Q, K, kv, O, dO, lse)


def _gate_fwd(z, y_pool, y_mla, lay, name):
    Lp, D = y_pool.shape
    tm = _div_tile(Lp, 256, 16)

    def body(gp_ref, gm_ref, yp_ref, ym_ref, o_ref):
        o_ref[...] = (jax.nn.sigmoid(gp_ref[...]) * yp_ref[...] + jax.nn.sigmoid(gm_ref[...]) * ym_ref[...]).astype(BF)

    return pl.pallas_call(
        body, name=name, grid=(Lp // tm,),
        in_specs=[_rb(tm, D, lay['gp'] // D), _rb(tm, D, lay['gm'] // D), _rb(tm, D), _rb(tm, D)], out_specs=_rb(tm, D),
        out_shape=jax.ShapeDtypeStruct((Lp, D), BF), compiler_params=_cp(('parallel',)),
    )(z, z, y_pool, y_mla)


def _gate_bwd(dy, z, y_pool, y_mla, lay, name):
    Lp, D = y_pool.shape
    tm = _div_tile(Lp, 256, 16)

    def body(dy_ref, gp_ref, gm_ref, yp_ref, ym_ref, dp_ref, dm_ref, dg_ref):
        dy_ = dy_ref[...]
        sp, sm = jax.nn.sigmoid(gp_ref[...]), jax.nn.sigmoid(gm_ref[...])
        dp_ref[...] = (dy_ * sp).astype(BF)
        dm_ref[...] = (dy_ * sm).astype(BF)
        dg_ref[:, :D] = (dy_ * yp_ref[...] * (sp * (1.0 - sp))).astype(BF)
        dg_ref[:, D:] = (dy_ * ym_ref[...] * (sm * (1.0 - sm))).astype(BF)

    return pl.pallas_call(
        body, name=name, grid=(Lp // tm,),
        in_specs=[_rb(tm, D), _rb(tm, D, lay['gp'] // D), _rb(tm, D, lay['gm'] // D), _rb(tm, D), _rb(tm, D)],
        out_specs=[_rb(tm, D), _rb(tm, D), _rb(tm, 2 * D)],
        out_shape=[jax.ShapeDtypeStruct((Lp, D), BF), jax.ShapeDtypeStruct((Lp, D), BF),
                   jax.ShapeDtypeStruct((Lp, 2 * D), BF)],
        compiler_params=_cp(('parallel',)),
    )(dy, z, z, y_pool, y_mla)


def _z_layout(D, QL, KVL):
    cq = POOL_WIDTH
    ckv = cq + QL
    gp = -(-(ckv + KVL) // D) * D
    gm = gp + D
    kr = gm + D
    return dict(QL=QL, KVL=KVL, cq=cq, ckv=ckv, gp=gp, gm=gm, kr=kr, width=kr + 128)


def _w_in_aligned(w_log, lay, D):
    n0 = POOL_WIDTH + lay['QL'] + lay['KVL']
    parts = [w_log[:, :n0], jnp.zeros((D, lay['gp'] - n0), w_log.dtype), w_log[:, n0 + QK_ROPE:],
             w_log[:, n0:n0 + QK_ROPE], jnp.zeros((D, 128 - QK_ROPE), w_log.dtype)]
    return jnp.concatenate(parts, axis=1)


def _w_in_logical(w_al, lay, D):
    n0 = POOL_WIDTH + lay['QL'] + lay['KVL']
    return jnp.concatenate([w_al[:, :n0], w_al[:, lay['kr']:lay['kr'] + QK_ROPE], w_al[:, lay['gp']:lay['gp'] + 2 * D]],
                           axis=1)


def _ffn_fwd(h, pre, post, w_gu, w_down, tag):
    n = _norm_fwd(h, pre, f'{tag}_norm')
    gu = _mm(n, w_gu, 'nn', BF, f'{tag}_gu')
    a = _swiglu_fwd(gu, f'{tag}_act')
    f = _mm(a, w_down, 'nn', F32, f'{tag}_down')
    out = _post_residual(h, f, post, 0.5, f'{tag}_res')
    return out, (h, n, gu, a, f)


def _ffn_bwd(dh, saved, pre, post, w_gu, w_down, tag):
    h, n, gu, a, f = saved
    df, d_post = _post_bwd(dh, f, post, 0.5, f'{tag}_res_bwd')
    d_w_down = _mm(a, df, 'tn', BF, f'{tag}_dw_down')
    da = _mm(df, w_down, 'nt', BF, f'{tag}_da')
    dgu = _swiglu_bwd(da, gu, f'{tag}_act_bwd')
    d_w_gu = _mm(n, dgu, 'tn', BF, f'{tag}_dw_gu')
    dn = _mm(dgu, w_gu, 'nt', F32, f'{tag}_dn')
    dh_in, d_pre = _pre_bwd(dh, dn, h, pre, f'{tag}_norm_bwd')
    return dh_in, d_pre, d_post, d_w_gu, d_w_down


def kernel(x, meta_tokens, norm_ffn1_pre, norm_ffn1_post, ffn1_w_gu, ffn1_w_down, norm_mix_pre, norm_mix_post, w_in, pool_w, pool_scale, w_pool_o, q_a_norm, w_q_b, kv_a_norm, w_kv_b, w_mla_o, w_out, norm_ffn2_pre, norm_ffn2_post, ffn2_w_gu, ffn2_w_down, loss_target, m_meta_tokens, m_norm_ffn1_pre, m_norm_ffn1_post, m_ffn1_w_gu, m_ffn1_w_down, m_norm_mix_pre, m_norm_mix_post, m_w_in, m_pool_w, m_pool_scale, m_w_pool_o, m_q_a_norm, m_w_q_b, m_kv_a_norm, m_w_kv_b, m_w_mla_o, m_w_out, m_norm_ffn2_pre, m_norm_ffn2_post, m_ffn2_w_gu, m_ffn2_w_down, v_meta_tokens, v_norm_ffn1_pre, v_norm_ffn1_post, v_ffn1_w_gu, v_ffn1_w_down, v_norm_mix_pre, v_norm_mix_post, v_w_in, v_pool_w, v_pool_scale, v_w_pool_o, v_q_a_norm, v_w_q_b, v_kv_a_norm, v_w_kv_b, v_w_mla_o, v_w_out, v_norm_ffn2_pre, v_norm_ffn2_post, v_ffn2_w_gu, v_ffn2_w_down):
    given = dict(locals())
    W = {n: given[n] for n in WEIGHTS}
    M = {n: given['m_' + n] for n in WEIGHTS}
    V = {n: given['v_' + n] for n in WEIGHTS}

    S, D = x.shape[1], x.shape[2]
    NM = meta_tokens.shape[0]
    L = NM + S
    Lp = -(-L // ROW_ALIGN) * ROW_ALIGN
    QL, KVL = w_q_b.shape[1], w_kv_b.shape[1]
    H = w_q_b.shape[2] * N_CHIPS // QK_DIM
    lay = _z_layout(D, QL, KVL)
    gx, gy = lax.axis_index('x'), lax.axis_index('y')
    chip = 2 * gx + gy

    names = list(BIG)
    slab = {n: W[n][0] for n in names}
    kinds = [BIG[n] for n in names] + ['col']
    gathered = _all_gather([slab[n].astype(BF) for n in names] + [meta_tokens], kinds)
    full = dict(zip(names, gathered[:-1]))
    meta_full = gathered[-1]

    in_cols = full['w_in'].shape[0] * full['w_in'].shape[2]
    w_in_al = _w_in_aligned(full['w_in'].transpose(1, 0, 2).reshape(D, in_cols), lay, D)
    w_q_pad = jnp.pad(full['w_q_b'].reshape(QL, H, QK_DIM), ((0, 0), (0, 0), (0, HEAD_PAD - QK_DIM))).reshape(
        QL, H * HEAD_PAD)

    pos = jnp.arange(Lp, dtype=F32)
    inv = ROPE_THETA ** (-jnp.arange(0, QK_ROPE, 2, dtype=F32) / QK_ROPE)
    ang = pos[:, None] * inv[None, :]
    ang = jnp.concatenate([ang, ang], axis=-1)
    cos = jnp.pad(jnp.cos(ang), ((0, 0), (0, 128 - QK_ROPE)), constant_values=1.0)
    sin = jnp.pad(jnp.sin(ang), ((0, 0), (0, 128 - QK_ROPE)))

    h0 = jnp.concatenate([meta_full, x[0], jnp.zeros((Lp - L, D), F32)], axis=0)
    tgt = jnp.pad(loss_target[0], ((NM, Lp - L), (0, 0)))

    h1, ffn1_saved = _ffn_fwd(h0, norm_ffn1_pre, norm_ffn1_post, full['ffn1_w_gu'], full['ffn1_w_down'], 'ffn1')

    n2 = _norm_fwd(h1, norm_mix_pre, 'mix_norm')
    z = _mm(n2, w_in_al, 'nn', F32, 'mix_in')
    d_pool = _pool_fwd(z, 'pool_fwd')
    e_pool, yp = _pool_mix_fwd(d_pool, full['pool_w'], pool_scale, 'pool_mix')
    y_pool = _mm(yp, full['w_pool_o'], 'nn', F32, 'pool_out')
    cqn, ckvn = _lora_norms(z, q_a_norm, kv_a_norm, lay, 'lora_norms')
    q_raw = _mm(cqn, w_q_pad, 'nn', F32, 'mla_q')
    kv = _mm(ckvn, full['w_kv_b'], 'nn', BF, 'mla_kv')
    Q, K = _qk_prep(q_raw, kv, z, cos, sin, lay, H, 'qk_prep')
    O, lse = _flash_fwd(Q, K, kv, H, 'flash_fwd')
    y_mla = _mm(O, full['w_mla_o'], 'nn', F32, 'mla_out')
    y = _gate_fwd(z, y_pool, y_mla, lay, 'gate')
    m_mix = _mm(y, full['w_out'], 'nn', F32, 'mix_out')
    h2 = _post_residual(h1, m_mix, norm_mix_post, 1.0, 'mix_res')

    h3, ffn2_saved = _ffn_fwd(h2, norm_ffn2_pre, norm_ffn2_post, full['ffn2_w_gu'], full['ffn2_w_down'], 'ffn2')

    dh3, sq = _loss_bwd(h3, tgt, NM, S, 'loss')
    G = {}
    dh2, G['norm_ffn2_pre'], G['norm_ffn2_post'], G['ffn2_w_gu'], G['ffn2_w_down'] = _ffn_bwd(
        dh3, ffn2_saved, norm_ffn2_pre, norm_ffn2_post, full['ffn2_w_gu'], full['ffn2_w_down'], 'ffn2')

    dm, G['norm_mix_post'] = _post_bwd(dh2, m_mix, norm_mix_post, 1.0, 'mix_res_bwd')
    G['w_out'] = _mm(y, dm, 'tn', BF, 'dw_out')
    dy = _mm(dm, full['w_out'], 'nt', F32, 'mix_out_bwd')
    dy_pool, dy_mla, d_gate = _gate_bwd(dy, z, y_pool, y_mla, lay, 'gate_bwd')
    G['w_pool_o'] = _mm(yp, dy_pool, 'tn', BF, 'dw_pool_o')
    dyp = _mm(dy_pool, full['w_pool_o'], 'nt', F32, 'pool_out_bwd')
    dd, G['pool_scale'], d_pool_w = _pool_mix_bwd(dyp, e_pool, d_pool, full['pool_w'], pool_scale, 'pool_mix_bwd')
    G['pool_w'] = d_pool_w.astype(BF)
    du_pool = _pool_bwd(dd, 'pool_bwd')
    G['w_mla_o'] = _mm(O, dy_mla, 'tn', BF, 'dw_mla_o')
    dO = _mm(dy_mla, full['w_mla_o'], 'nt', BF, 'mla_out_bwd')
    dQ, dK, dV = _flash_bwd(Q, K, kv, O, dO, lse, H, 'flash_bwd')
    dq_raw, dkv, dkr = _qk_prep_bwd(dQ, dK, dV, cos, sin, H, 'qk_prep_bwd')
    d_w_q_pad = _mm(cqn, dq_raw, 'tn', BF, 'dw_q_b')
    G['w_q_b'] = d_w_q_pad.reshape(QL, H, HEAD_PAD)[:, :, :QK_DIM].reshape(QL, H * QK_DIM)
    dcqn = _mm(dq_raw, w_q_pad, 'nt', F32, 'mla_q_bwd')
    G['w_kv_b'] = _mm(ckvn, dkv, 'tn', BF, 'dw_kv_b')
    dckvn = _mm(dkv, full['w_kv_b'], 'nt', F32, 'mla_kv_bwd')
    d_lora, G['q_a_norm'], G['kv_a_norm'] = _lora_norms_bwd(dcqn, dckvn, z, q_a_norm, kv_a_norm, lay, 'lora_norms_bwd')
    n0 = POOL_WIDTH + QL + KVL
    dz = jnp.concatenate([du_pool, d_lora, jnp.zeros((Lp, lay['gp'] - n0), BF), d_gate, dkr], axis=1)
    d_w_in_al = _mm(n2, dz, 'tn', BF, 'dw_in')
    G['w_in'] = _w_in_logical(d_w_in_al, lay, D).reshape(D, N_CHIPS, in_cols // N_CHIPS).transpose(1, 0, 2)
    dn2 = _mm(dz, w_in_al, 'nt', F32, 'mix_in_bwd')
    dh1, G['norm_mix_pre'] = _pre_bwd(dh2, dn2, h1, norm_mix_pre, 'mix_norm_bwd')

    dh0, G['norm_ffn1_pre'], G['norm_ffn1_post'], G['ffn1_w_gu'], G['ffn1_w_down'] = _ffn_bwd(
        dh1, ffn1_saved, norm_ffn1_pre, norm_ffn1_post, full['ffn1_w_gu'], full['ffn1_w_down'], 'ffn1')
    grad_x = dh0[NM:L][None]

    slabs = [tuple(slab[n].shape) for n in names]
    bkinds = kinds[:-1]
    mine, theirs = _pair_exchange([G[n] for n in names], bkinds, slabs)
    sums = [_elementwise(lambda a, b: (a.astype(F32) + b.astype(F32),), [a, b], [None, None], a.shape, [BF],
                         f'rs_pair_sum_{n}')[0] for n, a, b in zip(names, mine, theirs)]
    landed = _chip_exchange(sums)
    halves = [_elementwise(lambda a, b, c_, d: (((a.astype(F32) + b.astype(F32)) + c_.astype(F32)) + d.astype(F32),),
                           [r] * N_CHIPS, list(range(N_CHIPS)), r.shape[1:], [F32], f'rs_chip_sum_{n}')[0]
              for n, r in zip(names, landed)]
    reduced = dict(zip(names, _final_exchange(halves, bkinds, slabs)))

    SW = max(D, POOL_WIDTH)

    def widen(a, fill=0.0):
        return jnp.pad(a, ((0, 0), (0, SW - a.shape[1])), constant_values=fill)

    rows = [widen(G[n]) for n in SMALL_VEC] + [widen(dh0[:NM]), widen(sq)]
    n_rows = len(SMALL_VEC) + NM + 1
    pad_rows = -(-n_rows // 8) * 8 - n_rows
    small = _all_reduce_small(jnp.concatenate(rows + [jnp.zeros((pad_rows, SW), F32)], axis=0))
    loss = (0.5 / D) * small[len(SMALL_VEC) + NM, 0]
    for i, n in enumerate(SMALL_VEC):
        reduced[n] = small[i:i + 1, :G[n].shape[1]]
    mw = meta_tokens.shape[1]
    reduced['meta_tokens'] = lax.dynamic_slice(small, (len(SMALL_VEC), chip * mw), (NM, mw))

    grads, deltas, new_m, new_v = {}, {}, {}, {}
    for n in names:
        w = slab[n]
        deltas[n], new_m[n], new_v[n] = [o[None] for o in _adam(w, reduced[n], M[n][0], V[n][0], f'adam_{n}')]
        grads[n] = reduced[n][None]
    n_vec = len(SMALL_VEC)
    vec_w = jnp.concatenate([widen(W[n]) for n in SMALL_VEC] + [jnp.zeros((16 - n_vec, SW), F32)], axis=0)
    vec_m = jnp.concatenate([widen(M[n]) for n in SMALL_VEC] + [jnp.zeros((16 - n_vec, SW), F32)], axis=0)
    vec_v = jnp.concatenate([widen(V[n], 1.0) for n in SMALL_VEC] + [jnp.ones((16 - n_vec, SW), F32)], axis=0)
    vec_g = jnp.concatenate([small[:n_vec], jnp.zeros((16 - n_vec, SW), F32)], axis=0)
    vd, vm, vv = _adam(vec_w, vec_g, vec_m, vec_v, 'adam_vectors')
    for i, n in enumerate(SMALL_VEC):
        wdt = W[n].shape[1]
        grads[n], deltas[n], new_m[n], new_v[n] = reduced[n], vd[i:i + 1, :wdt], vm[i:i + 1, :wdt], vv[i:i + 1, :wdt]
    grads['meta_tokens'] = reduced['meta_tokens']
    deltas['meta_tokens'], new_m['meta_tokens'], new_v['meta_tokens'] = _adam(
        meta_tokens, reduced['meta_tokens'], m_meta_tokens, v_meta_tokens, 'adam_meta')

    return (loss, grad_x, *[grads[n] for n in WEIGHTS], *[deltas[n] for n in WEIGHTS], *[new_m[n] for n in WEIGHTS],
            *[new_v[n] for n in WEIGHTS])
```

```python
import functools

import jax
import jax.numpy as jnp
import numpy as np
from jax import lax
from jax.experimental import pallas as pl
from jax.experimental.pallas import tpu as pltpu

F32 = jnp.float32
BF = jnp.bfloat16
MESH = pl.DeviceIdType.MESH

EPS = 1e-6
N_CHIPS = 4
POOL_WINDOWS = (2, 4, 8, 16)
POOL_GROUP = 256
POOL_WIDTH = POOL_GROUP * len(POOL_WINDOWS)
QK_NOPE = 128
QK_ROPE = 64
V_DIM = 128
QK_DIM = QK_NOPE + QK_ROPE
HEAD_PAD = 256
ROPE_THETA = 10000.0
SOFTMAX_SCALE = QK_DIM ** -0.5
ADAM_LR = 0.001
ADAM_B1 = 0.9
ADAM_B2 = 0.999
ADAM_EPS = 1e-08
ADAM_WD = 0.01
ADAM_STEP = 10
ROW_ALIGN = 256
VMEM_LIMIT = 48 * 1024 * 1024
NEG = -1e30
FLASH_CHUNK = 4

WEIGHTS = ['meta_tokens', 'norm_ffn1_pre', 'norm_ffn1_post', 'ffn1_w_gu', 'ffn1_w_down', 'norm_mix_pre',
           'norm_mix_post', 'w_in', 'pool_w', 'pool_scale', 'w_pool_o', 'q_a_norm', 'w_q_b', 'kv_a_norm', 'w_kv_b',
           'w_mla_o', 'w_out', 'norm_ffn2_pre', 'norm_ffn2_post', 'ffn2_w_gu', 'ffn2_w_down']
BIG = {'ffn1_w_gu': 'col', 'ffn1_w_down': 'row', 'w_in': 'lead', 'pool_w': 'pool', 'w_pool_o': 'col', 'w_q_b': 'col',
       'w_kv_b': 'col', 'w_mla_o': 'row', 'w_out': 'row', 'ffn2_w_gu': 'col', 'ffn2_w_down': 'row'}
SMALL_VEC = ['norm_ffn1_pre', 'norm_ffn1_post', 'norm_mix_pre', 'norm_mix_post', 'norm_ffn2_pre', 'norm_ffn2_post',
             'pool_scale', 'q_a_norm', 'kv_a_norm']


def _div_tile(n, target, align):
    best = None
    for t in range(align, min(n, target) + 1, align):
        if n % t == 0:
            best = t
    return best if best is not None else n


def _cp(sem=None):
    return pltpu.CompilerParams(dimension_semantics=sem, vmem_limit_bytes=VMEM_LIMIT)


def _full_shape(kind, slab):
    if kind == 'col':
        return (slab[0], slab[1] * N_CHIPS)
    if kind == 'row':
        return (slab[0] * N_CHIPS, slab[1])
    if kind == 'lead':
        return (N_CHIPS,) + tuple(slab)
    return (slab[0], slab[1] * N_CHIPS, slab[2])


def _half_shape(kind, slab):
    if kind == 'pool':
        return (slab[0], slab[1] // 2, slab[2])
    return (slab[0] // 2, slab[1])


def _half_of_slab(ref, kind, c):
    if kind == 'pool':
        n = ref.shape[1] // 2
        return ref.at[:, pl.ds(c * n, n), :]
    n = ref.shape[0] // 2
    return ref.at[pl.ds(c * n, n), :]


def _piece(ref, kind, k, c):
    if kind == 'col':
        r, w = ref.shape[0] // 2, ref.shape[1] // N_CHIPS
        return ref.at[pl.ds(c * r, r), pl.ds(k * w, w)]
    if kind == 'row':
        r = ref.shape[0] // (2 * N_CHIPS)
        return ref.at[pl.ds((2 * k + c) * r, r), :]
    if kind == 'lead':
        r = ref.shape[1] // 2
        return ref.at[k, pl.ds(c * r, r), :]
    r = ref.shape[1] // (2 * N_CHIPS)
    return ref.at[:, pl.ds((2 * k + c) * r, r), :]


def _slab(ref, kind, k):
    if kind == 'col':
        w = ref.shape[1] // N_CHIPS
        return ref.at[:, pl.ds(k * w, w)]
    if kind == 'row':
        r = ref.shape[0] // N_CHIPS
        return ref.at[pl.ds(k * r, r), :]
    if kind == 'lead':
        return ref.at[k]
    r = ref.shape[1] // N_CHIPS
    return ref.at[:, pl.ds(k * r, r), :]


def _place():
    x, y, c = lax.axis_index('x'), lax.axis_index('y'), lax.axis_index('c')
    return x, y, c


def _peer_chip(x, y, j):
    px = 1 - x if (j >> 1) else x
    py = 1 - y if (j & 1) else y
    return px, py


ANY = pl.BlockSpec(memory_space=pl.ANY)


def _all_gather(bufs, kinds):
    n = len(bufs)

    def body(*refs):
        outs = refs[n:2 * n]
        ssem, rsem, fssem, frsem = refs[2 * n:]
        x, y, c = _place()
        me = 2 * x + y
        sib = (x, y, 1 - c)
        sends = []
        for w in range(n):
            for j in (1, 2, 3):
                px, py = _peer_chip(x, y, j)
                mine = _piece(outs[w], kinds[w], me, c)
                sends.append(pltpu.make_async_remote_copy(
                    src_ref=mine, dst_ref=mine, send_sem=ssem.at[w, j - 1], recv_sem=rsem.at[w, j - 1],
                    device_id=(px, py, c), device_id_type=MESH))
        for cp in sends:
            cp.start()
        fwds = []
        for w in range(n):
            for j in (1, 2, 3):
                px, py = _peer_chip(x, y, j)
                got = _piece(outs[w], kinds[w], 2 * px + py, c)
                pltpu.make_async_remote_copy(src_ref=got, dst_ref=got, send_sem=ssem.at[w, j - 1],
                                             recv_sem=rsem.at[w, j - 1], device_id=(px, py, c),
                                             device_id_type=MESH).wait_recv()
                fwd = pltpu.make_async_remote_copy(src_ref=got, dst_ref=got, send_sem=fssem.at[w, j - 1],
                                                   recv_sem=frsem.at[w, j - 1], device_id=sib, device_id_type=MESH)
                fwd.start()
                fwds.append(fwd)
        for w in range(n):
            for j in (1, 2, 3):
                px, py = _peer_chip(x, y, j)
                other = _piece(outs[w], kinds[w], 2 * px + py, 1 - c)
                pltpu.make_async_remote_copy(src_ref=other, dst_ref=other, send_sem=fssem.at[w, j - 1],
                                             recv_sem=frsem.at[w, j - 1], device_id=sib,
                                             device_id_type=MESH).wait_recv()
        for cp in sends + fwds:
            cp.wait_send()

    return pl.pallas_call(
        body, name='all_gather_weights', out_shape=[jax.ShapeDtypeStruct(b.shape, b.dtype) for b in bufs],
        in_specs=[ANY] * n, out_specs=[ANY] * n, input_output_aliases={w: w for w in range(n)},
        scratch_shapes=[pltpu.SemaphoreType.DMA((n, 3)), pltpu.SemaphoreType.DMA((n, 3)),
                        pltpu.SemaphoreType.DMA((n, 3)), pltpu.SemaphoreType.DMA((n, 3))],
    )(*bufs)


def _pair_exchange(grads, kinds, slabs):
    n = len(grads)

    def body(*refs):
        ins, got = refs[:n], refs[n:2 * n]
        ssem, rsem = refs[2 * n:]
        x, y, c = _place()
        sib = (x, y, 1 - c)
        started = []
        for w in range(n):
            for k in range(N_CHIPS):
                cp = pltpu.make_async_remote_copy(
                    src_ref=_piece(ins[w], kinds[w], k, 1 - c), dst_ref=got[w].at[k], send_sem=ssem.at[w, k],
                    recv_sem=rsem.at[w, k], device_id=sib, device_id_type=MESH)
                cp.start()
                started.append(cp)
        for cp in started:
            cp.wait_recv()
        for cp in started:
            cp.wait_send()

    shapes = [jax.ShapeDtypeStruct((N_CHIPS,) + _half_shape(k, s), g.dtype) for g, k, s in zip(grads, kinds, slabs)]
    return pl.pallas_call(
        body, name='rs_pair_exchange', out_shape=shapes, in_specs=[ANY] * n, out_specs=[ANY] * n,
        scratch_shapes=[pltpu.SemaphoreType.DMA((n, N_CHIPS)), pltpu.SemaphoreType.DMA((n, N_CHIPS))],
    )(*grads)


def _chip_exchange(sums):
    n = len(sums)

    def body(*refs):
        ins, outs = refs[:n], refs[n:2 * n]
        ssem, rsem = refs[2 * n:]
        x, y, c = _place()
        started = []
        for w in range(n):
            for j in (1, 2, 3):
                px, py = _peer_chip(x, y, j)
                cp = pltpu.make_async_remote_copy(
                    src_ref=ins[w].at[2 * px + py], dst_ref=outs[w].at[j - 1], send_sem=ssem.at[w, j - 1],
                    recv_sem=rsem.at[w, j - 1], device_id=(px, py, c), device_id_type=MESH)
                cp.start()
                started.append(cp)
        for cp in started:
            cp.wait_recv()
        for cp in started:
            cp.wait_send()

    return pl.pallas_call(
        body, name='rs_chip_exchange', out_shape=[jax.ShapeDtypeStruct((3,) + s.shape[1:], s.dtype) for s in sums],
        in_specs=[ANY] * n, out_specs=[ANY] * n,
        scratch_shapes=[pltpu.SemaphoreType.DMA((n, 3)), pltpu.SemaphoreType.DMA((n, 3))],
    )(*sums)


def _final_exchange(slabs_half, kinds):
    n = len(slabs_half)

    def body(*refs):
        outs = refs[n:2 * n]
        ssem, rsem = refs[2 * n:]
        x, y, c = _place()
        sib = (x, y, 1 - c)
        started = []
        for w in range(n):
            mine = _half_of_slab(outs[w], kinds[w], c)
            cp = pltpu.make_async_remote_copy(src_ref=mine, dst_ref=mine, send_sem=ssem.at[w], recv_sem=rsem.at[w],
                                              device_id=sib, device_id_type=MESH)
            cp.start()
            started.append(cp)
        for w in range(n):
            other = _half_of_slab(outs[w], kinds[w], 1 - c)
            pltpu.make_async_remote_copy(src_ref=other, dst_ref=other, send_sem=ssem.at[w], recv_sem=rsem.at[w],
                                         device_id=sib, device_id_type=MESH).wait_recv()
        for cp in started:
            cp.wait_send()

    return pl.pallas_call(
        body, name='rs_final_exchange', out_shape=[jax.ShapeDtypeStruct(s.shape, s.dtype) for s in slabs_half],
        in_specs=[ANY] * n, out_specs=[ANY] * n, input_output_aliases={w: w for w in range(n)},
        scratch_shapes=[pltpu.SemaphoreType.DMA((n,)), pltpu.SemaphoreType.DMA((n,))],
    )(*slabs_half)


def _half_blocks(kind, slab):
    hs = _half_shape(kind, slab)
    if kind == 'pool':
        return (hs[0],), hs[1], 1
    tr = _div_tile(hs[0], max(16, (1 << 19) // hs[1]), 16)
    return (hs[0] // tr,), tr, hs[0] // tr


def _pair_sum(grad, theirs, kind, slab, c_arr, name):
    hs = _half_shape(kind, slab)
    (n_i,), tr, nrb = _half_blocks(kind, slab)
    if kind == 'pool':
        own = pl.BlockSpec((None, hs[1], hs[2]), lambda k, i, s: (i, 2 * k + s[0], 0))
        stk = pl.BlockSpec((None, None, hs[1], hs[2]), lambda k, i, s: (k, i, 0, 0))
    else:
        if kind == 'col':
            own = pl.BlockSpec((tr, hs[1]), lambda k, i, s: (s[0] * nrb + i, k))
        elif kind == 'row':
            own = pl.BlockSpec((tr, hs[1]), lambda k, i, s: ((2 * k + s[0]) * nrb + i, 0))
        else:
            own = pl.BlockSpec((None, tr, hs[1]), lambda k, i, s: (k, s[0] * nrb + i, 0))
        stk = pl.BlockSpec((None, tr, hs[1]), lambda k, i, s: (k, i, 0))

    def body(s_ref, a_ref, b_ref, o_ref):
        o_ref[...] = (a_ref[...].astype(F32) + b_ref[...].astype(F32)).astype(o_ref.dtype)

    return pl.pallas_call(
        body, name=name,
        grid_spec=pltpu.PrefetchScalarGridSpec(num_scalar_prefetch=1, grid=(N_CHIPS, n_i), in_specs=[own, stk],
                                               out_specs=stk),
        out_shape=jax.ShapeDtypeStruct((N_CHIPS,) + hs, BF), compiler_params=_cp(('parallel', 'parallel')),
    )(c_arr, grad, theirs)


def _chip_sum(sums, landed, kind, slab, place_arr, name):
    hs = _half_shape(kind, slab)
    (n_i,), tr, nrb = _half_blocks(kind, slab)
    if kind == 'pool':
        blk = (None, None, hs[1], hs[2])
        mine = pl.BlockSpec(blk, lambda i, s: (s[1], i, 0, 0))
        land = [pl.BlockSpec(blk, lambda i, s, j=j: (j, i, 0, 0)) for j in range(3)]
        out = pl.BlockSpec((None, hs[1], hs[2]), lambda i, s: (i, s[0], 0))
    else:
        blk = (None, tr, hs[1])
        mine = pl.BlockSpec(blk, lambda i, s: (s[1], i, 0))
        land = [pl.BlockSpec(blk, lambda i, s, j=j: (j, i, 0)) for j in range(3)]
        out = pl.BlockSpec((tr, hs[1]), lambda i, s: (s[0] * nrb + i, 0))

    def body(s_ref, a_ref, b_ref, c_ref, d_ref, o_ref):
        o_ref[...] = ((a_ref[...].astype(F32) + b_ref[...].astype(F32)) + c_ref[...].astype(F32)) + d_ref[...].astype(F32)

    return pl.pallas_call(
        body, name=name,
        grid_spec=pltpu.PrefetchScalarGridSpec(num_scalar_prefetch=1, grid=(n_i,), in_specs=[mine] + land, out_specs=out),
        out_shape=jax.ShapeDtypeStruct(tuple(slab), F32), compiler_params=_cp(('parallel',)),
    )(place_arr, sums, landed, landed, landed)


def _all_reduce_small(buf):
    rows, cols = buf.shape

    def body(in_ref, out_ref, land, ssem, rsem):
        x, y, c = _place()
        me = 4 * x + 2 * y + c
        land[me] = in_ref[...]
        started = []
        for j in range(1, 8):
            px = 1 - x if (j >> 2) & 1 else x
            py = 1 - y if (j >> 1) & 1 else y
            pc = 1 - c if j & 1 else c
            cp = pltpu.make_async_remote_copy(src_ref=in_ref, dst_ref=land.at[me], send_sem=ssem.at[j - 1],
                                              recv_sem=rsem.at[j - 1], device_id=(px, py, pc), device_id_type=MESH)
            cp.start()
            started.append(cp)
        for j in range(1, 8):
            px = 1 - x if (j >> 2) & 1 else x
            py = 1 - y if (j >> 1) & 1 else y
            pc = 1 - c if j & 1 else c
            slot = land.at[4 * px + 2 * py + pc]
            pltpu.make_async_remote_copy(src_ref=slot, dst_ref=slot, send_sem=ssem.at[j - 1], recv_sem=rsem.at[j - 1],
                                         device_id=(px, py, pc), device_id_type=MESH).wait_recv()
        for cp in started:
            cp.wait_send()
        acc = land[0]
        for d in range(1, 8):
            acc = acc + land[d]
        out_ref[...] = acc

    return pl.pallas_call(
        body, name='all_reduce_small', out_shape=jax.ShapeDtypeStruct((rows, cols), F32),
        in_specs=[pl.BlockSpec(memory_space=pltpu.VMEM)], out_specs=pl.BlockSpec(memory_space=pltpu.VMEM),
        scratch_shapes=[pltpu.VMEM((8, rows, cols), F32), pltpu.SemaphoreType.DMA((7,)), pltpu.SemaphoreType.DMA((7,))],
    )(buf)


def _elementwise(fn, ins, lead_index, out_shape, out_dtypes, name):
    nd = len(out_shape)
    r, cdim = out_shape[-2], out_shape[-1]
    tr = _div_tile(r, max(16, (1 << 19) // cdim), 16)
    grid = tuple(out_shape[:-2]) + (r // tr,)
    block = (None,) * (nd - 2) + (tr, cdim)

    def spec(lead):
        if lead is None:
            return pl.BlockSpec(block, lambda *g: tuple(g) + (0,))
        return pl.BlockSpec((None,) + block, lambda *g, lead=lead: (lead,) + tuple(g) + (0,))

    n_in = len(ins)

    def body(*refs):
        res = fn(*[r_[...] for r_ in refs[:n_in]])
        for o_ref, v in zip(refs[n_in:], res):
            o_ref[...] = v.astype(o_ref.dtype)

    return pl.pallas_call(
        body, name=name, grid=grid, in_specs=[spec(l) for l in lead_index],
        out_specs=[spec(None) for _ in out_dtypes],
        out_shape=[jax.ShapeDtypeStruct(tuple(out_shape), dt) for dt in out_dtypes],
        compiler_params=_cp(('parallel',) * len(grid)),
    )(*ins)


def _adam_fn(w, g, m, v):
    m = ADAM_B1 * m + (1.0 - ADAM_B1) * g
    v = ADAM_B2 * v + (1.0 - ADAM_B2) * (g * g)
    m_hat = m / (1.0 - ADAM_B1 ** ADAM_STEP)
    v_hat = v / (1.0 - ADAM_B2 ** ADAM_STEP)
    delta = -ADAM_LR * (m_hat / (jnp.sqrt(v_hat) + ADAM_EPS) + ADAM_WD * w)
    return delta, m, v


def _adam(w, g, m, v, name):
    return _elementwise(_adam_fn, [w, g, m, v], [None] * 4, w.shape, [F32] * 3, name)


_DIMS = {'nn': (((1,), (0,)), ((), ())), 'nt': (((1,), (1,)), ((), ())), 'tn': (((0,), (0,)), ((), ()))}


def _mm(a, b, mode, out_dtype, name):
    if mode == 'nn':
        (M, K), N = a.shape, b.shape[1]
    elif mode == 'nt':
        (M, K), N = a.shape, b.shape[0]
    else:
        (K, M), N = a.shape, b.shape[1]
    if mode == 'tn':
        tm, tn, tk = _div_tile(M, 512, 128), _div_tile(N, 1024, 128), _div_tile(K, 2176, 16)
    else:
        tm, tn, tk = _div_tile(M, 1088, 16), _div_tile(N, 1024, 128), _div_tile(K, 2048, 128)
    nk = K // tk
    a_spec = {'nn': pl.BlockSpec((tm, tk), lambda i, j, k: (i, k)), 'nt': pl.BlockSpec((tm, tk), lambda i, j, k: (i, k)),
              'tn': pl.BlockSpec((tk, tm), lambda i, j, k: (k, i))}[mode]
    b_spec = {'nn': pl.BlockSpec((tk, tn), lambda i, j, k: (k, j)), 'nt': pl.BlockSpec((tn, tk), lambda i, j, k: (j, k)),
              'tn': pl.BlockSpec((tk, tn), lambda i, j, k: (k, j))}[mode]
    dims = _DIMS[mode]

    def body(a_ref, b_ref, o_ref, acc_ref):
        part = lax.dot_general(a_ref[...], b_ref[...], dims, preferred_element_type=F32)
        if nk == 1:
            o_ref[...] = part.astype(o_ref.dtype)
        else:
            k = pl.program_id(2)

            @pl.when(k == 0)
            def _():
                acc_ref[...] = part

            @pl.when(k > 0)
            def _():
                acc_ref[...] += part

            @pl.when(k == nk - 1)
            def _():
                o_ref[...] = acc_ref[...].astype(o_ref.dtype)

    return pl.pallas_call(
        body, name=name, grid=(M // tm, N // tn, nk), in_specs=[a_spec, b_spec],
        out_specs=pl.BlockSpec((tm, tn), lambda i, j, k: (i, j)), out_shape=jax.ShapeDtypeStruct((M, N), out_dtype),
        scratch_shapes=[pltpu.VMEM((tm, tn) if nk > 1 else (8, 128), F32)],
        compiler_params=_cp(('parallel', 'parallel', 'arbitrary')),
    )(a, b)


def _rb(tm, w, col=0):
    return pl.BlockSpec((tm, w), lambda i, col=col: (i, col))


def _fixed(shape):
    return pl.BlockSpec(shape, lambda i: (0,) * len(shape))


def _rms(x):
    return lax.rsqrt(jnp.mean(x * x, axis=-1, keepdims=True) + EPS)


def _norm_fwd(x, gain, name, width=None, col=0):
    Lp = x.shape[0]
    width = width or x.shape[1]
    tm = _div_tile(Lp, 256, 16)

    def body(x_ref, g_ref, o_ref):
        v = x_ref[...]
        o_ref[...] = (v * _rms(v) * g_ref[...]).astype(o_ref.dtype)

    return pl.pallas_call(
        body, name=name, grid=(Lp // tm,), in_specs=[_rb(tm, width, col), _fixed((1, width))], out_specs=_rb(tm, width),
        out_shape=jax.ShapeDtypeStruct((Lp, width), BF), compiler_params=_cp(('parallel',)),
    )(x, gain)


def _post_residual(h, f, gain, scale, name):
    Lp, D = h.shape
    tm = _div_tile(Lp, 256, 16)

    def body(h_ref, f_ref, g_ref, o_ref):
        v = f_ref[...]
        o_ref[...] = h_ref[...] + scale * (v * _rms(v) * g_ref[...])

    return pl.pallas_call(
        body, name=name, grid=(Lp // tm,), in_specs=[_rb(tm, D), _rb(tm, D), _fixed((1, D))], out_specs=_rb(tm, D),
        out_shape=jax.ShapeDtypeStruct((Lp, D), F32), compiler_params=_cp(('parallel',)),
    )(h, f, gain)


def _post_bwd(dh, f, gain, scale, name):
    Lp, D = dh.shape
    tm = _div_tile(Lp, 256, 16)

    def body(dh_ref, f_ref, g_ref, df_ref, dg_ref):
        v = f_ref[...]
        r = _rms(v)
        dy = scale * dh_ref[...]
        w = dy * g_ref[...]
        df_ref[...] = (r * w - v * (r * r * r) * jnp.mean(w * v, axis=-1, keepdims=True)).astype(df_ref.dtype)
        part = jnp.sum(dy * v * r, axis=0, keepdims=True)

        @pl.when(pl.program_id(0) == 0)
        def _():
            dg_ref[...] = part

        @pl.when(pl.program_id(0) > 0)
        def _():
            dg_ref[...] += part

    return pl.pallas_call(
        body, name=name, grid=(Lp // tm,), in_specs=[_rb(tm, D), _rb(tm, D), _fixed((1, D))],
        out_specs=[_rb(tm, D), _fixed((1, D))],
        out_shape=[jax.ShapeDtypeStruct((Lp, D), BF), jax.ShapeDtypeStruct((1, D), F32)],
        compiler_params=_cp(('arbitrary',)),
    )(dh, f, gain)


def _pre_bwd(dres, dn, h, gain, name):
    Lp, D = h.shape
    tm = _div_tile(Lp, 256, 16)

    def body(dres_ref, dn_ref, h_ref, g_ref, dh_ref, dg_ref):
        v = h_ref[...]
        r = _rms(v)
        dy = dn_ref[...]
        w = dy * g_ref[...]
        dh_ref[...] = dres_ref[...] + r * w - v * (r * r * r) * jnp.mean(w * v, axis=-1, keepdims=True)
        part = jnp.sum(dy * v * r, axis=0, keepdims=True)

        @pl.when(pl.program_id(0) == 0)
        def _():
            dg_ref[...] = part

        @pl.when(pl.program_id(0) > 0)
        def _():
            dg_ref[...] += part

    return pl.pallas_call(
        body, name=name, grid=(Lp // tm,), in_specs=[_rb(tm, D), _rb(tm, D), _rb(tm, D), _fixed((1, D))],
        out_specs=[_rb(tm, D), _fixed((1, D))],
        out_shape=[jax.ShapeDtypeStruct((Lp, D), F32), jax.ShapeDtypeStruct((1, D), F32)],
        compiler_params=_cp(('arbitrary',)),
    )(dres, dn, h, gain)


def _swiglu_fwd(gu, name):
    Lp, F2 = gu.shape
    F = F2 // 2
    tm = _div_tile(Lp, 256, 16)

    def body(gu_ref, a_ref):
        g = gu_ref[:, :F].astype(F32)
        u = gu_ref[:, F:].astype(F32)
        a_ref[...] = (g * jax.nn.sigmoid(g) * u).astype(a_ref.dtype)

    return pl.pallas_call(
        body, name=name, grid=(Lp // tm,), in_specs=[_rb(tm, F2)], out_specs=_rb(tm, F),
        out_shape=jax.ShapeDtypeStruct((Lp, F), BF), compiler_params=_cp(('parallel',)),
    )(gu)


def _swiglu_bwd(da, gu, name):
    Lp, F2 = gu.shape
    F = F2 // 2
    tm = _div_tile(Lp, 256, 16)

    def body(da_ref, gu_ref, o_ref):
        g = gu_ref[:, :F].astype(F32)
        u = gu_ref[:, F:].astype(F32)
        da_ = da_ref[...].astype(F32)
        s = jax.nn.sigmoid(g)
        o_ref[:, :F] = (da_ * u * (s * (1.0 + g * (1.0 - s)))).astype(o_ref.dtype)
        o_ref[:, F:] = (da_ * (g * s)).astype(o_ref.dtype)

    return pl.pallas_call(
        body, name=name, grid=(Lp // tm,), in_specs=[_rb(tm, F), _rb(tm, F2)], out_specs=_rb(tm, F2),
        out_shape=jax.ShapeDtypeStruct((Lp, F2), BF), compiler_params=_cp(('parallel',)),
    )(da, gu)


def _loss_bwd(h, tgt, n_meta, n_real, name):
    Lp, D = h.shape
    tm = _div_tile(Lp, 256, 16)

    def body(h_ref, t_ref, d_ref, l_ref):
        row = lax.broadcasted_iota(jnp.int32, (tm, 1), 0) + pl.program_id(0) * tm
        ok = (row >= n_meta) & (row < n_meta + n_real)
        err = jnp.where(ok, h_ref[...] - t_ref[...], 0.0)
        d_ref[...] = err / D
        part = jnp.full((1, 128), jnp.sum(err * err), F32)

        @pl.when(pl.program_id(0) == 0)
        def _():
            l_ref[...] = part

        @pl.when(pl.program_id(0) > 0)
        def _():
            l_ref[...] += part

    return pl.pallas_call(
        body, name=name, grid=(Lp // tm,), in_specs=[_rb(tm, D), _rb(tm, D)], out_specs=[_rb(tm, D), _fixed((1, 128))],
        out_shape=[jax.ShapeDtypeStruct((Lp, D), F32), jax.ShapeDtypeStruct((1, 128), F32)],
        compiler_params=_cp(('arbitrary',)),
    )(h, tgt)


def _split_bf16(v):
    hi = v.astype(BF)
    return hi, (v - hi.astype(F32)).astype(BF)


def _pool_fwd(z, name):
    Lp = z.shape[0]
    T = _div_tile(Lp, 256, 16)
    G = len(POOL_WINDOWS)

    def body(cur_ref, prev_ref, d_ref):
        i, g = pl.program_id(0), pl.program_id(1)
        w = jnp.left_shift(2, g)
        rr = lax.broadcasted_iota(jnp.int32, (T, T), 0)
        cc = lax.broadcasted_iota(jnp.int32, (T, T), 1)
        b_cur = jnp.where((cc <= rr) & (cc > rr - w), 1.0, 0.0).astype(BF)
        w_prev = jnp.where(i > 0, w, 0)
        b_prev = jnp.where(cc - T > rr - w_prev, 1.0, 0.0).astype(BF)
        u = cur_ref[...]
        s = jnp.zeros((T, POOL_GROUP), F32)
        for part in _split_bf16(u):
            s += jnp.dot(b_cur, part, preferred_element_type=F32)
        for part in _split_bf16(prev_ref[...]):
            s += jnp.dot(b_prev, part, preferred_element_type=F32)
        t = lax.broadcasted_iota(jnp.int32, (T, 1), 0) + i * T
        cnt = jnp.minimum(w, t + 1).astype(F32)
        d_ref[...] = (s / cnt - u).astype(d_ref.dtype)

    return pl.pallas_call(
        body, name=name, grid=(Lp // T, G),
        in_specs=[pl.BlockSpec((T, POOL_GROUP), lambda i, g: (i, g)),
                  pl.BlockSpec((T, POOL_GROUP), lambda i, g: (jnp.maximum(i - 1, 0), g))],
        out_specs=pl.BlockSpec((T, POOL_GROUP), lambda i, g: (i, g)),
        out_shape=jax.ShapeDtypeStruct((Lp, POOL_WIDTH), BF), compiler_params=_cp(('parallel', 'parallel')),
    )(z, z)


def _pool_bwd(dd, name):
    Lp = dd.shape[0]
    T = _div_tile(Lp, 256, 16)
    G = len(POOL_WINDOWS)
    n_t = Lp // T

    def body(cur_ref, next_ref, o_ref):
        i, g = pl.program_id(0), pl.program_id(1)
        w = jnp.left_shift(2, g)
        rr = lax.broadcasted_iota(jnp.int32, (T, T), 0)
        cc = lax.broadcasted_iota(jnp.int32, (T, T), 1)
        b_cur = jnp.where((cc >= rr) & (cc < rr + w), 1.0, 0.0).astype(BF)
        w_next = jnp.where(i < n_t - 1, w, 0)
        b_next = jnp.where(cc + T < rr + w_next, 1.0, 0.0).astype(BF)
        t = lax.broadcasted_iota(jnp.int32, (T, 1), 0) + i * T
        cur = cur_ref[...]
        e_cur = cur / jnp.minimum(w, t + 1).astype(F32)
        e_next = next_ref[...] / jnp.minimum(w, t + T + 1).astype(F32)
        s = jnp.zeros((T, POOL_GROUP), F32)
        for part in _split_bf16(e_cur):
            s += jnp.dot(b_cur, part, preferred_element_type=F32)
        for part in _split_bf16(e_next):
            s += jnp.dot(b_next, part, preferred_element_type=F32)
        o_ref[...] = (s - cur).astype(o_ref.dtype)

    return pl.pallas_call(
        body, name=name, grid=(n_t, G),
        in_specs=[pl.BlockSpec((T, POOL_GROUP), lambda i, g: (i, g)),
                  pl.BlockSpec((T, POOL_GROUP), lambda i, g: (jnp.minimum(i + 1, n_t - 1), g))],
        out_specs=pl.BlockSpec((T, POOL_GROUP), lambda i, g: (i, g)),
        out_shape=jax.ShapeDtypeStruct((Lp, POOL_WIDTH), BF), compiler_params=_cp(('parallel', 'parallel')),
    )(dd, dd)


def _pool_mix_fwd(d, pool_w, scale, name):
    Lp = d.shape[0]
    G = len(POOL_WINDOWS)
    tm = _div_tile(Lp, 1088, 16)

    def body(d_ref, w_ref, s_ref, e_ref, y_ref):
        e = jnp.dot(d_ref[...], w_ref[...], preferred_element_type=F32)
        e_ref[...] = e.astype(e_ref.dtype)
        y_ref[...] = (e * s_ref[...]).astype(y_ref.dtype)

    blk = pl.BlockSpec((tm, POOL_GROUP), lambda g, i: (i, g))
    return pl.pallas_call(
        body, name=name, grid=(G, Lp // tm),
        in_specs=[blk, pl.BlockSpec((None, POOL_GROUP, POOL_GROUP), lambda g, i: (g, 0, 0)),
                  pl.BlockSpec((1, POOL_GROUP), lambda g, i: (0, g))],
        out_specs=[blk, blk], out_shape=[jax.ShapeDtypeStruct((Lp, POOL_WIDTH), BF)] * 2,
        compiler_params=_cp(('parallel', 'parallel')),
    )(d, pool_w, scale)


def _pool_mix_bwd(dyp, e, d, pool_w, scale, name):
    Lp = d.shape[0]
    G = len(POOL_WINDOWS)
    tm = _div_tile(Lp, 1088, 16)

    def body(dy_ref, e_ref, d_ref, w_ref, s_ref, dd_ref, ds_ref, dw_ref):
        i = pl.program_id(1)
        dy = dy_ref[...]
        de = (dy * s_ref[...]).astype(BF)
        dd_ref[...] = lax.dot_general(de, w_ref[...], _DIMS['nt'], preferred_element_type=F32)
        ds_part = jnp.sum(dy * e_ref[...].astype(F32), axis=0, keepdims=True)
        dw_part = lax.dot_general(d_ref[...], de, _DIMS['tn'], preferred_element_type=F32)

        @pl.when(i == 0)
        def _():
            ds_ref[...] = ds_part
            dw_ref[...] = dw_part

        @pl.when(i > 0)
        def _():
            ds_ref[...] += ds_part
            dw_ref[...] += dw_part

    blk = pl.BlockSpec((tm, POOL_GROUP), lambda g, i: (i, g))
    wblk = pl.BlockSpec((None, POOL_GROUP, POOL_GROUP), lambda g, i: (g, 0, 0))
    sblk = pl.BlockSpec((1, POOL_GROUP), lambda g, i: (0, g))
    return pl.pallas_call(
        body, name=name, grid=(G, Lp // tm), in_specs=[blk, blk, blk, wblk, sblk], out_specs=[blk, sblk, wblk],
        out_shape=[jax.ShapeDtypeStruct((Lp, POOL_WIDTH), F32), jax.ShapeDtypeStruct((1, POOL_WIDTH), F32),
                   jax.ShapeDtypeStruct((G, POOL_GROUP, POOL_GROUP), F32)],
        compiler_params=_cp(('parallel', 'arbitrary')),
    )(dyp, e, d, pool_w, scale)


def _rot_half(t):
    lane = lax.broadcasted_iota(jnp.int32, t.shape, 1)
    half = QK_ROPE // 2
    return jnp.where(lane < half, -pltpu.roll(t, 128 - half, 1), pltpu.roll(t, half, 1))


def _lora_norms(z, q_gain, kv_gain, lay, name):
    Lp = z.shape[0]
    QL, KVL = lay['QL'], lay['KVL']
    tm = _div_tile(Lp, 256, 16)

    def body(q_ref, kv_ref, qg_ref, kg_ref, qo_ref, ko_ref):
        a = q_ref[...]
        qo_ref[...] = (a * _rms(a) * qg_ref[...]).astype(BF)
        b = kv_ref[...]
        ko_ref[...] = (b * _rms(b) * kg_ref[...]).astype(BF)

    return pl.pallas_call(
        body, name=name, grid=(Lp // tm,),
        in_specs=[_rb(tm, QL, lay['cq'] // QL), _rb(tm, KVL, lay['ckv'] // KVL), _fixed((1, QL)), _fixed((1, KVL))],
        out_specs=[_rb(tm, QL), _rb(tm, KVL)],
        out_shape=[jax.ShapeDtypeStruct((Lp, QL), BF), jax.ShapeDtypeStruct((Lp, KVL), BF)],
        compiler_params=_cp(('parallel',)),
    )(z, z, q_gain, kv_gain)


def _lora_norms_bwd(dqn, dkn, z, q_gain, kv_gain, lay, name):
    Lp = z.shape[0]
    QL, KVL = lay['QL'], lay['KVL']
    tm = _div_tile(Lp, 256, 16)

    def one(dy, v, gain):
        r = _rms(v)
        w = dy * gain
        return r * w - v * (r * r * r) * jnp.mean(w * v, axis=-1, keepdims=True), jnp.sum(dy * v * r, axis=0, keepdims=True)

    def body(dq_ref, dk_ref, q_ref, kv_ref, qg_ref, kg_ref, o_ref, dqg_ref, dkg_ref):
        da, ga = one(dq_ref[...], q_ref[...], qg_ref[...])
        db, gb = one(dk_ref[...], kv_ref[...], kg_ref[...])
        o_ref[:, :QL] = da.astype(BF)
        o_ref[:, QL:] = db.astype(BF)

        @pl.when(pl.program_id(0) == 0)
        def _():
            dqg_ref[...] = ga
            dkg_ref[...] = gb

        @pl.when(pl.program_id(0) > 0)
        def _():
            dqg_ref[...] += ga
            dkg_ref[...] += gb

    return pl.pallas_call(
        body, name=name, grid=(Lp // tm,),
        in_specs=[_rb(tm, QL), _rb(tm, KVL), _rb(tm, QL, lay['cq'] // QL), _rb(tm, KVL, lay['ckv'] // KVL),
                  _fixed((1, QL)), _fixed((1, KVL))],
        out_specs=[_rb(tm, QL + KVL), _fixed((1, QL)), _fixed((1, KVL))],
        out_shape=[jax.ShapeDtypeStruct((Lp, QL + KVL), BF), jax.ShapeDtypeStruct((1, QL), F32),
                   jax.ShapeDtypeStruct((1, KVL), F32)],
        compiler_params=_cp(('arbitrary',)),
    )(dqn, dkn, z, z, q_gain, kv_gain)


def _qk_prep(q_raw, kv, z, cos, sin, lay, H, name):
    Lp = q_raw.shape[0]
    W = H * HEAD_PAD
    tm = _div_tile(Lp, 256, 16)

    def body(q_ref, kv_ref, kr_ref, c_ref, s_ref, qo_ref, ko_ref):
        c, s = c_ref[...], s_ref[...]

        def rope(t):
            return t * c + _rot_half(t) * s

        kpe = rope(kr_ref[...]).astype(BF)
        for h in range(H):
            b = h * HEAD_PAD
            qo_ref[:, b:b + 128] = (q_ref[:, b:b + 128] * SOFTMAX_SCALE).astype(BF)
            qo_ref[:, b + 128:b + 256] = (rope(q_ref[:, b + 128:b + 256]) * SOFTMAX_SCALE).astype(BF)
            ko_ref[:, b:b + 128] = kv_ref[:, b:b + 128]
            ko_ref[:, b + 128:b + 256] = kpe

    return pl.pallas_call(
        body, name=name, grid=(Lp // tm,),
        in_specs=[_rb(tm, W), _rb(tm, W), _rb(tm, 128, lay['kr'] // 128), _rb(tm, 128), _rb(tm, 128)],
        out_specs=[_rb(tm, W), _rb(tm, W)], out_shape=[jax.ShapeDtypeStruct((Lp, W), BF)] * 2,
        compiler_params=_cp(('parallel',)),
    )(q_raw, kv, z, cos, sin)


def _qk_prep_bwd(dQ, dK, dV, cos, sin, H, name):
    Lp = dQ.shape[0]
    W = H * HEAD_PAD
    tm = _div_tile(Lp, 256, 16)

    def body(dq_ref, dk_ref, dv_ref, c_ref, s_ref, qo_ref, kvo_ref, kro_ref):
        c, s = c_ref[...], s_ref[...]

        def unrope(t):
            return t * c - _rot_half(t * s)

        acc = jnp.zeros((tm, 128), F32)
        for h in range(H):
            b = h * HEAD_PAD
            qo_ref[:, b:b + 128] = (dq_ref[:, b:b + 128] * SOFTMAX_SCALE).astype(BF)
            qo_ref[:, b + 128:b + 256] = (unrope(dq_ref[:, b + 128:b + 256]) * SOFTMAX_SCALE).astype(BF)
            kvo_ref[:, b:b + 128] = dk_ref[:, b:b + 128].astype(BF)
            kvo_ref[:, b + 128:b + 256] = dv_ref[:, h * V_DIM:(h + 1) * V_DIM]
            acc += dk_ref[:, b + 128:b + 256]
        kro_ref[...] = unrope(acc).astype(BF)

    return pl.pallas_call(
        body, name=name, grid=(Lp // tm,),
        in_specs=[_rb(tm, W), _rb(tm, W), _rb(tm, H * V_DIM), _rb(tm, 128), _rb(tm, 128)],
        out_specs=[_rb(tm, W), _rb(tm, W), _rb(tm, 128)],
        out_shape=[jax.ShapeDtypeStruct((Lp, W), BF), jax.ShapeDtypeStruct((Lp, W), BF),
                   jax.ShapeDtypeStruct((Lp, 128), BF)],
        compiler_params=_cp(('parallel',)),
    )(dQ, dK, dV, cos, sin)


def _flash_fwd(Q, K, kv, H, name):
    Lp = Q.shape[0]
    T = _div_tile(Lp, 256, 16)

    CH = FLASH_CHUNK * T

    def body(q_ref, k_ref, v_ref, o_ref, lse_ref, m_s, l_s, acc_s):
        i = pl.program_id(1)
        q = q_ref[...]
        m_s[...] = jnp.full((T, 1), NEG, F32)
        l_s[...] = jnp.zeros((T, 1), F32)
        acc_s[...] = jnp.zeros((T, V_DIM), F32)

        def step(start, width, masked):
            rows = pl.ds(start, width)
            s = lax.dot_general(q, k_ref[rows, :], _DIMS['nt'], preferred_element_type=F32)
            if masked:
                rr = lax.broadcasted_iota(jnp.int32, (T, width), 0) + i * T
                cc = lax.broadcasted_iota(jnp.int32, (T, width), 1) + start
                s = jnp.where(cc <= rr, s, NEG)
            m = m_s[...]
            m_new = jnp.maximum(m, jnp.max(s, axis=-1, keepdims=True))
            alpha = jnp.exp(m - m_new)
            p = jnp.exp(s - m_new)
            l_s[...] = alpha * l_s[...] + jnp.sum(p, axis=-1, keepdims=True)
            acc_s[...] = alpha * acc_s[...] + jnp.dot(p.astype(BF), v_ref[rows, :], preferred_element_type=F32)
            m_s[...] = m_new

        n_full = i // FLASH_CHUNK

        def full(cidx, carry):
            step(pl.multiple_of(cidx * CH, CH), CH, False)
            return carry

        lax.fori_loop(0, n_full, full, 0)
        for nb in range(1, FLASH_CHUNK + 1):
            @pl.when(i % FLASH_CHUNK == nb - 1)
            def _(nb=nb):
                step(pl.multiple_of(n_full * CH, CH), nb * T, True)

        l = l_s[...]
        o_ref[...] = (acc_s[...] / l).astype(o_ref.dtype)
        lse_ref[...] = jnp.broadcast_to(m_s[...] + jnp.log(l), (T, 128))

    return pl.pallas_call(
        body, name=name, grid=(H, Lp // T),
        in_specs=[pl.BlockSpec((T, HEAD_PAD), lambda h, i: (i, h)), pl.BlockSpec((Lp, HEAD_PAD), lambda h, i: (0, h)),
                  pl.BlockSpec((Lp, V_DIM), lambda h, i: (0, 2 * h + 1))],
        out_specs=[pl.BlockSpec((T, V_DIM), lambda h, i: (i, h)), pl.BlockSpec((None, T, 128), lambda h, i: (h, i, 0))],
        out_shape=[jax.ShapeDtypeStruct((Lp, H * V_DIM), BF), jax.ShapeDtypeStruct((H, Lp, 128), F32)],
        scratch_shapes=[pltpu.VMEM((T, 1), F32), pltpu.VMEM((T, 1), F32), pltpu.VMEM((T, V_DIM), F32)],
        compiler_params=_cp(('parallel', 'parallel')),
    )(Q, K, kv)


def _flash_bwd(Q, K, kv, O, dO, lse, H, name):
    Lp = Q.shape[0]
    T = _div_tile(Lp, 256, 16)
    n_t = Lp // T

    def body(q_ref, k_ref, v_ref, o_ref, do_ref, lse_ref, dq_ref, dk_ref, dv_ref, dk_acc, dv_acc):
        j = pl.program_id(1)

        @pl.when(j == 0)
        def _():
            dq_ref[...] = jnp.zeros_like(dq_ref)

        kj, vj = k_ref[...], v_ref[...]
        dk_acc[...] = jnp.zeros_like(dk_acc)
        dv_acc[...] = jnp.zeros_like(dv_acc)

        def step(start, width, masked):
            rows = pl.ds(start, width)
            qi, doi = q_ref[rows, :], do_ref[rows, :]
            delta = jnp.sum(doi.astype(F32) * o_ref[rows, :].astype(F32), axis=-1, keepdims=True)
            s = lax.dot_general(qi, kj, _DIMS['nt'], preferred_element_type=F32)
            p = jnp.exp(s - lse_ref[rows, :][:, :1])
            if masked:
                rr = lax.broadcasted_iota(jnp.int32, (width, T), 0) + start
                cc = lax.broadcasted_iota(jnp.int32, (width, T), 1) + j * T
                p = jnp.where(cc <= rr, p, 0.0)
            dp = lax.dot_general(doi, vj, _DIMS['nt'], preferred_element_type=F32)
            ds = (p * (dp - delta)).astype(BF)
            dv_acc[...] += lax.dot_general(p.astype(BF), doi, _DIMS['tn'], preferred_element_type=F32)
            dk_acc[...] += lax.dot_general(ds, qi, _DIMS['tn'], preferred_element_type=F32)
            dq_ref[rows, :] += jnp.dot(ds, kj, preferred_element_type=F32)

        head = (n_t - 1 - j) % FLASH_CHUNK + 1
        for nb in range(1, FLASH_CHUNK + 1):
            @pl.when(head == nb)
            def _(nb=nb):
                step(pl.multiple_of(j * T, T), nb * T, True)

        def full(cidx, carry):
            step(pl.multiple_of((j + head + cidx * FLASH_CHUNK) * T, T), FLASH_CHUNK * T, False)
            return carry

        lax.fori_loop(0, (n_t - j - head) // FLASH_CHUNK, full, 0)
        dk_ref[...] = dk_acc[...]
        dv_ref[...] = dv_acc[...].astype(dv_ref.dtype)

    head_q = pl.BlockSpec((Lp, HEAD_PAD), lambda h, j: (0, h))
    head_v = pl.BlockSpec((Lp, V_DIM), lambda h, j: (0, h))
    return pl.pallas_call(
        body, name=name, grid=(H, n_t),
        in_specs=[head_q, pl.BlockSpec((T, HEAD_PAD), lambda h, j: (j, h)),
                  pl.BlockSpec((T, V_DIM), lambda h, j: (j, 2 * h + 1)), head_v, head_v,
                  pl.BlockSpec((None, Lp, 128), lambda h, j: (h, 0, 0))],
        out_specs=[head_q, pl.BlockSpec((T, HEAD_PAD), lambda h, j: (j, h)),
                   pl.BlockSpec((T, V_DIM), lambda h, j: (j, h))],
        out_shape=[jax.ShapeDtypeStruct((Lp, H * HEAD_PAD), F32), jax.ShapeDtypeStruct((Lp, H * HEAD_PAD), F32),
                   jax.ShapeDtypeStruct((Lp, H * V_DIM), BF)],
        scratch_shapes=[pltpu.VMEM((T, HEAD_PAD), F32), pltpu.VMEM((T, V_DIM), F32)],
        compiler_params=_cp(('parallel', 'arbitrary')),
    )(Q, K, kv, O, dO, lse)


def _gate_fwd(z, y_pool, y_mla, lay, name):
    Lp, D = y_pool.shape
    tm = _div_tile(Lp, 256, 16)

    def body(gp_ref, gm_ref, yp_ref, ym_ref, o_ref):
        o_ref[...] = (jax.nn.sigmoid(gp_ref[...]) * yp_ref[...] + jax.nn.sigmoid(gm_ref[...]) * ym_ref[...]).astype(BF)

    return pl.pallas_call(
        body, name=name, grid=(Lp // tm,),
        in_specs=[_rb(tm, D, lay['gp'] // D), _rb(tm, D, lay['gm'] // D), _rb(tm, D), _rb(tm, D)], out_specs=_rb(tm, D),
        out_shape=jax.ShapeDtypeStruct((Lp, D), BF), compiler_params=_cp(('parallel',)),
    )(z, z, y_pool, y_mla)


def _gate_bwd(dy, z, y_pool, y_mla, lay, name):
    Lp, D = y_pool.shape
    tm = _div_tile(Lp, 256, 16)

    def body(dy_ref, gp_ref, gm_ref, yp_ref, ym_ref, dp_ref, dm_ref, dg_ref):
        dy_ = dy_ref[...]
        sp, sm = jax.nn.sigmoid(gp_ref[...]), jax.nn.sigmoid(gm_ref[...])
        dp_ref[...] = (dy_ * sp).astype(BF)
        dm_ref[...] = (dy_ * sm).astype(BF)
        dg_ref[:, :D] = (dy_ * yp_ref[...] * (sp * (1.0 - sp))).astype(BF)
        dg_ref[:, D:] = (dy_ * ym_ref[...] * (sm * (1.0 - sm))).astype(BF)

    return pl.pallas_call(
        body, name=name, grid=(Lp // tm,),
        in_specs=[_rb(tm, D), _rb(tm, D, lay['gp'] // D), _rb(tm, D, lay['gm'] // D), _rb(tm, D), _rb(tm, D)],
        out_specs=[_rb(tm, D), _rb(tm, D), _rb(tm, 2 * D)],
        out_shape=[jax.ShapeDtypeStruct((Lp, D), BF), jax.ShapeDtypeStruct((Lp, D), BF),
                   jax.ShapeDtypeStruct((Lp, 2 * D), BF)],
        compiler_params=_cp(('parallel',)),
    )(dy, z, z, y_pool, y_mla)


def _z_layout(D, QL, KVL):
    cq = POOL_WIDTH
    ckv = cq + QL
    gp = -(-(ckv + KVL) // D) * D
    gm = gp + D
    kr = gm + D
    return dict(QL=QL, KVL=KVL, cq=cq, ckv=ckv, gp=gp, gm=gm, kr=kr, width=kr + 128)


def _w_in_aligned(w_log, lay, D):
    n0 = POOL_WIDTH + lay['QL'] + lay['KVL']
    parts = [w_log[:, :n0], jnp.zeros((D, lay['gp'] - n0), w_log.dtype), w_log[:, n0 + QK_ROPE:],
             w_log[:, n0:n0 + QK_ROPE], jnp.zeros((D, 128 - QK_ROPE), w_log.dtype)]
    return jnp.concatenate(parts, axis=1)


def _w_in_logical(w_al, lay, D):
    n0 = POOL_WIDTH + lay['QL'] + lay['KVL']
    return jnp.concatenate([w_al[:, :n0], w_al[:, lay['kr']:lay['kr'] + QK_ROPE], w_al[:, lay['gp']:lay['gp'] + 2 * D]],
                           axis=1)


def _ffn_fwd(h, pre, post, w_gu, w_down, tag):
    n = _norm_fwd(h, pre, f'{tag}_norm')
    gu = _mm(n, w_gu, 'nn', BF, f'{tag}_gu')
    a = _swiglu_fwd(gu, f'{tag}_act')
    f = _mm(a, w_down, 'nn', F32, f'{tag}_down')
    out = _post_residual(h, f, post, 0.5, f'{tag}_res')
    return out, (h, n, gu, a, f)


def _ffn_bwd(dh, saved, pre, post, w_gu, w_down, tag):
    h, n, gu, a, f = saved
    df, d_post = _post_bwd(dh, f, post, 0.5, f'{tag}_res_bwd')
    d_w_down = _mm(a, df, 'tn', BF, f'{tag}_dw_down')
    da = _mm(df, w_down, 'nt', BF, f'{tag}_da')
    dgu = _swiglu_bwd(da, gu, f'{tag}_act_bwd')
    d_w_gu = _mm(n, dgu, 'tn', BF, f'{tag}_dw_gu')
    dn = _mm(dgu, w_gu, 'nt', F32, f'{tag}_dn')
    dh_in, d_pre = _pre_bwd(dh, dn, h, pre, f'{tag}_norm_bwd')
    return dh_in, d_pre, d_post, d_w_gu, d_w_down


def kernel(x, meta_tokens, norm_ffn1_pre, norm_ffn1_post, ffn1_w_gu, ffn1_w_down, norm_mix_pre, norm_mix_post, w_in, pool_w, pool_scale, w_pool_o, q_a_norm, w_q_b, kv_a_norm, w_kv_b, w_mla_o, w_out, norm_ffn2_pre, norm_ffn2_post, ffn2_w_gu, ffn2_w_down, loss_target, m_meta_tokens, m_norm_ffn1_pre, m_norm_ffn1_post, m_ffn1_w_gu, m_ffn1_w_down, m_norm_mix_pre, m_norm_mix_post, m_w_in, m_pool_w, m_pool_scale, m_w_pool_o, m_q_a_norm, m_w_q_b, m_kv_a_norm, m_w_kv_b, m_w_mla_o, m_w_out, m_norm_ffn2_pre, m_norm_ffn2_post, m_ffn2_w_gu, m_ffn2_w_down, v_meta_tokens, v_norm_ffn1_pre, v_norm_ffn1_post, v_ffn1_w_gu, v_ffn1_w_down, v_norm_mix_pre, v_norm_mix_post, v_w_in, v_pool_w, v_pool_scale, v_w_pool_o, v_q_a_norm, v_w_q_b, v_kv_a_norm, v_w_kv_b, v_w_mla_o, v_w_out, v_norm_ffn2_pre, v_norm_ffn2_post, v_ffn2_w_gu, v_ffn2_w_down):
    given = dict(locals())
    W = {n: given[n] for n in WEIGHTS}
    M = {n: given['m_' + n] for n in WEIGHTS}
    V = {n: given['v_' + n] for n in WEIGHTS}

    S, D = x.shape[1], x.shape[2]
    NM = meta_tokens.shape[0]
    L = NM + S
    Lp = -(-L // ROW_ALIGN) * ROW_ALIGN
    QL, KVL = w_q_b.shape[1], w_kv_b.shape[1]
    H = w_q_b.shape[2] * N_CHIPS // QK_DIM
    lay = _z_layout(D, QL, KVL)
    gx, gy = lax.axis_index('x'), lax.axis_index('y')
    chip = 2 * gx + gy

    names = list(BIG)
    slab = {n: W[n][0] for n in names}
    kinds = [BIG[n] for n in names] + ['col']
    gc = lax.axis_index('c')
    c_arr = jnp.stack([gc]).astype(jnp.int32)
    place_arr = jnp.stack([gc, chip]).astype(jnp.int32)

    def in_place(shard, kind):
        shape = _full_shape(kind, shard.shape)
        if kind == 'lead':
            start = (chip, 0, 0)
        elif kind == 'pool':
            start = (0, chip * shard.shape[1], 0)
        elif kind == 'row':
            start = (chip * shard.shape[0], 0)
        else:
            start = (0, chip * shard.shape[1])
        upd = shard[None] if kind == 'lead' else shard
        return lax.dynamic_update_slice(jnp.zeros(shape, shard.dtype), upd, start)

    gathered = _all_gather([in_place(s, k) for s, k in zip([slab[n].astype(BF) for n in names] + [meta_tokens], kinds)],
                           kinds)
    full = dict(zip(names, gathered[:-1]))
    meta_full = gathered[-1]

    in_cols = full['w_in'].shape[0] * full['w_in'].shape[2]
    w_in_al = _w_in_aligned(full['w_in'].transpose(1, 0, 2).reshape(D, in_cols), lay, D)
    w_q_pad = jnp.pad(full['w_q_b'].reshape(QL, H, QK_DIM), ((0, 0), (0, 0), (0, HEAD_PAD - QK_DIM))).reshape(
        QL, H * HEAD_PAD)

    pos = jnp.arange(Lp, dtype=F32)
    inv = ROPE_THETA ** (-jnp.arange(0, QK_ROPE, 2, dtype=F32) / QK_ROPE)
    ang = pos[:, None] * inv[None, :]
    ang = jnp.concatenate([ang, ang], axis=-1)
    cos = jnp.pad(jnp.cos(ang), ((0, 0), (0, 128 - QK_ROPE)), constant_values=1.0)
    sin = jnp.pad(jnp.sin(ang), ((0, 0), (0, 128 - QK_ROPE)))

    h0 = jnp.concatenate([meta_full, x[0], jnp.zeros((Lp - L, D), F32)], axis=0)
    tgt = jnp.pad(loss_target[0], ((NM, Lp - L), (0, 0)))

    h1, ffn1_saved = _ffn_fwd(h0, norm_ffn1_pre, norm_ffn1_post, full['ffn1_w_gu'], full['ffn1_w_down'], 'ffn1')

    n2 = _norm_fwd(h1, norm_mix_pre, 'mix_norm')
    z = _mm(n2, w_in_al, 'nn', F32, 'mix_in')
    d_pool = _pool_fwd(z, 'pool_fwd')
    e_pool, yp = _pool_mix_fwd(d_pool, full['pool_w'], pool_scale, 'pool_mix')
    y_pool = _mm(yp, full['w_pool_o'], 'nn', F32, 'pool_out')
    cqn, ckvn = _lora_norms(z, q_a_norm, kv_a_norm, lay, 'lora_norms')
    q_raw = _mm(cqn, w_q_pad, 'nn', F32, 'mla_q')
    kv = _mm(ckvn, full['w_kv_b'], 'nn', BF, 'mla_kv')
    Q, K = _qk_prep(q_raw, kv, z, cos, sin, lay, H, 'qk_prep')
    O, lse = _flash_fwd(Q, K, kv, H, 'flash_fwd')
    y_mla = _mm(O, full['w_mla_o'], 'nn', F32, 'mla_out')
    y = _gate_fwd(z, y_pool, y_mla, lay, 'gate')
    m_mix = _mm(y, full['w_out'], 'nn', F32, 'mix_out')
    h2 = _post_residual(h1, m_mix, norm_mix_post, 1.0, 'mix_res')

    h3, ffn2_saved = _ffn_fwd(h2, norm_ffn2_pre, norm_ffn2_post, full['ffn2_w_gu'], full['ffn2_w_down'], 'ffn2')

    dh3, sq = _loss_bwd(h3, tgt, NM, S, 'loss')
    G = {}
    dh2, G['norm_ffn2_pre'], G['norm_ffn2_post'], G['ffn2_w_gu'], G['ffn2_w_down'] = _ffn_bwd(
        dh3, ffn2_saved, norm_ffn2_pre, norm_ffn2_post, full['ffn2_w_gu'], full['ffn2_w_down'], 'ffn2')

    dm, G['norm_mix_post'] = _post_bwd(dh2, m_mix, norm_mix_post, 1.0, 'mix_res_bwd')
    G['w_out'] = _mm(y, dm, 'tn', BF, 'dw_out')
    dy = _mm(dm, full['w_out'], 'nt', F32, 'mix_out_bwd')
    dy_pool, dy_mla, d_gate = _gate_bwd(dy, z, y_pool, y_mla, lay, 'gate_bwd')
    G['w_pool_o'] = _mm(yp, dy_pool, 'tn', BF, 'dw_pool_o')
    dyp = _mm(dy_pool, full['w_pool_o'], 'nt', F32, 'pool_out_bwd')
    dd, G['pool_scale'], d_pool_w = _pool_mix_bwd(dyp, e_pool, d_pool, full['pool_w'], pool_scale, 'pool_mix_bwd')
    G['pool_w'] = d_pool_w.astype(BF)
    du_pool = _pool_bwd(dd, 'pool_bwd')
    G['w_mla_o'] = _mm(O, dy_mla, 'tn', BF, 'dw_mla_o')
    dO = _mm(dy_mla, full['w_mla_o'], 'nt', BF, 'mla_out_bwd')
    dQ, dK, dV = _flash_bwd(Q, K, kv, O, dO, lse, H, 'flash_bwd')
    dq_raw, dkv, dkr = _qk_prep_bwd(dQ, dK, dV, cos, sin, H, 'qk_prep_bwd')
    d_w_q_pad = _mm(cqn, dq_raw, 'tn', BF, 'dw_q_b')
    G['w_q_b'] = d_w_q_pad.reshape(QL, H, HEAD_PAD)[:, :, :QK_DIM].reshape(QL, H * QK_DIM)
    dcqn = _mm(dq_raw, w_q_pad, 'nt', F32, 'mla_q_bwd')
    G['w_kv_b'] = _mm(ckvn, dkv, 'tn', BF, 'dw_kv_b')
    dckvn = _mm(dkv, full['w_kv_b'], 'nt', F32, 'mla_kv_bwd')
    d_lora, G['q_a_norm'], G['kv_a_norm'] = _lora_norms_bwd(dcqn, dckvn, z, q_a_norm, kv_a_norm, lay, 'lora_norms_bwd')
    n0 = POOL_WIDTH + QL + KVL
    dz = jnp.concatenate([du_pool, d_lora, jnp.zeros((Lp, lay['gp'] - n0), BF), d_gate, dkr], axis=1)
    d_w_in_al = _mm(n2, dz, 'tn', BF, 'dw_in')
    G['w_in'] = _w_in_logical(d_w_in_al, lay, D).reshape(D, N_CHIPS, in_cols // N_CHIPS).transpose(1, 0, 2)
    dn2 = _mm(dz, w_in_al, 'nt', F32, 'mix_in_bwd')
    dh1, G['norm_mix_pre'] = _pre_bwd(dh2, dn2, h1, norm_mix_pre, 'mix_norm_bwd')

    dh0, G['norm_ffn1_pre'], G['norm_ffn1_post'], G['ffn1_w_gu'], G['ffn1_w_down'] = _ffn_bwd(
        dh1, ffn1_saved, norm_ffn1_pre, norm_ffn1_post, full['ffn1_w_gu'], full['ffn1_w_down'], 'ffn1')
    grad_x = dh0[NM:L][None]

    slabs = [tuple(slab[n].shape) for n in names]
    bkinds = kinds[:-1]
    theirs = _pair_exchange([G[n] for n in names], bkinds, slabs)
    sums = [_pair_sum(G[n], t, k, s, c_arr, f'rs_pair_sum_{n}') for n, t, k, s in zip(names, theirs, bkinds, slabs)]
    landed = _chip_exchange(sums)
    halves = [_chip_sum(sm, ld, k, s, place_arr, f'rs_chip_sum_{n}')
              for n, sm, ld, k, s in zip(names, sums, landed, bkinds, slabs)]
    reduced = dict(zip(names, _final_exchange(halves, bkinds)))

    SW = max(D, POOL_WIDTH)

    def widen(a, fill=0.0):
        return jnp.pad(a, ((0, 0), (0, SW - a.shape[1])), constant_values=fill)

    rows = [widen(G[n]) for n in SMALL_VEC] + [widen(dh0[:NM]), widen(sq)]
    n_rows = len(SMALL_VEC) + NM + 1
    pad_rows = -(-n_rows // 8) * 8 - n_rows
    small = _all_reduce_small(jnp.concatenate(rows + [jnp.zeros((pad_rows, SW), F32)], axis=0))
    loss = (0.5 / D) * small[len(SMALL_VEC) + NM, 0]
    for i, n in enumerate(SMALL_VEC):
        reduced[n] = small[i:i + 1, :G[n].shape[1]]
    mw = meta_tokens.shape[1]
    reduced['meta_tokens'] = lax.dynamic_slice(small, (len(SMALL_VEC), chip * mw), (NM, mw))

    grads, deltas, new_m, new_v = {}, {}, {}, {}
    for n in names:
        w = slab[n]
        deltas[n], new_m[n], new_v[n] = [o[None] for o in _adam(w, reduced[n], M[n][0], V[n][0], f'adam_{n}')]
        grads[n] = reduced[n][None]
    n_vec = len(SMALL_VEC)
    vec_w = jnp.concatenate([widen(W[n]) for n in SMALL_VEC] + [jnp.zeros((16 - n_vec, SW), F32)], axis=0)
    vec_m = jnp.concatenate([widen(M[n]) for n in SMALL_VEC] + [jnp.zeros((16 - n_vec, SW), F32)], axis=0)
    vec_v = jnp.concatenate([widen(V[n], 1.0) for n in SMALL_VEC] + [jnp.ones((16 - n_vec, SW), F32)], axis=0)
    vec_g = jnp.concatenate([small[:n_vec], jnp.zeros((16 - n_vec, SW), F32)], axis=0)
    vd, vm, vv = _adam(vec_w, vec_g, vec_m, vec_v, 'adam_vectors')
    for i, n in enumerate(SMALL_VEC):
        wdt = W[n].shape[1]
        grads[n], deltas[n], new_m[n], new_v[n] = reduced[n], vd[i:i + 1, :wdt], vm[i:i + 1, :wdt], vv[i:i + 1, :wdt]
    grads['meta_tokens'] = reduced['meta_tokens']
    deltas['meta_tokens'], new_m['meta_tokens'], new_v['meta_tokens'] = _adam(
        meta_tokens, reduced['meta_tokens'], m_meta_tokens, v_meta_tokens, 'adam_meta')

    return (loss, grad_x, *[grads[n] for n in WEIGHTS], *[deltas[n] for n in WEIGHTS], *[new_m[n] for n in WEIGHTS],
            *[new_v[n] for n in WEIGHTS])
```

```python
import functools

import jax
import jax.numpy as jnp
import numpy as np
from jax import lax
from jax.experimental import pallas as pl
from jax.experimental.pallas import tpu as pltpu

F32 = jnp.float32
BF = jnp.bfloat16
MESH = pl.DeviceIdType.MESH

EPS = 1e-6
N_CHIPS = 4
POOL_WINDOWS = (2, 4, 8, 16)
POOL_GROUP = 256
POOL_WIDTH = POOL_GROUP * len(POOL_WINDOWS)
QK_NOPE = 128
QK_ROPE = 64
V_DIM = 128
QK_DIM = QK_NOPE + QK_ROPE
HEAD_PAD = 256
ROPE_THETA = 10000.0
SOFTMAX_SCALE = QK_DIM ** -0.5
ADAM_LR = 0.001
ADAM_B1 = 0.9
ADAM_B2 = 0.999
ADAM_EPS = 1e-08
ADAM_WD = 0.01
ADAM_STEP = 10
ROW_ALIGN = 256
VMEM_LIMIT = 48 * 1024 * 1024
NEG = -1e30
FLASH_CHUNK = 4

WEIGHTS = ['meta_tokens', 'norm_ffn1_pre', 'norm_ffn1_post', 'ffn1_w_gu', 'ffn1_w_down', 'norm_mix_pre',
           'norm_mix_post', 'w_in', 'pool_w', 'pool_scale', 'w_pool_o', 'q_a_norm', 'w_q_b', 'kv_a_norm', 'w_kv_b',
           'w_mla_o', 'w_out', 'norm_ffn2_pre', 'norm_ffn2_post', 'ffn2_w_gu', 'ffn2_w_down']
BIG = {'ffn1_w_gu': 'col', 'ffn1_w_down': 'row', 'w_in': 'lead', 'pool_w': 'pool', 'w_pool_o': 'col', 'w_q_b': 'col',
       'w_kv_b': 'col', 'w_mla_o': 'row', 'w_out': 'row', 'ffn2_w_gu': 'col', 'ffn2_w_down': 'row'}
SMALL_VEC = ['norm_ffn1_pre', 'norm_ffn1_post', 'norm_mix_pre', 'norm_mix_post', 'norm_ffn2_pre', 'norm_ffn2_post',
             'pool_scale', 'q_a_norm', 'kv_a_norm']


def _div_tile(n, target, align):
    best = None
    for t in range(align, min(n, target) + 1, align):
        if n % t == 0:
            best = t
    return best if best is not None else n


def _cp(sem=None):
    return pltpu.CompilerParams(dimension_semantics=sem, vmem_limit_bytes=VMEM_LIMIT)


def _full_shape(kind, slab):
    if kind == 'col':
        return (slab[0], slab[1] * N_CHIPS)
    if kind == 'row':
        return (slab[0] * N_CHIPS, slab[1])
    if kind == 'lead':
        return (N_CHIPS,) + tuple(slab)
    return (slab[0], slab[1] * N_CHIPS, slab[2])


def _half_shape(kind, slab):
    if kind == 'pool':
        return (slab[0], slab[1] // 2, slab[2])
    return (slab[0] // 2, slab[1])


def _half_of_slab(ref, kind, c):
    if kind == 'pool':
        n = ref.shape[1] // 2
        return ref.at[:, pl.ds(c * n, n), :]
    n = ref.shape[0] // 2
    return ref.at[pl.ds(c * n, n), :]


def _piece(ref, kind, k, c):
    if kind == 'col':
        r, w = ref.shape[0] // 2, ref.shape[1] // N_CHIPS
        return ref.at[pl.ds(c * r, r), pl.ds(k * w, w)]
    if kind == 'row':
        r = ref.shape[0] // (2 * N_CHIPS)
        return ref.at[pl.ds((2 * k + c) * r, r), :]
    if kind == 'lead':
        r = ref.shape[1] // 2
        return ref.at[k, pl.ds(c * r, r), :]
    r = ref.shape[1] // (2 * N_CHIPS)
    return ref.at[:, pl.ds((2 * k + c) * r, r), :]


def _slab(ref, kind, k):
    if kind == 'col':
        w = ref.shape[1] // N_CHIPS
        return ref.at[:, pl.ds(k * w, w)]
    if kind == 'row':
        r = ref.shape[0] // N_CHIPS
        return ref.at[pl.ds(k * r, r), :]
    if kind == 'lead':
        return ref.at[k]
    r = ref.shape[1] // N_CHIPS
    return ref.at[:, pl.ds(k * r, r), :]


def _place():
    x, y, c = lax.axis_index('x'), lax.axis_index('y'), lax.axis_index('c')
    return x, y, c


def _peer_chip(x, y, j):
    px = 1 - x if (j >> 1) else x
    py = 1 - y if (j & 1) else y
    return px, py


ANY = pl.BlockSpec(memory_space=pl.ANY)


def _all_gather(bufs, kinds):
    n = len(bufs)

    def body(*refs):
        outs = refs[n:2 * n]
        ssem, rsem, fssem, frsem = refs[2 * n:]
        x, y, c = _place()
        me = 2 * x + y
        sib = (x, y, 1 - c)
        sends = []
        for w in range(n):
            for j in (1, 2, 3):
                px, py = _peer_chip(x, y, j)
                mine = _piece(outs[w], kinds[w], me, c)
                sends.append(pltpu.make_async_remote_copy(
                    src_ref=mine, dst_ref=mine, send_sem=ssem.at[w, j - 1], recv_sem=rsem.at[w, j - 1],
                    device_id=(px, py, c), device_id_type=MESH))
        for cp in sends:
            cp.start()
        fwds = []
        for w in range(n):
            for j in (1, 2, 3):
                px, py = _peer_chip(x, y, j)
                got = _piece(outs[w], kinds[w], 2 * px + py, c)
                pltpu.make_async_remote_copy(src_ref=got, dst_ref=got, send_sem=ssem.at[w, j - 1],
                                             recv_sem=rsem.at[w, j - 1], device_id=(px, py, c),
                                             device_id_type=MESH).wait_recv()
                fwd = pltpu.make_async_remote_copy(src_ref=got, dst_ref=got, send_sem=fssem.at[w, j - 1],
                                                   recv_sem=frsem.at[w, j - 1], device_id=sib, device_id_type=MESH)
                fwd.start()
                fwds.append(fwd)
        for w in range(n):
            for j in (1, 2, 3):
                px, py = _peer_chip(x, y, j)
                other = _piece(outs[w], kinds[w], 2 * px + py, 1 - c)
                pltpu.make_async_remote_copy(src_ref=other, dst_ref=other, send_sem=fssem.at[w, j - 1],
                                             recv_sem=frsem.at[w, j - 1], device_id=sib,
                                             device_id_type=MESH).wait_recv()
        for cp in sends + fwds:
            cp.wait_send()

    return pl.pallas_call(
        body, name='all_gather_weights', out_shape=[jax.ShapeDtypeStruct(b.shape, b.dtype) for b in bufs],
        in_specs=[ANY] * n, out_specs=[ANY] * n, input_output_aliases={w: w for w in range(n)},
        scratch_shapes=[pltpu.SemaphoreType.DMA((n, 3)), pltpu.SemaphoreType.DMA((n, 3)),
                        pltpu.SemaphoreType.DMA((n, 3)), pltpu.SemaphoreType.DMA((n, 3))],
    )(*bufs)


def _remote(src, dst, ssem, rsem, k, dev):
    return pltpu.make_async_remote_copy(src_ref=src, dst_ref=dst, send_sem=ssem.at[k], recv_sem=rsem.at[k],
                                        device_id=dev, device_id_type=MESH)


def _phase_gather_ici(bufs, kinds):
    n = len(bufs)

    def build(ins, outs, ssem, rsem):
        x, y, c = _place()
        me = 2 * x + y
        ds = []
        for w in range(n):
            for j in (1, 2, 3):
                px, py = _peer_chip(x, y, j)
                mine = _piece(outs[w], kinds[w], me, c)
                got = _piece(outs[w], kinds[w], 2 * px + py, c)
                k = 3 * w + j - 1
                ds.append((_remote(mine, mine, ssem, rsem, k, (px, py, c)), _remote(got, got, ssem, rsem, k, (px, py, c))))
        return ds

    return dict(ins=list(bufs), outs=[jax.ShapeDtypeStruct(b.shape, b.dtype) for b in bufs],
                alias={w: w for w in range(n)}, nsem=3 * n, build=build)


def _phase_gather_d2d(bufs, kinds):
    n = len(bufs)

    def build(ins, outs, ssem, rsem):
        x, y, c = _place()
        sib = (x, y, 1 - c)
        ds = []
        for w in range(n):
            for j in (1, 2, 3):
                px, py = _peer_chip(x, y, j)
                have = _piece(outs[w], kinds[w], 2 * px + py, c)
                want = _piece(outs[w], kinds[w], 2 * px + py, 1 - c)
                k = 3 * w + j - 1
                ds.append((_remote(have, have, ssem, rsem, k, sib), _remote(want, want, ssem, rsem, k, sib)))
        return ds

    return dict(ins=list(bufs), outs=[jax.ShapeDtypeStruct(b.shape, b.dtype) for b in bufs],
                alias={w: w for w in range(n)}, nsem=3 * n, build=build)


def _phase_pair(grads, kinds, slabs):
    n = len(grads)

    def build(ins, outs, ssem, rsem):
        x, y, c = _place()
        sib = (x, y, 1 - c)
        ds = []
        for w in range(n):
            for k in range(N_CHIPS):
                cp = _remote(_piece(ins[w], kinds[w], k, 1 - c), outs[w].at[k], ssem, rsem, N_CHIPS * w + k, sib)
                ds.append((cp, cp))
        return ds

    return dict(ins=list(grads), alias={}, nsem=N_CHIPS * n, build=build,
                outs=[jax.ShapeDtypeStruct((N_CHIPS,) + _half_shape(k, s), g.dtype)
                      for g, k, s in zip(grads, kinds, slabs)])


def _phase_chip(sums):
    n = len(sums)

    def build(ins, outs, ssem, rsem):
        x, y, c = _place()
        ds = []
        for w in range(n):
            for j in (1, 2, 3):
                px, py = _peer_chip(x, y, j)
                cp = _remote(ins[w].at[2 * px + py], outs[w].at[j - 1], ssem, rsem, 3 * w + j - 1, (px, py, c))
                ds.append((cp, cp))
        return ds

    return dict(ins=list(sums), alias={}, nsem=3 * n, build=build,
                outs=[jax.ShapeDtypeStruct((3,) + s.shape[1:], s.dtype) for s in sums])


def _phase_final(slabs_half, kinds):
    n = len(slabs_half)

    def build(ins, outs, ssem, rsem):
        x, y, c = _place()
        sib = (x, y, 1 - c)
        ds = []
        for w in range(n):
            mine = _half_of_slab(outs[w], kinds[w], c)
            other = _half_of_slab(outs[w], kinds[w], 1 - c)
            ds.append((_remote(mine, mine, ssem, rsem, w, sib), _remote(other, other, ssem, rsem, w, sib)))
        return ds

    return dict(ins=list(slabs_half), outs=[jax.ShapeDtypeStruct(s.shape, s.dtype) for s in slabs_half],
                alias={w: w for w in range(n)}, nsem=n, build=build)


def _phase_operands(phases, n_main_in, n_main_out):
    ins, outs, alias, sems = [], [], {}, []
    for ph in phases:
        for i, o in ph['alias'].items():
            alias[n_main_in + len(ins) + i] = n_main_out + len(outs) + o
        ins += ph['ins']
        outs += ph['outs']
        sems += [pltpu.SemaphoreType.DMA((ph['nsem'],)), pltpu.SemaphoreType.DMA((ph['nsem'],))]
    return ins, outs, alias, sems


def _phase_copies(phases, in_refs, out_refs, sem_refs):
    ds, a, b = [], 0, 0
    for p, ph in enumerate(phases):
        ds += ph['build'](in_refs[a:a + len(ph['ins'])], out_refs[b:b + len(ph['outs'])], sem_refs[2 * p],
                          sem_refs[2 * p + 1])
        a += len(ph['ins'])
        b += len(ph['outs'])
    return ds


def _phase_start(phases, in_refs, out_refs, sem_refs):
    for send, _ in _phase_copies(phases, in_refs, out_refs, sem_refs):
        send.start()


def _phase_finish(phases, in_refs, out_refs, sem_refs):
    ds = _phase_copies(phases, in_refs, out_refs, sem_refs)
    for _, recv in ds:
        recv.wait_recv()
    for send, _ in ds:
        send.wait_send()


def _phase_results(phases, outs):
    res, b = [], 0
    for ph in phases:
        res.append(list(outs[b:b + len(ph['outs'])]))
        b += len(ph['outs'])
    return res


def _run_phases(phases, name):
    ins, out_shapes, alias, sems = _phase_operands(phases, 0, 0)
    n_in, n_out = len(ins), len(out_shapes)

    def body(*refs):
        in_refs, out_refs, sem_refs = refs[:n_in], refs[n_in:n_in + n_out], refs[n_in + n_out:]
        _phase_start(phases, in_refs, out_refs, sem_refs)
        _phase_finish(phases, in_refs, out_refs, sem_refs)

    outs = pl.pallas_call(body, name=name, out_shape=out_shapes, in_specs=[ANY] * n_in, out_specs=[ANY] * n_out,
                          input_output_aliases=alias, scratch_shapes=sems)(*ins)
    return _phase_results(phases, outs)


def _cast_place(shard, kind, place_arr, dtype, name):
    full = _full_shape(kind, shard.shape)
    if kind == 'pool':
        g, r, w = shard.shape
        grid = (g,)
        src = pl.BlockSpec((None, r, w), lambda i, s: (i, 0, 0))
        dst = pl.BlockSpec((None, r, w), lambda i, s: (i, s[1], 0))
    else:
        r, w = shard.shape
        tr = _div_tile(r, max(16, (1 << 19) // w), 16)
        nrb = r // tr
        grid = (nrb,)
        src = pl.BlockSpec((tr, w), lambda i, s: (i, 0))
        if kind == 'col':
            dst = pl.BlockSpec((tr, w), lambda i, s: (i, s[1]))
        elif kind == 'row':
            dst = pl.BlockSpec((tr, w), lambda i, s: (s[1] * nrb + i, 0))
        else:
            dst = pl.BlockSpec((None, tr, w), lambda i, s: (s[1], i, 0))

    def body(s_ref, x_ref, o_ref):
        o_ref[...] = x_ref[...].astype(o_ref.dtype)

    return pl.pallas_call(
        body, name=name,
        grid_spec=pltpu.PrefetchScalarGridSpec(num_scalar_prefetch=1, grid=grid, in_specs=[src], out_specs=dst),
        out_shape=jax.ShapeDtypeStruct(full, dtype), compiler_params=_cp(('parallel',)),
    )(place_arr, shard)


def _half_blocks(kind, slab):
    hs = _half_shape(kind, slab)
    if kind == 'pool':
        return (hs[0],), hs[1], 1
    tr = _div_tile(hs[0], max(16, (1 << 19) // hs[1]), 16)
    return (hs[0] // tr,), tr, hs[0] // tr


def _pair_sum(grad, theirs, kind, slab, c_arr, name):
    hs = _half_shape(kind, slab)
    (n_i,), tr, nrb = _half_blocks(kind, slab)
    if kind == 'pool':
        own = pl.BlockSpec((None, hs[1], hs[2]), lambda k, i, s: (i, 2 * k + s[0], 0))
        stk = pl.BlockSpec((None, None, hs[1], hs[2]), lambda k, i, s: (k, i, 0, 0))
    else:
        if kind == 'col':
            own = pl.BlockSpec((tr, hs[1]), lambda k, i, s: (s[0] * nrb + i, k))
        elif kind == 'row':
            own = pl.BlockSpec((tr, hs[1]), lambda k, i, s: ((2 * k + s[0]) * nrb + i, 0))
        else:
            own = pl.BlockSpec((None, tr, hs[1]), lambda k, i, s: (k, s[0] * nrb + i, 0))
        stk = pl.BlockSpec((None, tr, hs[1]), lambda k, i, s: (k, i, 0))

    def body(s_ref, a_ref, b_ref, o_ref):
        o_ref[...] = (a_ref[...].astype(F32) + b_ref[...].astype(F32)).astype(o_ref.dtype)

    return pl.pallas_call(
        body, name=name,
        grid_spec=pltpu.PrefetchScalarGridSpec(num_scalar_prefetch=1, grid=(N_CHIPS, n_i), in_specs=[own, stk],
                                               out_specs=stk),
        out_shape=jax.ShapeDtypeStruct((N_CHIPS,) + hs, BF), compiler_params=_cp(('parallel', 'parallel')),
    )(c_arr, grad, theirs)


def _chip_sum(sums, landed, kind, slab, place_arr, name):
    hs = _half_shape(kind, slab)
    (n_i,), tr, nrb = _half_blocks(kind, slab)
    if kind == 'pool':
        blk = (None, None, hs[1], hs[2])
        mine = pl.BlockSpec(blk, lambda i, s: (s[1], i, 0, 0))
        land = [pl.BlockSpec(blk, lambda i, s, j=j: (j, i, 0, 0)) for j in range(3)]
        out = pl.BlockSpec((None, hs[1], hs[2]), lambda i, s: (i, s[0], 0))
    else:
        blk = (None, tr, hs[1])
        mine = pl.BlockSpec(blk, lambda i, s: (s[1], i, 0))
        land = [pl.BlockSpec(blk, lambda i, s, j=j: (j, i, 0)) for j in range(3)]
        out = pl.BlockSpec((tr, hs[1]), lambda i, s: (s[0] * nrb + i, 0))

    def body(s_ref, a_ref, b_ref, c_ref, d_ref, o_ref):
        o_ref[...] = ((a_ref[...].astype(F32) + b_ref[...].astype(F32)) + c_ref[...].astype(F32)) + d_ref[...].astype(F32)

    return pl.pallas_call(
        body, name=name,
        grid_spec=pltpu.PrefetchScalarGridSpec(num_scalar_prefetch=1, grid=(n_i,), in_specs=[mine] + land, out_specs=out),
        out_shape=jax.ShapeDtypeStruct(tuple(slab), F32), compiler_params=_cp(('parallel',)),
    )(place_arr, sums, landed, landed, landed)


def _all_reduce_small(buf):
    rows, cols = buf.shape

    def body(in_ref, out_ref, land, ssem, rsem):
        x, y, c = _place()
        me = 4 * x + 2 * y + c
        land[me] = in_ref[...]
        started = []
        for j in range(1, 8):
            px = 1 - x if (j >> 2) & 1 else x
            py = 1 - y if (j >> 1) & 1 else y
            pc = 1 - c if j & 1 else c
            cp = pltpu.make_async_remote_copy(src_ref=in_ref, dst_ref=land.at[me], send_sem=ssem.at[j - 1],
                                              recv_sem=rsem.at[j - 1], device_id=(px, py, pc), device_id_type=MESH)
            cp.start()
            started.append(cp)
        for j in range(1, 8):
            px = 1 - x if (j >> 2) & 1 else x
            py = 1 - y if (j >> 1) & 1 else y
            pc = 1 - c if j & 1 else c
            slot = land.at[4 * px + 2 * py + pc]
            pltpu.make_async_remote_copy(src_ref=slot, dst_ref=slot, send_sem=ssem.at[j - 1], recv_sem=rsem.at[j - 1],
                                         device_id=(px, py, pc), device_id_type=MESH).wait_recv()
        for cp in started:
            cp.wait_send()
        acc = land[0]
        for d in range(1, 8):
            acc = acc + land[d]
        out_ref[...] = acc

    return pl.pallas_call(
        body, name='all_reduce_small', out_shape=jax.ShapeDtypeStruct((rows, cols), F32),
        in_specs=[pl.BlockSpec(memory_space=pltpu.VMEM)], out_specs=pl.BlockSpec(memory_space=pltpu.VMEM),
        scratch_shapes=[pltpu.VMEM((8, rows, cols), F32), pltpu.SemaphoreType.DMA((7,)), pltpu.SemaphoreType.DMA((7,))],
    )(buf)


def _elementwise(fn, ins, lead_index, out_shape, out_dtypes, name):
    nd = len(out_shape)
    r, cdim = out_shape[-2], out_shape[-1]
    tr = _div_tile(r, max(16, (1 << 19) // cdim), 16)
    grid = tuple(out_shape[:-2]) + (r // tr,)
    block = (None,) * (nd - 2) + (tr, cdim)

    def spec(lead):
        if lead is None:
            return pl.BlockSpec(block, lambda *g: tuple(g) + (0,))
        return pl.BlockSpec((None,) + block, lambda *g, lead=lead: (lead,) + tuple(g) + (0,))

    n_in = len(ins)

    def body(*refs):
        res = fn(*[r_[...] for r_ in refs[:n_in]])
        for o_ref, v in zip(refs[n_in:], res):
            o_ref[...] = v.astype(o_ref.dtype)

    return pl.pallas_call(
        body, name=name, grid=grid, in_specs=[spec(l) for l in lead_index],
        out_specs=[spec(None) for _ in out_dtypes],
        out_shape=[jax.ShapeDtypeStruct(tuple(out_shape), dt) for dt in out_dtypes],
        compiler_params=_cp(('parallel',) * len(grid)),
    )(*ins)


def _adam_fn(w, g, m, v):
    m = ADAM_B1 * m + (1.0 - ADAM_B1) * g
    v = ADAM_B2 * v + (1.0 - ADAM_B2) * (g * g)
    m_hat = m / (1.0 - ADAM_B1 ** ADAM_STEP)
    v_hat = v / (1.0 - ADAM_B2 ** ADAM_STEP)
    delta = -ADAM_LR * (m_hat / (jnp.sqrt(v_hat) + ADAM_EPS) + ADAM_WD * w)
    return delta, m, v


def _adam(w, g, m, v, name):
    return _elementwise(_adam_fn, [w, g, m, v], [None] * 4, w.shape, [F32] * 3, name)


_DIMS = {'nn': (((1,), (0,)), ((), ())), 'nt': (((1,), (1,)), ((), ())), 'tn': (((0,), (0,)), ((), ()))}


def _mm(a, b, mode, out_dtype, name, phases=()):
    if mode == 'nn':
        (M, K), N = a.shape, b.shape[1]
    elif mode == 'nt':
        (M, K), N = a.shape, b.shape[0]
    else:
        (K, M), N = a.shape, b.shape[1]
    if mode == 'tn':
        tm, tn, tk = _div_tile(M, 512, 128), _div_tile(N, 1024, 128), _div_tile(K, 2176, 16)
    else:
        tm, tn, tk = _div_tile(M, 1088, 16), _div_tile(N, 1024, 128), _div_tile(K, 2048, 128)
    nk = K // tk
    a_spec = {'nn': pl.BlockSpec((tm, tk), lambda i, j, k: (i, k)), 'nt': pl.BlockSpec((tm, tk), lambda i, j, k: (i, k)),
              'tn': pl.BlockSpec((tk, tm), lambda i, j, k: (k, i))}[mode]
    b_spec = {'nn': pl.BlockSpec((tk, tn), lambda i, j, k: (k, j)), 'nt': pl.BlockSpec((tn, tk), lambda i, j, k: (j, k)),
              'tn': pl.BlockSpec((tk, tn), lambda i, j, k: (k, j))}[mode]
    dims = _DIMS[mode]
    gm, gn = M // tm, N // tn
    extra_in, extra_out, alias, sems = _phase_operands(phases, 2, 1)
    n_ei, n_eo = len(extra_in), len(extra_out)

    def body(*refs):
        a_ref, b_ref, ein = refs[0], refs[1], refs[2:2 + n_ei]
        o_ref, eout = refs[2 + n_ei], refs[3 + n_ei:3 + n_ei + n_eo]
        acc_ref, sem_refs = refs[3 + n_ei + n_eo], refs[4 + n_ei + n_eo:]
        i, j, k = pl.program_id(0), pl.program_id(1), pl.program_id(2)
        if phases:
            @pl.when((i == 0) & (j == 0) & (k == 0))
            def _():
                _phase_start(phases, ein, eout, sem_refs)

        part = lax.dot_general(a_ref[...], b_ref[...], dims, preferred_element_type=F32)
        if nk == 1:
            o_ref[...] = part.astype(o_ref.dtype)
        else:
            @pl.when(k == 0)
            def _():
                acc_ref[...] = part

            @pl.when(k > 0)
            def _():
                acc_ref[...] += part

            @pl.when(k == nk - 1)
            def _():
                o_ref[...] = acc_ref[...].astype(o_ref.dtype)

        if phases:
            @pl.when((i == gm - 1) & (j == gn - 1) & (k == nk - 1))
            def _():
                _phase_finish(phases, ein, eout, sem_refs)

    outs = pl.pallas_call(
        body, name=name, grid=(gm, gn, nk), in_specs=[a_spec, b_spec] + [ANY] * n_ei,
        out_specs=[pl.BlockSpec((tm, tn), lambda i, j, k: (i, j))] + [ANY] * n_eo,
        out_shape=[jax.ShapeDtypeStruct((M, N), out_dtype)] + extra_out, input_output_aliases=alias,
        scratch_shapes=[pltpu.VMEM((tm, tn) if nk > 1 else (8, 128), F32)] + sems,
        compiler_params=_cp(('arbitrary',) * 3 if phases else ('parallel', 'parallel', 'arbitrary')),
    )(a, b, *extra_in)
    if phases:
        return outs[0], _phase_results(phases, outs[1:])
    return outs[0]


def _rb(tm, w, col=0):
    return pl.BlockSpec((tm, w), lambda i, col=col: (i, col))


def _fixed(shape):
    return pl.BlockSpec(shape, lambda i: (0,) * len(shape))


def _rms(x):
    return lax.rsqrt(jnp.mean(x * x, axis=-1, keepdims=True) + EPS)


def _norm_fwd(x, gain, name, width=None, col=0):
    Lp = x.shape[0]
    width = width or x.shape[1]
    tm = _div_tile(Lp, 256, 16)

    def body(x_ref, g_ref, o_ref):
        v = x_ref[...]
        o_ref[...] = (v * _rms(v) * g_ref[...]).astype(o_ref.dtype)

    return pl.pallas_call(
        body, name=name, grid=(Lp // tm,), in_specs=[_rb(tm, width, col), _fixed((1, width))], out_specs=_rb(tm, width),
        out_shape=jax.ShapeDtypeStruct((Lp, width), BF), compiler_params=_cp(('parallel',)),
    )(x, gain)


def _post_residual(h, f, gain, scale, name):
    Lp, D = h.shape
    tm = _div_tile(Lp, 256, 16)

    def body(h_ref, f_ref, g_ref, o_ref):
        v = f_ref[...]
        o_ref[...] = h_ref[...] + scale * (v * _rms(v) * g_ref[...])

    return pl.pallas_call(
        body, name=name, grid=(Lp // tm,), in_specs=[_rb(tm, D), _rb(tm, D), _fixed((1, D))], out_specs=_rb(tm, D),
        out_shape=jax.ShapeDtypeStruct((Lp, D), F32), compiler_params=_cp(('parallel',)),
    )(h, f, gain)


def _post_bwd(dh, f, gain, scale, name):
    Lp, D = dh.shape
    tm = _div_tile(Lp, 256, 16)

    def body(dh_ref, f_ref, g_ref, df_ref, dg_ref):
        v = f_ref[...]
        r = _rms(v)
        dy = scale * dh_ref[...]
        w = dy * g_ref[...]
        df_ref[...] = (r * w - v * (r * r * r) * jnp.mean(w * v, axis=-1, keepdims=True)).astype(df_ref.dtype)
        part = jnp.sum(dy * v * r, axis=0, keepdims=True)

        @pl.when(pl.program_id(0) == 0)
        def _():
            dg_ref[...] = part

        @pl.when(pl.program_id(0) > 0)
        def _():
            dg_ref[...] += part

    return pl.pallas_call(
        body, name=name, grid=(Lp // tm,), in_specs=[_rb(tm, D), _rb(tm, D), _fixed((1, D))],
        out_specs=[_rb(tm, D), _fixed((1, D))],
        out_shape=[jax.ShapeDtypeStruct((Lp, D), BF), jax.ShapeDtypeStruct((1, D), F32)],
        compiler_params=_cp(('arbitrary',)),
    )(dh, f, gain)


def _pre_bwd(dres, dn, h, gain, name):
    Lp, D = h.shape
    tm = _div_tile(Lp, 256, 16)

    def body(dres_ref, dn_ref, h_ref, g_ref, dh_ref, dg_ref):
        v = h_ref[...]
        r = _rms(v)
        dy = dn_ref[...]
        w = dy * g_ref[...]
        dh_ref[...] = dres_ref[...] + r * w - v * (r * r * r) * jnp.mean(w * v, axis=-1, keepdims=True)
        part = jnp.sum(dy * v * r, axis=0, keepdims=True)

        @pl.when(pl.program_id(0) == 0)
        def _():
            dg_ref[...] = part

        @pl.when(pl.program_id(0) > 0)
        def _():
            dg_ref[...] += part

    return pl.pallas_call(
        body, name=name, grid=(Lp // tm,), in_specs=[_rb(tm, D), _rb(tm, D), _rb(tm, D), _fixed((1, D))],
        out_specs=[_rb(tm, D), _fixed((1, D))],
        out_shape=[jax.ShapeDtypeStruct((Lp, D), F32), jax.ShapeDtypeStruct((1, D), F32)],
        compiler_params=_cp(('arbitrary',)),
    )(dres, dn, h, gain)


def _swiglu_fwd(gu, name):
    Lp, F2 = gu.shape
    F = F2 // 2
    tm = _div_tile(Lp, 256, 16)

    def body(gu_ref, a_ref):
        g = gu_ref[:, :F].astype(F32)
        u = gu_ref[:, F:].astype(F32)
        a_ref[...] = (g * jax.nn.sigmoid(g) * u).astype(a_ref.dtype)

    return pl.pallas_call(
        body, name=name, grid=(Lp // tm,), in_specs=[_rb(tm, F2)], out_specs=_rb(tm, F),
        out_shape=jax.ShapeDtypeStruct((Lp, F), BF), compiler_params=_cp(('parallel',)),
    )(gu)


def _swiglu_bwd(da, gu, name):
    Lp, F2 = gu.shape
    F = F2 // 2
    tm = _div_tile(Lp, 256, 16)

    def body(da_ref, gu_ref, o_ref):
        g = gu_ref[:, :F].astype(F32)
        u = gu_ref[:, F:].astype(F32)
        da_ = da_ref[...].astype(F32)
        s = jax.nn.sigmoid(g)
        o_ref[:, :F] = (da_ * u * (s * (1.0 + g * (1.0 - s)))).astype(o_ref.dtype)
        o_ref[:, F:] = (da_ * (g * s)).astype(o_ref.dtype)

    return pl.pallas_call(
        body, name=name, grid=(Lp // tm,), in_specs=[_rb(tm, F), _rb(tm, F2)], out_specs=_rb(tm, F2),
        out_shape=jax.ShapeDtypeStruct((Lp, F2), BF), compiler_params=_cp(('parallel',)),
    )(da, gu)


def _loss_bwd(h, tgt, n_meta, n_real, name):
    Lp, D = h.shape
    tm = _div_tile(Lp, 256, 16)

    def body(h_ref, t_ref, d_ref, l_ref):
        row = lax.broadcasted_iota(jnp.int32, (tm, 1), 0) + pl.program_id(0) * tm
        ok = (row >= n_meta) & (row < n_meta + n_real)
        err = jnp.where(ok, h_ref[...] - t_ref[...], 0.0)
        d_ref[...] = err / D
        part = jnp.full((1, 128), jnp.sum(err * err), F32)

        @pl.when(pl.program_id(0) == 0)
        def _():
            l_ref[...] = part

        @pl.when(pl.program_id(0) > 0)
        def _():
            l_ref[...] += part

    return pl.pallas_call(
        body, name=name, grid=(Lp // tm,), in_specs=[_rb(tm, D), _rb(tm, D)], out_specs=[_rb(tm, D), _fixed((1, 128))],
        out_shape=[jax.ShapeDtypeStruct((Lp, D), F32), jax.ShapeDtypeStruct((1, 128), F32)],
        compiler_params=_cp(('arbitrary',)),
    )(h, tgt)


def _split_bf16(v):
    hi = v.astype(BF)
    return hi, (v - hi.astype(F32)).astype(BF)


def _pool_fwd(z, name):
    Lp = z.shape[0]
    T = _div_tile(Lp, 256, 16)
    G = len(POOL_WINDOWS)

    def body(cur_ref, prev_ref, d_ref):
        i, g = pl.program_id(0), pl.program_id(1)
        w = jnp.left_shift(2, g)
        rr = lax.broadcasted_iota(jnp.int32, (T, T), 0)
        cc = lax.broadcasted_iota(jnp.int32, (T, T), 1)
        b_cur = jnp.where((cc <= rr) & (cc > rr - w), 1.0, 0.0).astype(BF)
        w_prev = jnp.where(i > 0, w, 0)
        b_prev = jnp.where(cc - T > rr - w_prev, 1.0, 0.0).astype(BF)
        u = cur_ref[...]
        s = jnp.zeros((T, POOL_GROUP), F32)
        for part in _split_bf16(u):
            s += jnp.dot(b_cur, part, preferred_element_type=F32)
        for part in _split_bf16(prev_ref[...]):
            s += jnp.dot(b_prev, part, preferred_element_type=F32)
        t = lax.broadcasted_iota(jnp.int32, (T, 1), 0) + i * T
        cnt = jnp.minimum(w, t + 1).astype(F32)
        d_ref[...] = (s / cnt - u).astype(d_ref.dtype)

    return pl.pallas_call(
        body, name=name, grid=(Lp // T, G),
        in_specs=[pl.BlockSpec((T, POOL_GROUP), lambda i, g: (i, g)),
                  pl.BlockSpec((T, POOL_GROUP), lambda i, g: (jnp.maximum(i - 1, 0), g))],
        out_specs=pl.BlockSpec((T, POOL_GROUP), lambda i, g: (i, g)),
        out_shape=jax.ShapeDtypeStruct((Lp, POOL_WIDTH), BF), compiler_params=_cp(('parallel', 'parallel')),
    )(z, z)


def _pool_bwd(dd, name):
    Lp = dd.shape[0]
    T = _div_tile(Lp, 256, 16)
    G = len(POOL_WINDOWS)
    n_t = Lp // T

    def body(cur_ref, next_ref, o_ref):
        i, g = pl.program_id(0), pl.program_id(1)
        w = jnp.left_shift(2, g)
        rr = lax.broadcasted_iota(jnp.int32, (T, T), 0)
        cc = lax.broadcasted_iota(jnp.int32, (T, T), 1)
        b_cur = jnp.where((cc >= rr) & (cc < rr + w), 1.0, 0.0).astype(BF)
        w_next = jnp.where(i < n_t - 1, w, 0)
        b_next = jnp.where(cc + T < rr + w_next, 1.0, 0.0).astype(BF)
        t = lax.broadcasted_iota(jnp.int32, (T, 1), 0) + i * T
        cur = cur_ref[...]
        e_cur = cur / jnp.minimum(w, t + 1).astype(F32)
        e_next = next_ref[...] / jnp.minimum(w, t + T + 1).astype(F32)
        s = jnp.zeros((T, POOL_GROUP), F32)
        for part in _split_bf16(e_cur):
            s += jnp.dot(b_cur, part, preferred_element_type=F32)
        for part in _split_bf16(e_next):
            s += jnp.dot(b_next, part, preferred_element_type=F32)
        o_ref[...] = (s - cur).astype(o_ref.dtype)

    return pl.pallas_call(
        body, name=name, grid=(n_t, G),
        in_specs=[pl.BlockSpec((T, POOL_GROUP), lambda i, g: (i, g)),
                  pl.BlockSpec((T, POOL_GROUP), lambda i, g: (jnp.minimum(i + 1, n_t - 1), g))],
        out_specs=pl.BlockSpec((T, POOL_GROUP), lambda i, g: (i, g)),
        out_shape=jax.ShapeDtypeStruct((Lp, POOL_WIDTH), BF), compiler_params=_cp(('parallel', 'parallel')),
    )(dd, dd)


def _pool_mix_fwd(d, pool_w, scale, name):
    Lp = d.shape[0]
    G = len(POOL_WINDOWS)
    tm = _div_tile(Lp, 1088, 16)

    def body(d_ref, w_ref, s_ref, e_ref, y_ref):
        e = jnp.dot(d_ref[...], w_ref[...], preferred_element_type=F32)
        e_ref[...] = e.astype(e_ref.dtype)
        y_ref[...] = (e * s_ref[...]).astype(y_ref.dtype)

    blk = pl.BlockSpec((tm, POOL_GROUP), lambda g, i: (i, g))
    return pl.pallas_call(
        body, name=name, grid=(G, Lp // tm),
        in_specs=[blk, pl.BlockSpec((None, POOL_GROUP, POOL_GROUP), lambda g, i: (g, 0, 0)),
                  pl.BlockSpec((1, POOL_GROUP), lambda g, i: (0, g))],
        out_specs=[blk, blk], out_shape=[jax.ShapeDtypeStruct((Lp, POOL_WIDTH), BF)] * 2,
        compiler_params=_cp(('parallel', 'parallel')),
    )(d, pool_w, scale)


def _pool_mix_bwd(dyp, e, d, pool_w, scale, name):
    Lp = d.shape[0]
    G = len(POOL_WINDOWS)
    tm = _div_tile(Lp, 1088, 16)

    def body(dy_ref, e_ref, d_ref, w_ref, s_ref, dd_ref, ds_ref, dw_ref):
        i = pl.program_id(1)
        dy = dy_ref[...]
        de = (dy * s_ref[...]).astype(BF)
        dd_ref[...] = lax.dot_general(de, w_ref[...], _DIMS['nt'], preferred_element_type=F32)
        ds_part = jnp.sum(dy * e_ref[...].astype(F32), axis=0, keepdims=True)
        dw_part = lax.dot_general(d_ref[...], de, _DIMS['tn'], preferred_element_type=F32)

        @pl.when(i == 0)
        def _():
            ds_ref[...] = ds_part
            dw_ref[...] = dw_part

        @pl.when(i > 0)
        def _():
            ds_ref[...] += ds_part
            dw_ref[...] += dw_part

    blk = pl.BlockSpec((tm, POOL_GROUP), lambda g, i: (i, g))
    wblk = pl.BlockSpec((None, POOL_GROUP, POOL_GROUP), lambda g, i: (g, 0, 0))
    sblk = pl.BlockSpec((1, POOL_GROUP), lambda g, i: (0, g))
    return pl.pallas_call(
        body, name=name, grid=(G, Lp // tm), in_specs=[blk, blk, blk, wblk, sblk], out_specs=[blk, sblk, wblk],
        out_shape=[jax.ShapeDtypeStruct((Lp, POOL_WIDTH), F32), jax.ShapeDtypeStruct((1, POOL_WIDTH), F32),
                   jax.ShapeDtypeStruct((G, POOL_GROUP, POOL_GROUP), F32)],
        compiler_params=_cp(('parallel', 'arbitrary')),
    )(dyp, e, d, pool_w, scale)


def _rot_half(t):
    lane = lax.broadcasted_iota(jnp.int32, t.shape, 1)
    half = QK_ROPE // 2
    return jnp.where(lane < half, -pltpu.roll(t, 128 - half, 1), pltpu.roll(t, half, 1))


def _lora_norms(z, q_gain, kv_gain, lay, name):
    Lp = z.shape[0]
    QL, KVL = lay['QL'], lay['KVL']
    tm = _div_tile(Lp, 256, 16)

    def body(q_ref, kv_ref, qg_ref, kg_ref, qo_ref, ko_ref):
        a = q_ref[...]
        qo_ref[...] = (a * _rms(a) * qg_ref[...]).astype(BF)
        b = kv_ref[...]
        ko_ref[...] = (b * _rms(b) * kg_ref[...]).astype(BF)

    return pl.pallas_call(
        body, name=name, grid=(Lp // tm,),
        in_specs=[_rb(tm, QL, lay['cq'] // QL), _rb(tm, KVL, lay['ckv'] // KVL), _fixed((1, QL)), _fixed((1, KVL))],
        out_specs=[_rb(tm, QL), _rb(tm, KVL)],
        out_shape=[jax.ShapeDtypeStruct((Lp, QL), BF), jax.ShapeDtypeStruct((Lp, KVL), BF)],
        compiler_params=_cp(('parallel',)),
    )(z, z, q_gain, kv_gain)


def _lora_norms_bwd(dqn, dkn, z, q_gain, kv_gain, lay, name):
    Lp = z.shape[0]
    QL, KVL = lay['QL'], lay['KVL']
    tm = _div_tile(Lp, 256, 16)

    def one(dy, v, gain):
        r = _rms(v)
        w = dy * gain
        return r * w - v * (r * r * r) * jnp.mean(w * v, axis=-1, keepdims=True), jnp.sum(dy * v * r, axis=0, keepdims=True)

    def body(dq_ref, dk_ref, q_ref, kv_ref, qg_ref, kg_ref, o_ref, dqg_ref, dkg_ref):
        da, ga = one(dq_ref[...], q_ref[...], qg_ref[...])
        db, gb = one(dk_ref[...], kv_ref[...], kg_ref[...])
        o_ref[:, :QL] = da.astype(BF)
        o_ref[:, QL:] = db.astype(BF)

        @pl.when(pl.program_id(0) == 0)
        def _():
            dqg_ref[...] = ga
            dkg_ref[...] = gb

        @pl.when(pl.program_id(0) > 0)
        def _():
            dqg_ref[...] += ga
            dkg_ref[...] += gb

    return pl.pallas_call(
        body, name=name, grid=(Lp // tm,),
        in_specs=[_rb(tm, QL), _rb(tm, KVL), _rb(tm, QL, lay['cq'] // QL), _rb(tm, KVL, lay['ckv'] // KVL),
                  _fixed((1, QL)), _fixed((1, KVL))],
        out_specs=[_rb(tm, QL + KVL), _fixed((1, QL)), _fixed((1, KVL))],
        out_shape=[jax.ShapeDtypeStruct((Lp, QL + KVL), BF), jax.ShapeDtypeStruct((1, QL), F32),
                   jax.ShapeDtypeStruct((1, KVL), F32)],
        compiler_params=_cp(('arbitrary',)),
    )(dqn, dkn, z, z, q_gain, kv_gain)


def _qk_prep(q_raw, kv, z, cos, sin, lay, H, name):
    Lp = q_raw.shape[0]
    W = H * HEAD_PAD
    tm = _div_tile(Lp, 256, 16)

    def body(q_ref, kv_ref, kr_ref, c_ref, s_ref, qo_ref, ko_ref):
        c, s = c_ref[...], s_ref[...]

        def rope(t):
            return t * c + _rot_half(t) * s

        kpe = rope(kr_ref[...]).astype(BF)
        for h in range(H):
            b = h * HEAD_PAD
            qo_ref[:, b:b + 128] = (q_ref[:, b:b + 128] * SOFTMAX_SCALE).astype(BF)
            qo_ref[:, b + 128:b + 256] = (rope(q_ref[:, b + 128:b + 256]) * SOFTMAX_SCALE).astype(BF)
            ko_ref[:, b:b + 128] = kv_ref[:, b:b + 128]
            ko_ref[:, b + 128:b + 256] = kpe

    return pl.pallas_call(
        body, name=name, grid=(Lp // tm,),
        in_specs=[_rb(tm, W), _rb(tm, W), _rb(tm, 128, lay['kr'] // 128), _rb(tm, 128), _rb(tm, 128)],
        out_specs=[_rb(tm, W), _rb(tm, W)], out_shape=[jax.ShapeDtypeStruct((Lp, W), BF)] * 2,
        compiler_params=_cp(('parallel',)),
    )(q_raw, kv, z, cos, sin)


def _qk_prep_bwd(dQ, dK, dV, cos, sin, H, name):
    Lp = dQ.shape[0]
    W = H * HEAD_PAD
    tm = _div_tile(Lp, 256, 16)

    def body(dq_ref, dk_ref, dv_ref, c_ref, s_ref, qo_ref, kvo_ref, kro_ref):
        c, s = c_ref[...], s_ref[...]

        def unrope(t):
            return t * c - _rot_half(t * s)

        acc = jnp.zeros((tm, 128), F32)
        for h in range(H):
            b = h * HEAD_PAD
            qo_ref[:, b:b + 128] = (dq_ref[:, b:b + 128] * SOFTMAX_SCALE).astype(BF)
            qo_ref[:, b + 128:b + 256] = (unrope(dq_ref[:, b + 128:b + 256]) * SOFTMAX_SCALE).astype(BF)
            kvo_ref[:, b:b + 128] = dk_ref[:, b:b + 128].astype(BF)
            kvo_ref[:, b + 128:b + 256] = dv_ref[:, h * V_DIM:(h + 1) * V_DIM]
            acc += dk_ref[:, b + 128:b + 256]
        kro_ref[...] = unrope(acc).astype(BF)

    return pl.pallas_call(
        body, name=name, grid=(Lp // tm,),
        in_specs=[_rb(tm, W), _rb(tm, W), _rb(tm, H * V_DIM), _rb(tm, 128), _rb(tm, 128)],
        out_specs=[_rb(tm, W), _rb(tm, W), _rb(tm, 128)],
        out_shape=[jax.ShapeDtypeStruct((Lp, W), BF), jax.ShapeDtypeStruct((Lp, W), BF),
                   jax.ShapeDtypeStruct((Lp, 128), BF)],
        compiler_params=_cp(('parallel',)),
    )(dQ, dK, dV, cos, sin)


def _call_carrying(core, name, grid, in_specs, out_specs, out_shape, scratch, args, phases):
    n_in, n_out, n_scr = len(in_specs), len(out_specs), len(scratch)
    extra_in, extra_out, alias, sems = _phase_operands(phases, n_in, n_out)
    n_ei, n_eo = len(extra_in), len(extra_out)

    def body(*refs):
        ins, ein = refs[:n_in], refs[n_in:n_in + n_ei]
        outs = refs[n_in + n_ei:n_in + n_ei + n_out]
        eout = refs[n_in + n_ei + n_out:n_in + n_ei + n_out + n_eo]
        rest = refs[n_in + n_ei + n_out + n_eo:]
        scr, sem_refs = rest[:n_scr], rest[n_scr:]
        a, b = pl.program_id(0), pl.program_id(1)
        if phases:
            @pl.when((a == 0) & (b == 0))
            def _():
                _phase_start(phases, ein, eout, sem_refs)

        core(*ins, *outs, *scr)
        if phases:
            @pl.when((a == grid[0] - 1) & (b == grid[1] - 1))
            def _():
                _phase_finish(phases, ein, eout, sem_refs)

    outs = pl.pallas_call(
        body, name=name, grid=grid, in_specs=list(in_specs) + [ANY] * n_ei, out_specs=list(out_specs) + [ANY] * n_eo,
        out_shape=list(out_shape) + extra_out, input_output_aliases=alias, scratch_shapes=list(scratch) + sems,
        compiler_params=_cp(('arbitrary', 'arbitrary')),
    )(*args, *extra_in)
    return list(outs[:n_out]), _phase_results(phases, outs[n_out:])


def _flash_fwd(Q, K, kv, H, name, phases=()):
    Lp = Q.shape[0]
    T = _div_tile(Lp, 256, 16)

    CH = FLASH_CHUNK * T

    def body(q_ref, k_ref, v_ref, o_ref, lse_ref, m_s, l_s, acc_s):
        i = pl.program_id(1)
        q = q_ref[...]
        m_s[...] = jnp.full((T, 1), NEG, F32)
        l_s[...] = jnp.zeros((T, 1), F32)
        acc_s[...] = jnp.zeros((T, V_DIM), F32)

        def step(start, width, masked):
            rows = pl.ds(start, width)
            s = lax.dot_general(q, k_ref[rows, :], _DIMS['nt'], preferred_element_type=F32)
            if masked:
                rr = lax.broadcasted_iota(jnp.int32, (T, width), 0) + i * T
                cc = lax.broadcasted_iota(jnp.int32, (T, width), 1) + start
                s = jnp.where(cc <= rr, s, NEG)
            m = m_s[...]
            m_new = jnp.maximum(m, jnp.max(s, axis=-1, keepdims=True))
            alpha = jnp.exp(m - m_new)
            p = jnp.exp(s - m_new)
            l_s[...] = alpha * l_s[...] + jnp.sum(p, axis=-1, keepdims=True)
            acc_s[...] = alpha * acc_s[...] + jnp.dot(p.astype(BF), v_ref[rows, :], preferred_element_type=F32)
            m_s[...] = m_new

        n_full = i // FLASH_CHUNK

        def full(cidx, carry):
            step(pl.multiple_of(cidx * CH, CH), CH, False)
            return carry

        lax.fori_loop(0, n_full, full, 0)
        for nb in range(1, FLASH_CHUNK + 1):
            @pl.when(i % FLASH_CHUNK == nb - 1)
            def _(nb=nb):
                step(pl.multiple_of(n_full * CH, CH), nb * T, True)

        l = l_s[...]
        o_ref[...] = (acc_s[...] / l).astype(o_ref.dtype)
        lse_ref[...] = jnp.broadcast_to(m_s[...] + jnp.log(l), (T, 128))

    return _call_carrying(
        body, name, (H, Lp // T),
        [pl.BlockSpec((T, HEAD_PAD), lambda h, i: (i, h)), pl.BlockSpec((Lp, HEAD_PAD), lambda h, i: (0, h)),
         pl.BlockSpec((Lp, V_DIM), lambda h, i: (0, 2 * h + 1))],
        [pl.BlockSpec((T, V_DIM), lambda h, i: (i, h)), pl.BlockSpec((None, T, 128), lambda h, i: (h, i, 0))],
        [jax.ShapeDtypeStruct((Lp, H * V_DIM), BF), jax.ShapeDtypeStruct((H, Lp, 128), F32)],
        [pltpu.VMEM((T, 1), F32), pltpu.VMEM((T, 1), F32), pltpu.VMEM((T, V_DIM), F32)], (Q, K, kv), phases)


def _flash_bwd(Q, K, kv, O, dO, lse, H, name, phases=()):
    Lp = Q.shape[0]
    T = _div_tile(Lp, 256, 16)
    n_t = Lp // T

    def body(q_ref, k_ref, v_ref, o_ref, do_ref, lse_ref, dq_ref, dk_ref, dv_ref, dk_acc, dv_acc):
        j = pl.program_id(1)

        @pl.when(j == 0)
        def _():
            dq_ref[...] = jnp.zeros_like(dq_ref)

        kj, vj = k_ref[...], v_ref[...]
        dk_acc[...] = jnp.zeros_like(dk_acc)
        dv_acc[...] = jnp.zeros_like(dv_acc)

        def step(start, width, masked):
            rows = pl.ds(start, width)
            qi, doi = q_ref[rows, :], do_ref[rows, :]
            delta = jnp.sum(doi.astype(F32) * o_ref[rows, :].astype(F32), axis=-1, keepdims=True)
            s = lax.dot_general(qi, kj, _DIMS['nt'], preferred_element_type=F32)
            p = jnp.exp(s - lse_ref[rows, :][:, :1])
            if masked:
                rr = lax.broadcasted_iota(jnp.int32, (width, T), 0) + start
                cc = lax.broadcasted_iota(jnp.int32, (width, T), 1) + j * T
                p = jnp.where(cc <= rr, p, 0.0)
            dp = lax.dot_general(doi, vj, _DIMS['nt'], preferred_element_type=F32)
            ds = (p * (dp - delta)).astype(BF)
            dv_acc[...] += lax.dot_general(p.astype(BF), doi, _DIMS['tn'], preferred_element_type=F32)
            dk_acc[...] += lax.dot_general(ds, qi, _DIMS['tn'], preferred_element_type=F32)
            dq_ref[rows, :] += jnp.dot(ds, kj, preferred_element_type=F32)

        head = (n_t - 1 - j) % FLASH_CHUNK + 1
        for nb in range(1, FLASH_CHUNK + 1):
            @pl.when(head == nb)
            def _(nb=nb):
                step(pl.multiple_of(j * T, T), nb * T, True)

        def full(cidx, carry):
            step(pl.multiple_of((j + head + cidx * FLASH_CHUNK) * T, T), FLASH_CHUNK * T, False)
            return carry

        lax.fori_loop(0, (n_t - j - head) // FLASH_CHUNK, full, 0)
        dk_ref[...] = dk_acc[...]
        dv_ref[...] = dv_acc[...].astype(dv_ref.dtype)

    head_q = pl.BlockSpec((Lp, HEAD_PAD), lambda h, j: (0, h))
    head_v = pl.BlockSpec((Lp, V_DIM), lambda h, j: (0, h))
    return _call_carrying(
        body, name, (H, n_t),
        [head_q, pl.BlockSpec((T, HEAD_PAD), lambda h, j: (j, h)), pl.BlockSpec((T, V_DIM), lambda h, j: (j, 2 * h + 1)),
         head_v, head_v, pl.BlockSpec((None, Lp, 128), lambda h, j: (h, 0, 0))],
        [head_q, pl.BlockSpec((T, HEAD_PAD), lambda h, j: (j, h)), pl.BlockSpec((T, V_DIM), lambda h, j: (j, h))],
        [jax.ShapeDtypeStruct((Lp, H * HEAD_PAD), F32), jax.ShapeDtypeStruct((Lp, H * HEAD_PAD), F32),
         jax.ShapeDtypeStruct((Lp, H * V_DIM), BF)],
        [pltpu.VMEM((T, HEAD_PAD), F32), pltpu.VMEM((T, V_DIM), F32)], (Q, K, kv, O, dO, lse), phases)


def _gate_fwd(z, y_pool, y_mla, lay, name):
    Lp, D = y_pool.shape
    tm = _div_tile(Lp, 256, 16)

    def body(gp_ref, gm_ref, yp_ref, ym_ref, o_ref):
        o_ref[...] = (jax.nn.sigmoid(gp_ref[...]) * yp_ref[...] + jax.nn.sigmoid(gm_ref[...]) * ym_ref[...]).astype(BF)

    return pl.pallas_call(
        body, name=name, grid=(Lp // tm,),
        in_specs=[_rb(tm, D, lay['gp'] // D), _rb(tm, D, lay['gm'] // D), _rb(tm, D), _rb(tm, D)], out_specs=_rb(tm, D),
        out_shape=jax.ShapeDtypeStruct((Lp, D), BF), compiler_params=_cp(('parallel',)),
    )(z, z, y_pool, y_mla)


def _gate_bwd(dy, z, y_pool, y_mla, lay, name):
    Lp, D = y_pool.shape
    tm = _div_tile(Lp, 256, 16)

    def body(dy_ref, gp_ref, gm_ref, yp_ref, ym_ref, dp_ref, dm_ref, dg_ref):
        dy_ = dy_ref[...]
        sp, sm = jax.nn.sigmoid(gp_ref[...]), jax.nn.sigmoid(gm_ref[...])
        dp_ref[...] = (dy_ * sp).astype(BF)
        dm_ref[...] = (dy_ * sm).astype(BF)
        dg_ref[:, :D] = (dy_ * yp_ref[...] * (sp * (1.0 - sp))).astype(BF)
        dg_ref[:, D:] = (dy_ * ym_ref[...] * (sm * (1.0 - sm))).astype(BF)

    return pl.pallas_call(
        body, name=name, grid=(Lp // tm,),
        in_specs=[_rb(tm, D), _rb(tm, D, lay['gp'] // D), _rb(tm, D, lay['gm'] // D), _rb(tm, D), _rb(tm, D)],
        out_specs=[_rb(tm, D), _rb(tm, D), _rb(tm, 2 * D)],
        out_shape=[jax.ShapeDtypeStruct((Lp, D), BF), jax.ShapeDtypeStruct((Lp, D), BF),
                   jax.ShapeDtypeStruct((Lp, 2 * D), BF)],
        compiler_params=_cp(('parallel',)),
    )(dy, z, z, y_pool, y_mla)


def _z_layout(D, QL, KVL):
    cq = POOL_WIDTH
    ckv = cq + QL
    gp = -(-(ckv + KVL) // D) * D
    gm = gp + D
    kr = gm + D
    return dict(QL=QL, KVL=KVL, cq=cq, ckv=ckv, gp=gp, gm=gm, kr=kr, width=kr + 128)


def _w_in_aligned(w_log, lay, D):
    n0 = POOL_WIDTH + lay['QL'] + lay['KVL']
    parts = [w_log[:, :n0], jnp.zeros((D, lay['gp'] - n0), w_log.dtype), w_log[:, n0 + QK_ROPE:],
             w_log[:, n0:n0 + QK_ROPE], jnp.zeros((D, 128 - QK_ROPE), w_log.dtype)]
    return jnp.concatenate(parts, axis=1)


def _w_in_logical(w_al, lay, D):
    n0 = POOL_WIDTH + lay['QL'] + lay['KVL']
    return jnp.concatenate([w_al[:, :n0], w_al[:, lay['kr']:lay['kr'] + QK_ROPE], w_al[:, lay['gp']:lay['gp'] + 2 * D]],
                           axis=1)


def kernel(x, meta_tokens, norm_ffn1_pre, norm_ffn1_post, ffn1_w_gu, ffn1_w_down, norm_mix_pre, norm_mix_post, w_in, pool_w, pool_scale, w_pool_o, q_a_norm, w_q_b, kv_a_norm, w_kv_b, w_mla_o, w_out, norm_ffn2_pre, norm_ffn2_post, ffn2_w_gu, ffn2_w_down, loss_target, m_meta_tokens, m_norm_ffn1_pre, m_norm_ffn1_post, m_ffn1_w_gu, m_ffn1_w_down, m_norm_mix_pre, m_norm_mix_post, m_w_in, m_pool_w, m_pool_scale, m_w_pool_o, m_q_a_norm, m_w_q_b, m_kv_a_norm, m_w_kv_b, m_w_mla_o, m_w_out, m_norm_ffn2_pre, m_norm_ffn2_post, m_ffn2_w_gu, m_ffn2_w_down, v_meta_tokens, v_norm_ffn1_pre, v_norm_ffn1_post, v_ffn1_w_gu, v_ffn1_w_down, v_norm_mix_pre, v_norm_mix_post, v_w_in, v_pool_w, v_pool_scale, v_w_pool_o, v_q_a_norm, v_w_q_b, v_kv_a_norm, v_w_kv_b, v_w_mla_o, v_w_out, v_norm_ffn2_pre, v_norm_ffn2_post, v_ffn2_w_gu, v_ffn2_w_down):
    given = dict(locals())
    W = {n: given[n] for n in WEIGHTS}
    M = {n: given['m_' + n] for n in WEIGHTS}
    V = {n: given['v_' + n] for n in WEIGHTS}

    S, D = x.shape[1], x.shape[2]
    NM = meta_tokens.shape[0]
    L = NM + S
    Lp = -(-L // ROW_ALIGN) * ROW_ALIGN
    QL, KVL = w_q_b.shape[1], w_kv_b.shape[1]
    H = w_q_b.shape[2] * N_CHIPS // QK_DIM
    lay = _z_layout(D, QL, KVL)
    gx, gy = lax.axis_index('x'), lax.axis_index('y')
    chip = 2 * gx + gy

    names = list(BIG)
    slab = {n: W[n][0] for n in names}
    kind = dict(BIG, meta_tokens='col')
    slab_shape = {n: tuple(slab[n].shape) for n in names}
    gc = lax.axis_index('c')
    c_arr = jnp.stack([gc]).astype(jnp.int32)
    place_arr = jnp.stack([gc, chip]).astype(jnp.int32)

    full = {n: _cast_place(slab[n], BIG[n], place_arr, BF, f'place_{n}') for n in names}
    full['meta_tokens'] = _cast_place(meta_tokens, 'col', place_arr, F32, 'place_meta_tokens')
    G0 = ['ffn1_w_gu', 'ffn1_w_down', 'meta_tokens']
    G1 = ['w_in', 'pool_w', 'w_pool_o', 'w_q_b', 'w_kv_b']
    G2 = ['w_mla_o', 'w_out']
    G3 = ['ffn2_w_gu', 'ffn2_w_down']

    def gather(phase_fn, group):
        return phase_fn([full[n] for n in group], [kind[n] for n in group])

    def arrived(group, res):
        full.update(zip(group, res))

    arrived(G0, _all_gather([full[n] for n in G0], [kind[n] for n in G0]))
    meta_full = full['meta_tokens']

    pos = jnp.arange(Lp, dtype=F32)
    inv = ROPE_THETA ** (-jnp.arange(0, QK_ROPE, 2, dtype=F32) / QK_ROPE)
    ang = pos[:, None] * inv[None, :]
    ang = jnp.concatenate([ang, ang], axis=-1)
    cos = jnp.pad(jnp.cos(ang), ((0, 0), (0, 128 - QK_ROPE)), constant_values=1.0)
    sin = jnp.pad(jnp.sin(ang), ((0, 0), (0, 128 - QK_ROPE)))

    h0 = jnp.concatenate([meta_full, x[0], jnp.zeros((Lp - L, D), F32)], axis=0)
    tgt = jnp.pad(loss_target[0], ((NM, Lp - L), (0, 0)))

    n1 = _norm_fwd(h0, norm_ffn1_pre, 'ffn1_norm')
    gu1, (res,) = _mm(n1, full['ffn1_w_gu'], 'nn', BF, 'ffn1_gu', [gather(_phase_gather_ici, G1)])
    arrived(G1, res)
    a1 = _swiglu_fwd(gu1, 'ffn1_act')
    f1, (res,) = _mm(a1, full['ffn1_w_down'], 'nn', F32, 'ffn1_down', [gather(_phase_gather_d2d, G1)])
    arrived(G1, res)
    h1 = _post_residual(h0, f1, norm_ffn1_post, 0.5, 'ffn1_res')

    in_cols = full['w_in'].shape[0] * full['w_in'].shape[2]
    w_in_al = _w_in_aligned(full['w_in'].transpose(1, 0, 2).reshape(D, in_cols), lay, D)
    w_q_pad = jnp.pad(full['w_q_b'].reshape(QL, H, QK_DIM), ((0, 0), (0, 0), (0, HEAD_PAD - QK_DIM))).reshape(
        QL, H * HEAD_PAD)

    n2 = _norm_fwd(h1, norm_mix_pre, 'mix_norm')
    z, (res,) = _mm(n2, w_in_al, 'nn', F32, 'mix_in', [gather(_phase_gather_ici, G2)])
    arrived(G2, res)
    d_pool = _pool_fwd(z, 'pool_fwd')
    e_pool, yp = _pool_mix_fwd(d_pool, full['pool_w'], pool_scale, 'pool_mix')
    y_pool = _mm(yp, full['w_pool_o'], 'nn', F32, 'pool_out')
    cqn, ckvn = _lora_norms(z, q_a_norm, kv_a_norm, lay, 'lora_norms')
    q_raw = _mm(cqn, w_q_pad, 'nn', F32, 'mla_q')
    kv = _mm(ckvn, full['w_kv_b'], 'nn', BF, 'mla_kv')
    Q, K = _qk_prep(q_raw, kv, z, cos, sin, lay, H, 'qk_prep')
    (O, lse), (res2, res3) = _flash_fwd(Q, K, kv, H, 'flash_fwd',
                                        [gather(_phase_gather_d2d, G2), gather(_phase_gather_ici, G3)])
    arrived(G2, res2)
    arrived(G3, res3)
    y_mla, (res,) = _mm(O, full['w_mla_o'], 'nn', F32, 'mla_out', [gather(_phase_gather_d2d, G3)])
    arrived(G3, res)
    y = _gate_fwd(z, y_pool, y_mla, lay, 'gate')
    m_mix = _mm(y, full['w_out'], 'nn', F32, 'mix_out')
    h2 = _post_residual(h1, m_mix, norm_mix_post, 1.0, 'mix_res')

    n3 = _norm_fwd(h2, norm_ffn2_pre, 'ffn2_norm')
    gu2 = _mm(n3, full['ffn2_w_gu'], 'nn', BF, 'ffn2_gu')
    a2 = _swiglu_fwd(gu2, 'ffn2_act')
    f2 = _mm(a2, full['ffn2_w_down'], 'nn', F32, 'ffn2_down')
    h3 = _post_residual(h2, f2, norm_ffn2_post, 0.5, 'ffn2_res')

    G, theirs, sums, halves, reduced = {}, {}, {}, {}, {}
    RA = ['ffn2_w_down', 'ffn2_w_gu']
    RB = ['w_out', 'w_pool_o', 'pool_w', 'w_mla_o', 'w_q_b', 'w_kv_b', 'w_in']
    RC1 = ['ffn1_w_down']
    RC2 = ['ffn1_w_gu']

    def pair_phase(group):
        return _phase_pair([G[n] for n in group], [BIG[n] for n in group], [slab_shape[n] for n in group])

    def pair_sums(group, res):
        for n, t in zip(group, res):
            sums[n] = _pair_sum(G[n], t, BIG[n], slab_shape[n], c_arr, f'rs_pair_sum_{n}')

    def chip_phase(group):
        return _phase_chip([sums[n] for n in group])

    def chip_sums(group, res):
        for n, ld in zip(group, res):
            halves[n] = _chip_sum(sums[n], ld, BIG[n], slab_shape[n], place_arr, f'rs_chip_sum_{n}')

    def final_phase(group):
        return _phase_final([halves[n] for n in group], [BIG[n] for n in group])

    dh3, sq = _loss_bwd(h3, tgt, NM, S, 'loss')

    df2, G['norm_ffn2_post'] = _post_bwd(dh3, f2, norm_ffn2_post, 0.5, 'ffn2_res_bwd')
    G['ffn2_w_down'] = _mm(a2, df2, 'tn', BF, 'ffn2_dw_down')
    da2 = _mm(df2, full['ffn2_w_down'], 'nt', BF, 'ffn2_da')
    dgu2 = _swiglu_bwd(da2, gu2, 'ffn2_act_bwd')
    G['ffn2_w_gu'] = _mm(n3, dgu2, 'tn', BF, 'ffn2_dw_gu')
    dn3, (res,) = _mm(dgu2, full['ffn2_w_gu'], 'nt', F32, 'ffn2_dn', [pair_phase(RA)])
    dh2, G['norm_ffn2_pre'] = _pre_bwd(dh3, dn3, h2, norm_ffn2_pre, 'ffn2_norm_bwd')
    pair_sums(RA, res)

    dm, G['norm_mix_post'] = _post_bwd(dh2, m_mix, norm_mix_post, 1.0, 'mix_res_bwd')
    G['w_out'] = _mm(y, dm, 'tn', BF, 'dw_out')
    dy = _mm(dm, full['w_out'], 'nt', F32, 'mix_out_bwd')
    dy_pool, dy_mla, d_gate = _gate_bwd(dy, z, y_pool, y_mla, lay, 'gate_bwd')
    G['w_pool_o'] = _mm(yp, dy_pool, 'tn', BF, 'dw_pool_o')
    dyp = _mm(dy_pool, full['w_pool_o'], 'nt', F32, 'pool_out_bwd')
    dd, G['pool_scale'], d_pool_w = _pool_mix_bwd(dyp, e_pool, d_pool, full['pool_w'], pool_scale, 'pool_mix_bwd')
    G['pool_w'] = d_pool_w.astype(BF)
    du_pool = _pool_bwd(dd, 'pool_bwd')
    G['w_mla_o'] = _mm(O, dy_mla, 'tn', BF, 'dw_mla_o')
    dO = _mm(dy_mla, full['w_mla_o'], 'nt', BF, 'mla_out_bwd')
    (dQ, dK, dV), (res,) = _flash_bwd(Q, K, kv, O, dO, lse, H, 'flash_bwd', [chip_phase(RA)])
    chip_sums(RA, res)
    dq_raw, dkv, dkr = _qk_prep_bwd(dQ, dK, dV, cos, sin, H, 'qk_prep_bwd')
    d_w_q_pad = _mm(cqn, dq_raw, 'tn', BF, 'dw_q_b')
    G['w_q_b'] = d_w_q_pad.reshape(QL, H, HEAD_PAD)[:, :, :QK_DIM].reshape(QL, H * QK_DIM)
    dcqn = _mm(dq_raw, w_q_pad, 'nt', F32, 'mla_q_bwd')
    G['w_kv_b'] = _mm(ckvn, dkv, 'tn', BF, 'dw_kv_b')
    dckvn = _mm(dkv, full['w_kv_b'], 'nt', F32, 'mla_kv_bwd')
    d_lora, G['q_a_norm'], G['kv_a_norm'] = _lora_norms_bwd(dcqn, dckvn, z, q_a_norm, kv_a_norm, lay, 'lora_norms_bwd')
    n0 = POOL_WIDTH + QL + KVL
    dz = jnp.concatenate([du_pool, d_lora, jnp.zeros((Lp, lay['gp'] - n0), BF), d_gate, dkr], axis=1)
    d_w_in_al, (res,) = _mm(n2, dz, 'tn', BF, 'dw_in', [final_phase(RA)])
    reduced.update(zip(RA, res))
    G['w_in'] = _w_in_logical(d_w_in_al, lay, D).reshape(D, N_CHIPS, in_cols // N_CHIPS).transpose(1, 0, 2)
    dn2, (res,) = _mm(dz, w_in_al, 'nt', F32, 'mix_in_bwd', [pair_phase(RB)])
    dh1, G['norm_mix_pre'] = _pre_bwd(dh2, dn2, h1, norm_mix_pre, 'mix_norm_bwd')
    pair_sums(RB, res)

    df1, G['norm_ffn1_post'] = _post_bwd(dh1, f1, norm_ffn1_post, 0.5, 'ffn1_res_bwd')
    G['ffn1_w_down'] = _mm(a1, df1, 'tn', BF, 'ffn1_dw_down')
    da1, (res,) = _mm(df1, full['ffn1_w_down'], 'nt', BF, 'ffn1_da', [pair_phase(RC1)])
    dgu1 = _swiglu_bwd(da1, gu1, 'ffn1_act_bwd')
    pair_sums(RC1, res)
    G['ffn1_w_gu'], (res_b, res_c1) = _mm(n1, dgu1, 'tn', BF, 'ffn1_dw_gu', [chip_phase(RB), chip_phase(RC1)])
    chip_sums(RB, res_b)
    chip_sums(RC1, res_c1)
    dn1, (res, res_b, res_c1) = _mm(dgu1, full['ffn1_w_gu'], 'nt', F32, 'ffn1_dn',
                                    [pair_phase(RC2), final_phase(RB), final_phase(RC1)])
    reduced.update(zip(RB, res_b))
    reduced.update(zip(RC1, res_c1))
    dh0, G['norm_ffn1_pre'] = _pre_bwd(dh1, dn1, h0, norm_ffn1_pre, 'ffn1_norm_bwd')
    pair_sums(RC2, res)
    (res,) = _run_phases([chip_phase(RC2)], 'rs_chip_exchange_tail')
    chip_sums(RC2, res)
    (res,) = _run_phases([final_phase(RC2)], 'rs_final_exchange_tail')
    reduced.update(zip(RC2, res))
    grad_x = dh0[NM:L][None]

    SW = max(D, POOL_WIDTH)

    def widen(a, fill=0.0):
        return jnp.pad(a, ((0, 0), (0, SW - a.shape[1])), constant_values=fill)

    rows = [widen(G[n]) for n in SMALL_VEC] + [widen(dh0[:NM]), widen(sq)]
    n_rows = len(SMALL_VEC) + NM + 1
    pad_rows = -(-n_rows // 8) * 8 - n_rows
    small = _all_reduce_small(jnp.concatenate(rows + [jnp.zeros((pad_rows, SW), F32)], axis=0))
    loss = (0.5 / D) * small[len(SMALL_VEC) + NM, 0]
    for i, n in enumerate(SMALL_VEC):
        reduced[n] = small[i:i + 1, :G[n].shape[1]]
    mw = meta_tokens.shape[1]
    reduced['meta_tokens'] = lax.dynamic_slice(small, (len(SMALL_VEC), chip * mw), (NM, mw))

    grads, deltas, new_m, new_v = {}, {}, {}, {}
    for n in names:
        w = slab[n]
        deltas[n], new_m[n], new_v[n] = [o[None] for o in _adam(w, reduced[n], M[n][0], V[n][0], f'adam_{n}')]
        grads[n] = reduced[n][None]
    n_vec = len(SMALL_VEC)
    vec_w = jnp.concatenate([widen(W[n]) for n in SMALL_VEC] + [jnp.zeros((16 - n_vec, SW), F32)], axis=0)
    vec_m = jnp.concatenate([widen(M[n]) for n in SMALL_VEC] + [jnp.zeros((16 - n_vec, SW), F32)], axis=0)
    vec_v = jnp.concatenate([widen(V[n], 1.0) for n in SMALL_VEC] + [jnp.ones((16 - n_vec, SW), F32)], axis=0)
    vec_g = jnp.concatenate([small[:n_vec], jnp.zeros((16 - n_vec, SW), F32)], axis=0)
    vd, vm, vv = _adam(vec_w, vec_g, vec_m, vec_v, 'adam_vectors')
    for i, n in enumerate(SMALL_VEC):
        wdt = W[n].shape[1]
        grads[n], deltas[n], new_m[n], new_v[n] = reduced[n], vd[i:i + 1, :wdt], vm[i:i + 1, :wdt], vv[i:i + 1, :wdt]
    grads['meta_tokens'] = reduced['meta_tokens']
    deltas['meta_tokens'], new_m['meta_tokens'], new_v['meta_tokens'] = _adam(
        meta_tokens, reduced['meta_tokens'], m_meta_tokens, v_meta_tokens, 'adam_meta')

    return (loss, grad_x, *[grads[n] for n in WEIGHTS], *[deltas[n] for n in WEIGHTS], *[new_m[n] for n in WEIGHTS],
            *[new_v[n] for n in WEIGHTS])
```

```python
import functools

import jax
import jax.numpy as jnp
import numpy as np
from jax import lax
from jax.experimental import pallas as pl
from jax.experimental.pallas import tpu as pltpu

F32 = jnp.float32
BF = jnp.bfloat16
MESH = pl.DeviceIdType.MESH

EPS = 1e-6
N_CHIPS = 4
POOL_WINDOWS = (2, 4, 8, 16)
POOL_GROUP = 256
POOL_WIDTH = POOL_GROUP * len(POOL_WINDOWS)
QK_NOPE = 128
QK_ROPE = 64
V_DIM = 128
QK_DIM = QK_NOPE + QK_ROPE
HEAD_PAD = 256
ROPE_THETA = 10000.0
SOFTMAX_SCALE = QK_DIM ** -0.5
ADAM_LR = 0.001
ADAM_B1 = 0.9
ADAM_B2 = 0.999
ADAM_EPS = 1e-08
ADAM_WD = 0.01
ADAM_STEP = 10
ROW_ALIGN = 256
VMEM_LIMIT = 48 * 1024 * 1024
NEG = -1e30
FLASH_CHUNK = 4
WEIGHTS = ['meta_tokens', 'norm_ffn1_pre', 'norm_ffn1_post', 'ffn1_w_gu', 'ffn1_w_down', 'norm_mix_pre',
           'norm_mix_post', 'w_in', 'pool_w', 'pool_scale', 'w_pool_o', 'q_a_norm', 'w_q_b', 'kv_a_norm', 'w_kv_b',
           'w_mla_o', 'w_out', 'norm_ffn2_pre', 'norm_ffn2_post', 'ffn2_w_gu', 'ffn2_w_down']
BIG = {'ffn1_w_gu': 'col', 'ffn1_w_down': 'row', 'w_in': 'lead', 'pool_w': 'pool', 'w_pool_o': 'col', 'w_q_b': 'col',
       'w_kv_b': 'col', 'w_mla_o': 'row', 'w_out': 'row', 'ffn2_w_gu': 'col', 'ffn2_w_down': 'row'}
SMALL_VEC = ['norm_ffn1_pre', 'norm_ffn1_post', 'norm_mix_pre', 'norm_mix_post', 'norm_ffn2_pre', 'norm_ffn2_post',
             'pool_scale', 'q_a_norm', 'kv_a_norm']


def _div_tile(n, target, align):
    best = None
    for t in range(align, min(n, target) + 1, align):
        if n % t == 0:
            best = t
    return best if best is not None else n


def _cp(sem=None):
    return pltpu.CompilerParams(dimension_semantics=sem, vmem_limit_bytes=VMEM_LIMIT)


def _full_shape(kind, slab):
    if kind == 'col':
        return (slab[0], slab[1] * N_CHIPS)
    if kind == 'row':
        return (slab[0] * N_CHIPS, slab[1])
    if kind == 'lead':
        return (N_CHIPS,) + tuple(slab)
    return (slab[0], slab[1] * N_CHIPS, slab[2])


def _half_shape(kind, slab):
    if kind == 'pool':
        return (slab[0], slab[1] // 2, slab[2])
    return (slab[0] // 2, slab[1])


def _half_of_slab(ref, kind, c):
    if kind == 'pool':
        n = ref.shape[1] // 2
        return ref.at[:, pl.ds(c * n, n), :]
    n = ref.shape[0] // 2
    return ref.at[pl.ds(c * n, n), :]


def _piece(ref, kind, k, c):
    if kind == 'col':
        r, w = ref.shape[0] // 2, ref.shape[1] // N_CHIPS
        return ref.at[pl.ds(c * r, r), pl.ds(k * w, w)]
    if kind == 'row':
        r = ref.shape[0] // (2 * N_CHIPS)
        return ref.at[pl.ds((2 * k + c) * r, r), :]
    if kind == 'lead':
        r = ref.shape[1] // 2
        return ref.at[k, pl.ds(c * r, r), :]
    r = ref.shape[1] // (2 * N_CHIPS)
    return ref.at[:, pl.ds((2 * k + c) * r, r), :]


def _slab(ref, kind, k):
    if kind == 'col':
        w = ref.shape[1] // N_CHIPS
        return ref.at[:, pl.ds(k * w, w)]
    if kind == 'row':
        r = ref.shape[0] // N_CHIPS
        return ref.at[pl.ds(k * r, r), :]
    if kind == 'lead':
        return ref.at[k]
    r = ref.shape[1] // N_CHIPS
    return ref.at[:, pl.ds(k * r, r), :]


def _place():
    x, y, c = lax.axis_index('x'), lax.axis_index('y'), lax.axis_index('c')
    return x, y, c


def _peer_chip(x, y, j):
    px = 1 - x if (j >> 1) else x
    py = 1 - y if (j & 1) else y
    return px, py


ANY = pl.BlockSpec(memory_space=pl.ANY)


def _all_gather(bufs, kinds):
    n = len(bufs)

    def body(*refs):
        outs = refs[n:2 * n]
        ssem, rsem, fssem, frsem = refs[2 * n:]
        x, y, c = _place()
        me = 2 * x + y
        sib = (x, y, 1 - c)
        sends = []
        for w in range(n):
            for j in (1, 2, 3):
                px, py = _peer_chip(x, y, j)
                mine = _piece(outs[w], kinds[w], me, c)
                sends.append(pltpu.make_async_remote_copy(
                    src_ref=mine, dst_ref=mine, send_sem=ssem.at[w, j - 1], recv_sem=rsem.at[w, j - 1],
                    device_id=(px, py, c), device_id_type=MESH))
        for cp in sends:
            cp.start()
        fwds = []
        for w in range(n):
            for j in (1, 2, 3):
                px, py = _peer_chip(x, y, j)
                got = _piece(outs[w], kinds[w], 2 * px + py, c)
                pltpu.make_async_remote_copy(src_ref=got, dst_ref=got, send_sem=ssem.at[w, j - 1],
                                             recv_sem=rsem.at[w, j - 1], device_id=(px, py, c),
                                             device_id_type=MESH).wait_recv()
                fwd = pltpu.make_async_remote_copy(src_ref=got, dst_ref=got, send_sem=fssem.at[w, j - 1],
                                                   recv_sem=frsem.at[w, j - 1], device_id=sib, device_id_type=MESH)
                fwd.start()
                fwds.append(fwd)
        for w in range(n):
            for j in (1, 2, 3):
                px, py = _peer_chip(x, y, j)
                other = _piece(outs[w], kinds[w], 2 * px + py, 1 - c)
                pltpu.make_async_remote_copy(src_ref=other, dst_ref=other, send_sem=fssem.at[w, j - 1],
                                             recv_sem=frsem.at[w, j - 1], device_id=sib,
                                             device_id_type=MESH).wait_recv()
        for cp in sends + fwds:
            cp.wait_send()

    return pl.pallas_call(
        body, name='all_gather_weights', out_shape=[jax.ShapeDtypeStruct(b.shape, b.dtype) for b in bufs],
        in_specs=[ANY] * n, out_specs=[ANY] * n, input_output_aliases={w: w for w in range(n)},
        scratch_shapes=[pltpu.SemaphoreType.DMA((n, 3)), pltpu.SemaphoreType.DMA((n, 3)),
                        pltpu.SemaphoreType.DMA((n, 3)), pltpu.SemaphoreType.DMA((n, 3))],
    )(*bufs)


def _remote(src, dst, ssem, rsem, k, dev):
    return pltpu.make_async_remote_copy(src_ref=src, dst_ref=dst, send_sem=ssem.at[k], recv_sem=rsem.at[k],
                                        device_id=dev, device_id_type=MESH)


def _phase_gather_ici(bufs, kinds):
    n = len(bufs)

    def build(ins, outs, ssem, rsem):
        x, y, c = _place()
        me = 2 * x + y
        ds = []
        for w in range(n):
            for j in (1, 2, 3):
                px, py = _peer_chip(x, y, j)
                mine = _piece(outs[w], kinds[w], me, c)
                got = _piece(outs[w], kinds[w], 2 * px + py, c)
                k = 3 * w + j - 1
                ds.append((_remote(mine, mine, ssem, rsem, k, (px, py, c)), _remote(got, got, ssem, rsem, k, (px, py, c))))
        return ds

    return dict(ins=list(bufs), outs=[jax.ShapeDtypeStruct(b.shape, b.dtype) for b in bufs],
                alias={w: w for w in range(n)}, nsem=3 * n, build=build)


def _phase_gather_d2d(bufs, kinds):
    n = len(bufs)

    def build(ins, outs, ssem, rsem):
        x, y, c = _place()
        sib = (x, y, 1 - c)
        ds = []
        for w in range(n):
            for j in (1, 2, 3):
                px, py = _peer_chip(x, y, j)
                have = _piece(outs[w], kinds[w], 2 * px + py, c)
                want = _piece(outs[w], kinds[w], 2 * px + py, 1 - c)
                k = 3 * w + j - 1
                ds.append((_remote(have, have, ssem, rsem, k, sib), _remote(want, want, ssem, rsem, k, sib)))
        return ds

    return dict(ins=list(bufs), outs=[jax.ShapeDtypeStruct(b.shape, b.dtype) for b in bufs],
                alias={w: w for w in range(n)}, nsem=3 * n, build=build)


def _phase_pair(grads, kinds, slabs):
    n = len(grads)

    def build(ins, outs, ssem, rsem):
        x, y, c = _place()
        sib = (x, y, 1 - c)
        ds = []
        for w in range(n):
            for k in range(N_CHIPS):
                cp = _remote(_piece(ins[w], kinds[w], k, 1 - c), outs[w].at[k], ssem, rsem, N_CHIPS * w + k, sib)
                ds.append((cp, cp))
        return ds

    return dict(ins=list(grads), alias={}, nsem=N_CHIPS * n, build=build,
                outs=[jax.ShapeDtypeStruct((N_CHIPS,) + _half_shape(k, s), g.dtype)
                      for g, k, s in zip(grads, kinds, slabs)])


def _phase_chip(sums):
    n = len(sums)

    def build(ins, outs, ssem, rsem):
        x, y, c = _place()
        ds = []
        for w in range(n):
            for j in (1, 2, 3):
                px, py = _peer_chip(x, y, j)
                cp = _remote(ins[w].at[2 * px + py], outs[w].at[j - 1], ssem, rsem, 3 * w + j - 1, (px, py, c))
                ds.append((cp, cp))
        return ds

    return dict(ins=list(sums), alias={}, nsem=3 * n, build=build,
                outs=[jax.ShapeDtypeStruct((3,) + s.shape[1:], s.dtype) for s in sums])


def _phase_final(slabs_half, kinds):
    n = len(slabs_half)

    def build(ins, outs, ssem, rsem):
        x, y, c = _place()
        sib = (x, y, 1 - c)
        ds = []
        for w in range(n):
            mine = _half_of_slab(outs[w], kinds[w], c)
            other = _half_of_slab(outs[w], kinds[w], 1 - c)
            ds.append((_remote(mine, mine, ssem, rsem, w, sib), _remote(other, other, ssem, rsem, w, sib)))
        return ds

    return dict(ins=list(slabs_half), outs=[jax.ShapeDtypeStruct(s.shape, s.dtype) for s in slabs_half],
                alias={w: w for w in range(n)}, nsem=n, build=build)


def _phase_operands(phases, n_main_in, n_main_out):
    ins, outs, alias, sems = [], [], {}, []
    for ph in phases:
        for i, o in ph['alias'].items():
            alias[n_main_in + len(ins) + i] = n_main_out + len(outs) + o
        ins += ph['ins']
        outs += ph['outs']
        sems += [pltpu.SemaphoreType.DMA((ph['nsem'],)), pltpu.SemaphoreType.DMA((ph['nsem'],))]
    return ins, outs, alias, sems


def _phase_copies(phases, in_refs, out_refs, sem_refs):
    ds, a, b = [], 0, 0
    for p, ph in enumerate(phases):
        ds += ph['build'](in_refs[a:a + len(ph['ins'])], out_refs[b:b + len(ph['outs'])], sem_refs[2 * p],
                          sem_refs[2 * p + 1])
        a += len(ph['ins'])
        b += len(ph['outs'])
    return ds


def _phase_start(phases, in_refs, out_refs, sem_refs):
    for send, _ in _phase_copies(phases, in_refs, out_refs, sem_refs):
        send.start()


def _phase_finish(phases, in_refs, out_refs, sem_refs):
    ds = _phase_copies(phases, in_refs, out_refs, sem_refs)
    for _, recv in ds:
        recv.wait_recv()
    for send, _ in ds:
        send.wait_send()


def _phase_results(phases, outs):
    res, b = [], 0
    for ph in phases:
        res.append(list(outs[b:b + len(ph['outs'])]))
        b += len(ph['outs'])
    return res


def _run_phases(phases, name):
    ins, out_shapes, alias, sems = _phase_operands(phases, 0, 0)
    n_in, n_out = len(ins), len(out_shapes)

    def body(*refs):
        in_refs, out_refs, sem_refs = refs[:n_in], refs[n_in:n_in + n_out], refs[n_in + n_out:]
        _phase_start(phases, in_refs, out_refs, sem_refs)
        _phase_finish(phases, in_refs, out_refs, sem_refs)

    outs = pl.pallas_call(body, name=name, out_shape=out_shapes, in_specs=[ANY] * n_in, out_specs=[ANY] * n_out,
                          input_output_aliases=alias, scratch_shapes=sems)(*ins)
    return _phase_results(phases, outs)


def _cast_place(shard, kind, place_arr, dtype, name):
    full = _full_shape(kind, shard.shape)
    if kind == 'pool':
        g, r, w = shard.shape
        grid = (g,)
        src = pl.BlockSpec((None, r, w), lambda i, s: (i, 0, 0))
        dst = pl.BlockSpec((None, r, w), lambda i, s: (i, s[1], 0))
    else:
        r, w = shard.shape
        tr = _div_tile(r, max(16, (1 << 19) // w), 16)
        nrb = r // tr
        grid = (nrb,)
        src = pl.BlockSpec((tr, w), lambda i, s: (i, 0))
        if kind == 'col':
            dst = pl.BlockSpec((tr, w), lambda i, s: (i, s[1]))
        elif kind == 'row':
            dst = pl.BlockSpec((tr, w), lambda i, s: (s[1] * nrb + i, 0))
        else:
            dst = pl.BlockSpec((None, tr, w), lambda i, s: (s[1], i, 0))

    def body(s_ref, x_ref, o_ref):
        o_ref[...] = x_ref[...].astype(o_ref.dtype)

    return pl.pallas_call(
        body, name=name,
        grid_spec=pltpu.PrefetchScalarGridSpec(num_scalar_prefetch=1, grid=grid, in_specs=[src], out_specs=dst),
        out_shape=jax.ShapeDtypeStruct(full, dtype), compiler_params=_cp(('parallel',)),
    )(place_arr, shard)


def _half_blocks(kind, slab):
    hs = _half_shape(kind, slab)
    if kind == 'pool':
        return (hs[0],), hs[1], 1
    tr = _div_tile(hs[0], max(16, (1 << 19) // hs[1]), 16)
    return (hs[0] // tr,), tr, hs[0] // tr


def _pair_sum(grad, theirs, kind, slab, c_arr, name):
    hs = _half_shape(kind, slab)
    (n_i,), tr, nrb = _half_blocks(kind, slab)
    if kind == 'pool':
        own = pl.BlockSpec((None, hs[1], hs[2]), lambda k, i, s: (i, 2 * k + s[0], 0))
        stk = pl.BlockSpec((None, None, hs[1], hs[2]), lambda k, i, s: (k, i, 0, 0))
    else:
        if kind == 'col':
            own = pl.BlockSpec((tr, hs[1]), lambda k, i, s: (s[0] * nrb + i, k))
        elif kind == 'row':
            own = pl.BlockSpec((tr, hs[1]), lambda k, i, s: ((2 * k + s[0]) * nrb + i, 0))
        else:
            own = pl.BlockSpec((None, tr, hs[1]), lambda k, i, s: (k, s[0] * nrb + i, 0))
        stk = pl.BlockSpec((None, tr, hs[1]), lambda k, i, s: (k, i, 0))

    def body(s_ref, a_ref, b_ref, o_ref):
        o_ref[...] = (a_ref[...].astype(F32) + b_ref[...].astype(F32)).astype(o_ref.dtype)

    return pl.pallas_call(
        body, name=name,
        grid_spec=pltpu.PrefetchScalarGridSpec(num_scalar_prefetch=1, grid=(N_CHIPS, n_i), in_specs=[own, stk],
                                               out_specs=stk),
        out_shape=jax.ShapeDtypeStruct((N_CHIPS,) + hs, BF), compiler_params=_cp(('parallel', 'parallel')),
    )(c_arr, grad, theirs)


def _chip_sum(sums, landed, kind, slab, place_arr, name):
    hs = _half_shape(kind, slab)
    (n_i,), tr, nrb = _half_blocks(kind, slab)
    if kind == 'pool':
        blk = (None, None, hs[1], hs[2])
        mine = pl.BlockSpec(blk, lambda i, s: (s[1], i, 0, 0))
        land = [pl.BlockSpec(blk, lambda i, s, j=j: (j, i, 0, 0)) for j in range(3)]
        out = pl.BlockSpec((None, hs[1], hs[2]), lambda i, s: (i, s[0], 0))
    else:
        blk = (None, tr, hs[1])
        mine = pl.BlockSpec(blk, lambda i, s: (s[1], i, 0))
        land = [pl.BlockSpec(blk, lambda i, s, j=j: (j, i, 0)) for j in range(3)]
        out = pl.BlockSpec((tr, hs[1]), lambda i, s: (s[0] * nrb + i, 0))

    def body(s_ref, a_ref, b_ref, c_ref, d_ref, o_ref):
        o_ref[...] = ((a_ref[...].astype(F32) + b_ref[...].astype(F32)) + c_ref[...].astype(F32)) + d_ref[...].astype(F32)

    return pl.pallas_call(
        body, name=name,
        grid_spec=pltpu.PrefetchScalarGridSpec(num_scalar_prefetch=1, grid=(n_i,), in_specs=[mine] + land, out_specs=out),
        out_shape=jax.ShapeDtypeStruct(tuple(slab), F32), compiler_params=_cp(('parallel',)),
    )(place_arr, sums, landed, landed, landed)


def _all_reduce_small(buf):
    rows, cols = buf.shape

    def body(in_ref, out_ref, land, ssem, rsem):
        x, y, c = _place()
        me = 4 * x + 2 * y + c
        land[me] = in_ref[...]
        started = []
        for j in range(1, 8):
            px = 1 - x if (j >> 2) & 1 else x
            py = 1 - y if (j >> 1) & 1 else y
            pc = 1 - c if j & 1 else c
            cp = pltpu.make_async_remote_copy(src_ref=in_ref, dst_ref=land.at[me], send_sem=ssem.at[j - 1],
                                              recv_sem=rsem.at[j - 1], device_id=(px, py, pc), device_id_type=MESH)
            cp.start()
            started.append(cp)
        for j in range(1, 8):
            px = 1 - x if (j >> 2) & 1 else x
            py = 1 - y if (j >> 1) & 1 else y
            pc = 1 - c if j & 1 else c
            slot = land.at[4 * px + 2 * py + pc]
            pltpu.make_async_remote_copy(src_ref=slot, dst_ref=slot, send_sem=ssem.at[j - 1], recv_sem=rsem.at[j - 1],
                                         device_id=(px, py, pc), device_id_type=MESH).wait_recv()
        for cp in started:
            cp.wait_send()
        acc = land[0]
        for d in range(1, 8):
            acc = acc + land[d]
        out_ref[...] = acc

    return pl.pallas_call(
        body, name='all_reduce_small', out_shape=jax.ShapeDtypeStruct((rows, cols), F32),
        in_specs=[pl.BlockSpec(memory_space=pltpu.VMEM)], out_specs=pl.BlockSpec(memory_space=pltpu.VMEM),
        scratch_shapes=[pltpu.VMEM((8, rows, cols), F32), pltpu.SemaphoreType.DMA((7,)), pltpu.SemaphoreType.DMA((7,))],
    )(buf)


def _elementwise(fn, ins, lead_index, out_shape, out_dtypes, name):
    nd = len(out_shape)
    r, cdim = out_shape[-2], out_shape[-1]
    tr = _div_tile(r, max(16, (1 << 19) // cdim), 16)
    grid = tuple(out_shape[:-2]) + (r // tr,)
    block = (None,) * (nd - 2) + (tr, cdim)

    def spec(lead):
        if lead is None:
            return pl.BlockSpec(block, lambda *g: tuple(g) + (0,))
        return pl.BlockSpec((None,) + block, lambda *g, lead=lead: (lead,) + tuple(g) + (0,))

    n_in = len(ins)

    def body(*refs):
        res = fn(*[r_[...] for r_ in refs[:n_in]])
        for o_ref, v in zip(refs[n_in:], res):
            o_ref[...] = v.astype(o_ref.dtype)

    return pl.pallas_call(
        body, name=name, grid=grid, in_specs=[spec(l) for l in lead_index],
        out_specs=[spec(None) for _ in out_dtypes],
        out_shape=[jax.ShapeDtypeStruct(tuple(out_shape), dt) for dt in out_dtypes],
        compiler_params=_cp(('parallel',) * len(grid)),
    )(*ins)


def _adam_fn(w, g, m, v):
    m = ADAM_B1 * m + (1.0 - ADAM_B1) * g
    v = ADAM_B2 * v + (1.0 - ADAM_B2) * (g * g)
    m_hat = m / (1.0 - ADAM_B1 ** ADAM_STEP)
    v_hat = v / (1.0 - ADAM_B2 ** ADAM_STEP)
    delta = -ADAM_LR * (m_hat / (jnp.sqrt(v_hat) + ADAM_EPS) + ADAM_WD * w)
    return delta, m, v


def _adam(w, g, m, v, name):
    return _elementwise(_adam_fn, [w, g, m, v], [None] * 4, w.shape, [F32] * 3, name)


_DIMS = {'nn': (((1,), (0,)), ((), ())), 'nt': (((1,), (1,)), ((), ())), 'tn': (((0,), (0,)), ((), ()))}


def _mm(a, b, mode, out_dtype, name, phases=()):
    if mode == 'nn':
        (M, K), N = a.shape, b.shape[1]
    elif mode == 'nt':
        (M, K), N = a.shape, b.shape[0]
    else:
        (K, M), N = a.shape, b.shape[1]
    if mode == 'tn':
        tm, tn, tk = _div_tile(M, 512, 128), _div_tile(N, 1024, 128), _div_tile(K, 2176, 16)
    else:
        tm, tn, tk = _div_tile(M, 1088, 16), _div_tile(N, 1024, 128), _div_tile(K, 2048, 128)
    nk = K // tk
    a_spec = {'nn': pl.BlockSpec((tm, tk), lambda i, j, k: (i, k)), 'nt': pl.BlockSpec((tm, tk), lambda i, j, k: (i, k)),
              'tn': pl.BlockSpec((tk, tm), lambda i, j, k: (k, i))}[mode]
    b_spec = {'nn': pl.BlockSpec((tk, tn), lambda i, j, k: (k, j)), 'nt': pl.BlockSpec((tn, tk), lambda i, j, k: (j, k)),
              'tn': pl.BlockSpec((tk, tn), lambda i, j, k: (k, j))}[mode]
    dims = _DIMS[mode]
    gm, gn = M // tm, N // tn
    extra_in, extra_out, alias, sems = _phase_operands(phases, 2, 1)
    n_ei, n_eo = len(extra_in), len(extra_out)

    def body(*refs):
        a_ref, b_ref, ein = refs[0], refs[1], refs[2:2 + n_ei]
        o_ref, eout = refs[2 + n_ei], refs[3 + n_ei:3 + n_ei + n_eo]
        acc_ref, sem_refs = refs[3 + n_ei + n_eo], refs[4 + n_ei + n_eo:]
        i, j, k = pl.program_id(0), pl.program_id(1), pl.program_id(2)
        if phases:
            @pl.when((i == 0) & (j == 0) & (k == 0))
            def _():
                _phase_start(phases, ein, eout, sem_refs)

        part = lax.dot_general(a_ref[...], b_ref[...], dims, preferred_element_type=F32)
        if nk == 1:
            o_ref[...] = part.astype(o_ref.dtype)
        else:
            @pl.when(k == 0)
            def _():
                acc_ref[...] = part

            @pl.when(k > 0)
            def _():
                acc_ref[...] += part

            @pl.when(k == nk - 1)
            def _():
                o_ref[...] = acc_ref[...].astype(o_ref.dtype)

        if phases:
            @pl.when((i == gm - 1) & (j == gn - 1) & (k == nk - 1))
            def _():
                _phase_finish(phases, ein, eout, sem_refs)

    outs = pl.pallas_call(
        body, name=name, grid=(gm, gn, nk), in_specs=[a_spec, b_spec] + [ANY] * n_ei,
        out_specs=[pl.BlockSpec((tm, tn), lambda i, j, k: (i, j))] + [ANY] * n_eo,
        out_shape=[jax.ShapeDtypeStruct((M, N), out_dtype)] + extra_out, input_output_aliases=alias,
        scratch_shapes=[pltpu.VMEM((tm, tn) if nk > 1 else (8, 128), F32)] + sems,
        compiler_params=_cp(('arbitrary',) * 3 if phases else ('parallel', 'parallel', 'arbitrary')),
    )(a, b, *extra_in)
    if phases:
        return outs[0], _phase_results(phases, outs[1:])
    return outs[0]


def _rb(tm, w, col=0):
    return pl.BlockSpec((tm, w), lambda i, col=col: (i, col))


def _fixed(shape):
    return pl.BlockSpec(shape, lambda i: (0,) * len(shape))


def _rms(x):
    return lax.rsqrt(jnp.mean(x * x, axis=-1, keepdims=True) + EPS)


def _norm_fwd(x, gain, name, width=None, col=0):
    Lp = x.shape[0]
    width = width or x.shape[1]
    tm = _div_tile(Lp, 256, 16)

    def body(x_ref, g_ref, o_ref):
        v = x_ref[...]
        o_ref[...] = (v * _rms(v) * g_ref[...]).astype(o_ref.dtype)

    return pl.pallas_call(
        body, name=name, grid=(Lp // tm,), in_specs=[_rb(tm, width, col), _fixed((1, width))], out_specs=_rb(tm, width),
        out_shape=jax.ShapeDtypeStruct((Lp, width), BF), compiler_params=_cp(('parallel',)),
    )(x, gain)


def _post_residual(h, f, gain, scale, name):
    Lp, D = h.shape
    tm = _div_tile(Lp, 256, 16)

    def body(h_ref, f_ref, g_ref, o_ref):
        v = f_ref[...]
        o_ref[...] = h_ref[...] + scale * (v * _rms(v) * g_ref[...])

    return pl.pallas_call(
        body, name=name, grid=(Lp // tm,), in_specs=[_rb(tm, D), _rb(tm, D), _fixed((1, D))], out_specs=_rb(tm, D),
        out_shape=jax.ShapeDtypeStruct((Lp, D), F32), compiler_params=_cp(('parallel',)),
    )(h, f, gain)


def _post_bwd(dh, f, gain, scale, name):
    Lp, D = dh.shape
    tm = _div_tile(Lp, 256, 16)

    def body(dh_ref, f_ref, g_ref, df_ref, dg_ref):
        v = f_ref[...]
        r = _rms(v)
        dy = scale * dh_ref[...]
        w = dy * g_ref[...]
        df_ref[...] = (r * w - v * (r * r * r) * jnp.mean(w * v, axis=-1, keepdims=True)).astype(df_ref.dtype)
        part = jnp.sum(dy * v * r, axis=0, keepdims=True)

        @pl.when(pl.program_id(0) == 0)
        def _():
            dg_ref[...] = part

        @pl.when(pl.program_id(0) > 0)
        def _():
            dg_ref[...] += part

    return pl.pallas_call(
        body, name=name, grid=(Lp // tm,), in_specs=[_rb(tm, D), _rb(tm, D), _fixed((1, D))],
        out_specs=[_rb(tm, D), _fixed((1, D))],
        out_shape=[jax.ShapeDtypeStruct((Lp, D), BF), jax.ShapeDtypeStruct((1, D), F32)],
        compiler_params=_cp(('arbitrary',)),
    )(dh, f, gain)


def _pre_bwd(dres, dn, h, gain, name, phases=()):
    Lp, D = h.shape
    tm = _div_tile(Lp, 256, 16)

    def body(dres_ref, dn_ref, h_ref, g_ref, dh_ref, dg_ref):
        v = h_ref[...]
        r = _rms(v)
        dy = dn_ref[...]
        w = dy * g_ref[...]
        dh_ref[...] = dres_ref[...] + r * w - v * (r * r * r) * jnp.mean(w * v, axis=-1, keepdims=True)
        part = jnp.sum(dy * v * r, axis=0, keepdims=True)

        @pl.when(pl.program_id(0) == 0)
        def _():
            dg_ref[...] = part

        @pl.when(pl.program_id(0) > 0)
        def _():
            dg_ref[...] += part

    return _call_carrying(
        body, name, (Lp // tm,), [_rb(tm, D), _rb(tm, D), _rb(tm, D), _fixed((1, D))], [_rb(tm, D), _fixed((1, D))],
        [jax.ShapeDtypeStruct((Lp, D), F32), jax.ShapeDtypeStruct((1, D), F32)], [], (dres, dn, h, gain), phases)


def _swiglu_fwd(gu, name, phases=()):
    Lp, F2 = gu.shape
    F = F2 // 2
    tm = _div_tile(Lp, 256, 16)

    def body(gu_ref, a_ref):
        g = gu_ref[:, :F].astype(F32)
        u = gu_ref[:, F:].astype(F32)
        a_ref[...] = (g * jax.nn.sigmoid(g) * u).astype(a_ref.dtype)

    return _call_carrying(body, name, (Lp // tm,), [_rb(tm, F2)], [_rb(tm, F)],
                          [jax.ShapeDtypeStruct((Lp, F), BF)], [], (gu,), phases)


def _swiglu_bwd(da, gu, name):
    Lp, F2 = gu.shape
    F = F2 // 2
    tm = _div_tile(Lp, 256, 16)

    def body(da_ref, gu_ref, o_ref):
        g = gu_ref[:, :F].astype(F32)
        u = gu_ref[:, F:].astype(F32)
        da_ = da_ref[...].astype(F32)
        s = jax.nn.sigmoid(g)
        o_ref[:, :F] = (da_ * u * (s * (1.0 + g * (1.0 - s)))).astype(o_ref.dtype)
        o_ref[:, F:] = (da_ * (g * s)).astype(o_ref.dtype)

    return pl.pallas_call(
        body, name=name, grid=(Lp // tm,), in_specs=[_rb(tm, F), _rb(tm, F2)], out_specs=_rb(tm, F2),
        out_shape=jax.ShapeDtypeStruct((Lp, F2), BF), compiler_params=_cp(('parallel',)),
    )(da, gu)


def _loss_bwd(h, tgt, n_meta, n_real, name):
    Lp, D = h.shape
    tm = _div_tile(Lp, 256, 16)

    def body(h_ref, t_ref, d_ref, l_ref):
        row = lax.broadcasted_iota(jnp.int32, (tm, 1), 0) + pl.program_id(0) * tm
        ok = (row >= n_meta) & (row < n_meta + n_real)
        err = jnp.where(ok, h_ref[...] - t_ref[...], 0.0)
        d_ref[...] = err / D
        part = jnp.full((1, 128), jnp.sum(err * err), F32)

        @pl.when(pl.program_id(0) == 0)
        def _():
            l_ref[...] = part

        @pl.when(pl.program_id(0) > 0)
        def _():
            l_ref[...] += part

    return pl.pallas_call(
        body, name=name, grid=(Lp // tm,), in_specs=[_rb(tm, D), _rb(tm, D)], out_specs=[_rb(tm, D), _fixed((1, 128))],
        out_shape=[jax.ShapeDtypeStruct((Lp, D), F32), jax.ShapeDtypeStruct((1, 128), F32)],
        compiler_params=_cp(('arbitrary',)),
    )(h, tgt)


def _split_bf16(v):
    hi = v.astype(BF)
    return hi, (v - hi.astype(F32)).astype(BF)


def _pool_fwd(z, name):
    Lp = z.shape[0]
    T = _div_tile(Lp, 256, 16)
    G = len(POOL_WINDOWS)

    def body(cur_ref, prev_ref, d_ref):
        i, g = pl.program_id(0), pl.program_id(1)
        w = jnp.left_shift(2, g)
        rr = lax.broadcasted_iota(jnp.int32, (T, T), 0)
        cc = lax.broadcasted_iota(jnp.int32, (T, T), 1)
        b_cur = jnp.where((cc <= rr) & (cc > rr - w), 1.0, 0.0).astype(BF)
        w_prev = jnp.where(i > 0, w, 0)
        b_prev = jnp.where(cc - T > rr - w_prev, 1.0, 0.0).astype(BF)
        u = cur_ref[...]
        s = jnp.zeros((T, POOL_GROUP), F32)
        for part in _split_bf16(u):
            s += jnp.dot(b_cur, part, preferred_element_type=F32)
        for part in _split_bf16(prev_ref[...]):
            s += jnp.dot(b_prev, part, preferred_element_type=F32)
        t = lax.broadcasted_iota(jnp.int32, (T, 1), 0) + i * T
        cnt = jnp.minimum(w, t + 1).astype(F32)
        d_ref[...] = (s / cnt - u).astype(d_ref.dtype)

    return pl.pallas_call(
        body, name=name, grid=(Lp // T, G),
        in_specs=[pl.BlockSpec((T, POOL_GROUP), lambda i, g: (i, g)),
                  pl.BlockSpec((T, POOL_GROUP), lambda i, g: (jnp.maximum(i - 1, 0), g))],
        out_specs=pl.BlockSpec((T, POOL_GROUP), lambda i, g: (i, g)),
        out_shape=jax.ShapeDtypeStruct((Lp, POOL_WIDTH), BF), compiler_params=_cp(('parallel', 'parallel')),
    )(z, z)


def _pool_bwd(dd, name):
    Lp = dd.shape[0]
    T = _div_tile(Lp, 256, 16)
    G = len(POOL_WINDOWS)
    n_t = Lp // T

    def body(cur_ref, next_ref, o_ref):
        i, g = pl.program_id(0), pl.program_id(1)
        w = jnp.left_shift(2, g)
        rr = lax.broadcasted_iota(jnp.int32, (T, T), 0)
        cc = lax.broadcasted_iota(jnp.int32, (T, T), 1)
        b_cur = jnp.where((cc >= rr) & (cc < rr + w), 1.0, 0.0).astype(BF)
        w_next = jnp.where(i < n_t - 1, w, 0)
        b_next = jnp.where(cc + T < rr + w_next, 1.0, 0.0).astype(BF)
        t = lax.broadcasted_iota(jnp.int32, (T, 1), 0) + i * T
        cur = cur_ref[...]
        e_cur = cur / jnp.minimum(w, t + 1).astype(F32)
        e_next = next_ref[...] / jnp.minimum(w, t + T + 1).astype(F32)
        s = jnp.zeros((T, POOL_GROUP), F32)
        for part in _split_bf16(e_cur):
            s += jnp.dot(b_cur, part, preferred_element_type=F32)
        for part in _split_bf16(e_next):
            s += jnp.dot(b_next, part, preferred_element_type=F32)
        o_ref[...] = (s - cur).astype(o_ref.dtype)

    return pl.pallas_call(
        body, name=name, grid=(n_t, G),
        in_specs=[pl.BlockSpec((T, POOL_GROUP), lambda i, g: (i, g)),
                  pl.BlockSpec((T, POOL_GROUP), lambda i, g: (jnp.minimum(i + 1, n_t - 1), g))],
        out_specs=pl.BlockSpec((T, POOL_GROUP), lambda i, g: (i, g)),
        out_shape=jax.ShapeDtypeStruct((Lp, POOL_WIDTH), BF), compiler_params=_cp(('parallel', 'parallel')),
    )(dd, dd)


def _pool_mix_fwd(d, pool_w, scale, name):
    Lp = d.shape[0]
    G = len(POOL_WINDOWS)
    tm = _div_tile(Lp, 1088, 16)

    def body(d_ref, w_ref, s_ref, e_ref, y_ref):
        e = jnp.dot(d_ref[...], w_ref[...], preferred_element_type=F32)
        e_ref[...] = e.astype(e_ref.dtype)
        y_ref[...] = (e * s_ref[...]).astype(y_ref.dtype)

    blk = pl.BlockSpec((tm, POOL_GROUP), lambda g, i: (i, g))
    return pl.pallas_call(
        body, name=name, grid=(G, Lp // tm),
        in_specs=[blk, pl.BlockSpec((None, POOL_GROUP, POOL_GROUP), lambda g, i: (g, 0, 0)),
                  pl.BlockSpec((1, POOL_GROUP), lambda g, i: (0, g))],
        out_specs=[blk, blk], out_shape=[jax.ShapeDtypeStruct((Lp, POOL_WIDTH), BF)] * 2,
        compiler_params=_cp(('parallel', 'parallel')),
    )(d, pool_w, scale)


def _pool_mix_bwd(dyp, e, d, pool_w, scale, name):
    Lp = d.shape[0]
    G = len(POOL_WINDOWS)
    tm = _div_tile(Lp, 1088, 16)

    def body(dy_ref, e_ref, d_ref, w_ref, s_ref, dd_ref, ds_ref, dw_ref):
        i = pl.program_id(1)
        dy = dy_ref[...]
        de = (dy * s_ref[...]).astype(BF)
        dd_ref[...] = lax.dot_general(de, w_ref[...], _DIMS['nt'], preferred_element_type=F32)
        ds_part = jnp.sum(dy * e_ref[...].astype(F32), axis=0, keepdims=True)
        dw_part = lax.dot_general(d_ref[...], de, _DIMS['tn'], preferred_element_type=F32)

        @pl.when(i == 0)
        def _():
            ds_ref[...] = ds_part
            dw_ref[...] = dw_part

        @pl.when(i > 0)
        def _():
            ds_ref[...] += ds_part
            dw_ref[...] += dw_part

    blk = pl.BlockSpec((tm, POOL_GROUP), lambda g, i: (i, g))
    wblk = pl.BlockSpec((None, POOL_GROUP, POOL_GROUP), lambda g, i: (g, 0, 0))
    sblk = pl.BlockSpec((1, POOL_GROUP), lambda g, i: (0, g))
    return pl.pallas_call(
        body, name=name, grid=(G, Lp // tm), in_specs=[blk, blk, blk, wblk, sblk], out_specs=[blk, sblk, wblk],
        out_shape=[jax.ShapeDtypeStruct((Lp, POOL_WIDTH), F32), jax.ShapeDtypeStruct((1, POOL_WIDTH), F32),
                   jax.ShapeDtypeStruct((G, POOL_GROUP, POOL_GROUP), F32)],
        compiler_params=_cp(('parallel', 'arbitrary')),
    )(dyp, e, d, pool_w, scale)


def _rot_half(t):
    lane = lax.broadcasted_iota(jnp.int32, t.shape, 1)
    half = QK_ROPE // 2
    return jnp.where(lane < half, -pltpu.roll(t, 128 - half, 1), pltpu.roll(t, half, 1))


def _lora_norms(z, q_gain, kv_gain, lay, name):
    Lp = z.shape[0]
    QL, KVL = lay['QL'], lay['KVL']
    tm = _div_tile(Lp, 256, 16)

    def body(q_ref, kv_ref, qg_ref, kg_ref, qo_ref, ko_ref):
        a = q_ref[...]
        qo_ref[...] = (a * _rms(a) * qg_ref[...]).astype(BF)
        b = kv_ref[...]
        ko_ref[...] = (b * _rms(b) * kg_ref[...]).astype(BF)

    return pl.pallas_call(
        body, name=name, grid=(Lp // tm,),
        in_specs=[_rb(tm, QL, lay['cq'] // QL), _rb(tm, KVL, lay['ckv'] // KVL), _fixed((1, QL)), _fixed((1, KVL))],
        out_specs=[_rb(tm, QL), _rb(tm, KVL)],
        out_shape=[jax.ShapeDtypeStruct((Lp, QL), BF), jax.ShapeDtypeStruct((Lp, KVL), BF)],
        compiler_params=_cp(('parallel',)),
    )(z, z, q_gain, kv_gain)


def _lora_norms_bwd(dqn, dkn, z, q_gain, kv_gain, lay, name):
    Lp = z.shape[0]
    QL, KVL = lay['QL'], lay['KVL']
    tm = _div_tile(Lp, 256, 16)

    def one(dy, v, gain):
        r = _rms(v)
        w = dy * gain
        return r * w - v * (r * r * r) * jnp.mean(w * v, axis=-1, keepdims=True), jnp.sum(dy * v * r, axis=0, keepdims=True)

    def body(dq_ref, dk_ref, q_ref, kv_ref, qg_ref, kg_ref, o_ref, dqg_ref, dkg_ref):
        da, ga = one(dq_ref[...], q_ref[...], qg_ref[...])
        db, gb = one(dk_ref[...], kv_ref[...], kg_ref[...])
        o_ref[:, :QL] = da.astype(BF)
        o_ref[:, QL:] = db.astype(BF)

        @pl.when(pl.program_id(0) == 0)
        def _():
            dqg_ref[...] = ga
            dkg_ref[...] = gb

        @pl.when(pl.program_id(0) > 0)
        def _():
            dqg_ref[...] += ga
            dkg_ref[...] += gb

    return pl.pallas_call(
        body, name=name, grid=(Lp // tm,),
        in_specs=[_rb(tm, QL), _rb(tm, KVL), _rb(tm, QL, lay['cq'] // QL), _rb(tm, KVL, lay['ckv'] // KVL),
                  _fixed((1, QL)), _fixed((1, KVL))],
        out_specs=[_rb(tm, QL + KVL), _fixed((1, QL)), _fixed((1, KVL))],
        out_shape=[jax.ShapeDtypeStruct((Lp, QL + KVL), BF), jax.ShapeDtypeStruct((1, QL), F32),
                   jax.ShapeDtypeStruct((1, KVL), F32)],
        compiler_params=_cp(('arbitrary',)),
    )(dqn, dkn, z, z, q_gain, kv_gain)


def _qk_prep(q_raw, kv, z, cos, sin, lay, H, name):
    Lp = q_raw.shape[0]
    W = H * HEAD_PAD
    tm = _div_tile(Lp, 256, 16)

    def body(q_ref, kv_ref, kr_ref, c_ref, s_ref, qo_ref, ko_ref):
        c, s = c_ref[...], s_ref[...]

        def rope(t):
            return t * c + _rot_half(t) * s

        kpe = rope(kr_ref[...]).astype(BF)
        for h in range(H):
            b = h * HEAD_PAD
            qo_ref[:, b:b + 128] = (q_ref[:, b:b + 128] * SOFTMAX_SCALE).astype(BF)
            qo_ref[:, b + 128:b + 256] = (rope(q_ref[:, b + 128:b + 256]) * SOFTMAX_SCALE).astype(BF)
            ko_ref[:, b:b + 128] = kv_ref[:, b:b + 128]
            ko_ref[:, b + 128:b + 256] = kpe

    return pl.pallas_call(
        body, name=name, grid=(Lp // tm,),
        in_specs=[_rb(tm, W), _rb(tm, W), _rb(tm, 128, lay['kr'] // 128), _rb(tm, 128), _rb(tm, 128)],
        out_specs=[_rb(tm, W), _rb(tm, W)], out_shape=[jax.ShapeDtypeStruct((Lp, W), BF)] * 2,
        compiler_params=_cp(('parallel',)),
    )(q_raw, kv, z, cos, sin)


def _qk_prep_bwd(dQ, dK, dV, cos, sin, H, name):
    Lp = dQ.shape[0]
    W = H * HEAD_PAD
    tm = _div_tile(Lp, 256, 16)

    def body(dq_ref, dk_ref, dv_ref, c_ref, s_ref, qo_ref, kvo_ref, kro_ref):
        c, s = c_ref[...], s_ref[...]

        def unrope(t):
            return t * c - _rot_half(t * s)

        acc = jnp.zeros((tm, 128), F32)
        for h in range(H):
            b = h * HEAD_PAD
            qo_ref[:, b:b + 128] = (dq_ref[:, b:b + 128] * SOFTMAX_SCALE).astype(BF)
            qo_ref[:, b + 128:b + 256] = (unrope(dq_ref[:, b + 128:b + 256]) * SOFTMAX_SCALE).astype(BF)
            kvo_ref[:, b:b + 128] = dk_ref[:, b:b + 128].astype(BF)
            kvo_ref[:, b + 128:b + 256] = dv_ref[:, h * V_DIM:(h + 1) * V_DIM]
            acc += dk_ref[:, b + 128:b + 256]
        kro_ref[...] = unrope(acc).astype(BF)

    return pl.pallas_call(
        body, name=name, grid=(Lp // tm,),
        in_specs=[_rb(tm, W), _rb(tm, W), _rb(tm, H * V_DIM), _rb(tm, 128), _rb(tm, 128)],
        out_specs=[_rb(tm, W), _rb(tm, W), _rb(tm, 128)],
        out_shape=[jax.ShapeDtypeStruct((Lp, W), BF), jax.ShapeDtypeStruct((Lp, W), BF),
                   jax.ShapeDtypeStruct((Lp, 128), BF)],
        compiler_params=_cp(('parallel',)),
    )(dQ, dK, dV, cos, sin)


def _call_carrying(core, name, grid, in_specs, out_specs, out_shape, scratch, args, phases):
    n_in, n_out, n_scr = len(in_specs), len(out_specs), len(scratch)
    extra_in, extra_out, alias, sems = _phase_operands(phases, n_in, n_out)
    n_ei, n_eo = len(extra_in), len(extra_out)

    def body(*refs):
        ins, ein = refs[:n_in], refs[n_in:n_in + n_ei]
        outs = refs[n_in + n_ei:n_in + n_ei + n_out]
        eout = refs[n_in + n_ei + n_out:n_in + n_ei + n_out + n_eo]
        rest = refs[n_in + n_ei + n_out + n_eo:]
        scr, sem_refs = rest[:n_scr], rest[n_scr:]
        if phases:
            ids = [pl.program_id(d) for d in range(len(grid))]
            first, last = ids[0] == 0, ids[0] == grid[0] - 1
            for d in range(1, len(grid)):
                first, last = first & (ids[d] == 0), last & (ids[d] == grid[d] - 1)

            @pl.when(first)
            def _():
                _phase_start(phases, ein, eout, sem_refs)

        core(*ins, *outs, *scr)
        if phases:
            @pl.when(last)
            def _():
                _phase_finish(phases, ein, eout, sem_refs)

    outs = pl.pallas_call(
        body, name=name, grid=grid, in_specs=list(in_specs) + [ANY] * n_ei, out_specs=list(out_specs) + [ANY] * n_eo,
        out_shape=list(out_shape) + extra_out, input_output_aliases=alias, scratch_shapes=list(scratch) + sems,
        compiler_params=_cp(('arbitrary',) * len(grid)),
    )(*args, *extra_in)
    return list(outs[:n_out]), _phase_results(phases, outs[n_out:])


def _flash_fwd(Q, K, kv, H, name, phases=()):
    Lp = Q.shape[0]
    T = _div_tile(Lp, 256, 16)

    CH = FLASH_CHUNK * T

    def body(q_ref, k_ref, v_ref, o_ref, lse_ref, m_s, l_s, acc_s):
        i = pl.program_id(1)
        q = q_ref[...]
        m_s[...] = jnp.full((T, 1), NEG, F32)
        l_s[...] = jnp.zeros((T, 1), F32)
        acc_s[...] = jnp.zeros((T, V_DIM), F32)

        def step(start, width, masked):
            rows = pl.ds(start, width)
            s = lax.dot_general(q, k_ref[rows, :], _DIMS['nt'], preferred_element_type=F32)
            if masked:
                rr = lax.broadcasted_iota(jnp.int32, (T, width), 0) + i * T
                cc = lax.broadcasted_iota(jnp.int32, (T, width), 1) + start
                s = jnp.where(cc <= rr, s, NEG)
            m = m_s[...]
            m_new = jnp.maximum(m, jnp.max(s, axis=-1, keepdims=True))
            alpha = jnp.exp(m - m_new)
            p = jnp.exp(s - m_new)
            l_s[...] = alpha * l_s[...] + jnp.sum(p, axis=-1, keepdims=True)
            acc_s[...] = alpha * acc_s[...] + jnp.dot(p.astype(BF), v_ref[rows, :], preferred_element_type=F32)
            m_s[...] = m_new

        n_full = i // FLASH_CHUNK

        def full(cidx, carry):
            step(pl.multiple_of(cidx * CH, CH), CH, False)
            return carry

        lax.fori_loop(0, n_full, full, 0)
        for nb in range(1, FLASH_CHUNK + 1):
            @pl.when(i % FLASH_CHUNK == nb - 1)
            def _(nb=nb):
                step(pl.multiple_of(n_full * CH, CH), nb * T, True)

        l = l_s[...]
        o_ref[...] = (acc_s[...] / l).astype(o_ref.dtype)
        lse_ref[...] = jnp.broadcast_to(m_s[...] + jnp.log(l), (T, 128))

    return _call_carrying(
        body, name, (H, Lp // T),
        [pl.BlockSpec((T, HEAD_PAD), lambda h, i: (i, h)), pl.BlockSpec((Lp, HEAD_PAD), lambda h, i: (0, h)),
         pl.BlockSpec((Lp, V_DIM), lambda h, i: (0, 2 * h + 1))],
        [pl.BlockSpec((T, V_DIM), lambda h, i: (i, h)), pl.BlockSpec((None, T, 128), lambda h, i: (h, i, 0))],
        [jax.ShapeDtypeStruct((Lp, H * V_DIM), BF), jax.ShapeDtypeStruct((H, Lp, 128), F32)],
        [pltpu.VMEM((T, 1), F32), pltpu.VMEM((T, 1), F32), pltpu.VMEM((T, V_DIM), F32)], (Q, K, kv), phases)


def _flash_bwd(Q, K, kv, O, dO, lse, H, name, phases=()):
    Lp = Q.shape[0]
    T = _div_tile(Lp, 256, 16)
    n_t = Lp // T

    def body(q_ref, k_ref, v_ref, o_ref, do_ref, lse_ref, dq_ref, dk_ref, dv_ref, dk_acc, dv_acc):
        j = pl.program_id(1)

        @pl.when(j == 0)
        def _():
            dq_ref[...] = jnp.zeros_like(dq_ref)

        kj, vj = k_ref[...], v_ref[...]
        dk_acc[...] = jnp.zeros_like(dk_acc)
        dv_acc[...] = jnp.zeros_like(dv_acc)

        def step(start, width, masked):
            rows = pl.ds(start, width)
            qi, doi = q_ref[rows, :], do_ref[rows, :]
            delta = jnp.sum(doi.astype(F32) * o_ref[rows, :].astype(F32), axis=-1, keepdims=True)
            s = lax.dot_general(qi, kj, _DIMS['nt'], preferred_element_type=F32)
            p = jnp.exp(s - lse_ref[rows, :][:, :1])
            if masked:
                rr = lax.broadcasted_iota(jnp.int32, (width, T), 0) + start
                cc = lax.broadcasted_iota(jnp.int32, (width, T), 1) + j * T
                p = jnp.where(cc <= rr, p, 0.0)
            dp = lax.dot_general(doi, vj, _DIMS['nt'], preferred_element_type=F32)
            ds = (p * (dp - delta)).astype(BF)
            dv_acc[...] += lax.dot_general(p.astype(BF), doi, _DIMS['tn'], preferred_element_type=F32)
            dk_acc[...] += lax.dot_general(ds, qi, _DIMS['tn'], preferred_element_type=F32)
            dq_ref[rows, :] += jnp.dot(ds, kj, preferred_element_type=F32)

        head = (n_t - 1 - j) % FLASH_CHUNK + 1
        for nb in range(1, FLASH_CHUNK + 1):
            @pl.when(head == nb)
            def _(nb=nb):
                step(pl.multiple_of(j * T, T), nb * T, True)

        def full(cidx, carry):
            step(pl.multiple_of((j + head + cidx * FLASH_CHUNK) * T, T), FLASH_CHUNK * T, False)
            return carry

        lax.fori_loop(0, (n_t - j - head) // FLASH_CHUNK, full, 0)
        dk_ref[...] = dk_acc[...]
        dv_ref[...] = dv_acc[...].astype(dv_ref.dtype)

    head_q = pl.BlockSpec((Lp, HEAD_PAD), lambda h, j: (0, h))
    head_v = pl.BlockSpec((Lp, V_DIM), lambda h, j: (0, h))
    return _call_carrying(
        body, name, (H, n_t),
        [head_q, pl.BlockSpec((T, HEAD_PAD), lambda h, j: (j, h)), pl.BlockSpec((T, V_DIM), lambda h, j: (j, 2 * h + 1)),
         head_v, head_v, pl.BlockSpec((None, Lp, 128), lambda h, j: (h, 0, 0))],
        [head_q, pl.BlockSpec((T, HEAD_PAD), lambda h, j: (j, h)), pl.BlockSpec((T, V_DIM), lambda h, j: (j, h))],
        [jax.ShapeDtypeStruct((Lp, H * HEAD_PAD), F32), jax.ShapeDtypeStruct((Lp, H * HEAD_PAD), F32),
         jax.ShapeDtypeStruct((Lp, H * V_DIM), BF)],
        [pltpu.VMEM((T, HEAD_PAD), F32), pltpu.VMEM((T, V_DIM), F32)], (Q, K, kv, O, dO, lse), phases)


def _gate_fwd(z, y_pool, y_mla, lay, name):
    Lp, D = y_pool.shape
    tm = _div_tile(Lp, 256, 16)

    def body(gp_ref, gm_ref, yp_ref, ym_ref, o_ref):
        o_ref[...] = (jax.nn.sigmoid(gp_ref[...]) * yp_ref[...] + jax.nn.sigmoid(gm_ref[...]) * ym_ref[...]).astype(BF)

    return pl.pallas_call(
        body, name=name, grid=(Lp // tm,),
        in_specs=[_rb(tm, D, lay['gp'] // D), _rb(tm, D, lay['gm'] // D), _rb(tm, D), _rb(tm, D)], out_specs=_rb(tm, D),
        out_shape=jax.ShapeDtypeStruct((Lp, D), BF), compiler_params=_cp(('parallel',)),
    )(z, z, y_pool, y_mla)


def _gate_bwd(dy, z, y_pool, y_mla, lay, name):
    Lp, D = y_pool.shape
    tm = _div_tile(Lp, 256, 16)

    def body(dy_ref, gp_ref, gm_ref, yp_ref, ym_ref, dp_ref, dm_ref, dg_ref):
        dy_ = dy_ref[...]
        sp, sm = jax.nn.sigmoid(gp_ref[...]), jax.nn.sigmoid(gm_ref[...])
        dp_ref[...] = (dy_ * sp).astype(BF)
        dm_ref[...] = (dy_ * sm).astype(BF)
        dg_ref[:, :D] = (dy_ * yp_ref[...] * (sp * (1.0 - sp))).astype(BF)
        dg_ref[:, D:] = (dy_ * ym_ref[...] * (sm * (1.0 - sm))).astype(BF)

    return pl.pallas_call(
        body, name=name, grid=(Lp // tm,),
        in_specs=[_rb(tm, D), _rb(tm, D, lay['gp'] // D), _rb(tm, D, lay['gm'] // D), _rb(tm, D), _rb(tm, D)],
        out_specs=[_rb(tm, D), _rb(tm, D), _rb(tm, 2 * D)],
        out_shape=[jax.ShapeDtypeStruct((Lp, D), BF), jax.ShapeDtypeStruct((Lp, D), BF),
                   jax.ShapeDtypeStruct((Lp, 2 * D), BF)],
        compiler_params=_cp(('parallel',)),
    )(dy, z, z, y_pool, y_mla)


def _z_layout(D, QL, KVL):
    cq = POOL_WIDTH
    ckv = cq + QL
    gp = -(-(ckv + KVL) // D) * D
    gm = gp + D
    kr = gm + D
    return dict(QL=QL, KVL=KVL, cq=cq, ckv=ckv, gp=gp, gm=gm, kr=kr, width=kr + 128)


def _w_in_aligned(w_log, lay, D):
    n0 = POOL_WIDTH + lay['QL'] + lay['KVL']
    parts = [w_log[:, :n0], jnp.zeros((D, lay['gp'] - n0), w_log.dtype), w_log[:, n0 + QK_ROPE:],
             w_log[:, n0:n0 + QK_ROPE], jnp.zeros((D, 128 - QK_ROPE), w_log.dtype)]
    return jnp.concatenate(parts, axis=1)


def _w_in_logical(w_al, lay, D):
    n0 = POOL_WIDTH + lay['QL'] + lay['KVL']
    return jnp.concatenate([w_al[:, :n0], w_al[:, lay['kr']:lay['kr'] + QK_ROPE], w_al[:, lay['gp']:lay['gp'] + 2 * D]],
                           axis=1)


def kernel(x, meta_tokens, norm_ffn1_pre, norm_ffn1_post, ffn1_w_gu, ffn1_w_down, norm_mix_pre, norm_mix_post, w_in, pool_w, pool_scale, w_pool_o, q_a_norm, w_q_b, kv_a_norm, w_kv_b, w_mla_o, w_out, norm_ffn2_pre, norm_ffn2_post, ffn2_w_gu, ffn2_w_down, loss_target, m_meta_tokens, m_norm_ffn1_pre, m_norm_ffn1_post, m_ffn1_w_gu, m_ffn1_w_down, m_norm_mix_pre, m_norm_mix_post, m_w_in, m_pool_w, m_pool_scale, m_w_pool_o, m_q_a_norm, m_w_q_b, m_kv_a_norm, m_w_kv_b, m_w_mla_o, m_w_out, m_norm_ffn2_pre, m_norm_ffn2_post, m_ffn2_w_gu, m_ffn2_w_down, v_meta_tokens, v_norm_ffn1_pre, v_norm_ffn1_post, v_ffn1_w_gu, v_ffn1_w_down, v_norm_mix_pre, v_norm_mix_post, v_w_in, v_pool_w, v_pool_scale, v_w_pool_o, v_q_a_norm, v_w_q_b, v_kv_a_norm, v_w_kv_b, v_w_mla_o, v_w_out, v_norm_ffn2_pre, v_norm_ffn2_post, v_ffn2_w_gu, v_ffn2_w_down):
    given = dict(locals())
    W = {n: given[n] for n in WEIGHTS}
    M = {n: given['m_' + n] for n in WEIGHTS}
    V = {n: given['v_' + n] for n in WEIGHTS}

    S, D = x.shape[1], x.shape[2]
    NM = meta_tokens.shape[0]
    L = NM + S
    Lp = -(-L // ROW_ALIGN) * ROW_ALIGN
    QL, KVL = w_q_b.shape[1], w_kv_b.shape[1]
    H = w_q_b.shape[2] * N_CHIPS // QK_DIM
    lay = _z_layout(D, QL, KVL)
    gx, gy = lax.axis_index('x'), lax.axis_index('y')
    chip = 2 * gx + gy

    names = list(BIG)
    slab = {n: W[n][0] for n in names}
    kind = dict(BIG, meta_tokens='col')
    slab_shape = {n: tuple(slab[n].shape) for n in names}
    gc = lax.axis_index('c')
    c_arr = jnp.stack([gc]).astype(jnp.int32)
    place_arr = jnp.stack([gc, chip]).astype(jnp.int32)

    full = {n: _cast_place(slab[n], BIG[n], place_arr, BF, f'place_{n}') for n in names}
    full['meta_tokens'] = _cast_place(meta_tokens, 'col', place_arr, F32, 'place_meta_tokens')
    G0 = ['ffn1_w_gu', 'meta_tokens']
    G0B = ['ffn1_w_down']
    G1 = ['w_in', 'pool_w', 'w_pool_o', 'w_q_b', 'w_kv_b']
    G2 = ['w_mla_o', 'w_out']
    G3 = ['ffn2_w_gu', 'ffn2_w_down']

    def gather(phase_fn, group):
        return phase_fn([full[n] for n in group], [kind[n] for n in group])

    def arrived(group, res):
        full.update(zip(group, res))

    arrived(G0, _all_gather([full[n] for n in G0], [kind[n] for n in G0]))
    meta_full = full['meta_tokens']

    pos = jnp.arange(Lp, dtype=F32)
    inv = ROPE_THETA ** (-jnp.arange(0, QK_ROPE, 2, dtype=F32) / QK_ROPE)
    ang = pos[:, None] * inv[None, :]
    ang = jnp.concatenate([ang, ang], axis=-1)
    cos = jnp.pad(jnp.cos(ang), ((0, 0), (0, 128 - QK_ROPE)), constant_values=1.0)
    sin = jnp.pad(jnp.sin(ang), ((0, 0), (0, 128 - QK_ROPE)))

    h0 = jnp.concatenate([meta_full, x[0], jnp.zeros((Lp - L, D), F32)], axis=0)
    tgt = jnp.pad(loss_target[0], ((NM, Lp - L), (0, 0)))

    n1 = _norm_fwd(h0, norm_ffn1_pre, 'ffn1_norm')
    gu1, (res_b, res) = _mm(n1, full['ffn1_w_gu'], 'nn', BF, 'ffn1_gu',
                            [gather(_phase_gather_ici, G0B), gather(_phase_gather_ici, G1)])
    arrived(G0B, res_b)
    arrived(G1, res)
    (a1,), (res_b,) = _swiglu_fwd(gu1, 'ffn1_act', [gather(_phase_gather_d2d, G0B)])
    arrived(G0B, res_b)
    f1, (res,) = _mm(a1, full['ffn1_w_down'], 'nn', F32, 'ffn1_down', [gather(_phase_gather_d2d, G1)])
    arrived(G1, res)
    h1 = _post_residual(h0, f1, norm_ffn1_post, 0.5, 'ffn1_res')

    in_cols = full['w_in'].shape[0] * full['w_in'].shape[2]
    w_in_al = _w_in_aligned(full['w_in'].transpose(1, 0, 2).reshape(D, in_cols), lay, D)
    w_q_pad = jnp.pad(full['w_q_b'].reshape(QL, H, QK_DIM), ((0, 0), (0, 0), (0, HEAD_PAD - QK_DIM))).reshape(
        QL, H * HEAD_PAD)

    n2 = _norm_fwd(h1, norm_mix_pre, 'mix_norm')
    z, (res,) = _mm(n2, w_in_al, 'nn', F32, 'mix_in', [gather(_phase_gather_ici, G2)])
    arrived(G2, res)
    d_pool = _pool_fwd(z, 'pool_fwd')
    e_pool, yp = _pool_mix_fwd(d_pool, full['pool_w'], pool_scale, 'pool_mix')
    y_pool = _mm(yp, full['w_pool_o'], 'nn', F32, 'pool_out')
    cqn, ckvn = _lora_norms(z, q_a_norm, kv_a_norm, lay, 'lora_norms')
    q_raw = _mm(cqn, w_q_pad, 'nn', F32, 'mla_q')
    kv = _mm(ckvn, full['w_kv_b'], 'nn', BF, 'mla_kv')
    Q, K = _qk_prep(q_raw, kv, z, cos, sin, lay, H, 'qk_prep')
    (O, lse), (res2, res3) = _flash_fwd(Q, K, kv, H, 'flash_fwd',
                                        [gather(_phase_gather_d2d, G2), gather(_phase_gather_ici, G3)])
    arrived(G2, res2)
    arrived(G3, res3)
    y_mla, (res,) = _mm(O, full['w_mla_o'], 'nn', F32, 'mla_out', [gather(_phase_gather_d2d, G3)])
    arrived(G3, res)
    y = _gate_fwd(z, y_pool, y_mla, lay, 'gate')
    m_mix = _mm(y, full['w_out'], 'nn', F32, 'mix_out')
    h2 = _post_residual(h1, m_mix, norm_mix_post, 1.0, 'mix_res')

    n3 = _norm_fwd(h2, norm_ffn2_pre, 'ffn2_norm')
    gu2 = _mm(n3, full['ffn2_w_gu'], 'nn', BF, 'ffn2_gu')
    (a2,), _ = _swiglu_fwd(gu2, 'ffn2_act')
    f2 = _mm(a2, full['ffn2_w_down'], 'nn', F32, 'ffn2_down')
    h3 = _post_residual(h2, f2, norm_ffn2_post, 0.5, 'ffn2_res')

    G, theirs, sums, halves, reduced = {}, {}, {}, {}, {}
    RA = ['ffn2_w_down', 'ffn2_w_gu']
    RB = ['w_out', 'w_pool_o', 'pool_w', 'w_mla_o', 'w_q_b', 'w_kv_b', 'w_in']
    RC1 = ['ffn1_w_down']
    RC2 = ['ffn1_w_gu']

    def pair_phase(group):
        return _phase_pair([G[n] for n in group], [BIG[n] for n in group], [slab_shape[n] for n in group])

    def pair_sums(group, res):
        for n, t in zip(group, res):
            sums[n] = _pair_sum(G[n], t, BIG[n], slab_shape[n], c_arr, f'rs_pair_sum_{n}')

    def chip_phase(group):
        return _phase_chip([sums[n] for n in group])

    def chip_sums(group, res):
        for n, ld in zip(group, res):
            halves[n] = _chip_sum(sums[n], ld, BIG[n], slab_shape[n], place_arr, f'rs_chip_sum_{n}')

    def final_phase(group):
        return _phase_final([halves[n] for n in group], [BIG[n] for n in group])

    dh3, sq = _loss_bwd(h3, tgt, NM, S, 'loss')

    df2, G['norm_ffn2_post'] = _post_bwd(dh3, f2, norm_ffn2_post, 0.5, 'ffn2_res_bwd')
    G['ffn2_w_down'] = _mm(a2, df2, 'tn', BF, 'ffn2_dw_down')
    da2 = _mm(df2, full['ffn2_w_down'], 'nt', BF, 'ffn2_da')
    dgu2 = _swiglu_bwd(da2, gu2, 'ffn2_act_bwd')
    G['ffn2_w_gu'] = _mm(n3, dgu2, 'tn', BF, 'ffn2_dw_gu')
    dn3, (res,) = _mm(dgu2, full['ffn2_w_gu'], 'nt', F32, 'ffn2_dn', [pair_phase(RA)])
    (dh2, G['norm_ffn2_pre']), _ = _pre_bwd(dh3, dn3, h2, norm_ffn2_pre, 'ffn2_norm_bwd')
    pair_sums(RA, res)

    dm, G['norm_mix_post'] = _post_bwd(dh2, m_mix, norm_mix_post, 1.0, 'mix_res_bwd')
    G['w_out'] = _mm(y, dm, 'tn', BF, 'dw_out')
    dy = _mm(dm, full['w_out'], 'nt', F32, 'mix_out_bwd')
    dy_pool, dy_mla, d_gate = _gate_bwd(dy, z, y_pool, y_mla, lay, 'gate_bwd')
    G['w_pool_o'] = _mm(yp, dy_pool, 'tn', BF, 'dw_pool_o')
    dyp = _mm(dy_pool, full['w_pool_o'], 'nt', F32, 'pool_out_bwd')
    dd, G['pool_scale'], d_pool_w = _pool_mix_bwd(dyp, e_pool, d_pool, full['pool_w'], pool_scale, 'pool_mix_bwd')
    G['pool_w'] = d_pool_w.astype(BF)
    du_pool = _pool_bwd(dd, 'pool_bwd')
    G['w_mla_o'] = _mm(O, dy_mla, 'tn', BF, 'dw_mla_o')
    dO = _mm(dy_mla, full['w_mla_o'], 'nt', BF, 'mla_out_bwd')
    (dQ, dK, dV), (res,) = _flash_bwd(Q, K, kv, O, dO, lse, H, 'flash_bwd', [chip_phase(RA)])
    chip_sums(RA, res)
    dq_raw, dkv, dkr = _qk_prep_bwd(dQ, dK, dV, cos, sin, H, 'qk_prep_bwd')
    d_w_q_pad = _mm(cqn, dq_raw, 'tn', BF, 'dw_q_b')
    G['w_q_b'] = d_w_q_pad.reshape(QL, H, HEAD_PAD)[:, :, :QK_DIM].reshape(QL, H * QK_DIM)
    dcqn = _mm(dq_raw, w_q_pad, 'nt', F32, 'mla_q_bwd')
    G['w_kv_b'] = _mm(ckvn, dkv, 'tn', BF, 'dw_kv_b')
    dckvn = _mm(dkv, full['w_kv_b'], 'nt', F32, 'mla_kv_bwd')
    d_lora, G['q_a_norm'], G['kv_a_norm'] = _lora_norms_bwd(dcqn, dckvn, z, q_a_norm, kv_a_norm, lay, 'lora_norms_bwd')
    n0 = POOL_WIDTH + QL + KVL
    dz = jnp.concatenate([du_pool, d_lora, jnp.zeros((Lp, lay['gp'] - n0), BF), d_gate, dkr], axis=1)
    d_w_in_al, (res,) = _mm(n2, dz, 'tn', BF, 'dw_in', [final_phase(RA)])
    reduced.update(zip(RA, res))
    G['w_in'] = _w_in_logical(d_w_in_al, lay, D).reshape(D, N_CHIPS, in_cols // N_CHIPS).transpose(1, 0, 2)
    dn2, (res,) = _mm(dz, w_in_al, 'nt', F32, 'mix_in_bwd', [pair_phase(RB)])
    (dh1, G['norm_mix_pre']), _ = _pre_bwd(dh2, dn2, h1, norm_mix_pre, 'mix_norm_bwd')
    pair_sums(RB, res)

    df1, G['norm_ffn1_post'] = _post_bwd(dh1, f1, norm_ffn1_post, 0.5, 'ffn1_res_bwd')
    G['ffn1_w_down'] = _mm(a1, df1, 'tn', BF, 'ffn1_dw_down')
    da1, (res,) = _mm(df1, full['ffn1_w_down'], 'nt', BF, 'ffn1_da', [pair_phase(RC1)])
    dgu1 = _swiglu_bwd(da1, gu1, 'ffn1_act_bwd')
    pair_sums(RC1, res)
    G['ffn1_w_gu'], (res_b,) = _mm(n1, dgu1, 'tn', BF, 'ffn1_dw_gu', [chip_phase(RB)])
    chip_sums(RB, res_b)
    (res,) = _run_phases([pair_phase(RC2)], 'rs_pair_exchange_tail')
    pair_sums(RC2, res)
    dn1, (res_c2, res_c1, res_b) = _mm(dgu1, full['ffn1_w_gu'], 'nt', F32, 'ffn1_dn',
                                       [chip_phase(RC2), chip_phase(RC1), final_phase(RB)])
    chip_sums(RC2, res_c2)
    chip_sums(RC1, res_c1)
    reduced.update(zip(RB, res_b))
    (dh0, G['norm_ffn1_pre']), (res_c1, res_c2) = _pre_bwd(dh1, dn1, h0, norm_ffn1_pre, 'ffn1_norm_bwd',
                                                          [final_phase(RC1), final_phase(RC2)])
    reduced.update(zip(RC1, res_c1))
    reduced.update(zip(RC2, res_c2))
    grad_x = dh0[NM:L][None]

    SW = max(D, POOL_WIDTH)

    def widen(a, fill=0.0):
        return jnp.pad(a, ((0, 0), (0, SW - a.shape[1])), constant_values=fill)

    rows = [widen(G[n]) for n in SMALL_VEC] + [widen(dh0[:NM]), widen(sq)]
    n_rows = len(SMALL_VEC) + NM + 1
    pad_rows = -(-n_rows // 8) * 8 - n_rows
    small = _all_reduce_small(jnp.concatenate(rows + [jnp.zeros((pad_rows, SW), F32)], axis=0))
    loss = (0.5 / D) * small[len(SMALL_VEC) + NM, 0]
    for i, n in enumerate(SMALL_VEC):
        reduced[n] = small[i:i + 1, :G[n].shape[1]]
    mw = meta_tokens.shape[1]
    reduced['meta_tokens'] = lax.dynamic_slice(small, (len(SMALL_VEC), chip * mw), (NM, mw))

    grads, deltas, new_m, new_v = {}, {}, {}, {}
    for n in names:
        w = slab[n]
        deltas[n], new_m[n], new_v[n] = [o[None] for o in _adam(w, reduced[n], M[n][0], V[n][0], f'adam_{n}')]
        grads[n] = reduced[n][None]
    n_vec = len(SMALL_VEC)
    vec_w = jnp.concatenate([widen(W[n]) for n in SMALL_VEC] + [jnp.zeros((16 - n_vec, SW), F32)], axis=0)
    vec_m = jnp.concatenate([widen(M[n]) for n in SMALL_VEC] + [jnp.zeros((16 - n_vec, SW), F32)], axis=0)
    vec_v = jnp.concatenate([widen(V[n], 1.0) for n in SMALL_VEC] + [jnp.ones((16 - n_vec, SW), F32)], axis=0)
    vec_g = jnp.concatenate([small[:n_vec], jnp.zeros((16 - n_vec, SW), F32)], axis=0)
    vd, vm, vv = _adam(vec_w, vec_g, vec_m, vec_v, 'adam_vectors')
    for i, n in enumerate(SMALL_VEC):
        wdt = W[n].shape[1]
        grads[n], deltas[n], new_m[n], new_v[n] = reduced[n], vd[i:i + 1, :wdt], vm[i:i + 1, :wdt], vv[i:i + 1, :wdt]
    grads['meta_tokens'] = reduced['meta_tokens']
    deltas['meta_tokens'], new_m['meta_tokens'], new_v['meta_tokens'] = _adam(
        meta_tokens, reduced['meta_tokens'], m_meta_tokens, v_meta_tokens, 'adam_meta')

    return (loss, grad_x, *[grads[n] for n in WEIGHTS], *[deltas[n] for n in WEIGHTS], *[new_m[n] for n in WEIGHTS],
            *[new_v[n] for n in WEIGHTS])
```

```python
import functools

import jax
import jax.numpy as jnp
import numpy as np
from jax import lax
from jax.experimental import pallas as pl
from jax.experimental.pallas import tpu as pltpu

F32 = jnp.float32
BF = jnp.bfloat16
MESH = pl.DeviceIdType.MESH

EPS = 1e-6
N_CHIPS = 4
POOL_WINDOWS = (2, 4, 8, 16)
POOL_GROUP = 256
POOL_WIDTH = POOL_GROUP * len(POOL_WINDOWS)
QK_NOPE = 128
QK_ROPE = 64
V_DIM = 128
QK_DIM = QK_NOPE + QK_ROPE
HEAD_PAD = 256
ROPE_THETA = 10000.0
SOFTMAX_SCALE = QK_DIM ** -0.5
ADAM_LR = 0.001
ADAM_B1 = 0.9
ADAM_B2 = 0.999
ADAM_EPS = 1e-08
ADAM_WD = 0.01
ADAM_STEP = 10
ROW_ALIGN = 256
VMEM_LIMIT = 56 * 1024 * 1024
MM_TK_ROWS = 4352
MM_TK_COLS = 6400
NEG = -1e30
FLASH_CHUNK = 4
WEIGHTS = ['meta_tokens', 'norm_ffn1_pre', 'norm_ffn1_post', 'ffn1_w_gu', 'ffn1_w_down', 'norm_mix_pre',
           'norm_mix_post', 'w_in', 'pool_w', 'pool_scale', 'w_pool_o', 'q_a_norm', 'w_q_b', 'kv_a_norm', 'w_kv_b',
           'w_mla_o', 'w_out', 'norm_ffn2_pre', 'norm_ffn2_post', 'ffn2_w_gu', 'ffn2_w_down']
BIG = {'ffn1_w_gu': 'col', 'ffn1_w_down': 'row', 'w_in': 'lead', 'pool_w': 'pool', 'w_pool_o': 'col', 'w_q_b': 'col',
       'w_kv_b': 'col', 'w_mla_o': 'row', 'w_out': 'row', 'ffn2_w_gu': 'col', 'ffn2_w_down': 'row'}
SMALL_VEC = ['norm_ffn1_pre', 'norm_ffn1_post', 'norm_mix_pre', 'norm_mix_post', 'norm_ffn2_pre', 'norm_ffn2_post',
             'pool_scale', 'q_a_norm', 'kv_a_norm']


def _div_tile(n, target, align):
    best = None
    for t in range(align, min(n, target) + 1, align):
        if n % t == 0:
            best = t
    return best if best is not None else n


def _cp(sem=None):
    return pltpu.CompilerParams(dimension_semantics=sem, vmem_limit_bytes=VMEM_LIMIT)


def _full_shape(kind, slab):
    if kind == 'col':
        return (slab[0], slab[1] * N_CHIPS)
    if kind == 'row':
        return (slab[0] * N_CHIPS, slab[1])
    if kind == 'lead':
        return (N_CHIPS,) + tuple(slab)
    return (slab[0], slab[1] * N_CHIPS, slab[2])


def _half_shape(kind, slab):
    if kind == 'pool':
        return (slab[0], slab[1] // 2, slab[2])
    return (slab[0] // 2, slab[1])


def _half_of_slab(ref, kind, c):
    if kind == 'pool':
        n = ref.shape[1] // 2
        return ref.at[:, pl.ds(c * n, n), :]
    n = ref.shape[0] // 2
    return ref.at[pl.ds(c * n, n), :]


def _piece(ref, kind, k, c):
    if kind == 'col':
        r, w = ref.shape[0] // 2, ref.shape[1] // N_CHIPS
        return ref.at[pl.ds(c * r, r), pl.ds(k * w, w)]
    if kind == 'row':
        r = ref.shape[0] // (2 * N_CHIPS)
        return ref.at[pl.ds((2 * k + c) * r, r), :]
    if kind == 'lead':
        r = ref.shape[1] // 2
        return ref.at[k, pl.ds(c * r, r), :]
    r = ref.shape[1] // (2 * N_CHIPS)
    return ref.at[:, pl.ds((2 * k + c) * r, r), :]


def _slab(ref, kind, k):
    if kind == 'col':
        w = ref.shape[1] // N_CHIPS
        return ref.at[:, pl.ds(k * w, w)]
    if kind == 'row':
        r = ref.shape[0] // N_CHIPS
        return ref.at[pl.ds(k * r, r), :]
    if kind == 'lead':
        return ref.at[k]
    r = ref.shape[1] // N_CHIPS
    return ref.at[:, pl.ds(k * r, r), :]


def _place():
    x, y, c = lax.axis_index('x'), lax.axis_index('y'), lax.axis_index('c')
    return x, y, c


def _peer_chip(x, y, j):
    px = 1 - x if (j >> 1) else x
    py = 1 - y if (j & 1) else y
    return px, py


ANY = pl.BlockSpec(memory_space=pl.ANY)


def _all_gather(bufs, kinds):
    n = len(bufs)

    def body(*refs):
        outs = refs[n:2 * n]
        ssem, rsem, fssem, frsem = refs[2 * n:]
        x, y, c = _place()
        me = 2 * x + y
        sib = (x, y, 1 - c)
        sends = []
        for w in range(n):
            for j in (1, 2, 3):
                px, py = _peer_chip(x, y, j)
                mine = _piece(outs[w], kinds[w], me, c)
                sends.append(pltpu.make_async_remote_copy(
                    src_ref=mine, dst_ref=mine, send_sem=ssem.at[w, j - 1], recv_sem=rsem.at[w, j - 1],
                    device_id=(px, py, c), device_id_type=MESH))
        for cp in sends:
            cp.start()
        fwds = []
        for w in range(n):
            for j in (1, 2, 3):
                px, py = _peer_chip(x, y, j)
                got = _piece(outs[w], kinds[w], 2 * px + py, c)
                pltpu.make_async_remote_copy(src_ref=got, dst_ref=got, send_sem=ssem.at[w, j - 1],
                                             recv_sem=rsem.at[w, j - 1], device_id=(px, py, c),
                                             device_id_type=MESH).wait_recv()
                fwd = pltpu.make_async_remote_copy(src_ref=got, dst_ref=got, send_sem=fssem.at[w, j - 1],
                                                   recv_sem=frsem.at[w, j - 1], device_id=sib, device_id_type=MESH)
                fwd.start()
                fwds.append(fwd)
        for w in range(n):
            for j in (1, 2, 3):
                px, py = _peer_chip(x, y, j)
                other = _piece(outs[w], kinds[w], 2 * px + py, 1 - c)
                pltpu.make_async_remote_copy(src_ref=other, dst_ref=other, send_sem=fssem.at[w, j - 1],
                                             recv_sem=frsem.at[w, j - 1], device_id=sib,
                                             device_id_type=MESH).wait_recv()
        for cp in sends + fwds:
            cp.wait_send()

    return pl.pallas_call(
        body, name='all_gather_weights', out_shape=[jax.ShapeDtypeStruct(b.shape, b.dtype) for b in bufs],
        in_specs=[ANY] * n, out_specs=[ANY] * n, input_output_aliases={w: w for w in range(n)},
        scratch_shapes=[pltpu.SemaphoreType.DMA((n, 3)), pltpu.SemaphoreType.DMA((n, 3)),
                        pltpu.SemaphoreType.DMA((n, 3)), pltpu.SemaphoreType.DMA((n, 3))],
    )(*bufs)


def _remote(src, dst, ssem, rsem, k, dev):
    return pltpu.make_async_remote_copy(src_ref=src, dst_ref=dst, send_sem=ssem.at[k], recv_sem=rsem.at[k],
                                        device_id=dev, device_id_type=MESH)


def _phase_gather_ici(bufs, kinds):
    n = len(bufs)

    def build(ins, outs, ssem, rsem):
        x, y, c = _place()
        me = 2 * x + y
        ds = []
        for w in range(n):
            for j in (1, 2, 3):
                px, py = _peer_chip(x, y, j)
                mine = _piece(outs[w], kinds[w], me, c)
                got = _piece(outs[w], kinds[w], 2 * px + py, c)
                k = 3 * w + j - 1
                ds.append((_remote(mine, mine, ssem, rsem, k, (px, py, c)), _remote(got, got, ssem, rsem, k, (px, py, c))))
        return ds

    return dict(ins=list(bufs), outs=[jax.ShapeDtypeStruct(b.shape, b.dtype) for b in bufs],
                alias={w: w for w in range(n)}, nsem=3 * n, build=build)


def _phase_gather_d2d(bufs, kinds):
    n = len(bufs)

    def build(ins, outs, ssem, rsem):
        x, y, c = _place()
        sib = (x, y, 1 - c)
        ds = []
        for w in range(n):
            for j in (1, 2, 3):
                px, py = _peer_chip(x, y, j)
                have = _piece(outs[w], kinds[w], 2 * px + py, c)
                want = _piece(outs[w], kinds[w], 2 * px + py, 1 - c)
                k = 3 * w + j - 1
                ds.append((_remote(have, have, ssem, rsem, k, sib), _remote(want, want, ssem, rsem, k, sib)))
        return ds

    return dict(ins=list(bufs), outs=[jax.ShapeDtypeStruct(b.shape, b.dtype) for b in bufs],
                alias={w: w for w in range(n)}, nsem=3 * n, build=build)


def _phase_pair(grads, kinds, slabs):
    n = len(grads)

    def build(ins, outs, ssem, rsem):
        x, y, c = _place()
        sib = (x, y, 1 - c)
        ds = []
        for w in range(n):
            for k in range(N_CHIPS):
                cp = _remote(_piece(ins[w], kinds[w], k, 1 - c), outs[w].at[k], ssem, rsem, N_CHIPS * w + k, sib)
                ds.append((cp, cp))
        return ds

    return dict(ins=list(grads), alias={}, nsem=N_CHIPS * n, build=build,
                outs=[jax.ShapeDtypeStruct((N_CHIPS,) + _half_shape(k, s), g.dtype)
                      for g, k, s in zip(grads, kinds, slabs)])


def _phase_chip(sums):
    n = len(sums)

    def build(ins, outs, ssem, rsem):
        x, y, c = _place()
        ds = []
        for w in range(n):
            for j in (1, 2, 3):
                px, py = _peer_chip(x, y, j)
                cp = _remote(ins[w].at[2 * px + py], outs[w].at[j - 1], ssem, rsem, 3 * w + j - 1, (px, py, c))
                ds.append((cp, cp))
        return ds

    return dict(ins=list(sums), alias={}, nsem=3 * n, build=build,
                outs=[jax.ShapeDtypeStruct((3,) + s.shape[1:], s.dtype) for s in sums])


def _phase_final(slabs_half, kinds):
    n = len(slabs_half)

    def build(ins, outs, ssem, rsem):
        x, y, c = _place()
        sib = (x, y, 1 - c)
        ds = []
        for w in range(n):
            mine = _half_of_slab(outs[w], kinds[w], c)
            other = _half_of_slab(outs[w], kinds[w], 1 - c)
            ds.append((_remote(mine, mine, ssem, rsem, w, sib), _remote(other, other, ssem, rsem, w, sib)))
        return ds

    return dict(ins=list(slabs_half), outs=[jax.ShapeDtypeStruct(s.shape, s.dtype) for s in slabs_half],
                alias={w: w for w in range(n)}, nsem=n, build=build)


def _phase_operands(phases, n_main_in, n_main_out):
    ins, outs, alias, sems = [], [], {}, []
    for ph in phases:
        for i, o in ph['alias'].items():
            alias[n_main_in + len(ins) + i] = n_main_out + len(outs) + o
        ins += ph['ins']
        outs += ph['outs']
        sems += [pltpu.SemaphoreType.DMA((ph['nsem'],)), pltpu.SemaphoreType.DMA((ph['nsem'],))]
    return ins, outs, alias, sems


def _phase_copies(phases, in_refs, out_refs, sem_refs):
    ds, a, b = [], 0, 0
    for p, ph in enumerate(phases):
        ds += ph['build'](in_refs[a:a + len(ph['ins'])], out_refs[b:b + len(ph['outs'])], sem_refs[2 * p],
                          sem_refs[2 * p + 1])
        a += len(ph['ins'])
        b += len(ph['outs'])
    return ds


def _phase_start(phases, in_refs, out_refs, sem_refs):
    for send, _ in _phase_copies(phases, in_refs, out_refs, sem_refs):
        send.start()


def _phase_finish(phases, in_refs, out_refs, sem_refs):
    ds = _phase_copies(phases, in_refs, out_refs, sem_refs)
    for _, recv in ds:
        recv.wait_recv()
    for send, _ in ds:
        send.wait_send()


def _phase_results(phases, outs):
    res, b = [], 0
    for ph in phases:
        res.append(list(outs[b:b + len(ph['outs'])]))
        b += len(ph['outs'])
    return res


def _run_phases(phases, name):
    ins, out_shapes, alias, sems = _phase_operands(phases, 0, 0)
    n_in, n_out = len(ins), len(out_shapes)

    def body(*refs):
        in_refs, out_refs, sem_refs = refs[:n_in], refs[n_in:n_in + n_out], refs[n_in + n_out:]
        _phase_start(phases, in_refs, out_refs, sem_refs)
        _phase_finish(phases, in_refs, out_refs, sem_refs)

    outs = pl.pallas_call(body, name=name, out_shape=out_shapes, in_specs=[ANY] * n_in, out_specs=[ANY] * n_out,
                          input_output_aliases=alias, scratch_shapes=sems)(*ins)
    return _phase_results(phases, outs)


def _cast_place(shard, kind, place_arr, dtype, name):
    full = _full_shape(kind, shard.shape)
    if kind == 'pool':
        g, r, w = shard.shape
        grid = (g,)
        src = pl.BlockSpec((None, r, w), lambda i, s: (i, 0, 0))
        dst = pl.BlockSpec((None, r, w), lambda i, s: (i, s[1], 0))
    else:
        r, w = shard.shape
        tr = _div_tile(r, max(16, (1 << 19) // w), 16)
        nrb = r // tr
        grid = (nrb,)
        src = pl.BlockSpec((tr, w), lambda i, s: (i, 0))
        if kind == 'col':
            dst = pl.BlockSpec((tr, w), lambda i, s: (i, s[1]))
        elif kind == 'row':
            dst = pl.BlockSpec((tr, w), lambda i, s: (s[1] * nrb + i, 0))
        else:
            dst = pl.BlockSpec((None, tr, w), lambda i, s: (s[1], i, 0))

    def body(s_ref, x_ref, o_ref):
        o_ref[...] = x_ref[...].astype(o_ref.dtype)

    return pl.pallas_call(
        body, name=name,
        grid_spec=pltpu.PrefetchScalarGridSpec(num_scalar_prefetch=1, grid=grid, in_specs=[src], out_specs=dst),
        out_shape=jax.ShapeDtypeStruct(full, dtype), compiler_params=_cp(('parallel',)),
    )(place_arr, shard)


def _half_blocks(kind, slab):
    hs = _half_shape(kind, slab)
    if kind == 'pool':
        return (hs[0],), hs[1], 1
    tr = _div_tile(hs[0], max(16, (1 << 19) // hs[1]), 16)
    return (hs[0] // tr,), tr, hs[0] // tr


def _pair_sum(grad, theirs, kind, slab, c_arr, name):
    hs = _half_shape(kind, slab)
    (n_i,), tr, nrb = _half_blocks(kind, slab)
    if kind == 'pool':
        own = pl.BlockSpec((None, hs[1], hs[2]), lambda k, i, s: (i, 2 * k + s[0], 0))
        stk = pl.BlockSpec((None, None, hs[1], hs[2]), lambda k, i, s: (k, i, 0, 0))
    else:
        if kind == 'col':
            own = pl.BlockSpec((tr, hs[1]), lambda k, i, s: (s[0] * nrb + i, k))
        elif kind == 'row':
            own = pl.BlockSpec((tr, hs[1]), lambda k, i, s: ((2 * k + s[0]) * nrb + i, 0))
        else:
            own = pl.BlockSpec((None, tr, hs[1]), lambda k, i, s: (k, s[0] * nrb + i, 0))
        stk = pl.BlockSpec((None, tr, hs[1]), lambda k, i, s: (k, i, 0))

    def body(s_ref, a_ref, b_ref, o_ref):
        o_ref[...] = (a_ref[...].astype(F32) + b_ref[...].astype(F32)).astype(o_ref.dtype)

    return pl.pallas_call(
        body, name=name,
        grid_spec=pltpu.PrefetchScalarGridSpec(num_scalar_prefetch=1, grid=(N_CHIPS, n_i), in_specs=[own, stk],
                                               out_specs=stk),
        out_shape=jax.ShapeDtypeStruct((N_CHIPS,) + hs, BF), compiler_params=_cp(('parallel', 'parallel')),
    )(c_arr, grad, theirs)


def _chip_sum(sums, landed, kind, slab, place_arr, name):
    hs = _half_shape(kind, slab)
    (n_i,), tr, nrb = _half_blocks(kind, slab)
    if kind == 'pool':
        blk = (None, None, hs[1], hs[2])
        mine = pl.BlockSpec(blk, lambda i, s: (s[1], i, 0, 0))
        land = [pl.BlockSpec(blk, lambda i, s, j=j: (j, i, 0, 0)) for j in range(3)]
        out = pl.BlockSpec((None, hs[1], hs[2]), lambda i, s: (i, s[0], 0))
    else:
        blk = (None, tr, hs[1])
        mine = pl.BlockSpec(blk, lambda i, s: (s[1], i, 0))
        land = [pl.BlockSpec(blk, lambda i, s, j=j: (j, i, 0)) for j in range(3)]
        out = pl.BlockSpec((tr, hs[1]), lambda i, s: (s[0] * nrb + i, 0))

    def body(s_ref, a_ref, b_ref, c_ref, d_ref, o_ref):
        o_ref[...] = ((a_ref[...].astype(F32) + b_ref[...].astype(F32)) + c_ref[...].astype(F32)) + d_ref[...].astype(F32)

    return pl.pallas_call(
        body, name=name,
        grid_spec=pltpu.PrefetchScalarGridSpec(num_scalar_prefetch=1, grid=(n_i,), in_specs=[mine] + land, out_specs=out),
        out_shape=jax.ShapeDtypeStruct(tuple(slab), F32), compiler_params=_cp(('parallel',)),
    )(place_arr, sums, landed, landed, landed)


def _all_reduce_small(buf):
    rows, cols = buf.shape

    def body(in_ref, out_ref, land, ssem, rsem):
        x, y, c = _place()
        me = 4 * x + 2 * y + c
        land[me] = in_ref[...]
        started = []
        for j in range(1, 8):
            px = 1 - x if (j >> 2) & 1 else x
            py = 1 - y if (j >> 1) & 1 else y
            pc = 1 - c if j & 1 else c
            cp = pltpu.make_async_remote_copy(src_ref=in_ref, dst_ref=land.at[me], send_sem=ssem.at[j - 1],
                                              recv_sem=rsem.at[j - 1], device_id=(px, py, pc), device_id_type=MESH)
            cp.start()
            started.append(cp)
        for j in range(1, 8):
            px = 1 - x if (j >> 2) & 1 else x
            py = 1 - y if (j >> 1) & 1 else y
            pc = 1 - c if j & 1 else c
            slot = land.at[4 * px + 2 * py + pc]
            pltpu.make_async_remote_copy(src_ref=slot, dst_ref=slot, send_sem=ssem.at[j - 1], recv_sem=rsem.at[j - 1],
                                         device_id=(px, py, pc), device_id_type=MESH).wait_recv()
        for cp in started:
            cp.wait_send()
        acc = land[0]
        for d in range(1, 8):
            acc = acc + land[d]
        out_ref[...] = acc

    return pl.pallas_call(
        body, name='all_reduce_small', out_shape=jax.ShapeDtypeStruct((rows, cols), F32),
        in_specs=[pl.BlockSpec(memory_space=pltpu.VMEM)], out_specs=pl.BlockSpec(memory_space=pltpu.VMEM),
        scratch_shapes=[pltpu.VMEM((8, rows, cols), F32), pltpu.SemaphoreType.DMA((7,)), pltpu.SemaphoreType.DMA((7,))],
    )(buf)


def _elementwise(fn, ins, lead_index, out_shape, out_dtypes, name):
    nd = len(out_shape)
    r, cdim = out_shape[-2], out_shape[-1]
    tr = _div_tile(r, max(16, (1 << 19) // cdim), 16)
    grid = tuple(out_shape[:-2]) + (r // tr,)
    block = (None,) * (nd - 2) + (tr, cdim)

    def spec(lead):
        if lead is None:
            return pl.BlockSpec(block, lambda *g: tuple(g) + (0,))
        return pl.BlockSpec((None,) + block, lambda *g, lead=lead: (lead,) + tuple(g) + (0,))

    n_in = len(ins)

    def body(*refs):
        res = fn(*[r_[...] for r_ in refs[:n_in]])
        for o_ref, v in zip(refs[n_in:], res):
            o_ref[...] = v.astype(o_ref.dtype)

    return pl.pallas_call(
        body, name=name, grid=grid, in_specs=[spec(l) for l in lead_index],
        out_specs=[spec(None) for _ in out_dtypes],
        out_shape=[jax.ShapeDtypeStruct(tuple(out_shape), dt) for dt in out_dtypes],
        compiler_params=_cp(('parallel',) * len(grid)),
    )(*ins)


def _adam_fn(w, g, m, v):
    m = ADAM_B1 * m + (1.0 - ADAM_B1) * g
    v = ADAM_B2 * v + (1.0 - ADAM_B2) * (g * g)
    m_hat = m / (1.0 - ADAM_B1 ** ADAM_STEP)
    v_hat = v / (1.0 - ADAM_B2 ** ADAM_STEP)
    delta = -ADAM_LR * (m_hat / (jnp.sqrt(v_hat) + ADAM_EPS) + ADAM_WD * w)
    return delta, m, v


def _adam(w, g, m, v, name):
    return _elementwise(_adam_fn, [w, g, m, v], [None] * 4, w.shape, [F32] * 3, name)


_DIMS = {'nn': (((1,), (0,)), ((), ())), 'nt': (((1,), (1,)), ((), ())), 'tn': (((0,), (0,)), ((), ()))}


def _mm(a, b, mode, out_dtype, name, phases=()):
    if mode == 'nn':
        (M, K), N = a.shape, b.shape[1]
    elif mode == 'nt':
        (M, K), N = a.shape, b.shape[0]
    else:
        (K, M), N = a.shape, b.shape[1]
    if mode == 'tn':
        tm, tn, tk = _div_tile(M, 512, 128), _div_tile(N, 1024, 128), _div_tile(K, MM_TK_ROWS, 16)
    else:
        tk = _div_tile(K, MM_TK_COLS, 128)
        tm, tn = _div_tile(M, 1088, 16), _div_tile(N, 1024 if tk <= 2816 else 512, 128)
    nk = K // tk
    a_spec = {'nn': pl.BlockSpec((tm, tk), lambda i, j, k: (i, k)), 'nt': pl.BlockSpec((tm, tk), lambda i, j, k: (i, k)),
              'tn': pl.BlockSpec((tk, tm), lambda i, j, k: (k, i))}[mode]
    b_spec = {'nn': pl.BlockSpec((tk, tn), lambda i, j, k: (k, j)), 'nt': pl.BlockSpec((tn, tk), lambda i, j, k: (j, k)),
              'tn': pl.BlockSpec((tk, tn), lambda i, j, k: (k, j))}[mode]
    dims = _DIMS[mode]
    gm, gn = M // tm, N // tn
    extra_in, extra_out, alias, sems = _phase_operands(phases, 2, 1)
    n_ei, n_eo = len(extra_in), len(extra_out)

    def body(*refs):
        a_ref, b_ref, ein = refs[0], refs[1], refs[2:2 + n_ei]
        o_ref, eout = refs[2 + n_ei], refs[3 + n_ei:3 + n_ei + n_eo]
        acc_ref, sem_refs = refs[3 + n_ei + n_eo], refs[4 + n_ei + n_eo:]
        i, j, k = pl.program_id(0), pl.program_id(1), pl.program_id(2)
        if phases:
            @pl.when((i == 0) & (j == 0) & (k == 0))
            def _():
                _phase_start(phases, ein, eout, sem_refs)

        part = lax.dot_general(a_ref[...], b_ref[...], dims, preferred_element_type=F32)
        if nk == 1:
            o_ref[...] = part.astype(o_ref.dtype)
        else:
            @pl.when(k == 0)
            def _():
                acc_ref[...] = part

            @pl.when(k > 0)
            def _():
                acc_ref[...] += part

            @pl.when(k == nk - 1)
            def _():
                o_ref[...] = acc_ref[...].astype(o_ref.dtype)

        if phases:
            @pl.when((i == gm - 1) & (j == gn - 1) & (k == nk - 1))
            def _():
                _phase_finish(phases, ein, eout, sem_refs)

    outs = pl.pallas_call(
        body, name=name, grid=(gm, gn, nk), in_specs=[a_spec, b_spec] + [ANY] * n_ei,
        out_specs=[pl.BlockSpec((tm, tn), lambda i, j, k: (i, j))] + [ANY] * n_eo,
        out_shape=[jax.ShapeDtypeStruct((M, N), out_dtype)] + extra_out, input_output_aliases=alias,
        scratch_shapes=[pltpu.VMEM((tm, tn) if nk > 1 else (8, 128), F32)] + sems,
        compiler_params=_cp(('arbitrary',) * 3 if phases else ('parallel', 'parallel', 'arbitrary')),
    )(a, b, *extra_in)
    if phases:
        return outs[0], _phase_results(phases, outs[1:])
    return outs[0]


def _rb(tm, w, col=0):
    return pl.BlockSpec((tm, w), lambda i, col=col: (i, col))


def _fixed(shape):
    return pl.BlockSpec(shape, lambda i: (0,) * len(shape))


def _rms(x):
    return lax.rsqrt(jnp.mean(x * x, axis=-1, keepdims=True) + EPS)


def _norm_fwd(x, gain, name, width=None, col=0):
    Lp = x.shape[0]
    width = width or x.shape[1]
    tm = _div_tile(Lp, 256, 16)

    def body(x_ref, g_ref, o_ref):
        v = x_ref[...]
        o_ref[...] = (v * _rms(v) * g_ref[...]).astype(o_ref.dtype)

    return pl.pallas_call(
        body, name=name, grid=(Lp // tm,), in_specs=[_rb(tm, width, col), _fixed((1, width))], out_specs=_rb(tm, width),
        out_shape=jax.ShapeDtypeStruct((Lp, width), BF), compiler_params=_cp(('parallel',)),
    )(x, gain)


def _post_residual(h, f, gain, scale, name, phases=()):
    Lp, D = h.shape
    tm = _div_tile(Lp, 256, 16)

    def body(h_ref, f_ref, g_ref, o_ref):
        v = f_ref[...]
        o_ref[...] = h_ref[...] + scale * (v * _rms(v) * g_ref[...])

    (out,), res = _call_carrying(body, name, (Lp // tm,), [_rb(tm, D), _rb(tm, D), _fixed((1, D))], [_rb(tm, D)],
                                 [jax.ShapeDtypeStruct((Lp, D), F32)], [], (h, f, gain), phases)
    return out, res


def _post_bwd(dh, f, gain, scale, name):
    Lp, D = dh.shape
    tm = _div_tile(Lp, 256, 16)

    def body(dh_ref, f_ref, g_ref, df_ref, dg_ref):
        v = f_ref[...]
        r = _rms(v)
        dy = scale * dh_ref[...]
        w = dy * g_ref[...]
        df_ref[...] = (r * w - v * (r * r * r) * jnp.mean(w * v, axis=-1, keepdims=True)).astype(df_ref.dtype)
        part = jnp.sum(dy * v * r, axis=0, keepdims=True)

        @pl.when(pl.program_id(0) == 0)
        def _():
            dg_ref[...] = part

        @pl.when(pl.program_id(0) > 0)
        def _():
            dg_ref[...] += part

    return pl.pallas_call(
        body, name=name, grid=(Lp // tm,), in_specs=[_rb(tm, D), _rb(tm, D), _fixed((1, D))],
        out_specs=[_rb(tm, D), _fixed((1, D))],
        out_shape=[jax.ShapeDtypeStruct((Lp, D), BF), jax.ShapeDtypeStruct((1, D), F32)],
        compiler_params=_cp(('arbitrary',)),
    )(dh, f, gain)


def _pre_bwd(dres, dn, h, gain, name, phases=()):
    Lp, D = h.shape
    tm = _div_tile(Lp, 256, 16)

    def body(dres_ref, dn_ref, h_ref, g_ref, dh_ref, dg_ref):
        v = h_ref[...]
        r = _rms(v)
        dy = dn_ref[...]
        w = dy * g_ref[...]
        dh_ref[...] = dres_ref[...] + r * w - v * (r * r * r) * jnp.mean(w * v, axis=-1, keepdims=True)
        part = jnp.sum(dy * v * r, axis=0, keepdims=True)

        @pl.when(pl.program_id(0) == 0)
        def _():
            dg_ref[...] = part

        @pl.when(pl.program_id(0) > 0)
        def _():
            dg_ref[...] += part

    return _call_carrying(
        body, name, (Lp // tm,), [_rb(tm, D), _rb(tm, D), _rb(tm, D), _fixed((1, D))], [_rb(tm, D), _fixed((1, D))],
        [jax.ShapeDtypeStruct((Lp, D), F32), jax.ShapeDtypeStruct((1, D), F32)], [], (dres, dn, h, gain), phases)


def _swiglu_fwd(gu, name, phases=()):
    Lp, F2 = gu.shape
    F = F2 // 2
    tm = _div_tile(Lp, 256, 16)

    def body(gu_ref, a_ref):
        g = gu_ref[:, :F].astype(F32)
        u = gu_ref[:, F:].astype(F32)
        a_ref[...] = (g * jax.nn.sigmoid(g) * u).astype(a_ref.dtype)

    return _call_carrying(body, name, (Lp // tm,), [_rb(tm, F2)], [_rb(tm, F)],
                          [jax.ShapeDtypeStruct((Lp, F), BF)], [], (gu,), phases)


def _swiglu_bwd(da, gu, name):
    Lp, F2 = gu.shape
    F = F2 // 2
    tm = _div_tile(Lp, 256, 16)

    def body(da_ref, gu_ref, o_ref):
        g = gu_ref[:, :F].astype(F32)
        u = gu_ref[:, F:].astype(F32)
        da_ = da_ref[...].astype(F32)
        s = jax.nn.sigmoid(g)
        o_ref[:, :F] = (da_ * u * (s * (1.0 + g * (1.0 - s)))).astype(o_ref.dtype)
        o_ref[:, F:] = (da_ * (g * s)).astype(o_ref.dtype)

    return pl.pallas_call(
        body, name=name, grid=(Lp // tm,), in_specs=[_rb(tm, F), _rb(tm, F2)], out_specs=_rb(tm, F2),
        out_shape=jax.ShapeDtypeStruct((Lp, F2), BF), compiler_params=_cp(('parallel',)),
    )(da, gu)


def _loss_bwd(h, tgt, n_meta, n_real, name):
    Lp, D = h.shape
    tm = _div_tile(Lp, 256, 16)

    def body(h_ref, t_ref, d_ref, l_ref):
        row = lax.broadcasted_iota(jnp.int32, (tm, 1), 0) + pl.program_id(0) * tm
        ok = (row >= n_meta) & (row < n_meta + n_real)
        err = jnp.where(ok, h_ref[...] - t_ref[...], 0.0)
        d_ref[...] = err / D
        part = jnp.full((1, 128), jnp.sum(err * err), F32)

        @pl.when(pl.program_id(0) == 0)
        def _():
            l_ref[...] = part

        @pl.when(pl.program_id(0) > 0)
        def _():
            l_ref[...] += part

    return pl.pallas_call(
        body, name=name, grid=(Lp // tm,), in_specs=[_rb(tm, D), _rb(tm, D)], out_specs=[_rb(tm, D), _fixed((1, 128))],
        out_shape=[jax.ShapeDtypeStruct((Lp, D), F32), jax.ShapeDtypeStruct((1, 128), F32)],
        compiler_params=_cp(('arbitrary',)),
    )(h, tgt)


def _split_bf16(v):
    hi = v.astype(BF)
    return hi, (v - hi.astype(F32)).astype(BF)


def _pool_fwd(z, name):
    Lp = z.shape[0]
    T = _div_tile(Lp, 256, 16)
    G = len(POOL_WINDOWS)

    def body(cur_ref, prev_ref, d_ref):
        i, g = pl.program_id(0), pl.program_id(1)
        w = jnp.left_shift(2, g)
        rr = lax.broadcasted_iota(jnp.int32, (T, T), 0)
        cc = lax.broadcasted_iota(jnp.int32, (T, T), 1)
        b_cur = jnp.where((cc <= rr) & (cc > rr - w), 1.0, 0.0).astype(BF)
        w_prev = jnp.where(i > 0, w, 0)
        b_prev = jnp.where(cc - T > rr - w_prev, 1.0, 0.0).astype(BF)
        u = cur_ref[...]
        s = jnp.zeros((T, POOL_GROUP), F32)
        for part in _split_bf16(u):
            s += jnp.dot(b_cur, part, preferred_element_type=F32)
        for part in _split_bf16(prev_ref[...]):
            s += jnp.dot(b_prev, part, preferred_element_type=F32)
        t = lax.broadcasted_iota(jnp.int32, (T, 1), 0) + i * T
        cnt = jnp.minimum(w, t + 1).astype(F32)
        d_ref[...] = (s / cnt - u).astype(d_ref.dtype)

    return pl.pallas_call(
        body, name=name, grid=(Lp // T, G),
        in_specs=[pl.BlockSpec((T, POOL_GROUP), lambda i, g: (i, g)),
                  pl.BlockSpec((T, POOL_GROUP), lambda i, g: (jnp.maximum(i - 1, 0), g))],
        out_specs=pl.BlockSpec((T, POOL_GROUP), lambda i, g: (i, g)),
        out_shape=jax.ShapeDtypeStruct((Lp, POOL_WIDTH), BF), compiler_params=_cp(('parallel', 'parallel')),
    )(z, z)


def _pool_bwd(dd, name):
    Lp = dd.shape[0]
    T = _div_tile(Lp, 256, 16)
    G = len(POOL_WINDOWS)
    n_t = Lp // T

    def body(cur_ref, next_ref, o_ref):
        i, g = pl.program_id(0), pl.program_id(1)
        w = jnp.left_shift(2, g)
        rr = lax.broadcasted_iota(jnp.int32, (T, T), 0)
        cc = lax.broadcasted_iota(jnp.int32, (T, T), 1)
        b_cur = jnp.where((cc >= rr) & (cc < rr + w), 1.0, 0.0).astype(BF)
        w_next = jnp.where(i < n_t - 1, w, 0)
        b_next = jnp.where(cc + T < rr + w_next, 1.0, 0.0).astype(BF)
        t = lax.broadcasted_iota(jnp.int32, (T, 1), 0) + i * T
        cur = cur_ref[...]
        e_cur = cur / jnp.minimum(w, t + 1).astype(F32)
        e_next = next_ref[...] / jnp.minimum(w, t + T + 1).astype(F32)
        s = jnp.zeros((T, POOL_GROUP), F32)
        for part in _split_bf16(e_cur):
            s += jnp.dot(b_cur, part, preferred_element_type=F32)
        for part in _split_bf16(e_next):
            s += jnp.dot(b_next, part, preferred_element_type=F32)
        o_ref[...] = (s - cur).astype(o_ref.dtype)

    return pl.pallas_call(
        body, name=name, grid=(n_t, G),
        in_specs=[pl.BlockSpec((T, POOL_GROUP), lambda i, g: (i, g)),
                  pl.BlockSpec((T, POOL_GROUP), lambda i, g: (jnp.minimum(i + 1, n_t - 1), g))],
        out_specs=pl.BlockSpec((T, POOL_GROUP), lambda i, g: (i, g)),
        out_shape=jax.ShapeDtypeStruct((Lp, POOL_WIDTH), BF), compiler_params=_cp(('parallel', 'parallel')),
    )(dd, dd)


def _pool_mix_fwd(d, pool_w, scale, name):
    Lp = d.shape[0]
    G = len(POOL_WINDOWS)
    tm = _div_tile(Lp, 1088, 16)

    def body(d_ref, w_ref, s_ref, e_ref, y_ref):
        e = jnp.dot(d_ref[...], w_ref[...], preferred_element_type=F32)
        e_ref[...] = e.astype(e_ref.dtype)
        y_ref[...] = (e * s_ref[...]).astype(y_ref.dtype)

    blk = pl.BlockSpec((tm, POOL_GROUP), lambda g, i: (i, g))
    return pl.pallas_call(
        body, name=name, grid=(G, Lp // tm),
        in_specs=[blk, pl.BlockSpec((None, POOL_GROUP, POOL_GROUP), lambda g, i: (g, 0, 0)),
                  pl.BlockSpec((1, POOL_GROUP), lambda g, i: (0, g))],
        out_specs=[blk, blk], out_shape=[jax.ShapeDtypeStruct((Lp, POOL_WIDTH), BF)] * 2,
        compiler_params=_cp(('parallel', 'parallel')),
    )(d, pool_w, scale)


def _pool_mix_bwd(dyp, e, d, pool_w, scale, name):
    Lp = d.shape[0]
    G = len(POOL_WINDOWS)
    tm = _div_tile(Lp, 1088, 16)

    def body(dy_ref, e_ref, d_ref, w_ref, s_ref, dd_ref, ds_ref, dw_ref):
        i = pl.program_id(1)
        dy = dy_ref[...]
        de = (dy * s_ref[...]).astype(BF)
        dd_ref[...] = lax.dot_general(de, w_ref[...], _DIMS['nt'], preferred_element_type=F32)
        ds_part = jnp.sum(dy * e_ref[...].astype(F32), axis=0, keepdims=True)
        dw_part = lax.dot_general(d_ref[...], de, _DIMS['tn'], preferred_element_type=F32)

        @pl.when(i == 0)
        def _():
            ds_ref[...] = ds_part
            dw_ref[...] = dw_part

        @pl.when(i > 0)
        def _():
            ds_ref[...] += ds_part
            dw_ref[...] += dw_part

    blk = pl.BlockSpec((tm, POOL_GROUP), lambda g, i: (i, g))
    wblk = pl.BlockSpec((None, POOL_GROUP, POOL_GROUP), lambda g, i: (g, 0, 0))
    sblk = pl.BlockSpec((1, POOL_GROUP), lambda g, i: (0, g))
    return pl.pallas_call(
        body, name=name, grid=(G, Lp // tm), in_specs=[blk, blk, blk, wblk, sblk], out_specs=[blk, sblk, wblk],
        out_shape=[jax.ShapeDtypeStruct((Lp, POOL_WIDTH), F32), jax.ShapeDtypeStruct((1, POOL_WIDTH), F32),
                   jax.ShapeDtypeStruct((G, POOL_GROUP, POOL_GROUP), F32)],
        compiler_params=_cp(('parallel', 'arbitrary')),
    )(dyp, e, d, pool_w, scale)


def _rot_half(t):
    lane = lax.broadcasted_iota(jnp.int32, t.shape, 1)
    half = QK_ROPE // 2
    return jnp.where(lane < half, -pltpu.roll(t, 128 - half, 1), pltpu.roll(t, half, 1))


def _lora_norms(z, q_gain, kv_gain, lay, name):
    Lp = z.shape[0]
    QL, KVL = lay['QL'], lay['KVL']
    tm = _div_tile(Lp, 256, 16)

    def body(q_ref, kv_ref, qg_ref, kg_ref, qo_ref, ko_ref):
        a = q_ref[...]
        qo_ref[...] = (a * _rms(a) * qg_ref[...]).astype(BF)
        b = kv_ref[...]
        ko_ref[...] = (b * _rms(b) * kg_ref[...]).astype(BF)

    return pl.pallas_call(
        body, name=name, grid=(Lp // tm,),
        in_specs=[_rb(tm, QL, lay['cq'] // QL), _rb(tm, KVL, lay['ckv'] // KVL), _fixed((1, QL)), _fixed((1, KVL))],
        out_specs=[_rb(tm, QL), _rb(tm, KVL)],
        out_shape=[jax.ShapeDtypeStruct((Lp, QL), BF), jax.ShapeDtypeStruct((Lp, KVL), BF)],
        compiler_params=_cp(('parallel',)),
    )(z, z, q_gain, kv_gain)


def _lora_norms_bwd(dqn, dkn, z, q_gain, kv_gain, lay, name):
    Lp = z.shape[0]
    QL, KVL = lay['QL'], lay['KVL']
    tm = _div_tile(Lp, 256, 16)

    def one(dy, v, gain):
        r = _rms(v)
        w = dy * gain
        return r * w - v * (r * r * r) * jnp.mean(w * v, axis=-1, keepdims=True), jnp.sum(dy * v * r, axis=0, keepdims=True)

    def body(dq_ref, dk_ref, q_ref, kv_ref, qg_ref, kg_ref, o_ref, dqg_ref, dkg_ref):
        da, ga = one(dq_ref[...], q_ref[...], qg_ref[...])
        db, gb = one(dk_ref[...], kv_ref[...], kg_ref[...])
        o_ref[:, :QL] = da.astype(BF)
        o_ref[:, QL:] = db.astype(BF)

        @pl.when(pl.program_id(0) == 0)
        def _():
            dqg_ref[...] = ga
            dkg_ref[...] = gb

        @pl.when(pl.program_id(0) > 0)
        def _():
            dqg_ref[...] += ga
            dkg_ref[...] += gb

    return pl.pallas_call(
        body, name=name, grid=(Lp // tm,),
        in_specs=[_rb(tm, QL), _rb(tm, KVL), _rb(tm, QL, lay['cq'] // QL), _rb(tm, KVL, lay['ckv'] // KVL),
                  _fixed((1, QL)), _fixed((1, KVL))],
        out_specs=[_rb(tm, QL + KVL), _fixed((1, QL)), _fixed((1, KVL))],
        out_shape=[jax.ShapeDtypeStruct((Lp, QL + KVL), BF), jax.ShapeDtypeStruct((1, QL), F32),
                   jax.ShapeDtypeStruct((1, KVL), F32)],
        compiler_params=_cp(('arbitrary',)),
    )(dqn, dkn, z, z, q_gain, kv_gain)


def _qk_prep(q_raw, kv, z, cos, sin, lay, H, name):
    Lp = q_raw.shape[0]
    W = H * HEAD_PAD
    tm = _div_tile(Lp, 256, 16)

    def body(q_ref, kv_ref, kr_ref, c_ref, s_ref, qo_ref, ko_ref):
        c, s = c_ref[...], s_ref[...]

        def rope(t):
            return t * c + _rot_half(t) * s

        kpe = rope(kr_ref[...]).astype(BF)
        for h in range(H):
            b = h * HEAD_PAD
            qo_ref[:, b:b + 128] = (q_ref[:, b:b + 128] * SOFTMAX_SCALE).astype(BF)
            qo_ref[:, b + 128:b + 256] = (rope(q_ref[:, b + 128:b + 256]) * SOFTMAX_SCALE).astype(BF)
            ko_ref[:, b:b + 128] = kv_ref[:, b:b + 128]
            ko_ref[:, b + 128:b + 256] = kpe

    return pl.pallas_call(
        body, name=name, grid=(Lp // tm,),
        in_specs=[_rb(tm, W), _rb(tm, W), _rb(tm, 128, lay['kr'] // 128), _rb(tm, 128), _rb(tm, 128)],
        out_specs=[_rb(tm, W), _rb(tm, W)], out_shape=[jax.ShapeDtypeStruct((Lp, W), BF)] * 2,
        compiler_params=_cp(('parallel',)),
    )(q_raw, kv, z, cos, sin)


def _qk_prep_bwd(dQ, dK, dV, cos, sin, H, name):
    Lp = dQ.shape[0]
    W = H * HEAD_PAD
    tm = _div_tile(Lp, 256, 16)

    def body(dq_ref, dk_ref, dv_ref, c_ref, s_ref, qo_ref, kvo_ref, kro_ref):
        c, s = c_ref[...], s_ref[...]

        def unrope(t):
            return t * c - _rot_half(t * s)

        acc = jnp.zeros((tm, 128), F32)
        for h in range(H):
            b = h * HEAD_PAD
            qo_ref[:, b:b + 128] = (dq_ref[:, b:b + 128] * SOFTMAX_SCALE).astype(BF)
            qo_ref[:, b + 128:b + 256] = (unrope(dq_ref[:, b + 128:b + 256]) * SOFTMAX_SCALE).astype(BF)
            kvo_ref[:, b:b + 128] = dk_ref[:, b:b + 128].astype(BF)
            kvo_ref[:, b + 128:b + 256] = dv_ref[:, h * V_DIM:(h + 1) * V_DIM]
            acc += dk_ref[:, b + 128:b + 256]
        kro_ref[...] = unrope(acc).astype(BF)

    return pl.pallas_call(
        body, name=name, grid=(Lp // tm,),
        in_specs=[_rb(tm, W), _rb(tm, W), _rb(tm, H * V_DIM), _rb(tm, 128), _rb(tm, 128)],
        out_specs=[_rb(tm, W), _rb(tm, W), _rb(tm, 128)],
        out_shape=[jax.ShapeDtypeStruct((Lp, W), BF), jax.ShapeDtypeStruct((Lp, W), BF),
                   jax.ShapeDtypeStruct((Lp, 128), BF)],
        compiler_params=_cp(('parallel',)),
    )(dQ, dK, dV, cos, sin)


def _call_carrying(core, name, grid, in_specs, out_specs, out_shape, scratch, args, phases):
    n_in, n_out, n_scr = len(in_specs), len(out_specs), len(scratch)
    extra_in, extra_out, alias, sems = _phase_operands(phases, n_in, n_out)
    n_ei, n_eo = len(extra_in), len(extra_out)

    def body(*refs):
        ins, ein = refs[:n_in], refs[n_in:n_in + n_ei]
        outs = refs[n_in + n_ei:n_in + n_ei + n_out]
        eout = refs[n_in + n_ei + n_out:n_in + n_ei + n_out + n_eo]
        rest = refs[n_in + n_ei + n_out + n_eo:]
        scr, sem_refs = rest[:n_scr], rest[n_scr:]
        if phases:
            ids = [pl.program_id(d) for d in range(len(grid))]
            first, last = ids[0] == 0, ids[0] == grid[0] - 1
            for d in range(1, len(grid)):
                first, last = first & (ids[d] == 0), last & (ids[d] == grid[d] - 1)

            @pl.when(first)
            def _():
                _phase_start(phases, ein, eout, sem_refs)

        core(*ins, *outs, *scr)
        if phases:
            @pl.when(last)
            def _():
                _phase_finish(phases, ein, eout, sem_refs)

    outs = pl.pallas_call(
        body, name=name, grid=grid, in_specs=list(in_specs) + [ANY] * n_ei, out_specs=list(out_specs) + [ANY] * n_eo,
        out_shape=list(out_shape) + extra_out, input_output_aliases=alias, scratch_shapes=list(scratch) + sems,
        compiler_params=_cp(('arbitrary',) * len(grid)),
    )(*args, *extra_in)
    return list(outs[:n_out]), _phase_results(phases, outs[n_out:])


def _flash_fwd(Q, K, kv, H, name, phases=()):
    Lp = Q.shape[0]
    T = _div_tile(Lp, 256, 16)

    CH = FLASH_CHUNK * T

    def body(q_ref, k_ref, v_ref, o_ref, lse_ref, m_s, l_s, acc_s):
        i = pl.program_id(1)
        q = q_ref[...]
        m_s[...] = jnp.full((T, 1), NEG, F32)
        l_s[...] = jnp.zeros((T, 1), F32)
        acc_s[...] = jnp.zeros((T, V_DIM), F32)

        def step(start, width, masked):
            rows = pl.ds(start, width)
            s = lax.dot_general(q, k_ref[rows, :], _DIMS['nt'], preferred_element_type=F32)
            if masked:
                rr = lax.broadcasted_iota(jnp.int32, (T, width), 0) + i * T
                cc = lax.broadcasted_iota(jnp.int32, (T, width), 1) + start
                s = jnp.where(cc <= rr, s, NEG)
            m = m_s[...]
            m_new = jnp.maximum(m, jnp.max(s, axis=-1, keepdims=True))
            alpha = jnp.exp(m - m_new)
            p = jnp.exp(s - m_new)
            l_s[...] = alpha * l_s[...] + jnp.sum(p, axis=-1, keepdims=True)
            acc_s[...] = alpha * acc_s[...] + jnp.dot(p.astype(BF), v_ref[rows, :], preferred_element_type=F32)
            m_s[...] = m_new

        n_full = i // FLASH_CHUNK

        def full(cidx, carry):
            step(pl.multiple_of(cidx * CH, CH), CH, False)
            return carry

        lax.fori_loop(0, n_full, full, 0)
        for nb in range(1, FLASH_CHUNK + 1):
            @pl.when(i % FLASH_CHUNK == nb - 1)
            def _(nb=nb):
                step(pl.multiple_of(n_full * CH, CH), nb * T, True)

        l = l_s[...]
        o_ref[...] = (acc_s[...] / l).astype(o_ref.dtype)
        lse_ref[...] = jnp.broadcast_to(m_s[...] + jnp.log(l), (T, 128))

    return _call_carrying(
        body, name, (H, Lp // T),
        [pl.BlockSpec((T, HEAD_PAD), lambda h, i: (i, h)), pl.BlockSpec((Lp, HEAD_PAD), lambda h, i: (0, h)),
         pl.BlockSpec((Lp, V_DIM), lambda h, i: (0, 2 * h + 1))],
        [pl.BlockSpec((T, V_DIM), lambda h, i: (i, h)), pl.BlockSpec((None, T, 128), lambda h, i: (h, i, 0))],
        [jax.ShapeDtypeStruct((Lp, H * V_DIM), BF), jax.ShapeDtypeStruct((H, Lp, 128), F32)],
        [pltpu.VMEM((T, 1), F32), pltpu.VMEM((T, 1), F32), pltpu.VMEM((T, V_DIM), F32)], (Q, K, kv), phases)


def _flash_bwd(Q, K, kv, O, dO, lse, H, name, phases=()):
    Lp = Q.shape[0]
    T = _div_tile(Lp, 256, 16)
    n_t = Lp // T

    def body(q_ref, k_ref, v_ref, o_ref, do_ref, lse_ref, dq_ref, dk_ref, dv_ref, dk_acc, dv_acc):
        j = pl.program_id(1)

        @pl.when(j == 0)
        def _():
            dq_ref[...] = jnp.zeros_like(dq_ref)

        kj, vj = k_ref[...], v_ref[...]
        dk_acc[...] = jnp.zeros_like(dk_acc)
        dv_acc[...] = jnp.zeros_like(dv_acc)

        def step(start, width, masked):
            rows = pl.ds(start, width)
            qi, doi = q_ref[rows, :], do_ref[rows, :]
            delta = jnp.sum(doi.astype(F32) * o_ref[rows, :].astype(F32), axis=-1, keepdims=True)
            s = lax.dot_general(qi, kj, _DIMS['nt'], preferred_element_type=F32)
            p = jnp.exp(s - lse_ref[rows, :][:, :1])
            if masked:
                rr = lax.broadcasted_iota(jnp.int32, (width, T), 0) + start
                cc = lax.broadcasted_iota(jnp.int32, (width, T), 1) + j * T
                p = jnp.where(cc <= rr, p, 0.0)
            dp = lax.dot_general(doi, vj, _DIMS['nt'], preferred_element_type=F32)
            ds = (p * (dp - delta)).astype(BF)
            dv_acc[...] += lax.dot_general(p.astype(BF), doi, _DIMS['tn'], preferred_element_type=F32)
            dk_acc[...] += lax.dot_general(ds, qi, _DIMS['tn'], preferred_element_type=F32)
            dq_ref[rows, :] += jnp.dot(ds, kj, preferred_element_type=F32)

        head = (n_t - 1 - j) % FLASH_CHUNK + 1
        for nb in range(1, FLASH_CHUNK + 1):
            @pl.when(head == nb)
            def _(nb=nb):
                step(pl.multiple_of(j * T, T), nb * T, True)

        def full(cidx, carry):
            step(pl.multiple_of((j + head + cidx * FLASH_CHUNK) * T, T), FLASH_CHUNK * T, False)
            return carry

        lax.fori_loop(0, (n_t - j - head) // FLASH_CHUNK, full, 0)
        dk_ref[...] = dk_acc[...]
        dv_ref[...] = dv_acc[...].astype(dv_ref.dtype)

    head_q = pl.BlockSpec((Lp, HEAD_PAD), lambda h, j: (0, h))
    head_v = pl.BlockSpec((Lp, V_DIM), lambda h, j: (0, h))
    return _call_carrying(
        body, name, (H, n_t),
        [head_q, pl.BlockSpec((T, HEAD_PAD), lambda h, j: (j, h)), pl.BlockSpec((T, V_DIM), lambda h, j: (j, 2 * h + 1)),
         head_v, head_v, pl.BlockSpec((None, Lp, 128), lambda h, j: (h, 0, 0))],
        [head_q, pl.BlockSpec((T, HEAD_PAD), lambda h, j: (j, h)), pl.BlockSpec((T, V_DIM), lambda h, j: (j, h))],
        [jax.ShapeDtypeStruct((Lp, H * HEAD_PAD), F32), jax.ShapeDtypeStruct((Lp, H * HEAD_PAD), F32),
         jax.ShapeDtypeStruct((Lp, H * V_DIM), BF)],
        [pltpu.VMEM((T, HEAD_PAD), F32), pltpu.VMEM((T, V_DIM), F32)], (Q, K, kv, O, dO, lse), phases)


def _gate_fwd(z, y_pool, y_mla, lay, name):
    Lp, D = y_pool.shape
    tm = _div_tile(Lp, 256, 16)

    def body(gp_ref, gm_ref, yp_ref, ym_ref, o_ref):
        o_ref[...] = (jax.nn.sigmoid(gp_ref[...]) * yp_ref[...] + jax.nn.sigmoid(gm_ref[...]) * ym_ref[...]).astype(BF)

    return pl.pallas_call(
        body, name=name, grid=(Lp // tm,),
        in_specs=[_rb(tm, D, lay['gp'] // D), _rb(tm, D, lay['gm'] // D), _rb(tm, D), _rb(tm, D)], out_specs=_rb(tm, D),
        out_shape=jax.ShapeDtypeStruct((Lp, D), BF), compiler_params=_cp(('parallel',)),
    )(z, z, y_pool, y_mla)


def _gate_bwd(dy, z, y_pool, y_mla, lay, name):
    Lp, D = y_pool.shape
    tm = _div_tile(Lp, 256, 16)

    def body(dy_ref, gp_ref, gm_ref, yp_ref, ym_ref, dp_ref, dm_ref, dg_ref):
        dy_ = dy_ref[...]
        sp, sm = jax.nn.sigmoid(gp_ref[...]), jax.nn.sigmoid(gm_ref[...])
        dp_ref[...] = (dy_ * sp).astype(BF)
        dm_ref[...] = (dy_ * sm).astype(BF)
        dg_ref[:, :D] = (dy_ * yp_ref[...] * (sp * (1.0 - sp))).astype(BF)
        dg_ref[:, D:] = (dy_ * ym_ref[...] * (sm * (1.0 - sm))).astype(BF)

    return pl.pallas_call(
        body, name=name, grid=(Lp // tm,),
        in_specs=[_rb(tm, D), _rb(tm, D, lay['gp'] // D), _rb(tm, D, lay['gm'] // D), _rb(tm, D), _rb(tm, D)],
        out_specs=[_rb(tm, D), _rb(tm, D), _rb(tm, 2 * D)],
        out_shape=[jax.ShapeDtypeStruct((Lp, D), BF), jax.ShapeDtypeStruct((Lp, D), BF),
                   jax.ShapeDtypeStruct((Lp, 2 * D), BF)],
        compiler_params=_cp(('parallel',)),
    )(dy, z, z, y_pool, y_mla)


def _z_layout(D, QL, KVL):
    cq = POOL_WIDTH
    ckv = cq + QL
    gp = -(-(ckv + KVL) // D) * D
    gm = gp + D
    kr = gm + D
    return dict(QL=QL, KVL=KVL, cq=cq, ckv=ckv, gp=gp, gm=gm, kr=kr, width=kr + 128)


def _w_in_aligned(w4, lay, D):
    w = w4.shape[2]
    n0 = POOL_WIDTH + lay['QL'] + lay['KVL']

    def cols(a, b):
        return [w4[k][:, max(a, k * w) - k * w:min(b, (k + 1) * w) - k * w] for k in range(N_CHIPS)
                if max(a, k * w) < min(b, (k + 1) * w)]

    parts = (cols(0, n0) + [jnp.zeros((D, lay['gp'] - n0), w4.dtype)] + cols(n0 + QK_ROPE, N_CHIPS * w) +
             cols(n0, n0 + QK_ROPE) + [jnp.zeros((D, 128 - QK_ROPE), w4.dtype)])
    return jnp.concatenate(parts, axis=1)


def _w_in_slabs(w_al, lay, D, w):
    n0 = POOL_WIDTH + lay['QL'] + lay['KVL']
    segs = [(0, n0, 0), (n0, n0 + QK_ROPE, lay['kr']), (n0 + QK_ROPE, N_CHIPS * w, lay['gp'])]

    def slab(k):
        parts = []
        for a, b, at in segs:
            lo, hi = max(a, k * w), min(b, (k + 1) * w)
            if lo < hi:
                parts.append(w_al[:, at + lo - a:at + hi - a])
        return jnp.concatenate(parts, axis=1)

    return jnp.stack([slab(k) for k in range(N_CHIPS)], axis=0)


def kernel(x, meta_tokens, norm_ffn1_pre, norm_ffn1_post, ffn1_w_gu, ffn1_w_down, norm_mix_pre, norm_mix_post, w_in, pool_w, pool_scale, w_pool_o, q_a_norm, w_q_b, kv_a_norm, w_kv_b, w_mla_o, w_out, norm_ffn2_pre, norm_ffn2_post, ffn2_w_gu, ffn2_w_down, loss_target, m_meta_tokens, m_norm_ffn1_pre, m_norm_ffn1_post, m_ffn1_w_gu, m_ffn1_w_down, m_norm_mix_pre, m_norm_mix_post, m_w_in, m_pool_w, m_pool_scale, m_w_pool_o, m_q_a_norm, m_w_q_b, m_kv_a_norm, m_w_kv_b, m_w_mla_o, m_w_out, m_norm_ffn2_pre, m_norm_ffn2_post, m_ffn2_w_gu, m_ffn2_w_down, v_meta_tokens, v_norm_ffn1_pre, v_norm_ffn1_post, v_ffn1_w_gu, v_ffn1_w_down, v_norm_mix_pre, v_norm_mix_post, v_w_in, v_pool_w, v_pool_scale, v_w_pool_o, v_q_a_norm, v_w_q_b, v_kv_a_norm, v_w_kv_b, v_w_mla_o, v_w_out, v_norm_ffn2_pre, v_norm_ffn2_post, v_ffn2_w_gu, v_ffn2_w_down):
    given = dict(locals())
    W = {n: given[n] for n in WEIGHTS}
    M = {n: given['m_' + n] for n in WEIGHTS}
    V = {n: given['v_' + n] for n in WEIGHTS}

    S, D = x.shape[1], x.shape[2]
    NM = meta_tokens.shape[0]
    L = NM + S
    Lp = -(-L // ROW_ALIGN) * ROW_ALIGN
    QL, KVL = w_q_b.shape[1], w_kv_b.shape[1]
    H = w_q_b.shape[2] * N_CHIPS // QK_DIM
    lay = _z_layout(D, QL, KVL)
    gx, gy = lax.axis_index('x'), lax.axis_index('y')
    chip = 2 * gx + gy

    names = list(BIG)
    slab = {n: W[n][0] for n in names}
    kind = dict(BIG, meta_tokens='col')
    slab_shape = {n: tuple(slab[n].shape) for n in names}
    gc = lax.axis_index('c')
    c_arr = jnp.stack([gc]).astype(jnp.int32)
    place_arr = jnp.stack([gc, chip]).astype(jnp.int32)

    full = {n: _cast_place(slab[n], BIG[n], place_arr, BF, f'place_{n}') for n in names}
    full['meta_tokens'] = _cast_place(meta_tokens, 'col', place_arr, F32, 'place_meta_tokens')
    G0 = ['ffn1_w_gu', 'meta_tokens']
    G0B = ['ffn1_w_down']
    G1W = ['w_in']
    G1R = ['pool_w', 'w_pool_o', 'w_q_b', 'w_kv_b']
    G2 = ['w_mla_o', 'w_out']
    G3 = ['ffn2_w_gu', 'ffn2_w_down']

    def gather(phase_fn, group):
        return phase_fn([full[n] for n in group], [kind[n] for n in group])

    def arrived(group, res):
        full.update(zip(group, res))

    arrived(G0, _all_gather([full[n] for n in G0], [kind[n] for n in G0]))
    meta_full = full['meta_tokens']

    pos = jnp.arange(Lp, dtype=F32)
    inv = ROPE_THETA ** (-jnp.arange(0, QK_ROPE, 2, dtype=F32) / QK_ROPE)
    ang = pos[:, None] * inv[None, :]
    ang = jnp.concatenate([ang, ang], axis=-1)
    cos = jnp.pad(jnp.cos(ang), ((0, 0), (0, 128 - QK_ROPE)), constant_values=1.0)
    sin = jnp.pad(jnp.sin(ang), ((0, 0), (0, 128 - QK_ROPE)))

    h0 = jnp.concatenate([meta_full, x[0], jnp.zeros((Lp - L, D), F32)], axis=0)
    tgt = jnp.pad(loss_target[0], ((NM, Lp - L), (0, 0)))

    n1 = _norm_fwd(h0, norm_ffn1_pre, 'ffn1_norm')
    gu1, (res_b, res) = _mm(n1, full['ffn1_w_gu'], 'nn', BF, 'ffn1_gu',
                            [gather(_phase_gather_ici, G0B), gather(_phase_gather_ici, G1R)])
    arrived(G0B, res_b)
    arrived(G1R, res)
    (a1,), (res_b,) = _swiglu_fwd(gu1, 'ffn1_act', [gather(_phase_gather_d2d, G0B)])
    arrived(G0B, res_b)
    f1, (res, res_w) = _mm(a1, full['ffn1_w_down'], 'nn', F32, 'ffn1_down',
                           [gather(_phase_gather_d2d, G1R), gather(_phase_gather_ici, G1W)])
    arrived(G1R, res)
    arrived(G1W, res_w)
    h1, (res_w,) = _post_residual(h0, f1, norm_ffn1_post, 0.5, 'ffn1_res', [gather(_phase_gather_d2d, G1W)])
    arrived(G1W, res_w)

    w_in_al = _w_in_aligned(full['w_in'], lay, D)
    w_q_pad = jnp.pad(full['w_q_b'].reshape(QL, H, QK_DIM), ((0, 0), (0, 0), (0, HEAD_PAD - QK_DIM))).reshape(
        QL, H * HEAD_PAD)

    n2 = _norm_fwd(h1, norm_mix_pre, 'mix_norm')
    z, (res,) = _mm(n2, w_in_al, 'nn', F32, 'mix_in', [gather(_phase_gather_ici, G2)])
    arrived(G2, res)
    d_pool = _pool_fwd(z, 'pool_fwd')
    e_pool, yp = _pool_mix_fwd(d_pool, full['pool_w'], pool_scale, 'pool_mix')
    y_pool = _mm(yp, full['w_pool_o'], 'nn', F32, 'pool_out')
    cqn, ckvn = _lora_norms(z, q_a_norm, kv_a_norm, lay, 'lora_norms')
    q_raw = _mm(cqn, w_q_pad, 'nn', F32, 'mla_q')
    kv = _mm(ckvn, full['w_kv_b'], 'nn', BF, 'mla_kv')
    Q, K = _qk_prep(q_raw, kv, z, cos, sin, lay, H, 'qk_prep')
    (O, lse), (res2, res3) = _flash_fwd(Q, K, kv, H, 'flash_fwd',
                                        [gather(_phase_gather_d2d, G2), gather(_phase_gather_ici, G3)])
    arrived(G2, res2)
    arrived(G3, res3)
    y_mla, (res,) = _mm(O, full['w_mla_o'], 'nn', F32, 'mla_out', [gather(_phase_gather_d2d, G3)])
    arrived(G3, res)
    y = _gate_fwd(z, y_pool, y_mla, lay, 'gate')
    m_mix = _mm(y, full['w_out'], 'nn', F32, 'mix_out')
    h2, _ = _post_residual(h1, m_mix, norm_mix_post, 1.0, 'mix_res')

    n3 = _norm_fwd(h2, norm_ffn2_pre, 'ffn2_norm')
    gu2 = _mm(n3, full['ffn2_w_gu'], 'nn', BF, 'ffn2_gu')
    (a2,), _ = _swiglu_fwd(gu2, 'ffn2_act')
    f2 = _mm(a2, full['ffn2_w_down'], 'nn', F32, 'ffn2_down')
    h3, _ = _post_residual(h2, f2, norm_ffn2_post, 0.5, 'ffn2_res')

    G, theirs, sums, halves, reduced = {}, {}, {}, {}, {}
    RA = ['ffn2_w_down', 'ffn2_w_gu']
    RB1 = ['w_out', 'w_pool_o', 'pool_w', 'w_mla_o']
    RB2 = ['w_q_b', 'w_kv_b']
    RB3 = ['w_in']
    RC1 = ['ffn1_w_down']
    RC2 = ['ffn1_w_gu']

    def pair_phase(group):
        return _phase_pair([G[n] for n in group], [BIG[n] for n in group], [slab_shape[n] for n in group])

    def pair_sums(group, res):
        for n, t in zip(group, res):
            sums[n] = _pair_sum(G[n], t, BIG[n], slab_shape[n], c_arr, f'rs_pair_sum_{n}')

    def chip_phase(group):
        return _phase_chip([sums[n] for n in group])

    def chip_sums(group, res):
        for n, ld in zip(group, res):
            halves[n] = _chip_sum(sums[n], ld, BIG[n], slab_shape[n], place_arr, f'rs_chip_sum_{n}')

    def final_phase(group):
        return _phase_final([halves[n] for n in group], [BIG[n] for n in group])

    dh3, sq = _loss_bwd(h3, tgt, NM, S, 'loss')

    df2, G['norm_ffn2_post'] = _post_bwd(dh3, f2, norm_ffn2_post, 0.5, 'ffn2_res_bwd')
    G['ffn2_w_down'] = _mm(a2, df2, 'tn', BF, 'ffn2_dw_down')
    da2 = _mm(df2, full['ffn2_w_down'], 'nt', BF, 'ffn2_da')
    dgu2 = _swiglu_bwd(da2, gu2, 'ffn2_act_bwd')
    G['ffn2_w_gu'] = _mm(n3, dgu2, 'tn', BF, 'ffn2_dw_gu')
    dn3, (res,) = _mm(dgu2, full['ffn2_w_gu'], 'nt', F32, 'ffn2_dn', [pair_phase(RA)])
    (dh2, G['norm_ffn2_pre']), _ = _pre_bwd(dh3, dn3, h2, norm_ffn2_pre, 'ffn2_norm_bwd')
    pair_sums(RA, res)

    dm, G['norm_mix_post'] = _post_bwd(dh2, m_mix, norm_mix_post, 1.0, 'mix_res_bwd')
    G['w_out'] = _mm(y, dm, 'tn', BF, 'dw_out')
    dy = _mm(dm, full['w_out'], 'nt', F32, 'mix_out_bwd')
    dy_pool, dy_mla, d_gate = _gate_bwd(dy, z, y_pool, y_mla, lay, 'gate_bwd')
    G['w_pool_o'] = _mm(yp, dy_pool, 'tn', BF, 'dw_pool_o')
    dyp = _mm(dy_pool, full['w_pool_o'], 'nt', F32, 'pool_out_bwd')
    dd, G['pool_scale'], d_pool_w = _pool_mix_bwd(dyp, e_pool, d_pool, full['pool_w'], pool_scale, 'pool_mix_bwd')
    G['pool_w'] = d_pool_w.astype(BF)
    du_pool = _pool_bwd(dd, 'pool_bwd')
    G['w_mla_o'] = _mm(O, dy_mla, 'tn', BF, 'dw_mla_o')
    dO, (res,) = _mm(dy_mla, full['w_mla_o'], 'nt', BF, 'mla_out_bwd', [pair_phase(RB1)])
    pair_sums(RB1, res)
    (dQ, dK, dV), (res, res_b) = _flash_bwd(Q, K, kv, O, dO, lse, H, 'flash_bwd', [chip_phase(RA), chip_phase(RB1)])
    chip_sums(RA, res)
    chip_sums(RB1, res_b)
    dq_raw, dkv, dkr = _qk_prep_bwd(dQ, dK, dV, cos, sin, H, 'qk_prep_bwd')
    d_w_q_pad, (res_b,) = _mm(cqn, dq_raw, 'tn', BF, 'dw_q_b', [final_phase(RB1)])
    reduced.update(zip(RB1, res_b))
    G['w_q_b'] = d_w_q_pad.reshape(QL, H, HEAD_PAD)[:, :, :QK_DIM].reshape(QL, H * QK_DIM)
    dcqn = _mm(dq_raw, w_q_pad, 'nt', F32, 'mla_q_bwd')
    G['w_kv_b'] = _mm(ckvn, dkv, 'tn', BF, 'dw_kv_b')
    dckvn, (res,) = _mm(dkv, full['w_kv_b'], 'nt', F32, 'mla_kv_bwd', [pair_phase(RB2)])
    pair_sums(RB2, res)
    d_lora, G['q_a_norm'], G['kv_a_norm'] = _lora_norms_bwd(dcqn, dckvn, z, q_a_norm, kv_a_norm, lay, 'lora_norms_bwd')
    n0 = POOL_WIDTH + QL + KVL
    dz = jnp.concatenate([du_pool, d_lora, jnp.zeros((Lp, lay['gp'] - n0), BF), d_gate, dkr], axis=1)
    d_w_in_al, (res, res_b) = _mm(n2, dz, 'tn', BF, 'dw_in', [final_phase(RA), chip_phase(RB2)])
    reduced.update(zip(RA, res))
    chip_sums(RB2, res_b)
    G['w_in'] = _w_in_slabs(d_w_in_al, lay, D, w_in.shape[2])
    dn2, (res, res_b) = _mm(dz, w_in_al, 'nt', F32, 'mix_in_bwd', [pair_phase(RB3), final_phase(RB2)])
    reduced.update(zip(RB2, res_b))
    (dh1, G['norm_mix_pre']), _ = _pre_bwd(dh2, dn2, h1, norm_mix_pre, 'mix_norm_bwd')
    pair_sums(RB3, res)

    df1, G['norm_ffn1_post'] = _post_bwd(dh1, f1, norm_ffn1_post, 0.5, 'ffn1_res_bwd')
    G['ffn1_w_down'], (res_b,) = _mm(a1, df1, 'tn', BF, 'ffn1_dw_down', [chip_phase(RB3)])
    chip_sums(RB3, res_b)
    da1, (res, res_b) = _mm(df1, full['ffn1_w_down'], 'nt', BF, 'ffn1_da', [pair_phase(RC1), final_phase(RB3)])
    reduced.update(zip(RB3, res_b))
    dgu1 = _swiglu_bwd(da1, gu1, 'ffn1_act_bwd')
    pair_sums(RC1, res)
    G['ffn1_w_gu'], (res_c1,) = _mm(n1, dgu1, 'tn', BF, 'ffn1_dw_gu', [chip_phase(RC1)])
    chip_sums(RC1, res_c1)
    (res,) = _run_phases([pair_phase(RC2)], 'rs_pair_exchange_tail')
    pair_sums(RC2, res)
    dn1, (res_c2, res_c1) = _mm(dgu1, full['ffn1_w_gu'], 'nt', F32, 'ffn1_dn', [chip_phase(RC2), final_phase(RC1)])
    chip_sums(RC2, res_c2)
    reduced.update(zip(RC1, res_c1))
    (dh0, G['norm_ffn1_pre']), (res_c2,) = _pre_bwd(dh1, dn1, h0, norm_ffn1_pre, 'ffn1_norm_bwd', [final_phase(RC2)])
    reduced.update(zip(RC2, res_c2))
    grad_x = dh0[NM:L][None]

    SW = max(D, POOL_WIDTH)

    def widen(a, fill=0.0):
        return jnp.pad(a, ((0, 0), (0, SW - a.shape[1])), constant_values=fill)

    rows = [widen(G[n]) for n in SMALL_VEC] + [widen(dh0[:NM]), widen(sq)]
    n_rows = len(SMALL_VEC) + NM + 1
    pad_rows = -(-n_rows // 8) * 8 - n_rows
    small = _all_reduce_small(jnp.concatenate(rows + [jnp.zeros((pad_rows, SW), F32)], axis=0))
    loss = (0.5 / D) * small[len(SMALL_VEC) + NM, 0]
    for i, n in enumerate(SMALL_VEC):
        reduced[n] = small[i:i + 1, :G[n].shape[1]]
    mw = meta_tokens.shape[1]
    reduced['meta_tokens'] = lax.dynamic_slice(small, (len(SMALL_VEC), chip * mw), (NM, mw))

    grads, deltas, new_m, new_v = {}, {}, {}, {}
    for n in names:
        w = slab[n]
        deltas[n], new_m[n], new_v[n] = [o[None] for o in _adam(w, reduced[n], M[n][0], V[n][0], f'adam_{n}')]
        grads[n] = reduced[n][None]
    n_vec = len(SMALL_VEC)
    vec_w = jnp.concatenate([widen(W[n]) for n in SMALL_VEC] + [jnp.zeros((16 - n_vec, SW), F32)], axis=0)
    vec_m = jnp.concatenate([widen(M[n]) for n in SMALL_VEC] + [jnp.zeros((16 - n_vec, SW), F32)], axis=0)
    vec_v = jnp.concatenate([widen(V[n], 1.0) for n in SMALL_VEC] + [jnp.ones((16 - n_vec, SW), F32)], axis=0)
    vec_g = jnp.concatenate([small[:n_vec], jnp.zeros((16 - n_vec, SW), F32)], axis=0)
    vd, vm, vv = _adam(vec_w, vec_g, vec_m, vec_v, 'adam_vectors')
    for i, n in enumerate(SMALL_VEC):
        wdt = W[n].shape[1]
        grads[n], deltas[n], new_m[n], new_v[n] = reduced[n], vd[i:i + 1, :wdt], vm[i:i + 1, :wdt], vv[i:i + 1, :wdt]
    grads['meta_tokens'] = reduced['meta_tokens']
    deltas['meta_tokens'], new_m['meta_tokens'], new_v['meta_tokens'] = _adam(
        meta_tokens, reduced['meta_tokens'], m_meta_tokens, v_meta_tokens, 'adam_meta')

    return (loss, grad_x, *[grads[n] for n in WEIGHTS], *[deltas[n] for n in WEIGHTS], *[new_m[n] for n in WEIGHTS],
            *[new_v[n] for n in WEIGHTS])
```

```python
import functools

import jax
import jax.numpy as jnp
import numpy as np
from jax import lax
from jax.experimental import pallas as pl
from jax.experimental.pallas import tpu as pltpu

F32 = jnp.float32
BF = jnp.bfloat16
MESH = pl.DeviceIdType.MESH

EPS = 1e-6
N_CHIPS = 4
POOL_WINDOWS = (2, 4, 8, 16)
POOL_GROUP = 256
POOL_WIDTH = POOL_GROUP * len(POOL_WINDOWS)
QK_NOPE = 128
QK_ROPE = 64
V_DIM = 128
QK_DIM = QK_NOPE + QK_ROPE
HEAD_PAD = 256
ROPE_THETA = 10000.0
SOFTMAX_SCALE = QK_DIM ** -0.5
ADAM_LR = 0.001
ADAM_B1 = 0.9
ADAM_B2 = 0.999
ADAM_EPS = 1e-08
ADAM_WD = 0.01
ADAM_STEP = 10
ROW_ALIGN = 256
VMEM_LIMIT = 56 * 1024 * 1024
MM_TK_ROWS = 4352
MM_TK_COLS = 6400
NEG = -1e30
FLASH_CHUNK = 4
WEIGHTS = ['meta_tokens', 'norm_ffn1_pre', 'norm_ffn1_post', 'ffn1_w_gu', 'ffn1_w_down', 'norm_mix_pre',
           'norm_mix_post', 'w_in', 'pool_w', 'pool_scale', 'w_pool_o', 'q_a_norm', 'w_q_b', 'kv_a_norm', 'w_kv_b',
           'w_mla_o', 'w_out', 'norm_ffn2_pre', 'norm_ffn2_post', 'ffn2_w_gu', 'ffn2_w_down']
BIG = {'ffn1_w_gu': 'col', 'ffn1_w_down': 'row', 'w_in': 'lead', 'pool_w': 'pool', 'w_pool_o': 'col', 'w_q_b': 'col',
       'w_kv_b': 'col', 'w_mla_o': 'row', 'w_out': 'row', 'ffn2_w_gu': 'col', 'ffn2_w_down': 'row'}
SMALL_VEC = ['norm_ffn1_pre', 'norm_ffn1_post', 'norm_mix_pre', 'norm_mix_post', 'norm_ffn2_pre', 'norm_ffn2_post',
             'pool_scale', 'q_a_norm', 'kv_a_norm']


def _div_tile(n, target, align):
    best = None
    for t in range(align, min(n, target) + 1, align):
        if n % t == 0:
            best = t
    return best if best is not None else n


def _cp(sem=None):
    return pltpu.CompilerParams(dimension_semantics=sem, vmem_limit_bytes=VMEM_LIMIT)


def _full_shape(kind, slab):
    if kind == 'col':
        return (slab[0], slab[1] * N_CHIPS)
    if kind == 'row':
        return (slab[0] * N_CHIPS, slab[1])
    if kind == 'lead':
        return (N_CHIPS,) + tuple(slab)
    return (slab[0], slab[1] * N_CHIPS, slab[2])


def _half_shape(kind, slab):
    if kind == 'pool':
        return (slab[0], slab[1] // 2, slab[2])
    return (slab[0] // 2, slab[1])


def _half_of_slab(ref, kind, c):
    if kind == 'pool':
        n = ref.shape[1] // 2
        return ref.at[:, pl.ds(c * n, n), :]
    n = ref.shape[0] // 2
    return ref.at[pl.ds(c * n, n), :]


def _piece(ref, kind, k, c):
    if kind == 'col':
        r, w = ref.shape[0] // 2, ref.shape[1] // N_CHIPS
        return ref.at[pl.ds(c * r, r), pl.ds(k * w, w)]
    if kind == 'row':
        r = ref.shape[0] // (2 * N_CHIPS)
        return ref.at[pl.ds((2 * k + c) * r, r), :]
    if kind == 'lead':
        r = ref.shape[1] // 2
        return ref.at[k, pl.ds(c * r, r), :]
    r = ref.shape[1] // (2 * N_CHIPS)
    return ref.at[:, pl.ds((2 * k + c) * r, r), :]


def _slab(ref, kind, k):
    if kind == 'col':
        w = ref.shape[1] // N_CHIPS
        return ref.at[:, pl.ds(k * w, w)]
    if kind == 'row':
        r = ref.shape[0] // N_CHIPS
        return ref.at[pl.ds(k * r, r), :]
    if kind == 'lead':
        return ref.at[k]
    r = ref.shape[1] // N_CHIPS
    return ref.at[:, pl.ds(k * r, r), :]


def _place():
    x, y, c = lax.axis_index('x'), lax.axis_index('y'), lax.axis_index('c')
    return x, y, c


def _peer_chip(x, y, j):
    px = 1 - x if (j >> 1) else x
    py = 1 - y if (j & 1) else y
    return px, py


ANY = pl.BlockSpec(memory_space=pl.ANY)


def _all_gather(bufs, kinds):
    n = len(bufs)

    def body(*refs):
        outs = refs[n:2 * n]
        ssem, rsem, fssem, frsem = refs[2 * n:]
        x, y, c = _place()
        me = 2 * x + y
        sib = (x, y, 1 - c)
        sends = []
        for w in range(n):
            for j in (1, 2, 3):
                px, py = _peer_chip(x, y, j)
                mine = _piece(outs[w], kinds[w], me, c)
                sends.append(pltpu.make_async_remote_copy(
                    src_ref=mine, dst_ref=mine, send_sem=ssem.at[w, j - 1], recv_sem=rsem.at[w, j - 1],
                    device_id=(px, py, c), device_id_type=MESH))
        for cp in sends:
            cp.start()
        fwds = []
        for w in range(n):
            for j in (1, 2, 3):
                px, py = _peer_chip(x, y, j)
                got = _piece(outs[w], kinds[w], 2 * px + py, c)
                pltpu.make_async_remote_copy(src_ref=got, dst_ref=got, send_sem=ssem.at[w, j - 1],
                                             recv_sem=rsem.at[w, j - 1], device_id=(px, py, c),
                                             device_id_type=MESH).wait_recv()
                fwd = pltpu.make_async_remote_copy(src_ref=got, dst_ref=got, send_sem=fssem.at[w, j - 1],
                                                   recv_sem=frsem.at[w, j - 1], device_id=sib, device_id_type=MESH)
                fwd.start()
                fwds.append(fwd)
        for w in range(n):
            for j in (1, 2, 3):
                px, py = _peer_chip(x, y, j)
                other = _piece(outs[w], kinds[w], 2 * px + py, 1 - c)
                pltpu.make_async_remote_copy(src_ref=other, dst_ref=other, send_sem=fssem.at[w, j - 1],
                                             recv_sem=frsem.at[w, j - 1], device_id=sib,
                                             device_id_type=MESH).wait_recv()
        for cp in sends + fwds:
            cp.wait_send()

    return pl.pallas_call(
        body, name='all_gather_weights', out_shape=[jax.ShapeDtypeStruct(b.shape, b.dtype) for b in bufs],
        in_specs=[ANY] * n, out_specs=[ANY] * n, input_output_aliases={w: w for w in range(n)},
        scratch_shapes=[pltpu.SemaphoreType.DMA((n, 3)), pltpu.SemaphoreType.DMA((n, 3)),
                        pltpu.SemaphoreType.DMA((n, 3)), pltpu.SemaphoreType.DMA((n, 3))],
    )(*bufs)


def _remote(src, dst, ssem, rsem, k, dev):
    return pltpu.make_async_remote_copy(src_ref=src, dst_ref=dst, send_sem=ssem.at[k], recv_sem=rsem.at[k],
                                        device_id=dev, device_id_type=MESH)


def _phase_gather_ici(bufs, kinds):
    n = len(bufs)

    def build(ins, outs, ssem, rsem):
        x, y, c = _place()
        me = 2 * x + y
        ds = []
        for w in range(n):
            for j in (1, 2, 3):
                px, py = _peer_chip(x, y, j)
                mine = _piece(outs[w], kinds[w], me, c)
                got = _piece(outs[w], kinds[w], 2 * px + py, c)
                k = 3 * w + j - 1
                ds.append((_remote(mine, mine, ssem, rsem, k, (px, py, c)), _remote(got, got, ssem, rsem, k, (px, py, c))))
        return ds

    return dict(ins=list(bufs), outs=[jax.ShapeDtypeStruct(b.shape, b.dtype) for b in bufs],
                alias={w: w for w in range(n)}, nsem=3 * n, build=build)


def _phase_gather_d2d(bufs, kinds):
    n = len(bufs)

    def build(ins, outs, ssem, rsem):
        x, y, c = _place()
        sib = (x, y, 1 - c)
        ds = []
        for w in range(n):
            for j in (1, 2, 3):
                px, py = _peer_chip(x, y, j)
                have = _piece(outs[w], kinds[w], 2 * px + py, c)
                want = _piece(outs[w], kinds[w], 2 * px + py, 1 - c)
                k = 3 * w + j - 1
                ds.append((_remote(have, have, ssem, rsem, k, sib), _remote(want, want, ssem, rsem, k, sib)))
        return ds

    return dict(ins=list(bufs), outs=[jax.ShapeDtypeStruct(b.shape, b.dtype) for b in bufs],
                alias={w: w for w in range(n)}, nsem=3 * n, build=build)


def _phase_gather_both(bufs, kinds, peers):
    n, m = len(bufs), len(peers)

    def legs(outs, ssem, rsem, second):
        x, y, c = _place()
        me = 2 * x + y
        sib = (x, y, 1 - c)
        ds = []
        for w in range(n):
            for q, j in enumerate(peers):
                px, py = _peer_chip(x, y, j)
                got = _piece(outs[w], kinds[w], 2 * px + py, c)
                k = (2 * w + second) * m + q
                if second:
                    want = _piece(outs[w], kinds[w], 2 * px + py, 1 - c)
                    ds.append((_remote(got, got, ssem, rsem, k, sib), _remote(want, want, ssem, rsem, k, sib)))
                else:
                    mine = _piece(outs[w], kinds[w], me, c)
                    ds.append((_remote(mine, mine, ssem, rsem, k, (px, py, c)),
                               _remote(got, got, ssem, rsem, k, (px, py, c))))
        return ds

    return dict(ins=list(bufs), outs=[jax.ShapeDtypeStruct(b.shape, b.dtype) for b in bufs],
                alias={w: w for w in range(n)}, nsem=2 * n * m,
                build=lambda ins, outs, ssem, rsem: legs(outs, ssem, rsem, 0),
                build_late=lambda ins, outs, ssem, rsem: legs(outs, ssem, rsem, 1))


def _phase_slab_gather(slab, peers):
    m = len(peers)
    r = slab.shape[0] // 2

    def legs(ins, outs, ssem, rsem, second):
        x, y, c = _place()
        sib = (x, y, 1 - c)
        ds = []
        for q, j in enumerate(peers):
            px, py = _peer_chip(x, y, j)
            got = outs[q].at[pl.ds(c * r, r), :]
            k = second * m + q
            if second:
                want = outs[q].at[pl.ds((1 - c) * r, r), :]
                ds.append((_remote(got, got, ssem, rsem, k, sib), _remote(want, want, ssem, rsem, k, sib)))
            else:
                mine = ins[0].at[pl.ds(c * r, r), :]
                ds.append((_remote(mine, got, ssem, rsem, k, (px, py, c)), _remote(got, got, ssem, rsem, k, (px, py, c))))
        return ds

    return dict(ins=[slab], outs=[jax.ShapeDtypeStruct(slab.shape, slab.dtype)] * m, alias={}, nsem=2 * m,
                build=lambda ins, outs, ssem, rsem: legs(ins, outs, ssem, rsem, 0),
                build_late=lambda ins, outs, ssem, rsem: legs(ins, outs, ssem, rsem, 1))


def _phase_pair(grads, kinds, slabs):
    n = len(grads)

    def build(ins, outs, ssem, rsem):
        x, y, c = _place()
        sib = (x, y, 1 - c)
        ds = []
        for w in range(n):
            for k in range(N_CHIPS):
                cp = _remote(_piece(ins[w], kinds[w], k, 1 - c), outs[w].at[k], ssem, rsem, N_CHIPS * w + k, sib)
                ds.append((cp, cp))
        return ds

    return dict(ins=list(grads), alias={}, nsem=N_CHIPS * n, build=build,
                outs=[jax.ShapeDtypeStruct((N_CHIPS,) + _half_shape(k, s), g.dtype)
                      for g, k, s in zip(grads, kinds, slabs)])


def _phase_chip(sums):
    n = len(sums)

    def build(ins, outs, ssem, rsem):
        x, y, c = _place()
        ds = []
        for w in range(n):
            for j in (1, 2, 3):
                px, py = _peer_chip(x, y, j)
                cp = _remote(ins[w].at[2 * px + py], outs[w].at[j - 1], ssem, rsem, 3 * w + j - 1, (px, py, c))
                ds.append((cp, cp))
        return ds

    return dict(ins=list(sums), alias={}, nsem=3 * n, build=build,
                outs=[jax.ShapeDtypeStruct((3,) + s.shape[1:], s.dtype) for s in sums])


def _phase_final(slabs_half, kinds):
    n = len(slabs_half)

    def build(ins, outs, ssem, rsem):
        x, y, c = _place()
        sib = (x, y, 1 - c)
        ds = []
        for w in range(n):
            mine = _half_of_slab(outs[w], kinds[w], c)
            other = _half_of_slab(outs[w], kinds[w], 1 - c)
            ds.append((_remote(mine, mine, ssem, rsem, w, sib), _remote(other, other, ssem, rsem, w, sib)))
        return ds

    return dict(ins=list(slabs_half), outs=[jax.ShapeDtypeStruct(s.shape, s.dtype) for s in slabs_half],
                alias={w: w for w in range(n)}, nsem=n, build=build)


def _phase_operands(phases, n_main_in, n_main_out):
    ins, outs, alias, sems = [], [], {}, []
    for ph in phases:
        for i, o in ph['alias'].items():
            alias[n_main_in + len(ins) + i] = n_main_out + len(outs) + o
        ins += ph['ins']
        outs += ph['outs']
        sems += [pltpu.SemaphoreType.DMA((ph['nsem'],)), pltpu.SemaphoreType.DMA((ph['nsem'],))]
    return ins, outs, alias, sems


def _phase_copies(phases, in_refs, out_refs, sem_refs, which='build'):
    ds, a, b = [], 0, 0
    for p, ph in enumerate(phases):
        if which in ph:
            ds += ph[which](in_refs[a:a + len(ph['ins'])], out_refs[b:b + len(ph['outs'])], sem_refs[2 * p],
                            sem_refs[2 * p + 1])
        a += len(ph['ins'])
        b += len(ph['outs'])
    return ds


def _phase_start(phases, in_refs, out_refs, sem_refs):
    for send, _ in _phase_copies(phases, in_refs, out_refs, sem_refs):
        send.start()


def _phase_finish(phases, in_refs, out_refs, sem_refs):
    for which in ('build', 'build_late'):
        ds = _phase_copies(phases, in_refs, out_refs, sem_refs, which)
        if which == 'build_late':
            for send, _ in ds:
                send.start()
        for _, recv in ds:
            recv.wait_recv()
        for send, _ in ds:
            send.wait_send()


def _phase_results(phases, outs):
    res, b = [], 0
    for ph in phases:
        res.append(list(outs[b:b + len(ph['outs'])]))
        b += len(ph['outs'])
    return res


def _run_phases(phases, name):
    ins, out_shapes, alias, sems = _phase_operands(phases, 0, 0)
    n_in, n_out = len(ins), len(out_shapes)

    def body(*refs):
        in_refs, out_refs, sem_refs = refs[:n_in], refs[n_in:n_in + n_out], refs[n_in + n_out:]
        _phase_start(phases, in_refs, out_refs, sem_refs)
        _phase_finish(phases, in_refs, out_refs, sem_refs)

    outs = pl.pallas_call(body, name=name, out_shape=out_shapes, in_specs=[ANY] * n_in, out_specs=[ANY] * n_out,
                          input_output_aliases=alias, scratch_shapes=sems)(*ins)
    return _phase_results(phases, outs)


def _cast_place(shard, kind, place_arr, dtype, name):
    full = _full_shape(kind, shard.shape)
    if kind == 'pool':
        g, r, w = shard.shape
        grid = (g,)
        src = pl.BlockSpec((None, r, w), lambda i, s: (i, 0, 0))
        dst = pl.BlockSpec((None, r, w), lambda i, s: (i, s[1], 0))
    else:
        r, w = shard.shape
        tr = _div_tile(r, max(16, (1 << 19) // w), 16)
        nrb = r // tr
        grid = (nrb,)
        src = pl.BlockSpec((tr, w), lambda i, s: (i, 0))
        if kind == 'col':
            dst = pl.BlockSpec((tr, w), lambda i, s: (i, s[1]))
        elif kind == 'row':
            dst = pl.BlockSpec((tr, w), lambda i, s: (s[1] * nrb + i, 0))
        else:
            dst = pl.BlockSpec((None, tr, w), lambda i, s: (s[1], i, 0))

    def body(s_ref, x_ref, o_ref):
        o_ref[...] = x_ref[...].astype(o_ref.dtype)

    return pl.pallas_call(
        body, name=name,
        grid_spec=pltpu.PrefetchScalarGridSpec(num_scalar_prefetch=1, grid=grid, in_specs=[src], out_specs=dst),
        out_shape=jax.ShapeDtypeStruct(full, dtype), compiler_params=_cp(('parallel',)),
    )(place_arr, shard)


def _half_blocks(kind, slab):
    hs = _half_shape(kind, slab)
    if kind == 'pool':
        return (hs[0],), hs[1], 1
    tr = _div_tile(hs[0], max(16, (1 << 19) // hs[1]), 16)
    return (hs[0] // tr,), tr, hs[0] // tr


def _pair_sum(grad, theirs, kind, slab, c_arr, name):
    hs = _half_shape(kind, slab)
    (n_i,), tr, nrb = _half_blocks(kind, slab)
    if kind == 'pool':
        own = pl.BlockSpec((None, hs[1], hs[2]), lambda k, i, s: (i, 2 * k + s[0], 0))
        stk = pl.BlockSpec((None, None, hs[1], hs[2]), lambda k, i, s: (k, i, 0, 0))
    else:
        if kind == 'col':
            own = pl.BlockSpec((tr, hs[1]), lambda k, i, s: (s[0] * nrb + i, k))
        elif kind == 'row':
            own = pl.BlockSpec((tr, hs[1]), lambda k, i, s: ((2 * k + s[0]) * nrb + i, 0))
        else:
            own = pl.BlockSpec((None, tr, hs[1]), lambda k, i, s: (k, s[0] * nrb + i, 0))
        stk = pl.BlockSpec((None, tr, hs[1]), lambda k, i, s: (k, i, 0))

    def body(s_ref, a_ref, b_ref, o_ref):
        o_ref[...] = (a_ref[...].astype(F32) + b_ref[...].astype(F32)).astype(o_ref.dtype)

    return pl.pallas_call(
        body, name=name,
        grid_spec=pltpu.PrefetchScalarGridSpec(num_scalar_prefetch=1, grid=(N_CHIPS, n_i), in_specs=[own, stk],
                                               out_specs=stk),
        out_shape=jax.ShapeDtypeStruct((N_CHIPS,) + hs, BF), compiler_params=_cp(('parallel', 'parallel')),
    )(c_arr, grad, theirs)


def _chip_sum(sums, landed, kind, slab, place_arr, name):
    hs = _half_shape(kind, slab)
    (n_i,), tr, nrb = _half_blocks(kind, slab)
    if kind == 'pool':
        blk = (None, None, hs[1], hs[2])
        mine = pl.BlockSpec(blk, lambda i, s: (s[1], i, 0, 0))
        land = [pl.BlockSpec(blk, lambda i, s, j=j: (j, i, 0, 0)) for j in range(3)]
        out = pl.BlockSpec((None, hs[1], hs[2]), lambda i, s: (i, s[0], 0))
    else:
        blk = (None, tr, hs[1])
        mine = pl.BlockSpec(blk, lambda i, s: (s[1], i, 0))
        land = [pl.BlockSpec(blk, lambda i, s, j=j: (j, i, 0)) for j in range(3)]
        out = pl.BlockSpec((tr, hs[1]), lambda i, s: (s[0] * nrb + i, 0))

    def body(s_ref, a_ref, b_ref, c_ref, d_ref, o_ref):
        o_ref[...] = ((a_ref[...].astype(F32) + b_ref[...].astype(F32)) + c_ref[...].astype(F32)) + d_ref[...].astype(F32)

    return pl.pallas_call(
        body, name=name,
        grid_spec=pltpu.PrefetchScalarGridSpec(num_scalar_prefetch=1, grid=(n_i,), in_specs=[mine] + land, out_specs=out),
        out_shape=jax.ShapeDtypeStruct(tuple(slab), F32), compiler_params=_cp(('parallel',)),
    )(place_arr, sums, landed, landed, landed)


def _all_reduce_small(buf):
    rows, cols = buf.shape

    def body(in_ref, out_ref, land, ssem, rsem):
        x, y, c = _place()
        me = 4 * x + 2 * y + c
        land[me] = in_ref[...]
        started = []
        for j in range(1, 8):
            px = 1 - x if (j >> 2) & 1 else x
            py = 1 - y if (j >> 1) & 1 else y
            pc = 1 - c if j & 1 else c
            cp = pltpu.make_async_remote_copy(src_ref=in_ref, dst_ref=land.at[me], send_sem=ssem.at[j - 1],
                                              recv_sem=rsem.at[j - 1], device_id=(px, py, pc), device_id_type=MESH)
            cp.start()
            started.append(cp)
        for j in range(1, 8):
            px = 1 - x if (j >> 2) & 1 else x
            py = 1 - y if (j >> 1) & 1 else y
            pc = 1 - c if j & 1 else c
            slot = land.at[4 * px + 2 * py + pc]
            pltpu.make_async_remote_copy(src_ref=slot, dst_ref=slot, send_sem=ssem.at[j - 1], recv_sem=rsem.at[j - 1],
                                         device_id=(px, py, pc), device_id_type=MESH).wait_recv()
        for cp in started:
            cp.wait_send()
        acc = land[0]
        for d in range(1, 8):
            acc = acc + land[d]
        out_ref[...] = acc

    return pl.pallas_call(
        body, name='all_reduce_small', out_shape=jax.ShapeDtypeStruct((rows, cols), F32),
        in_specs=[pl.BlockSpec(memory_space=pltpu.VMEM)], out_specs=pl.BlockSpec(memory_space=pltpu.VMEM),
        scratch_shapes=[pltpu.VMEM((8, rows, cols), F32), pltpu.SemaphoreType.DMA((7,)), pltpu.SemaphoreType.DMA((7,))],
    )(buf)


def _elementwise(fn, ins, lead_index, out_shape, out_dtypes, name):
    nd = len(out_shape)
    r, cdim = out_shape[-2], out_shape[-1]
    tr = _div_tile(r, max(16, (1 << 19) // cdim), 16)
    grid = tuple(out_shape[:-2]) + (r // tr,)
    block = (None,) * (nd - 2) + (tr, cdim)

    def spec(lead):
        if lead is None:
            return pl.BlockSpec(block, lambda *g: tuple(g) + (0,))
        return pl.BlockSpec((None,) + block, lambda *g, lead=lead: (lead,) + tuple(g) + (0,))

    n_in = len(ins)

    def body(*refs):
        res = fn(*[r_[...] for r_ in refs[:n_in]])
        for o_ref, v in zip(refs[n_in:], res):
            o_ref[...] = v.astype(o_ref.dtype)

    return pl.pallas_call(
        body, name=name, grid=grid, in_specs=[spec(l) for l in lead_index],
        out_specs=[spec(None) for _ in out_dtypes],
        out_shape=[jax.ShapeDtypeStruct(tuple(out_shape), dt) for dt in out_dtypes],
        compiler_params=_cp(('parallel',) * len(grid)),
    )(*ins)


def _adam_fn(w, g, m, v):
    m = ADAM_B1 * m + (1.0 - ADAM_B1) * g
    v = ADAM_B2 * v + (1.0 - ADAM_B2) * (g * g)
    m_hat = m / (1.0 - ADAM_B1 ** ADAM_STEP)
    v_hat = v / (1.0 - ADAM_B2 ** ADAM_STEP)
    delta = -ADAM_LR * (m_hat / (jnp.sqrt(v_hat) + ADAM_EPS) + ADAM_WD * w)
    return delta, m, v


def _adam(w, g, m, v, name):
    return _elementwise(_adam_fn, [w, g, m, v], [None] * 4, w.shape, [F32] * 3, name)


_DIMS = {'nn': (((1,), (0,)), ((), ())), 'nt': (((1,), (1,)), ((), ())), 'tn': (((0,), (0,)), ((), ()))}


def _mm(a, b, mode, out_dtype, name, phases=()):
    if mode == 'nn':
        (M, K), N = a.shape, b.shape[1]
    elif mode == 'nt':
        (M, K), N = a.shape, b.shape[0]
    else:
        (K, M), N = a.shape, b.shape[1]
    if mode == 'tn':
        tm, tn, tk = _div_tile(M, 512, 128), _div_tile(N, 1024, 128), _div_tile(K, MM_TK_ROWS, 16)
    else:
        tk = _div_tile(K, MM_TK_COLS, 128)
        tm, tn = _div_tile(M, 1088, 16), _div_tile(N, 1024 if tk <= 2816 else 512, 128)
    nk = K // tk
    a_spec = {'nn': pl.BlockSpec((tm, tk), lambda i, j, k: (i, k)), 'nt': pl.BlockSpec((tm, tk), lambda i, j, k: (i, k)),
              'tn': pl.BlockSpec((tk, tm), lambda i, j, k: (k, i))}[mode]
    b_spec = {'nn': pl.BlockSpec((tk, tn), lambda i, j, k: (k, j)), 'nt': pl.BlockSpec((tn, tk), lambda i, j, k: (j, k)),
              'tn': pl.BlockSpec((tk, tn), lambda i, j, k: (k, j))}[mode]
    dims = _DIMS[mode]
    gm, gn = M // tm, N // tn
    extra_in, extra_out, alias, sems = _phase_operands(phases, 2, 1)
    n_ei, n_eo = len(extra_in), len(extra_out)

    def body(*refs):
        a_ref, b_ref, ein = refs[0], refs[1], refs[2:2 + n_ei]
        o_ref, eout = refs[2 + n_ei], refs[3 + n_ei:3 + n_ei + n_eo]
        acc_ref, sem_refs = refs[3 + n_ei + n_eo], refs[4 + n_ei + n_eo:]
        i, j, k = pl.program_id(0), pl.program_id(1), pl.program_id(2)
        if phases:
            @pl.when((i == 0) & (j == 0) & (k == 0))
            def _():
                _phase_start(phases, ein, eout, sem_refs)

        part = lax.dot_general(a_ref[...], b_ref[...], dims, preferred_element_type=F32)
        if nk == 1:
            o_ref[...] = part.astype(o_ref.dtype)
        else:
            @pl.when(k == 0)
            def _():
                acc_ref[...] = part

            @pl.when(k > 0)
            def _():
                acc_ref[...] += part

            @pl.when(k == nk - 1)
            def _():
                o_ref[...] = acc_ref[...].astype(o_ref.dtype)

        if phases:
            @pl.when((i == gm - 1) & (j == gn - 1) & (k == nk - 1))
            def _():
                _phase_finish(phases, ein, eout, sem_refs)

    outs = pl.pallas_call(
        body, name=name, grid=(gm, gn, nk), in_specs=[a_spec, b_spec] + [ANY] * n_ei,
        out_specs=[pl.BlockSpec((tm, tn), lambda i, j, k: (i, j))] + [ANY] * n_eo,
        out_shape=[jax.ShapeDtypeStruct((M, N), out_dtype)] + extra_out, input_output_aliases=alias,
        scratch_shapes=[pltpu.VMEM((tm, tn) if nk > 1 else (8, 128), F32)] + sems,
        compiler_params=_cp(('arbitrary',) * 3 if phases else ('parallel', 'parallel', 'arbitrary')),
    )(a, b, *extra_in)
    if phases:
        return outs[0], _phase_results(phases, outs[1:])
    return outs[0]


def _mm_slab(a, w_slab, prev, s, n_total, place_arr, name, phases):
    M, K = a.shape
    ws = w_slab.shape[1]
    tm, tn = _div_tile(M, 1088, 16), _div_tile(ws, 1408, 128)
    gm, tps = M // tm, ws // tn
    n_main_in = 3 + (prev is not None)
    extra_in, extra_out, alias, sems = _phase_operands(phases, n_main_in, 1)
    n_ei, n_eo = len(extra_in), len(extra_out)
    if prev is not None:
        alias[3] = 0

    def body(*refs):
        a_ref, w_ref = refs[1], refs[2]
        ein = refs[n_main_in:n_main_in + n_ei]
        o_ref = refs[n_main_in + n_ei]
        eout = refs[n_main_in + n_ei + 1:n_main_in + n_ei + 1 + n_eo]
        sem_refs = refs[n_main_in + n_ei + 1 + n_eo:]
        i, j = pl.program_id(0), pl.program_id(1)

        @pl.when((i == 0) & (j == 0))
        def _():
            _phase_start(phases, ein, eout, sem_refs)

        o_ref[...] = jnp.dot(a_ref[...], w_ref[...], preferred_element_type=F32).astype(o_ref.dtype)

        @pl.when((i == gm - 1) & (j == tps - 1))
        def _():
            _phase_finish(phases, ein, eout, sem_refs)

    in_specs = [pl.BlockSpec((tm, K), lambda i, j, p: (i, 0)), pl.BlockSpec((K, tn), lambda i, j, p: (0, j))]
    args = [a, w_slab]
    if prev is not None:
        in_specs.append(ANY)
        args.append(prev)
    outs = pl.pallas_call(
        body, name=name,
        grid_spec=pltpu.PrefetchScalarGridSpec(
            num_scalar_prefetch=1, grid=(gm, tps), in_specs=in_specs + [ANY] * n_ei,
            out_specs=[pl.BlockSpec((tm, tn), lambda i, j, p: (i, jnp.bitwise_xor(p[1], s) * tps + j))] + [ANY] * n_eo,
            scratch_shapes=sems),
        out_shape=[jax.ShapeDtypeStruct((M, n_total), BF)] + extra_out, input_output_aliases=alias,
        compiler_params=_cp(('arbitrary', 'arbitrary')),
    )(place_arr, *args, *extra_in)
    return outs[0], _phase_results(phases, outs[1:])


def _rb(tm, w, col=0):
    return pl.BlockSpec((tm, w), lambda i, col=col: (i, col))


def _fixed(shape):
    return pl.BlockSpec(shape, lambda i: (0,) * len(shape))


def _rms(x):
    return lax.rsqrt(jnp.mean(x * x, axis=-1, keepdims=True) + EPS)


def _norm_fwd(x, gain, name, width=None, col=0):
    Lp = x.shape[0]
    width = width or x.shape[1]
    tm = _div_tile(Lp, 256, 16)

    def body(x_ref, g_ref, o_ref):
        v = x_ref[...]
        o_ref[...] = (v * _rms(v) * g_ref[...]).astype(o_ref.dtype)

    return pl.pallas_call(
        body, name=name, grid=(Lp // tm,), in_specs=[_rb(tm, width, col), _fixed((1, width))], out_specs=_rb(tm, width),
        out_shape=jax.ShapeDtypeStruct((Lp, width), BF), compiler_params=_cp(('parallel',)),
    )(x, gain)


def _post_residual(h, f, gain, scale, name, phases=()):
    Lp, D = h.shape
    tm = _div_tile(Lp, 256, 16)

    def body(h_ref, f_ref, g_ref, o_ref):
        v = f_ref[...]
        o_ref[...] = h_ref[...] + scale * (v * _rms(v) * g_ref[...])

    (out,), res = _call_carrying(body, name, (Lp // tm,), [_rb(tm, D), _rb(tm, D), _fixed((1, D))], [_rb(tm, D)],
                                 [jax.ShapeDtypeStruct((Lp, D), F32)], [], (h, f, gain), phases)
    return out, res


def _post_bwd(dh, f, gain, scale, name):
    Lp, D = dh.shape
    tm = _div_tile(Lp, 256, 16)

    def body(dh_ref, f_ref, g_ref, df_ref, dg_ref):
        v = f_ref[...]
        r = _rms(v)
        dy = scale * dh_ref[...]
        w = dy * g_ref[...]
        df_ref[...] = (r * w - v * (r * r * r) * jnp.mean(w * v, axis=-1, keepdims=True)).astype(df_ref.dtype)
        part = jnp.sum(dy * v * r, axis=0, keepdims=True)

        @pl.when(pl.program_id(0) == 0)
        def _():
            dg_ref[...] = part

        @pl.when(pl.program_id(0) > 0)
        def _():
            dg_ref[...] += part

    return pl.pallas_call(
        body, name=name, grid=(Lp // tm,), in_specs=[_rb(tm, D), _rb(tm, D), _fixed((1, D))],
        out_specs=[_rb(tm, D), _fixed((1, D))],
        out_shape=[jax.ShapeDtypeStruct((Lp, D), BF), jax.ShapeDtypeStruct((1, D), F32)],
        compiler_params=_cp(('arbitrary',)),
    )(dh, f, gain)


def _pre_bwd(dres, dn, h, gain, name, phases=()):
    Lp, D = h.shape
    tm = _div_tile(Lp, 256, 16)

    def body(dres_ref, dn_ref, h_ref, g_ref, dh_ref, dg_ref):
        v = h_ref[...]
        r = _rms(v)
        dy = dn_ref[...]
        w = dy * g_ref[...]
        dh_ref[...] = dres_ref[...] + r * w - v * (r * r * r) * jnp.mean(w * v, axis=-1, keepdims=True)
        part = jnp.sum(dy * v * r, axis=0, keepdims=True)

        @pl.when(pl.program_id(0) == 0)
        def _():
            dg_ref[...] = part

        @pl.when(pl.program_id(0) > 0)
        def _():
            dg_ref[...] += part

    return _call_carrying(
        body, name, (Lp // tm,), [_rb(tm, D), _rb(tm, D), _rb(tm, D), _fixed((1, D))], [_rb(tm, D), _fixed((1, D))],
        [jax.ShapeDtypeStruct((Lp, D), F32), jax.ShapeDtypeStruct((1, D), F32)], [], (dres, dn, h, gain), phases)


def _swiglu_fwd(gu, name, phases=()):
    Lp, F2 = gu.shape
    F = F2 // 2
    tm = _div_tile(Lp, 256, 16)

    def body(gu_ref, a_ref):
        g = gu_ref[:, :F].astype(F32)
        u = gu_ref[:, F:].astype(F32)
        a_ref[...] = (g * jax.nn.sigmoid(g) * u).astype(a_ref.dtype)

    return _call_carrying(body, name, (Lp // tm,), [_rb(tm, F2)], [_rb(tm, F)],
                          [jax.ShapeDtypeStruct((Lp, F), BF)], [], (gu,), phases)


def _swiglu_bwd(da, gu, name):
    Lp, F2 = gu.shape
    F = F2 // 2
    tm = _div_tile(Lp, 256, 16)

    def body(da_ref, gu_ref, o_ref):
        g = gu_ref[:, :F].astype(F32)
        u = gu_ref[:, F:].astype(F32)
        da_ = da_ref[...].astype(F32)
        s = jax.nn.sigmoid(g)
        o_ref[:, :F] = (da_ * u * (s * (1.0 + g * (1.0 - s)))).astype(o_ref.dtype)
        o_ref[:, F:] = (da_ * (g * s)).astype(o_ref.dtype)

    return pl.pallas_call(
        body, name=name, grid=(Lp // tm,), in_specs=[_rb(tm, F), _rb(tm, F2)], out_specs=_rb(tm, F2),
        out_shape=jax.ShapeDtypeStruct((Lp, F2), BF), compiler_params=_cp(('parallel',)),
    )(da, gu)


def _loss_bwd(h, tgt, n_meta, n_real, name):
    Lp, D = h.shape
    tm = _div_tile(Lp, 256, 16)

    def body(h_ref, t_ref, d_ref, l_ref):
        row = lax.broadcasted_iota(jnp.int32, (tm, 1), 0) + pl.program_id(0) * tm
        ok = (row >= n_meta) & (row < n_meta + n_real)
        err = jnp.where(ok, h_ref[...] - t_ref[...], 0.0)
        d_ref[...] = err / D
        part = jnp.full((1, 128), jnp.sum(err * err), F32)

        @pl.when(pl.program_id(0) == 0)
        def _():
            l_ref[...] = part

        @pl.when(pl.program_id(0) > 0)
        def _():
            l_ref[...] += part

    return pl.pallas_call(
        body, name=name, grid=(Lp // tm,), in_specs=[_rb(tm, D), _rb(tm, D)], out_specs=[_rb(tm, D), _fixed((1, 128))],
        out_shape=[jax.ShapeDtypeStruct((Lp, D), F32), jax.ShapeDtypeStruct((1, 128), F32)],
        compiler_params=_cp(('arbitrary',)),
    )(h, tgt)


def _split_bf16(v):
    hi = v.astype(BF)
    return hi, (v - hi.astype(F32)).astype(BF)


def _pool_fwd(z, name):
    Lp = z.shape[0]
    T = _div_tile(Lp, 256, 16)
    G = len(POOL_WINDOWS)

    def body(cur_ref, prev_ref, d_ref):
        i, g = pl.program_id(0), pl.program_id(1)
        w = jnp.left_shift(2, g)
        rr = lax.broadcasted_iota(jnp.int32, (T, T), 0)
        cc = lax.broadcasted_iota(jnp.int32, (T, T), 1)
        b_cur = jnp.where((cc <= rr) & (cc > rr - w), 1.0, 0.0).astype(BF)
        w_prev = jnp.where(i > 0, w, 0)
        b_prev = jnp.where(cc - T > rr - w_prev, 1.0, 0.0).astype(BF)
        u = cur_ref[...]
        s = jnp.zeros((T, POOL_GROUP), F32)
        for part in _split_bf16(u):
            s += jnp.dot(b_cur, part, preferred_element_type=F32)
        for part in _split_bf16(prev_ref[...]):
            s += jnp.dot(b_prev, part, preferred_element_type=F32)
        t = lax.broadcasted_iota(jnp.int32, (T, 1), 0) + i * T
        cnt = jnp.minimum(w, t + 1).astype(F32)
        d_ref[...] = (s / cnt - u).astype(d_ref.dtype)

    return pl.pallas_call(
        body, name=name, grid=(Lp // T, G),
        in_specs=[pl.BlockSpec((T, POOL_GROUP), lambda i, g: (i, g)),
                  pl.BlockSpec((T, POOL_GROUP), lambda i, g: (jnp.maximum(i - 1, 0), g))],
        out_specs=pl.BlockSpec((T, POOL_GROUP), lambda i, g: (i, g)),
        out_shape=jax.ShapeDtypeStruct((Lp, POOL_WIDTH), BF), compiler_params=_cp(('parallel', 'parallel')),
    )(z, z)


def _pool_bwd(dd, name):
    Lp = dd.shape[0]
    T = _div_tile(Lp, 256, 16)
    G = len(POOL_WINDOWS)
    n_t = Lp // T

    def body(cur_ref, next_ref, o_ref):
        i, g = pl.program_id(0), pl.program_id(1)
        w = jnp.left_shift(2, g)
        rr = lax.broadcasted_iota(jnp.int32, (T, T), 0)
        cc = lax.broadcasted_iota(jnp.int32, (T, T), 1)
        b_cur = jnp.where((cc >= rr) & (cc < rr + w), 1.0, 0.0).astype(BF)
        w_next = jnp.where(i < n_t - 1, w, 0)
        b_next = jnp.where(cc + T < rr + w_next, 1.0, 0.0).astype(BF)
        t = lax.broadcasted_iota(jnp.int32, (T, 1), 0) + i * T
        cur = cur_ref[...]
        e_cur = cur / jnp.minimum(w, t + 1).astype(F32)
        e_next = next_ref[...] / jnp.minimum(w, t + T + 1).astype(F32)
        s = jnp.zeros((T, POOL_GROUP), F32)
        for part in _split_bf16(e_cur):
            s += jnp.dot(b_cur, part, preferred_element_type=F32)
        for part in _split_bf16(e_next):
            s += jnp.dot(b_next, part, preferred_element_type=F32)
        o_ref[...] = (s - cur).astype(o_ref.dtype)

    return pl.pallas_call(
        body, name=name, grid=(n_t, G),
        in_specs=[pl.BlockSpec((T, POOL_GROUP), lambda i, g: (i, g)),
                  pl.BlockSpec((T, POOL_GROUP), lambda i, g: (jnp.minimum(i + 1, n_t - 1), g))],
        out_specs=pl.BlockSpec((T, POOL_GROUP), lambda i, g: (i, g)),
        out_shape=jax.ShapeDtypeStruct((Lp, POOL_WIDTH), BF), compiler_params=_cp(('parallel', 'parallel')),
    )(dd, dd)


def _pool_mix_fwd(d, pool_w, scale, name):
    Lp = d.shape[0]
    G = len(POOL_WINDOWS)
    tm = _div_tile(Lp, 1088, 16)

    def body(d_ref, w_ref, s_ref, e_ref, y_ref):
        e = jnp.dot(d_ref[...], w_ref[...], preferred_element_type=F32)
        e_ref[...] = e.astype(e_ref.dtype)
        y_ref[...] = (e * s_ref[...]).astype(y_ref.dtype)

    blk = pl.BlockSpec((tm, POOL_GROUP), lambda g, i: (i, g))
    return pl.pallas_call(
        body, name=name, grid=(G, Lp // tm),
        in_specs=[blk, pl.BlockSpec((None, POOL_GROUP, POOL_GROUP), lambda g, i: (g, 0, 0)),
                  pl.BlockSpec((1, POOL_GROUP), lambda g, i: (0, g))],
        out_specs=[blk, blk], out_shape=[jax.ShapeDtypeStruct((Lp, POOL_WIDTH), BF)] * 2,
        compiler_params=_cp(('parallel', 'parallel')),
    )(d, pool_w, scale)


def _pool_mix_bwd(dyp, e, d, pool_w, scale, name):
    Lp = d.shape[0]
    G = len(POOL_WINDOWS)
    tm = _div_tile(Lp, 1088, 16)

    def body(dy_ref, e_ref, d_ref, w_ref, s_ref, dd_ref, ds_ref, dw_ref):
        i = pl.program_id(1)
        dy = dy_ref[...]
        de = (dy * s_ref[...]).astype(BF)
        dd_ref[...] = lax.dot_general(de, w_ref[...], _DIMS['nt'], preferred_element_type=F32)
        ds_part = jnp.sum(dy * e_ref[...].astype(F32), axis=0, keepdims=True)
        dw_part = lax.dot_general(d_ref[...], de, _DIMS['tn'], preferred_element_type=F32)

        @pl.when(i == 0)
        def _():
            ds_ref[...] = ds_part
            dw_ref[...] = dw_part

        @pl.when(i > 0)
        def _():
            ds_ref[...] += ds_part
            dw_ref[...] += dw_part

    blk = pl.BlockSpec((tm, POOL_GROUP), lambda g, i: (i, g))
    wblk = pl.BlockSpec((None, POOL_GROUP, POOL_GROUP), lambda g, i: (g, 0, 0))
    sblk = pl.BlockSpec((1, POOL_GROUP), lambda g, i: (0, g))
    return pl.pallas_call(
        body, name=name, grid=(G, Lp // tm), in_specs=[blk, blk, blk, wblk, sblk], out_specs=[blk, sblk, wblk],
        out_shape=[jax.ShapeDtypeStruct((Lp, POOL_WIDTH), F32), jax.ShapeDtypeStruct((1, POOL_WIDTH), F32),
                   jax.ShapeDtypeStruct((G, POOL_GROUP, POOL_GROUP), F32)],
        compiler_params=_cp(('parallel', 'arbitrary')),
    )(dyp, e, d, pool_w, scale)


def _rot_half(t):
    lane = lax.broadcasted_iota(jnp.int32, t.shape, 1)
    half = QK_ROPE // 2
    return jnp.where(lane < half, -pltpu.roll(t, 128 - half, 1), pltpu.roll(t, half, 1))


def _lora_norms(z, q_gain, kv_gain, lay, name):
    Lp = z.shape[0]
    QL, KVL = lay['QL'], lay['KVL']
    tm = _div_tile(Lp, 256, 16)

    def body(q_ref, kv_ref, qg_ref, kg_ref, qo_ref, ko_ref):
        a = q_ref[...]
        qo_ref[...] = (a * _rms(a) * qg_ref[...]).astype(BF)
        b = kv_ref[...]
        ko_ref[...] = (b * _rms(b) * kg_ref[...]).astype(BF)

    return pl.pallas_call(
        body, name=name, grid=(Lp // tm,),
        in_specs=[_rb(tm, QL, lay['cq'] // QL), _rb(tm, KVL, lay['ckv'] // KVL), _fixed((1, QL)), _fixed((1, KVL))],
        out_specs=[_rb(tm, QL), _rb(tm, KVL)],
        out_shape=[jax.ShapeDtypeStruct((Lp, QL), BF), jax.ShapeDtypeStruct((Lp, KVL), BF)],
        compiler_params=_cp(('parallel',)),
    )(z, z, q_gain, kv_gain)


def _lora_norms_bwd(dqn, dkn, z, q_gain, kv_gain, lay, name):
    Lp = z.shape[0]
    QL, KVL = lay['QL'], lay['KVL']
    tm = _div_tile(Lp, 256, 16)

    def one(dy, v, gain):
        r = _rms(v)
        w = dy * gain
        return r * w - v * (r * r * r) * jnp.mean(w * v, axis=-1, keepdims=True), jnp.sum(dy * v * r, axis=0, keepdims=True)

    def body(dq_ref, dk_ref, q_ref, kv_ref, qg_ref, kg_ref, o_ref, dqg_ref, dkg_ref):
        da, ga = one(dq_ref[...], q_ref[...], qg_ref[...])
        db, gb = one(dk_ref[...], kv_ref[...], kg_ref[...])
        o_ref[:, :QL] = da.astype(BF)
        o_ref[:, QL:] = db.astype(BF)

        @pl.when(pl.program_id(0) == 0)
        def _():
            dqg_ref[...] = ga
            dkg_ref[...] = gb

        @pl.when(pl.program_id(0) > 0)
        def _():
            dqg_ref[...] += ga
            dkg_ref[...] += gb

    return pl.pallas_call(
        body, name=name, grid=(Lp // tm,),
        in_specs=[_rb(tm, QL), _rb(tm, KVL), _rb(tm, QL, lay['cq'] // QL), _rb(tm, KVL, lay['ckv'] // KVL),
                  _fixed((1, QL)), _fixed((1, KVL))],
        out_specs=[_rb(tm, QL + KVL), _fixed((1, QL)), _fixed((1, KVL))],
        out_shape=[jax.ShapeDtypeStruct((Lp, QL + KVL), BF), jax.ShapeDtypeStruct((1, QL), F32),
                   jax.ShapeDtypeStruct((1, KVL), F32)],
        compiler_params=_cp(('arbitrary',)),
    )(dqn, dkn, z, z, q_gain, kv_gain)


def _qk_prep(q_raw, kv, z, cos, sin, lay, H, name):
    Lp = q_raw.shape[0]
    W = H * HEAD_PAD
    tm = _div_tile(Lp, 256, 16)

    def body(q_ref, kv_ref, kr_ref, c_ref, s_ref, qo_ref, ko_ref):
        c, s = c_ref[...], s_ref[...]

        def rope(t):
            return t * c + _rot_half(t) * s

        kpe = rope(kr_ref[...]).astype(BF)
        for h in range(H):
            b = h * HEAD_PAD
            qo_ref[:, b:b + 128] = (q_ref[:, b:b + 128] * SOFTMAX_SCALE).astype(BF)
            qo_ref[:, b + 128:b + 256] = (rope(q_ref[:, b + 128:b + 256]) * SOFTMAX_SCALE).astype(BF)
            ko_ref[:, b:b + 128] = kv_ref[:, b:b + 128]
            ko_ref[:, b + 128:b + 256] = kpe

    return pl.pallas_call(
        body, name=name, grid=(Lp // tm,),
        in_specs=[_rb(tm, W), _rb(tm, W), _rb(tm, 128, lay['kr'] // 128), _rb(tm, 128), _rb(tm, 128)],
        out_specs=[_rb(tm, W), _rb(tm, W)], out_shape=[jax.ShapeDtypeStruct((Lp, W), BF)] * 2,
        compiler_params=_cp(('parallel',)),
    )(q_raw, kv, z, cos, sin)


def _qk_prep_bwd(dQ, dK, dV, cos, sin, H, name):
    Lp = dQ.shape[0]
    W = H * HEAD_PAD
    tm = _div_tile(Lp, 256, 16)

    def body(dq_ref, dk_ref, dv_ref, c_ref, s_ref, qo_ref, kvo_ref, kro_ref):
        c, s = c_ref[...], s_ref[...]

        def unrope(t):
            return t * c - _rot_half(t * s)

        acc = jnp.zeros((tm, 128), F32)
        for h in range(H):
            b = h * HEAD_PAD
            qo_ref[:, b:b + 128] = (dq_ref[:, b:b + 128] * SOFTMAX_SCALE).astype(BF)
            qo_ref[:, b + 128:b + 256] = (unrope(dq_ref[:, b + 128:b + 256]) * SOFTMAX_SCALE).astype(BF)
            kvo_ref[:, b:b + 128] = dk_ref[:, b:b + 128].astype(BF)
            kvo_ref[:, b + 128:b + 256] = dv_ref[:, h * V_DIM:(h + 1) * V_DIM]
            acc += dk_ref[:, b + 128:b + 256]
        kro_ref[...] = unrope(acc).astype(BF)

    return pl.pallas_call(
        body, name=name, grid=(Lp // tm,),
        in_specs=[_rb(tm, W), _rb(tm, W), _rb(tm, H * V_DIM), _rb(tm, 128), _rb(tm, 128)],
        out_specs=[_rb(tm, W), _rb(tm, W), _rb(tm, 128)],
        out_shape=[jax.ShapeDtypeStruct((Lp, W), BF), jax.ShapeDtypeStruct((Lp, W), BF),
                   jax.ShapeDtypeStruct((Lp, 128), BF)],
        compiler_params=_cp(('parallel',)),
    )(dQ, dK, dV, cos, sin)


def _call_carrying(core, name, grid, in_specs, out_specs, out_shape, scratch, args, phases):
    n_in, n_out, n_scr = len(in_specs), len(out_specs), len(scratch)
    extra_in, extra_out, alias, sems = _phase_operands(phases, n_in, n_out)
    n_ei, n_eo = len(extra_in), len(extra_out)

    def body(*refs):
        ins, ein = refs[:n_in], refs[n_in:n_in + n_ei]
        outs = refs[n_in + n_ei:n_in + n_ei + n_out]
        eout = refs[n_in + n_ei + n_out:n_in + n_ei + n_out + n_eo]
        rest = refs[n_in + n_ei + n_out + n_eo:]
        scr, sem_refs = rest[:n_scr], rest[n_scr:]
        if phases:
            ids = [pl.program_id(d) for d in range(len(grid))]
            first, last = ids[0] == 0, ids[0] == grid[0] - 1
            for d in range(1, len(grid)):
                first, last = first & (ids[d] == 0), last & (ids[d] == grid[d] - 1)

            @pl.when(first)
            def _():
                _phase_start(phases, ein, eout, sem_refs)

        core(*ins, *outs, *scr)
        if phases:
            @pl.when(last)
            def _():
                _phase_finish(phases, ein, eout, sem_refs)

    outs = pl.pallas_call(
        body, name=name, grid=grid, in_specs=list(in_specs) + [ANY] * n_ei, out_specs=list(out_specs) + [ANY] * n_eo,
        out_shape=list(out_shape) + extra_out, input_output_aliases=alias, scratch_shapes=list(scratch) + sems,
        compiler_params=_cp(('arbitrary',) * len(grid)),
    )(*args, *extra_in)
    return list(outs[:n_out]), _phase_results(phases, outs[n_out:])


def _flash_fwd(Q, K, kv, H, name, phases=()):
    Lp = Q.shape[0]
    T = _div_tile(Lp, 256, 16)

    CH = FLASH_CHUNK * T

    def body(q_ref, k_ref, v_ref, o_ref, lse_ref, m_s, l_s, acc_s):
        i = pl.program_id(1)
        q = q_ref[...]
        m_s[...] = jnp.full((T, 1), NEG, F32)
        l_s[...] = jnp.zeros((T, 1), F32)
        acc_s[...] = jnp.zeros((T, V_DIM), F32)

        def step(start, width, masked):
            rows = pl.ds(start, width)
            s = lax.dot_general(q, k_ref[rows, :], _DIMS['nt'], preferred_element_type=F32)
            if masked:
                rr = lax.broadcasted_iota(jnp.int32, (T, width), 0) + i * T
                cc = lax.broadcasted_iota(jnp.int32, (T, width), 1) + start
                s = jnp.where(cc <= rr, s, NEG)
            m = m_s[...]
            m_new = jnp.maximum(m, jnp.max(s, axis=-1, keepdims=True))
            alpha = jnp.exp(m - m_new)
            p = jnp.exp(s - m_new)
            l_s[...] = alpha * l_s[...] + jnp.sum(p, axis=-1, keepdims=True)
            acc_s[...] = alpha * acc_s[...] + jnp.dot(p.astype(BF), v_ref[rows, :], preferred_element_type=F32)
            m_s[...] = m_new

        n_full = i // FLASH_CHUNK

        def full(cidx, carry):
            step(pl.multiple_of(cidx * CH, CH), CH, False)
            return carry

        lax.fori_loop(0, n_full, full, 0)
        for nb in range(1, FLASH_CHUNK + 1):
            @pl.when(i % FLASH_CHUNK == nb - 1)
            def _(nb=nb):
                step(pl.multiple_of(n_full * CH, CH), nb * T, True)

        l = l_s[...]
        o_ref[...] = (acc_s[...] / l).astype(o_ref.dtype)
        lse_ref[...] = jnp.broadcast_to(m_s[...] + jnp.log(l), (T, 128))

    return _call_carrying(
        body, name, (H, Lp // T),
        [pl.BlockSpec((T, HEAD_PAD), lambda h, i: (i, h)), pl.BlockSpec((Lp, HEAD_PAD), lambda h, i: (0, h)),
         pl.BlockSpec((Lp, V_DIM), lambda h, i: (0, 2 * h + 1))],
        [pl.BlockSpec((T, V_DIM), lambda h, i: (i, h)), pl.BlockSpec((None, T, 128), lambda h, i: (h, i, 0))],
        [jax.ShapeDtypeStruct((Lp, H * V_DIM), BF), jax.ShapeDtypeStruct((H, Lp, 128), F32)],
        [pltpu.VMEM((T, 1), F32), pltpu.VMEM((T, 1), F32), pltpu.VMEM((T, V_DIM), F32)], (Q, K, kv), phases)


def _flash_bwd(Q, K, kv, O, dO, lse, H, name, phases=()):
    Lp = Q.shape[0]
    T = _div_tile(Lp, 256, 16)
    n_t = Lp // T

    def body(q_ref, k_ref, v_ref, o_ref, do_ref, lse_ref, dq_ref, dk_ref, dv_ref, dk_acc, dv_acc):
        j = pl.program_id(1)

        @pl.when(j == 0)
        def _():
            dq_ref[...] = jnp.zeros_like(dq_ref)

        kj, vj = k_ref[...], v_ref[...]
        dk_acc[...] = jnp.zeros_like(dk_acc)
        dv_acc[...] = jnp.zeros_like(dv_acc)

        def step(start, width, masked):
            rows = pl.ds(start, width)
            qi, doi = q_ref[rows, :], do_ref[rows, :]
            delta = jnp.sum(doi.astype(F32) * o_ref[rows, :].astype(F32), axis=-1, keepdims=True)
            s = lax.dot_general(qi, kj, _DIMS['nt'], preferred_element_type=F32)
            p = jnp.exp(s - lse_ref[rows, :][:, :1])
            if masked:
                rr = lax.broadcasted_iota(jnp.int32, (width, T), 0) + start
                cc = lax.broadcasted_iota(jnp.int32, (width, T), 1) + j * T
                p = jnp.where(cc <= rr, p, 0.0)
            dp = lax.dot_general(doi, vj, _DIMS['nt'], preferred_element_type=F32)
            ds = (p * (dp - delta)).astype(BF)
            dv_acc[...] += lax.dot_general(p.astype(BF), doi, _DIMS['tn'], preferred_element_type=F32)
            dk_acc[...] += lax.dot_general(ds, qi, _DIMS['tn'], preferred_element_type=F32)
            dq_ref[rows, :] += jnp.dot(ds, kj, preferred_element_type=F32)

        head = (n_t - 1 - j) % FLASH_CHUNK + 1
        for nb in range(1, FLASH_CHUNK + 1):
            @pl.when(head == nb)
            def _(nb=nb):
                step(pl.multiple_of(j * T, T), nb * T, True)

        def full(cidx, carry):
            step(pl.multiple_of((j + head + cidx * FLASH_CHUNK) * T, T), FLASH_CHUNK * T, False)
            return carry

        lax.fori_loop(0, (n_t - j - head) // FLASH_CHUNK, full, 0)
        dk_ref[...] = dk_acc[...]
        dv_ref[...] = dv_acc[...].astype(dv_ref.dtype)

    head_q = pl.BlockSpec((Lp, HEAD_PAD), lambda h, j: (0, h))
    head_v = pl.BlockSpec((Lp, V_DIM), lambda h, j: (0, h))
    return _call_carrying(
        body, name, (H, n_t),
        [head_q, pl.BlockSpec((T, HEAD_PAD), lambda h, j: (j, h)), pl.BlockSpec((T, V_DIM), lambda h, j: (j, 2 * h + 1)),
         head_v, head_v, pl.BlockSpec((None, Lp, 128), lambda h, j: (h, 0, 0))],
        [head_q, pl.BlockSpec((T, HEAD_PAD), lambda h, j: (j, h)), pl.BlockSpec((T, V_DIM), lambda h, j: (j, h))],
        [jax.ShapeDtypeStruct((Lp, H * HEAD_PAD), F32), jax.ShapeDtypeStruct((Lp, H * HEAD_PAD), F32),
         jax.ShapeDtypeStruct((Lp, H * V_DIM), BF)],
        [pltpu.VMEM((T, HEAD_PAD), F32), pltpu.VMEM((T, V_DIM), F32)], (Q, K, kv, O, dO, lse), phases)


def _gate_fwd(z, y_pool, y_mla, lay, name):
    Lp, D = y_pool.shape
    tm = _div_tile(Lp, 256, 16)

    def body(gp_ref, gm_ref, yp_ref, ym_ref, o_ref):
        o_ref[...] = (jax.nn.sigmoid(gp_ref[...]) * yp_ref[...] + jax.nn.sigmoid(gm_ref[...]) * ym_ref[...]).astype(BF)

    return pl.pallas_call(
        body, name=name, grid=(Lp // tm,),
        in_specs=[_rb(tm, D, lay['gp'] // D), _rb(tm, D, lay['gm'] // D), _rb(tm, D), _rb(tm, D)], out_specs=_rb(tm, D),
        out_shape=jax.ShapeDtypeStruct((Lp, D), BF), compiler_params=_cp(('parallel',)),
    )(z, z, y_pool, y_mla)


def _gate_bwd(dy, z, y_pool, y_mla, lay, name):
    Lp, D = y_pool.shape
    tm = _div_tile(Lp, 256, 16)

    def body(dy_ref, gp_ref, gm_ref, yp_ref, ym_ref, dp_ref, dm_ref, dg_ref):
        dy_ = dy_ref[...]
        sp, sm = jax.nn.sigmoid(gp_ref[...]), jax.nn.sigmoid(gm_ref[...])
        dp_ref[...] = (dy_ * sp).astype(BF)
        dm_ref[...] = (dy_ * sm).astype(BF)
        dg_ref[:, :D] = (dy_ * yp_ref[...] * (sp * (1.0 - sp))).astype(BF)
        dg_ref[:, D:] = (dy_ * ym_ref[...] * (sm * (1.0 - sm))).astype(BF)

    return pl.pallas_call(
        body, name=name, grid=(Lp // tm,),
        in_specs=[_rb(tm, D), _rb(tm, D, lay['gp'] // D), _rb(tm, D, lay['gm'] // D), _rb(tm, D), _rb(tm, D)],
        out_specs=[_rb(tm, D), _rb(tm, D), _rb(tm, 2 * D)],
        out_shape=[jax.ShapeDtypeStruct((Lp, D), BF), jax.ShapeDtypeStruct((Lp, D), BF),
                   jax.ShapeDtypeStruct((Lp, 2 * D), BF)],
        compiler_params=_cp(('parallel',)),
    )(dy, z, z, y_pool, y_mla)


def _z_layout(D, QL, KVL):
    cq = POOL_WIDTH
    ckv = cq + QL
    gp = -(-(ckv + KVL) // D) * D
    gm = gp + D
    kr = gm + D
    return dict(QL=QL, KVL=KVL, cq=cq, ckv=ckv, gp=gp, gm=gm, kr=kr, width=kr + 128)


def _w_in_aligned(w4, lay, D):
    w = w4.shape[2]
    n0 = POOL_WIDTH + lay['QL'] + lay['KVL']

    def cols(a, b):
        return [w4[k][:, max(a, k * w) - k * w:min(b, (k + 1) * w) - k * w] for k in range(N_CHIPS)
                if max(a, k * w) < min(b, (k + 1) * w)]

    parts = (cols(0, n0) + [jnp.zeros((D, lay['gp'] - n0), w4.dtype)] + cols(n0 + QK_ROPE, N_CHIPS * w) +
             cols(n0, n0 + QK_ROPE) + [jnp.zeros((D, 128 - QK_ROPE), w4.dtype)])
    return jnp.concatenate(parts, axis=1)


def _w_in_slabs(w_al, lay, D, w):
    n0 = POOL_WIDTH + lay['QL'] + lay['KVL']
    segs = [(0, n0, 0), (n0, n0 + QK_ROPE, lay['kr']), (n0 + QK_ROPE, N_CHIPS * w, lay['gp'])]

    def slab(k):
        parts = []
        for a, b, at in segs:
            lo, hi = max(a, k * w), min(b, (k + 1) * w)
            if lo < hi:
                parts.append(w_al[:, at + lo - a:at + hi - a])
        return jnp.concatenate(parts, axis=1)

    return jnp.stack([slab(k) for k in range(N_CHIPS)], axis=0)


def kernel(x, meta_tokens, norm_ffn1_pre, norm_ffn1_post, ffn1_w_gu, ffn1_w_down, norm_mix_pre, norm_mix_post, w_in, pool_w, pool_scale, w_pool_o, q_a_norm, w_q_b, kv_a_norm, w_kv_b, w_mla_o, w_out, norm_ffn2_pre, norm_ffn2_post, ffn2_w_gu, ffn2_w_down, loss_target, m_meta_tokens, m_norm_ffn1_pre, m_norm_ffn1_post, m_ffn1_w_gu, m_ffn1_w_down, m_norm_mix_pre, m_norm_mix_post, m_w_in, m_pool_w, m_pool_scale, m_w_pool_o, m_q_a_norm, m_w_q_b, m_kv_a_norm, m_w_kv_b, m_w_mla_o, m_w_out, m_norm_ffn2_pre, m_norm_ffn2_post, m_ffn2_w_gu, m_ffn2_w_down, v_meta_tokens, v_norm_ffn1_pre, v_norm_ffn1_post, v_ffn1_w_gu, v_ffn1_w_down, v_norm_mix_pre, v_norm_mix_post, v_w_in, v_pool_w, v_pool_scale, v_w_pool_o, v_q_a_norm, v_w_q_b, v_kv_a_norm, v_w_kv_b, v_w_mla_o, v_w_out, v_norm_ffn2_pre, v_norm_ffn2_post, v_ffn2_w_gu, v_ffn2_w_down):
    given = dict(locals())
    W = {n: given[n] for n in WEIGHTS}
    M = {n: given['m_' + n] for n in WEIGHTS}
    V = {n: given['v_' + n] for n in WEIGHTS}

    S, D = x.shape[1], x.shape[2]
    NM = meta_tokens.shape[0]
    L = NM + S
    Lp = -(-L // ROW_ALIGN) * ROW_ALIGN
    QL, KVL = w_q_b.shape[1], w_kv_b.shape[1]
    H = w_q_b.shape[2] * N_CHIPS // QK_DIM
    lay = _z_layout(D, QL, KVL)
    gx, gy = lax.axis_index('x'), lax.axis_index('y')
    chip = 2 * gx + gy

    names = list(BIG)
    slab = {n: W[n][0] for n in names}
    kind = dict(BIG, meta_tokens='col')
    slab_shape = {n: tuple(slab[n].shape) for n in names}
    gc = lax.axis_index('c')
    c_arr = jnp.stack([gc]).astype(jnp.int32)
    place_arr = jnp.stack([gc, chip]).astype(jnp.int32)

    full = {n: _cast_place(slab[n], BIG[n], place_arr, BF, f'place_{n}') for n in names if n != 'ffn1_w_gu'}
    full['meta_tokens'] = _cast_place(meta_tokens, 'col', place_arr, F32, 'place_meta_tokens')
    G0 = ['meta_tokens']
    G0B = ['ffn1_w_down']
    G1W = ['w_in']
    G1R = ['pool_w', 'w_pool_o', 'w_q_b', 'w_kv_b']
    G2 = ['w_mla_o', 'w_out']
    G3 = ['ffn2_w_gu', 'ffn2_w_down']

    def gather(phase_fn, group):
        return phase_fn([full[n] for n in group], [kind[n] for n in group])

    def arrived(group, res):
        full.update(zip(group, res))

    arrived(G0, _all_gather([full[n] for n in G0], [kind[n] for n in G0]))
    meta_full = full['meta_tokens']

    pos = jnp.arange(Lp, dtype=F32)
    inv = ROPE_THETA ** (-jnp.arange(0, QK_ROPE, 2, dtype=F32) / QK_ROPE)
    ang = pos[:, None] * inv[None, :]
    ang = jnp.concatenate([ang, ang], axis=-1)
    cos = jnp.pad(jnp.cos(ang), ((0, 0), (0, 128 - QK_ROPE)), constant_values=1.0)
    sin = jnp.pad(jnp.sin(ang), ((0, 0), (0, 128 - QK_ROPE)))

    h0 = jnp.concatenate([meta_full, x[0], jnp.zeros((Lp - L, D), F32)], axis=0)
    tgt = jnp.pad(loss_target[0], ((NM, Lp - L), (0, 0)))

    n1 = _norm_fwd(h0, norm_ffn1_pre, 'ffn1_norm')
    F2 = ffn1_w_gu.shape[2] * N_CHIPS
    w_gu_own = _elementwise(lambda v: (v,), [slab['ffn1_w_gu']], [None], slab_shape['ffn1_w_gu'], [BF],
                            'cast_ffn1_w_gu')[0]

    def gather_both(group, peers):
        return _phase_gather_both([full[n] for n in group], [kind[n] for n in group], peers)

    gu1, ((w_gu_1, w_gu_2),) = _mm_slab(n1, w_gu_own, None, 0, F2, place_arr, 'ffn1_gu_0',
                                        [_phase_slab_gather(w_gu_own, (1, 2))])
    gu1, ((w_gu_3,), res_b) = _mm_slab(n1, w_gu_1, gu1, 1, F2, place_arr, 'ffn1_gu_1',
                                       [_phase_slab_gather(w_gu_own, (3,)), gather_both(G0B, (1, 2))])
    arrived(G0B, res_b)
    gu1, (res_b,) = _mm_slab(n1, w_gu_2, gu1, 2, F2, place_arr, 'ffn1_gu_2', [gather_both(G0B, (3,))])
    arrived(G0B, res_b)
    gu1, (res,) = _mm_slab(n1, w_gu_3, gu1, 3, F2, place_arr, 'ffn1_gu_3', [gather(_phase_gather_ici, G1R)])
    arrived(G1R, res)
    ws = w_gu_own.shape[1]
    full['ffn1_w_gu'] = jnp.zeros((D, F2), BF)
    for s, part in enumerate((w_gu_own, w_gu_1, w_gu_2, w_gu_3)):
        full['ffn1_w_gu'] = lax.dynamic_update_slice(full['ffn1_w_gu'], part, (0, jnp.bitwise_xor(chip, s) * ws))
    (a1,), (res,) = _swiglu_fwd(gu1, 'ffn1_act', [gather(_phase_gather_d2d, G1R)])
    arrived(G1R, res)
    f1, (res_w,) = _mm(a1, full['ffn1_w_down'], 'nn', F32, 'ffn1_down', [gather(_phase_gather_ici, G1W)])
    arrived(G1W, res_w)
    h1, (res_w,) = _post_residual(h0, f1, norm_ffn1_post, 0.5, 'ffn1_res', [gather(_phase_gather_d2d, G1W)])
    arrived(G1W, res_w)

    w_in_al = _w_in_aligned(full['w_in'], lay, D)
    w_q_pad = jnp.pad(full['w_q_b'].reshape(QL, H, QK_DIM), ((0, 0), (0, 0), (0, HEAD_PAD - QK_DIM))).reshape(
        QL, H * HEAD_PAD)

    n2 = _norm_fwd(h1, norm_mix_pre, 'mix_norm')
    z, (res,) = _mm(n2, w_in_al, 'nn', F32, 'mix_in', [gather(_phase_gather_ici, G2)])
    arrived(G2, res)
    d_pool = _pool_fwd(z, 'pool_fwd')
    e_pool, yp = _pool_mix_fwd(d_pool, full['pool_w'], pool_scale, 'pool_mix')
    y_pool = _mm(yp, full['w_pool_o'], 'nn', F32, 'pool_out')
    cqn, ckvn = _lora_norms(z, q_a_norm, kv_a_norm, lay, 'lora_norms')
    q_raw = _mm(cqn, w_q_pad, 'nn', F32, 'mla_q')
    kv = _mm(ckvn, full['w_kv_b'], 'nn', BF, 'mla_kv')
    Q, K = _qk_prep(q_raw, kv, z, cos, sin, lay, H, 'qk_prep')
    (O, lse), (res2, res3) = _flash_fwd(Q, K, kv, H, 'flash_fwd',
                                        [gather(_phase_gather_d2d, G2), gather(_phase_gather_ici, G3)])
    arrived(G2, res2)
    arrived(G3, res3)
    y_mla, (res,) = _mm(O, full['w_mla_o'], 'nn', F32, 'mla_out', [gather(_phase_gather_d2d, G3)])
    arrived(G3, res)
    y = _gate_fwd(z, y_pool, y_mla, lay, 'gate')
    m_mix = _mm(y, full['w_out'], 'nn', F32, 'mix_out')
    h2, _ = _post_residual(h1, m_mix, norm_mix_post, 1.0, 'mix_res')

    n3 = _norm_fwd(h2, norm_ffn2_pre, 'ffn2_norm')
    gu2 = _mm(n3, full['ffn2_w_gu'], 'nn', BF, 'ffn2_gu')
    (a2,), _ = _swiglu_fwd(gu2, 'ffn2_act')
    f2 = _mm(a2, full['ffn2_w_down'], 'nn', F32, 'ffn2_down')
    h3, _ = _post_residual(h2, f2, norm_ffn2_post, 0.5, 'ffn2_res')

    G, theirs, sums, halves, reduced = {}, {}, {}, {}, {}
    RA = ['ffn2_w_down', 'ffn2_w_gu']
    RB1 = ['w_out', 'w_pool_o', 'pool_w', 'w_mla_o']
    RB2 = ['w_q_b', 'w_kv_b']
    RB3 = ['w_in']
    RC1 = ['ffn1_w_down']
    RC2 = ['ffn1_w_gu']

    def pair_phase(group):
        return _phase_pair([G[n] for n in group], [BIG[n] for n in group], [slab_shape[n] for n in group])

    def pair_sums(group, res):
        for n, t in zip(group, res):
            sums[n] = _pair_sum(G[n], t, BIG[n], slab_shape[n], c_arr, f'rs_pair_sum_{n}')

    def chip_phase(group):
        return _phase_chip([sums[n] for n in group])

    def chip_sums(group, res):
        for n, ld in zip(group, res):
            halves[n] = _chip_sum(sums[n], ld, BIG[n], slab_shape[n], place_arr, f'rs_chip_sum_{n}')

    def final_phase(group):
        return _phase_final([halves[n] for n in group], [BIG[n] for n in group])

    dh3, sq = _loss_bwd(h3, tgt, NM, S, 'loss')

    df2, G['norm_ffn2_post'] = _post_bwd(dh3, f2, norm_ffn2_post, 0.5, 'ffn2_res_bwd')
    G['ffn2_w_down'] = _mm(a2, df2, 'tn', BF, 'ffn2_dw_down')
    da2 = _mm(df2, full['ffn2_w_down'], 'nt', BF, 'ffn2_da')
    dgu2 = _swiglu_bwd(da2, gu2, 'ffn2_act_bwd')
    G['ffn2_w_gu'] = _mm(n3, dgu2, 'tn', BF, 'ffn2_dw_gu')
    dn3, (res,) = _mm(dgu2, full['ffn2_w_gu'], 'nt', F32, 'ffn2_dn', [pair_phase(RA)])
    (dh2, G['norm_ffn2_pre']), _ = _pre_bwd(dh3, dn3, h2, norm_ffn2_pre, 'ffn2_norm_bwd')
    pair_sums(RA, res)

    dm, G['norm_mix_post'] = _post_bwd(dh2, m_mix, norm_mix_post, 1.0, 'mix_res_bwd')
    G['w_out'] = _mm(y, dm, 'tn', BF, 'dw_out')
    dy = _mm(dm, full['w_out'], 'nt', F32, 'mix_out_bwd')
    dy_pool, dy_mla, d_gate = _gate_bwd(dy, z, y_pool, y_mla, lay, 'gate_bwd')
    G['w_pool_o'] = _mm(yp, dy_pool, 'tn', BF, 'dw_pool_o')
    dyp = _mm(dy_pool, full['w_pool_o'], 'nt', F32, 'pool_out_bwd')
    dd, G['pool_scale'], d_pool_w = _pool_mix_bwd(dyp, e_pool, d_pool, full['pool_w'], pool_scale, 'pool_mix_bwd')
    G['pool_w'] = d_pool_w.astype(BF)
    du_pool = _pool_bwd(dd, 'pool_bwd')
    G['w_mla_o'] = _mm(O, dy_mla, 'tn', BF, 'dw_mla_o')
    dO, (res,) = _mm(dy_mla, full['w_mla_o'], 'nt', BF, 'mla_out_bwd', [pair_phase(RB1)])
    pair_sums(RB1, res)
    (dQ, dK, dV), (res, res_b) = _flash_bwd(Q, K, kv, O, dO, lse, H, 'flash_bwd', [chip_phase(RA), chip_phase(RB1)])
    chip_sums(RA, res)
    chip_sums(RB1, res_b)
    dq_raw, dkv, dkr = _qk_prep_bwd(dQ, dK, dV, cos, sin, H, 'qk_prep_bwd')
    d_w_q_pad, (res_b,) = _mm(cqn, dq_raw, 'tn', BF, 'dw_q_b', [final_phase(RB1)])
    reduced.update(zip(RB1, res_b))
    G['w_q_b'] = d_w_q_pad.reshape(QL, H, HEAD_PAD)[:, :, :QK_DIM].reshape(QL, H * QK_DIM)
    dcqn = _mm(dq_raw, w_q_pad, 'nt', F32, 'mla_q_bwd')
    G['w_kv_b'] = _mm(ckvn, dkv, 'tn', BF, 'dw_kv_b')
    dckvn, (res,) = _mm(dkv, full['w_kv_b'], 'nt', F32, 'mla_kv_bwd', [pair_phase(RB2)])
    pair_sums(RB2, res)
    d_lora, G['q_a_norm'], G['kv_a_norm'] = _lora_norms_bwd(dcqn, dckvn, z, q_a_norm, kv_a_norm, lay, 'lora_norms_bwd')
    n0 = POOL_WIDTH + QL + KVL
    dz = jnp.concatenate([du_pool, d_lora, jnp.zeros((Lp, lay['gp'] - n0), BF), d_gate, dkr], axis=1)
    d_w_in_al, (res, res_b) = _mm(n2, dz, 'tn', BF, 'dw_in', [final_phase(RA), chip_phase(RB2)])
    reduced.update(zip(RA, res))
    chip_sums(RB2, res_b)
    G['w_in'] = _w_in_slabs(d_w_in_al, lay, D, w_in.shape[2])
    dn2, (res, res_b) = _mm(dz, w_in_al, 'nt', F32, 'mix_in_bwd', [pair_phase(RB3), final_phase(RB2)])
    reduced.update(zip(RB2, res_b))
    (dh1, G['norm_mix_pre']), _ = _pre_bwd(dh2, dn2, h1, norm_mix_pre, 'mix_norm_bwd')
    pair_sums(RB3, res)

    df1, G['norm_ffn1_post'] = _post_bwd(dh1, f1, norm_ffn1_post, 0.5, 'ffn1_res_bwd')
    G['ffn1_w_down'], (res_b,) = _mm(a1, df1, 'tn', BF, 'ffn1_dw_down', [chip_phase(RB3)])
    chip_sums(RB3, res_b)
    da1, (res, res_b) = _mm(df1, full['ffn1_w_down'], 'nt', BF, 'ffn1_da', [pair_phase(RC1), final_phase(RB3)])
    reduced.update(zip(RB3, res_b))
    dgu1 = _swiglu_bwd(da1, gu1, 'ffn1_act_bwd')
    pair_sums(RC1, res)
    G['ffn1_w_gu'], (res_c1,) = _mm(n1, dgu1, 'tn', BF, 'ffn1_dw_gu', [chip_phase(RC1)])
    chip_sums(RC1, res_c1)
    (res,) = _run_phases([pair_phase(RC2)], 'rs_pair_exchange_tail')
    pair_sums(RC2, res)
    dn1, (res_c2, res_c1) = _mm(dgu1, full['ffn1_w_gu'], 'nt', F32, 'ffn1_dn', [chip_phase(RC2), final_phase(RC1)])
    chip_sums(RC2, res_c2)
    reduced.update(zip(RC1, res_c1))
    (dh0, G['norm_ffn1_pre']), (res_c2,) = _pre_bwd(dh1, dn1, h0, norm_ffn1_pre, 'ffn1_norm_bwd', [final_phase(RC2)])
    reduced.update(zip(RC2, res_c2))
    grad_x = dh0[NM:L][None]

    SW = max(D, POOL_WIDTH)

    def widen(a, fill=0.0):
        return jnp.pad(a, ((0, 0), (0, SW - a.shape[1])), constant_values=fill)

    rows = [widen(G[n]) for n in SMALL_VEC] + [widen(dh0[:NM]), widen(sq)]
    n_rows = len(SMALL_VEC) + NM + 1
    pad_rows = -(-n_rows // 8) * 8 - n_rows
    small = _all_reduce_small(jnp.concatenate(rows + [jnp.zeros((pad_rows, SW), F32)], axis=0))
    loss = (0.5 / D) * small[len(SMALL_VEC) + NM, 0]
    for i, n in enumerate(SMALL_VEC):
        reduced[n] = small[i:i + 1, :G[n].shape[1]]
    mw = meta_tokens.shape[1]
    reduced['meta_tokens'] = lax.dynamic_slice(small, (len(SMALL_VEC), chip * mw), (NM, mw))

    grads, deltas, new_m, new_v = {}, {}, {}, {}
    for n in names:
        w = slab[n]
        deltas[n], new_m[n], new_v[n] = [o[None] for o in _adam(w, reduced[n], M[n][0], V[n][0], f'adam_{n}')]
        grads[n] = reduced[n][None]
    n_vec = len(SMALL_VEC)
    vec_w = jnp.concatenate([widen(W[n]) for n in SMALL_VEC] + [jnp.zeros((16 - n_vec, SW), F32)], axis=0)
    vec_m = jnp.concatenate([widen(M[n]) for n in SMALL_VEC] + [jnp.zeros((16 - n_vec, SW), F32)], axis=0)
    vec_v = jnp.concatenate([widen(V[n], 1.0) for n in SMALL_VEC] + [jnp.ones((16 - n_vec, SW), F32)], axis=0)
    vec_g = jnp.concatenate([small[:n_vec], jnp.zeros((16 - n_vec, SW), F32)], axis=0)
    vd, vm, vv = _adam(vec_w, vec_g, vec_m, vec_v, 'adam_vectors')
    for i, n in enumerate(SMALL_VEC):
        wdt = W[n].shape[1]
        grads[n], deltas[n], new_m[n], new_v[n] = reduced[n], vd[i:i + 1, :wdt], vm[i:i + 1, :wdt], vv[i:i + 1, :wdt]
    grads['meta_tokens'] = reduced['meta_tokens']
    deltas['meta_tokens'], new_m['meta_tokens'], new_v['meta_tokens'] = _adam(
        meta_tokens, reduced['meta_tokens'], m_meta_tokens, v_meta_tokens, 'adam_meta')

    return (loss, grad_x, *[grads[n] for n in WEIGHTS], *[deltas[n] for n in WEIGHTS], *[new_m[n] for n in WEIGHTS],
            *[new_v[n] for n in WEIGHTS])
```

```python
import functools

import jax
import jax.numpy as jnp
import numpy as np
from jax import lax
from jax.experimental import pallas as pl
from jax.experimental.pallas import tpu as pltpu

F32 = jnp.float32
BF = jnp.bfloat16
MESH = pl.DeviceIdType.MESH

EPS = 1e-6
N_CHIPS = 4
POOL_WINDOWS = (2, 4, 8, 16)
POOL_GROUP = 256
POOL_WIDTH = POOL_GROUP * len(POOL_WINDOWS)
QK_NOPE = 128
QK_ROPE = 64
V_DIM = 128
QK_DIM = QK_NOPE + QK_ROPE
HEAD_PAD = 256
ROPE_THETA = 10000.0
SOFTMAX_SCALE = QK_DIM ** -0.5
ADAM_LR = 0.001
ADAM_B1 = 0.9
ADAM_B2 = 0.999
ADAM_EPS = 1e-08
ADAM_WD = 0.01
ADAM_STEP = 10
ROW_ALIGN = 256
VMEM_LIMIT = 56 * 1024 * 1024
MM_TK_ROWS = 4352
MM_TK_COLS = 6400
NEG = -1e30
FLASH_FWD_CHUNK = 17
FLASH_BWD_CHUNK = 8
WEIGHTS = ['meta_tokens', 'norm_ffn1_pre', 'norm_ffn1_post', 'ffn1_w_gu', 'ffn1_w_down', 'norm_mix_pre',
           'norm_mix_post', 'w_in', 'pool_w', 'pool_scale', 'w_pool_o', 'q_a_norm', 'w_q_b', 'kv_a_norm', 'w_kv_b',
           'w_mla_o', 'w_out', 'norm_ffn2_pre', 'norm_ffn2_post', 'ffn2_w_gu', 'ffn2_w_down']
BIG = {'ffn1_w_gu': 'col', 'ffn1_w_down': 'row', 'w_in': 'lead', 'pool_w': 'pool', 'w_pool_o': 'col', 'w_q_b': 'col',
       'w_kv_b': 'col', 'w_mla_o': 'row', 'w_out': 'row', 'ffn2_w_gu': 'col', 'ffn2_w_down': 'row'}
SMALL_VEC = ['norm_ffn1_pre', 'norm_ffn1_post', 'norm_mix_pre', 'norm_mix_post', 'norm_ffn2_pre', 'norm_ffn2_post',
             'pool_scale', 'q_a_norm', 'kv_a_norm']


def _div_tile(n, target, align):
    best = None
    for t in range(align, min(n, target) + 1, align):
        if n % t == 0:
            best = t
    return best if best is not None else n


def _cp(sem=None):
    return pltpu.CompilerParams(dimension_semantics=sem, vmem_limit_bytes=VMEM_LIMIT)


def _full_shape(kind, slab):
    if kind == 'col':
        return (slab[0], slab[1] * N_CHIPS)
    if kind == 'row':
        return (slab[0] * N_CHIPS, slab[1])
    if kind == 'lead':
        return (N_CHIPS,) + tuple(slab)
    return (slab[0], slab[1] * N_CHIPS, slab[2])


def _half_shape(kind, slab):
    if kind == 'pool':
        return (slab[0], slab[1] // 2, slab[2])
    return (slab[0] // 2, slab[1])


def _half_of_slab(ref, kind, c):
    if kind == 'pool':
        n = ref.shape[1] // 2
        return ref.at[:, pl.ds(c * n, n), :]
    n = ref.shape[0] // 2
    return ref.at[pl.ds(c * n, n), :]


def _piece(ref, kind, k, c):
    if kind == 'col':
        r, w = ref.shape[0] // 2, ref.shape[1] // N_CHIPS
        return ref.at[pl.ds(c * r, r), pl.ds(k * w, w)]
    if kind == 'row':
        r = ref.shape[0] // (2 * N_CHIPS)
        return ref.at[pl.ds((2 * k + c) * r, r), :]
    if kind == 'lead':
        r = ref.shape[1] // 2
        return ref.at[k, pl.ds(c * r, r), :]
    r = ref.shape[1] // (2 * N_CHIPS)
    return ref.at[:, pl.ds((2 * k + c) * r, r), :]


def _slab(ref, kind, k):
    if kind == 'col':
        w = ref.shape[1] // N_CHIPS
        return ref.at[:, pl.ds(k * w, w)]
    if kind == 'row':
        r = ref.shape[0] // N_CHIPS
        return ref.at[pl.ds(k * r, r), :]
    if kind == 'lead':
        return ref.at[k]
    r = ref.shape[1] // N_CHIPS
    return ref.at[:, pl.ds(k * r, r), :]


def _place():
    x, y, c = lax.axis_index('x'), lax.axis_index('y'), lax.axis_index('c')
    return x, y, c


def _peer_chip(x, y, j):
    px = 1 - x if (j >> 1) else x
    py = 1 - y if (j & 1) else y
    return px, py


ANY = pl.BlockSpec(memory_space=pl.ANY)


def _all_gather(bufs, kinds):
    n = len(bufs)

    def body(*refs):
        outs = refs[n:2 * n]
        ssem, rsem, fssem, frsem = refs[2 * n:]
        x, y, c = _place()
        me = 2 * x + y
        sib = (x, y, 1 - c)
        sends = []
        for w in range(n):
            for j in (1, 2, 3):
                px, py = _peer_chip(x, y, j)
                mine = _piece(outs[w], kinds[w], me, c)
                sends.append(pltpu.make_async_remote_copy(
                    src_ref=mine, dst_ref=mine, send_sem=ssem.at[w, j - 1], recv_sem=rsem.at[w, j - 1],
                    device_id=(px, py, c), device_id_type=MESH))
        for cp in sends:
            cp.start()
        fwds = []
        for w in range(n):
            for j in (1, 2, 3):
                px, py = _peer_chip(x, y, j)
                got = _piece(outs[w], kinds[w], 2 * px + py, c)
                pltpu.make_async_remote_copy(src_ref=got, dst_ref=got, send_sem=ssem.at[w, j - 1],
                                             recv_sem=rsem.at[w, j - 1], device_id=(px, py, c),
                                             device_id_type=MESH).wait_recv()
                fwd = pltpu.make_async_remote_copy(src_ref=got, dst_ref=got, send_sem=fssem.at[w, j - 1],
                                                   recv_sem=frsem.at[w, j - 1], device_id=sib, device_id_type=MESH)
                fwd.start()
                fwds.append(fwd)
        for w in range(n):
            for j in (1, 2, 3):
                px, py = _peer_chip(x, y, j)
                other = _piece(outs[w], kinds[w], 2 * px + py, 1 - c)
                pltpu.make_async_remote_copy(src_ref=other, dst_ref=other, send_sem=fssem.at[w, j - 1],
                                             recv_sem=frsem.at[w, j - 1], device_id=sib,
                                             device_id_type=MESH).wait_recv()
        for cp in sends + fwds:
            cp.wait_send()

    return pl.pallas_call(
        body, name='all_gather_weights', out_shape=[jax.ShapeDtypeStruct(b.shape, b.dtype) for b in bufs],
        in_specs=[ANY] * n, out_specs=[ANY] * n, input_output_aliases={w: w for w in range(n)},
        scratch_shapes=[pltpu.SemaphoreType.DMA((n, 3)), pltpu.SemaphoreType.DMA((n, 3)),
                        pltpu.SemaphoreType.DMA((n, 3)), pltpu.SemaphoreType.DMA((n, 3))],
    )(*bufs)


def _remote(src, dst, ssem, rsem, k, dev):
    return pltpu.make_async_remote_copy(src_ref=src, dst_ref=dst, send_sem=ssem.at[k], recv_sem=rsem.at[k],
                                        device_id=dev, device_id_type=MESH)


def _phase_gather_ici(bufs, kinds):
    n = len(bufs)

    def build(ins, outs, ssem, rsem):
        x, y, c = _place()
        me = 2 * x + y
        ds = []
        for w in range(n):
            for j in (1, 2, 3):
                px, py = _peer_chip(x, y, j)
                mine = _piece(outs[w], kinds[w], me, c)
                got = _piece(outs[w], kinds[w], 2 * px + py, c)
                k = 3 * w + j - 1
                ds.append((_remote(mine, mine, ssem, rsem, k, (px, py, c)), _remote(got, got, ssem, rsem, k, (px, py, c))))
        return ds

    return dict(ins=list(bufs), outs=[jax.ShapeDtypeStruct(b.shape, b.dtype) for b in bufs],
                alias={w: w for w in range(n)}, nsem=3 * n, build=build)


def _phase_gather_d2d(bufs, kinds):
    n = len(bufs)

    def build(ins, outs, ssem, rsem):
        x, y, c = _place()
        sib = (x, y, 1 - c)
        ds = []
        for w in range(n):
            for j in (1, 2, 3):
                px, py = _peer_chip(x, y, j)
                have = _piece(outs[w], kinds[w], 2 * px + py, c)
                want = _piece(outs[w], kinds[w], 2 * px + py, 1 - c)
                k = 3 * w + j - 1
                ds.append((_remote(have, have, ssem, rsem, k, sib), _remote(want, want, ssem, rsem, k, sib)))
        return ds

    return dict(ins=list(bufs), outs=[jax.ShapeDtypeStruct(b.shape, b.dtype) for b in bufs],
                alias={w: w for w in range(n)}, nsem=3 * n, build=build)


def _phase_pair(grads, kinds, slabs):
    n = len(grads)

    def build(ins, outs, ssem, rsem):
        x, y, c = _place()
        sib = (x, y, 1 - c)
        ds = []
        for w in range(n):
            for k in range(N_CHIPS):
                cp = _remote(_piece(ins[w], kinds[w], k, 1 - c), outs[w].at[k], ssem, rsem, N_CHIPS * w + k, sib)
                ds.append((cp, cp))
        return ds

    return dict(ins=list(grads), alias={}, nsem=N_CHIPS * n, build=build,
                outs=[jax.ShapeDtypeStruct((N_CHIPS,) + _half_shape(k, s), g.dtype)
                      for g, k, s in zip(grads, kinds, slabs)])


def _phase_chip(sums):
    n = len(sums)

    def build(ins, outs, ssem, rsem):
        x, y, c = _place()
        ds = []
        for w in range(n):
            for j in (1, 2, 3):
                px, py = _peer_chip(x, y, j)
                cp = _remote(ins[w].at[2 * px + py], outs[w].at[j - 1], ssem, rsem, 3 * w + j - 1, (px, py, c))
                ds.append((cp, cp))
        return ds

    return dict(ins=list(sums), alias={}, nsem=3 * n, build=build,
                outs=[jax.ShapeDtypeStruct((3,) + s.shape[1:], s.dtype) for s in sums])


def _phase_final(slabs_half, kinds):
    n = len(slabs_half)

    def build(ins, outs, ssem, rsem):
        x, y, c = _place()
        sib = (x, y, 1 - c)
        ds = []
        for w in range(n):
            mine = _half_of_slab(outs[w], kinds[w], c)
            other = _half_of_slab(outs[w], kinds[w], 1 - c)
            ds.append((_remote(mine, mine, ssem, rsem, w, sib), _remote(other, other, ssem, rsem, w, sib)))
        return ds

    return dict(ins=list(slabs_half), outs=[jax.ShapeDtypeStruct(s.shape, s.dtype) for s in slabs_half],
                alias={w: w for w in range(n)}, nsem=n, build=build)


def _phase_operands(phases, n_main_in, n_main_out):
    ins, outs, alias, sems = [], [], {}, []
    for ph in phases:
        for i, o in ph['alias'].items():
            alias[n_main_in + len(ins) + i] = n_main_out + len(outs) + o
        ins += ph['ins']
        outs += ph['outs']
        sems += [pltpu.SemaphoreType.DMA((ph['nsem'],)), pltpu.SemaphoreType.DMA((ph['nsem'],))]
    return ins, outs, alias, sems


def _phase_copies(phases, in_refs, out_refs, sem_refs):
    ds, a, b = [], 0, 0
    for p, ph in enumerate(phases):
        ds += ph['build'](in_refs[a:a + len(ph['ins'])], out_refs[b:b + len(ph['outs'])], sem_refs[2 * p],
                          sem_refs[2 * p + 1])
        a += len(ph['ins'])
        b += len(ph['outs'])
    return ds


def _phase_start(phases, in_refs, out_refs, sem_refs):
    for send, _ in _phase_copies(phases, in_refs, out_refs, sem_refs):
        send.start()


def _phase_finish(phases, in_refs, out_refs, sem_refs):
    ds = _phase_copies(phases, in_refs, out_refs, sem_refs)
    for _, recv in ds:
        recv.wait_recv()
    for send, _ in ds:
        send.wait_send()


def _phase_results(phases, outs):
    res, b = [], 0
    for ph in phases:
        res.append(list(outs[b:b + len(ph['outs'])]))
        b += len(ph['outs'])
    return res


def _run_phases(phases, name):
    ins, out_shapes, alias, sems = _phase_operands(phases, 0, 0)
    n_in, n_out = len(ins), len(out_shapes)

    def body(*refs):
        in_refs, out_refs, sem_refs = refs[:n_in], refs[n_in:n_in + n_out], refs[n_in + n_out:]
        _phase_start(phases, in_refs, out_refs, sem_refs)
        _phase_finish(phases, in_refs, out_refs, sem_refs)

    outs = pl.pallas_call(body, name=name, out_shape=out_shapes, in_specs=[ANY] * n_in, out_specs=[ANY] * n_out,
                          input_output_aliases=alias, scratch_shapes=sems)(*ins)
    return _phase_results(phases, outs)


def _cast_place(shard, kind, place_arr, dtype, name):
    full = _full_shape(kind, shard.shape)
    if kind == 'pool':
        g, r, w = shard.shape
        grid = (g,)
        src = pl.BlockSpec((None, r, w), lambda i, s: (i, 0, 0))
        dst = pl.BlockSpec((None, r, w), lambda i, s: (i, s[1], 0))
    else:
        r, w = shard.shape
        tr = _div_tile(r, max(16, (1 << 19) // w), 16)
        nrb = r // tr
        grid = (nrb,)
        src = pl.BlockSpec((tr, w), lambda i, s: (i, 0))
        if kind == 'col':
            dst = pl.BlockSpec((tr, w), lambda i, s: (i, s[1]))
        elif kind == 'row':
            dst = pl.BlockSpec((tr, w), lambda i, s: (s[1] * nrb + i, 0))
        else:
            dst = pl.BlockSpec((None, tr, w), lambda i, s: (s[1], i, 0))

    def body(s_ref, x_ref, o_ref):
        o_ref[...] = x_ref[...].astype(o_ref.dtype)

    return pl.pallas_call(
        body, name=name,
        grid_spec=pltpu.PrefetchScalarGridSpec(num_scalar_prefetch=1, grid=grid, in_specs=[src], out_specs=dst),
        out_shape=jax.ShapeDtypeStruct(full, dtype), compiler_params=_cp(('parallel',)),
    )(place_arr, shard)


def _half_blocks(kind, slab):
    hs = _half_shape(kind, slab)
    if kind == 'pool':
        return (hs[0],), hs[1], 1
    tr = _div_tile(hs[0], max(16, (1 << 19) // hs[1]), 16)
    return (hs[0] // tr,), tr, hs[0] // tr


def _pair_sum(grad, theirs, kind, slab, c_arr, name):
    hs = _half_shape(kind, slab)
    (n_i,), tr, nrb = _half_blocks(kind, slab)
    if kind == 'pool':
        own = pl.BlockSpec((None, hs[1], hs[2]), lambda k, i, s: (i, 2 * k + s[0], 0))
        stk = pl.BlockSpec((None, None, hs[1], hs[2]), lambda k, i, s: (k, i, 0, 0))
    else:
        if kind == 'col':
            own = pl.BlockSpec((tr, hs[1]), lambda k, i, s: (s[0] * nrb + i, k))
        elif kind == 'row':
            own = pl.BlockSpec((tr, hs[1]), lambda k, i, s: ((2 * k + s[0]) * nrb + i, 0))
        else:
            own = pl.BlockSpec((None, tr, hs[1]), lambda k, i, s: (k, s[0] * nrb + i, 0))
        stk = pl.BlockSpec((None, tr, hs[1]), lambda k, i, s: (k, i, 0))

    def body(s_ref, a_ref, b_ref, o_ref):
        o_ref[...] = (a_ref[...].astype(F32) + b_ref[...].astype(F32)).astype(o_ref.dtype)

    return pl.pallas_call(
        body, name=name,
        grid_spec=pltpu.PrefetchScalarGridSpec(num_scalar_prefetch=1, grid=(N_CHIPS, n_i), in_specs=[own, stk],
                                               out_specs=stk),
        out_shape=jax.ShapeDtypeStruct((N_CHIPS,) + hs, BF), compiler_params=_cp(('parallel', 'parallel')),
    )(c_arr, grad, theirs)


def _chip_sum(sums, landed, kind, slab, place_arr, name):
    hs = _half_shape(kind, slab)
    (n_i,), tr, nrb = _half_blocks(kind, slab)
    if kind == 'pool':
        blk = (None, None, hs[1], hs[2])
        mine = pl.BlockSpec(blk, lambda i, s: (s[1], i, 0, 0))
        land = [pl.BlockSpec(blk, lambda i, s, j=j: (j, i, 0, 0)) for j in range(3)]
        out = pl.BlockSpec((None, hs[1], hs[2]), lambda i, s: (i, s[0], 0))
    else:
        blk = (None, tr, hs[1])
        mine = pl.BlockSpec(blk, lambda i, s: (s[1], i, 0))
        land = [pl.BlockSpec(blk, lambda i, s, j=j: (j, i, 0)) for j in range(3)]
        out = pl.BlockSpec((tr, hs[1]), lambda i, s: (s[0] * nrb + i, 0))

    def body(s_ref, a_ref, b_ref, c_ref, d_ref, o_ref):
        o_ref[...] = ((a_ref[...].astype(F32) + b_ref[...].astype(F32)) + c_ref[...].astype(F32)) + d_ref[...].astype(F32)

    return pl.pallas_call(
        body, name=name,
        grid_spec=pltpu.PrefetchScalarGridSpec(num_scalar_prefetch=1, grid=(n_i,), in_specs=[mine] + land, out_specs=out),
        out_shape=jax.ShapeDtypeStruct(tuple(slab), F32), compiler_params=_cp(('parallel',)),
    )(place_arr, sums, landed, landed, landed)


def _all_reduce_small(buf):
    rows, cols = buf.shape

    def body(in_ref, out_ref, land, ssem, rsem):
        x, y, c = _place()
        me = 4 * x + 2 * y + c
        land[me] = in_ref[...]
        started = []
        for j in range(1, 8):
            px = 1 - x if (j >> 2) & 1 else x
            py = 1 - y if (j >> 1) & 1 else y
            pc = 1 - c if j & 1 else c
            cp = pltpu.make_async_remote_copy(src_ref=in_ref, dst_ref=land.at[me], send_sem=ssem.at[j - 1],
                                              recv_sem=rsem.at[j - 1], device_id=(px, py, pc), device_id_type=MESH)
            cp.start()
            started.append(cp)
        for j in range(1, 8):
            px = 1 - x if (j >> 2) & 1 else x
            py = 1 - y if (j >> 1) & 1 else y
            pc = 1 - c if j & 1 else c
            slot = land.at[4 * px + 2 * py + pc]
            pltpu.make_async_remote_copy(src_ref=slot, dst_ref=slot, send_sem=ssem.at[j - 1], recv_sem=rsem.at[j - 1],
                                         device_id=(px, py, pc), device_id_type=MESH).wait_recv()
        for cp in started:
            cp.wait_send()
        acc = land[0]
        for d in range(1, 8):
            acc = acc + land[d]
        out_ref[...] = acc

    return pl.pallas_call(
        body, name='all_reduce_small', out_shape=jax.ShapeDtypeStruct((rows, cols), F32),
        in_specs=[pl.BlockSpec(memory_space=pltpu.VMEM)], out_specs=pl.BlockSpec(memory_space=pltpu.VMEM),
        scratch_shapes=[pltpu.VMEM((8, rows, cols), F32), pltpu.SemaphoreType.DMA((7,)), pltpu.SemaphoreType.DMA((7,))],
    )(buf)


def _elementwise(fn, ins, lead_index, out_shape, out_dtypes, name):
    nd = len(out_shape)
    r, cdim = out_shape[-2], out_shape[-1]
    tr = _div_tile(r, max(16, (1 << 19) // cdim), 16)
    grid = tuple(out_shape[:-2]) + (r // tr,)
    block = (None,) * (nd - 2) + (tr, cdim)

    def spec(lead):
        if lead is None:
            return pl.BlockSpec(block, lambda *g: tuple(g) + (0,))
        return pl.BlockSpec((None,) + block, lambda *g, lead=lead: (lead,) + tuple(g) + (0,))

    n_in = len(ins)

    def body(*refs):
        res = fn(*[r_[...] for r_ in refs[:n_in]])
        for o_ref, v in zip(refs[n_in:], res):
            o_ref[...] = v.astype(o_ref.dtype)

    return pl.pallas_call(
        body, name=name, grid=grid, in_specs=[spec(l) for l in lead_index],
        out_specs=[spec(None) for _ in out_dtypes],
        out_shape=[jax.ShapeDtypeStruct(tuple(out_shape), dt) for dt in out_dtypes],
        compiler_params=_cp(('parallel',) * len(grid)),
    )(*ins)


def _adam_fn(w, g, m, v):
    m = ADAM_B1 * m + (1.0 - ADAM_B1) * g
    v = ADAM_B2 * v + (1.0 - ADAM_B2) * (g * g)
    m_hat = m / (1.0 - ADAM_B1 ** ADAM_STEP)
    v_hat = v / (1.0 - ADAM_B2 ** ADAM_STEP)
    delta = -ADAM_LR * (m_hat / (jnp.sqrt(v_hat) + ADAM_EPS) + ADAM_WD * w)
    return g, delta, m, v


def _adam(w, g, m, v, name):
    return _elementwise(_adam_fn, [w, g, m, v], [None] * 4, w.shape, [F32] * 4, name)


_DIMS = {'nn': (((1,), (0,)), ((), ())), 'nt': (((1,), (1,)), ((), ())), 'tn': (((0,), (0,)), ((), ()))}


def _mm(a, b, mode, out_dtype, name, phases=()):
    if mode == 'nn':
        (M, K), N = a.shape, b.shape[1]
    elif mode == 'nt':
        (M, K), N = a.shape, b.shape[0]
    else:
        (K, M), N = a.shape, b.shape[1]
    if mode == 'tn':
        tm, tn, tk = _div_tile(M, 512, 128), _div_tile(N, 1024, 128), _div_tile(K, MM_TK_ROWS, 16)
    else:
        tk = _div_tile(K, MM_TK_COLS, 128)
        tm, tn = _div_tile(M, 1088, 16), _div_tile(N, 1024 if tk <= 2816 else 512, 128)
    nk = K // tk
    a_spec = {'nn': pl.BlockSpec((tm, tk), lambda i, j, k: (i, k)), 'nt': pl.BlockSpec((tm, tk), lambda i, j, k: (i, k)),
              'tn': pl.BlockSpec((tk, tm), lambda i, j, k: (k, i))}[mode]
    b_spec = {'nn': pl.BlockSpec((tk, tn), lambda i, j, k: (k, j)), 'nt': pl.BlockSpec((tn, tk), lambda i, j, k: (j, k)),
              'tn': pl.BlockSpec((tk, tn), lambda i, j, k: (k, j))}[mode]
    dims = _DIMS[mode]
    gm, gn = M // tm, N // tn
    extra_in, extra_out, alias, sems = _phase_operands(phases, 2, 1)
    n_ei, n_eo = len(extra_in), len(extra_out)

    def body(*refs):
        a_ref, b_ref, ein = refs[0], refs[1], refs[2:2 + n_ei]
        o_ref, eout = refs[2 + n_ei], refs[3 + n_ei:3 + n_ei + n_eo]
        acc_ref, sem_refs = refs[3 + n_ei + n_eo], refs[4 + n_ei + n_eo:]
        i, j, k = pl.program_id(0), pl.program_id(1), pl.program_id(2)
        if phases:
            @pl.when((i == 0) & (j == 0) & (k == 0))
            def _():
                _phase_start(phases, ein, eout, sem_refs)

        part = lax.dot_general(a_ref[...], b_ref[...], dims, preferred_element_type=F32)
        if nk == 1:
            o_ref[...] = part.astype(o_ref.dtype)
        else:
            @pl.when(k == 0)
            def _():
                acc_ref[...] = part

            @pl.when(k > 0)
            def _():
                acc_ref[...] += part

            @pl.when(k == nk - 1)
            def _():
                o_ref[...] = acc_ref[...].astype(o_ref.dtype)

        if phases:
            @pl.when((i == gm - 1) & (j == gn - 1) & (k == nk - 1))
            def _():
                _phase_finish(phases, ein, eout, sem_refs)

    outs = pl.pallas_call(
        body, name=name, grid=(gm, gn, nk), in_specs=[a_spec, b_spec] + [ANY] * n_ei,
        out_specs=[pl.BlockSpec((tm, tn), lambda i, j, k: (i, j))] + [ANY] * n_eo,
        out_shape=[jax.ShapeDtypeStruct((M, N), out_dtype)] + extra_out, input_output_aliases=alias,
        scratch_shapes=[pltpu.VMEM((tm, tn) if nk > 1 else (8, 128), F32)] + sems,
        compiler_params=_cp(('arbitrary',) * 3 if phases else ('parallel', 'parallel', 'arbitrary')),
    )(a, b, *extra_in)
    if phases:
        return outs[0], _phase_results(phases, outs[1:])
    return outs[0]


def _rb(tm, w, col=0):
    return pl.BlockSpec((tm, w), lambda i, col=col: (i, col))


def _fixed(shape):
    return pl.BlockSpec(shape, lambda i: (0,) * len(shape))


def _rms(x):
    return lax.rsqrt(jnp.mean(x * x, axis=-1, keepdims=True) + EPS)


def _norm_fwd(x, gain, name, width=None, col=0):
    Lp = x.shape[0]
    width = width or x.shape[1]
    tm = _div_tile(Lp, 256, 16)

    def body(x_ref, g_ref, o_ref):
        v = x_ref[...]
        o_ref[...] = (v * _rms(v) * g_ref[...]).astype(o_ref.dtype)

    return pl.pallas_call(
        body, name=name, grid=(Lp // tm,), in_specs=[_rb(tm, width, col), _fixed((1, width))], out_specs=_rb(tm, width),
        out_shape=jax.ShapeDtypeStruct((Lp, width), BF), compiler_params=_cp(('parallel',)),
    )(x, gain)


def _post_residual(h, f, gain, scale, name, phases=()):
    Lp, D = h.shape
    tm = _div_tile(Lp, 256, 16)

    def body(h_ref, f_ref, g_ref, o_ref):
        v = f_ref[...]
        o_ref[...] = h_ref[...] + scale * (v * _rms(v) * g_ref[...])

    (out,), res = _call_carrying(body, name, (Lp // tm,), [_rb(tm, D), _rb(tm, D), _fixed((1, D))], [_rb(tm, D)],
                                 [jax.ShapeDtypeStruct((Lp, D), F32)], [], (h, f, gain), phases)
    return out, res


def _post_bwd(dh, f, gain, scale, name):
    Lp, D = dh.shape
    tm = _div_tile(Lp, 256, 16)

    def body(dh_ref, f_ref, g_ref, df_ref, dg_ref):
        v = f_ref[...]
        r = _rms(v)
        dy = scale * dh_ref[...]
        w = dy * g_ref[...]
        df_ref[...] = (r * w - v * (r * r * r) * jnp.mean(w * v, axis=-1, keepdims=True)).astype(df_ref.dtype)
        part = jnp.sum(dy * v * r, axis=0, keepdims=True)

        @pl.when(pl.program_id(0) == 0)
        def _():
            dg_ref[...] = part

        @pl.when(pl.program_id(0) > 0)
        def _():
            dg_ref[...] += part

    return pl.pallas_call(
        body, name=name, grid=(Lp // tm,), in_specs=[_rb(tm, D), _rb(tm, D), _fixed((1, D))],
        out_specs=[_rb(tm, D), _fixed((1, D))],
        out_shape=[jax.ShapeDtypeStruct((Lp, D), BF), jax.ShapeDtypeStruct((1, D), F32)],
        compiler_params=_cp(('arbitrary',)),
    )(dh, f, gain)


def _pre_bwd(dres, dn, h, gain, name, phases=()):
    Lp, D = h.shape
    tm = _div_tile(Lp, 256, 16)

    def body(dres_ref, dn_ref, h_ref, g_ref, dh_ref, dg_ref):
        v = h_ref[...]
        r = _rms(v)
        dy = dn_ref[...]
        w = dy * g_ref[...]
        dh_ref[...] = dres_ref[...] + r * w - v * (r * r * r) * jnp.mean(w * v, axis=-1, keepdims=True)
        part = jnp.sum(dy * v * r, axis=0, keepdims=True)

        @pl.when(pl.program_id(0) == 0)
        def _():
            dg_ref[...] = part

        @pl.when(pl.program_id(0) > 0)
        def _():
            dg_ref[...] += part

    return _call_carrying(
        body, name, (Lp // tm,), [_rb(tm, D), _rb(tm, D), _rb(tm, D), _fixed((1, D))], [_rb(tm, D), _fixed((1, D))],
        [jax.ShapeDtypeStruct((Lp, D), F32), jax.ShapeDtypeStruct((1, D), F32)], [], (dres, dn, h, gain), phases)


def _swiglu_fwd(gu, name, phases=()):
    Lp, F2 = gu.shape
    F = F2 // 2
    tm = _div_tile(Lp, 256, 16)

    def body(gu_ref, a_ref):
        g = gu_ref[:, :F].astype(F32)
        u = gu_ref[:, F:].astype(F32)
        a_ref[...] = (g * jax.nn.sigmoid(g) * u).astype(a_ref.dtype)

    return _call_carrying(body, name, (Lp // tm,), [_rb(tm, F2)], [_rb(tm, F)],
                          [jax.ShapeDtypeStruct((Lp, F), BF)], [], (gu,), phases)


def _swiglu_bwd(da, gu, name):
    Lp, F2 = gu.shape
    F = F2 // 2
    tm = _div_tile(Lp, 256, 16)

    def body(da_ref, gu_ref, o_ref):
        g = gu_ref[:, :F].astype(F32)
        u = gu_ref[:, F:].astype(F32)
        da_ = da_ref[...].astype(F32)
        s = jax.nn.sigmoid(g)
        o_ref[:, :F] = (da_ * u * (s * (1.0 + g * (1.0 - s)))).astype(o_ref.dtype)
        o_ref[:, F:] = (da_ * (g * s)).astype(o_ref.dtype)

    return pl.pallas_call(
        body, name=name, grid=(Lp // tm,), in_specs=[_rb(tm, F), _rb(tm, F2)], out_specs=_rb(tm, F2),
        out_shape=jax.ShapeDtypeStruct((Lp, F2), BF), compiler_params=_cp(('parallel',)),
    )(da, gu)


def _tail(h, f, gain, scale, tgt, n_meta, n_real, name):
    Lp, D = h.shape
    tm = _div_tile(Lp, 256, 16)

    def body(h_ref, f_ref, g_ref, t_ref, d_ref, df_ref, dg_ref, l_ref):
        v = f_ref[...]
        r = _rms(v)
        gain_ = g_ref[...]
        row = lax.broadcasted_iota(jnp.int32, (tm, 1), 0) + pl.program_id(0) * tm
        ok = (row >= n_meta) & (row < n_meta + n_real)
        err = jnp.where(ok, h_ref[...] + scale * (v * r * gain_) - t_ref[...], 0.0)
        d_out = err / D
        d_ref[...] = d_out
        dy = scale * d_out
        w = dy * gain_
        df_ref[...] = (r * w - v * (r * r * r) * jnp.mean(w * v, axis=-1, keepdims=True)).astype(df_ref.dtype)
        dg_part = jnp.sum(dy * v * r, axis=0, keepdims=True)
        l_part = jnp.full((1, 128), jnp.sum(err * err), F32)

        @pl.when(pl.program_id(0) == 0)
        def _():
            dg_ref[...] = dg_part
            l_ref[...] = l_part

        @pl.when(pl.program_id(0) > 0)
        def _():
            dg_ref[...] += dg_part
            l_ref[...] += l_part

    return pl.pallas_call(
        body, name=name, grid=(Lp // tm,), in_specs=[_rb(tm, D), _rb(tm, D), _fixed((1, D)), _rb(tm, D)],
        out_specs=[_rb(tm, D), _rb(tm, D), _fixed((1, D)), _fixed((1, 128))],
        out_shape=[jax.ShapeDtypeStruct((Lp, D), F32), jax.ShapeDtypeStruct((Lp, D), BF),
                   jax.ShapeDtypeStruct((1, D), F32), jax.ShapeDtypeStruct((1, 128), F32)],
        compiler_params=_cp(('arbitrary',)),
    )(h, f, gain, tgt)


def _split_bf16(v):
    hi = v.astype(BF)
    return hi, (v - hi.astype(F32)).astype(BF)


def _pool_fwd(z, name):
    Lp = z.shape[0]
    T = _div_tile(Lp, 256, 16)
    G = len(POOL_WINDOWS)

    def body(cur_ref, prev_ref, d_ref):
        i, g = pl.program_id(0), pl.program_id(1)
        w = jnp.left_shift(2, g)
        rr = lax.broadcasted_iota(jnp.int32, (T, T), 0)
        cc = lax.broadcasted_iota(jnp.int32, (T, T), 1)
        b_cur = jnp.where((cc <= rr) & (cc > rr - w), 1.0, 0.0).astype(BF)
        w_prev = jnp.where(i > 0, w, 0)
        b_prev = jnp.where(cc - T > rr - w_prev, 1.0, 0.0).astype(BF)
        u = cur_ref[...]
        s = jnp.zeros((T, POOL_GROUP), F32)
        for part in _split_bf16(u):
            s += jnp.dot(b_cur, part, preferred_element_type=F32)
        for part in _split_bf16(prev_ref[...]):
            s += jnp.dot(b_prev, part, preferred_element_type=F32)
        t = lax.broadcasted_iota(jnp.int32, (T, 1), 0) + i * T
        cnt = jnp.minimum(w, t + 1).astype(F32)
        d_ref[...] = (s / cnt - u).astype(d_ref.dtype)

    return pl.pallas_call(
        body, name=name, grid=(Lp // T, G),
        in_specs=[pl.BlockSpec((T, POOL_GROUP), lambda i, g: (i, g)),
                  pl.BlockSpec((T, POOL_GROUP), lambda i, g: (jnp.maximum(i - 1, 0), g))],
        out_specs=pl.BlockSpec((T, POOL_GROUP), lambda i, g: (i, g)),
        out_shape=jax.ShapeDtypeStruct((Lp, POOL_WIDTH), BF), compiler_params=_cp(('parallel', 'parallel')),
    )(z, z)


def _pool_bwd(dd, name):
    Lp = dd.shape[0]
    T = _div_tile(Lp, 256, 16)
    G = len(POOL_WINDOWS)
    n_t = Lp // T

    def body(cur_ref, next_ref, o_ref):
        i, g = pl.program_id(0), pl.program_id(1)
        w = jnp.left_shift(2, g)
        rr = lax.broadcasted_iota(jnp.int32, (T, T), 0)
        cc = lax.broadcasted_iota(jnp.int32, (T, T), 1)
        b_cur = jnp.where((cc >= rr) & (cc < rr + w), 1.0, 0.0).astype(BF)
        w_next = jnp.where(i < n_t - 1, w, 0)
        b_next = jnp.where(cc + T < rr + w_next, 1.0, 0.0).astype(BF)
        t = lax.broadcasted_iota(jnp.int32, (T, 1), 0) + i * T
        cur = cur_ref[...]
        e_cur = cur / jnp.minimum(w, t + 1).astype(F32)
        e_next = next_ref[...] / jnp.minimum(w, t + T + 1).astype(F32)
        s = jnp.zeros((T, POOL_GROUP), F32)
        for part in _split_bf16(e_cur):
            s += jnp.dot(b_cur, part, preferred_element_type=F32)
        for part in _split_bf16(e_next):
            s += jnp.dot(b_next, part, preferred_element_type=F32)
        o_ref[...] = (s - cur).astype(o_ref.dtype)

    return pl.pallas_call(
        body, name=name, grid=(n_t, G),
        in_specs=[pl.BlockSpec((T, POOL_GROUP), lambda i, g: (i, g)),
                  pl.BlockSpec((T, POOL_GROUP), lambda i, g: (jnp.minimum(i + 1, n_t - 1), g))],
        out_specs=pl.BlockSpec((T, POOL_GROUP), lambda i, g: (i, g)),
        out_shape=jax.ShapeDtypeStruct((Lp, POOL_WIDTH), BF), compiler_params=_cp(('parallel', 'parallel')),
    )(dd, dd)


def _pool_mix_fwd(d, pool_w, scale, name):
    Lp = d.shape[0]
    G = len(POOL_WINDOWS)
    tm = _div_tile(Lp, 1088, 16)

    def body(d_ref, w_ref, s_ref, e_ref, y_ref):
        e = jnp.dot(d_ref[...], w_ref[...], preferred_element_type=F32)
        e_ref[...] = e.astype(e_ref.dtype)
        y_ref[...] = (e * s_ref[...]).astype(y_ref.dtype)

    blk = pl.BlockSpec((tm, POOL_GROUP), lambda g, i: (i, g))
    return pl.pallas_call(
        body, name=name, grid=(G, Lp // tm),
        in_specs=[blk, pl.BlockSpec((None, POOL_GROUP, POOL_GROUP), lambda g, i: (g, 0, 0)),
                  pl.BlockSpec((1, POOL_GROUP), lambda g, i: (0, g))],
        out_specs=[blk, blk], out_shape=[jax.ShapeDtypeStruct((Lp, POOL_WIDTH), BF)] * 2,
        compiler_params=_cp(('parallel', 'parallel')),
    )(d, pool_w, scale)


def _pool_mix_bwd(dyp, e, d, pool_w, scale, name):
    Lp = d.shape[0]
    G = len(POOL_WINDOWS)
    tm = _div_tile(Lp, 1088, 16)

    def body(dy_ref, e_ref, d_ref, w_ref, s_ref, dd_ref, ds_ref, dw_ref):
        i = pl.program_id(1)
        dy = dy_ref[...]
        de = (dy * s_ref[...]).astype(BF)
        dd_ref[...] = lax.dot_general(de, w_ref[...], _DIMS['nt'], preferred_element_type=F32)
        ds_part = jnp.sum(dy * e_ref[...].astype(F32), axis=0, keepdims=True)
        dw_part = lax.dot_general(d_ref[...], de, _DIMS['tn'], preferred_element_type=F32)

        @pl.when(i == 0)
        def _():
            ds_ref[...] = ds_part
            dw_ref[...] = dw_part

        @pl.when(i > 0)
        def _():
            ds_ref[...] += ds_part
            dw_ref[...] += dw_part

    blk = pl.BlockSpec((tm, POOL_GROUP), lambda g, i: (i, g))
    wblk = pl.BlockSpec((None, POOL_GROUP, POOL_GROUP), lambda g, i: (g, 0, 0))
    sblk = pl.BlockSpec((1, POOL_GROUP), lambda g, i: (0, g))
    return pl.pallas_call(
        body, name=name, grid=(G, Lp // tm), in_specs=[blk, blk, blk, wblk, sblk], out_specs=[blk, sblk, wblk],
        out_shape=[jax.ShapeDtypeStruct((Lp, POOL_WIDTH), F32), jax.ShapeDtypeStruct((1, POOL_WIDTH), F32),
                   jax.ShapeDtypeStruct((G, POOL_GROUP, POOL_GROUP), F32)],
        compiler_params=_cp(('parallel', 'arbitrary')),
    )(dyp, e, d, pool_w, scale)


def _rot_half(t):
    lane = lax.broadcasted_iota(jnp.int32, t.shape, 1)
    half = QK_ROPE // 2
    return jnp.where(lane < half, -pltpu.roll(t, 128 - half, 1), pltpu.roll(t, half, 1))


def _lora_norms(z, q_gain, kv_gain, lay, name):
    Lp = z.shape[0]
    QL, KVL = lay['QL'], lay['KVL']
    tm = _div_tile(Lp, 256, 16)

    def body(q_ref, kv_ref, qg_ref, kg_ref, qo_ref, ko_ref):
        a = q_ref[...]
        qo_ref[...] = (a * _rms(a) * qg_ref[...]).astype(BF)
        b = kv_ref[...]
        ko_ref[...] = (b * _rms(b) * kg_ref[...]).astype(BF)

    return pl.pallas_call(
        body, name=name, grid=(Lp // tm,),
        in_specs=[_rb(tm, QL, lay['cq'] // QL), _rb(tm, KVL, lay['ckv'] // KVL), _fixed((1, QL)), _fixed((1, KVL))],
        out_specs=[_rb(tm, QL), _rb(tm, KVL)],
        out_shape=[jax.ShapeDtypeStruct((Lp, QL), BF), jax.ShapeDtypeStruct((Lp, KVL), BF)],
        compiler_params=_cp(('parallel',)),
    )(z, z, q_gain, kv_gain)


def _lora_norms_bwd(dqn, dkn, z, q_gain, kv_gain, lay, name):
    Lp = z.shape[0]
    QL, KVL = lay['QL'], lay['KVL']
    tm = _div_tile(Lp, 256, 16)

    def one(dy, v, gain):
        r = _rms(v)
        w = dy * gain
        return r * w - v * (r * r * r) * jnp.mean(w * v, axis=-1, keepdims=True), jnp.sum(dy * v * r, axis=0, keepdims=True)

    def body(dq_ref, dk_ref, q_ref, kv_ref, qg_ref, kg_ref, o_ref, dqg_ref, dkg_ref):
        da, ga = one(dq_ref[...], q_ref[...], qg_ref[...])
        db, gb = one(dk_ref[...], kv_ref[...], kg_ref[...])
        o_ref[:, :QL] = da.astype(BF)
        o_ref[:, QL:] = db.astype(BF)

        @pl.when(pl.program_id(0) == 0)
        def _():
            dqg_ref[...] = ga
            dkg_ref[...] = gb

        @pl.when(pl.program_id(0) > 0)
        def _():
            dqg_ref[...] += ga
            dkg_ref[...] += gb

    return pl.pallas_call(
        body, name=name, grid=(Lp // tm,),
        in_specs=[_rb(tm, QL), _rb(tm, KVL), _rb(tm, QL, lay['cq'] // QL), _rb(tm, KVL, lay['ckv'] // KVL),
                  _fixed((1, QL)), _fixed((1, KVL))],
        out_specs=[_rb(tm, QL + KVL), _fixed((1, QL)), _fixed((1, KVL))],
        out_shape=[jax.ShapeDtypeStruct((Lp, QL + KVL), BF), jax.ShapeDtypeStruct((1, QL), F32),
                   jax.ShapeDtypeStruct((1, KVL), F32)],
        compiler_params=_cp(('arbitrary',)),
    )(dqn, dkn, z, z, q_gain, kv_gain)


def _qk_prep(q_raw, kv, z, cos, sin, lay, H, name):
    Lp = q_raw.shape[0]
    W = H * HEAD_PAD
    tm = _div_tile(Lp, 256, 16)

    def body(q_ref, kv_ref, kr_ref, c_ref, s_ref, qo_ref, ko_ref):
        c, s = c_ref[...], s_ref[...]

        def rope(t):
            return t * c + _rot_half(t) * s

        kpe = rope(kr_ref[...]).astype(BF)
        for h in range(H):
            b = h * HEAD_PAD
            qo_ref[:, b:b + 128] = (q_ref[:, b:b + 128] * SOFTMAX_SCALE).astype(BF)
            qo_ref[:, b + 128:b + 256] = (rope(q_ref[:, b + 128:b + 256]) * SOFTMAX_SCALE).astype(BF)
            ko_ref[:, b:b + 128] = kv_ref[:, b:b + 128]
            ko_ref[:, b + 128:b + 256] = kpe

    return pl.pallas_call(
        body, name=name, grid=(Lp // tm,),
        in_specs=[_rb(tm, W), _rb(tm, W), _rb(tm, 128, lay['kr'] // 128), _rb(tm, 128), _rb(tm, 128)],
        out_specs=[_rb(tm, W), _rb(tm, W)], out_shape=[jax.ShapeDtypeStruct((Lp, W), BF)] * 2,
        compiler_params=_cp(('parallel',)),
    )(q_raw, kv, z, cos, sin)


def _qk_prep_bwd(dQ, dK, dV, cos, sin, H, name):
    Lp = dQ.shape[0]
    W = H * HEAD_PAD
    tm = _div_tile(Lp, 256, 16)

    def body(dq_ref, dk_ref, dv_ref, c_ref, s_ref, qo_ref, kvo_ref, kro_ref):
        c, s = c_ref[...], s_ref[...]

        def unrope(t):
            return t * c - _rot_half(t * s)

        acc = jnp.zeros((tm, 128), F32)
        for h in range(H):
            b = h * HEAD_PAD
            qo_ref[:, b:b + 128] = (dq_ref[:, b:b + 128] * SOFTMAX_SCALE).astype(BF)
            qo_ref[:, b + 128:b + 256] = (unrope(dq_ref[:, b + 128:b + 256]) * SOFTMAX_SCALE).astype(BF)
            kvo_ref[:, b:b + 128] = dk_ref[:, b:b + 128].astype(BF)
            kvo_ref[:, b + 128:b + 256] = dv_ref[:, h * V_DIM:(h + 1) * V_DIM]
            acc += dk_ref[:, b + 128:b + 256]
        kro_ref[...] = unrope(acc).astype(BF)

    return pl.pallas_call(
        body, name=name, grid=(Lp // tm,),
        in_specs=[_rb(tm, W), _rb(tm, W), _rb(tm, H * V_DIM), _rb(tm, 128), _rb(tm, 128)],
        out_specs=[_rb(tm, W), _rb(tm, W), _rb(tm, 128)],
        out_shape=[jax.ShapeDtypeStruct((Lp, W), BF), jax.ShapeDtypeStruct((Lp, W), BF),
                   jax.ShapeDtypeStruct((Lp, 128), BF)],
        compiler_params=_cp(('parallel',)),
    )(dQ, dK, dV, cos, sin)


def _call_carrying(core, name, grid, in_specs, out_specs, out_shape, scratch, args, phases):
    n_in, n_out, n_scr = len(in_specs), len(out_specs), len(scratch)
    extra_in, extra_out, alias, sems = _phase_operands(phases, n_in, n_out)
    n_ei, n_eo = len(extra_in), len(extra_out)

    def body(*refs):
        ins, ein = refs[:n_in], refs[n_in:n_in + n_ei]
        outs = refs[n_in + n_ei:n_in + n_ei + n_out]
        eout = refs[n_in + n_ei + n_out:n_in + n_ei + n_out + n_eo]
        rest = refs[n_in + n_ei + n_out + n_eo:]
        scr, sem_refs = rest[:n_scr], rest[n_scr:]
        if phases:
            ids = [pl.program_id(d) for d in range(len(grid))]
            first, last = ids[0] == 0, ids[0] == grid[0] - 1
            for d in range(1, len(grid)):
                first, last = first & (ids[d] == 0), last & (ids[d] == grid[d] - 1)

            @pl.when(first)
            def _():
                _phase_start(phases, ein, eout, sem_refs)

        core(*ins, *outs, *scr)
        if phases:
            @pl.when(last)
            def _():
                _phase_finish(phases, ein, eout, sem_refs)

    outs = pl.pallas_call(
        body, name=name, grid=grid, in_specs=list(in_specs) + [ANY] * n_ei, out_specs=list(out_specs) + [ANY] * n_eo,
        out_shape=list(out_shape) + extra_out, input_output_aliases=alias, scratch_shapes=list(scratch) + sems,
        compiler_params=_cp(('arbitrary',) * len(grid)),
    )(*args, *extra_in)
    return list(outs[:n_out]), _phase_results(phases, outs[n_out:])


def _flash_fwd(Q, K, kv, H, name, phases=()):
    Lp = Q.shape[0]
    T = _div_tile(Lp, 256, 16)

    n_t = Lp // T
    FC = min(FLASH_FWD_CHUNK, n_t)
    CH = FC * T

    def body(q_ref, k_ref, v_ref, o_ref, lse_ref, m_s, l_s, acc_s):
        i = pl.program_id(1)
        q = q_ref[...]
        m_s[...] = jnp.full((T, 1), NEG, F32)
        l_s[...] = jnp.zeros((T, 1), F32)
        acc_s[...] = jnp.zeros((T, V_DIM), F32)

        def step(start, width, masked):
            rows = pl.ds(start, width)
            s = lax.dot_general(q, k_ref[rows, :], _DIMS['nt'], preferred_element_type=F32)
            if masked:
                rr = lax.broadcasted_iota(jnp.int32, (T, width), 0) + i * T
                cc = lax.broadcasted_iota(jnp.int32, (T, width), 1) + start
                s = jnp.where(cc <= rr, s, NEG)
            m = m_s[...]
            m_new = jnp.maximum(m, jnp.max(s, axis=-1, keepdims=True))
            alpha = jnp.exp(m - m_new)
            p = jnp.exp(s - m_new)
            l_s[...] = alpha * l_s[...] + jnp.sum(p, axis=-1, keepdims=True)
            acc_s[...] = alpha * acc_s[...] + jnp.dot(p.astype(BF), v_ref[rows, :], preferred_element_type=F32)
            m_s[...] = m_new

        n_full = i // FC

        def full(cidx, carry):
            step(pl.multiple_of(cidx * CH, CH), CH, False)
            return carry

        if n_t > FC:
            lax.fori_loop(0, n_full, full, 0)
        for nb in range(1, FC + 1):
            @pl.when(i % FC == nb - 1)
            def _(nb=nb):
                step(pl.multiple_of(n_full * CH, CH), nb * T, True)

        l = l_s[...]
        o_ref[...] = (acc_s[...] / l).astype(o_ref.dtype)
        lse_ref[...] = jnp.broadcast_to(m_s[...] + jnp.log(l), (T, 128))

    return _call_carrying(
        body, name, (H, Lp // T),
        [pl.BlockSpec((T, HEAD_PAD), lambda h, i: (i, h)), pl.BlockSpec((Lp, HEAD_PAD), lambda h, i: (0, h)),
         pl.BlockSpec((Lp, V_DIM), lambda h, i: (0, 2 * h + 1))],
        [pl.BlockSpec((T, V_DIM), lambda h, i: (i, h)), pl.BlockSpec((None, T, 128), lambda h, i: (h, i, 0))],
        [jax.ShapeDtypeStruct((Lp, H * V_DIM), BF), jax.ShapeDtypeStruct((H, Lp, 128), F32)],
        [pltpu.VMEM((T, 1), F32), pltpu.VMEM((T, 1), F32), pltpu.VMEM((T, V_DIM), F32)], (Q, K, kv), phases)


def _flash_bwd(Q, K, kv, O, dO, lse, H, name, phases=()):
    Lp = Q.shape[0]
    T = _div_tile(Lp, 256, 16)
    n_t = Lp // T
    FC = min(FLASH_BWD_CHUNK, n_t)

    def body(q_ref, k_ref, v_ref, o_ref, do_ref, lse_ref, dq_ref, dk_ref, dv_ref, dk_acc, dv_acc):
        j = pl.program_id(1)

        @pl.when(j == 0)
        def _():
            dq_ref[...] = jnp.zeros_like(dq_ref)

        kj, vj = k_ref[...], v_ref[...]
        dk_acc[...] = jnp.zeros_like(dk_acc)
        dv_acc[...] = jnp.zeros_like(dv_acc)

        def step(start, width, masked):
            rows = pl.ds(start, width)
            qi, doi = q_ref[rows, :], do_ref[rows, :]
            delta = jnp.sum(doi.astype(F32) * o_ref[rows, :].astype(F32), axis=-1, keepdims=True)
            s = lax.dot_general(qi, kj, _DIMS['nt'], preferred_element_type=F32)
            p = jnp.exp(s - lse_ref[rows, :][:, :1])
            if masked:
                rr = lax.broadcasted_iota(jnp.int32, (width, T), 0) + start
                cc = lax.broadcasted_iota(jnp.int32, (width, T), 1) + j * T
                p = jnp.where(cc <= rr, p, 0.0)
            dp = lax.dot_general(doi, vj, _DIMS['nt'], preferred_element_type=F32)
            ds = (p * (dp - delta)).astype(BF)
            dv_acc[...] += lax.dot_general(p.astype(BF), doi, _DIMS['tn'], preferred_element_type=F32)
            dk_acc[...] += lax.dot_general(ds, qi, _DIMS['tn'], preferred_element_type=F32)
            dq_ref[rows, :] += jnp.dot(ds, kj, preferred_element_type=F32)

        head = (n_t - 1 - j) % FC + 1
        for nb in range(1, FC + 1):
            @pl.when(head == nb)
            def _(nb=nb):
                step(pl.multiple_of(j * T, T), nb * T, True)

        def full(cidx, carry):
            step(pl.multiple_of((j + head + cidx * FC) * T, T), FC * T, False)
            return carry

        if n_t > FC:
            lax.fori_loop(0, (n_t - j - head) // FC, full, 0)
        dk_ref[...] = dk_acc[...]
        dv_ref[...] = dv_acc[...].astype(dv_ref.dtype)

    head_q = pl.BlockSpec((Lp, HEAD_PAD), lambda h, j: (0, h))
    head_v = pl.BlockSpec((Lp, V_DIM), lambda h, j: (0, h))
    return _call_carrying(
        body, name, (H, n_t),
        [head_q, pl.BlockSpec((T, HEAD_PAD), lambda h, j: (j, h)), pl.BlockSpec((T, V_DIM), lambda h, j: (j, 2 * h + 1)),
         head_v, head_v, pl.BlockSpec((None, Lp, 128), lambda h, j: (h, 0, 0))],
        [head_q, pl.BlockSpec((T, HEAD_PAD), lambda h, j: (j, h)), pl.BlockSpec((T, V_DIM), lambda h, j: (j, h))],
        [jax.ShapeDtypeStruct((Lp, H * HEAD_PAD), F32), jax.ShapeDtypeStruct((Lp, H * HEAD_PAD), F32),
         jax.ShapeDtypeStruct((Lp, H * V_DIM), BF)],
        [pltpu.VMEM((T, HEAD_PAD), F32), pltpu.VMEM((T, V_DIM), F32)], (Q, K, kv, O, dO, lse), phases)


def _gate_fwd(z, y_pool, y_mla, lay, name):
    Lp, D = y_pool.shape
    tm = _div_tile(Lp, 256, 16)

    def body(gp_ref, gm_ref, yp_ref, ym_ref, o_ref):
        o_ref[...] = (jax.nn.sigmoid(gp_ref[...]) * yp_ref[...] + jax.nn.sigmoid(gm_ref[...]) * ym_ref[...]).astype(BF)

    return pl.pallas_call(
        body, name=name, grid=(Lp // tm,),
        in_specs=[_rb(tm, D, lay['gp'] // D), _rb(tm, D, lay['gm'] // D), _rb(tm, D), _rb(tm, D)], out_specs=_rb(tm, D),
        out_shape=jax.ShapeDtypeStruct((Lp, D), BF), compiler_params=_cp(('parallel',)),
    )(z, z, y_pool, y_mla)


def _gate_bwd(dy, z, y_pool, y_mla, lay, name):
    Lp, D = y_pool.shape
    tm = _div_tile(Lp, 256, 16)

    def body(dy_ref, gp_ref, gm_ref, yp_ref, ym_ref, dp_ref, dm_ref, dg_ref):
        dy_ = dy_ref[...]
        sp, sm = jax.nn.sigmoid(gp_ref[...]), jax.nn.sigmoid(gm_ref[...])
        dp_ref[...] = (dy_ * sp).astype(BF)
        dm_ref[...] = (dy_ * sm).astype(BF)
        dg_ref[:, :D] = (dy_ * yp_ref[...] * (sp * (1.0 - sp))).astype(BF)
        dg_ref[:, D:] = (dy_ * ym_ref[...] * (sm * (1.0 - sm))).astype(BF)

    return pl.pallas_call(
        body, name=name, grid=(Lp // tm,),
        in_specs=[_rb(tm, D), _rb(tm, D, lay['gp'] // D), _rb(tm, D, lay['gm'] // D), _rb(tm, D), _rb(tm, D)],
        out_specs=[_rb(tm, D), _rb(tm, D), _rb(tm, 2 * D)],
        out_shape=[jax.ShapeDtypeStruct((Lp, D), BF), jax.ShapeDtypeStruct((Lp, D), BF),
                   jax.ShapeDtypeStruct((Lp, 2 * D), BF)],
        compiler_params=_cp(('parallel',)),
    )(dy, z, z, y_pool, y_mla)


def _z_layout(D, QL, KVL):
    cq = POOL_WIDTH
    ckv = cq + QL
    gp = -(-(ckv + KVL) // D) * D
    gm = gp + D
    kr = gm + D
    return dict(QL=QL, KVL=KVL, cq=cq, ckv=ckv, gp=gp, gm=gm, kr=kr, width=kr + 128)


def _w_in_aligned(w4, lay, D):
    w = w4.shape[2]
    n0 = POOL_WIDTH + lay['QL'] + lay['KVL']

    def cols(a, b):
        return [w4[k][:, max(a, k * w) - k * w:min(b, (k + 1) * w) - k * w] for k in range(N_CHIPS)
                if max(a, k * w) < min(b, (k + 1) * w)]

    parts = (cols(0, n0) + [jnp.zeros((D, lay['gp'] - n0), w4.dtype)] + cols(n0 + QK_ROPE, N_CHIPS * w) +
             cols(n0, n0 + QK_ROPE) + [jnp.zeros((D, 128 - QK_ROPE), w4.dtype)])
    return jnp.concatenate(parts, axis=1)


def _w_in_slabs(w_al, lay, D, w):
    n0 = POOL_WIDTH + lay['QL'] + lay['KVL']
    segs = [(0, n0, 0), (n0, n0 + QK_ROPE, lay['kr']), (n0 + QK_ROPE, N_CHIPS * w, lay['gp'])]

    def slab(k):
        parts = []
        for a, b, at in segs:
            lo, hi = max(a, k * w), min(b, (k + 1) * w)
            if lo < hi:
                parts.append(w_al[:, at + lo - a:at + hi - a])
        return jnp.concatenate(parts, axis=1)

    return jnp.stack([slab(k) for k in range(N_CHIPS)], axis=0)


def kernel(x, meta_tokens, norm_ffn1_pre, norm_ffn1_post, ffn1_w_gu, ffn1_w_down, norm_mix_pre, norm_mix_post, w_in, pool_w, pool_scale, w_pool_o, q_a_norm, w_q_b, kv_a_norm, w_kv_b, w_mla_o, w_out, norm_ffn2_pre, norm_ffn2_post, ffn2_w_gu, ffn2_w_down, loss_target, m_meta_tokens, m_norm_ffn1_pre, m_norm_ffn1_post, m_ffn1_w_gu, m_ffn1_w_down, m_norm_mix_pre, m_norm_mix_post, m_w_in, m_pool_w, m_pool_scale, m_w_pool_o, m_q_a_norm, m_w_q_b, m_kv_a_norm, m_w_kv_b, m_w_mla_o, m_w_out, m_norm_ffn2_pre, m_norm_ffn2_post, m_ffn2_w_gu, m_ffn2_w_down, v_meta_tokens, v_norm_ffn1_pre, v_norm_ffn1_post, v_ffn1_w_gu, v_ffn1_w_down, v_norm_mix_pre, v_norm_mix_post, v_w_in, v_pool_w, v_pool_scale, v_w_pool_o, v_q_a_norm, v_w_q_b, v_kv_a_norm, v_w_kv_b, v_w_mla_o, v_w_out, v_norm_ffn2_pre, v_norm_ffn2_post, v_ffn2_w_gu, v_ffn2_w_down):
    given = dict(locals())
    W = {n: given[n] for n in WEIGHTS}
    M = {n: given['m_' + n] for n in WEIGHTS}
    V = {n: given['v_' + n] for n in WEIGHTS}

    S, D = x.shape[1], x.shape[2]
    NM = meta_tokens.shape[0]
    L = NM + S
    Lp = -(-L // ROW_ALIGN) * ROW_ALIGN
    QL, KVL = w_q_b.shape[1], w_kv_b.shape[1]
    H = w_q_b.shape[2] * N_CHIPS // QK_DIM
    lay = _z_layout(D, QL, KVL)
    gx, gy = lax.axis_index('x'), lax.axis_index('y')
    chip = 2 * gx + gy

    names = list(BIG)
    slab = {n: W[n][0] for n in names}
    kind = dict(BIG, meta_tokens='col')
    slab_shape = {n: tuple(slab[n].shape) for n in names}
    gc = lax.axis_index('c')
    c_arr = jnp.stack([gc]).astype(jnp.int32)
    place_arr = jnp.stack([gc, chip]).astype(jnp.int32)

    full = {n: _cast_place(slab[n], BIG[n], place_arr, BF, f'place_{n}') for n in names}
    full['meta_tokens'] = _cast_place(meta_tokens, 'col', place_arr, F32, 'place_meta_tokens')
    G0 = ['ffn1_w_gu', 'meta_tokens']
    G0B = ['ffn1_w_down']
    G1W = ['w_in']
    G1R = ['pool_w', 'w_pool_o', 'w_q_b', 'w_kv_b']
    G2 = ['w_mla_o', 'w_out']
    G3 = ['ffn2_w_gu', 'ffn2_w_down']

    def gather(phase_fn, group):
        return phase_fn([full[n] for n in group], [kind[n] for n in group])

    def arrived(group, res):
        full.update(zip(group, res))

    arrived(G0, _all_gather([full[n] for n in G0], [kind[n] for n in G0]))
    meta_full = full['meta_tokens']

    pos = jnp.arange(Lp, dtype=F32)
    inv = ROPE_THETA ** (-jnp.arange(0, QK_ROPE, 2, dtype=F32) / QK_ROPE)
    ang = pos[:, None] * inv[None, :]
    ang = jnp.concatenate([ang, ang], axis=-1)
    cos = jnp.pad(jnp.cos(ang), ((0, 0), (0, 128 - QK_ROPE)), constant_values=1.0)
    sin = jnp.pad(jnp.sin(ang), ((0, 0), (0, 128 - QK_ROPE)))

    h0 = jnp.concatenate([meta_full, x[0], jnp.zeros((Lp - L, D), F32)], axis=0)
    tgt = jnp.pad(loss_target[0], ((NM, Lp - L), (0, 0)))

    n1 = _norm_fwd(h0, norm_ffn1_pre, 'ffn1_norm')
    gu1, (res_b, res) = _mm(n1, full['ffn1_w_gu'], 'nn', BF, 'ffn1_gu',
                            [gather(_phase_gather_ici, G0B), gather(_phase_gather_ici, G1R)])
    arrived(G0B, res_b)
    arrived(G1R, res)
    (a1,), (res_b,) = _swiglu_fwd(gu1, 'ffn1_act', [gather(_phase_gather_d2d, G0B)])
    arrived(G0B, res_b)
    f1, (res, res_w) = _mm(a1, full['ffn1_w_down'], 'nn', F32, 'ffn1_down',
                           [gather(_phase_gather_d2d, G1R), gather(_phase_gather_ici, G1W)])
    arrived(G1R, res)
    arrived(G1W, res_w)
    h1, (res_w,) = _post_residual(h0, f1, norm_ffn1_post, 0.5, 'ffn1_res', [gather(_phase_gather_d2d, G1W)])
    arrived(G1W, res_w)

    w_in_al = _w_in_aligned(full['w_in'], lay, D)
    w_q_pad = jnp.pad(full['w_q_b'].reshape(QL, H, QK_DIM), ((0, 0), (0, 0), (0, HEAD_PAD - QK_DIM))).reshape(
        QL, H * HEAD_PAD)

    n2 = _norm_fwd(h1, norm_mix_pre, 'mix_norm')
    z, (res,) = _mm(n2, w_in_al, 'nn', F32, 'mix_in', [gather(_phase_gather_ici, G2)])
    arrived(G2, res)
    d_pool = _pool_fwd(z, 'pool_fwd')
    e_pool, yp = _pool_mix_fwd(d_pool, full['pool_w'], pool_scale, 'pool_mix')
    y_pool = _mm(yp, full['w_pool_o'], 'nn', F32, 'pool_out')
    cqn, ckvn = _lora_norms(z, q_a_norm, kv_a_norm, lay, 'lora_norms')
    q_raw = _mm(cqn, w_q_pad, 'nn', F32, 'mla_q')
    kv = _mm(ckvn, full['w_kv_b'], 'nn', BF, 'mla_kv')
    Q, K = _qk_prep(q_raw, kv, z, cos, sin, lay, H, 'qk_prep')
    (O, lse), (res2, res3) = _flash_fwd(Q, K, kv, H, 'flash_fwd',
                                        [gather(_phase_gather_d2d, G2), gather(_phase_gather_ici, G3)])
    arrived(G2, res2)
    arrived(G3, res3)
    y_mla, (res,) = _mm(O, full['w_mla_o'], 'nn', F32, 'mla_out', [gather(_phase_gather_d2d, G3)])
    arrived(G3, res)
    y = _gate_fwd(z, y_pool, y_mla, lay, 'gate')
    m_mix = _mm(y, full['w_out'], 'nn', F32, 'mix_out')
    h2, _ = _post_residual(h1, m_mix, norm_mix_post, 1.0, 'mix_res')

    n3 = _norm_fwd(h2, norm_ffn2_pre, 'ffn2_norm')
    gu2 = _mm(n3, full['ffn2_w_gu'], 'nn', BF, 'ffn2_gu')
    (a2,), _ = _swiglu_fwd(gu2, 'ffn2_act')
    f2 = _mm(a2, full['ffn2_w_down'], 'nn', F32, 'ffn2_down')

    G, theirs, sums, halves, reduced = {}, {}, {}, {}, {}
    RA = ['ffn2_w_down', 'ffn2_w_gu']
    RB1 = ['w_out', 'w_pool_o', 'pool_w', 'w_mla_o']
    RB2 = ['w_q_b', 'w_kv_b']
    RB3 = ['w_in']
    RC1 = ['ffn1_w_down']
    RC2 = ['ffn1_w_gu']

    def pair_phase(group):
        return _phase_pair([G[n] for n in group], [BIG[n] for n in group], [slab_shape[n] for n in group])

    def pair_sums(group, res):
        for n, t in zip(group, res):
            sums[n] = _pair_sum(G[n], t, BIG[n], slab_shape[n], c_arr, f'rs_pair_sum_{n}')

    def chip_phase(group):
        return _phase_chip([sums[n] for n in group])

    def chip_sums(group, res):
        for n, ld in zip(group, res):
            halves[n] = _chip_sum(sums[n], ld, BIG[n], slab_shape[n], place_arr, f'rs_chip_sum_{n}')

    def final_phase(group):
        return _phase_final([halves[n] for n in group], [BIG[n] for n in group])

    dh3, df2, G['norm_ffn2_post'], sq = _tail(h2, f2, norm_ffn2_post, 0.5, tgt, NM, S, 'ffn2_tail')

    G['ffn2_w_down'] = _mm(a2, df2, 'tn', BF, 'ffn2_dw_down')
    da2 = _mm(df2, full['ffn2_w_down'], 'nt', BF, 'ffn2_da')
    dgu2 = _swiglu_bwd(da2, gu2, 'ffn2_act_bwd')
    G['ffn2_w_gu'] = _mm(n3, dgu2, 'tn', BF, 'ffn2_dw_gu')
    dn3, (res,) = _mm(dgu2, full['ffn2_w_gu'], 'nt', F32, 'ffn2_dn', [pair_phase(RA)])
    (dh2, G['norm_ffn2_pre']), _ = _pre_bwd(dh3, dn3, h2, norm_ffn2_pre, 'ffn2_norm_bwd')
    pair_sums(RA, res)

    dm, G['norm_mix_post'] = _post_bwd(dh2, m_mix, norm_mix_post, 1.0, 'mix_res_bwd')
    G['w_out'] = _mm(y, dm, 'tn', BF, 'dw_out')
    dy = _mm(dm, full['w_out'], 'nt', F32, 'mix_out_bwd')
    dy_pool, dy_mla, d_gate = _gate_bwd(dy, z, y_pool, y_mla, lay, 'gate_bwd')
    G['w_pool_o'] = _mm(yp, dy_pool, 'tn', BF, 'dw_pool_o')
    dyp = _mm(dy_pool, full['w_pool_o'], 'nt', F32, 'pool_out_bwd')
    dd, G['pool_scale'], d_pool_w = _pool_mix_bwd(dyp, e_pool, d_pool, full['pool_w'], pool_scale, 'pool_mix_bwd')
    G['pool_w'] = d_pool_w.astype(BF)
    du_pool = _pool_bwd(dd, 'pool_bwd')
    G['w_mla_o'] = _mm(O, dy_mla, 'tn', BF, 'dw_mla_o')
    dO, (res,) = _mm(dy_mla, full['w_mla_o'], 'nt', BF, 'mla_out_bwd', [pair_phase(RB1)])
    pair_sums(RB1, res)
    (dQ, dK, dV), (res, res_b) = _flash_bwd(Q, K, kv, O, dO, lse, H, 'flash_bwd', [chip_phase(RA), chip_phase(RB1)])
    chip_sums(RA, res)
    chip_sums(RB1, res_b)
    dq_raw, dkv, dkr = _qk_prep_bwd(dQ, dK, dV, cos, sin, H, 'qk_prep_bwd')
    d_w_q_pad, (res_b,) = _mm(cqn, dq_raw, 'tn', BF, 'dw_q_b', [final_phase(RB1)])
    reduced.update(zip(RB1, res_b))
    G['w_q_b'] = d_w_q_pad.reshape(QL, H, HEAD_PAD)[:, :, :QK_DIM].reshape(QL, H * QK_DIM)
    dcqn = _mm(dq_raw, w_q_pad, 'nt', F32, 'mla_q_bwd')
    G['w_kv_b'] = _mm(ckvn, dkv, 'tn', BF, 'dw_kv_b')
    dckvn, (res,) = _mm(dkv, full['w_kv_b'], 'nt', F32, 'mla_kv_bwd', [pair_phase(RB2)])
    pair_sums(RB2, res)
    d_lora, G['q_a_norm'], G['kv_a_norm'] = _lora_norms_bwd(dcqn, dckvn, z, q_a_norm, kv_a_norm, lay, 'lora_norms_bwd')
    n0 = POOL_WIDTH + QL + KVL
    dz = jnp.concatenate([du_pool, d_lora, jnp.zeros((Lp, lay['gp'] - n0), BF), d_gate, dkr], axis=1)
    d_w_in_al, (res, res_b) = _mm(n2, dz, 'tn', BF, 'dw_in', [final_phase(RA), chip_phase(RB2)])
    reduced.update(zip(RA, res))
    chip_sums(RB2, res_b)
    G['w_in'] = _w_in_slabs(d_w_in_al, lay, D, w_in.shape[2])
    dn2, (res, res_b) = _mm(dz, w_in_al, 'nt', F32, 'mix_in_bwd', [pair_phase(RB3), final_phase(RB2)])
    reduced.update(zip(RB2, res_b))
    (dh1, G['norm_mix_pre']), _ = _pre_bwd(dh2, dn2, h1, norm_mix_pre, 'mix_norm_bwd')
    pair_sums(RB3, res)

    df1, G['norm_ffn1_post'] = _post_bwd(dh1, f1, norm_ffn1_post, 0.5, 'ffn1_res_bwd')
    G['ffn1_w_down'], (res_b,) = _mm(a1, df1, 'tn', BF, 'ffn1_dw_down', [chip_phase(RB3)])
    chip_sums(RB3, res_b)
    da1, (res, res_b) = _mm(df1, full['ffn1_w_down'], 'nt', BF, 'ffn1_da', [pair_phase(RC1), final_phase(RB3)])
    reduced.update(zip(RB3, res_b))
    dgu1 = _swiglu_bwd(da1, gu1, 'ffn1_act_bwd')
    pair_sums(RC1, res)
    G['ffn1_w_gu'], (res_c1,) = _mm(n1, dgu1, 'tn', BF, 'ffn1_dw_gu', [chip_phase(RC1)])
    chip_sums(RC1, res_c1)
    (res,) = _run_phases([pair_phase(RC2)], 'rs_pair_exchange_tail')
    pair_sums(RC2, res)
    dn1, (res_c2, res_c1) = _mm(dgu1, full['ffn1_w_gu'], 'nt', F32, 'ffn1_dn', [chip_phase(RC2), final_phase(RC1)])
    chip_sums(RC2, res_c2)
    reduced.update(zip(RC1, res_c1))
    (dh0, G['norm_ffn1_pre']), (res_c2,) = _pre_bwd(dh1, dn1, h0, norm_ffn1_pre, 'ffn1_norm_bwd', [final_phase(RC2)])
    reduced.update(zip(RC2, res_c2))
    grad_x = dh0[NM:L][None]

    SW = max(D, POOL_WIDTH)

    def widen(a, fill=0.0):
        return jnp.pad(a, ((0, 0), (0, SW - a.shape[1])), constant_values=fill)

    rows = [widen(G[n]) for n in SMALL_VEC] + [widen(dh0[:NM]), widen(sq)]
    n_rows = len(SMALL_VEC) + NM + 1
    pad_rows = -(-n_rows // 8) * 8 - n_rows
    small = _all_reduce_small(jnp.concatenate(rows + [jnp.zeros((pad_rows, SW), F32)], axis=0))
    loss = (0.5 / D) * small[len(SMALL_VEC) + NM, 0]
    for i, n in enumerate(SMALL_VEC):
        reduced[n] = small[i:i + 1, :G[n].shape[1]]
    mw = meta_tokens.shape[1]
    reduced['meta_tokens'] = lax.dynamic_slice(small, (len(SMALL_VEC), chip * mw), (NM, mw))

    grads, deltas, new_m, new_v = {}, {}, {}, {}
    for n in names:
        grads[n], deltas[n], new_m[n], new_v[n] = _adam(W[n], reduced[n][None], M[n], V[n], f'adam_{n}')
    n_vec = len(SMALL_VEC)
    vec_w = jnp.concatenate([widen(W[n]) for n in SMALL_VEC] + [jnp.zeros((16 - n_vec, SW), F32)], axis=0)
    vec_m = jnp.concatenate([widen(M[n]) for n in SMALL_VEC] + [jnp.zeros((16 - n_vec, SW), F32)], axis=0)
    vec_v = jnp.concatenate([widen(V[n], 1.0) for n in SMALL_VEC] + [jnp.ones((16 - n_vec, SW), F32)], axis=0)
    vec_g = jnp.concatenate([small[:n_vec], jnp.zeros((16 - n_vec, SW), F32)], axis=0)
    _, vd, vm, vv = _adam(vec_w, vec_g, vec_m, vec_v, 'adam_vectors')
    for i, n in enumerate(SMALL_VEC):
        wdt = W[n].shape[1]
        grads[n], deltas[n], new_m[n], new_v[n] = reduced[n], vd[i:i + 1, :wdt], vm[i:i + 1, :wdt], vv[i:i + 1, :wdt]
    grads['meta_tokens'], deltas['meta_tokens'], new_m['meta_tokens'], new_v['meta_tokens'] = _adam(
        meta_tokens, reduced['meta_tokens'], m_meta_tokens, v_meta_tokens, 'adam_meta')

    return (loss, grad_x, *[grads[n] for n in WEIGHTS], *[deltas[n] for n in WEIGHTS], *[new_m[n] for n in WEIGHTS],
            *[new_v[n] for n in WEIGHTS])
```

```python
import functools

import jax
import jax.numpy as jnp
import numpy as np
from jax import lax
from jax.experimental import pallas as pl
from jax.experimental.pallas import tpu as pltpu

F32 = jnp.float32
BF = jnp.bfloat16
MESH = pl.DeviceIdType.MESH

EPS = 1e-6
N_CHIPS = 4
POOL_WINDOWS = (2, 4, 8, 16)
POOL_GROUP = 256
POOL_WIDTH = POOL_GROUP * len(POOL_WINDOWS)
QK_NOPE = 128
QK_ROPE = 64
V_DIM = 128
QK_DIM = QK_NOPE + QK_ROPE
HEAD_PAD = 256
ROPE_THETA = 10000.0
SOFTMAX_SCALE = QK_DIM ** -0.5
ADAM_LR = 0.001
ADAM_B1 = 0.9
ADAM_B2 = 0.999
ADAM_EPS = 1e-08
ADAM_WD = 0.01
ADAM_STEP = 10
ROW_ALIGN = 256
VMEM_LIMIT = 56 * 1024 * 1024
MM_TK_ROWS = 4352
MM_TK_COLS = 6400
NEG = -1e30
FLASH_FWD_CHUNK = 17
FLASH_BWD_CHUNK = 8
WEIGHTS = ['meta_tokens', 'norm_ffn1_pre', 'norm_ffn1_post', 'ffn1_w_gu', 'ffn1_w_down', 'norm_mix_pre',
           'norm_mix_post', 'w_in', 'pool_w', 'pool_scale', 'w_pool_o', 'q_a_norm', 'w_q_b', 'kv_a_norm', 'w_kv_b',
           'w_mla_o', 'w_out', 'norm_ffn2_pre', 'norm_ffn2_post', 'ffn2_w_gu', 'ffn2_w_down']
BIG = {'ffn1_w_gu': 'col', 'ffn1_w_down': 'row', 'w_in': 'row', 'pool_w': 'pool', 'w_pool_o': 'col', 'w_q_b': 'col',
       'w_kv_b': 'col', 'w_mla_o': 'row', 'w_out': 'row', 'ffn2_w_gu': 'col', 'ffn2_w_down': 'row'}
SMALL_VEC = ['norm_ffn1_pre', 'norm_ffn1_post', 'norm_mix_pre', 'norm_mix_post', 'norm_ffn2_pre', 'norm_ffn2_post',
             'pool_scale', 'q_a_norm', 'kv_a_norm']


def _div_tile(n, target, align):
    best = None
    for t in range(align, min(n, target) + 1, align):
        if n % t == 0:
            best = t
    return best if best is not None else n


def _cp(sem=None):
    return pltpu.CompilerParams(dimension_semantics=sem, vmem_limit_bytes=VMEM_LIMIT)


def _full_shape(kind, slab):
    if kind == 'col':
        return (slab[0], slab[1] * N_CHIPS)
    if kind == 'row':
        return (slab[0] * N_CHIPS, slab[1])
    return (slab[0], slab[1] * N_CHIPS, slab[2])


def _half_shape(kind, slab):
    if kind == 'pool':
        return (slab[0], slab[1] // 2, slab[2])
    if kind == 'row':
        return (slab[0], slab[1] // 2)
    return (slab[0] // 2, slab[1])


def _half_of_slab(ref, kind, c):
    if kind == 'pool':
        n = ref.shape[1] // 2
        return ref.at[:, pl.ds(c * n, n), :]
    if kind == 'row':
        n = ref.shape[1] // 2
        return ref.at[:, pl.ds(c * n, n)]
    n = ref.shape[0] // 2
    return ref.at[pl.ds(c * n, n), :]


def _piece(ref, kind, k, c):
    if kind == 'col':
        r, w = ref.shape[0] // 2, ref.shape[1] // N_CHIPS
        return ref.at[pl.ds(c * r, r), pl.ds(k * w, w)]
    if kind == 'row':
        r, w = ref.shape[0] // N_CHIPS, ref.shape[1] // 2
        return ref.at[pl.ds(k * r, r), pl.ds(c * w, w)]
    r = ref.shape[1] // (2 * N_CHIPS)
    return ref.at[:, pl.ds((2 * k + c) * r, r), :]


def _place():
    x, y, c = lax.axis_index('x'), lax.axis_index('y'), lax.axis_index('c')
    return x, y, c


def _peer_chip(x, y, j):
    px = 1 - x if (j >> 1) else x
    py = 1 - y if (j & 1) else y
    return px, py


ANY = pl.BlockSpec(memory_space=pl.ANY)


def _all_gather(bufs, kinds):
    n = len(bufs)

    def body(*refs):
        outs = refs[n:2 * n]
        ssem, rsem, fssem, frsem = refs[2 * n:]
        x, y, c = _place()
        me = 2 * x + y
        sib = (x, y, 1 - c)
        sends = []
        for w in range(n):
            for j in (1, 2, 3):
                px, py = _peer_chip(x, y, j)
                mine = _piece(outs[w], kinds[w], me, c)
                sends.append(pltpu.make_async_remote_copy(
                    src_ref=mine, dst_ref=mine, send_sem=ssem.at[w, j - 1], recv_sem=rsem.at[w, j - 1],
                    device_id=(px, py, c), device_id_type=MESH))
        for cp in sends:
            cp.start()
        fwds = []
        for w in range(n):
            for j in (1, 2, 3):
                px, py = _peer_chip(x, y, j)
                got = _piece(outs[w], kinds[w], 2 * px + py, c)
                pltpu.make_async_remote_copy(src_ref=got, dst_ref=got, send_sem=ssem.at[w, j - 1],
                                             recv_sem=rsem.at[w, j - 1], device_id=(px, py, c),
                                             device_id_type=MESH).wait_recv()
                fwd = pltpu.make_async_remote_copy(src_ref=got, dst_ref=got, send_sem=fssem.at[w, j - 1],
                                                   recv_sem=frsem.at[w, j - 1], device_id=sib, device_id_type=MESH)
                fwd.start()
                fwds.append(fwd)
        for w in range(n):
            for j in (1, 2, 3):
                px, py = _peer_chip(x, y, j)
                other = _piece(outs[w], kinds[w], 2 * px + py, 1 - c)
                pltpu.make_async_remote_copy(src_ref=other, dst_ref=other, send_sem=fssem.at[w, j - 1],
                                             recv_sem=frsem.at[w, j - 1], device_id=sib,
                                             device_id_type=MESH).wait_recv()
        for cp in sends + fwds:
            cp.wait_send()

    return pl.pallas_call(
        body, name='all_gather_weights', out_shape=[jax.ShapeDtypeStruct(b.shape, b.dtype) for b in bufs],
        in_specs=[ANY] * n, out_specs=[ANY] * n, input_output_aliases={w: w for w in range(n)},
        scratch_shapes=[pltpu.SemaphoreType.DMA((n, 3)), pltpu.SemaphoreType.DMA((n, 3)),
                        pltpu.SemaphoreType.DMA((n, 3)), pltpu.SemaphoreType.DMA((n, 3))],
    )(*bufs)


def _remote(src, dst, ssem, rsem, k, dev):
    return pltpu.make_async_remote_copy(src_ref=src, dst_ref=dst, send_sem=ssem.at[k], recv_sem=rsem.at[k],
                                        device_id=dev, device_id_type=MESH)


def _phase_gather_ici(bufs, kinds):
    n = len(bufs)

    def build(ins, outs, ssem, rsem):
        x, y, c = _place()
        me = 2 * x + y
        ds = []
        for w in range(n):
            for j in (1, 2, 3):
                px, py = _peer_chip(x, y, j)
                mine = _piece(outs[w], kinds[w], me, c)
                got = _piece(outs[w], kinds[w], 2 * px + py, c)
                k = 3 * w + j - 1
                ds.append((_remote(mine, mine, ssem, rsem, k, (px, py, c)), _remote(got, got, ssem, rsem, k, (px, py, c))))
        return ds

    return dict(ins=list(bufs), outs=[jax.ShapeDtypeStruct(b.shape, b.dtype) for b in bufs],
                alias={w: w for w in range(n)}, nsem=3 * n, build=build)


def _phase_gather_d2d(bufs, kinds):
    n = len(bufs)

    def build(ins, outs, ssem, rsem):
        x, y, c = _place()
        sib = (x, y, 1 - c)
        ds = []
        for w in range(n):
            for j in (1, 2, 3):
                px, py = _peer_chip(x, y, j)
                have = _piece(outs[w], kinds[w], 2 * px + py, c)
                want = _piece(outs[w], kinds[w], 2 * px + py, 1 - c)
                k = 3 * w + j - 1
                ds.append((_remote(have, have, ssem, rsem, k, sib), _remote(want, want, ssem, rsem, k, sib)))
        return ds

    return dict(ins=list(bufs), outs=[jax.ShapeDtypeStruct(b.shape, b.dtype) for b in bufs],
                alias={w: w for w in range(n)}, nsem=3 * n, build=build)


def _phase_pair(grads, kinds, slabs):
    n = len(grads)

    def build(ins, outs, ssem, rsem):
        x, y, c = _place()
        sib = (x, y, 1 - c)
        ds = []
        for w in range(n):
            for k in range(N_CHIPS):
                cp = _remote(_piece(ins[w], kinds[w], k, 1 - c), outs[w].at[k], ssem, rsem, N_CHIPS * w + k, sib)
                ds.append((cp, cp))
        return ds

    return dict(ins=list(grads), alias={}, nsem=N_CHIPS * n, build=build,
                outs=[jax.ShapeDtypeStruct((N_CHIPS,) + _half_shape(k, s), g.dtype)
                      for g, k, s in zip(grads, kinds, slabs)])


def _phase_chip(sums):
    n = len(sums)

    def build(ins, outs, ssem, rsem):
        x, y, c = _place()
        ds = []
        for w in range(n):
            for j in (1, 2, 3):
                px, py = _peer_chip(x, y, j)
                cp = _remote(ins[w].at[2 * px + py], outs[w].at[j - 1], ssem, rsem, 3 * w + j - 1, (px, py, c))
                ds.append((cp, cp))
        return ds

    return dict(ins=list(sums), alias={}, nsem=3 * n, build=build,
                outs=[jax.ShapeDtypeStruct((3,) + s.shape[1:], s.dtype) for s in sums])


def _phase_final(slabs_half, kinds):
    n = len(slabs_half)

    def build(ins, outs, ssem, rsem):
        x, y, c = _place()
        sib = (x, y, 1 - c)
        ds = []
        for w in range(n):
            mine = _half_of_slab(outs[w], kinds[w], c)
            other = _half_of_slab(outs[w], kinds[w], 1 - c)
            ds.append((_remote(mine, mine, ssem, rsem, w, sib), _remote(other, other, ssem, rsem, w, sib)))
        return ds

    return dict(ins=list(slabs_half), outs=[jax.ShapeDtypeStruct(s.shape, s.dtype) for s in slabs_half],
                alias={w: w for w in range(n)}, nsem=n, build=build)


def _phase_operands(phases, n_main_in, n_main_out):
    ins, outs, alias, sems = [], [], {}, []
    for ph in phases:
        for i, o in ph['alias'].items():
            alias[n_main_in + len(ins) + i] = n_main_out + len(outs) + o
        ins += ph['ins']
        outs += ph['outs']
        sems += [pltpu.SemaphoreType.DMA((ph['nsem'],)), pltpu.SemaphoreType.DMA((ph['nsem'],))]
    return ins, outs, alias, sems


def _phase_copies(phases, in_refs, out_refs, sem_refs):
    ds, a, b = [], 0, 0
    for p, ph in enumerate(phases):
        ds += ph['build'](in_refs[a:a + len(ph['ins'])], out_refs[b:b + len(ph['outs'])], sem_refs[2 * p],
                          sem_refs[2 * p + 1])
        a += len(ph['ins'])
        b += len(ph['outs'])
    return ds


def _phase_start(phases, in_refs, out_refs, sem_refs):
    for send, _ in _phase_copies(phases, in_refs, out_refs, sem_refs):
        send.start()


def _phase_finish(phases, in_refs, out_refs, sem_refs):
    ds = _phase_copies(phases, in_refs, out_refs, sem_refs)
    for _, recv in ds:
        recv.wait_recv()
    for send, _ in ds:
        send.wait_send()


def _phase_results(phases, outs):
    res, b = [], 0
    for ph in phases:
        res.append(list(outs[b:b + len(ph['outs'])]))
        b += len(ph['outs'])
    return res


def _run_phases(phases, name):
    ins, out_shapes, alias, sems = _phase_operands(phases, 0, 0)
    n_in, n_out = len(ins), len(out_shapes)

    def body(*refs):
        in_refs, out_refs, sem_refs = refs[:n_in], refs[n_in:n_in + n_out], refs[n_in + n_out:]
        _phase_start(phases, in_refs, out_refs, sem_refs)
        _phase_finish(phases, in_refs, out_refs, sem_refs)

    outs = pl.pallas_call(body, name=name, out_shape=out_shapes, in_specs=[ANY] * n_in, out_specs=[ANY] * n_out,
                          input_output_aliases=alias, scratch_shapes=sems)(*ins)
    return _phase_results(phases, outs)


def _cast_place(shard, kind, place_arr, dtype, name):
    full = _full_shape(kind, shard.shape)
    if kind == 'pool':
        g, r, w = shard.shape
        grid = (g,)
        src = pl.BlockSpec((None, r, w), lambda i, s: (i, 0, 0))
        dst = pl.BlockSpec((None, r, w), lambda i, s: (i, s[1], 0))
    else:
        r, w = shard.shape
        tr, tc = _tile2(r, w)
        nrb, ncb = r // tr, w // tc
        grid = (nrb, ncb)
        src = pl.BlockSpec((tr, tc), lambda i, j, s: (i, j))
        if kind == 'col':
            dst = pl.BlockSpec((tr, tc), lambda i, j, s: (i, s[1] * ncb + j))
        else:
            dst = pl.BlockSpec((tr, tc), lambda i, j, s: (s[1] * nrb + i, j))

    def body(s_ref, x_ref, o_ref):
        o_ref[...] = x_ref[...].astype(o_ref.dtype)

    return pl.pallas_call(
        body, name=name,
        grid_spec=pltpu.PrefetchScalarGridSpec(num_scalar_prefetch=1, grid=grid, in_specs=[src], out_specs=dst),
        out_shape=jax.ShapeDtypeStruct(full, dtype), compiler_params=_cp(('parallel',) * len(grid)),
    )(place_arr, shard)


def _tile2(r, c):
    tr = _div_tile(r, max(16, (1 << 19) // c), 16)
    if tr == r or tr * c >= (1 << 17):
        return tr, c
    return r, _div_tile(c, max(128, (1 << 19) // r), 128)


def _pair_sum(grad, theirs, kind, slab, c_arr, name):
    hs = _half_shape(kind, slab)
    if kind == 'pool':
        grid = (N_CHIPS, hs[0], 1)
        own = pl.BlockSpec((None, hs[1], hs[2]), lambda k, i, j, s: (i, 2 * k + s[0], 0))
        stk = pl.BlockSpec((None, None, hs[1], hs[2]), lambda k, i, j, s: (k, i, 0, 0))
    else:
        tr, tc = _tile2(*hs)
        nrb, ncb = hs[0] // tr, hs[1] // tc
        grid = (N_CHIPS, nrb, ncb)
        if kind == 'col':
            own = pl.BlockSpec((tr, tc), lambda k, i, j, s: (s[0] * nrb + i, k * ncb + j))
        else:
            own = pl.BlockSpec((tr, tc), lambda k, i, j, s: (k * nrb + i, s[0] * ncb + j))
        stk = pl.BlockSpec((None, tr, tc), lambda k, i, j, s: (k, i, j))

    def body(s_ref, a_ref, b_ref, o_ref):
        o_ref[...] = (a_ref[...].astype(F32) + b_ref[...].astype(F32)).astype(o_ref.dtype)

    return pl.pallas_call(
        body, name=name,
        grid_spec=pltpu.PrefetchScalarGridSpec(num_scalar_prefetch=1, grid=grid, in_specs=[own, stk], out_specs=stk),
        out_shape=jax.ShapeDtypeStruct((N_CHIPS,) + hs, BF), compiler_params=_cp(('parallel',) * 3),
    )(c_arr, grad, theirs)


def _chip_sum(sums, landed, kind, slab, place_arr, name):
    hs = _half_shape(kind, slab)
    if kind == 'pool':
        grid = (hs[0], 1)
        blk = (None, None, hs[1], hs[2])
        mine = pl.BlockSpec(blk, lambda i, j, s: (s[1], i, 0, 0))
        land = [pl.BlockSpec(blk, lambda i, j, s, p=p: (p, i, 0, 0)) for p in range(3)]
        out = pl.BlockSpec((None, hs[1], hs[2]), lambda i, j, s: (i, s[0], 0))
    else:
        tr, tc = _tile2(*hs)
        nrb, ncb = hs[0] // tr, hs[1] // tc
        grid = (nrb, ncb)
        blk = (None, tr, tc)
        mine = pl.BlockSpec(blk, lambda i, j, s: (s[1], i, j))
        land = [pl.BlockSpec(blk, lambda i, j, s, p=p: (p, i, j)) for p in range(3)]
        if kind == 'col':
            out = pl.BlockSpec((tr, tc), lambda i, j, s: (s[0] * nrb + i, j))
        else:
            out = pl.BlockSpec((tr, tc), lambda i, j, s: (i, s[0] * ncb + j))

    def body(s_ref, a_ref, b_ref, c_ref, d_ref, o_ref):
        o_ref[...] = ((a_ref[...].astype(F32) + b_ref[...].astype(F32)) + c_ref[...].astype(F32)) + d_ref[...].astype(F32)

    return pl.pallas_call(
        body, name=name,
        grid_spec=pltpu.PrefetchScalarGridSpec(num_scalar_prefetch=1, grid=grid, in_specs=[mine] + land, out_specs=out),
        out_shape=jax.ShapeDtypeStruct(tuple(slab), F32), compiler_params=_cp(('parallel',) * 2),
    )(place_arr, sums, landed, landed, landed)


def _all_reduce_small(buf):
    rows, cols = buf.shape

    def body(in_ref, out_ref, land, ssem, rsem):
        x, y, c = _place()
        me = 4 * x + 2 * y + c
        land[me] = in_ref[...]
        started = []
        for j in range(1, 8):
            px = 1 - x if (j >> 2) & 1 else x
            py = 1 - y if (j >> 1) & 1 else y
            pc = 1 - c if j & 1 else c
            cp = pltpu.make_async_remote_copy(src_ref=in_ref, dst_ref=land.at[me], send_sem=ssem.at[j - 1],
                                              recv_sem=rsem.at[j - 1], device_id=(px, py, pc), device_id_type=MESH)
            cp.start()
            started.append(cp)
        for j in range(1, 8):
            px = 1 - x if (j >> 2) & 1 else x
            py = 1 - y if (j >> 1) & 1 else y
            pc = 1 - c if j & 1 else c
            slot = land.at[4 * px + 2 * py + pc]
            pltpu.make_async_remote_copy(src_ref=slot, dst_ref=slot, send_sem=ssem.at[j - 1], recv_sem=rsem.at[j - 1],
                                         device_id=(px, py, pc), device_id_type=MESH).wait_recv()
        for cp in started:
            cp.wait_send()
        acc = land[0]
        for d in range(1, 8):
            acc = acc + land[d]
        out_ref[...] = acc

    return pl.pallas_call(
        body, name='all_reduce_small', out_shape=jax.ShapeDtypeStruct((rows, cols), F32),
        in_specs=[pl.BlockSpec(memory_space=pltpu.VMEM)], out_specs=pl.BlockSpec(memory_space=pltpu.VMEM),
        scratch_shapes=[pltpu.VMEM((8, rows, cols), F32), pltpu.SemaphoreType.DMA((7,)), pltpu.SemaphoreType.DMA((7,))],
    )(buf)


def _elementwise(fn, ins, lead_index, out_shape, out_dtypes, name):
    nd = len(out_shape)
    r, cdim = out_shape[-2], out_shape[-1]
    tr, tc = _tile2(r, cdim)
    grid = tuple(out_shape[:-2]) + (r // tr, cdim // tc)
    block = (None,) * (nd - 2) + (tr, tc)

    def spec(lead):
        if lead is None:
            return pl.BlockSpec(block, lambda *g: tuple(g))
        return pl.BlockSpec((None,) + block, lambda *g, lead=lead: (lead,) + tuple(g))

    n_in = len(ins)

    def body(*refs):
        res = fn(*[r_[...] for r_ in refs[:n_in]])
        for o_ref, v in zip(refs[n_in:], res):
            o_ref[...] = v.astype(o_ref.dtype)

    return pl.pallas_call(
        body, name=name, grid=grid, in_specs=[spec(l) for l in lead_index],
        out_specs=[spec(None) for _ in out_dtypes],
        out_shape=[jax.ShapeDtypeStruct(tuple(out_shape), dt) for dt in out_dtypes],
        compiler_params=_cp(('parallel',) * len(grid)),
    )(*ins)


def _adam_fn(w, g, m, v):
    m = ADAM_B1 * m + (1.0 - ADAM_B1) * g
    v = ADAM_B2 * v + (1.0 - ADAM_B2) * (g * g)
    m_hat = m / (1.0 - ADAM_B1 ** ADAM_STEP)
    v_hat = v / (1.0 - ADAM_B2 ** ADAM_STEP)
    delta = -ADAM_LR * (m_hat / (jnp.sqrt(v_hat) + ADAM_EPS) + ADAM_WD * w)
    return g, delta, m, v


def _adam(w, g, m, v, name):
    return _elementwise(_adam_fn, [w, g, m, v], [None] * 4, w.shape, [F32] * 4, name)


_DIMS = {'nn': (((1,), (0,)), ((), ())), 'nt': (((1,), (1,)), ((), ())), 'tn': (((0,), (0,)), ((), ()))}


def _mm(a, b, mode, out_dtype, name, phases=()):
    if mode == 'nn':
        (M, K), N = a.shape, b.shape[1]
    elif mode == 'nt':
        (M, K), N = a.shape, b.shape[0]
    else:
        (K, M), N = a.shape, b.shape[1]
    if mode == 'tn':
        tm, tn, tk = _div_tile(M, 512, 128), _div_tile(N, 1024, 128), _div_tile(K, MM_TK_ROWS, 16)
        if tm < 256:
            tm = _div_tile(M, 1024, 128)
    else:
        tk = _div_tile(K, MM_TK_COLS, 128)
        tm, tn = _div_tile(M, 1088, 16), _div_tile(N, 1024 if tk <= 2816 else 512, 128)
    nk = K // tk
    a_spec = {'nn': pl.BlockSpec((tm, tk), lambda i, j, k: (i, k)), 'nt': pl.BlockSpec((tm, tk), lambda i, j, k: (i, k)),
              'tn': pl.BlockSpec((tk, tm), lambda i, j, k: (k, i))}[mode]
    b_spec = {'nn': pl.BlockSpec((tk, tn), lambda i, j, k: (k, j)), 'nt': pl.BlockSpec((tn, tk), lambda i, j, k: (j, k)),
              'tn': pl.BlockSpec((tk, tn), lambda i, j, k: (k, j))}[mode]
    dims = _DIMS[mode]
    gm, gn = M // tm, N // tn
    extra_in, extra_out, alias, sems = _phase_operands(phases, 2, 1)
    n_ei, n_eo = len(extra_in), len(extra_out)

    def body(*refs):
        a_ref, b_ref, ein = refs[0], refs[1], refs[2:2 + n_ei]
        o_ref, eout = refs[2 + n_ei], refs[3 + n_ei:3 + n_ei + n_eo]
        acc_ref, sem_refs = refs[3 + n_ei + n_eo], refs[4 + n_ei + n_eo:]
        i, j, k = pl.program_id(0), pl.program_id(1), pl.program_id(2)
        if phases:
            @pl.when((i == 0) & (j == 0) & (k == 0))
            def _():
                _phase_start(phases, ein, eout, sem_refs)

        part = lax.dot_general(a_ref[...], b_ref[...], dims, preferred_element_type=F32)
        if nk == 1:
            o_ref[...] = part.astype(o_ref.dtype)
        else:
            @pl.when(k == 0)
            def _():
                acc_ref[...] = part

            @pl.when(k > 0)
            def _():
                acc_ref[...] += part

            @pl.when(k == nk - 1)
            def _():
                o_ref[...] = acc_ref[...].astype(o_ref.dtype)

        if phases:
            @pl.when((i == gm - 1) & (j == gn - 1) & (k == nk - 1))
            def _():
                _phase_finish(phases, ein, eout, sem_refs)

    outs = pl.pallas_call(
        body, name=name, grid=(gm, gn, nk), in_specs=[a_spec, b_spec] + [ANY] * n_ei,
        out_specs=[pl.BlockSpec((tm, tn), lambda i, j, k: (i, j))] + [ANY] * n_eo,
        out_shape=[jax.ShapeDtypeStruct((M, N), out_dtype)] + extra_out, input_output_aliases=alias,
        scratch_shapes=[pltpu.VMEM((tm, tn) if nk > 1 else (8, 128), F32)] + sems,
        compiler_params=_cp(('arbitrary',) * 3 if phases else ('parallel', 'parallel', 'arbitrary')),
    )(a, b, *extra_in)
    if phases:
        return outs[0], _phase_results(phases, outs[1:])
    return outs[0]


def _rb(tm, w, col=0):
    return pl.BlockSpec((tm, w), lambda i, col=col: (i, col))


def _fixed(shape):
    return pl.BlockSpec(shape, lambda i: (0,) * len(shape))


def _rms(x):
    return lax.rsqrt(jnp.mean(x * x, axis=-1, keepdims=True) + EPS)


def _norm_fwd(x, gain, name, width=None, col=0):
    Lp = x.shape[0]
    width = width or x.shape[1]
    tm = _div_tile(Lp, 256, 16)

    def body(x_ref, g_ref, o_ref):
        v = x_ref[...]
        o_ref[...] = (v * _rms(v) * g_ref[...]).astype(o_ref.dtype)

    return pl.pallas_call(
        body, name=name, grid=(Lp // tm,), in_specs=[_rb(tm, width, col), _fixed((1, width))], out_specs=_rb(tm, width),
        out_shape=jax.ShapeDtypeStruct((Lp, width), BF), compiler_params=_cp(('parallel',)),
    )(x, gain)


def _post_residual(h, f, gain, scale, name, phases=()):
    Lp, D = h.shape
    tm = _div_tile(Lp, 256, 16)

    def body(h_ref, f_ref, g_ref, o_ref):
        v = f_ref[...]
        o_ref[...] = h_ref[...] + scale * (v * _rms(v) * g_ref[...])

    (out,), res = _call_carrying(body, name, (Lp // tm,), [_rb(tm, D), _rb(tm, D), _fixed((1, D))], [_rb(tm, D)],
                                 [jax.ShapeDtypeStruct((Lp, D), F32)], [], (h, f, gain), phases)
    return out, res


def _post_bwd(dh, f, gain, scale, name):
    Lp, D = dh.shape
    tm = _div_tile(Lp, 256, 16)

    def body(dh_ref, f_ref, g_ref, df_ref, dg_ref):
        v = f_ref[...]
        r = _rms(v)
        dy = scale * dh_ref[...]
        w = dy * g_ref[...]
        df_ref[...] = (r * w - v * (r * r * r) * jnp.mean(w * v, axis=-1, keepdims=True)).astype(df_ref.dtype)
        part = jnp.sum(dy * v * r, axis=0, keepdims=True)

        @pl.when(pl.program_id(0) == 0)
        def _():
            dg_ref[...] = part

        @pl.when(pl.program_id(0) > 0)
        def _():
            dg_ref[...] += part

    return pl.pallas_call(
        body, name=name, grid=(Lp // tm,), in_specs=[_rb(tm, D), _rb(tm, D), _fixed((1, D))],
        out_specs=[_rb(tm, D), _fixed((1, D))],
        out_shape=[jax.ShapeDtypeStruct((Lp, D), BF), jax.ShapeDtypeStruct((1, D), F32)],
        compiler_params=_cp(('arbitrary',)),
    )(dh, f, gain)


def _pre_bwd(dres, dn, h, gain, name, phases=()):
    Lp, D = h.shape
    tm = _div_tile(Lp, 256, 16)

    def body(dres_ref, dn_ref, h_ref, g_ref, dh_ref, dg_ref):
        v = h_ref[...]
        r = _rms(v)
        dy = dn_ref[...]
        w = dy * g_ref[...]
        dh_ref[...] = dres_ref[...] + r * w - v * (r * r * r) * jnp.mean(w * v, axis=-1, keepdims=True)
        part = jnp.sum(dy * v * r, axis=0, keepdims=True)

        @pl.when(pl.program_id(0) == 0)
        def _():
            dg_ref[...] = part

        @pl.when(pl.program_id(0) > 0)
        def _():
            dg_ref[...] += part

    return _call_carrying(
        body, name, (Lp // tm,), [_rb(tm, D), _rb(tm, D), _rb(tm, D), _fixed((1, D))], [_rb(tm, D), _fixed((1, D))],
        [jax.ShapeDtypeStruct((Lp, D), F32), jax.ShapeDtypeStruct((1, D), F32)], [], (dres, dn, h, gain), phases)


def _swiglu_fwd(gu, name, phases=()):
    Lp, F2 = gu.shape
    F = F2 // 2
    tm = _div_tile(Lp, 256, 16)

    def body(gu_ref, a_ref):
        g = gu_ref[:, :F].astype(F32)
        u = gu_ref[:, F:].astype(F32)
        a_ref[...] = (g * jax.nn.sigmoid(g) * u).astype(a_ref.dtype)

    return _call_carrying(body, name, (Lp // tm,), [_rb(tm, F2)], [_rb(tm, F)],
                          [jax.ShapeDtypeStruct((Lp, F), BF)], [], (gu,), phases)


def _swiglu_bwd(da, gu, name):
    Lp, F2 = gu.shape
    F = F2 // 2
    tm = _div_tile(Lp, 256, 16)

    def body(da_ref, gu_ref, o_ref):
        g = gu_ref[:, :F].astype(F32)
        u = gu_ref[:, F:].astype(F32)
        da_ = da_ref[...].astype(F32)
        s = jax.nn.sigmoid(g)
        o_ref[:, :F] = (da_ * u * (s * (1.0 + g * (1.0 - s)))).astype(o_ref.dtype)
        o_ref[:, F:] = (da_ * (g * s)).astype(o_ref.dtype)

    return pl.pallas_call(
        body, name=name, grid=(Lp // tm,), in_specs=[_rb(tm, F), _rb(tm, F2)], out_specs=_rb(tm, F2),
        out_shape=jax.ShapeDtypeStruct((Lp, F2), BF), compiler_params=_cp(('parallel',)),
    )(da, gu)


def _tail(h, f, gain, scale, tgt, n_meta, n_real, name):
    Lp, D = h.shape
    tm = _div_tile(Lp, 256, 16)

    def body(h_ref, f_ref, g_ref, t_ref, d_ref, df_ref, dg_ref, l_ref):
        v = f_ref[...]
        r = _rms(v)
        gain_ = g_ref[...]
        row = lax.broadcasted_iota(jnp.int32, (tm, 1), 0) + pl.program_id(0) * tm
        ok = (row >= n_meta) & (row < n_meta + n_real)
        err = jnp.where(ok, h_ref[...] + scale * (v * r * gain_) - t_ref[...], 0.0)
        d_out = err / D
        d_ref[...] = d_out
        dy = scale * d_out
        w = dy * gain_
        df_ref[...] = (r * w - v * (r * r * r) * jnp.mean(w * v, axis=-1, keepdims=True)).astype(df_ref.dtype)
        dg_part = jnp.sum(dy * v * r, axis=0, keepdims=True)
        l_part = jnp.full((1, 128), jnp.sum(err * err), F32)

        @pl.when(pl.program_id(0) == 0)
        def _():
            dg_ref[...] = dg_part
            l_ref[...] = l_part

        @pl.when(pl.program_id(0) > 0)
        def _():
            dg_ref[...] += dg_part
            l_ref[...] += l_part

    return pl.pallas_call(
        body, name=name, grid=(Lp // tm,), in_specs=[_rb(tm, D), _rb(tm, D), _fixed((1, D)), _rb(tm, D)],
        out_specs=[_rb(tm, D), _rb(tm, D), _fixed((1, D)), _fixed((1, 128))],
        out_shape=[jax.ShapeDtypeStruct((Lp, D), F32), jax.ShapeDtypeStruct((Lp, D), BF),
                   jax.ShapeDtypeStruct((1, D), F32), jax.ShapeDtypeStruct((1, 128), F32)],
        compiler_params=_cp(('arbitrary',)),
    )(h, f, gain, tgt)


def _split_bf16(v):
    hi = v.astype(BF)
    return hi, (v - hi.astype(F32)).astype(BF)


def _pool_fwd(z, name):
    Lp = z.shape[0]
    T = _div_tile(Lp, 256, 16)
    G = len(POOL_WINDOWS)

    def body(cur_ref, prev_ref, d_ref):
        i, g = pl.program_id(0), pl.program_id(1)
        w = jnp.left_shift(2, g)
        rr = lax.broadcasted_iota(jnp.int32, (T, T), 0)
        cc = lax.broadcasted_iota(jnp.int32, (T, T), 1)
        b_cur = jnp.where((cc <= rr) & (cc > rr - w), 1.0, 0.0).astype(BF)
        w_prev = jnp.where(i > 0, w, 0)
        b_prev = jnp.where(cc - T > rr - w_prev, 1.0, 0.0).astype(BF)
        u = cur_ref[...]
        s = jnp.zeros((T, POOL_GROUP), F32)
        for part in _split_bf16(u):
            s += jnp.dot(b_cur, part, preferred_element_type=F32)
        for part in _split_bf16(prev_ref[...]):
            s += jnp.dot(b_prev, part, preferred_element_type=F32)
        t = lax.broadcasted_iota(jnp.int32, (T, 1), 0) + i * T
        cnt = jnp.minimum(w, t + 1).astype(F32)
        d_ref[...] = (s / cnt - u).astype(d_ref.dtype)

    return pl.pallas_call(
        body, name=name, grid=(Lp // T, G),
        in_specs=[pl.BlockSpec((T, POOL_GROUP), lambda i, g: (i, g)),
                  pl.BlockSpec((T, POOL_GROUP), lambda i, g: (jnp.maximum(i - 1, 0), g))],
        out_specs=pl.BlockSpec((T, POOL_GROUP), lambda i, g: (i, g)),
        out_shape=jax.ShapeDtypeStruct((Lp, POOL_WIDTH), BF), compiler_params=_cp(('parallel', 'parallel')),
    )(z, z)


def _pool_bwd(dd, name):
    Lp = dd.shape[0]
    T = _div_tile(Lp, 256, 16)
    G = len(POOL_WINDOWS)
    n_t = Lp // T

    def body(cur_ref, next_ref, o_ref):
        i, g = pl.program_id(0), pl.program_id(1)
        w = jnp.left_shift(2, g)
        rr = lax.broadcasted_iota(jnp.int32, (T, T), 0)
        cc = lax.broadcasted_iota(jnp.int32, (T, T), 1)
        b_cur = jnp.where((cc >= rr) & (cc < rr + w), 1.0, 0.0).astype(BF)
        w_next = jnp.where(i < n_t - 1, w, 0)
        b_next = jnp.where(cc + T < rr + w_next, 1.0, 0.0).astype(BF)
        t = lax.broadcasted_iota(jnp.int32, (T, 1), 0) + i * T
        cur = cur_ref[...]
        e_cur = cur / jnp.minimum(w, t + 1).astype(F32)
        e_next = next_ref[...] / jnp.minimum(w, t + T + 1).astype(F32)
        s = jnp.zeros((T, POOL_GROUP), F32)
        for part in _split_bf16(e_cur):
            s += jnp.dot(b_cur, part, preferred_element_type=F32)
        for part in _split_bf16(e_next):
            s += jnp.dot(b_next, part, preferred_element_type=F32)
        o_ref[...] = (s - cur).astype(o_ref.dtype)

    return pl.pallas_call(
        body, name=name, grid=(n_t, G),
        in_specs=[pl.BlockSpec((T, POOL_GROUP), lambda i, g: (i, g)),
                  pl.BlockSpec((T, POOL_GROUP), lambda i, g: (jnp.minimum(i + 1, n_t - 1), g))],
        out_specs=pl.BlockSpec((T, POOL_GROUP), lambda i, g: (i, g)),
        out_shape=jax.ShapeDtypeStruct((Lp, POOL_WIDTH), BF), compiler_params=_cp(('parallel', 'parallel')),
    )(dd, dd)


def _pool_mix_fwd(d, pool_w, scale, name):
    Lp = d.shape[0]
    G = len(POOL_WINDOWS)
    tm = _div_tile(Lp, 1088, 16)

    def body(d_ref, w_ref, s_ref, e_ref, y_ref):
        e = jnp.dot(d_ref[...], w_ref[...], preferred_element_type=F32)
        e_ref[...] = e.astype(e_ref.dtype)
        y_ref[...] = (e * s_ref[...]).astype(y_ref.dtype)

    blk = pl.BlockSpec((tm, POOL_GROUP), lambda g, i: (i, g))
    return pl.pallas_call(
        body, name=name, grid=(G, Lp // tm),
        in_specs=[blk, pl.BlockSpec((None, POOL_GROUP, POOL_GROUP), lambda g, i: (g, 0, 0)),
                  pl.BlockSpec((1, POOL_GROUP), lambda g, i: (0, g))],
        out_specs=[blk, blk], out_shape=[jax.ShapeDtypeStruct((Lp, POOL_WIDTH), BF)] * 2,
        compiler_params=_cp(('parallel', 'parallel')),
    )(d, pool_w, scale)


def _pool_mix_bwd(dyp, e, d, pool_w, scale, name):
    Lp = d.shape[0]
    G = len(POOL_WINDOWS)
    tm = _div_tile(Lp, 1088, 16)

    def body(dy_ref, e_ref, d_ref, w_ref, s_ref, dd_ref, ds_ref, dw_ref):
        i = pl.program_id(1)
        dy = dy_ref[...]
        de = (dy * s_ref[...]).astype(BF)
        dd_ref[...] = lax.dot_general(de, w_ref[...], _DIMS['nt'], preferred_element_type=F32)
        ds_part = jnp.sum(dy * e_ref[...].astype(F32), axis=0, keepdims=True)
        dw_part = lax.dot_general(d_ref[...], de, _DIMS['tn'], preferred_element_type=F32)

        @pl.when(i == 0)
        def _():
            ds_ref[...] = ds_part
            dw_ref[...] = dw_part

        @pl.when(i > 0)
        def _():
            ds_ref[...] += ds_part
            dw_ref[...] += dw_part

    blk = pl.BlockSpec((tm, POOL_GROUP), lambda g, i: (i, g))
    wblk = pl.BlockSpec((None, POOL_GROUP, POOL_GROUP), lambda g, i: (g, 0, 0))
    sblk = pl.BlockSpec((1, POOL_GROUP), lambda g, i: (0, g))
    return pl.pallas_call(
        body, name=name, grid=(G, Lp // tm), in_specs=[blk, blk, blk, wblk, sblk], out_specs=[blk, sblk, wblk],
        out_shape=[jax.ShapeDtypeStruct((Lp, POOL_WIDTH), F32), jax.ShapeDtypeStruct((1, POOL_WIDTH), F32),
                   jax.ShapeDtypeStruct((G, POOL_GROUP, POOL_GROUP), F32)],
        compiler_params=_cp(('parallel', 'arbitrary')),
    )(dyp, e, d, pool_w, scale)


def _rot_half(t):
    lane = lax.broadcasted_iota(jnp.int32, t.shape, 1)
    half = QK_ROPE // 2
    return jnp.where(lane < half, -pltpu.roll(t, 128 - half, 1), pltpu.roll(t, half, 1))


def _lora_norms(z, q_gain, kv_gain, lay, name):
    Lp = z.shape[0]
    QL, KVL = lay['QL'], lay['KVL']
    tm = _div_tile(Lp, 256, 16)

    def body(q_ref, kv_ref, qg_ref, kg_ref, qo_ref, ko_ref):
        a = q_ref[...]
        qo_ref[...] = (a * _rms(a) * qg_ref[...]).astype(BF)
        b = kv_ref[...]
        ko_ref[...] = (b * _rms(b) * kg_ref[...]).astype(BF)

    return pl.pallas_call(
        body, name=name, grid=(Lp // tm,),
        in_specs=[_rb(tm, QL, lay['cq'] // QL), _rb(tm, KVL, lay['ckv'] // KVL), _fixed((1, QL)), _fixed((1, KVL))],
        out_specs=[_rb(tm, QL), _rb(tm, KVL)],
        out_shape=[jax.ShapeDtypeStruct((Lp, QL), BF), jax.ShapeDtypeStruct((Lp, KVL), BF)],
        compiler_params=_cp(('parallel',)),
    )(z, z, q_gain, kv_gain)


def _lora_norms_bwd(dqn, dkn, z, q_gain, kv_gain, lay, name):
    Lp = z.shape[0]
    QL, KVL = lay['QL'], lay['KVL']
    tm = _div_tile(Lp, 256, 16)

    def one(dy, v, gain):
        r = _rms(v)
        w = dy * gain
        return r * w - v * (r * r * r) * jnp.mean(w * v, axis=-1, keepdims=True), jnp.sum(dy * v * r, axis=0, keepdims=True)

    def body(dq_ref, dk_ref, q_ref, kv_ref, qg_ref, kg_ref, o_ref, dqg_ref, dkg_ref):
        da, ga = one(dq_ref[...], q_ref[...], qg_ref[...])
        db, gb = one(dk_ref[...], kv_ref[...], kg_ref[...])
        o_ref[:, :QL] = da.astype(BF)
        o_ref[:, QL:] = db.astype(BF)

        @pl.when(pl.program_id(0) == 0)
        def _():
            dqg_ref[...] = ga
            dkg_ref[...] = gb

        @pl.when(pl.program_id(0) > 0)
        def _():
            dqg_ref[...] += ga
            dkg_ref[...] += gb

    return pl.pallas_call(
        body, name=name, grid=(Lp // tm,),
        in_specs=[_rb(tm, QL), _rb(tm, KVL), _rb(tm, QL, lay['cq'] // QL), _rb(tm, KVL, lay['ckv'] // KVL),
                  _fixed((1, QL)), _fixed((1, KVL))],
        out_specs=[_rb(tm, QL + KVL), _fixed((1, QL)), _fixed((1, KVL))],
        out_shape=[jax.ShapeDtypeStruct((Lp, QL + KVL), BF), jax.ShapeDtypeStruct((1, QL), F32),
                   jax.ShapeDtypeStruct((1, KVL), F32)],
        compiler_params=_cp(('arbitrary',)),
    )(dqn, dkn, z, z, q_gain, kv_gain)


def _qk_prep(q_raw, kv, z, cos, sin, lay, H, name):
    Lp = q_raw.shape[0]
    W = H * HEAD_PAD
    tm = _div_tile(Lp, 256, 16)

    def body(q_ref, kv_ref, kr_ref, c_ref, s_ref, qo_ref, ko_ref):
        c, s = c_ref[...], s_ref[...]

        def rope(t):
            return t * c + _rot_half(t) * s

        kpe = rope(kr_ref[...]).astype(BF)
        for h in range(H):
            b = h * HEAD_PAD
            qo_ref[:, b:b + 128] = (q_ref[:, b:b + 128] * SOFTMAX_SCALE).astype(BF)
            qo_ref[:, b + 128:b + 256] = (rope(q_ref[:, b + 128:b + 256]) * SOFTMAX_SCALE).astype(BF)
            ko_ref[:, b:b + 128] = kv_ref[:, b:b + 128]
            ko_ref[:, b + 128:b + 256] = kpe

    return pl.pallas_call(
        body, name=name, grid=(Lp // tm,),
        in_specs=[_rb(tm, W), _rb(tm, W), _rb(tm, 128, lay['kr'] // 128), _rb(tm, 128), _rb(tm, 128)],
        out_specs=[_rb(tm, W), _rb(tm, W)], out_shape=[jax.ShapeDtypeStruct((Lp, W), BF)] * 2,
        compiler_params=_cp(('parallel',)),
    )(q_raw, kv, z, cos, sin)


def _qk_prep_bwd(dQ, dK, dV, cos, sin, H, name):
    Lp = dQ.shape[0]
    W = H * HEAD_PAD
    tm = _div_tile(Lp, 256, 16)

    def body(dq_ref, dk_ref, dv_ref, c_ref, s_ref, qo_ref, kvo_ref, kro_ref):
        c, s = c_ref[...], s_ref[...]

        def unrope(t):
            return t * c - _rot_half(t * s)

        acc = jnp.zeros((tm, 128), F32)
        for h in range(H):
            b = h * HEAD_PAD
            qo_ref[:, b:b + 128] = (dq_ref[:, b:b + 128] * SOFTMAX_SCALE).astype(BF)
            qo_ref[:, b + 128:b + 256] = (unrope(dq_ref[:, b + 128:b + 256]) * SOFTMAX_SCALE).astype(BF)
            kvo_ref[:, b:b + 128] = dk_ref[:, b:b + 128].astype(BF)
            kvo_ref[:, b + 128:b + 256] = dv_ref[:, h * V_DIM:(h + 1) * V_DIM]
            acc += dk_ref[:, b + 128:b + 256]
        kro_ref[...] = unrope(acc).astype(BF)

    return pl.pallas_call(
        body, name=name, grid=(Lp // tm,),
        in_specs=[_rb(tm, W), _rb(tm, W), _rb(tm, H * V_DIM), _rb(tm, 128), _rb(tm, 128)],
        out_specs=[_rb(tm, W), _rb(tm, W), _rb(tm, 128)],
        out_shape=[jax.ShapeDtypeStruct((Lp, W), BF), jax.ShapeDtypeStruct((Lp, W), BF),
                   jax.ShapeDtypeStruct((Lp, 128), BF)],
        compiler_params=_cp(('parallel',)),
    )(dQ, dK, dV, cos, sin)


def _call_carrying(core, name, grid, in_specs, out_specs, out_shape, scratch, args, phases):
    n_in, n_out, n_scr = len(in_specs), len(out_specs), len(scratch)
    extra_in, extra_out, alias, sems = _phase_operands(phases, n_in, n_out)
    n_ei, n_eo = len(extra_in), len(extra_out)

    def body(*refs):
        ins, ein = refs[:n_in], refs[n_in:n_in + n_ei]
        outs = refs[n_in + n_ei:n_in + n_ei + n_out]
        eout = refs[n_in + n_ei + n_out:n_in + n_ei + n_out + n_eo]
        rest = refs[n_in + n_ei + n_out + n_eo:]
        scr, sem_refs = rest[:n_scr], rest[n_scr:]
        if phases:
            ids = [pl.program_id(d) for d in range(len(grid))]
            first, last = ids[0] == 0, ids[0] == grid[0] - 1
            for d in range(1, len(grid)):
                first, last = first & (ids[d] == 0), last & (ids[d] == grid[d] - 1)

            @pl.when(first)
            def _():
                _phase_start(phases, ein, eout, sem_refs)

        core(*ins, *outs, *scr)
        if phases:
            @pl.when(last)
            def _():
                _phase_finish(phases, ein, eout, sem_refs)

    outs = pl.pallas_call(
        body, name=name, grid=grid, in_specs=list(in_specs) + [ANY] * n_ei, out_specs=list(out_specs) + [ANY] * n_eo,
        out_shape=list(out_shape) + extra_out, input_output_aliases=alias, scratch_shapes=list(scratch) + sems,
        compiler_params=_cp(('arbitrary',) * len(grid)),
    )(*args, *extra_in)
    return list(outs[:n_out]), _phase_results(phases, outs[n_out:])


def _flash_fwd(Q, K, kv, H, name, phases=()):
    Lp = Q.shape[0]
    T = _div_tile(Lp, 256, 16)

    n_t = Lp // T
    FC = min(FLASH_FWD_CHUNK, n_t)
    CH = FC * T

    def body(q_ref, k_ref, v_ref, o_ref, lse_ref, m_s, l_s, acc_s):
        i = pl.program_id(1)
        q = q_ref[...]
        m_s[...] = jnp.full((T, 1), NEG, F32)
        l_s[...] = jnp.zeros((T, 1), F32)
        acc_s[...] = jnp.zeros((T, V_DIM), F32)

        def step(start, width, masked):
            rows = pl.ds(start, width)
            s = lax.dot_general(q, k_ref[rows, :], _DIMS['nt'], preferred_element_type=F32)
            if masked:
                rr = lax.broadcasted_iota(jnp.int32, (T, width), 0) + i * T
                cc = lax.broadcasted_iota(jnp.int32, (T, width), 1) + start
                s = jnp.where(cc <= rr, s, NEG)
            m = m_s[...]
            m_new = jnp.maximum(m, jnp.max(s, axis=-1, keepdims=True))
            alpha = jnp.exp(m - m_new)
            p = jnp.exp(s - m_new)
            l_s[...] = alpha * l_s[...] + jnp.sum(p, axis=-1, keepdims=True)
            acc_s[...] = alpha * acc_s[...] + jnp.dot(p.astype(BF), v_ref[rows, :], preferred_element_type=F32)
            m_s[...] = m_new

        n_full = i // FC

        def full(cidx, carry):
            step(pl.multiple_of(cidx * CH, CH), CH, False)
            return carry

        if n_t > FC:
            lax.fori_loop(0, n_full, full, 0)
        for nb in range(1, FC + 1):
            @pl.when(i % FC == nb - 1)
            def _(nb=nb):
                step(pl.multiple_of(n_full * CH, CH), nb * T, True)

        l = l_s[...]
        o_ref[...] = (acc_s[...] / l).astype(o_ref.dtype)
        lse_ref[...] = jnp.broadcast_to(m_s[...] + jnp.log(l), (T, 128))

    return _call_carrying(
        body, name, (H, Lp // T),
        [pl.BlockSpec((T, HEAD_PAD), lambda h, i: (i, h)), pl.BlockSpec((Lp, HEAD_PAD), lambda h, i: (0, h)),
         pl.BlockSpec((Lp, V_DIM), lambda h, i: (0, 2 * h + 1))],
        [pl.BlockSpec((T, V_DIM), lambda h, i: (i, h)), pl.BlockSpec((None, T, 128), lambda h, i: (h, i, 0))],
        [jax.ShapeDtypeStruct((Lp, H * V_DIM), BF), jax.ShapeDtypeStruct((H, Lp, 128), F32)],
        [pltpu.VMEM((T, 1), F32), pltpu.VMEM((T, 1), F32), pltpu.VMEM((T, V_DIM), F32)], (Q, K, kv), phases)


def _flash_bwd(Q, K, kv, O, dO, lse, H, name, phases=()):
    Lp = Q.shape[0]
    T = _div_tile(Lp, 256, 16)
    n_t = Lp // T
    FC = min(FLASH_BWD_CHUNK, n_t)

    def body(q_ref, k_ref, v_ref, o_ref, do_ref, lse_ref, dq_ref, dk_ref, dv_ref, dk_acc, dv_acc):
        j = pl.program_id(1)

        @pl.when(j == 0)
        def _():
            dq_ref[...] = jnp.zeros_like(dq_ref)

        kj, vj = k_ref[...], v_ref[...]
        dk_acc[...] = jnp.zeros_like(dk_acc)
        dv_acc[...] = jnp.zeros_like(dv_acc)

        def step(start, width, masked):
            rows = pl.ds(start, width)
            qi, doi = q_ref[rows, :], do_ref[rows, :]
            delta = jnp.sum(doi.astype(F32) * o_ref[rows, :].astype(F32), axis=-1, keepdims=True)
            s = lax.dot_general(qi, kj, _DIMS['nt'], preferred_element_type=F32)
            p = jnp.exp(s - lse_ref[rows, :][:, :1])
            if masked:
                rr = lax.broadcasted_iota(jnp.int32, (width, T), 0) + start
                cc = lax.broadcasted_iota(jnp.int32, (width, T), 1) + j * T
                p = jnp.where(cc <= rr, p, 0.0)
            dp = lax.dot_general(doi, vj, _DIMS['nt'], preferred_element_type=F32)
            ds = (p * (dp - delta)).astype(BF)
            dv_acc[...] += lax.dot_general(p.astype(BF), doi, _DIMS['tn'], preferred_element_type=F32)
            dk_acc[...] += lax.dot_general(ds, qi, _DIMS['tn'], preferred_element_type=F32)
            dq_ref[rows, :] += jnp.dot(ds, kj, preferred_element_type=F32)

        head = (n_t - 1 - j) % FC + 1
        for nb in range(1, FC + 1):
            @pl.when(head == nb)
            def _(nb=nb):
                step(pl.multiple_of(j * T, T), nb * T, True)

        def full(cidx, carry):
            step(pl.multiple_of((j + head + cidx * FC) * T, T), FC * T, False)
            return carry

        if n_t > FC:
            lax.fori_loop(0, (n_t - j - head) // FC, full, 0)
        dk_ref[...] = dk_acc[...]
        dv_ref[...] = dv_acc[...].astype(dv_ref.dtype)

    head_q = pl.BlockSpec((Lp, HEAD_PAD), lambda h, j: (0, h))
    head_v = pl.BlockSpec((Lp, V_DIM), lambda h, j: (0, h))
    return _call_carrying(
        body, name, (H, n_t),
        [head_q, pl.BlockSpec((T, HEAD_PAD), lambda h, j: (j, h)), pl.BlockSpec((T, V_DIM), lambda h, j: (j, 2 * h + 1)),
         head_v, head_v, pl.BlockSpec((None, Lp, 128), lambda h, j: (h, 0, 0))],
        [head_q, pl.BlockSpec((T, HEAD_PAD), lambda h, j: (j, h)), pl.BlockSpec((T, V_DIM), lambda h, j: (j, h))],
        [jax.ShapeDtypeStruct((Lp, H * HEAD_PAD), F32), jax.ShapeDtypeStruct((Lp, H * HEAD_PAD), F32),
         jax.ShapeDtypeStruct((Lp, H * V_DIM), BF)],
        [pltpu.VMEM((T, HEAD_PAD), F32), pltpu.VMEM((T, V_DIM), F32)], (Q, K, kv, O, dO, lse), phases)


def _gate_fwd(z, y_pool, y_mla, lay, name):
    Lp, D = y_pool.shape
    tm = _div_tile(Lp, 256, 16)

    def body(gp_ref, gm_ref, yp_ref, ym_ref, o_ref):
        o_ref[...] = (jax.nn.sigmoid(gp_ref[...]) * yp_ref[...] + jax.nn.sigmoid(gm_ref[...]) * ym_ref[...]).astype(BF)

    return pl.pallas_call(
        body, name=name, grid=(Lp // tm,),
        in_specs=[_rb(tm, D, lay['gp'] // D), _rb(tm, D, lay['gm'] // D), _rb(tm, D), _rb(tm, D)], out_specs=_rb(tm, D),
        out_shape=jax.ShapeDtypeStruct((Lp, D), BF), compiler_params=_cp(('parallel',)),
    )(z, z, y_pool, y_mla)


def _gate_bwd(dy, z, y_pool, y_mla, lay, name):
    Lp, D = y_pool.shape
    tm = _div_tile(Lp, 256, 16)

    def body(dy_ref, gp_ref, gm_ref, yp_ref, ym_ref, dp_ref, dm_ref, dg_ref):
        dy_ = dy_ref[...]
        sp, sm = jax.nn.sigmoid(gp_ref[...]), jax.nn.sigmoid(gm_ref[...])
        dp_ref[...] = (dy_ * sp).astype(BF)
        dm_ref[...] = (dy_ * sm).astype(BF)
        dg_ref[:, :D] = (dy_ * yp_ref[...] * (sp * (1.0 - sp))).astype(BF)
        dg_ref[:, D:] = (dy_ * ym_ref[...] * (sm * (1.0 - sm))).astype(BF)

    return pl.pallas_call(
        body, name=name, grid=(Lp // tm,),
        in_specs=[_rb(tm, D), _rb(tm, D, lay['gp'] // D), _rb(tm, D, lay['gm'] // D), _rb(tm, D), _rb(tm, D)],
        out_specs=[_rb(tm, D), _rb(tm, D), _rb(tm, 2 * D)],
        out_shape=[jax.ShapeDtypeStruct((Lp, D), BF), jax.ShapeDtypeStruct((Lp, D), BF),
                   jax.ShapeDtypeStruct((Lp, 2 * D), BF)],
        compiler_params=_cp(('parallel',)),
    )(dy, z, z, y_pool, y_mla)


def _z_layout(D, QL, KVL):
    cq = POOL_WIDTH
    ckv = cq + QL
    gp = -(-(ckv + KVL) // D) * D
    gm = gp + D
    kr = gm + D
    return dict(QL=QL, KVL=KVL, cq=cq, ckv=ckv, gp=gp, gm=gm, kr=kr, width=kr + 128)


def _w_in_aligned(wt, lay, D):
    n0 = POOL_WIDTH + lay['QL'] + lay['KVL']
    parts = [wt[:n0], jnp.zeros((lay['gp'] - n0, D), wt.dtype), wt[n0 + QK_ROPE:], wt[n0:n0 + QK_ROPE],
             jnp.zeros((128 - QK_ROPE, D), wt.dtype)]
    return jnp.concatenate(parts, axis=0)


def _w_in_logical(wt_al, lay, D):
    n0 = POOL_WIDTH + lay['QL'] + lay['KVL']
    return jnp.concatenate([wt_al[:n0], wt_al[lay['kr']:lay['kr'] + QK_ROPE], wt_al[lay['gp']:lay['gp'] + 2 * D]],
                           axis=0)


def kernel(x, meta_tokens, norm_ffn1_pre, norm_ffn1_post, ffn1_w_gu, ffn1_w_down, norm_mix_pre, norm_mix_post, w_in, pool_w, pool_scale, w_pool_o, q_a_norm, w_q_b, kv_a_norm, w_kv_b, w_mla_o, w_out, norm_ffn2_pre, norm_ffn2_post, ffn2_w_gu, ffn2_w_down, loss_target, m_meta_tokens, m_norm_ffn1_pre, m_norm_ffn1_post, m_ffn1_w_gu, m_ffn1_w_down, m_norm_mix_pre, m_norm_mix_post, m_w_in, m_pool_w, m_pool_scale, m_w_pool_o, m_q_a_norm, m_w_q_b, m_kv_a_norm, m_w_kv_b, m_w_mla_o, m_w_out, m_norm_ffn2_pre, m_norm_ffn2_post, m_ffn2_w_gu, m_ffn2_w_down, v_meta_tokens, v_norm_ffn1_pre, v_norm_ffn1_post, v_ffn1_w_gu, v_ffn1_w_down, v_norm_mix_pre, v_norm_mix_post, v_w_in, v_pool_w, v_pool_scale, v_w_pool_o, v_q_a_norm, v_w_q_b, v_kv_a_norm, v_w_kv_b, v_w_mla_o, v_w_out, v_norm_ffn2_pre, v_norm_ffn2_post, v_ffn2_w_gu, v_ffn2_w_down):
    given = dict(locals())
    W = {n: given[n] for n in WEIGHTS}
    M = {n: given['m_' + n] for n in WEIGHTS}
    V = {n: given['v_' + n] for n in WEIGHTS}

    S, D = x.shape[1], x.shape[2]
    NM = meta_tokens.shape[0]
    L = NM + S
    Lp = -(-L // ROW_ALIGN) * ROW_ALIGN
    QL, KVL = w_q_b.shape[1], w_kv_b.shape[1]
    H = w_q_b.shape[2] * N_CHIPS // QK_DIM
    lay = _z_layout(D, QL, KVL)
    gx, gy = lax.axis_index('x'), lax.axis_index('y')
    chip = 2 * gx + gy

    names = list(BIG)
    for d in (W, M, V):
        d['w_in'] = jnp.swapaxes(d['w_in'], 1, 2)
    slab = {n: W[n][0] for n in names}
    kind = dict(BIG, meta_tokens='col')
    slab_shape = {n: tuple(slab[n].shape) for n in names}
    gc = lax.axis_index('c')
    c_arr = jnp.stack([gc]).astype(jnp.int32)
    place_arr = jnp.stack([gc, chip]).astype(jnp.int32)

    full = {n: _cast_place(slab[n], BIG[n], place_arr, BF, f'place_{n}') for n in names}
    full['meta_tokens'] = _cast_place(meta_tokens, 'col', place_arr, F32, 'place_meta_tokens')
    G0 = ['ffn1_w_gu', 'meta_tokens']
    G0B = ['ffn1_w_down']
    G1W = ['w_in']
    G1R = ['pool_w', 'w_pool_o', 'w_q_b', 'w_kv_b']
    G2 = ['w_mla_o', 'w_out']
    G3 = ['ffn2_w_gu', 'ffn2_w_down']

    def gather(phase_fn, group):
        return phase_fn([full[n] for n in group], [kind[n] for n in group])

    def arrived(group, res):
        full.update(zip(group, res))

    arrived(G0, _all_gather([full[n] for n in G0], [kind[n] for n in G0]))
    meta_full = full['meta_tokens']

    pos = jnp.arange(Lp, dtype=F32)
    inv = ROPE_THETA ** (-jnp.arange(0, QK_ROPE, 2, dtype=F32) / QK_ROPE)
    ang = pos[:, None] * inv[None, :]
    ang = jnp.concatenate([ang, ang], axis=-1)
    cos = jnp.pad(jnp.cos(ang), ((0, 0), (0, 128 - QK_ROPE)), constant_values=1.0)
    sin = jnp.pad(jnp.sin(ang), ((0, 0), (0, 128 - QK_ROPE)))

    h0 = jnp.concatenate([meta_full, x[0], jnp.zeros((Lp - L, D), F32)], axis=0)
    tgt = jnp.pad(loss_target[0], ((NM, Lp - L), (0, 0)))

    n1 = _norm_fwd(h0, norm_ffn1_pre, 'ffn1_norm')
    gu1, (res_b, res) = _mm(n1, full['ffn1_w_gu'], 'nn', BF, 'ffn1_gu',
                            [gather(_phase_gather_ici, G0B), gather(_phase_gather_ici, G1R)])
    arrived(G0B, res_b)
    arrived(G1R, res)
    (a1,), (res_b,) = _swiglu_fwd(gu1, 'ffn1_act', [gather(_phase_gather_d2d, G0B)])
    arrived(G0B, res_b)
    f1, (res, res_w) = _mm(a1, full['ffn1_w_down'], 'nn', F32, 'ffn1_down',
                           [gather(_phase_gather_d2d, G1R), gather(_phase_gather_ici, G1W)])
    arrived(G1R, res)
    arrived(G1W, res_w)
    h1, (res_w,) = _post_residual(h0, f1, norm_ffn1_post, 0.5, 'ffn1_res', [gather(_phase_gather_d2d, G1W)])
    arrived(G1W, res_w)

    w_in_al = _w_in_aligned(full['w_in'], lay, D)
    w_q_pad = jnp.pad(full['w_q_b'].reshape(QL, H, QK_DIM), ((0, 0), (0, 0), (0, HEAD_PAD - QK_DIM))).reshape(
        QL, H * HEAD_PAD)

    n2 = _norm_fwd(h1, norm_mix_pre, 'mix_norm')
    z, (res,) = _mm(n2, w_in_al, 'nt', F32, 'mix_in', [gather(_phase_gather_ici, G2)])
    arrived(G2, res)
    d_pool = _pool_fwd(z, 'pool_fwd')
    e_pool, yp = _pool_mix_fwd(d_pool, full['pool_w'], pool_scale, 'pool_mix')
    y_pool = _mm(yp, full['w_pool_o'], 'nn', F32, 'pool_out')
    cqn, ckvn = _lora_norms(z, q_a_norm, kv_a_norm, lay, 'lora_norms')
    q_raw = _mm(cqn, w_q_pad, 'nn', F32, 'mla_q')
    kv = _mm(ckvn, full['w_kv_b'], 'nn', BF, 'mla_kv')
    Q, K = _qk_prep(q_raw, kv, z, cos, sin, lay, H, 'qk_prep')
    (O, lse), (res2, res3) = _flash_fwd(Q, K, kv, H, 'flash_fwd',
                                        [gather(_phase_gather_d2d, G2), gather(_phase_gather_ici, G3)])
    arrived(G2, res2)
    arrived(G3, res3)
    y_mla, (res,) = _mm(O, full['w_mla_o'], 'nn', F32, 'mla_out', [gather(_phase_gather_d2d, G3)])
    arrived(G3, res)
    y = _gate_fwd(z, y_pool, y_mla, lay, 'gate')
    m_mix = _mm(y, full['w_out'], 'nn', F32, 'mix_out')
    h2, _ = _post_residual(h1, m_mix, norm_mix_post, 1.0, 'mix_res')

    n3 = _norm_fwd(h2, norm_ffn2_pre, 'ffn2_norm')
    gu2 = _mm(n3, full['ffn2_w_gu'], 'nn', BF, 'ffn2_gu')
    (a2,), _ = _swiglu_fwd(gu2, 'ffn2_act')
    f2 = _mm(a2, full['ffn2_w_down'], 'nn', F32, 'ffn2_down')

    G, theirs, sums, halves, reduced = {}, {}, {}, {}, {}
    RA = ['ffn2_w_down', 'ffn2_w_gu']
    RB1 = ['w_out', 'w_pool_o', 'pool_w', 'w_mla_o']
    RB2 = ['w_q_b', 'w_kv_b']
    RB3 = ['w_in']
    RC1 = ['ffn1_w_down']
    RC2 = ['ffn1_w_gu']

    def pair_phase(group):
        return _phase_pair([G[n] for n in group], [BIG[n] for n in group], [slab_shape[n] for n in group])

    def pair_sums(group, res):
        for n, t in zip(group, res):
            sums[n] = _pair_sum(G[n], t, BIG[n], slab_shape[n], c_arr, f'rs_pair_sum_{n}')

    def chip_phase(group):
        return _phase_chip([sums[n] for n in group])

    def chip_sums(group, res):
        for n, ld in zip(group, res):
            halves[n] = _chip_sum(sums[n], ld, BIG[n], slab_shape[n], place_arr, f'rs_chip_sum_{n}')

    def final_phase(group):
        return _phase_final([halves[n] for n in group], [BIG[n] for n in group])

    dh3, df2, G['norm_ffn2_post'], sq = _tail(h2, f2, norm_ffn2_post, 0.5, tgt, NM, S, 'ffn2_tail')

    G['ffn2_w_down'] = _mm(a2, df2, 'tn', BF, 'ffn2_dw_down')
    da2 = _mm(df2, full['ffn2_w_down'], 'nt', BF, 'ffn2_da')
    dgu2 = _swiglu_bwd(da2, gu2, 'ffn2_act_bwd')
    G['ffn2_w_gu'] = _mm(n3, dgu2, 'tn', BF, 'ffn2_dw_gu')
    dn3, (res,) = _mm(dgu2, full['ffn2_w_gu'], 'nt', F32, 'ffn2_dn', [pair_phase(RA)])
    (dh2, G['norm_ffn2_pre']), _ = _pre_bwd(dh3, dn3, h2, norm_ffn2_pre, 'ffn2_norm_bwd')
    pair_sums(RA, res)

    dm, G['norm_mix_post'] = _post_bwd(dh2, m_mix, norm_mix_post, 1.0, 'mix_res_bwd')
    G['w_out'] = _mm(y, dm, 'tn', BF, 'dw_out')
    dy = _mm(dm, full['w_out'], 'nt', F32, 'mix_out_bwd')
    dy_pool, dy_mla, d_gate = _gate_bwd(dy, z, y_pool, y_mla, lay, 'gate_bwd')
    G['w_pool_o'] = _mm(yp, dy_pool, 'tn', BF, 'dw_pool_o')
    dyp = _mm(dy_pool, full['w_pool_o'], 'nt', F32, 'pool_out_bwd')
    dd, G['pool_scale'], d_pool_w = _pool_mix_bwd(dyp, e_pool, d_pool, full['pool_w'], pool_scale, 'pool_mix_bwd')
    G['pool_w'] = d_pool_w.astype(BF)
    du_pool = _pool_bwd(dd, 'pool_bwd')
    G['w_mla_o'] = _mm(O, dy_mla, 'tn', BF, 'dw_mla_o')
    dO, (res,) = _mm(dy_mla, full['w_mla_o'], 'nt', BF, 'mla_out_bwd', [pair_phase(RB1)])
    pair_sums(RB1, res)
    (dQ, dK, dV), (res, res_b) = _flash_bwd(Q, K, kv, O, dO, lse, H, 'flash_bwd', [chip_phase(RA), chip_phase(RB1)])
    chip_sums(RA, res)
    chip_sums(RB1, res_b)
    dq_raw, dkv, dkr = _qk_prep_bwd(dQ, dK, dV, cos, sin, H, 'qk_prep_bwd')
    d_w_q_pad, (res_b,) = _mm(cqn, dq_raw, 'tn', BF, 'dw_q_b', [final_phase(RB1)])
    reduced.update(zip(RB1, res_b))
    G['w_q_b'] = d_w_q_pad.reshape(QL, H, HEAD_PAD)[:, :, :QK_DIM].reshape(QL, H * QK_DIM)
    dcqn = _mm(dq_raw, w_q_pad, 'nt', F32, 'mla_q_bwd')
    G['w_kv_b'] = _mm(ckvn, dkv, 'tn', BF, 'dw_kv_b')
    dckvn, (res,) = _mm(dkv, full['w_kv_b'], 'nt', F32, 'mla_kv_bwd', [pair_phase(RB2)])
    pair_sums(RB2, res)
    d_lora, G['q_a_norm'], G['kv_a_norm'] = _lora_norms_bwd(dcqn, dckvn, z, q_a_norm, kv_a_norm, lay, 'lora_norms_bwd')
    n0 = POOL_WIDTH + QL + KVL
    dz = jnp.concatenate([du_pool, d_lora, jnp.zeros((Lp, lay['gp'] - n0), BF), d_gate, dkr], axis=1)
    d_w_in_al, (res, res_b) = _mm(dz, n2, 'tn', BF, 'dw_in', [final_phase(RA), chip_phase(RB2)])
    reduced.update(zip(RA, res))
    chip_sums(RB2, res_b)
    G['w_in'] = _w_in_logical(d_w_in_al, lay, D)
    dn2, (res, res_b) = _mm(dz, w_in_al, 'nn', F32, 'mix_in_bwd', [pair_phase(RB3), final_phase(RB2)])
    reduced.update(zip(RB2, res_b))
    (dh1, G['norm_mix_pre']), _ = _pre_bwd(dh2, dn2, h1, norm_mix_pre, 'mix_norm_bwd')
    pair_sums(RB3, res)

    df1, G['norm_ffn1_post'] = _post_bwd(dh1, f1, norm_ffn1_post, 0.5, 'ffn1_res_bwd')
    G['ffn1_w_down'], (res_b,) = _mm(a1, df1, 'tn', BF, 'ffn1_dw_down', [chip_phase(RB3)])
    chip_sums(RB3, res_b)
    da1, (res, res_b) = _mm(df1, full['ffn1_w_down'], 'nt', BF, 'ffn1_da', [pair_phase(RC1), final_phase(RB3)])
    reduced.update(zip(RB3, res_b))
    dgu1 = _swiglu_bwd(da1, gu1, 'ffn1_act_bwd')
    pair_sums(RC1, res)
    G['ffn1_w_gu'], (res_c1,) = _mm(n1, dgu1, 'tn', BF, 'ffn1_dw_gu', [chip_phase(RC1)])
    chip_sums(RC1, res_c1)
    (res,) = _run_phases([pair_phase(RC2)], 'rs_pair_exchange_tail')
    pair_sums(RC2, res)
    dn1, (res_c2, res_c1) = _mm(dgu1, full['ffn1_w_gu'], 'nt', F32, 'ffn1_dn', [chip_phase(RC2), final_phase(RC1)])
    chip_sums(RC2, res_c2)
    reduced.update(zip(RC1, res_c1))
    (dh0, G['norm_ffn1_pre']), (res_c2,) = _pre_bwd(dh1, dn1, h0, norm_ffn1_pre, 'ffn1_norm_bwd', [final_phase(RC2)])
    reduced.update(zip(RC2, res_c2))
    grad_x = dh0[NM:L][None]

    SW = max(D, POOL_WIDTH)

    def widen(a, fill=0.0):
        return jnp.pad(a, ((0, 0), (0, SW - a.shape[1])), constant_values=fill)

    rows = [widen(G[n]) for n in SMALL_VEC] + [widen(dh0[:NM]), widen(sq)]
    n_rows = len(SMALL_VEC) + NM + 1
    pad_rows = -(-n_rows // 8) * 8 - n_rows
    small = _all_reduce_small(jnp.concatenate(rows + [jnp.zeros((pad_rows, SW), F32)], axis=0))
    loss = (0.5 / D) * small[len(SMALL_VEC) + NM, 0]
    for i, n in enumerate(SMALL_VEC):
        reduced[n] = small[i:i + 1, :G[n].shape[1]]
    mw = meta_tokens.shape[1]
    reduced['meta_tokens'] = lax.dynamic_slice(small, (len(SMALL_VEC), chip * mw), (NM, mw))

    grads, deltas, new_m, new_v = {}, {}, {}, {}
    for n in names:
        grads[n], deltas[n], new_m[n], new_v[n] = _adam(W[n], reduced[n][None], M[n], V[n], f'adam_{n}')
    for d in (grads, deltas, new_m, new_v):
        d['w_in'] = jnp.swapaxes(d['w_in'], 1, 2)
    n_vec = len(SMALL_VEC)
    vec_w = jnp.concatenate([widen(W[n]) for n in SMALL_VEC] + [jnp.zeros((16 - n_vec, SW), F32)], axis=0)
    vec_m = jnp.concatenate([widen(M[n]) for n in SMALL_VEC] + [jnp.zeros((16 - n_vec, SW), F32)], axis=0)
    vec_v = jnp.concatenate([widen(V[n], 1.0) for n in SMALL_VEC] + [jnp.ones((16 - n_vec, SW), F32)], axis=0)
    vec_g = jnp.concatenate([small[:n_vec], jnp.zeros((16 - n_vec, SW), F32)], axis=0)
    _, vd, vm, vv = _adam(vec_w, vec_g, vec_m, vec_v, 'adam_vectors')
    for i, n in enumerate(SMALL_VEC):
        wdt = W[n].shape[1]
        grads[n], deltas[n], new_m[n], new_v[n] = reduced[n], vd[i:i + 1, :wdt], vm[i:i + 1, :wdt], vv[i:i + 1, :wdt]
    grads['meta_tokens'], deltas['meta_tokens'], new_m['meta_tokens'], new_v['meta_tokens'] = _adam(
        meta_tokens, reduced['meta_tokens'], m_meta_tokens, v_meta_tokens, 'adam_meta')

    return (loss, grad_x, *[grads[n] for n in WEIGHTS], *[deltas[n] for n in WEIGHTS], *[new_m[n] for n in WEIGHTS],
            *[new_v[n] for n in WEIGHTS])
```

```python
import functools

import jax
import jax.numpy as jnp
import numpy as np
from jax import lax
from jax.experimental import pallas as pl
from jax.experimental.pallas import tpu as pltpu

F32 = jnp.float32
BF = jnp.bfloat16
MESH = pl.DeviceIdType.MESH

EPS = 1e-6
N_CHIPS = 4
POOL_WINDOWS = (2, 4, 8, 16)
POOL_GROUP = 256
POOL_WIDTH = POOL_GROUP * len(POOL_WINDOWS)
QK_NOPE = 128
QK_ROPE = 64
V_DIM = 128
QK_DIM = QK_NOPE + QK_ROPE
HEAD_PAD = 256
ROPE_THETA = 10000.0
SOFTMAX_SCALE = QK_DIM ** -0.5
ADAM_LR = 0.001
ADAM_B1 = 0.9
ADAM_B2 = 0.999
ADAM_EPS = 1e-08
ADAM_WD = 0.01
ADAM_STEP = 10
ROW_ALIGN = 256
VMEM_LIMIT = 56 * 1024 * 1024
MM_TK_ROWS = 4352
MM_TK_COLS = 6400
NEG = -1e30
FLASH_FWD_CHUNK = 17
FLASH_BWD_CHUNK = 8
WEIGHTS = ['meta_tokens', 'norm_ffn1_pre', 'norm_ffn1_post', 'ffn1_w_gu', 'ffn1_w_down', 'norm_mix_pre',
           'norm_mix_post', 'w_in', 'pool_w', 'pool_scale', 'w_pool_o', 'q_a_norm', 'w_q_b', 'kv_a_norm', 'w_kv_b',
           'w_mla_o', 'w_out', 'norm_ffn2_pre', 'norm_ffn2_post', 'ffn2_w_gu', 'ffn2_w_down']
BIG = {'ffn1_w_gu': 'col', 'ffn1_w_down': 'row', 'w_in': 'row', 'pool_w': 'pool', 'w_pool_o': 'col', 'w_q_b': 'col',
       'w_kv_b': 'col', 'w_mla_o': 'row', 'w_out': 'row', 'ffn2_w_gu': 'col', 'ffn2_w_down': 'row'}
SMALL_VEC = ['norm_ffn1_pre', 'norm_ffn1_post', 'norm_mix_pre', 'norm_mix_post', 'norm_ffn2_pre', 'norm_ffn2_post',
             'pool_scale', 'q_a_norm', 'kv_a_norm']


def _div_tile(n, target, align):
    best = None
    for t in range(align, min(n, target) + 1, align):
        if n % t == 0:
            best = t
    return best if best is not None else n


def _cp(sem=None):
    return pltpu.CompilerParams(dimension_semantics=sem, vmem_limit_bytes=VMEM_LIMIT)


def _full_shape(kind, slab):
    if kind == 'col':
        return (slab[0], slab[1] * N_CHIPS)
    if kind == 'row':
        return (slab[0] * N_CHIPS, slab[1])
    return (slab[0], slab[1] * N_CHIPS, slab[2])


def _half_shape(kind, slab):
    if kind == 'pool':
        return (slab[0], slab[1] // 2, slab[2])
    if kind == 'row':
        return (slab[0], slab[1] // 2)
    return (slab[0] // 2, slab[1])


def _half_of_slab(ref, kind, c):
    if kind == 'pool':
        n = ref.shape[1] // 2
        return ref.at[:, pl.ds(c * n, n), :]
    if kind == 'row':
        n = ref.shape[1] // 2
        return ref.at[:, pl.ds(c * n, n)]
    n = ref.shape[0] // 2
    return ref.at[pl.ds(c * n, n), :]


def _piece(ref, kind, k, c):
    if kind == 'col':
        r, w = ref.shape[0] // 2, ref.shape[1] // N_CHIPS
        return ref.at[pl.ds(c * r, r), pl.ds(k * w, w)]
    if kind == 'row':
        r, w = ref.shape[0] // N_CHIPS, ref.shape[1] // 2
        return ref.at[pl.ds(k * r, r), pl.ds(c * w, w)]
    r = ref.shape[1] // (2 * N_CHIPS)
    return ref.at[:, pl.ds((2 * k + c) * r, r), :]


def _place():
    x, y, c = lax.axis_index('x'), lax.axis_index('y'), lax.axis_index('c')
    return x, y, c


def _peer_chip(x, y, j):
    px = 1 - x if (j >> 1) else x
    py = 1 - y if (j & 1) else y
    return px, py


ANY = pl.BlockSpec(memory_space=pl.ANY)


def _all_gather(bufs, kinds):
    n = len(bufs)

    def body(*refs):
        outs = refs[n:2 * n]
        ssem, rsem, fssem, frsem = refs[2 * n:]
        x, y, c = _place()
        me = 2 * x + y
        sib = (x, y, 1 - c)
        sends = []
        for w in range(n):
            for j in (1, 2, 3):
                px, py = _peer_chip(x, y, j)
                mine = _piece(outs[w], kinds[w], me, c)
                sends.append(pltpu.make_async_remote_copy(
                    src_ref=mine, dst_ref=mine, send_sem=ssem.at[w, j - 1], recv_sem=rsem.at[w, j - 1],
                    device_id=(px, py, c), device_id_type=MESH))
        for cp in sends:
            cp.start()
        fwds = []
        for w in range(n):
            for j in (1, 2, 3):
                px, py = _peer_chip(x, y, j)
                got = _piece(outs[w], kinds[w], 2 * px + py, c)
                pltpu.make_async_remote_copy(src_ref=got, dst_ref=got, send_sem=ssem.at[w, j - 1],
                                             recv_sem=rsem.at[w, j - 1], device_id=(px, py, c),
                                             device_id_type=MESH).wait_recv()
                fwd = pltpu.make_async_remote_copy(src_ref=got, dst_ref=got, send_sem=fssem.at[w, j - 1],
                                                   recv_sem=frsem.at[w, j - 1], device_id=sib, device_id_type=MESH)
                fwd.start()
                fwds.append(fwd)
        for w in range(n):
            for j in (1, 2, 3):
                px, py = _peer_chip(x, y, j)
                other = _piece(outs[w], kinds[w], 2 * px + py, 1 - c)
                pltpu.make_async_remote_copy(src_ref=other, dst_ref=other, send_sem=fssem.at[w, j - 1],
                                             recv_sem=frsem.at[w, j - 1], device_id=sib,
                                             device_id_type=MESH).wait_recv()
        for cp in sends + fwds:
            cp.wait_send()

    return pl.pallas_call(
        body, name='all_gather_weights', out_shape=[jax.ShapeDtypeStruct(b.shape, b.dtype) for b in bufs],
        in_specs=[ANY] * n, out_specs=[ANY] * n, input_output_aliases={w: w for w in range(n)},
        scratch_shapes=[pltpu.SemaphoreType.DMA((n, 3)), pltpu.SemaphoreType.DMA((n, 3)),
                        pltpu.SemaphoreType.DMA((n, 3)), pltpu.SemaphoreType.DMA((n, 3))],
    )(*bufs)


def _remote(src, dst, ssem, rsem, k, dev):
    return pltpu.make_async_remote_copy(src_ref=src, dst_ref=dst, send_sem=ssem.at[k], recv_sem=rsem.at[k],
                                        device_id=dev, device_id_type=MESH)


def _phase_gather_ici(bufs, kinds):
    n = len(bufs)

    def build(ins, outs, ssem, rsem):
        x, y, c = _place()
        me = 2 * x + y
        ds = []
        for w in range(n):
            for j in (1, 2, 3):
                px, py = _peer_chip(x, y, j)
                mine = _piece(outs[w], kinds[w], me, c)
                got = _piece(outs[w], kinds[w], 2 * px + py, c)
                k = 3 * w + j - 1
                ds.append((_remote(mine, mine, ssem, rsem, k, (px, py, c)), _remote(got, got, ssem, rsem, k, (px, py, c))))
        return ds

    return dict(ins=list(bufs), outs=[jax.ShapeDtypeStruct(b.shape, b.dtype) for b in bufs],
                alias={w: w for w in range(n)}, nsem=3 * n, build=build)


def _phase_gather_d2d(bufs, kinds):
    n = len(bufs)

    def build(ins, outs, ssem, rsem):
        x, y, c = _place()
        sib = (x, y, 1 - c)
        ds = []
        for w in range(n):
            for j in (1, 2, 3):
                px, py = _peer_chip(x, y, j)
                have = _piece(outs[w], kinds[w], 2 * px + py, c)
                want = _piece(outs[w], kinds[w], 2 * px + py, 1 - c)
                k = 3 * w + j - 1
                ds.append((_remote(have, have, ssem, rsem, k, sib), _remote(want, want, ssem, rsem, k, sib)))
        return ds

    return dict(ins=list(bufs), outs=[jax.ShapeDtypeStruct(b.shape, b.dtype) for b in bufs],
                alias={w: w for w in range(n)}, nsem=3 * n, build=build)


def _phase_pair(grads, kinds, slabs):
    n = len(grads)

    def build(ins, outs, ssem, rsem):
        x, y, c = _place()
        sib = (x, y, 1 - c)
        ds = []
        for w in range(n):
            for k in range(N_CHIPS):
                cp = _remote(_piece(ins[w], kinds[w], k, 1 - c), outs[w].at[k], ssem, rsem, N_CHIPS * w + k, sib)
                ds.append((cp, cp))
        return ds

    return dict(ins=list(grads), alias={}, nsem=N_CHIPS * n, build=build,
                outs=[jax.ShapeDtypeStruct((N_CHIPS,) + _half_shape(k, s), g.dtype)
                      for g, k, s in zip(grads, kinds, slabs)])


def _phase_chip(sums):
    n = len(sums)

    def build(ins, outs, ssem, rsem):
        x, y, c = _place()
        ds = []
        for w in range(n):
            for j in (1, 2, 3):
                px, py = _peer_chip(x, y, j)
                cp = _remote(ins[w].at[2 * px + py], outs[w].at[j - 1], ssem, rsem, 3 * w + j - 1, (px, py, c))
                ds.append((cp, cp))
        return ds

    return dict(ins=list(sums), alias={}, nsem=3 * n, build=build,
                outs=[jax.ShapeDtypeStruct((3,) + s.shape[1:], s.dtype) for s in sums])


def _phase_final(slabs_half, kinds):
    n = len(slabs_half)

    def build(ins, outs, ssem, rsem):
        x, y, c = _place()
        sib = (x, y, 1 - c)
        ds = []
        for w in range(n):
            mine = _half_of_slab(outs[w], kinds[w], c)
            other = _half_of_slab(outs[w], kinds[w], 1 - c)
            ds.append((_remote(mine, mine, ssem, rsem, w, sib), _remote(other, other, ssem, rsem, w, sib)))
        return ds

    return dict(ins=list(slabs_half), outs=[jax.ShapeDtypeStruct(s.shape, s.dtype) for s in slabs_half],
                alias={w: w for w in range(n)}, nsem=n, build=build)


def _phase_operands(phases, n_main_in, n_main_out):
    ins, outs, alias, sems = [], [], {}, []
    for ph in phases:
        for i, o in ph['alias'].items():
            alias[n_main_in + len(ins) + i] = n_main_out + len(outs) + o
        ins += ph['ins']
        outs += ph['outs']
        sems += [pltpu.SemaphoreType.DMA((ph['nsem'],)), pltpu.SemaphoreType.DMA((ph['nsem'],))]
    return ins, outs, alias, sems


def _phase_copies(phases, in_refs, out_refs, sem_refs):
    ds, a, b = [], 0, 0
    for p, ph in enumerate(phases):
        ds += ph['build'](in_refs[a:a + len(ph['ins'])], out_refs[b:b + len(ph['outs'])], sem_refs[2 * p],
                          sem_refs[2 * p + 1])
        a += len(ph['ins'])
        b += len(ph['outs'])
    return ds


def _phase_start(phases, in_refs, out_refs, sem_refs):
    for send, _ in _phase_copies(phases, in_refs, out_refs, sem_refs):
        send.start()


def _phase_finish(phases, in_refs, out_refs, sem_refs):
    ds = _phase_copies(phases, in_refs, out_refs, sem_refs)
    for _, recv in ds:
        recv.wait_recv()
    for send, _ in ds:
        send.wait_send()


def _phase_results(phases, outs):
    res, b = [], 0
    for ph in phases:
        res.append(list(outs[b:b + len(ph['outs'])]))
        b += len(ph['outs'])
    return res


def _run_phases(phases, name):
    ins, out_shapes, alias, sems = _phase_operands(phases, 0, 0)
    n_in, n_out = len(ins), len(out_shapes)

    def body(*refs):
        in_refs, out_refs, sem_refs = refs[:n_in], refs[n_in:n_in + n_out], refs[n_in + n_out:]
        _phase_start(phases, in_refs, out_refs, sem_refs)
        _phase_finish(phases, in_refs, out_refs, sem_refs)

    outs = pl.pallas_call(body, name=name, out_shape=out_shapes, in_specs=[ANY] * n_in, out_specs=[ANY] * n_out,
                          input_output_aliases=alias, scratch_shapes=sems)(*ins)
    return _phase_results(phases, outs)


def _cast_place(shard, kind, place_arr, dtype, name):
    full = _full_shape(kind, shard.shape)
    if kind == 'pool':
        g, r, w = shard.shape
        grid = (g,)
        src = pl.BlockSpec((None, r, w), lambda i, s: (i, 0, 0))
        dst = pl.BlockSpec((None, r, w), lambda i, s: (i, s[1], 0))
    else:
        r, w = shard.shape
        tr, tc = _tile2(r, w)
        nrb, ncb = r // tr, w // tc
        grid = (nrb, ncb)
        src = pl.BlockSpec((tr, tc), lambda i, j, s: (i, j))
        if kind == 'col':
            dst = pl.BlockSpec((tr, tc), lambda i, j, s: (i, s[1] * ncb + j))
        else:
            dst = pl.BlockSpec((tr, tc), lambda i, j, s: (s[1] * nrb + i, j))

    def body(s_ref, x_ref, o_ref):
        o_ref[...] = x_ref[...].astype(o_ref.dtype)

    return pl.pallas_call(
        body, name=name,
        grid_spec=pltpu.PrefetchScalarGridSpec(num_scalar_prefetch=1, grid=grid, in_specs=[src], out_specs=dst),
        out_shape=jax.ShapeDtypeStruct(full, dtype), compiler_params=_cp(('parallel',) * len(grid)),
    )(place_arr, shard)


def _tile2(r, c):
    tr = _div_tile(r, max(16, (1 << 19) // c), 16)
    if tr == r or tr * c >= (1 << 17):
        return tr, c
    return r, _div_tile(c, max(128, (1 << 19) // r), 128)


def _pair_sum(grad, theirs, kind, slab, c_arr, name):
    hs = _half_shape(kind, slab)
    if kind == 'pool':
        grid = (N_CHIPS, hs[0], 1)
        own = pl.BlockSpec((None, hs[1], hs[2]), lambda k, i, j, s: (i, 2 * k + s[0], 0))
        stk = pl.BlockSpec((None, None, hs[1], hs[2]), lambda k, i, j, s: (k, i, 0, 0))
    else:
        tr, tc = _tile2(*hs)
        nrb, ncb = hs[0] // tr, hs[1] // tc
        grid = (N_CHIPS, nrb, ncb)
        if kind == 'col':
            own = pl.BlockSpec((tr, tc), lambda k, i, j, s: (s[0] * nrb + i, k * ncb + j))
        else:
            own = pl.BlockSpec((tr, tc), lambda k, i, j, s: (k * nrb + i, s[0] * ncb + j))
        stk = pl.BlockSpec((None, tr, tc), lambda k, i, j, s: (k, i, j))

    def body(s_ref, a_ref, b_ref, o_ref):
        o_ref[...] = (a_ref[...].astype(F32) + b_ref[...].astype(F32)).astype(o_ref.dtype)

    return pl.pallas_call(
        body, name=name,
        grid_spec=pltpu.PrefetchScalarGridSpec(num_scalar_prefetch=1, grid=grid, in_specs=[own, stk], out_specs=stk),
        out_shape=jax.ShapeDtypeStruct((N_CHIPS,) + hs, BF), compiler_params=_cp(('parallel',) * 3),
    )(c_arr, grad, theirs)


def _chip_sum(sums, landed, kind, slab, place_arr, name):
    hs = _half_shape(kind, slab)
    if kind == 'pool':
        grid = (hs[0], 1)
        blk = (None, None, hs[1], hs[2])
        mine = pl.BlockSpec(blk, lambda i, j, s: (s[1], i, 0, 0))
        land = [pl.BlockSpec(blk, lambda i, j, s, p=p: (p, i, 0, 0)) for p in range(3)]
        out = pl.BlockSpec((None, hs[1], hs[2]), lambda i, j, s: (i, s[0], 0))
    else:
        tr, tc = _tile2(*hs)
        nrb, ncb = hs[0] // tr, hs[1] // tc
        grid = (nrb, ncb)
        blk = (None, tr, tc)
        mine = pl.BlockSpec(blk, lambda i, j, s: (s[1], i, j))
        land = [pl.BlockSpec(blk, lambda i, j, s, p=p: (p, i, j)) for p in range(3)]
        if kind == 'col':
            out = pl.BlockSpec((tr, tc), lambda i, j, s: (s[0] * nrb + i, j))
        else:
            out = pl.BlockSpec((tr, tc), lambda i, j, s: (i, s[0] * ncb + j))

    def body(s_ref, a_ref, b_ref, c_ref, d_ref, o_ref):
        o_ref[...] = ((a_ref[...].astype(F32) + b_ref[...].astype(F32)) + c_ref[...].astype(F32)) + d_ref[...].astype(F32)

    return pl.pallas_call(
        body, name=name,
        grid_spec=pltpu.PrefetchScalarGridSpec(num_scalar_prefetch=1, grid=grid, in_specs=[mine] + land, out_specs=out),
        out_shape=jax.ShapeDtypeStruct(tuple(slab), F32), compiler_params=_cp(('parallel',) * 2),
    )(place_arr, sums, landed, landed, landed)


def _all_reduce_small(buf):
    rows, cols = buf.shape

    def body(in_ref, out_ref, land, ssem, rsem):
        x, y, c = _place()
        me = 4 * x + 2 * y + c
        land[me] = in_ref[...]
        started = []
        for j in range(1, 8):
            px = 1 - x if (j >> 2) & 1 else x
            py = 1 - y if (j >> 1) & 1 else y
            pc = 1 - c if j & 1 else c
            cp = pltpu.make_async_remote_copy(src_ref=in_ref, dst_ref=land.at[me], send_sem=ssem.at[j - 1],
                                              recv_sem=rsem.at[j - 1], device_id=(px, py, pc), device_id_type=MESH)
            cp.start()
            started.append(cp)
        for j in range(1, 8):
            px = 1 - x if (j >> 2) & 1 else x
            py = 1 - y if (j >> 1) & 1 else y
            pc = 1 - c if j & 1 else c
            slot = land.at[4 * px + 2 * py + pc]
            pltpu.make_async_remote_copy(src_ref=slot, dst_ref=slot, send_sem=ssem.at[j - 1], recv_sem=rsem.at[j - 1],
                                         device_id=(px, py, pc), device_id_type=MESH).wait_recv()
        for cp in started:
            cp.wait_send()
        acc = land[0]
        for d in range(1, 8):
            acc = acc + land[d]
        out_ref[...] = acc

    return pl.pallas_call(
        body, name='all_reduce_small', out_shape=jax.ShapeDtypeStruct((rows, cols), F32),
        in_specs=[pl.BlockSpec(memory_space=pltpu.VMEM)], out_specs=pl.BlockSpec(memory_space=pltpu.VMEM),
        scratch_shapes=[pltpu.VMEM((8, rows, cols), F32), pltpu.SemaphoreType.DMA((7,)), pltpu.SemaphoreType.DMA((7,))],
    )(buf)


def _elementwise(fn, ins, lead_index, out_shape, out_dtypes, name):
    nd = len(out_shape)
    r, cdim = out_shape[-2], out_shape[-1]
    tr, tc = _tile2(r, cdim)
    grid = tuple(out_shape[:-2]) + (r // tr, cdim // tc)
    block = (None,) * (nd - 2) + (tr, tc)

    def spec(lead):
        if lead is None:
            return pl.BlockSpec(block, lambda *g: tuple(g))
        return pl.BlockSpec((None,) + block, lambda *g, lead=lead: (lead,) + tuple(g))

    n_in = len(ins)

    def body(*refs):
        res = fn(*[r_[...] for r_ in refs[:n_in]])
        for o_ref, v in zip(refs[n_in:], res):
            o_ref[...] = v.astype(o_ref.dtype)

    return pl.pallas_call(
        body, name=name, grid=grid, in_specs=[spec(l) for l in lead_index],
        out_specs=[spec(None) for _ in out_dtypes],
        out_shape=[jax.ShapeDtypeStruct(tuple(out_shape), dt) for dt in out_dtypes],
        compiler_params=_cp(('parallel',) * len(grid)),
    )(*ins)


def _adam_fn(w, g, m, v):
    m = ADAM_B1 * m + (1.0 - ADAM_B1) * g
    v = ADAM_B2 * v + (1.0 - ADAM_B2) * (g * g)
    m_hat = m / (1.0 - ADAM_B1 ** ADAM_STEP)
    v_hat = v / (1.0 - ADAM_B2 ** ADAM_STEP)
    delta = -ADAM_LR * (m_hat / (jnp.sqrt(v_hat) + ADAM_EPS) + ADAM_WD * w)
    return g, delta, m, v


def _adam(w, g, m, v, name):
    return _elementwise(_adam_fn, [w, g, m, v], [None] * 4, w.shape, [F32] * 4, name)


_DIMS = {'nn': (((1,), (0,)), ((), ())), 'nt': (((1,), (1,)), ((), ())), 'tn': (((0,), (0,)), ((), ()))}


def _mm(a, b, mode, out_dtype, name, phases=()):
    if mode == 'nn':
        (M, K), N = a.shape, b.shape[1]
    elif mode == 'nt':
        (M, K), N = a.shape, b.shape[0]
    else:
        (K, M), N = a.shape, b.shape[1]
    if mode == 'tn':
        tm, tn, tk = _div_tile(M, 512, 128), _div_tile(N, 1024, 128), _div_tile(K, MM_TK_ROWS, 16)
        if tm < 256:
            tm = _div_tile(M, 1024, 128)
    else:
        tk = _div_tile(K, MM_TK_COLS, 128)
        tm, tn = _div_tile(M, 1088, 16), _div_tile(N, 1024 if tk <= 2816 else 512, 128)
    nk = K // tk
    a_spec = {'nn': pl.BlockSpec((tm, tk), lambda i, j, k: (i, k)), 'nt': pl.BlockSpec((tm, tk), lambda i, j, k: (i, k)),
              'tn': pl.BlockSpec((tk, tm), lambda i, j, k: (k, i))}[mode]
    b_spec = {'nn': pl.BlockSpec((tk, tn), lambda i, j, k: (k, j)), 'nt': pl.BlockSpec((tn, tk), lambda i, j, k: (j, k)),
              'tn': pl.BlockSpec((tk, tn), lambda i, j, k: (k, j))}[mode]
    dims = _DIMS[mode]
    gm, gn = M // tm, N // tn
    extra_in, extra_out, alias, sems = _phase_operands(phases, 2, 1)
    n_ei, n_eo = len(extra_in), len(extra_out)

    def body(*refs):
        a_ref, b_ref, ein = refs[0], refs[1], refs[2:2 + n_ei]
        o_ref, eout = refs[2 + n_ei], refs[3 + n_ei:3 + n_ei + n_eo]
        acc_ref, sem_refs = refs[3 + n_ei + n_eo], refs[4 + n_ei + n_eo:]
        i, j, k = pl.program_id(0), pl.program_id(1), pl.program_id(2)
        if phases:
            @pl.when((i == 0) & (j == 0) & (k == 0))
            def _():
                _phase_start(phases, ein, eout, sem_refs)

        part = lax.dot_general(a_ref[...], b_ref[...], dims, preferred_element_type=F32)
        if nk == 1:
            o_ref[...] = part.astype(o_ref.dtype)
        else:
            @pl.when(k == 0)
            def _():
                acc_ref[...] = part

            @pl.when(k > 0)
            def _():
                acc_ref[...] += part

            @pl.when(k == nk - 1)
            def _():
                o_ref[...] = acc_ref[...].astype(o_ref.dtype)

        if phases:
            @pl.when((i == gm - 1) & (j == gn - 1) & (k == nk - 1))
            def _():
                _phase_finish(phases, ein, eout, sem_refs)

    outs = pl.pallas_call(
        body, name=name, grid=(gm, gn, nk), in_specs=[a_spec, b_spec] + [ANY] * n_ei,
        out_specs=[pl.BlockSpec((tm, tn), lambda i, j, k: (i, j))] + [ANY] * n_eo,
        out_shape=[jax.ShapeDtypeStruct((M, N), out_dtype)] + extra_out, input_output_aliases=alias,
        scratch_shapes=[pltpu.VMEM((tm, tn) if nk > 1 else (8, 128), F32)] + sems,
        compiler_params=_cp(('arbitrary',) * 3 if phases else ('parallel', 'parallel', 'arbitrary')),
    )(a, b, *extra_in)
    if phases:
        return outs[0], _phase_results(phases, outs[1:])
    return outs[0]


def _rb(tm, w, col=0):
    return pl.BlockSpec((tm, w), lambda i, col=col: (i, col))


def _fixed(shape):
    return pl.BlockSpec(shape, lambda i: (0,) * len(shape))


def _rms(x):
    return lax.rsqrt(jnp.mean(x * x, axis=-1, keepdims=True) + EPS)


def _norm_fwd(x, gain, name, width=None, col=0):
    Lp = x.shape[0]
    width = width or x.shape[1]
    tm = _div_tile(Lp, 256, 16)

    def body(x_ref, g_ref, o_ref):
        v = x_ref[...]
        o_ref[...] = (v * _rms(v) * g_ref[...]).astype(o_ref.dtype)

    return pl.pallas_call(
        body, name=name, grid=(Lp // tm,), in_specs=[_rb(tm, width, col), _fixed((1, width))], out_specs=_rb(tm, width),
        out_shape=jax.ShapeDtypeStruct((Lp, width), BF), compiler_params=_cp(('parallel',)),
    )(x, gain)


def _post_residual(h, f, gain, scale, next_gain, name, phases=()):
    Lp, D = h.shape
    tm = _div_tile(Lp, 256, 16)

    def body(h_ref, f_ref, g_ref, ng_ref, o_ref, n_ref):
        v = f_ref[...]
        out = h_ref[...] + scale * (v * _rms(v) * g_ref[...])
        o_ref[...] = out
        n_ref[...] = (out * _rms(out) * ng_ref[...]).astype(n_ref.dtype)

    return _call_carrying(
        body, name, (Lp // tm,), [_rb(tm, D), _rb(tm, D), _fixed((1, D)), _fixed((1, D))], [_rb(tm, D), _rb(tm, D)],
        [jax.ShapeDtypeStruct((Lp, D), F32), jax.ShapeDtypeStruct((Lp, D), BF)], [], (h, f, gain, next_gain), phases)


def _pre_bwd(dres, dn, h, gain, name, phases=(), before=None):
    Lp, D = h.shape
    tm = _div_tile(Lp, 256, 16)

    def accumulate(ref, part):
        @pl.when(pl.program_id(0) == 0)
        def _():
            ref[...] = part

        @pl.when(pl.program_id(0) > 0)
        def _():
            ref[...] += part

    def body(dres_ref, dn_ref, h_ref, g_ref, *rest):
        v = h_ref[...]
        r = _rms(v)
        dy = dn_ref[...]
        w = dy * g_ref[...]
        dh = dres_ref[...] + r * w - v * (r * r * r) * jnp.mean(w * v, axis=-1, keepdims=True)
        if before is None:
            dh_ref, dg_ref = rest
        else:
            f_ref, pg_ref, dh_ref, dg_ref, df_ref, dpg_ref = rest
            fv = f_ref[...]
            fr = _rms(fv)
            fdy = before[2] * dh
            fw = fdy * pg_ref[...]
            df_ref[...] = (fr * fw - fv * (fr * fr * fr) * jnp.mean(fw * fv, axis=-1, keepdims=True)).astype(BF)
            accumulate(dpg_ref, jnp.sum(fdy * fv * fr, axis=0, keepdims=True))
        dh_ref[...] = dh
        accumulate(dg_ref, jnp.sum(dy * v * r, axis=0, keepdims=True))

    in_specs = [_rb(tm, D), _rb(tm, D), _rb(tm, D), _fixed((1, D))]
    out_specs = [_rb(tm, D), _fixed((1, D))]
    out_shape = [jax.ShapeDtypeStruct((Lp, D), F32), jax.ShapeDtypeStruct((1, D), F32)]
    args = (dres, dn, h, gain)
    if before is not None:
        in_specs += [_rb(tm, D), _fixed((1, D))]
        out_specs += [_rb(tm, D), _fixed((1, D))]
        out_shape += [jax.ShapeDtypeStruct((Lp, D), BF), jax.ShapeDtypeStruct((1, D), F32)]
        args += (before[0], before[1])
    return _call_carrying(body, name, (Lp // tm,), in_specs, out_specs, out_shape, [], args, phases)


def _swiglu_fwd(gu, name, phases=()):
    Lp, F2 = gu.shape
    F = F2 // 2
    tm = _div_tile(Lp, 256, 16)

    def body(gu_ref, a_ref):
        g = gu_ref[:, :F].astype(F32)
        u = gu_ref[:, F:].astype(F32)
        a_ref[...] = (g * jax.nn.sigmoid(g) * u).astype(a_ref.dtype)

    return _call_carrying(body, name, (Lp // tm,), [_rb(tm, F2)], [_rb(tm, F)],
                          [jax.ShapeDtypeStruct((Lp, F), BF)], [], (gu,), phases)


def _swiglu_bwd(da, gu, name):
    Lp, F2 = gu.shape
    F = F2 // 2
    tm = _div_tile(Lp, 256, 16)

    def body(da_ref, gu_ref, o_ref):
        g = gu_ref[:, :F].astype(F32)
        u = gu_ref[:, F:].astype(F32)
        da_ = da_ref[...].astype(F32)
        s = jax.nn.sigmoid(g)
        o_ref[:, :F] = (da_ * u * (s * (1.0 + g * (1.0 - s)))).astype(o_ref.dtype)
        o_ref[:, F:] = (da_ * (g * s)).astype(o_ref.dtype)

    return pl.pallas_call(
        body, name=name, grid=(Lp // tm,), in_specs=[_rb(tm, F), _rb(tm, F2)], out_specs=_rb(tm, F2),
        out_shape=jax.ShapeDtypeStruct((Lp, F2), BF), compiler_params=_cp(('parallel',)),
    )(da, gu)


def _tail(h, f, gain, scale, tgt, n_meta, n_real, name):
    Lp, D = h.shape
    tm = _div_tile(Lp, 256, 16)

    def body(h_ref, f_ref, g_ref, t_ref, d_ref, df_ref, dg_ref, l_ref):
        v = f_ref[...]
        r = _rms(v)
        gain_ = g_ref[...]
        row = lax.broadcasted_iota(jnp.int32, (tm, 1), 0) + pl.program_id(0) * tm
        ok = (row >= n_meta) & (row < n_meta + n_real)
        err = jnp.where(ok, h_ref[...] + scale * (v * r * gain_) - t_ref[...], 0.0)
        d_out = err / D
        d_ref[...] = d_out
        dy = scale * d_out
        w = dy * gain_
        df_ref[...] = (r * w - v * (r * r * r) * jnp.mean(w * v, axis=-1, keepdims=True)).astype(df_ref.dtype)
        dg_part = jnp.sum(dy * v * r, axis=0, keepdims=True)
        l_part = jnp.full((1, 128), jnp.sum(err * err), F32)

        @pl.when(pl.program_id(0) == 0)
        def _():
            dg_ref[...] = dg_part
            l_ref[...] = l_part

        @pl.when(pl.program_id(0) > 0)
        def _():
            dg_ref[...] += dg_part
            l_ref[...] += l_part

    return pl.pallas_call(
        body, name=name, grid=(Lp // tm,), in_specs=[_rb(tm, D), _rb(tm, D), _fixed((1, D)), _rb(tm, D)],
        out_specs=[_rb(tm, D), _rb(tm, D), _fixed((1, D)), _fixed((1, 128))],
        out_shape=[jax.ShapeDtypeStruct((Lp, D), F32), jax.ShapeDtypeStruct((Lp, D), BF),
                   jax.ShapeDtypeStruct((1, D), F32), jax.ShapeDtypeStruct((1, 128), F32)],
        compiler_params=_cp(('arbitrary',)),
    )(h, f, gain, tgt)


def _split_bf16(v):
    hi = v.astype(BF)
    return hi, (v - hi.astype(F32)).astype(BF)


def _pool_fwd(z, name):
    Lp = z.shape[0]
    T = _div_tile(Lp, 256, 16)
    G = len(POOL_WINDOWS)

    def body(cur_ref, prev_ref, d_ref):
        i, g = pl.program_id(0), pl.program_id(1)
        w = jnp.left_shift(2, g)
        rr = lax.broadcasted_iota(jnp.int32, (T, T), 0)
        cc = lax.broadcasted_iota(jnp.int32, (T, T), 1)
        b_cur = jnp.where((cc <= rr) & (cc > rr - w), 1.0, 0.0).astype(BF)
        w_prev = jnp.where(i > 0, w, 0)
        b_prev = jnp.where(cc - T > rr - w_prev, 1.0, 0.0).astype(BF)
        u = cur_ref[...]
        s = jnp.zeros((T, POOL_GROUP), F32)
        for part in _split_bf16(u):
            s += jnp.dot(b_cur, part, preferred_element_type=F32)
        for part in _split_bf16(prev_ref[...]):
            s += jnp.dot(b_prev, part, preferred_element_type=F32)
        t = lax.broadcasted_iota(jnp.int32, (T, 1), 0) + i * T
        cnt = jnp.minimum(w, t + 1).astype(F32)
        d_ref[...] = (s / cnt - u).astype(d_ref.dtype)

    return pl.pallas_call(
        body, name=name, grid=(Lp // T, G),
        in_specs=[pl.BlockSpec((T, POOL_GROUP), lambda i, g: (i, g)),
                  pl.BlockSpec((T, POOL_GROUP), lambda i, g: (jnp.maximum(i - 1, 0), g))],
        out_specs=pl.BlockSpec((T, POOL_GROUP), lambda i, g: (i, g)),
        out_shape=jax.ShapeDtypeStruct((Lp, POOL_WIDTH), BF), compiler_params=_cp(('parallel', 'parallel')),
    )(z, z)


def _pool_bwd(dd, name):
    Lp = dd.shape[0]
    T = _div_tile(Lp, 256, 16)
    G = len(POOL_WINDOWS)
    n_t = Lp // T

    def body(cur_ref, next_ref, o_ref):
        i, g = pl.program_id(0), pl.program_id(1)
        w = jnp.left_shift(2, g)
        rr = lax.broadcasted_iota(jnp.int32, (T, T), 0)
        cc = lax.broadcasted_iota(jnp.int32, (T, T), 1)
        b_cur = jnp.where((cc >= rr) & (cc < rr + w), 1.0, 0.0).astype(BF)
        w_next = jnp.where(i < n_t - 1, w, 0)
        b_next = jnp.where(cc + T < rr + w_next, 1.0, 0.0).astype(BF)
        t = lax.broadcasted_iota(jnp.int32, (T, 1), 0) + i * T
        cur = cur_ref[...]
        e_cur = cur / jnp.minimum(w, t + 1).astype(F32)
        e_next = next_ref[...] / jnp.minimum(w, t + T + 1).astype(F32)
        s = jnp.zeros((T, POOL_GROUP), F32)
        for part in _split_bf16(e_cur):
            s += jnp.dot(b_cur, part, preferred_element_type=F32)
        for part in _split_bf16(e_next):
            s += jnp.dot(b_next, part, preferred_element_type=F32)
        o_ref[...] = (s - cur).astype(o_ref.dtype)

    return pl.pallas_call(
        body, name=name, grid=(n_t, G),
        in_specs=[pl.BlockSpec((T, POOL_GROUP), lambda i, g: (i, g)),
                  pl.BlockSpec((T, POOL_GROUP), lambda i, g: (jnp.minimum(i + 1, n_t - 1), g))],
        out_specs=pl.BlockSpec((T, POOL_GROUP), lambda i, g: (i, g)),
        out_shape=jax.ShapeDtypeStruct((Lp, POOL_WIDTH), BF), compiler_params=_cp(('parallel', 'parallel')),
    )(dd, dd)


def _pool_mix_fwd(d, pool_w, scale, name):
    Lp = d.shape[0]
    G = len(POOL_WINDOWS)
    tm = _div_tile(Lp, 1088, 16)

    def body(d_ref, w_ref, s_ref, e_ref, y_ref):
        e = jnp.dot(d_ref[...], w_ref[...], preferred_element_type=F32)
        e_ref[...] = e.astype(e_ref.dtype)
        y_ref[...] = (e * s_ref[...]).astype(y_ref.dtype)

    blk = pl.BlockSpec((tm, POOL_GROUP), lambda g, i: (i, g))
    return pl.pallas_call(
        body, name=name, grid=(G, Lp // tm),
        in_specs=[blk, pl.BlockSpec((None, POOL_GROUP, POOL_GROUP), lambda g, i: (g, 0, 0)),
                  pl.BlockSpec((1, POOL_GROUP), lambda g, i: (0, g))],
        out_specs=[blk, blk], out_shape=[jax.ShapeDtypeStruct((Lp, POOL_WIDTH), BF)] * 2,
        compiler_params=_cp(('parallel', 'parallel')),
    )(d, pool_w, scale)


def _pool_mix_bwd(dyp, e, d, pool_w, scale, name):
    Lp = d.shape[0]
    G = len(POOL_WINDOWS)
    tm = _div_tile(Lp, 1088, 16)

    def body(dy_ref, e_ref, d_ref, w_ref, s_ref, dd_ref, ds_ref, dw_ref):
        i = pl.program_id(1)
        dy = dy_ref[...]
        de = (dy * s_ref[...]).astype(BF)
        dd_ref[...] = lax.dot_general(de, w_ref[...], _DIMS['nt'], preferred_element_type=F32)
        ds_part = jnp.sum(dy * e_ref[...].astype(F32), axis=0, keepdims=True)
        dw_part = lax.dot_general(d_ref[...], de, _DIMS['tn'], preferred_element_type=F32)

        @pl.when(i == 0)
        def _():
            ds_ref[...] = ds_part
            dw_ref[...] = dw_part

        @pl.when(i > 0)
        def _():
            ds_ref[...] += ds_part
            dw_ref[...] += dw_part

    blk = pl.BlockSpec((tm, POOL_GROUP), lambda g, i: (i, g))
    wblk = pl.BlockSpec((None, POOL_GROUP, POOL_GROUP), lambda g, i: (g, 0, 0))
    sblk = pl.BlockSpec((1, POOL_GROUP), lambda g, i: (0, g))
    return pl.pallas_call(
        body, name=name, grid=(G, Lp // tm), in_specs=[blk, blk, blk, wblk, sblk], out_specs=[blk, sblk, wblk],
        out_shape=[jax.ShapeDtypeStruct((Lp, POOL_WIDTH), F32), jax.ShapeDtypeStruct((1, POOL_WIDTH), F32),
                   jax.ShapeDtypeStruct((G, POOL_GROUP, POOL_GROUP), F32)],
        compiler_params=_cp(('parallel', 'arbitrary')),
    )(dyp, e, d, pool_w, scale)


def _rot_half(t):
    lane = lax.broadcasted_iota(jnp.int32, t.shape, 1)
    half = QK_ROPE // 2
    return jnp.where(lane < half, -pltpu.roll(t, 128 - half, 1), pltpu.roll(t, half, 1))


def _lora_norms(z, q_gain, kv_gain, lay, name):
    Lp = z.shape[0]
    QL, KVL = lay['QL'], lay['KVL']
    tm = _div_tile(Lp, 256, 16)

    def body(q_ref, kv_ref, qg_ref, kg_ref, qo_ref, ko_ref):
        a = q_ref[...]
        qo_ref[...] = (a * _rms(a) * qg_ref[...]).astype(BF)
        b = kv_ref[...]
        ko_ref[...] = (b * _rms(b) * kg_ref[...]).astype(BF)

    return pl.pallas_call(
        body, name=name, grid=(Lp // tm,),
        in_specs=[_rb(tm, QL, lay['cq'] // QL), _rb(tm, KVL, lay['ckv'] // KVL), _fixed((1, QL)), _fixed((1, KVL))],
        out_specs=[_rb(tm, QL), _rb(tm, KVL)],
        out_shape=[jax.ShapeDtypeStruct((Lp, QL), BF), jax.ShapeDtypeStruct((Lp, KVL), BF)],
        compiler_params=_cp(('parallel',)),
    )(z, z, q_gain, kv_gain)


def _lora_norms_bwd(dqn, dkn, z, q_gain, kv_gain, lay, name):
    Lp = z.shape[0]
    QL, KVL = lay['QL'], lay['KVL']
    tm = _div_tile(Lp, 256, 16)

    def one(dy, v, gain):
        r = _rms(v)
        w = dy * gain
        return r * w - v * (r * r * r) * jnp.mean(w * v, axis=-1, keepdims=True), jnp.sum(dy * v * r, axis=0, keepdims=True)

    def body(dq_ref, dk_ref, q_ref, kv_ref, qg_ref, kg_ref, o_ref, dqg_ref, dkg_ref):
        da, ga = one(dq_ref[...], q_ref[...], qg_ref[...])
        db, gb = one(dk_ref[...], kv_ref[...], kg_ref[...])
        o_ref[:, :QL] = da.astype(BF)
        o_ref[:, QL:] = db.astype(BF)

        @pl.when(pl.program_id(0) == 0)
        def _():
            dqg_ref[...] = ga
            dkg_ref[...] = gb

        @pl.when(pl.program_id(0) > 0)
        def _():
            dqg_ref[...] += ga
            dkg_ref[...] += gb

    return pl.pallas_call(
        body, name=name, grid=(Lp // tm,),
        in_specs=[_rb(tm, QL), _rb(tm, KVL), _rb(tm, QL, lay['cq'] // QL), _rb(tm, KVL, lay['ckv'] // KVL),
                  _fixed((1, QL)), _fixed((1, KVL))],
        out_specs=[_rb(tm, QL + KVL), _fixed((1, QL)), _fixed((1, KVL))],
        out_shape=[jax.ShapeDtypeStruct((Lp, QL + KVL), BF), jax.ShapeDtypeStruct((1, QL), F32),
                   jax.ShapeDtypeStruct((1, KVL), F32)],
        compiler_params=_cp(('arbitrary',)),
    )(dqn, dkn, z, z, q_gain, kv_gain)


def _qk_prep(q_raw, kv, z, cos, sin, lay, H, name):
    Lp = q_raw.shape[0]
    W = H * HEAD_PAD
    tm = _div_tile(Lp, 256, 16)

    def body(q_ref, kv_ref, kr_ref, c_ref, s_ref, qo_ref, ko_ref):
        c, s = c_ref[...], s_ref[...]

        def rope(t):
            return t * c + _rot_half(t) * s

        kpe = rope(kr_ref[...]).astype(BF)
        for h in range(H):
            b = h * HEAD_PAD
            qo_ref[:, b:b + 128] = (q_ref[:, b:b + 128] * SOFTMAX_SCALE).astype(BF)
            qo_ref[:, b + 128:b + 256] = (rope(q_ref[:, b + 128:b + 256]) * SOFTMAX_SCALE).astype(BF)
            ko_ref[:, b:b + 128] = kv_ref[:, b:b + 128]
            ko_ref[:, b + 128:b + 256] = kpe

    return pl.pallas_call(
        body, name=name, grid=(Lp // tm,),
        in_specs=[_rb(tm, W), _rb(tm, W), _rb(tm, 128, lay['kr'] // 128), _rb(tm, 128), _rb(tm, 128)],
        out_specs=[_rb(tm, W), _rb(tm, W)], out_shape=[jax.ShapeDtypeStruct((Lp, W), BF)] * 2,
        compiler_params=_cp(('parallel',)),
    )(q_raw, kv, z, cos, sin)


def _qk_prep_bwd(dQ, dK, dV, cos, sin, H, name):
    Lp = dQ.shape[0]
    W = H * HEAD_PAD
    tm = _div_tile(Lp, 256, 16)

    def body(dq_ref, dk_ref, dv_ref, c_ref, s_ref, qo_ref, kvo_ref, kro_ref):
        c, s = c_ref[...], s_ref[...]

        def unrope(t):
            return t * c - _rot_half(t * s)

        acc = jnp.zeros((tm, 128), F32)
        for h in range(H):
            b = h * HEAD_PAD
            qo_ref[:, b:b + 128] = (dq_ref[:, b:b + 128] * SOFTMAX_SCALE).astype(BF)
            qo_ref[:, b + 128:b + 256] = (unrope(dq_ref[:, b + 128:b + 256]) * SOFTMAX_SCALE).astype(BF)
            kvo_ref[:, b:b + 128] = dk_ref[:, b:b + 128].astype(BF)
            kvo_ref[:, b + 128:b + 256] = dv_ref[:, h * V_DIM:(h + 1) * V_DIM]
            acc += dk_ref[:, b + 128:b + 256]
        kro_ref[...] = unrope(acc).astype(BF)

    return pl.pallas_call(
        body, name=name, grid=(Lp // tm,),
        in_specs=[_rb(tm, W), _rb(tm, W), _rb(tm, H * V_DIM), _rb(tm, 128), _rb(tm, 128)],
        out_specs=[_rb(tm, W), _rb(tm, W), _rb(tm, 128)],
        out_shape=[jax.ShapeDtypeStruct((Lp, W), BF), jax.ShapeDtypeStruct((Lp, W), BF),
                   jax.ShapeDtypeStruct((Lp, 128), BF)],
        compiler_params=_cp(('parallel',)),
    )(dQ, dK, dV, cos, sin)


def _call_carrying(core, name, grid, in_specs, out_specs, out_shape, scratch, args, phases):
    n_in, n_out, n_scr = len(in_specs), len(out_specs), len(scratch)
    extra_in, extra_out, alias, sems = _phase_operands(phases, n_in, n_out)
    n_ei, n_eo = len(extra_in), len(extra_out)

    def body(*refs):
        ins, ein = refs[:n_in], refs[n_in:n_in + n_ei]
        outs = refs[n_in + n_ei:n_in + n_ei + n_out]
        eout = refs[n_in + n_ei + n_out:n_in + n_ei + n_out + n_eo]
        rest = refs[n_in + n_ei + n_out + n_eo:]
        scr, sem_refs = rest[:n_scr], rest[n_scr:]
        if phases:
            ids = [pl.program_id(d) for d in range(len(grid))]
            first, last = ids[0] == 0, ids[0] == grid[0] - 1
            for d in range(1, len(grid)):
                first, last = first & (ids[d] == 0), last & (ids[d] == grid[d] - 1)

            @pl.when(first)
            def _():
                _phase_start(phases, ein, eout, sem_refs)

        core(*ins, *outs, *scr)
        if phases:
            @pl.when(last)
            def _():
                _phase_finish(phases, ein, eout, sem_refs)

    outs = pl.pallas_call(
        body, name=name, grid=grid, in_specs=list(in_specs) + [ANY] * n_ei, out_specs=list(out_specs) + [ANY] * n_eo,
        out_shape=list(out_shape) + extra_out, input_output_aliases=alias, scratch_shapes=list(scratch) + sems,
        compiler_params=_cp(('arbitrary',) * len(grid)),
    )(*args, *extra_in)
    return list(outs[:n_out]), _phase_results(phases, outs[n_out:])


def _flash_fwd(Q, K, kv, H, name, phases=()):
    Lp = Q.shape[0]
    T = _div_tile(Lp, 256, 16)

    n_t = Lp // T
    FC = min(FLASH_FWD_CHUNK, n_t)
    CH = FC * T

    def body(q_ref, k_ref, v_ref, o_ref, lse_ref, m_s, l_s, acc_s):
        i = pl.program_id(1)
        q = q_ref[...]
        m_s[...] = jnp.full((T, 1), NEG, F32)
        l_s[...] = jnp.zeros((T, 1), F32)
        acc_s[...] = jnp.zeros((T, V_DIM), F32)

        def step(start, width, masked):
            rows = pl.ds(start, width)
            s = lax.dot_general(q, k_ref[rows, :], _DIMS['nt'], preferred_element_type=F32)
            if masked:
                rr = lax.broadcasted_iota(jnp.int32, (T, width), 0) + i * T
                cc = lax.broadcasted_iota(jnp.int32, (T, width), 1) + start
                s = jnp.where(cc <= rr, s, NEG)
            m = m_s[...]
            m_new = jnp.maximum(m, jnp.max(s, axis=-1, keepdims=True))
            alpha = jnp.exp(m - m_new)
            p = jnp.exp(s - m_new)
            l_s[...] = alpha * l_s[...] + jnp.sum(p, axis=-1, keepdims=True)
            acc_s[...] = alpha * acc_s[...] + jnp.dot(p.astype(BF), v_ref[rows, :], preferred_element_type=F32)
            m_s[...] = m_new

        n_full = i // FC

        def full(cidx, carry):
            step(pl.multiple_of(cidx * CH, CH), CH, False)
            return carry

        if n_t > FC:
            lax.fori_loop(0, n_full, full, 0)
        for nb in range(1, FC + 1):
            @pl.when(i % FC == nb - 1)
            def _(nb=nb):
                step(pl.multiple_of(n_full * CH, CH), nb * T, True)

        l = l_s[...]
        o_ref[...] = (acc_s[...] / l).astype(o_ref.dtype)
        lse_ref[...] = jnp.broadcast_to(m_s[...] + jnp.log(l), (T, 128))

    return _call_carrying(
        body, name, (H, Lp // T),
        [pl.BlockSpec((T, HEAD_PAD), lambda h, i: (i, h)), pl.BlockSpec((Lp, HEAD_PAD), lambda h, i: (0, h)),
         pl.BlockSpec((Lp, V_DIM), lambda h, i: (0, 2 * h + 1))],
        [pl.BlockSpec((T, V_DIM), lambda h, i: (i, h)), pl.BlockSpec((None, T, 128), lambda h, i: (h, i, 0))],
        [jax.ShapeDtypeStruct((Lp, H * V_DIM), BF), jax.ShapeDtypeStruct((H, Lp, 128), F32)],
        [pltpu.VMEM((T, 1), F32), pltpu.VMEM((T, 1), F32), pltpu.VMEM((T, V_DIM), F32)], (Q, K, kv), phases)


def _flash_bwd(Q, K, kv, O, dO, lse, H, name, phases=()):
    Lp = Q.shape[0]
    T = _div_tile(Lp, 256, 16)
    n_t = Lp // T
    FC = min(FLASH_BWD_CHUNK, n_t)

    def body(q_ref, k_ref, v_ref, o_ref, do_ref, lse_ref, dq_ref, dk_ref, dv_ref, dk_acc, dv_acc):
        j = pl.program_id(1)

        @pl.when(j == 0)
        def _():
            dq_ref[...] = jnp.zeros_like(dq_ref)

        kj, vj = k_ref[...], v_ref[...]
        dk_acc[...] = jnp.zeros_like(dk_acc)
        dv_acc[...] = jnp.zeros_like(dv_acc)

        def step(start, width, masked):
            rows = pl.ds(start, width)
            qi, doi = q_ref[rows, :], do_ref[rows, :]
            delta = jnp.sum(doi.astype(F32) * o_ref[rows, :].astype(F32), axis=-1, keepdims=True)
            s = lax.dot_general(qi, kj, _DIMS['nt'], preferred_element_type=F32)
            p = jnp.exp(s - lse_ref[rows, :][:, :1])
            if masked:
                rr = lax.broadcasted_iota(jnp.int32, (width, T), 0) + start
                cc = lax.broadcasted_iota(jnp.int32, (width, T), 1) + j * T
                p = jnp.where(cc <= rr, p, 0.0)
            dp = lax.dot_general(doi, vj, _DIMS['nt'], preferred_element_type=F32)
            ds = (p * (dp - delta)).astype(BF)
            dv_acc[...] += lax.dot_general(p.astype(BF), doi, _DIMS['tn'], preferred_element_type=F32)
            dk_acc[...] += lax.dot_general(ds, qi, _DIMS['tn'], preferred_element_type=F32)
            dq_ref[rows, :] += jnp.dot(ds, kj, preferred_element_type=F32)

        head = (n_t - 1 - j) % FC + 1
        for nb in range(1, FC + 1):
            @pl.when(head == nb)
            def _(nb=nb):
                step(pl.multiple_of(j * T, T), nb * T, True)

        def full(cidx, carry):
            step(pl.multiple_of((j + head + cidx * FC) * T, T), FC * T, False)
            return carry

        if n_t > FC:
            lax.fori_loop(0, (n_t - j - head) // FC, full, 0)
        dk_ref[...] = dk_acc[...]
        dv_ref[...] = dv_acc[...].astype(dv_ref.dtype)

    head_q = pl.BlockSpec((Lp, HEAD_PAD), lambda h, j: (0, h))
    head_v = pl.BlockSpec((Lp, V_DIM), lambda h, j: (0, h))
    return _call_carrying(
        body, name, (H, n_t),
        [head_q, pl.BlockSpec((T, HEAD_PAD), lambda h, j: (j, h)), pl.BlockSpec((T, V_DIM), lambda h, j: (j, 2 * h + 1)),
         head_v, head_v, pl.BlockSpec((None, Lp, 128), lambda h, j: (h, 0, 0))],
        [head_q, pl.BlockSpec((T, HEAD_PAD), lambda h, j: (j, h)), pl.BlockSpec((T, V_DIM), lambda h, j: (j, h))],
        [jax.ShapeDtypeStruct((Lp, H * HEAD_PAD), F32), jax.ShapeDtypeStruct((Lp, H * HEAD_PAD), F32),
         jax.ShapeDtypeStruct((Lp, H * V_DIM), BF)],
        [pltpu.VMEM((T, HEAD_PAD), F32), pltpu.VMEM((T, V_DIM), F32)], (Q, K, kv, O, dO, lse), phases)


def _gate_fwd(z, y_pool, y_mla, lay, name):
    Lp, D = y_pool.shape
    tm = _div_tile(Lp, 256, 16)

    def body(gp_ref, gm_ref, yp_ref, ym_ref, o_ref):
        o_ref[...] = (jax.nn.sigmoid(gp_ref[...]) * yp_ref[...] + jax.nn.sigmoid(gm_ref[...]) * ym_ref[...]).astype(BF)

    return pl.pallas_call(
        body, name=name, grid=(Lp // tm,),
        in_specs=[_rb(tm, D, lay['gp'] // D), _rb(tm, D, lay['gm'] // D), _rb(tm, D), _rb(tm, D)], out_specs=_rb(tm, D),
        out_shape=jax.ShapeDtypeStruct((Lp, D), BF), compiler_params=_cp(('parallel',)),
    )(z, z, y_pool, y_mla)


def _gate_bwd(dy, z, y_pool, y_mla, lay, name):
    Lp, D = y_pool.shape
    tm = _div_tile(Lp, 256, 16)

    def body(dy_ref, gp_ref, gm_ref, yp_ref, ym_ref, dp_ref, dm_ref, dg_ref):
        dy_ = dy_ref[...]
        sp, sm = jax.nn.sigmoid(gp_ref[...]), jax.nn.sigmoid(gm_ref[...])
        dp_ref[...] = (dy_ * sp).astype(BF)
        dm_ref[...] = (dy_ * sm).astype(BF)
        dg_ref[:, :D] = (dy_ * yp_ref[...] * (sp * (1.0 - sp))).astype(BF)
        dg_ref[:, D:] = (dy_ * ym_ref[...] * (sm * (1.0 - sm))).astype(BF)

    return pl.pallas_call(
        body, name=name, grid=(Lp // tm,),
        in_specs=[_rb(tm, D), _rb(tm, D, lay['gp'] // D), _rb(tm, D, lay['gm'] // D), _rb(tm, D), _rb(tm, D)],
        out_specs=[_rb(tm, D), _rb(tm, D), _rb(tm, 2 * D)],
        out_shape=[jax.ShapeDtypeStruct((Lp, D), BF), jax.ShapeDtypeStruct((Lp, D), BF),
                   jax.ShapeDtypeStruct((Lp, 2 * D), BF)],
        compiler_params=_cp(('parallel',)),
    )(dy, z, z, y_pool, y_mla)


def _z_layout(D, QL, KVL):
    cq = POOL_WIDTH
    ckv = cq + QL
    gp = -(-(ckv + KVL) // D) * D
    gm = gp + D
    kr = gm + D
    return dict(QL=QL, KVL=KVL, cq=cq, ckv=ckv, gp=gp, gm=gm, kr=kr, width=kr + 128)


def _w_in_aligned(wt, lay, D):
    n0 = POOL_WIDTH + lay['QL'] + lay['KVL']
    parts = [wt[:n0], jnp.zeros((lay['gp'] - n0, D), wt.dtype), wt[n0 + QK_ROPE:], wt[n0:n0 + QK_ROPE],
             jnp.zeros((128 - QK_ROPE, D), wt.dtype)]
    return jnp.concatenate(parts, axis=0)


def _w_in_logical(wt_al, lay, D):
    n0 = POOL_WIDTH + lay['QL'] + lay['KVL']
    return jnp.concatenate([wt_al[:n0], wt_al[lay['kr']:lay['kr'] + QK_ROPE], wt_al[lay['gp']:lay['gp'] + 2 * D]],
                           axis=0)


def kernel(x, meta_tokens, norm_ffn1_pre, norm_ffn1_post, ffn1_w_gu, ffn1_w_down, norm_mix_pre, norm_mix_post, w_in, pool_w, pool_scale, w_pool_o, q_a_norm, w_q_b, kv_a_norm, w_kv_b, w_mla_o, w_out, norm_ffn2_pre, norm_ffn2_post, ffn2_w_gu, ffn2_w_down, loss_target, m_meta_tokens, m_norm_ffn1_pre, m_norm_ffn1_post, m_ffn1_w_gu, m_ffn1_w_down, m_norm_mix_pre, m_norm_mix_post, m_w_in, m_pool_w, m_pool_scale, m_w_pool_o, m_q_a_norm, m_w_q_b, m_kv_a_norm, m_w_kv_b, m_w_mla_o, m_w_out, m_norm_ffn2_pre, m_norm_ffn2_post, m_ffn2_w_gu, m_ffn2_w_down, v_meta_tokens, v_norm_ffn1_pre, v_norm_ffn1_post, v_ffn1_w_gu, v_ffn1_w_down, v_norm_mix_pre, v_norm_mix_post, v_w_in, v_pool_w, v_pool_scale, v_w_pool_o, v_q_a_norm, v_w_q_b, v_kv_a_norm, v_w_kv_b, v_w_mla_o, v_w_out, v_norm_ffn2_pre, v_norm_ffn2_post, v_ffn2_w_gu, v_ffn2_w_down):
    given = dict(locals())
    W = {n: given[n] for n in WEIGHTS}
    M = {n: given['m_' + n] for n in WEIGHTS}
    V = {n: given['v_' + n] for n in WEIGHTS}

    S, D = x.shape[1], x.shape[2]
    NM = meta_tokens.shape[0]
    L = NM + S
    Lp = -(-L // ROW_ALIGN) * ROW_ALIGN
    QL, KVL = w_q_b.shape[1], w_kv_b.shape[1]
    H = w_q_b.shape[2] * N_CHIPS // QK_DIM
    lay = _z_layout(D, QL, KVL)
    gx, gy = lax.axis_index('x'), lax.axis_index('y')
    chip = 2 * gx + gy

    names = list(BIG)
    for d in (W, M, V):
        d['w_in'] = jnp.swapaxes(d['w_in'], 1, 2)
    slab = {n: W[n][0] for n in names}
    kind = dict(BIG, meta_tokens='col')
    slab_shape = {n: tuple(slab[n].shape) for n in names}
    gc = lax.axis_index('c')
    c_arr = jnp.stack([gc]).astype(jnp.int32)
    place_arr = jnp.stack([gc, chip]).astype(jnp.int32)

    full = {n: _cast_place(slab[n], BIG[n], place_arr, BF, f'place_{n}') for n in names}
    full['meta_tokens'] = _cast_place(meta_tokens, 'col', place_arr, F32, 'place_meta_tokens')
    G0 = ['ffn1_w_gu', 'meta_tokens']
    G0B = ['ffn1_w_down']
    G1W = ['w_in']
    G1R = ['pool_w', 'w_pool_o', 'w_q_b', 'w_kv_b']
    G2 = ['w_mla_o', 'w_out']
    G3 = ['ffn2_w_gu', 'ffn2_w_down']

    def gather(phase_fn, group):
        return phase_fn([full[n] for n in group], [kind[n] for n in group])

    def arrived(group, res):
        full.update(zip(group, res))

    arrived(G0, _all_gather([full[n] for n in G0], [kind[n] for n in G0]))
    meta_full = full['meta_tokens']

    pos = jnp.arange(Lp, dtype=F32)
    inv = ROPE_THETA ** (-jnp.arange(0, QK_ROPE, 2, dtype=F32) / QK_ROPE)
    ang = pos[:, None] * inv[None, :]
    ang = jnp.concatenate([ang, ang], axis=-1)
    cos = jnp.pad(jnp.cos(ang), ((0, 0), (0, 128 - QK_ROPE)), constant_values=1.0)
    sin = jnp.pad(jnp.sin(ang), ((0, 0), (0, 128 - QK_ROPE)))

    h0 = jnp.concatenate([meta_full, x[0], jnp.zeros((Lp - L, D), F32)], axis=0)
    tgt = jnp.pad(loss_target[0], ((NM, Lp - L), (0, 0)))

    n1 = _norm_fwd(h0, norm_ffn1_pre, 'ffn1_norm')
    gu1, (res_b, res) = _mm(n1, full['ffn1_w_gu'], 'nn', BF, 'ffn1_gu',
                            [gather(_phase_gather_ici, G0B), gather(_phase_gather_ici, G1R)])
    arrived(G0B, res_b)
    arrived(G1R, res)
    (a1,), (res_b,) = _swiglu_fwd(gu1, 'ffn1_act', [gather(_phase_gather_d2d, G0B)])
    arrived(G0B, res_b)
    f1, (res, res_w) = _mm(a1, full['ffn1_w_down'], 'nn', F32, 'ffn1_down',
                           [gather(_phase_gather_d2d, G1R), gather(_phase_gather_ici, G1W)])
    arrived(G1R, res)
    arrived(G1W, res_w)
    (h1, n2), (res_w,) = _post_residual(h0, f1, norm_ffn1_post, 0.5, norm_mix_pre, 'ffn1_res',
                                        [gather(_phase_gather_d2d, G1W)])
    arrived(G1W, res_w)

    w_in_al = _w_in_aligned(full['w_in'], lay, D)
    w_q_pad = jnp.pad(full['w_q_b'].reshape(QL, H, QK_DIM), ((0, 0), (0, 0), (0, HEAD_PAD - QK_DIM))).reshape(
        QL, H * HEAD_PAD)

    z, (res,) = _mm(n2, w_in_al, 'nt', F32, 'mix_in', [gather(_phase_gather_ici, G2)])
    arrived(G2, res)
    d_pool = _pool_fwd(z, 'pool_fwd')
    e_pool, yp = _pool_mix_fwd(d_pool, full['pool_w'], pool_scale, 'pool_mix')
    y_pool = _mm(yp, full['w_pool_o'], 'nn', F32, 'pool_out')
    cqn, ckvn = _lora_norms(z, q_a_norm, kv_a_norm, lay, 'lora_norms')
    q_raw = _mm(cqn, w_q_pad, 'nn', F32, 'mla_q')
    kv = _mm(ckvn, full['w_kv_b'], 'nn', BF, 'mla_kv')
    Q, K = _qk_prep(q_raw, kv, z, cos, sin, lay, H, 'qk_prep')
    (O, lse), (res2, res3) = _flash_fwd(Q, K, kv, H, 'flash_fwd',
                                        [gather(_phase_gather_d2d, G2), gather(_phase_gather_ici, G3)])
    arrived(G2, res2)
    arrived(G3, res3)
    y_mla, (res,) = _mm(O, full['w_mla_o'], 'nn', F32, 'mla_out', [gather(_phase_gather_d2d, G3)])
    arrived(G3, res)
    y = _gate_fwd(z, y_pool, y_mla, lay, 'gate')
    m_mix = _mm(y, full['w_out'], 'nn', F32, 'mix_out')
    (h2, n3), _ = _post_residual(h1, m_mix, norm_mix_post, 1.0, norm_ffn2_pre, 'mix_res')

    gu2 = _mm(n3, full['ffn2_w_gu'], 'nn', BF, 'ffn2_gu')
    (a2,), _ = _swiglu_fwd(gu2, 'ffn2_act')
    f2 = _mm(a2, full['ffn2_w_down'], 'nn', F32, 'ffn2_down')

    G, theirs, sums, halves, reduced = {}, {}, {}, {}, {}
    RA = ['ffn2_w_down', 'ffn2_w_gu']
    RB1 = ['w_out', 'w_pool_o', 'pool_w', 'w_mla_o']
    RB2 = ['w_q_b', 'w_kv_b']
    RB3 = ['w_in']
    RC1 = ['ffn1_w_down']
    RC2 = ['ffn1_w_gu']

    def pair_phase(group):
        return _phase_pair([G[n] for n in group], [BIG[n] for n in group], [slab_shape[n] for n in group])

    def pair_sums(group, res):
        for n, t in zip(group, res):
            sums[n] = _pair_sum(G[n], t, BIG[n], slab_shape[n], c_arr, f'rs_pair_sum_{n}')

    def chip_phase(group):
        return _phase_chip([sums[n] for n in group])

    def chip_sums(group, res):
        for n, ld in zip(group, res):
            halves[n] = _chip_sum(sums[n], ld, BIG[n], slab_shape[n], place_arr, f'rs_chip_sum_{n}')

    def final_phase(group):
        return _phase_final([halves[n] for n in group], [BIG[n] for n in group])

    dh3, df2, G['norm_ffn2_post'], sq = _tail(h2, f2, norm_ffn2_post, 0.5, tgt, NM, S, 'ffn2_tail')

    G['ffn2_w_down'] = _mm(a2, df2, 'tn', BF, 'ffn2_dw_down')
    da2 = _mm(df2, full['ffn2_w_down'], 'nt', BF, 'ffn2_da')
    dgu2 = _swiglu_bwd(da2, gu2, 'ffn2_act_bwd')
    G['ffn2_w_gu'] = _mm(n3, dgu2, 'tn', BF, 'ffn2_dw_gu')
    dn3, (res,) = _mm(dgu2, full['ffn2_w_gu'], 'nt', F32, 'ffn2_dn', [pair_phase(RA)])
    (dh2, G['norm_ffn2_pre'], dm, G['norm_mix_post']), _ = _pre_bwd(
        dh3, dn3, h2, norm_ffn2_pre, 'ffn2_norm_bwd', before=(m_mix, norm_mix_post, 1.0))
    pair_sums(RA, res)

    G['w_out'] = _mm(y, dm, 'tn', BF, 'dw_out')
    dy = _mm(dm, full['w_out'], 'nt', F32, 'mix_out_bwd')
    dy_pool, dy_mla, d_gate = _gate_bwd(dy, z, y_pool, y_mla, lay, 'gate_bwd')
    G['w_pool_o'] = _mm(yp, dy_pool, 'tn', BF, 'dw_pool_o')
    dyp = _mm(dy_pool, full['w_pool_o'], 'nt', F32, 'pool_out_bwd')
    dd, G['pool_scale'], d_pool_w = _pool_mix_bwd(dyp, e_pool, d_pool, full['pool_w'], pool_scale, 'pool_mix_bwd')
    G['pool_w'] = d_pool_w.astype(BF)
    du_pool = _pool_bwd(dd, 'pool_bwd')
    G['w_mla_o'] = _mm(O, dy_mla, 'tn', BF, 'dw_mla_o')
    dO, (res,) = _mm(dy_mla, full['w_mla_o'], 'nt', BF, 'mla_out_bwd', [pair_phase(RB1)])
    pair_sums(RB1, res)
    (dQ, dK, dV), (res, res_b) = _flash_bwd(Q, K, kv, O, dO, lse, H, 'flash_bwd', [chip_phase(RA), chip_phase(RB1)])
    chip_sums(RA, res)
    chip_sums(RB1, res_b)
    dq_raw, dkv, dkr = _qk_prep_bwd(dQ, dK, dV, cos, sin, H, 'qk_prep_bwd')
    d_w_q_pad, (res_b,) = _mm(cqn, dq_raw, 'tn', BF, 'dw_q_b', [final_phase(RB1)])
    reduced.update(zip(RB1, res_b))
    G['w_q_b'] = d_w_q_pad.reshape(QL, H, HEAD_PAD)[:, :, :QK_DIM].reshape(QL, H * QK_DIM)
    dcqn = _mm(dq_raw, w_q_pad, 'nt', F32, 'mla_q_bwd')
    G['w_kv_b'] = _mm(ckvn, dkv, 'tn', BF, 'dw_kv_b')
    dckvn, (res,) = _mm(dkv, full['w_kv_b'], 'nt', F32, 'mla_kv_bwd', [pair_phase(RB2)])
    pair_sums(RB2, res)
    d_lora, G['q_a_norm'], G['kv_a_norm'] = _lora_norms_bwd(dcqn, dckvn, z, q_a_norm, kv_a_norm, lay, 'lora_norms_bwd')
    n0 = POOL_WIDTH + QL + KVL
    dz = jnp.concatenate([du_pool, d_lora, jnp.zeros((Lp, lay['gp'] - n0), BF), d_gate, dkr], axis=1)
    d_w_in_al, (res, res_b) = _mm(dz, n2, 'tn', BF, 'dw_in', [final_phase(RA), chip_phase(RB2)])
    reduced.update(zip(RA, res))
    chip_sums(RB2, res_b)
    G['w_in'] = _w_in_logical(d_w_in_al, lay, D)
    dn2, (res, res_b) = _mm(dz, w_in_al, 'nn', F32, 'mix_in_bwd', [pair_phase(RB3), final_phase(RB2)])
    reduced.update(zip(RB2, res_b))
    (dh1, G['norm_mix_pre'], df1, G['norm_ffn1_post']), _ = _pre_bwd(
        dh2, dn2, h1, norm_mix_pre, 'mix_norm_bwd', before=(f1, norm_ffn1_post, 0.5))
    pair_sums(RB3, res)

    G['ffn1_w_down'], (res_b,) = _mm(a1, df1, 'tn', BF, 'ffn1_dw_down', [chip_phase(RB3)])
    chip_sums(RB3, res_b)
    da1, (res, res_b) = _mm(df1, full['ffn1_w_down'], 'nt', BF, 'ffn1_da', [pair_phase(RC1), final_phase(RB3)])
    reduced.update(zip(RB3, res_b))
    dgu1 = _swiglu_bwd(da1, gu1, 'ffn1_act_bwd')
    pair_sums(RC1, res)
    G['ffn1_w_gu'], (res_c1,) = _mm(n1, dgu1, 'tn', BF, 'ffn1_dw_gu', [chip_phase(RC1)])
    chip_sums(RC1, res_c1)
    (res,) = _run_phases([pair_phase(RC2)], 'rs_pair_exchange_tail')
    pair_sums(RC2, res)
    dn1, (res_c2, res_c1) = _mm(dgu1, full['ffn1_w_gu'], 'nt', F32, 'ffn1_dn', [chip_phase(RC2), final_phase(RC1)])
    chip_sums(RC2, res_c2)
    reduced.update(zip(RC1, res_c1))
    (dh0, G['norm_ffn1_pre']), (res_c2,) = _pre_bwd(dh1, dn1, h0, norm_ffn1_pre, 'ffn1_norm_bwd', [final_phase(RC2)])
    reduced.update(zip(RC2, res_c2))
    grad_x = dh0[NM:L][None]

    SW = max(D, POOL_WIDTH)

    def widen(a, fill=0.0):
        return jnp.pad(a, ((0, 0), (0, SW - a.shape[1])), constant_values=fill)

    rows = [widen(G[n]) for n in SMALL_VEC] + [widen(dh0[:NM]), widen(sq)]
    n_rows = len(SMALL_VEC) + NM + 1
    pad_rows = -(-n_rows // 8) * 8 - n_rows
    small = _all_reduce_small(jnp.concatenate(rows + [jnp.zeros((pad_rows, SW), F32)], axis=0))
    loss = (0.5 / D) * small[len(SMALL_VEC) + NM, 0]
    for i, n in enumerate(SMALL_VEC):
        reduced[n] = small[i:i + 1, :G[n].shape[1]]
    mw = meta_tokens.shape[1]
    reduced['meta_tokens'] = lax.dynamic_slice(small, (len(SMALL_VEC), chip * mw), (NM, mw))

    grads, deltas, new_m, new_v = {}, {}, {}, {}
    for n in names:
        grads[n], deltas[n], new_m[n], new_v[n] = _adam(W[n], reduced[n][None], M[n], V[n], f'adam_{n}')
    for d in (grads, deltas, new_m, new_v):
        d['w_in'] = jnp.swapaxes(d['w_in'], 1, 2)
    n_vec = len(SMALL_VEC)
    vec_w = jnp.concatenate([widen(W[n]) for n in SMALL_VEC] + [jnp.zeros((16 - n_vec, SW), F32)], axis=0)
    vec_m = jnp.concatenate([widen(M[n]) for n in SMALL_VEC] + [jnp.zeros((16 - n_vec, SW), F32)], axis=0)
    vec_v = jnp.concatenate([widen(V[n], 1.0) for n in SMALL_VEC] + [jnp.ones((16 - n_vec, SW), F32)], axis=0)
    vec_g = jnp.concatenate([small[:n_vec], jnp.zeros((16 - n_vec, SW), F32)], axis=0)
    _, vd, vm, vv = _adam(vec_w, vec_g, vec_m, vec_v, 'adam_vectors')
    for i, n in enumerate(SMALL_VEC):
        wdt = W[n].shape[1]
        grads[n], deltas[n], new_m[n], new_v[n] = reduced[n], vd[i:i + 1, :wdt], vm[i:i + 1, :wdt], vv[i:i + 1, :wdt]
    grads['meta_tokens'], deltas['meta_tokens'], new_m['meta_tokens'], new_v['meta_tokens'] = _adam(
        meta_tokens, reduced['meta_tokens'], m_meta_tokens, v_meta_tokens, 'adam_meta')

    return (loss, grad_x, *[grads[n] for n in WEIGHTS], *[deltas[n] for n in WEIGHTS], *[new_m[n] for n in WEIGHTS],
            *[new_v[n] for n in WEIGHTS])
```

```python
import functools

import jax
import jax.numpy as jnp
import numpy as np
from jax import lax
from jax.experimental import pallas as pl
from jax.experimental.pallas import tpu as pltpu

F32 = jnp.float32
BF = jnp.bfloat16
MESH = pl.DeviceIdType.MESH

EPS = 1e-6
N_CHIPS = 4
POOL_WINDOWS = (2, 4, 8, 16)
POOL_GROUP = 256
POOL_WIDTH = POOL_GROUP * len(POOL_WINDOWS)
QK_NOPE = 128
QK_ROPE = 64
V_DIM = 128
QK_DIM = QK_NOPE + QK_ROPE
HEAD_PAD = 256
ROPE_THETA = 10000.0
SOFTMAX_SCALE = QK_DIM ** -0.5
ADAM_LR = 0.001
ADAM_B1 = 0.9
ADAM_B2 = 0.999
ADAM_EPS = 1e-08
ADAM_WD = 0.01
ADAM_STEP = 10
ROW_ALIGN = 256
VMEM_LIMIT = 56 * 1024 * 1024
MM_TK_ROWS = 4352
MM_TK_COLS = 6400
NEG = -1e30
FLASH_FWD_CHUNK = 17
FLASH_BWD_CHUNK = 8
WEIGHTS = ['meta_tokens', 'norm_ffn1_pre', 'norm_ffn1_post', 'ffn1_w_gu', 'ffn1_w_down', 'norm_mix_pre',
           'norm_mix_post', 'w_in', 'pool_w', 'pool_scale', 'w_pool_o', 'q_a_norm', 'w_q_b', 'kv_a_norm', 'w_kv_b',
           'w_mla_o', 'w_out', 'norm_ffn2_pre', 'norm_ffn2_post', 'ffn2_w_gu', 'ffn2_w_down']
BIG = {'ffn1_w_gu': 'col', 'ffn1_w_down': 'row', 'w_in': 'row', 'pool_w': 'pool', 'w_pool_o': 'col', 'w_q_b': 'col',
       'w_kv_b': 'col', 'w_mla_o': 'row', 'w_out': 'row', 'ffn2_w_gu': 'col', 'ffn2_w_down': 'row'}
SMALL_VEC = ['norm_ffn1_pre', 'norm_ffn1_post', 'norm_mix_pre', 'norm_mix_post', 'norm_ffn2_pre', 'norm_ffn2_post',
             'pool_scale', 'q_a_norm', 'kv_a_norm']


def _div_tile(n, target, align):
    best = None
    for t in range(align, min(n, target) + 1, align):
        if n % t == 0:
            best = t
    return best if best is not None else n


def _cp(sem=None):
    return pltpu.CompilerParams(dimension_semantics=sem, vmem_limit_bytes=VMEM_LIMIT)


def _full_shape(kind, slab):
    if kind == 'col':
        return (slab[0], slab[1] * N_CHIPS)
    if kind == 'row':
        return (slab[0] * N_CHIPS, slab[1])
    return (slab[0], slab[1] * N_CHIPS, slab[2])


def _half_shape(kind, slab):
    if kind == 'pool':
        return (slab[0], slab[1] // 2, slab[2])
    if kind == 'row':
        return (slab[0], slab[1] // 2)
    return (slab[0] // 2, slab[1])


def _half_of_slab(ref, kind, c):
    if kind == 'pool':
        n = ref.shape[1] // 2
        return ref.at[:, pl.ds(c * n, n), :]
    if kind == 'row':
        n = ref.shape[1] // 2
        return ref.at[:, pl.ds(c * n, n)]
    n = ref.shape[0] // 2
    return ref.at[pl.ds(c * n, n), :]


def _piece(ref, kind, k, c):
    if kind == 'col':
        r, w = ref.shape[0] // 2, ref.shape[1] // N_CHIPS
        return ref.at[pl.ds(c * r, r), pl.ds(k * w, w)]
    if kind == 'row':
        r, w = ref.shape[0] // N_CHIPS, ref.shape[1] // 2
        return ref.at[pl.ds(k * r, r), pl.ds(c * w, w)]
    r = ref.shape[1] // (2 * N_CHIPS)
    return ref.at[:, pl.ds((2 * k + c) * r, r), :]


def _place():
    x, y, c = lax.axis_index('x'), lax.axis_index('y'), lax.axis_index('c')
    return x, y, c


def _peer_chip(x, y, j):
    px = 1 - x if (j >> 1) else x
    py = 1 - y if (j & 1) else y
    return px, py


ANY = pl.BlockSpec(memory_space=pl.ANY)


def _all_gather(bufs, kinds):
    n = len(bufs)

    def body(*refs):
        outs = refs[n:2 * n]
        ssem, rsem, fssem, frsem = refs[2 * n:]
        x, y, c = _place()
        me = 2 * x + y
        sib = (x, y, 1 - c)
        sends = []
        for w in range(n):
            for j in (1, 2, 3):
                px, py = _peer_chip(x, y, j)
                mine = _piece(outs[w], kinds[w], me, c)
                sends.append(pltpu.make_async_remote_copy(
                    src_ref=mine, dst_ref=mine, send_sem=ssem.at[w, j - 1], recv_sem=rsem.at[w, j - 1],
                    device_id=(px, py, c), device_id_type=MESH))
        for cp in sends:
            cp.start()
        fwds = []
        for w in range(n):
            for j in (1, 2, 3):
                px, py = _peer_chip(x, y, j)
                got = _piece(outs[w], kinds[w], 2 * px + py, c)
                pltpu.make_async_remote_copy(src_ref=got, dst_ref=got, send_sem=ssem.at[w, j - 1],
                                             recv_sem=rsem.at[w, j - 1], device_id=(px, py, c),
                                             device_id_type=MESH).wait_recv()
                fwd = pltpu.make_async_remote_copy(src_ref=got, dst_ref=got, send_sem=fssem.at[w, j - 1],
                                                   recv_sem=frsem.at[w, j - 1], device_id=sib, device_id_type=MESH)
                fwd.start()
                fwds.append(fwd)
        for w in range(n):
            for j in (1, 2, 3):
                px, py = _peer_chip(x, y, j)
                other = _piece(outs[w], kinds[w], 2 * px + py, 1 - c)
                pltpu.make_async_remote_copy(src_ref=other, dst_ref=other, send_sem=fssem.at[w, j - 1],
                                             recv_sem=frsem.at[w, j - 1], device_id=sib,
                                             device_id_type=MESH).wait_recv()
        for cp in sends + fwds:
            cp.wait_send()

    return pl.pallas_call(
        body, name='all_gather_weights', out_shape=[jax.ShapeDtypeStruct(b.shape, b.dtype) for b in bufs],
        in_specs=[ANY] * n, out_specs=[ANY] * n, input_output_aliases={w: w for w in range(n)},
        scratch_shapes=[pltpu.SemaphoreType.DMA((n, 3)), pltpu.SemaphoreType.DMA((n, 3)),
                        pltpu.SemaphoreType.DMA((n, 3)), pltpu.SemaphoreType.DMA((n, 3))],
    )(*bufs)


def _remote(src, dst, ssem, rsem, k, dev):
    return pltpu.make_async_remote_copy(src_ref=src, dst_ref=dst, send_sem=ssem.at[k], recv_sem=rsem.at[k],
                                        device_id=dev, device_id_type=MESH)


def _phase_gather_ici(bufs, kinds, peers=(1, 2, 3)):
    n, m = len(bufs), len(peers)

    def build(ins, outs, ssem, rsem):
        x, y, c = _place()
        me = 2 * x + y
        ds = []
        for w in range(n):
            for q, j in enumerate(peers):
                px, py = _peer_chip(x, y, j)
                mine = _piece(outs[w], kinds[w], me, c)
                got = _piece(outs[w], kinds[w], 2 * px + py, c)
                k = m * w + q
                ds.append((_remote(mine, mine, ssem, rsem, k, (px, py, c)), _remote(got, got, ssem, rsem, k, (px, py, c))))
        return ds

    return dict(ins=list(bufs), outs=[jax.ShapeDtypeStruct(b.shape, b.dtype) for b in bufs],
                alias={w: w for w in range(n)}, nsem=m * n, build=build)


def _phase_gather_d2d(bufs, kinds):
    n = len(bufs)

    def build(ins, outs, ssem, rsem):
        x, y, c = _place()
        sib = (x, y, 1 - c)
        ds = []
        for w in range(n):
            for j in (1, 2, 3):
                px, py = _peer_chip(x, y, j)
                have = _piece(outs[w], kinds[w], 2 * px + py, c)
                want = _piece(outs[w], kinds[w], 2 * px + py, 1 - c)
                k = 3 * w + j - 1
                ds.append((_remote(have, have, ssem, rsem, k, sib), _remote(want, want, ssem, rsem, k, sib)))
        return ds

    return dict(ins=list(bufs), outs=[jax.ShapeDtypeStruct(b.shape, b.dtype) for b in bufs],
                alias={w: w for w in range(n)}, nsem=3 * n, build=build)


def _phase_pair(grads, kinds, slabs):
    n = len(grads)

    def build(ins, outs, ssem, rsem):
        x, y, c = _place()
        sib = (x, y, 1 - c)
        ds = []
        for w in range(n):
            for k in range(N_CHIPS):
                cp = _remote(_piece(ins[w], kinds[w], k, 1 - c), outs[w].at[k], ssem, rsem, N_CHIPS * w + k, sib)
                ds.append((cp, cp))
        return ds

    return dict(ins=list(grads), alias={}, nsem=N_CHIPS * n, build=build,
                outs=[jax.ShapeDtypeStruct((N_CHIPS,) + _half_shape(k, s), g.dtype)
                      for g, k, s in zip(grads, kinds, slabs)])


def _phase_chip(sums, peers=(1, 2, 3), landing=None):
    n, m = len(sums), len(peers)

    def build(ins, outs, ssem, rsem):
        x, y, c = _place()
        ds = []
        for w in range(n):
            for q, j in enumerate(peers):
                px, py = _peer_chip(x, y, j)
                cp = _remote(ins[w].at[2 * px + py], outs[w].at[j - 1], ssem, rsem, m * w + q, (px, py, c))
                ds.append((cp, cp))
        return ds

    return dict(ins=list(sums) + (list(landing) if landing is not None else []),
                alias={n + w: w for w in range(n)} if landing is not None else {}, nsem=m * n, build=build,
                outs=[jax.ShapeDtypeStruct((3,) + s.shape[1:], s.dtype) for s in sums])


def _phase_final(slabs_half, kinds):
    n = len(slabs_half)

    def build(ins, outs, ssem, rsem):
        x, y, c = _place()
        sib = (x, y, 1 - c)
        ds = []
        for w in range(n):
            mine = _half_of_slab(outs[w], kinds[w], c)
            other = _half_of_slab(outs[w], kinds[w], 1 - c)
            ds.append((_remote(mine, mine, ssem, rsem, w, sib), _remote(other, other, ssem, rsem, w, sib)))
        return ds

    return dict(ins=list(slabs_half), outs=[jax.ShapeDtypeStruct(s.shape, s.dtype) for s in slabs_half],
                alias={w: w for w in range(n)}, nsem=n, build=build)


def _phase_operands(phases, n_main_in, n_main_out):
    ins, outs, alias, sems = [], [], {}, []
    for ph in phases:
        for i, o in ph['alias'].items():
            alias[n_main_in + len(ins) + i] = n_main_out + len(outs) + o
        ins += ph['ins']
        outs += ph['outs']
        sems += [pltpu.SemaphoreType.DMA((ph['nsem'],)), pltpu.SemaphoreType.DMA((ph['nsem'],))]
    return ins, outs, alias, sems


def _phase_copies(phases, in_refs, out_refs, sem_refs):
    ds, a, b = [], 0, 0
    for p, ph in enumerate(phases):
        ds += ph['build'](in_refs[a:a + len(ph['ins'])], out_refs[b:b + len(ph['outs'])], sem_refs[2 * p],
                          sem_refs[2 * p + 1])
        a += len(ph['ins'])
        b += len(ph['outs'])
    return ds


def _phase_start(phases, in_refs, out_refs, sem_refs):
    for send, _ in _phase_copies(phases, in_refs, out_refs, sem_refs):
        send.start()


def _phase_finish(phases, in_refs, out_refs, sem_refs):
    ds = _phase_copies(phases, in_refs, out_refs, sem_refs)
    for _, recv in ds:
        recv.wait_recv()
    for send, _ in ds:
        send.wait_send()


def _phase_results(phases, outs):
    res, b = [], 0
    for ph in phases:
        res.append(list(outs[b:b + len(ph['outs'])]))
        b += len(ph['outs'])
    return res


def _run_phases(phases, name):
    ins, out_shapes, alias, sems = _phase_operands(phases, 0, 0)
    n_in, n_out = len(ins), len(out_shapes)

    def body(*refs):
        in_refs, out_refs, sem_refs = refs[:n_in], refs[n_in:n_in + n_out], refs[n_in + n_out:]
        _phase_start(phases, in_refs, out_refs, sem_refs)
        _phase_finish(phases, in_refs, out_refs, sem_refs)

    outs = pl.pallas_call(body, name=name, out_shape=out_shapes, in_specs=[ANY] * n_in, out_specs=[ANY] * n_out,
                          input_output_aliases=alias, scratch_shapes=sems)(*ins)
    return _phase_results(phases, outs)


def _cast_place(shard, kind, place_arr, dtype, name):
    full = _full_shape(kind, shard.shape)
    if kind == 'pool':
        g, r, w = shard.shape
        grid = (g,)
        src = pl.BlockSpec((None, r, w), lambda i, s: (i, 0, 0))
        dst = pl.BlockSpec((None, r, w), lambda i, s: (i, s[1], 0))
    else:
        r, w = shard.shape
        tr, tc = _tile2(r, w)
        nrb, ncb = r // tr, w // tc
        grid = (nrb, ncb)
        src = pl.BlockSpec((tr, tc), lambda i, j, s: (i, j))
        if kind == 'col':
            dst = pl.BlockSpec((tr, tc), lambda i, j, s: (i, s[1] * ncb + j))
        else:
            dst = pl.BlockSpec((tr, tc), lambda i, j, s: (s[1] * nrb + i, j))

    def body(s_ref, x_ref, o_ref):
        o_ref[...] = x_ref[...].astype(o_ref.dtype)

    return pl.pallas_call(
        body, name=name,
        grid_spec=pltpu.PrefetchScalarGridSpec(num_scalar_prefetch=1, grid=grid, in_specs=[src], out_specs=dst),
        out_shape=jax.ShapeDtypeStruct(full, dtype), compiler_params=_cp(('parallel',) * len(grid)),
    )(place_arr, shard)


def _tile2(r, c):
    tr = _div_tile(r, max(16, (1 << 19) // c), 16)
    if tr == r or tr * c >= (1 << 17):
        return tr, c
    return r, _div_tile(c, max(128, (1 << 19) // r), 128)


def _pair_sum(grad, theirs, kind, slab, c_arr, name):
    hs = _half_shape(kind, slab)
    if kind == 'pool':
        grid = (N_CHIPS, hs[0], 1)
        own = pl.BlockSpec((None, hs[1], hs[2]), lambda k, i, j, s: (i, 2 * k + s[0], 0))
        stk = pl.BlockSpec((None, None, hs[1], hs[2]), lambda k, i, j, s: (k, i, 0, 0))
    else:
        tr, tc = _tile2(*hs)
        nrb, ncb = hs[0] // tr, hs[1] // tc
        grid = (N_CHIPS, nrb, ncb)
        if kind == 'col':
            own = pl.BlockSpec((tr, tc), lambda k, i, j, s: (s[0] * nrb + i, k * ncb + j))
        else:
            own = pl.BlockSpec((tr, tc), lambda k, i, j, s: (k * nrb + i, s[0] * ncb + j))
        stk = pl.BlockSpec((None, tr, tc), lambda k, i, j, s: (k, i, j))

    def body(s_ref, a_ref, b_ref, o_ref):
        o_ref[...] = (a_ref[...].astype(F32) + b_ref[...].astype(F32)).astype(o_ref.dtype)

    return pl.pallas_call(
        body, name=name,
        grid_spec=pltpu.PrefetchScalarGridSpec(num_scalar_prefetch=1, grid=grid, in_specs=[own, stk], out_specs=stk),
        out_shape=jax.ShapeDtypeStruct((N_CHIPS,) + hs, BF), compiler_params=_cp(('parallel',) * 3),
    )(c_arr, grad, theirs)


def _chip_sum(sums, landed, kind, slab, place_arr, name):
    hs = _half_shape(kind, slab)
    if kind == 'pool':
        grid = (hs[0], 1)
        blk = (None, None, hs[1], hs[2])
        mine = pl.BlockSpec(blk, lambda i, j, s: (s[1], i, 0, 0))
        land = [pl.BlockSpec(blk, lambda i, j, s, p=p: (p, i, 0, 0)) for p in range(3)]
        out = pl.BlockSpec((None, hs[1], hs[2]), lambda i, j, s: (i, s[0], 0))
    else:
        tr, tc = _tile2(*hs)
        nrb, ncb = hs[0] // tr, hs[1] // tc
        grid = (nrb, ncb)
        blk = (None, tr, tc)
        mine = pl.BlockSpec(blk, lambda i, j, s: (s[1], i, j))
        land = [pl.BlockSpec(blk, lambda i, j, s, p=p: (p, i, j)) for p in range(3)]
        if kind == 'col':
            out = pl.BlockSpec((tr, tc), lambda i, j, s: (s[0] * nrb + i, j))
        else:
            out = pl.BlockSpec((tr, tc), lambda i, j, s: (i, s[0] * ncb + j))

    def body(s_ref, a_ref, b_ref, c_ref, d_ref, o_ref):
        o_ref[...] = ((a_ref[...].astype(F32) + b_ref[...].astype(F32)) + c_ref[...].astype(F32)) + d_ref[...].astype(F32)

    return pl.pallas_call(
        body, name=name,
        grid_spec=pltpu.PrefetchScalarGridSpec(num_scalar_prefetch=1, grid=grid, in_specs=[mine] + land, out_specs=out),
        out_shape=jax.ShapeDtypeStruct(tuple(slab), F32), compiler_params=_cp(('parallel',) * 2),
    )(place_arr, sums, landed, landed, landed)


def _all_reduce_small(buf):
    rows, cols = buf.shape

    def body(in_ref, out_ref, land, ssem, rsem):
        x, y, c = _place()
        me = 4 * x + 2 * y + c
        land[me] = in_ref[...]
        started = []
        for j in range(1, 8):
            px = 1 - x if (j >> 2) & 1 else x
            py = 1 - y if (j >> 1) & 1 else y
            pc = 1 - c if j & 1 else c
            cp = pltpu.make_async_remote_copy(src_ref=in_ref, dst_ref=land.at[me], send_sem=ssem.at[j - 1],
                                              recv_sem=rsem.at[j - 1], device_id=(px, py, pc), device_id_type=MESH)
            cp.start()
            started.append(cp)
        for j in range(1, 8):
            px = 1 - x if (j >> 2) & 1 else x
            py = 1 - y if (j >> 1) & 1 else y
            pc = 1 - c if j & 1 else c
            slot = land.at[4 * px + 2 * py + pc]
            pltpu.make_async_remote_copy(src_ref=slot, dst_ref=slot, send_sem=ssem.at[j - 1], recv_sem=rsem.at[j - 1],
                                         device_id=(px, py, pc), device_id_type=MESH).wait_recv()
        for cp in started:
            cp.wait_send()
        acc = land[0]
        for d in range(1, 8):
            acc = acc + land[d]
        out_ref[...] = acc

    return pl.pallas_call(
        body, name='all_reduce_small', out_shape=jax.ShapeDtypeStruct((rows, cols), F32),
        in_specs=[pl.BlockSpec(memory_space=pltpu.VMEM)], out_specs=pl.BlockSpec(memory_space=pltpu.VMEM),
        scratch_shapes=[pltpu.VMEM((8, rows, cols), F32), pltpu.SemaphoreType.DMA((7,)), pltpu.SemaphoreType.DMA((7,))],
    )(buf)


def _elementwise(fn, ins, lead_index, out_shape, out_dtypes, name):
    nd = len(out_shape)
    r, cdim = out_shape[-2], out_shape[-1]
    tr, tc = _tile2(r, cdim)
    grid = tuple(out_shape[:-2]) + (r // tr, cdim // tc)
    block = (None,) * (nd - 2) + (tr, tc)

    def spec(lead):
        if lead is None:
            return pl.BlockSpec(block, lambda *g: tuple(g))
        return pl.BlockSpec((None,) + block, lambda *g, lead=lead: (lead,) + tuple(g))

    n_in = len(ins)

    def body(*refs):
        res = fn(*[r_[...] for r_ in refs[:n_in]])
        for o_ref, v in zip(refs[n_in:], res):
            o_ref[...] = v.astype(o_ref.dtype)

    return pl.pallas_call(
        body, name=name, grid=grid, in_specs=[spec(l) for l in lead_index],
        out_specs=[spec(None) for _ in out_dtypes],
        out_shape=[jax.ShapeDtypeStruct(tuple(out_shape), dt) for dt in out_dtypes],
        compiler_params=_cp(('parallel',) * len(grid)),
    )(*ins)


def _adam_fn(w, g, m, v):
    m = ADAM_B1 * m + (1.0 - ADAM_B1) * g
    v = ADAM_B2 * v + (1.0 - ADAM_B2) * (g * g)
    m_hat = m / (1.0 - ADAM_B1 ** ADAM_STEP)
    v_hat = v / (1.0 - ADAM_B2 ** ADAM_STEP)
    delta = -ADAM_LR * (m_hat / (jnp.sqrt(v_hat) + ADAM_EPS) + ADAM_WD * w)
    return g, delta, m, v


def _adam(w, g, m, v, name):
    return _elementwise(_adam_fn, [w, g, m, v], [None] * 4, w.shape, [F32] * 4, name)


_DIMS = {'nn': (((1,), (0,)), ((), ())), 'nt': (((1,), (1,)), ((), ())), 'tn': (((0,), (0,)), ((), ()))}


def _mm(a, b, mode, out_dtype, name, phases=()):
    if mode == 'nn':
        (M, K), N = a.shape, b.shape[1]
    elif mode == 'nt':
        (M, K), N = a.shape, b.shape[0]
    else:
        (K, M), N = a.shape, b.shape[1]
    if mode == 'tn':
        tm, tn, tk = _div_tile(M, 512, 128), _div_tile(N, 1024, 128), _div_tile(K, MM_TK_ROWS, 16)
        if tm < 256:
            tm = _div_tile(M, 1024, 128)
    else:
        tk = _div_tile(K, MM_TK_COLS, 128)
        tm, tn = _div_tile(M, 1088, 16), _div_tile(N, 1024 if tk <= 2816 else 512, 128)
    nk = K // tk
    a_spec = {'nn': pl.BlockSpec((tm, tk), lambda i, j, k: (i, k)), 'nt': pl.BlockSpec((tm, tk), lambda i, j, k: (i, k)),
              'tn': pl.BlockSpec((tk, tm), lambda i, j, k: (k, i))}[mode]
    b_spec = {'nn': pl.BlockSpec((tk, tn), lambda i, j, k: (k, j)), 'nt': pl.BlockSpec((tn, tk), lambda i, j, k: (j, k)),
              'tn': pl.BlockSpec((tk, tn), lambda i, j, k: (k, j))}[mode]
    dims = _DIMS[mode]
    gm, gn = M // tm, N // tn
    extra_in, extra_out, alias, sems = _phase_operands(phases, 2, 1)
    n_ei, n_eo = len(extra_in), len(extra_out)

    def body(*refs):
        a_ref, b_ref, ein = refs[0], refs[1], refs[2:2 + n_ei]
        o_ref, eout = refs[2 + n_ei], refs[3 + n_ei:3 + n_ei + n_eo]
        acc_ref, sem_refs = refs[3 + n_ei + n_eo], refs[4 + n_ei + n_eo:]
        i, j, k = pl.program_id(0), pl.program_id(1), pl.program_id(2)
        if phases:
            @pl.when((i == 0) & (j == 0) & (k == 0))
            def _():
                _phase_start(phases, ein, eout, sem_refs)

        part = lax.dot_general(a_ref[...], b_ref[...], dims, preferred_element_type=F32)
        if nk == 1:
            o_ref[...] = part.astype(o_ref.dtype)
        else:
            @pl.when(k == 0)
            def _():
                acc_ref[...] = part

            @pl.when(k > 0)
            def _():
                acc_ref[...] += part

            @pl.when(k == nk - 1)
            def _():
                o_ref[...] = acc_ref[...].astype(o_ref.dtype)

        if phases:
            @pl.when((i == gm - 1) & (j == gn - 1) & (k == nk - 1))
            def _():
                _phase_finish(phases, ein, eout, sem_refs)

    outs = pl.pallas_call(
        body, name=name, grid=(gm, gn, nk), in_specs=[a_spec, b_spec] + [ANY] * n_ei,
        out_specs=[pl.BlockSpec((tm, tn), lambda i, j, k: (i, j))] + [ANY] * n_eo,
        out_shape=[jax.ShapeDtypeStruct((M, N), out_dtype)] + extra_out, input_output_aliases=alias,
        scratch_shapes=[pltpu.VMEM((tm, tn) if nk > 1 else (8, 128), F32)] + sems,
        compiler_params=_cp(('arbitrary',) * 3 if phases else ('parallel', 'parallel', 'arbitrary')),
    )(a, b, *extra_in)
    if phases:
        return outs[0], _phase_results(phases, outs[1:])
    return outs[0]


def _rb(tm, w, col=0):
    return pl.BlockSpec((tm, w), lambda i, col=col: (i, col))


def _fixed(shape):
    return pl.BlockSpec(shape, lambda i: (0,) * len(shape))


def _rms(x):
    return lax.rsqrt(jnp.mean(x * x, axis=-1, keepdims=True) + EPS)


def _norm_fwd(x, gain, name, width=None, col=0):
    Lp = x.shape[0]
    width = width or x.shape[1]
    tm = _div_tile(Lp, 256, 16)

    def body(x_ref, g_ref, o_ref):
        v = x_ref[...]
        o_ref[...] = (v * _rms(v) * g_ref[...]).astype(o_ref.dtype)

    return pl.pallas_call(
        body, name=name, grid=(Lp // tm,), in_specs=[_rb(tm, width, col), _fixed((1, width))], out_specs=_rb(tm, width),
        out_shape=jax.ShapeDtypeStruct((Lp, width), BF), compiler_params=_cp(('parallel',)),
    )(x, gain)


def _post_residual(h, f, gain, scale, next_gain, name, phases=()):
    Lp, D = h.shape
    tm = _div_tile(Lp, 256, 16)

    def body(h_ref, f_ref, g_ref, ng_ref, o_ref, n_ref):
        v = f_ref[...]
        out = h_ref[...] + scale * (v * _rms(v) * g_ref[...])
        o_ref[...] = out
        n_ref[...] = (out * _rms(out) * ng_ref[...]).astype(n_ref.dtype)

    return _call_carrying(
        body, name, (Lp // tm,), [_rb(tm, D), _rb(tm, D), _fixed((1, D)), _fixed((1, D))], [_rb(tm, D), _rb(tm, D)],
        [jax.ShapeDtypeStruct((Lp, D), F32), jax.ShapeDtypeStruct((Lp, D), BF)], [], (h, f, gain, next_gain), phases)


def _pre_bwd(dres, dn, h, gain, name, phases=(), before=None):
    Lp, D = h.shape
    tm = _div_tile(Lp, 256, 16)

    def accumulate(ref, part):
        @pl.when(pl.program_id(0) == 0)
        def _():
            ref[...] = part

        @pl.when(pl.program_id(0) > 0)
        def _():
            ref[...] += part

    def body(dres_ref, dn_ref, h_ref, g_ref, *rest):
        v = h_ref[...]
        r = _rms(v)
        dy = dn_ref[...]
        w = dy * g_ref[...]
        dh = dres_ref[...] + r * w - v * (r * r * r) * jnp.mean(w * v, axis=-1, keepdims=True)
        if before is None:
            dh_ref, dg_ref = rest
        else:
            f_ref, pg_ref, dh_ref, dg_ref, df_ref, dpg_ref = rest
            fv = f_ref[...]
            fr = _rms(fv)
            fdy = before[2] * dh
            fw = fdy * pg_ref[...]
            df_ref[...] = (fr * fw - fv * (fr * fr * fr) * jnp.mean(fw * fv, axis=-1, keepdims=True)).astype(BF)
            accumulate(dpg_ref, jnp.sum(fdy * fv * fr, axis=0, keepdims=True))
        dh_ref[...] = dh
        accumulate(dg_ref, jnp.sum(dy * v * r, axis=0, keepdims=True))

    in_specs = [_rb(tm, D), _rb(tm, D), _rb(tm, D), _fixed((1, D))]
    out_specs = [_rb(tm, D), _fixed((1, D))]
    out_shape = [jax.ShapeDtypeStruct((Lp, D), F32), jax.ShapeDtypeStruct((1, D), F32)]
    args = (dres, dn, h, gain)
    if before is not None:
        in_specs += [_rb(tm, D), _fixed((1, D))]
        out_specs += [_rb(tm, D), _fixed((1, D))]
        out_shape += [jax.ShapeDtypeStruct((Lp, D), BF), jax.ShapeDtypeStruct((1, D), F32)]
        args += (before[0], before[1])
    return _call_carrying(body, name, (Lp // tm,), in_specs, out_specs, out_shape, [], args, phases)


def _swiglu_fwd(gu, name, phases=()):
    Lp, F2 = gu.shape
    F = F2 // 2
    tm = _div_tile(Lp, 256, 16)

    def body(gu_ref, a_ref):
        g = gu_ref[:, :F].astype(F32)
        u = gu_ref[:, F:].astype(F32)
        a_ref[...] = (g * jax.nn.sigmoid(g) * u).astype(a_ref.dtype)

    return _call_carrying(body, name, (Lp // tm,), [_rb(tm, F2)], [_rb(tm, F)],
                          [jax.ShapeDtypeStruct((Lp, F), BF)], [], (gu,), phases)


def _swiglu_bwd(da, gu, name):
    Lp, F2 = gu.shape
    F = F2 // 2
    tm = _div_tile(Lp, 256, 16)

    def body(da_ref, gu_ref, o_ref):
        g = gu_ref[:, :F].astype(F32)
        u = gu_ref[:, F:].astype(F32)
        da_ = da_ref[...].astype(F32)
        s = jax.nn.sigmoid(g)
        o_ref[:, :F] = (da_ * u * (s * (1.0 + g * (1.0 - s)))).astype(o_ref.dtype)
        o_ref[:, F:] = (da_ * (g * s)).astype(o_ref.dtype)

    return pl.pallas_call(
        body, name=name, grid=(Lp // tm,), in_specs=[_rb(tm, F), _rb(tm, F2)], out_specs=_rb(tm, F2),
        out_shape=jax.ShapeDtypeStruct((Lp, F2), BF), compiler_params=_cp(('parallel',)),
    )(da, gu)


def _tail(h, f, gain, scale, tgt, n_meta, n_real, name):
    Lp, D = h.shape
    tm = _div_tile(Lp, 256, 16)

    def body(h_ref, f_ref, g_ref, t_ref, d_ref, df_ref, dg_ref, l_ref):
        v = f_ref[...]
        r = _rms(v)
        gain_ = g_ref[...]
        row = lax.broadcasted_iota(jnp.int32, (tm, 1), 0) + pl.program_id(0) * tm
        ok = (row >= n_meta) & (row < n_meta + n_real)
        err = jnp.where(ok, h_ref[...] + scale * (v * r * gain_) - t_ref[...], 0.0)
        d_out = err / D
        d_ref[...] = d_out
        dy = scale * d_out
        w = dy * gain_
        df_ref[...] = (r * w - v * (r * r * r) * jnp.mean(w * v, axis=-1, keepdims=True)).astype(df_ref.dtype)
        dg_part = jnp.sum(dy * v * r, axis=0, keepdims=True)
        l_part = jnp.full((1, 128), jnp.sum(err * err), F32)

        @pl.when(pl.program_id(0) == 0)
        def _():
            dg_ref[...] = dg_part
            l_ref[...] = l_part

        @pl.when(pl.program_id(0) > 0)
        def _():
            dg_ref[...] += dg_part
            l_ref[...] += l_part

    return pl.pallas_call(
        body, name=name, grid=(Lp // tm,), in_specs=[_rb(tm, D), _rb(tm, D), _fixed((1, D)), _rb(tm, D)],
        out_specs=[_rb(tm, D), _rb(tm, D), _fixed((1, D)), _fixed((1, 128))],
        out_shape=[jax.ShapeDtypeStruct((Lp, D), F32), jax.ShapeDtypeStruct((Lp, D), BF),
                   jax.ShapeDtypeStruct((1, D), F32), jax.ShapeDtypeStruct((1, 128), F32)],
        compiler_params=_cp(('arbitrary',)),
    )(h, f, gain, tgt)


def _split_bf16(v):
    hi = v.astype(BF)
    return hi, (v - hi.astype(F32)).astype(BF)


def _pool_fwd(z, name):
    Lp = z.shape[0]
    T = _div_tile(Lp, 256, 16)
    G = len(POOL_WINDOWS)

    def body(cur_ref, prev_ref, d_ref):
        i, g = pl.program_id(0), pl.program_id(1)
        w = jnp.left_shift(2, g)
        rr = lax.broadcasted_iota(jnp.int32, (T, T), 0)
        cc = lax.broadcasted_iota(jnp.int32, (T, T), 1)
        b_cur = jnp.where((cc <= rr) & (cc > rr - w), 1.0, 0.0).astype(BF)
        w_prev = jnp.where(i > 0, w, 0)
        b_prev = jnp.where(cc - T > rr - w_prev, 1.0, 0.0).astype(BF)
        u = cur_ref[...]
        s = jnp.zeros((T, POOL_GROUP), F32)
        for part in _split_bf16(u):
            s += jnp.dot(b_cur, part, preferred_element_type=F32)
        for part in _split_bf16(prev_ref[...]):
            s += jnp.dot(b_prev, part, preferred_element_type=F32)
        t = lax.broadcasted_iota(jnp.int32, (T, 1), 0) + i * T
        cnt = jnp.minimum(w, t + 1).astype(F32)
        d_ref[...] = (s / cnt - u).astype(d_ref.dtype)

    return pl.pallas_call(
        body, name=name, grid=(Lp // T, G),
        in_specs=[pl.BlockSpec((T, POOL_GROUP), lambda i, g: (i, g)),
                  pl.BlockSpec((T, POOL_GROUP), lambda i, g: (jnp.maximum(i - 1, 0), g))],
        out_specs=pl.BlockSpec((T, POOL_GROUP), lambda i, g: (i, g)),
        out_shape=jax.ShapeDtypeStruct((Lp, POOL_WIDTH), BF), compiler_params=_cp(('parallel', 'parallel')),
    )(z, z)


def _pool_bwd(dd, name):
    Lp = dd.shape[0]
    T = _div_tile(Lp, 256, 16)
    G = len(POOL_WINDOWS)
    n_t = Lp // T

    def body(cur_ref, next_ref, o_ref):
        i, g = pl.program_id(0), pl.program_id(1)
        w = jnp.left_shift(2, g)
        rr = lax.broadcasted_iota(jnp.int32, (T, T), 0)
        cc = lax.broadcasted_iota(jnp.int32, (T, T), 1)
        b_cur = jnp.where((cc >= rr) & (cc < rr + w), 1.0, 0.0).astype(BF)
        w_next = jnp.where(i < n_t - 1, w, 0)
        b_next = jnp.where(cc + T < rr + w_next, 1.0, 0.0).astype(BF)
        t = lax.broadcasted_iota(jnp.int32, (T, 1), 0) + i * T
        cur = cur_ref[...]
        e_cur = cur / jnp.minimum(w, t + 1).astype(F32)
        e_next = next_ref[...] / jnp.minimum(w, t + T + 1).astype(F32)
        s = jnp.zeros((T, POOL_GROUP), F32)
        for part in _split_bf16(e_cur):
            s += jnp.dot(b_cur, part, preferred_element_type=F32)
        for part in _split_bf16(e_next):
            s += jnp.dot(b_next, part, preferred_element_type=F32)
        o_ref[...] = (s - cur).astype(o_ref.dtype)

    return pl.pallas_call(
        body, name=name, grid=(n_t, G),
        in_specs=[pl.BlockSpec((T, POOL_GROUP), lambda i, g: (i, g)),
                  pl.BlockSpec((T, POOL_GROUP), lambda i, g: (jnp.minimum(i + 1, n_t - 1), g))],
        out_specs=pl.BlockSpec((T, POOL_GROUP), lambda i, g: (i, g)),
        out_shape=jax.ShapeDtypeStruct((Lp, POOL_WIDTH), BF), compiler_params=_cp(('parallel', 'parallel')),
    )(dd, dd)


def _pool_mix_fwd(d, pool_w, scale, name):
    Lp = d.shape[0]
    G = len(POOL_WINDOWS)
    tm = _div_tile(Lp, 1088, 16)

    def body(d_ref, w_ref, s_ref, e_ref, y_ref):
        e = jnp.dot(d_ref[...], w_ref[...], preferred_element_type=F32)
        e_ref[...] = e.astype(e_ref.dtype)
        y_ref[...] = (e * s_ref[...]).astype(y_ref.dtype)

    blk = pl.BlockSpec((tm, POOL_GROUP), lambda g, i: (i, g))
    return pl.pallas_call(
        body, name=name, grid=(G, Lp // tm),
        in_specs=[blk, pl.BlockSpec((None, POOL_GROUP, POOL_GROUP), lambda g, i: (g, 0, 0)),
                  pl.BlockSpec((1, POOL_GROUP), lambda g, i: (0, g))],
        out_specs=[blk, blk], out_shape=[jax.ShapeDtypeStruct((Lp, POOL_WIDTH), BF)] * 2,
        compiler_params=_cp(('parallel', 'parallel')),
    )(d, pool_w, scale)


def _pool_mix_bwd(dyp, e, d, pool_w, scale, name):
    Lp = d.shape[0]
    G = len(POOL_WINDOWS)
    tm = _div_tile(Lp, 1088, 16)

    def body(dy_ref, e_ref, d_ref, w_ref, s_ref, dd_ref, ds_ref, dw_ref):
        i = pl.program_id(1)
        dy = dy_ref[...]
        de = (dy * s_ref[...]).astype(BF)
        dd_ref[...] = lax.dot_general(de, w_ref[...], _DIMS['nt'], preferred_element_type=F32)
        ds_part = jnp.sum(dy * e_ref[...].astype(F32), axis=0, keepdims=True)
        dw_part = lax.dot_general(d_ref[...], de, _DIMS['tn'], preferred_element_type=F32)

        @pl.when(i == 0)
        def _():
            ds_ref[...] = ds_part
            dw_ref[...] = dw_part

        @pl.when(i > 0)
        def _():
            ds_ref[...] += ds_part
            dw_ref[...] += dw_part

    blk = pl.BlockSpec((tm, POOL_GROUP), lambda g, i: (i, g))
    wblk = pl.BlockSpec((None, POOL_GROUP, POOL_GROUP), lambda g, i: (g, 0, 0))
    sblk = pl.BlockSpec((1, POOL_GROUP), lambda g, i: (0, g))
    return pl.pallas_call(
        body, name=name, grid=(G, Lp // tm), in_specs=[blk, blk, blk, wblk, sblk], out_specs=[blk, sblk, wblk],
        out_shape=[jax.ShapeDtypeStruct((Lp, POOL_WIDTH), F32), jax.ShapeDtypeStruct((1, POOL_WIDTH), F32),
                   jax.ShapeDtypeStruct((G, POOL_GROUP, POOL_GROUP), F32)],
        compiler_params=_cp(('parallel', 'arbitrary')),
    )(dyp, e, d, pool_w, scale)


def _rot_half(t):
    lane = lax.broadcasted_iota(jnp.int32, t.shape, 1)
    half = QK_ROPE // 2
    return jnp.where(lane < half, -pltpu.roll(t, 128 - half, 1), pltpu.roll(t, half, 1))


def _lora_norms(z, q_gain, kv_gain, lay, name):
    Lp = z.shape[0]
    QL, KVL = lay['QL'], lay['KVL']
    tm = _div_tile(Lp, 256, 16)

    def body(q_ref, kv_ref, qg_ref, kg_ref, qo_ref, ko_ref):
        a = q_ref[...]
        qo_ref[...] = (a * _rms(a) * qg_ref[...]).astype(BF)
        b = kv_ref[...]
        ko_ref[...] = (b * _rms(b) * kg_ref[...]).astype(BF)

    return pl.pallas_call(
        body, name=name, grid=(Lp // tm,),
        in_specs=[_rb(tm, QL, lay['cq'] // QL), _rb(tm, KVL, lay['ckv'] // KVL), _fixed((1, QL)), _fixed((1, KVL))],
        out_specs=[_rb(tm, QL), _rb(tm, KVL)],
        out_shape=[jax.ShapeDtypeStruct((Lp, QL), BF), jax.ShapeDtypeStruct((Lp, KVL), BF)],
        compiler_params=_cp(('parallel',)),
    )(z, z, q_gain, kv_gain)


def _lora_norms_bwd(dqn, dkn, z, q_gain, kv_gain, lay, name):
    Lp = z.shape[0]
    QL, KVL = lay['QL'], lay['KVL']
    tm = _div_tile(Lp, 256, 16)

    def one(dy, v, gain):
        r = _rms(v)
        w = dy * gain
        return r * w - v * (r * r * r) * jnp.mean(w * v, axis=-1, keepdims=True), jnp.sum(dy * v * r, axis=0, keepdims=True)

    def body(dq_ref, dk_ref, q_ref, kv_ref, qg_ref, kg_ref, o_ref, dqg_ref, dkg_ref):
        da, ga = one(dq_ref[...], q_ref[...], qg_ref[...])
        db, gb = one(dk_ref[...], kv_ref[...], kg_ref[...])
        o_ref[:, :QL] = da.astype(BF)
        o_ref[:, QL:] = db.astype(BF)

        @pl.when(pl.program_id(0) == 0)
        def _():
            dqg_ref[...] = ga
            dkg_ref[...] = gb

        @pl.when(pl.program_id(0) > 0)
        def _():
            dqg_ref[...] += ga
            dkg_ref[...] += gb

    return pl.pallas_call(
        body, name=name, grid=(Lp // tm,),
        in_specs=[_rb(tm, QL), _rb(tm, KVL), _rb(tm, QL, lay['cq'] // QL), _rb(tm, KVL, lay['ckv'] // KVL),
                  _fixed((1, QL)), _fixed((1, KVL))],
        out_specs=[_rb(tm, QL + KVL), _fixed((1, QL)), _fixed((1, KVL))],
        out_shape=[jax.ShapeDtypeStruct((Lp, QL + KVL), BF), jax.ShapeDtypeStruct((1, QL), F32),
                   jax.ShapeDtypeStruct((1, KVL), F32)],
        compiler_params=_cp(('arbitrary',)),
    )(dqn, dkn, z, z, q_gain, kv_gain)


def _qk_prep(q_raw, kv, z, cos, sin, lay, H, name):
    Lp = q_raw.shape[0]
    W = H * HEAD_PAD
    tm = _div_tile(Lp, 256, 16)

    def body(q_ref, kv_ref, kr_ref, c_ref, s_ref, qo_ref, ko_ref):
        c, s = c_ref[...], s_ref[...]

        def rope(t):
            return t * c + _rot_half(t) * s

        kpe = rope(kr_ref[...]).astype(BF)
        for h in range(H):
            b = h * HEAD_PAD
            qo_ref[:, b:b + 128] = (q_ref[:, b:b + 128] * SOFTMAX_SCALE).astype(BF)
            qo_ref[:, b + 128:b + 256] = (rope(q_ref[:, b + 128:b + 256]) * SOFTMAX_SCALE).astype(BF)
            ko_ref[:, b:b + 128] = kv_ref[:, b:b + 128]
            ko_ref[:, b + 128:b + 256] = kpe

    return pl.pallas_call(
        body, name=name, grid=(Lp // tm,),
        in_specs=[_rb(tm, W), _rb(tm, W), _rb(tm, 128, lay['kr'] // 128), _rb(tm, 128), _rb(tm, 128)],
        out_specs=[_rb(tm, W), _rb(tm, W)], out_shape=[jax.ShapeDtypeStruct((Lp, W), BF)] * 2,
        compiler_params=_cp(('parallel',)),
    )(q_raw, kv, z, cos, sin)


def _qk_prep_bwd(dQ, dK, dV, cos, sin, H, name):
    Lp = dQ.shape[0]
    W = H * HEAD_PAD
    tm = _div_tile(Lp, 256, 16)

    def body(dq_ref, dk_ref, dv_ref, c_ref, s_ref, qo_ref, kvo_ref, kro_ref):
        c, s = c_ref[...], s_ref[...]

        def unrope(t):
            return t * c - _rot_half(t * s)

        acc = jnp.zeros((tm, 128), F32)
        for h in range(H):
            b = h * HEAD_PAD
            qo_ref[:, b:b + 128] = (dq_ref[:, b:b + 128] * SOFTMAX_SCALE).astype(BF)
            qo_ref[:, b + 128:b + 256] = (unrope(dq_ref[:, b + 128:b + 256]) * SOFTMAX_SCALE).astype(BF)
            kvo_ref[:, b:b + 128] = dk_ref[:, b:b + 128].astype(BF)
            kvo_ref[:, b + 128:b + 256] = dv_ref[:, h * V_DIM:(h + 1) * V_DIM]
            acc += dk_ref[:, b + 128:b + 256]
        kro_ref[...] = unrope(acc).astype(BF)

    return pl.pallas_call(
        body, name=name, grid=(Lp // tm,),
        in_specs=[_rb(tm, W), _rb(tm, W), _rb(tm, H * V_DIM), _rb(tm, 128), _rb(tm, 128)],
        out_specs=[_rb(tm, W), _rb(tm, W), _rb(tm, 128)],
        out_shape=[jax.ShapeDtypeStruct((Lp, W), BF), jax.ShapeDtypeStruct((Lp, W), BF),
                   jax.ShapeDtypeStruct((Lp, 128), BF)],
        compiler_params=_cp(('parallel',)),
    )(dQ, dK, dV, cos, sin)


def _call_carrying(core, name, grid, in_specs, out_specs, out_shape, scratch, args, phases):
    n_in, n_out, n_scr = len(in_specs), len(out_specs), len(scratch)
    extra_in, extra_out, alias, sems = _phase_operands(phases, n_in, n_out)
    n_ei, n_eo = len(extra_in), len(extra_out)

    def body(*refs):
        ins, ein = refs[:n_in], refs[n_in:n_in + n_ei]
        outs = refs[n_in + n_ei:n_in + n_ei + n_out]
        eout = refs[n_in + n_ei + n_out:n_in + n_ei + n_out + n_eo]
        rest = refs[n_in + n_ei + n_out + n_eo:]
        scr, sem_refs = rest[:n_scr], rest[n_scr:]
        if phases:
            ids = [pl.program_id(d) for d in range(len(grid))]
            first, last = ids[0] == 0, ids[0] == grid[0] - 1
            for d in range(1, len(grid)):
                first, last = first & (ids[d] == 0), last & (ids[d] == grid[d] - 1)

            @pl.when(first)
            def _():
                _phase_start(phases, ein, eout, sem_refs)

        core(*ins, *outs, *scr)
        if phases:
            @pl.when(last)
            def _():
                _phase_finish(phases, ein, eout, sem_refs)

    outs = pl.pallas_call(
        body, name=name, grid=grid, in_specs=list(in_specs) + [ANY] * n_ei, out_specs=list(out_specs) + [ANY] * n_eo,
        out_shape=list(out_shape) + extra_out, input_output_aliases=alias, scratch_shapes=list(scratch) + sems,
        compiler_params=_cp(('arbitrary',) * len(grid)),
    )(*args, *extra_in)
    return list(outs[:n_out]), _phase_results(phases, outs[n_out:])


def _flash_fwd(Q, K, kv, H, name, phases=()):
    Lp = Q.shape[0]
    T = _div_tile(Lp, 256, 16)

    n_t = Lp // T
    FC = min(FLASH_FWD_CHUNK, n_t)
    CH = FC * T

    def body(q_ref, k_ref, v_ref, o_ref, lse_ref, m_s, l_s, acc_s):
        i = pl.program_id(1)
        q = q_ref[...]
        m_s[...] = jnp.full((T, 1), NEG, F32)
        l_s[...] = jnp.zeros((T, 1), F32)
        acc_s[...] = jnp.zeros((T, V_DIM), F32)

        def step(start, width, masked):
            rows = pl.ds(start, width)
            s = lax.dot_general(q, k_ref[rows, :], _DIMS['nt'], preferred_element_type=F32)
            if masked:
                rr = lax.broadcasted_iota(jnp.int32, (T, width), 0) + i * T
                cc = lax.broadcasted_iota(jnp.int32, (T, width), 1) + start
                s = jnp.where(cc <= rr, s, NEG)
            m = m_s[...]
            m_new = jnp.maximum(m, jnp.max(s, axis=-1, keepdims=True))
            alpha = jnp.exp(m - m_new)
            p = jnp.exp(s - m_new)
            l_s[...] = alpha * l_s[...] + jnp.sum(p, axis=-1, keepdims=True)
            acc_s[...] = alpha * acc_s[...] + jnp.dot(p.astype(BF), v_ref[rows, :], preferred_element_type=F32)
            m_s[...] = m_new

        n_full = i // FC

        def full(cidx, carry):
            step(pl.multiple_of(cidx * CH, CH), CH, False)
            return carry

        if n_t > FC:
            lax.fori_loop(0, n_full, full, 0)
        for nb in range(1, FC + 1):
            @pl.when(i % FC == nb - 1)
            def _(nb=nb):
                step(pl.multiple_of(n_full * CH, CH), nb * T, True)

        l = l_s[...]
        o_ref[...] = (acc_s[...] / l).astype(o_ref.dtype)
        lse_ref[...] = jnp.broadcast_to(m_s[...] + jnp.log(l), (T, 128))

    return _call_carrying(
        body, name, (H, Lp // T),
        [pl.BlockSpec((T, HEAD_PAD), lambda h, i: (i, h)), pl.BlockSpec((Lp, HEAD_PAD), lambda h, i: (0, h)),
         pl.BlockSpec((Lp, V_DIM), lambda h, i: (0, 2 * h + 1))],
        [pl.BlockSpec((T, V_DIM), lambda h, i: (i, h)), pl.BlockSpec((None, T, 128), lambda h, i: (h, i, 0))],
        [jax.ShapeDtypeStruct((Lp, H * V_DIM), BF), jax.ShapeDtypeStruct((H, Lp, 128), F32)],
        [pltpu.VMEM((T, 1), F32), pltpu.VMEM((T, 1), F32), pltpu.VMEM((T, V_DIM), F32)], (Q, K, kv), phases)


def _flash_bwd(Q, K, kv, O, dO, lse, H, name, phases=()):
    Lp = Q.shape[0]
    T = _div_tile(Lp, 256, 16)
    n_t = Lp // T
    FC = min(FLASH_BWD_CHUNK, n_t)

    def body(q_ref, k_ref, v_ref, o_ref, do_ref, lse_ref, dq_ref, dk_ref, dv_ref, dk_acc, dv_acc):
        j = pl.program_id(1)

        @pl.when(j == 0)
        def _():
            dq_ref[...] = jnp.zeros_like(dq_ref)

        kj, vj = k_ref[...], v_ref[...]
        dk_acc[...] = jnp.zeros_like(dk_acc)
        dv_acc[...] = jnp.zeros_like(dv_acc)

        def step(start, width, masked):
            rows = pl.ds(start, width)
            qi, doi = q_ref[rows, :], do_ref[rows, :]
            delta = jnp.sum(doi.astype(F32) * o_ref[rows, :].astype(F32), axis=-1, keepdims=True)
            s = lax.dot_general(qi, kj, _DIMS['nt'], preferred_element_type=F32)
            p = jnp.exp(s - lse_ref[rows, :][:, :1])
            if masked:
                rr = lax.broadcasted_iota(jnp.int32, (width, T), 0) + start
                cc = lax.broadcasted_iota(jnp.int32, (width, T), 1) + j * T
                p = jnp.where(cc <= rr, p, 0.0)
            dp = lax.dot_general(doi, vj, _DIMS['nt'], preferred_element_type=F32)
            ds = (p * (dp - delta)).astype(BF)
            dv_acc[...] += lax.dot_general(p.astype(BF), doi, _DIMS['tn'], preferred_element_type=F32)
            dk_acc[...] += lax.dot_general(ds, qi, _DIMS['tn'], preferred_element_type=F32)
            dq_ref[rows, :] += jnp.dot(ds, kj, preferred_element_type=F32)

        head = (n_t - 1 - j) % FC + 1
        for nb in range(1, FC + 1):
            @pl.when(head == nb)
            def _(nb=nb):
                step(pl.multiple_of(j * T, T), nb * T, True)

        def full(cidx, carry):
            step(pl.multiple_of((j + head + cidx * FC) * T, T), FC * T, False)
            return carry

        if n_t > FC:
            lax.fori_loop(0, (n_t - j - head) // FC, full, 0)
        dk_ref[...] = dk_acc[...]
        dv_ref[...] = dv_acc[...].astype(dv_ref.dtype)

    head_q = pl.BlockSpec((Lp, HEAD_PAD), lambda h, j: (0, h))
    head_v = pl.BlockSpec((Lp, V_DIM), lambda h, j: (0, h))
    return _call_carrying(
        body, name, (H, n_t),
        [head_q, pl.BlockSpec((T, HEAD_PAD), lambda h, j: (j, h)), pl.BlockSpec((T, V_DIM), lambda h, j: (j, 2 * h + 1)),
         head_v, head_v, pl.BlockSpec((None, Lp, 128), lambda h, j: (h, 0, 0))],
        [head_q, pl.BlockSpec((T, HEAD_PAD), lambda h, j: (j, h)), pl.BlockSpec((T, V_DIM), lambda h, j: (j, h))],
        [jax.ShapeDtypeStruct((Lp, H * HEAD_PAD), F32), jax.ShapeDtypeStruct((Lp, H * HEAD_PAD), F32),
         jax.ShapeDtypeStruct((Lp, H * V_DIM), BF)],
        [pltpu.VMEM((T, HEAD_PAD), F32), pltpu.VMEM((T, V_DIM), F32)], (Q, K, kv, O, dO, lse), phases)


def _gate_fwd(z, y_pool, y_mla, lay, name):
    Lp, D = y_pool.shape
    tm = _div_tile(Lp, 256, 16)

    def body(gp_ref, gm_ref, yp_ref, ym_ref, o_ref):
        o_ref[...] = (jax.nn.sigmoid(gp_ref[...]) * yp_ref[...] + jax.nn.sigmoid(gm_ref[...]) * ym_ref[...]).astype(BF)

    return pl.pallas_call(
        body, name=name, grid=(Lp // tm,),
        in_specs=[_rb(tm, D, lay['gp'] // D), _rb(tm, D, lay['gm'] // D), _rb(tm, D), _rb(tm, D)], out_specs=_rb(tm, D),
        out_shape=jax.ShapeDtypeStruct((Lp, D), BF), compiler_params=_cp(('parallel',)),
    )(z, z, y_pool, y_mla)


def _gate_bwd(dy, z, y_pool, y_mla, lay, name):
    Lp, D = y_pool.shape
    tm = _div_tile(Lp, 256, 16)

    def body(dy_ref, gp_ref, gm_ref, yp_ref, ym_ref, dp_ref, dm_ref, dg_ref):
        dy_ = dy_ref[...]
        sp, sm = jax.nn.sigmoid(gp_ref[...]), jax.nn.sigmoid(gm_ref[...])
        dp_ref[...] = (dy_ * sp).astype(BF)
        dm_ref[...] = (dy_ * sm).astype(BF)
        dg_ref[:, :D] = (dy_ * yp_ref[...] * (sp * (1.0 - sp))).astype(BF)
        dg_ref[:, D:] = (dy_ * ym_ref[...] * (sm * (1.0 - sm))).astype(BF)

    return pl.pallas_call(
        body, name=name, grid=(Lp // tm,),
        in_specs=[_rb(tm, D), _rb(tm, D, lay['gp'] // D), _rb(tm, D, lay['gm'] // D), _rb(tm, D), _rb(tm, D)],
        out_specs=[_rb(tm, D), _rb(tm, D), _rb(tm, 2 * D)],
        out_shape=[jax.ShapeDtypeStruct((Lp, D), BF), jax.ShapeDtypeStruct((Lp, D), BF),
                   jax.ShapeDtypeStruct((Lp, 2 * D), BF)],
        compiler_params=_cp(('parallel',)),
    )(dy, z, z, y_pool, y_mla)


def _z_layout(D, QL, KVL):
    cq = POOL_WIDTH
    ckv = cq + QL
    gp = -(-(ckv + KVL) // D) * D
    gm = gp + D
    kr = gm + D
    return dict(QL=QL, KVL=KVL, cq=cq, ckv=ckv, gp=gp, gm=gm, kr=kr, width=kr + 128)


def _w_in_aligned(wt, lay, D):
    n0 = POOL_WIDTH + lay['QL'] + lay['KVL']
    parts = [wt[:n0], jnp.zeros((lay['gp'] - n0, D), wt.dtype), wt[n0 + QK_ROPE:], wt[n0:n0 + QK_ROPE],
             jnp.zeros((128 - QK_ROPE, D), wt.dtype)]
    return jnp.concatenate(parts, axis=0)


def _w_in_logical(wt_al, lay, D):
    n0 = POOL_WIDTH + lay['QL'] + lay['KVL']
    return jnp.concatenate([wt_al[:n0], wt_al[lay['kr']:lay['kr'] + QK_ROPE], wt_al[lay['gp']:lay['gp'] + 2 * D]],
                           axis=0)


def kernel(x, meta_tokens, norm_ffn1_pre, norm_ffn1_post, ffn1_w_gu, ffn1_w_down, norm_mix_pre, norm_mix_post, w_in, pool_w, pool_scale, w_pool_o, q_a_norm, w_q_b, kv_a_norm, w_kv_b, w_mla_o, w_out, norm_ffn2_pre, norm_ffn2_post, ffn2_w_gu, ffn2_w_down, loss_target, m_meta_tokens, m_norm_ffn1_pre, m_norm_ffn1_post, m_ffn1_w_gu, m_ffn1_w_down, m_norm_mix_pre, m_norm_mix_post, m_w_in, m_pool_w, m_pool_scale, m_w_pool_o, m_q_a_norm, m_w_q_b, m_kv_a_norm, m_w_kv_b, m_w_mla_o, m_w_out, m_norm_ffn2_pre, m_norm_ffn2_post, m_ffn2_w_gu, m_ffn2_w_down, v_meta_tokens, v_norm_ffn1_pre, v_norm_ffn1_post, v_ffn1_w_gu, v_ffn1_w_down, v_norm_mix_pre, v_norm_mix_post, v_w_in, v_pool_w, v_pool_scale, v_w_pool_o, v_q_a_norm, v_w_q_b, v_kv_a_norm, v_w_kv_b, v_w_mla_o, v_w_out, v_norm_ffn2_pre, v_norm_ffn2_post, v_ffn2_w_gu, v_ffn2_w_down):
    given = dict(locals())
    W = {n: given[n] for n in WEIGHTS}
    M = {n: given['m_' + n] for n in WEIGHTS}
    V = {n: given['v_' + n] for n in WEIGHTS}

    S, D = x.shape[1], x.shape[2]
    NM = meta_tokens.shape[0]
    L = NM + S
    Lp = -(-L // ROW_ALIGN) * ROW_ALIGN
    QL, KVL = w_q_b.shape[1], w_kv_b.shape[1]
    H = w_q_b.shape[2] * N_CHIPS // QK_DIM
    lay = _z_layout(D, QL, KVL)
    gx, gy = lax.axis_index('x'), lax.axis_index('y')
    chip = 2 * gx + gy

    names = list(BIG)
    for d in (W, M, V):
        d['w_in'] = jnp.swapaxes(d['w_in'], 1, 2)
    slab = {n: W[n][0] for n in names}
    kind = dict(BIG, meta_tokens='col')
    slab_shape = {n: tuple(slab[n].shape) for n in names}
    gc = lax.axis_index('c')
    c_arr = jnp.stack([gc]).astype(jnp.int32)
    place_arr = jnp.stack([gc, chip]).astype(jnp.int32)

    full = {n: _cast_place(slab[n], BIG[n], place_arr, BF, f'place_{n}') for n in names}
    full['meta_tokens'] = _cast_place(meta_tokens, 'col', place_arr, F32, 'place_meta_tokens')
    G0 = ['ffn1_w_gu', 'meta_tokens']
    G0B = ['ffn1_w_down']
    G1W = ['w_in']
    G1R = ['pool_w', 'w_pool_o', 'w_q_b', 'w_kv_b']
    G2 = ['w_mla_o', 'w_out']
    G3 = ['ffn2_w_gu', 'ffn2_w_down']

    def gather(phase_fn, group):
        return phase_fn([full[n] for n in group], [kind[n] for n in group])

    def arrived(group, res):
        full.update(zip(group, res))

    arrived(G0, _all_gather([full[n] for n in G0], [kind[n] for n in G0]))
    meta_full = full['meta_tokens']

    pos = jnp.arange(Lp, dtype=F32)
    inv = ROPE_THETA ** (-jnp.arange(0, QK_ROPE, 2, dtype=F32) / QK_ROPE)
    ang = pos[:, None] * inv[None, :]
    ang = jnp.concatenate([ang, ang], axis=-1)
    cos = jnp.pad(jnp.cos(ang), ((0, 0), (0, 128 - QK_ROPE)), constant_values=1.0)
    sin = jnp.pad(jnp.sin(ang), ((0, 0), (0, 128 - QK_ROPE)))

    h0 = jnp.concatenate([meta_full, x[0], jnp.zeros((Lp - L, D), F32)], axis=0)
    tgt = jnp.pad(loss_target[0], ((NM, Lp - L), (0, 0)))

    n1 = _norm_fwd(h0, norm_ffn1_pre, 'ffn1_norm')
    gu1, (res_b, res) = _mm(n1, full['ffn1_w_gu'], 'nn', BF, 'ffn1_gu',
                            [gather(_phase_gather_ici, G0B), gather(_phase_gather_ici, G1R)])
    arrived(G0B, res_b)
    arrived(G1R, res)
    def gather_ici(group, peers):
        return _phase_gather_ici([full[n] for n in group], [kind[n] for n in group], peers)

    (a1,), (res_b, res_w) = _swiglu_fwd(gu1, 'ffn1_act', [gather(_phase_gather_d2d, G0B), gather_ici(G1W, (1,))])
    arrived(G0B, res_b)
    arrived(G1W, res_w)
    f1, (res, res_w) = _mm(a1, full['ffn1_w_down'], 'nn', F32, 'ffn1_down',
                           [gather(_phase_gather_d2d, G1R), gather_ici(G1W, (2, 3))])
    arrived(G1R, res)
    arrived(G1W, res_w)
    (h1, n2), (res_w,) = _post_residual(h0, f1, norm_ffn1_post, 0.5, norm_mix_pre, 'ffn1_res',
                                        [gather(_phase_gather_d2d, G1W)])
    arrived(G1W, res_w)

    w_in_al = _w_in_aligned(full['w_in'], lay, D)
    w_q_pad = jnp.pad(full['w_q_b'].reshape(QL, H, QK_DIM), ((0, 0), (0, 0), (0, HEAD_PAD - QK_DIM))).reshape(
        QL, H * HEAD_PAD)

    z, (res,) = _mm(n2, w_in_al, 'nt', F32, 'mix_in', [gather(_phase_gather_ici, G2)])
    arrived(G2, res)
    d_pool = _pool_fwd(z, 'pool_fwd')
    e_pool, yp = _pool_mix_fwd(d_pool, full['pool_w'], pool_scale, 'pool_mix')
    y_pool = _mm(yp, full['w_pool_o'], 'nn', F32, 'pool_out')
    cqn, ckvn = _lora_norms(z, q_a_norm, kv_a_norm, lay, 'lora_norms')
    q_raw = _mm(cqn, w_q_pad, 'nn', F32, 'mla_q')
    kv = _mm(ckvn, full['w_kv_b'], 'nn', BF, 'mla_kv')
    Q, K = _qk_prep(q_raw, kv, z, cos, sin, lay, H, 'qk_prep')
    (O, lse), (res2, res3) = _flash_fwd(Q, K, kv, H, 'flash_fwd',
                                        [gather(_phase_gather_d2d, G2), gather(_phase_gather_ici, G3)])
    arrived(G2, res2)
    arrived(G3, res3)
    y_mla, (res,) = _mm(O, full['w_mla_o'], 'nn', F32, 'mla_out', [gather(_phase_gather_d2d, G3)])
    arrived(G3, res)
    y = _gate_fwd(z, y_pool, y_mla, lay, 'gate')
    m_mix = _mm(y, full['w_out'], 'nn', F32, 'mix_out')
    (h2, n3), _ = _post_residual(h1, m_mix, norm_mix_post, 1.0, norm_ffn2_pre, 'mix_res')

    gu2 = _mm(n3, full['ffn2_w_gu'], 'nn', BF, 'ffn2_gu')
    (a2,), _ = _swiglu_fwd(gu2, 'ffn2_act')
    f2 = _mm(a2, full['ffn2_w_down'], 'nn', F32, 'ffn2_down')

    G, theirs, sums, halves, reduced = {}, {}, {}, {}, {}
    RA = ['ffn2_w_down', 'ffn2_w_gu']
    RB1 = ['w_out', 'w_pool_o', 'pool_w', 'w_mla_o']
    RB2 = ['w_q_b', 'w_kv_b']
    RB3 = ['w_in']
    RC1 = ['ffn1_w_down']
    RC2 = ['ffn1_w_gu']

    def pair_phase(group):
        return _phase_pair([G[n] for n in group], [BIG[n] for n in group], [slab_shape[n] for n in group])

    def pair_sums(group, res):
        for n, t in zip(group, res):
            sums[n] = _pair_sum(G[n], t, BIG[n], slab_shape[n], c_arr, f'rs_pair_sum_{n}')

    def chip_phase(group):
        return _phase_chip([sums[n] for n in group])

    def chip_sums(group, res):
        for n, ld in zip(group, res):
            halves[n] = _chip_sum(sums[n], ld, BIG[n], slab_shape[n], place_arr, f'rs_chip_sum_{n}')

    def final_phase(group):
        return _phase_final([halves[n] for n in group], [BIG[n] for n in group])

    dh3, df2, G['norm_ffn2_post'], sq = _tail(h2, f2, norm_ffn2_post, 0.5, tgt, NM, S, 'ffn2_tail')

    G['ffn2_w_down'] = _mm(a2, df2, 'tn', BF, 'ffn2_dw_down')
    da2 = _mm(df2, full['ffn2_w_down'], 'nt', BF, 'ffn2_da')
    dgu2 = _swiglu_bwd(da2, gu2, 'ffn2_act_bwd')
    G['ffn2_w_gu'] = _mm(n3, dgu2, 'tn', BF, 'ffn2_dw_gu')
    dn3, (res,) = _mm(dgu2, full['ffn2_w_gu'], 'nt', F32, 'ffn2_dn', [pair_phase(RA)])
    (dh2, G['norm_ffn2_pre'], dm, G['norm_mix_post']), _ = _pre_bwd(
        dh3, dn3, h2, norm_ffn2_pre, 'ffn2_norm_bwd', before=(m_mix, norm_mix_post, 1.0))
    pair_sums(RA, res)

    G['w_out'] = _mm(y, dm, 'tn', BF, 'dw_out')
    dy = _mm(dm, full['w_out'], 'nt', F32, 'mix_out_bwd')
    dy_pool, dy_mla, d_gate = _gate_bwd(dy, z, y_pool, y_mla, lay, 'gate_bwd')
    G['w_pool_o'] = _mm(yp, dy_pool, 'tn', BF, 'dw_pool_o')
    dyp = _mm(dy_pool, full['w_pool_o'], 'nt', F32, 'pool_out_bwd')
    dd, G['pool_scale'], d_pool_w = _pool_mix_bwd(dyp, e_pool, d_pool, full['pool_w'], pool_scale, 'pool_mix_bwd')
    G['pool_w'] = d_pool_w.astype(BF)
    du_pool = _pool_bwd(dd, 'pool_bwd')
    G['w_mla_o'] = _mm(O, dy_mla, 'tn', BF, 'dw_mla_o')
    dO, (res,) = _mm(dy_mla, full['w_mla_o'], 'nt', BF, 'mla_out_bwd', [pair_phase(RB1)])
    pair_sums(RB1, res)
    (dQ, dK, dV), (res, res_b) = _flash_bwd(Q, K, kv, O, dO, lse, H, 'flash_bwd', [chip_phase(RA), chip_phase(RB1)])
    chip_sums(RA, res)
    chip_sums(RB1, res_b)
    dq_raw, dkv, dkr = _qk_prep_bwd(dQ, dK, dV, cos, sin, H, 'qk_prep_bwd')
    d_w_q_pad, (res_b,) = _mm(cqn, dq_raw, 'tn', BF, 'dw_q_b', [final_phase(RB1)])
    reduced.update(zip(RB1, res_b))
    G['w_q_b'] = d_w_q_pad.reshape(QL, H, HEAD_PAD)[:, :, :QK_DIM].reshape(QL, H * QK_DIM)
    dcqn = _mm(dq_raw, w_q_pad, 'nt', F32, 'mla_q_bwd')
    G['w_kv_b'] = _mm(ckvn, dkv, 'tn', BF, 'dw_kv_b')
    dckvn, (res,) = _mm(dkv, full['w_kv_b'], 'nt', F32, 'mla_kv_bwd', [pair_phase(RB2)])
    pair_sums(RB2, res)
    d_lora, G['q_a_norm'], G['kv_a_norm'] = _lora_norms_bwd(dcqn, dckvn, z, q_a_norm, kv_a_norm, lay, 'lora_norms_bwd')
    n0 = POOL_WIDTH + QL + KVL
    dz = jnp.concatenate([du_pool, d_lora, jnp.zeros((Lp, lay['gp'] - n0), BF), d_gate, dkr], axis=1)
    d_w_in_al, (res, res_b) = _mm(dz, n2, 'tn', BF, 'dw_in', [final_phase(RA), chip_phase(RB2)])
    reduced.update(zip(RA, res))
    chip_sums(RB2, res_b)
    G['w_in'] = _w_in_logical(d_w_in_al, lay, D)
    dn2, (res, res_b) = _mm(dz, w_in_al, 'nn', F32, 'mix_in_bwd', [pair_phase(RB3), final_phase(RB2)])
    reduced.update(zip(RB2, res_b))
    (dh1, G['norm_mix_pre'], df1, G['norm_ffn1_post']), _ = _pre_bwd(
        dh2, dn2, h1, norm_mix_pre, 'mix_norm_bwd', before=(f1, norm_ffn1_post, 0.5))
    pair_sums(RB3, res)

    sums_b3 = [sums[n] for n in RB3]
    G['ffn1_w_down'], (land_b,) = _mm(a1, df1, 'tn', BF, 'ffn1_dw_down', [_phase_chip(sums_b3, (1, 2))])
    da1, (res, land_b) = _mm(df1, full['ffn1_w_down'], 'nt', BF, 'ffn1_da',
                             [pair_phase(RC1), _phase_chip(sums_b3, (3,), land_b)])
    chip_sums(RB3, land_b)
    dgu1 = _swiglu_bwd(da1, gu1, 'ffn1_act_bwd')
    pair_sums(RC1, res)
    G['ffn1_w_gu'], (res_c1, res_b) = _mm(n1, dgu1, 'tn', BF, 'ffn1_dw_gu', [chip_phase(RC1), final_phase(RB3)])
    chip_sums(RC1, res_c1)
    reduced.update(zip(RB3, res_b))
    (res,) = _run_phases([pair_phase(RC2)], 'rs_pair_exchange_tail')
    pair_sums(RC2, res)
    dn1, (res_c2, res_c1) = _mm(dgu1, full['ffn1_w_gu'], 'nt', F32, 'ffn1_dn', [chip_phase(RC2), final_phase(RC1)])
    chip_sums(RC2, res_c2)
    reduced.update(zip(RC1, res_c1))
    (dh0, G['norm_ffn1_pre']), (res_c2,) = _pre_bwd(dh1, dn1, h0, norm_ffn1_pre, 'ffn1_norm_bwd', [final_phase(RC2)])
    reduced.update(zip(RC2, res_c2))
    grad_x = dh0[NM:L][None]

    SW = max(D, POOL_WIDTH)

    def widen(a, fill=0.0):
        return jnp.pad(a, ((0, 0), (0, SW - a.shape[1])), constant_values=fill)

    rows = [widen(G[n]) for n in SMALL_VEC] + [widen(dh0[:NM]), widen(sq)]
    n_rows = len(SMALL_VEC) + NM + 1
    pad_rows = -(-n_rows // 8) * 8 - n_rows
    small = _all_reduce_small(jnp.concatenate(rows + [jnp.zeros((pad_rows, SW), F32)], axis=0))
    loss = (0.5 / D) * small[len(SMALL_VEC) + NM, 0]
    for i, n in enumerate(SMALL_VEC):
        reduced[n] = small[i:i + 1, :G[n].shape[1]]
    mw = meta_tokens.shape[1]
    reduced['meta_tokens'] = lax.dynamic_slice(small, (len(SMALL_VEC), chip * mw), (NM, mw))

    grads, deltas, new_m, new_v = {}, {}, {}, {}
    for n in names:
        grads[n], deltas[n], new_m[n], new_v[n] = _adam(W[n], reduced[n][None], M[n], V[n], f'adam_{n}')
    for d in (grads, deltas, new_m, new_v):
        d['w_in'] = jnp.swapaxes(d['w_in'], 1, 2)
    n_vec = len(SMALL_VEC)
    vec_w = jnp.concatenate([widen(W[n]) for n in SMALL_VEC] + [jnp.zeros((16 - n_vec, SW), F32)], axis=0)
    vec_m = jnp.concatenate([widen(M[n]) for n in SMALL_VEC] + [jnp.zeros((16 - n_vec, SW), F32)], axis=0)
    vec_v = jnp.concatenate([widen(V[n], 1.0) for n in SMALL_VEC] + [jnp.ones((16 - n_vec, SW), F32)], axis=0)
    vec_g = jnp.concatenate([small[:n_vec], jnp.zeros((16 - n_vec, SW), F32)], axis=0)
    _, vd, vm, vv = _adam(vec_w, vec_g, vec_m, vec_v, 'adam_vectors')
    for i, n in enumerate(SMALL_VEC):
        wdt = W[n].shape[1]
        grads[n], deltas[n], new_m[n], new_v[n] = reduced[n], vd[i:i + 1, :wdt], vm[i:i + 1, :wdt], vv[i:i + 1, :wdt]
    grads['meta_tokens'], deltas['meta_tokens'], new_m['meta_tokens'], new_v['meta_tokens'] = _adam(
        meta_tokens, reduced['meta_tokens'], m_meta_tokens, v_meta_tokens, 'adam_meta')

    return (loss, grad_x, *[grads[n] for n in WEIGHTS], *[deltas[n] for n in WEIGHTS], *[new_m[n] for n in WEIGHTS],
            *[new_v[n] for n in WEIGHTS])
```

```python
import functools

import jax
import jax.numpy as jnp
import numpy as np
from jax import lax
from jax.experimental import pallas as pl
from jax.experimental.pallas import tpu as pltpu

F32 = jnp.float32
BF = jnp.bfloat16
MESH = pl.DeviceIdType.MESH

EPS = 1e-6
N_CHIPS = 4
POOL_WINDOWS = (2, 4, 8, 16)
POOL_GROUP = 256
POOL_WIDTH = POOL_GROUP * len(POOL_WINDOWS)
QK_NOPE = 128
QK_ROPE = 64
V_DIM = 128
QK_DIM = QK_NOPE + QK_ROPE
HEAD_PAD = 256
ROPE_THETA = 10000.0
SOFTMAX_SCALE = QK_DIM ** -0.5
ADAM_LR = 0.001
ADAM_B1 = 0.9
ADAM_B2 = 0.999
ADAM_EPS = 1e-08
ADAM_WD = 0.01
ADAM_STEP = 10
ROW_ALIGN = 256
VMEM_LIMIT = 56 * 1024 * 1024
MM_TK_ROWS = 4352
MM_TK_COLS = 6400
NEG = -1e30
FLASH_FWD_CHUNK = 17
FLASH_BWD_CHUNK = 8
WEIGHTS = ['meta_tokens', 'norm_ffn1_pre', 'norm_ffn1_post', 'ffn1_w_gu', 'ffn1_w_down', 'norm_mix_pre',
           'norm_mix_post', 'w_in', 'pool_w', 'pool_scale', 'w_pool_o', 'q_a_norm', 'w_q_b', 'kv_a_norm', 'w_kv_b',
           'w_mla_o', 'w_out', 'norm_ffn2_pre', 'norm_ffn2_post', 'ffn2_w_gu', 'ffn2_w_down']
BIG = {'ffn1_w_gu': 'col', 'ffn1_w_down': 'row', 'w_in': 'row', 'pool_w': 'pool', 'w_pool_o': 'col', 'w_q_b': 'col',
       'w_kv_b': 'col', 'w_mla_o': 'row', 'w_out': 'row', 'ffn2_w_gu': 'col', 'ffn2_w_down': 'row'}
SMALL_VEC = ['norm_ffn1_pre', 'norm_ffn1_post', 'norm_mix_pre', 'norm_mix_post', 'norm_ffn2_pre', 'norm_ffn2_post',
             'pool_scale', 'q_a_norm', 'kv_a_norm']


def _div_tile(n, target, align):
    best = None
    for t in range(align, min(n, target) + 1, align):
        if n % t == 0:
            best = t
    return best if best is not None else n


def _cp(sem=None):
    return pltpu.CompilerParams(dimension_semantics=sem, vmem_limit_bytes=VMEM_LIMIT)


def _full_shape(kind, slab):
    if kind == 'col':
        return (slab[0], slab[1] * N_CHIPS)
    if kind == 'row':
        return (slab[0] * N_CHIPS, slab[1])
    return (slab[0], slab[1] * N_CHIPS, slab[2])


def _half_shape(kind, slab):
    if kind == 'pool':
        return (slab[0], slab[1] // 2, slab[2])
    if kind == 'row':
        return (slab[0], slab[1] // 2)
    return (slab[0] // 2, slab[1])


def _half_of_slab(ref, kind, c):
    if kind == 'pool':
        n = ref.shape[1] // 2
        return ref.at[:, pl.ds(c * n, n), :]
    if kind == 'row':
        n = ref.shape[1] // 2
        return ref.at[:, pl.ds(c * n, n)]
    n = ref.shape[0] // 2
    return ref.at[pl.ds(c * n, n), :]


def _piece(ref, kind, k, c):
    if kind == 'col':
        r, w = ref.shape[0] // 2, ref.shape[1] // N_CHIPS
        return ref.at[pl.ds(c * r, r), pl.ds(k * w, w)]
    if kind == 'row':
        r, w = ref.shape[0] // N_CHIPS, ref.shape[1] // 2
        return ref.at[pl.ds(k * r, r), pl.ds(c * w, w)]
    r = ref.shape[1] // (2 * N_CHIPS)
    return ref.at[:, pl.ds((2 * k + c) * r, r), :]


def _place():
    x, y, c = lax.axis_index('x'), lax.axis_index('y'), lax.axis_index('c')
    return x, y, c


def _peer_chip(x, y, j):
    px = 1 - x if (j >> 1) else x
    py = 1 - y if (j & 1) else y
    return px, py


ANY = pl.BlockSpec(memory_space=pl.ANY)


def _all_gather(bufs, kinds):
    n = len(bufs)

    def body(*refs):
        outs = refs[n:2 * n]
        ssem, rsem, fssem, frsem = refs[2 * n:]
        x, y, c = _place()
        me = 2 * x + y
        sib = (x, y, 1 - c)
        sends = []
        for w in range(n):
            for j in (1, 2, 3):
                px, py = _peer_chip(x, y, j)
                mine = _piece(outs[w], kinds[w], me, c)
                sends.append(pltpu.make_async_remote_copy(
                    src_ref=mine, dst_ref=mine, send_sem=ssem.at[w, j - 1], recv_sem=rsem.at[w, j - 1],
                    device_id=(px, py, c), device_id_type=MESH))
        for cp in sends:
            cp.start()
        fwds = []
        for w in range(n):
            for j in (1, 2, 3):
                px, py = _peer_chip(x, y, j)
                got = _piece(outs[w], kinds[w], 2 * px + py, c)
                pltpu.make_async_remote_copy(src_ref=got, dst_ref=got, send_sem=ssem.at[w, j - 1],
                                             recv_sem=rsem.at[w, j - 1], device_id=(px, py, c),
                                             device_id_type=MESH).wait_recv()
                fwd = pltpu.make_async_remote_copy(src_ref=got, dst_ref=got, send_sem=fssem.at[w, j - 1],
                                                   recv_sem=frsem.at[w, j - 1], device_id=sib, device_id_type=MESH)
                fwd.start()
                fwds.append(fwd)
        for w in range(n):
            for j in (1, 2, 3):
                px, py = _peer_chip(x, y, j)
                other = _piece(outs[w], kinds[w], 2 * px + py, 1 - c)
                pltpu.make_async_remote_copy(src_ref=other, dst_ref=other, send_sem=fssem.at[w, j - 1],
                                             recv_sem=frsem.at[w, j - 1], device_id=sib,
                                             device_id_type=MESH).wait_recv()
        for cp in sends + fwds:
            cp.wait_send()

    return pl.pallas_call(
        body, name='all_gather_weights', out_shape=[jax.ShapeDtypeStruct(b.shape, b.dtype) for b in bufs],
        in_specs=[ANY] * n, out_specs=[ANY] * n, input_output_aliases={w: w for w in range(n)},
        scratch_shapes=[pltpu.SemaphoreType.DMA((n, 3)), pltpu.SemaphoreType.DMA((n, 3)),
                        pltpu.SemaphoreType.DMA((n, 3)), pltpu.SemaphoreType.DMA((n, 3))],
    )(*bufs)


def _remote(src, dst, ssem, rsem, k, dev):
    return pltpu.make_async_remote_copy(src_ref=src, dst_ref=dst, send_sem=ssem.at[k], recv_sem=rsem.at[k],
                                        device_id=dev, device_id_type=MESH)


def _phase_gather_ici(bufs, kinds, peers=(1, 2, 3)):
    n, m = len(bufs), len(peers)

    def build(ins, outs, ssem, rsem):
        x, y, c = _place()
        me = 2 * x + y
        ds = []
        for w in range(n):
            for q, j in enumerate(peers):
                px, py = _peer_chip(x, y, j)
                mine = _piece(outs[w], kinds[w], me, c)
                got = _piece(outs[w], kinds[w], 2 * px + py, c)
                k = m * w + q
                ds.append((_remote(mine, mine, ssem, rsem, k, (px, py, c)), _remote(got, got, ssem, rsem, k, (px, py, c))))
        return ds

    return dict(ins=list(bufs), outs=[jax.ShapeDtypeStruct(b.shape, b.dtype) for b in bufs],
                alias={w: w for w in range(n)}, nsem=m * n, build=build)


def _phase_gather_d2d(bufs, kinds):
    n = len(bufs)

    def build(ins, outs, ssem, rsem):
        x, y, c = _place()
        sib = (x, y, 1 - c)
        ds = []
        for w in range(n):
            for j in (1, 2, 3):
                px, py = _peer_chip(x, y, j)
                have = _piece(outs[w], kinds[w], 2 * px + py, c)
                want = _piece(outs[w], kinds[w], 2 * px + py, 1 - c)
                k = 3 * w + j - 1
                ds.append((_remote(have, have, ssem, rsem, k, sib), _remote(want, want, ssem, rsem, k, sib)))
        return ds

    return dict(ins=list(bufs), outs=[jax.ShapeDtypeStruct(b.shape, b.dtype) for b in bufs],
                alias={w: w for w in range(n)}, nsem=3 * n, build=build)


def _phase_pair(grads, kinds, slabs):
    n = len(grads)

    def build(ins, outs, ssem, rsem):
        x, y, c = _place()
        sib = (x, y, 1 - c)
        ds = []
        for w in range(n):
            for k in range(N_CHIPS):
                cp = _remote(_piece(ins[w], kinds[w], k, 1 - c), outs[w].at[k], ssem, rsem, N_CHIPS * w + k, sib)
                ds.append((cp, cp))
        return ds

    return dict(ins=list(grads), alias={}, nsem=N_CHIPS * n, build=build,
                outs=[jax.ShapeDtypeStruct((N_CHIPS,) + _half_shape(k, s), g.dtype)
                      for g, k, s in zip(grads, kinds, slabs)])


def _phase_chip(sums, peers=(1, 2, 3), landing=None):
    n, m = len(sums), len(peers)

    def build(ins, outs, ssem, rsem):
        x, y, c = _place()
        ds = []
        for w in range(n):
            for q, j in enumerate(peers):
                px, py = _peer_chip(x, y, j)
                cp = _remote(ins[w].at[2 * px + py], outs[w].at[j - 1], ssem, rsem, m * w + q, (px, py, c))
                ds.append((cp, cp))
        return ds

    return dict(ins=list(sums) + (list(landing) if landing is not None else []),
                alias={n + w: w for w in range(n)} if landing is not None else {}, nsem=m * n, build=build,
                outs=[jax.ShapeDtypeStruct((3,) + s.shape[1:], s.dtype) for s in sums])


def _phase_final(slabs_half, kinds):
    n = len(slabs_half)

    def build(ins, outs, ssem, rsem):
        x, y, c = _place()
        sib = (x, y, 1 - c)
        ds = []
        for w in range(n):
            mine = _half_of_slab(outs[w], kinds[w], c)
            other = _half_of_slab(outs[w], kinds[w], 1 - c)
            ds.append((_remote(mine, mine, ssem, rsem, w, sib), _remote(other, other, ssem, rsem, w, sib)))
        return ds

    return dict(ins=list(slabs_half), outs=[jax.ShapeDtypeStruct(s.shape, s.dtype) for s in slabs_half],
                alias={w: w for w in range(n)}, nsem=n, build=build)


def _phase_operands(phases, n_main_in, n_main_out):
    ins, outs, alias, sems = [], [], {}, []
    for ph in phases:
        for i, o in ph['alias'].items():
            alias[n_main_in + len(ins) + i] = n_main_out + len(outs) + o
        ins += ph['ins']
        outs += ph['outs']
        sems += [pltpu.SemaphoreType.DMA((ph['nsem'],)), pltpu.SemaphoreType.DMA((ph['nsem'],))]
    return ins, outs, alias, sems


def _phase_copies(phases, in_refs, out_refs, sem_refs):
    ds, a, b = [], 0, 0
    for p, ph in enumerate(phases):
        ds += ph['build'](in_refs[a:a + len(ph['ins'])], out_refs[b:b + len(ph['outs'])], sem_refs[2 * p],
                          sem_refs[2 * p + 1])
        a += len(ph['ins'])
        b += len(ph['outs'])
    return ds


def _phase_start(phases, in_refs, out_refs, sem_refs):
    for send, _ in _phase_copies(phases, in_refs, out_refs, sem_refs):
        send.start()


def _phase_finish(phases, in_refs, out_refs, sem_refs):
    ds = _phase_copies(phases, in_refs, out_refs, sem_refs)
    for _, recv in ds:
        recv.wait_recv()
    for send, _ in ds:
        send.wait_send()


def _phase_results(phases, outs):
    res, b = [], 0
    for ph in phases:
        res.append(list(outs[b:b + len(ph['outs'])]))
        b += len(ph['outs'])
    return res


def _run_phases(phases, name):
    ins, out_shapes, alias, sems = _phase_operands(phases, 0, 0)
    n_in, n_out = len(ins), len(out_shapes)

    def body(*refs):
        in_refs, out_refs, sem_refs = refs[:n_in], refs[n_in:n_in + n_out], refs[n_in + n_out:]
        _phase_start(phases, in_refs, out_refs, sem_refs)
        _phase_finish(phases, in_refs, out_refs, sem_refs)

    outs = pl.pallas_call(body, name=name, out_shape=out_shapes, in_specs=[ANY] * n_in, out_specs=[ANY] * n_out,
                          input_output_aliases=alias, scratch_shapes=sems)(*ins)
    return _phase_results(phases, outs)


def _cast_place(shard, kind, place_arr, dtype, name):
    full = _full_shape(kind, shard.shape)
    if kind == 'pool':
        g, r, w = shard.shape
        grid = (g,)
        src = pl.BlockSpec((None, r, w), lambda i, s: (i, 0, 0))
        dst = pl.BlockSpec((None, r, w), lambda i, s: (i, s[1], 0))
    else:
        r, w = shard.shape
        tr, tc = _tile2(r, w)
        nrb, ncb = r // tr, w // tc
        grid = (nrb, ncb)
        src = pl.BlockSpec((tr, tc), lambda i, j, s: (i, j))
        if kind == 'col':
            dst = pl.BlockSpec((tr, tc), lambda i, j, s: (i, s[1] * ncb + j))
        else:
            dst = pl.BlockSpec((tr, tc), lambda i, j, s: (s[1] * nrb + i, j))

    def body(s_ref, x_ref, o_ref):
        o_ref[...] = x_ref[...].astype(o_ref.dtype)

    return pl.pallas_call(
        body, name=name,
        grid_spec=pltpu.PrefetchScalarGridSpec(num_scalar_prefetch=1, grid=grid, in_specs=[src], out_specs=dst),
        out_shape=jax.ShapeDtypeStruct(full, dtype), compiler_params=_cp(('parallel',) * len(grid)),
    )(place_arr, shard)


def _tile2(r, c):
    tr = _div_tile(r, max(16, (1 << 19) // c), 16)
    if tr == r or tr * c >= (1 << 17):
        return tr, c
    return r, _div_tile(c, max(128, (1 << 19) // r), 128)


def _pair_sum(grad, theirs, kind, slab, c_arr, name):
    hs = _half_shape(kind, slab)
    if kind == 'pool':
        grid = (N_CHIPS, hs[0], 1)
        own = pl.BlockSpec((None, hs[1], hs[2]), lambda k, i, j, s: (i, 2 * k + s[0], 0))
        stk = pl.BlockSpec((None, None, hs[1], hs[2]), lambda k, i, j, s: (k, i, 0, 0))
    else:
        tr, tc = _tile2(*hs)
        nrb, ncb = hs[0] // tr, hs[1] // tc
        grid = (N_CHIPS, nrb, ncb)
        if kind == 'col':
            own = pl.BlockSpec((tr, tc), lambda k, i, j, s: (s[0] * nrb + i, k * ncb + j))
        else:
            own = pl.BlockSpec((tr, tc), lambda k, i, j, s: (k * nrb + i, s[0] * ncb + j))
        stk = pl.BlockSpec((None, tr, tc), lambda k, i, j, s: (k, i, j))

    def body(s_ref, a_ref, b_ref, o_ref):
        o_ref[...] = (a_ref[...].astype(F32) + b_ref[...].astype(F32)).astype(o_ref.dtype)

    return pl.pallas_call(
        body, name=name,
        grid_spec=pltpu.PrefetchScalarGridSpec(num_scalar_prefetch=1, grid=grid, in_specs=[own, stk], out_specs=stk),
        out_shape=jax.ShapeDtypeStruct((N_CHIPS,) + hs, BF), compiler_params=_cp(('parallel',) * 3),
    )(c_arr, grad, theirs)


def _chip_sum(sums, landed, kind, slab, place_arr, name):
    hs = _half_shape(kind, slab)
    if kind == 'pool':
        grid = (hs[0], 1)
        blk = (None, None, hs[1], hs[2])
        mine = pl.BlockSpec(blk, lambda i, j, s: (s[1], i, 0, 0))
        land = [pl.BlockSpec(blk, lambda i, j, s, p=p: (p, i, 0, 0)) for p in range(3)]
        out = pl.BlockSpec((None, hs[1], hs[2]), lambda i, j, s: (i, s[0], 0))
    else:
        tr, tc = _tile2(*hs)
        nrb, ncb = hs[0] // tr, hs[1] // tc
        grid = (nrb, ncb)
        blk = (None, tr, tc)
        mine = pl.BlockSpec(blk, lambda i, j, s: (s[1], i, j))
        land = [pl.BlockSpec(blk, lambda i, j, s, p=p: (p, i, j)) for p in range(3)]
        if kind == 'col':
            out = pl.BlockSpec((tr, tc), lambda i, j, s: (s[0] * nrb + i, j))
        else:
            out = pl.BlockSpec((tr, tc), lambda i, j, s: (i, s[0] * ncb + j))

    def body(s_ref, a_ref, b_ref, c_ref, d_ref, o_ref):
        o_ref[...] = ((a_ref[...].astype(F32) + b_ref[...].astype(F32)) + c_ref[...].astype(F32)) + d_ref[...].astype(F32)

    return pl.pallas_call(
        body, name=name,
        grid_spec=pltpu.PrefetchScalarGridSpec(num_scalar_prefetch=1, grid=grid, in_specs=[mine] + land, out_specs=out),
        out_shape=jax.ShapeDtypeStruct(tuple(slab), F32), compiler_params=_cp(('parallel',) * 2),
    )(place_arr, sums, landed, landed, landed)


def _all_reduce_small(buf):
    rows, cols = buf.shape

    def body(in_ref, out_ref, land, ssem, rsem):
        x, y, c = _place()
        me = 4 * x + 2 * y + c
        land[me] = in_ref[...]
        started = []
        for j in range(1, 8):
            px = 1 - x if (j >> 2) & 1 else x
            py = 1 - y if (j >> 1) & 1 else y
            pc = 1 - c if j & 1 else c
            cp = pltpu.make_async_remote_copy(src_ref=in_ref, dst_ref=land.at[me], send_sem=ssem.at[j - 1],
                                              recv_sem=rsem.at[j - 1], device_id=(px, py, pc), device_id_type=MESH)
            cp.start()
            started.append(cp)
        for j in range(1, 8):
            px = 1 - x if (j >> 2) & 1 else x
            py = 1 - y if (j >> 1) & 1 else y
            pc = 1 - c if j & 1 else c
            slot = land.at[4 * px + 2 * py + pc]
            pltpu.make_async_remote_copy(src_ref=slot, dst_ref=slot, send_sem=ssem.at[j - 1], recv_sem=rsem.at[j - 1],
                                         device_id=(px, py, pc), device_id_type=MESH).wait_recv()
        for cp in started:
            cp.wait_send()
        acc = land[0]
        for d in range(1, 8):
            acc = acc + land[d]
        out_ref[...] = acc

    return pl.pallas_call(
        body, name='all_reduce_small', out_shape=jax.ShapeDtypeStruct((rows, cols), F32),
        in_specs=[pl.BlockSpec(memory_space=pltpu.VMEM)], out_specs=pl.BlockSpec(memory_space=pltpu.VMEM),
        scratch_shapes=[pltpu.VMEM((8, rows, cols), F32), pltpu.SemaphoreType.DMA((7,)), pltpu.SemaphoreType.DMA((7,))],
    )(buf)


def _elementwise(fn, ins, lead_index, out_shape, out_dtypes, name):
    nd = len(out_shape)
    r, cdim = out_shape[-2], out_shape[-1]
    tr, tc = _tile2(r, cdim)
    grid = tuple(out_shape[:-2]) + (r // tr, cdim // tc)
    block = (None,) * (nd - 2) + (tr, tc)

    def spec(lead):
        if lead is None:
            return pl.BlockSpec(block, lambda *g: tuple(g))
        return pl.BlockSpec((None,) + block, lambda *g, lead=lead: (lead,) + tuple(g))

    n_in = len(ins)

    def body(*refs):
        res = fn(*[r_[...] for r_ in refs[:n_in]])
        for o_ref, v in zip(refs[n_in:], res):
            o_ref[...] = v.astype(o_ref.dtype)

    return pl.pallas_call(
        body, name=name, grid=grid, in_specs=[spec(l) for l in lead_index],
        out_specs=[spec(None) for _ in out_dtypes],
        out_shape=[jax.ShapeDtypeStruct(tuple(out_shape), dt) for dt in out_dtypes],
        compiler_params=_cp(('parallel',) * len(grid)),
    )(*ins)


def _adam_fn(w, g, m, v):
    m = ADAM_B1 * m + (1.0 - ADAM_B1) * g
    v = ADAM_B2 * v + (1.0 - ADAM_B2) * (g * g)
    m_hat = m / (1.0 - ADAM_B1 ** ADAM_STEP)
    v_hat = v / (1.0 - ADAM_B2 ** ADAM_STEP)
    delta = -ADAM_LR * (m_hat / (jnp.sqrt(v_hat) + ADAM_EPS) + ADAM_WD * w)
    return g, delta, m, v


def _adam(w, g, m, v, name):
    return _elementwise(_adam_fn, [w, g, m, v], [None] * 4, w.shape, [F32] * 4, name)


_DIMS = {'nn': (((1,), (0,)), ((), ())), 'nt': (((1,), (1,)), ((), ())), 'tn': (((0,), (0,)), ((), ()))}


def _mm(a, b, mode, out_dtype, name, phases=()):
    if mode == 'nn':
        (M, K), N = a.shape, b.shape[1]
    elif mode == 'nt':
        (M, K), N = a.shape, b.shape[0]
    else:
        (K, M), N = a.shape, b.shape[1]
    if mode == 'tn':
        tm, tn, tk = _div_tile(M, 512, 128), _div_tile(N, 1024, 128), _div_tile(K, MM_TK_ROWS, 16)
        if tm < 256:
            tm = _div_tile(M, 1024, 128)
    else:
        tk = _div_tile(K, MM_TK_COLS, 128)
        tm, tn = _div_tile(M, 1088, 16), _div_tile(N, 1024 if tk <= 2816 else 512, 128)
    nk = K // tk
    a_spec = {'nn': pl.BlockSpec((tm, tk), lambda i, j, k: (i, k)), 'nt': pl.BlockSpec((tm, tk), lambda i, j, k: (i, k)),
              'tn': pl.BlockSpec((tk, tm), lambda i, j, k: (k, i))}[mode]
    b_spec = {'nn': pl.BlockSpec((tk, tn), lambda i, j, k: (k, j)), 'nt': pl.BlockSpec((tn, tk), lambda i, j, k: (j, k)),
              'tn': pl.BlockSpec((tk, tn), lambda i, j, k: (k, j))}[mode]
    dims = _DIMS[mode]
    gm, gn = M // tm, N // tn
    extra_in, extra_out, alias, sems = _phase_operands(phases, 2, 1)
    n_ei, n_eo = len(extra_in), len(extra_out)

    def body(*refs):
        a_ref, b_ref, ein = refs[0], refs[1], refs[2:2 + n_ei]
        o_ref, eout = refs[2 + n_ei], refs[3 + n_ei:3 + n_ei + n_eo]
        acc_ref, sem_refs = refs[3 + n_ei + n_eo], refs[4 + n_ei + n_eo:]
        i, j, k = pl.program_id(0), pl.program_id(1), pl.program_id(2)
        if phases:
            @pl.when((i == 0) & (j == 0) & (k == 0))
            def _():
                _phase_start(phases, ein, eout, sem_refs)

        part = lax.dot_general(a_ref[...], b_ref[...], dims, preferred_element_type=F32)
        if nk == 1:
            o_ref[...] = part.astype(o_ref.dtype)
        else:
            @pl.when(k == 0)
            def _():
                acc_ref[...] = part

            @pl.when(k > 0)
            def _():
                acc_ref[...] += part

            @pl.when(k == nk - 1)
            def _():
                o_ref[...] = acc_ref[...].astype(o_ref.dtype)

        if phases:
            @pl.when((i == gm - 1) & (j == gn - 1) & (k == nk - 1))
            def _():
                _phase_finish(phases, ein, eout, sem_refs)

    outs = pl.pallas_call(
        body, name=name, grid=(gm, gn, nk), in_specs=[a_spec, b_spec] + [ANY] * n_ei,
        out_specs=[pl.BlockSpec((tm, tn), lambda i, j, k: (i, j))] + [ANY] * n_eo,
        out_shape=[jax.ShapeDtypeStruct((M, N), out_dtype)] + extra_out, input_output_aliases=alias,
        scratch_shapes=[pltpu.VMEM((tm, tn) if nk > 1 else (8, 128), F32)] + sems,
        compiler_params=_cp(('arbitrary',) * 3 if phases else ('parallel', 'parallel', 'arbitrary')),
    )(a, b, *extra_in)
    if phases:
        return outs[0], _phase_results(phases, outs[1:])
    return outs[0]


def _rb(tm, w, col=0):
    return pl.BlockSpec((tm, w), lambda i, col=col: (i, col))


def _fixed(shape):
    return pl.BlockSpec(shape, lambda i: (0,) * len(shape))


def _rms(x):
    return lax.rsqrt(jnp.mean(x * x, axis=-1, keepdims=True) + EPS)


def _norm_fwd(x, gain, name, width=None, col=0):
    Lp = x.shape[0]
    width = width or x.shape[1]
    tm = _div_tile(Lp, 256, 16)

    def body(x_ref, g_ref, o_ref):
        v = x_ref[...]
        o_ref[...] = (v * _rms(v) * g_ref[...]).astype(o_ref.dtype)

    return pl.pallas_call(
        body, name=name, grid=(Lp // tm,), in_specs=[_rb(tm, width, col), _fixed((1, width))], out_specs=_rb(tm, width),
        out_shape=jax.ShapeDtypeStruct((Lp, width), BF), compiler_params=_cp(('parallel',)),
    )(x, gain)


def _post_residual(h, f, gain, scale, next_gain, name, phases=()):
    Lp, D = h.shape
    tm = _div_tile(Lp, 256, 16)

    def body(h_ref, f_ref, g_ref, ng_ref, o_ref, n_ref):
        v = f_ref[...]
        out = h_ref[...] + scale * (v * _rms(v) * g_ref[...])
        o_ref[...] = out
        n_ref[...] = (out * _rms(out) * ng_ref[...]).astype(n_ref.dtype)

    return _call_carrying(
        body, name, (Lp // tm,), [_rb(tm, D), _rb(tm, D), _fixed((1, D)), _fixed((1, D))], [_rb(tm, D), _rb(tm, D)],
        [jax.ShapeDtypeStruct((Lp, D), F32), jax.ShapeDtypeStruct((Lp, D), BF)], [], (h, f, gain, next_gain), phases)


def _pre_bwd(dres, dn, h, gain, name, phases=(), before=None):
    Lp, D = h.shape
    tm = _div_tile(Lp, 256, 16)

    def accumulate(ref, part):
        @pl.when(pl.program_id(0) == 0)
        def _():
            ref[...] = part

        @pl.when(pl.program_id(0) > 0)
        def _():
            ref[...] += part

    def body(dres_ref, dn_ref, h_ref, g_ref, *rest):
        v = h_ref[...]
        r = _rms(v)
        dy = dn_ref[...]
        w = dy * g_ref[...]
        dh = dres_ref[...] + r * w - v * (r * r * r) * jnp.mean(w * v, axis=-1, keepdims=True)
        if before is None:
            dh_ref, dg_ref = rest
        else:
            f_ref, pg_ref, dh_ref, dg_ref, df_ref, dpg_ref = rest
            fv = f_ref[...]
            fr = _rms(fv)
            fdy = before[2] * dh
            fw = fdy * pg_ref[...]
            df_ref[...] = (fr * fw - fv * (fr * fr * fr) * jnp.mean(fw * fv, axis=-1, keepdims=True)).astype(BF)
            accumulate(dpg_ref, jnp.sum(fdy * fv * fr, axis=0, keepdims=True))
        dh_ref[...] = dh
        accumulate(dg_ref, jnp.sum(dy * v * r, axis=0, keepdims=True))

    in_specs = [_rb(tm, D), _rb(tm, D), _rb(tm, D), _fixed((1, D))]
    out_specs = [_rb(tm, D), _fixed((1, D))]
    out_shape = [jax.ShapeDtypeStruct((Lp, D), F32), jax.ShapeDtypeStruct((1, D), F32)]
    args = (dres, dn, h, gain)
    if before is not None:
        in_specs += [_rb(tm, D), _fixed((1, D))]
        out_specs += [_rb(tm, D), _fixed((1, D))]
        out_shape += [jax.ShapeDtypeStruct((Lp, D), BF), jax.ShapeDtypeStruct((1, D), F32)]
        args += (before[0], before[1])
    return _call_carrying(body, name, (Lp // tm,), in_specs, out_specs, out_shape, [], args, phases)


def _swiglu_fwd(gu, name, phases=()):
    Lp, F2 = gu.shape
    F = F2 // 2
    tm = _div_tile(Lp, 256, 16)

    def body(gu_ref, a_ref):
        g = gu_ref[:, :F].astype(F32)
        u = gu_ref[:, F:].astype(F32)
        a_ref[...] = (g * jax.nn.sigmoid(g) * u).astype(a_ref.dtype)

    return _call_carrying(body, name, (Lp // tm,), [_rb(tm, F2)], [_rb(tm, F)],
                          [jax.ShapeDtypeStruct((Lp, F), BF)], [], (gu,), phases)


def _swiglu_bwd(da, gu, name):
    Lp, F2 = gu.shape
    F = F2 // 2
    tm = _div_tile(Lp, 256, 16)

    def body(da_ref, gu_ref, o_ref):
        g = gu_ref[:, :F].astype(F32)
        u = gu_ref[:, F:].astype(F32)
        da_ = da_ref[...].astype(F32)
        s = jax.nn.sigmoid(g)
        o_ref[:, :F] = (da_ * u * (s * (1.0 + g * (1.0 - s)))).astype(o_ref.dtype)
        o_ref[:, F:] = (da_ * (g * s)).astype(o_ref.dtype)

    return pl.pallas_call(
        body, name=name, grid=(Lp // tm,), in_specs=[_rb(tm, F), _rb(tm, F2)], out_specs=_rb(tm, F2),
        out_shape=jax.ShapeDtypeStruct((Lp, F2), BF), compiler_params=_cp(('parallel',)),
    )(da, gu)


def _tail(h, f, gain, scale, tgt, n_meta, n_real, name):
    Lp, D = h.shape
    tm = _div_tile(Lp, 256, 16)

    def body(h_ref, f_ref, g_ref, t_ref, d_ref, df_ref, dg_ref, l_ref):
        v = f_ref[...]
        r = _rms(v)
        gain_ = g_ref[...]
        row = lax.broadcasted_iota(jnp.int32, (tm, 1), 0) + pl.program_id(0) * tm
        ok = (row >= n_meta) & (row < n_meta + n_real)
        err = jnp.where(ok, h_ref[...] + scale * (v * r * gain_) - t_ref[...], 0.0)
        d_out = err / D
        d_ref[...] = d_out
        dy = scale * d_out
        w = dy * gain_
        df_ref[...] = (r * w - v * (r * r * r) * jnp.mean(w * v, axis=-1, keepdims=True)).astype(df_ref.dtype)
        dg_part = jnp.sum(dy * v * r, axis=0, keepdims=True)
        l_part = jnp.full((1, 128), jnp.sum(err * err), F32)

        @pl.when(pl.program_id(0) == 0)
        def _():
            dg_ref[...] = dg_part
            l_ref[...] = l_part

        @pl.when(pl.program_id(0) > 0)
        def _():
            dg_ref[...] += dg_part
            l_ref[...] += l_part

    return pl.pallas_call(
        body, name=name, grid=(Lp // tm,), in_specs=[_rb(tm, D), _rb(tm, D), _fixed((1, D)), _rb(tm, D)],
        out_specs=[_rb(tm, D), _rb(tm, D), _fixed((1, D)), _fixed((1, 128))],
        out_shape=[jax.ShapeDtypeStruct((Lp, D), F32), jax.ShapeDtypeStruct((Lp, D), BF),
                   jax.ShapeDtypeStruct((1, D), F32), jax.ShapeDtypeStruct((1, 128), F32)],
        compiler_params=_cp(('arbitrary',)),
    )(h, f, gain, tgt)


def _split_bf16(v):
    hi = v.astype(BF)
    return hi, (v - hi.astype(F32)).astype(BF)


def _pool_fwd(z, name):
    Lp = z.shape[0]
    T = _div_tile(Lp, 256, 16)
    G = len(POOL_WINDOWS)

    def body(cur_ref, prev_ref, d_ref):
        i, g = pl.program_id(0), pl.program_id(1)
        w = jnp.left_shift(2, g)
        rr = lax.broadcasted_iota(jnp.int32, (T, T), 0)
        cc = lax.broadcasted_iota(jnp.int32, (T, T), 1)
        b_cur = jnp.where((cc <= rr) & (cc > rr - w), 1.0, 0.0).astype(BF)
        w_prev = jnp.where(i > 0, w, 0)
        b_prev = jnp.where(cc - T > rr - w_prev, 1.0, 0.0).astype(BF)
        u = cur_ref[...]
        s = jnp.zeros((T, POOL_GROUP), F32)
        for part in _split_bf16(u):
            s += jnp.dot(b_cur, part, preferred_element_type=F32)
        for part in _split_bf16(prev_ref[...]):
            s += jnp.dot(b_prev, part, preferred_element_type=F32)
        t = lax.broadcasted_iota(jnp.int32, (T, 1), 0) + i * T
        cnt = jnp.minimum(w, t + 1).astype(F32)
        d_ref[...] = (s / cnt - u).astype(d_ref.dtype)

    return pl.pallas_call(
        body, name=name, grid=(Lp // T, G),
        in_specs=[pl.BlockSpec((T, POOL_GROUP), lambda i, g: (i, g)),
                  pl.BlockSpec((T, POOL_GROUP), lambda i, g: (jnp.maximum(i - 1, 0), g))],
        out_specs=pl.BlockSpec((T, POOL_GROUP), lambda i, g: (i, g)),
        out_shape=jax.ShapeDtypeStruct((Lp, POOL_WIDTH), BF), compiler_params=_cp(('parallel', 'parallel')),
    )(z, z)


def _pool_bwd(dd, name):
    Lp = dd.shape[0]
    T = _div_tile(Lp, 256, 16)
    G = len(POOL_WINDOWS)
    n_t = Lp // T

    def body(cur_ref, next_ref, o_ref):
        i, g = pl.program_id(0), pl.program_id(1)
        w = jnp.left_shift(2, g)
        rr = lax.broadcasted_iota(jnp.int32, (T, T), 0)
        cc = lax.broadcasted_iota(jnp.int32, (T, T), 1)
        b_cur = jnp.where((cc >= rr) & (cc < rr + w), 1.0, 0.0).astype(BF)
        w_next = jnp.where(i < n_t - 1, w, 0)
        b_next = jnp.where(cc + T < rr + w_next, 1.0, 0.0).astype(BF)
        t = lax.broadcasted_iota(jnp.int32, (T, 1), 0) + i * T
        cur = cur_ref[...]
        e_cur = cur / jnp.minimum(w, t + 1).astype(F32)
        e_next = next_ref[...] / jnp.minimum(w, t + T + 1).astype(F32)
        s = jnp.zeros((T, POOL_GROUP), F32)
        for part in _split_bf16(e_cur):
            s += jnp.dot(b_cur, part, preferred_element_type=F32)
        for part in _split_bf16(e_next):
            s += jnp.dot(b_next, part, preferred_element_type=F32)
        o_ref[...] = (s - cur).astype(o_ref.dtype)

    return pl.pallas_call(
        body, name=name, grid=(n_t, G),
        in_specs=[pl.BlockSpec((T, POOL_GROUP), lambda i, g: (i, g)),
                  pl.BlockSpec((T, POOL_GROUP), lambda i, g: (jnp.minimum(i + 1, n_t - 1), g))],
        out_specs=pl.BlockSpec((T, POOL_GROUP), lambda i, g: (i, g)),
        out_shape=jax.ShapeDtypeStruct((Lp, POOL_WIDTH), BF), compiler_params=_cp(('parallel', 'parallel')),
    )(dd, dd)


def _pool_mix_fwd(d, pool_w, scale, name):
    Lp = d.shape[0]
    G = len(POOL_WINDOWS)
    tm = _div_tile(Lp, 1088, 16)

    def body(d_ref, w_ref, s_ref, e_ref, y_ref):
        e = jnp.dot(d_ref[...], w_ref[...], preferred_element_type=F32)
        e_ref[...] = e.astype(e_ref.dtype)
        y_ref[...] = (e * s_ref[...]).astype(y_ref.dtype)

    blk = pl.BlockSpec((tm, POOL_GROUP), lambda g, i: (i, g))
    return pl.pallas_call(
        body, name=name, grid=(G, Lp // tm),
        in_specs=[blk, pl.BlockSpec((None, POOL_GROUP, POOL_GROUP), lambda g, i: (g, 0, 0)),
                  pl.BlockSpec((1, POOL_GROUP), lambda g, i: (0, g))],
        out_specs=[blk, blk], out_shape=[jax.ShapeDtypeStruct((Lp, POOL_WIDTH), BF)] * 2,
        compiler_params=_cp(('parallel', 'parallel')),
    )(d, pool_w, scale)


def _pool_mix_bwd(dyp, e, d, pool_w, scale, name):
    Lp = d.shape[0]
    G = len(POOL_WINDOWS)
    tm = _div_tile(Lp, 1088, 16)

    def body(dy_ref, e_ref, d_ref, w_ref, s_ref, dd_ref, ds_ref, dw_ref):
        i = pl.program_id(1)
        dy = dy_ref[...]
        de = (dy * s_ref[...]).astype(BF)
        dd_ref[...] = lax.dot_general(de, w_ref[...], _DIMS['nt'], preferred_element_type=F32)
        ds_part = jnp.sum(dy * e_ref[...].astype(F32), axis=0, keepdims=True)
        dw_part = lax.dot_general(d_ref[...], de, _DIMS['tn'], preferred_element_type=F32)

        @pl.when(i == 0)
        def _():
            ds_ref[...] = ds_part
            dw_ref[...] = dw_part

        @pl.when(i > 0)
        def _():
            ds_ref[...] += ds_part
            dw_ref[...] += dw_part

    blk = pl.BlockSpec((tm, POOL_GROUP), lambda g, i: (i, g))
    wblk = pl.BlockSpec((None, POOL_GROUP, POOL_GROUP), lambda g, i: (g, 0, 0))
    sblk = pl.BlockSpec((1, POOL_GROUP), lambda g, i: (0, g))
    return pl.pallas_call(
        body, name=name, grid=(G, Lp // tm), in_specs=[blk, blk, blk, wblk, sblk], out_specs=[blk, sblk, wblk],
        out_shape=[jax.ShapeDtypeStruct((Lp, POOL_WIDTH), F32), jax.ShapeDtypeStruct((1, POOL_WIDTH), F32),
                   jax.ShapeDtypeStruct((G, POOL_GROUP, POOL_GROUP), F32)],
        compiler_params=_cp(('parallel', 'arbitrary')),
    )(dyp, e, d, pool_w, scale)


def _rot_half(t):
    lane = lax.broadcasted_iota(jnp.int32, t.shape, 1)
    half = QK_ROPE // 2
    return jnp.where(lane < half, -pltpu.roll(t, 128 - half, 1), pltpu.roll(t, half, 1))


def _lora_norms(z, q_gain, kv_gain, lay, name):
    Lp = z.shape[0]
    QL, KVL = lay['QL'], lay['KVL']
    tm = _div_tile(Lp, 256, 16)

    def body(q_ref, kv_ref, qg_ref, kg_ref, qo_ref, ko_ref):
        a = q_ref[...]
        qo_ref[...] = (a * _rms(a) * qg_ref[...]).astype(BF)
        b = kv_ref[...]
        ko_ref[...] = (b * _rms(b) * kg_ref[...]).astype(BF)

    return pl.pallas_call(
        body, name=name, grid=(Lp // tm,),
        in_specs=[_rb(tm, QL, lay['cq'] // QL), _rb(tm, KVL, lay['ckv'] // KVL), _fixed((1, QL)), _fixed((1, KVL))],
        out_specs=[_rb(tm, QL), _rb(tm, KVL)],
        out_shape=[jax.ShapeDtypeStruct((Lp, QL), BF), jax.ShapeDtypeStruct((Lp, KVL), BF)],
        compiler_params=_cp(('parallel',)),
    )(z, z, q_gain, kv_gain)


def _lora_norms_bwd(dqn, dkn, z, q_gain, kv_gain, lay, name):
    Lp = z.shape[0]
    QL, KVL = lay['QL'], lay['KVL']
    tm = _div_tile(Lp, 256, 16)

    def one(dy, v, gain):
        r = _rms(v)
        w = dy * gain
        return r * w - v * (r * r * r) * jnp.mean(w * v, axis=-1, keepdims=True), jnp.sum(dy * v * r, axis=0, keepdims=True)

    def body(dq_ref, dk_ref, q_ref, kv_ref, qg_ref, kg_ref, o_ref, dqg_ref, dkg_ref):
        da, ga = one(dq_ref[...], q_ref[...], qg_ref[...])
        db, gb = one(dk_ref[...], kv_ref[...], kg_ref[...])
        o_ref[:, :QL] = da.astype(BF)
        o_ref[:, QL:] = db.astype(BF)

        @pl.when(pl.program_id(0) == 0)
        def _():
            dqg_ref[...] = ga
            dkg_ref[...] = gb

        @pl.when(pl.program_id(0) > 0)
        def _():
            dqg_ref[...] += ga
            dkg_ref[...] += gb

    return pl.pallas_call(
        body, name=name, grid=(Lp // tm,),
        in_specs=[_rb(tm, QL), _rb(tm, KVL), _rb(tm, QL, lay['cq'] // QL), _rb(tm, KVL, lay['ckv'] // KVL),
                  _fixed((1, QL)), _fixed((1, KVL))],
        out_specs=[_rb(tm, QL + KVL), _fixed((1, QL)), _fixed((1, KVL))],
        out_shape=[jax.ShapeDtypeStruct((Lp, QL + KVL), BF), jax.ShapeDtypeStruct((1, QL), F32),
                   jax.ShapeDtypeStruct((1, KVL), F32)],
        compiler_params=_cp(('arbitrary',)),
    )(dqn, dkn, z, z, q_gain, kv_gain)


def _qk_prep(q_raw, kv, z, cos, sin, lay, H, name):
    Lp = q_raw.shape[0]
    W = H * HEAD_PAD
    tm = _div_tile(Lp, 256, 16)

    def body(q_ref, kv_ref, kr_ref, c_ref, s_ref, qo_ref, ko_ref):
        c, s = c_ref[...], s_ref[...]

        def rope(t):
            return t * c + _rot_half(t) * s

        kpe = rope(kr_ref[...]).astype(BF)
        for h in range(H):
            b = h * HEAD_PAD
            qo_ref[:, b:b + 128] = (q_ref[:, b:b + 128] * SOFTMAX_SCALE).astype(BF)
            qo_ref[:, b + 128:b + 256] = (rope(q_ref[:, b + 128:b + 256]) * SOFTMAX_SCALE).astype(BF)
            ko_ref[:, b:b + 128] = kv_ref[:, b:b + 128]
            ko_ref[:, b + 128:b + 256] = kpe

    return pl.pallas_call(
        body, name=name, grid=(Lp // tm,),
        in_specs=[_rb(tm, W), _rb(tm, W), _rb(tm, 128, lay['kr'] // 128), _rb(tm, 128), _rb(tm, 128)],
        out_specs=[_rb(tm, W), _rb(tm, W)], out_shape=[jax.ShapeDtypeStruct((Lp, W), BF)] * 2,
        compiler_params=_cp(('parallel',)),
    )(q_raw, kv, z, cos, sin)


def _qk_prep_bwd(dQ, dK, dV, cos, sin, H, name):
    Lp = dQ.shape[0]
    W = H * HEAD_PAD
    tm = _div_tile(Lp, 256, 16)

    def body(dq_ref, dk_ref, dv_ref, c_ref, s_ref, qo_ref, kvo_ref, kro_ref):
        c, s = c_ref[...], s_ref[...]

        def unrope(t):
            return t * c - _rot_half(t * s)

        acc = jnp.zeros((tm, 128), F32)
        for h in range(H):
            b = h * HEAD_PAD
            qo_ref[:, b:b + 128] = (dq_ref[:, b:b + 128] * SOFTMAX_SCALE).astype(BF)
            qo_ref[:, b + 128:b + 256] = (unrope(dq_ref[:, b + 128:b + 256]) * SOFTMAX_SCALE).astype(BF)
            kvo_ref[:, b:b + 128] = dk_ref[:, b:b + 128].astype(BF)
            kvo_ref[:, b + 128:b + 256] = dv_ref[:, h * V_DIM:(h + 1) * V_DIM]
            acc += dk_ref[:, b + 128:b + 256]
        kro_ref[...] = unrope(acc).astype(BF)

    return pl.pallas_call(
        body, name=name, grid=(Lp // tm,),
        in_specs=[_rb(tm, W), _rb(tm, W), _rb(tm, H * V_DIM), _rb(tm, 128), _rb(tm, 128)],
        out_specs=[_rb(tm, W), _rb(tm, W), _rb(tm, 128)],
        out_shape=[jax.ShapeDtypeStruct((Lp, W), BF), jax.ShapeDtypeStruct((Lp, W), BF),
                   jax.ShapeDtypeStruct((Lp, 128), BF)],
        compiler_params=_cp(('parallel',)),
    )(dQ, dK, dV, cos, sin)


def _call_carrying(core, name, grid, in_specs, out_specs, out_shape, scratch, args, phases):
    n_in, n_out, n_scr = len(in_specs), len(out_specs), len(scratch)
    extra_in, extra_out, alias, sems = _phase_operands(phases, n_in, n_out)
    n_ei, n_eo = len(extra_in), len(extra_out)

    def body(*refs):
        ins, ein = refs[:n_in], refs[n_in:n_in + n_ei]
        outs = refs[n_in + n_ei:n_in + n_ei + n_out]
        eout = refs[n_in + n_ei + n_out:n_in + n_ei + n_out + n_eo]
        rest = refs[n_in + n_ei + n_out + n_eo:]
        scr, sem_refs = rest[:n_scr], rest[n_scr:]
        if phases:
            ids = [pl.program_id(d) for d in range(len(grid))]
            first, last = ids[0] == 0, ids[0] == grid[0] - 1
            for d in range(1, len(grid)):
                first, last = first & (ids[d] == 0), last & (ids[d] == grid[d] - 1)

            @pl.when(first)
            def _():
                _phase_start(phases, ein, eout, sem_refs)

        core(*ins, *outs, *scr)
        if phases:
            @pl.when(last)
            def _():
                _phase_finish(phases, ein, eout, sem_refs)

    outs = pl.pallas_call(
        body, name=name, grid=grid, in_specs=list(in_specs) + [ANY] * n_ei, out_specs=list(out_specs) + [ANY] * n_eo,
        out_shape=list(out_shape) + extra_out, input_output_aliases=alias, scratch_shapes=list(scratch) + sems,
        compiler_params=_cp(('arbitrary',) * len(grid)),
    )(*args, *extra_in)
    return list(outs[:n_out]), _phase_results(phases, outs[n_out:])


def _flash_fwd(Q, K, kv, H, name, phases=()):
    Lp = Q.shape[0]
    T = _div_tile(Lp, 256, 16)

    n_t = Lp // T
    FC = min(FLASH_FWD_CHUNK, n_t)
    CH = FC * T

    def body(q_ref, k_ref, v_ref, o_ref, lse_ref, m_s, l_s, acc_s):
        i = pl.program_id(1)
        q = q_ref[...]
        m_s[...] = jnp.full((T, 1), NEG, F32)
        l_s[...] = jnp.zeros((T, 1), F32)
        acc_s[...] = jnp.zeros((T, V_DIM), F32)

        def step(start, width, masked):
            parts = [(start, width, False)] if not masked else (
                ([(start, width - T, False)] if width > T else []) + [(start + (width - T), T, True)])
            scores = []
            for at, wd, diag in parts:
                s = lax.dot_general(q, k_ref[pl.ds(at, wd), :], _DIMS['nt'], preferred_element_type=F32)
                if diag:
                    rr = lax.broadcasted_iota(jnp.int32, (T, T), 0)
                    cc = lax.broadcasted_iota(jnp.int32, (T, T), 1)
                    s = jnp.where(cc <= rr, s, NEG)
                scores.append(s)
            m = m_s[...]
            m_new = m
            for s in scores:
                m_new = jnp.maximum(m_new, jnp.max(s, axis=-1, keepdims=True))
            alpha = jnp.exp(m - m_new)
            l = alpha * l_s[...]
            acc = alpha * acc_s[...]
            for (at, wd, _), s in zip(parts, scores):
                p = jnp.exp(s - m_new)
                l = l + jnp.sum(p, axis=-1, keepdims=True)
                acc = acc + jnp.dot(p.astype(BF), v_ref[pl.ds(at, wd), :], preferred_element_type=F32)
            l_s[...] = l
            acc_s[...] = acc
            m_s[...] = m_new

        n_full = i // FC

        def full(cidx, carry):
            step(pl.multiple_of(cidx * CH, CH), CH, False)
            return carry

        if n_t > FC:
            lax.fori_loop(0, n_full, full, 0)
        for nb in range(1, FC + 1):
            @pl.when(i % FC == nb - 1)
            def _(nb=nb):
                step(pl.multiple_of(n_full * CH, CH), nb * T, True)

        l = l_s[...]
        o_ref[...] = (acc_s[...] / l).astype(o_ref.dtype)
        lse_ref[...] = jnp.broadcast_to(m_s[...] + jnp.log(l), (T, 128))

    return _call_carrying(
        body, name, (H, Lp // T),
        [pl.BlockSpec((T, HEAD_PAD), lambda h, i: (i, h)), pl.BlockSpec((Lp, HEAD_PAD), lambda h, i: (0, h)),
         pl.BlockSpec((Lp, V_DIM), lambda h, i: (0, 2 * h + 1))],
        [pl.BlockSpec((T, V_DIM), lambda h, i: (i, h)), pl.BlockSpec((None, T, 128), lambda h, i: (h, i, 0))],
        [jax.ShapeDtypeStruct((Lp, H * V_DIM), BF), jax.ShapeDtypeStruct((H, Lp, 128), F32)],
        [pltpu.VMEM((T, 1), F32), pltpu.VMEM((T, 1), F32), pltpu.VMEM((T, V_DIM), F32)], (Q, K, kv), phases)


def _flash_bwd(Q, K, kv, O, dO, lse, H, name, phases=()):
    Lp = Q.shape[0]
    T = _div_tile(Lp, 256, 16)
    n_t = Lp // T
    FC = min(FLASH_BWD_CHUNK, n_t)

    def body(q_ref, k_ref, v_ref, o_ref, do_ref, lse_ref, dq_ref, dk_ref, dv_ref, dk_acc, dv_acc):
        j = pl.program_id(1)

        @pl.when(j == 0)
        def _():
            dq_ref[...] = jnp.zeros_like(dq_ref)

        kj, vj = k_ref[...], v_ref[...]
        dk_acc[...] = jnp.zeros_like(dk_acc)
        dv_acc[...] = jnp.zeros_like(dv_acc)

        def step(start, width, masked):
            rows = pl.ds(start, width)
            qi, doi = q_ref[rows, :], do_ref[rows, :]
            delta = jnp.sum(doi.astype(F32) * o_ref[rows, :].astype(F32), axis=-1, keepdims=True)
            s = lax.dot_general(qi, kj, _DIMS['nt'], preferred_element_type=F32)
            p = jnp.exp(s - lse_ref[rows, :][:, :1])
            if masked:
                rr = lax.broadcasted_iota(jnp.int32, (width, T), 0) + start
                cc = lax.broadcasted_iota(jnp.int32, (width, T), 1) + j * T
                p = jnp.where(cc <= rr, p, 0.0)
            dp = lax.dot_general(doi, vj, _DIMS['nt'], preferred_element_type=F32)
            ds = (p * (dp - delta)).astype(BF)
            dv_acc[...] += lax.dot_general(p.astype(BF), doi, _DIMS['tn'], preferred_element_type=F32)
            dk_acc[...] += lax.dot_general(ds, qi, _DIMS['tn'], preferred_element_type=F32)
            dq_ref[rows, :] += jnp.dot(ds, kj, preferred_element_type=F32)

        head = (n_t - 1 - j) % FC + 1
        for nb in range(1, FC + 1):
            @pl.when(head == nb)
            def _(nb=nb):
                step(pl.multiple_of(j * T, T), nb * T, True)

        def full(cidx, carry):
            step(pl.multiple_of((j + head + cidx * FC) * T, T), FC * T, False)
            return carry

        if n_t > FC:
            lax.fori_loop(0, (n_t - j - head) // FC, full, 0)
        dk_ref[...] = dk_acc[...]
        dv_ref[...] = dv_acc[...].astype(dv_ref.dtype)

    head_q = pl.BlockSpec((Lp, HEAD_PAD), lambda h, j: (0, h))
    head_v = pl.BlockSpec((Lp, V_DIM), lambda h, j: (0, h))
    return _call_carrying(
        body, name, (H, n_t),
        [head_q, pl.BlockSpec((T, HEAD_PAD), lambda h, j: (j, h)), pl.BlockSpec((T, V_DIM), lambda h, j: (j, 2 * h + 1)),
         head_v, head_v, pl.BlockSpec((None, Lp, 128), lambda h, j: (h, 0, 0))],
        [head_q, pl.BlockSpec((T, HEAD_PAD), lambda h, j: (j, h)), pl.BlockSpec((T, V_DIM), lambda h, j: (j, h))],
        [jax.ShapeDtypeStruct((Lp, H * HEAD_PAD), F32), jax.ShapeDtypeStruct((Lp, H * HEAD_PAD), F32),
         jax.ShapeDtypeStruct((Lp, H * V_DIM), BF)],
        [pltpu.VMEM((T, HEAD_PAD), F32), pltpu.VMEM((T, V_DIM), F32)], (Q, K, kv, O, dO, lse), phases)


def _gate_fwd(z, y_pool, y_mla, lay, name):
    Lp, D = y_pool.shape
    tm = _div_tile(Lp, 256, 16)

    def body(gp_ref, gm_ref, yp_ref, ym_ref, o_ref):
        o_ref[...] = (jax.nn.sigmoid(gp_ref[...]) * yp_ref[...] + jax.nn.sigmoid(gm_ref[...]) * ym_ref[...]).astype(BF)

    return pl.pallas_call(
        body, name=name, grid=(Lp // tm,),
        in_specs=[_rb(tm, D, lay['gp'] // D), _rb(tm, D, lay['gm'] // D), _rb(tm, D), _rb(tm, D)], out_specs=_rb(tm, D),
        out_shape=jax.ShapeDtypeStruct((Lp, D), BF), compiler_params=_cp(('parallel',)),
    )(z, z, y_pool, y_mla)


def _gate_bwd(dy, z, y_pool, y_mla, lay, name):
    Lp, D = y_pool.shape
    tm = _div_tile(Lp, 256, 16)

    def body(dy_ref, gp_ref, gm_ref, yp_ref, ym_ref, dp_ref, dm_ref, dg_ref):
        dy_ = dy_ref[...]
        sp, sm = jax.nn.sigmoid(gp_ref[...]), jax.nn.sigmoid(gm_ref[...])
        dp_ref[...] = (dy_ * sp).astype(BF)
        dm_ref[...] = (dy_ * sm).astype(BF)
        dg_ref[:, :D] = (dy_ * yp_ref[...] * (sp * (1.0 - sp))).astype(BF)
        dg_ref[:, D:] = (dy_ * ym_ref[...] * (sm * (1.0 - sm))).astype(BF)

    return pl.pallas_call(
        body, name=name, grid=(Lp // tm,),
        in_specs=[_rb(tm, D), _rb(tm, D, lay['gp'] // D), _rb(tm, D, lay['gm'] // D), _rb(tm, D), _rb(tm, D)],
        out_specs=[_rb(tm, D), _rb(tm, D), _rb(tm, 2 * D)],
        out_shape=[jax.ShapeDtypeStruct((Lp, D), BF), jax.ShapeDtypeStruct((Lp, D), BF),
                   jax.ShapeDtypeStruct((Lp, 2 * D), BF)],
        compiler_params=_cp(('parallel',)),
    )(dy, z, z, y_pool, y_mla)


def _z_layout(D, QL, KVL):
    cq = POOL_WIDTH
    ckv = cq + QL
    gp = -(-(ckv + KVL) // D) * D
    gm = gp + D
    kr = gm + D
    return dict(QL=QL, KVL=KVL, cq=cq, ckv=ckv, gp=gp, gm=gm, kr=kr, width=kr + 128)


def _w_in_aligned(wt, lay, D):
    n0 = POOL_WIDTH + lay['QL'] + lay['KVL']
    parts = [wt[:n0], jnp.zeros((lay['gp'] - n0, D), wt.dtype), wt[n0 + QK_ROPE:], wt[n0:n0 + QK_ROPE],
             jnp.zeros((128 - QK_ROPE, D), wt.dtype)]
    return jnp.concatenate(parts, axis=0)


def _w_in_logical(wt_al, lay, D):
    n0 = POOL_WIDTH + lay['QL'] + lay['KVL']
    return jnp.concatenate([wt_al[:n0], wt_al[lay['kr']:lay['kr'] + QK_ROPE], wt_al[lay['gp']:lay['gp'] + 2 * D]],
                           axis=0)


def kernel(x, meta_tokens, norm_ffn1_pre, norm_ffn1_post, ffn1_w_gu, ffn1_w_down, norm_mix_pre, norm_mix_post, w_in, pool_w, pool_scale, w_pool_o, q_a_norm, w_q_b, kv_a_norm, w_kv_b, w_mla_o, w_out, norm_ffn2_pre, norm_ffn2_post, ffn2_w_gu, ffn2_w_down, loss_target, m_meta_tokens, m_norm_ffn1_pre, m_norm_ffn1_post, m_ffn1_w_gu, m_ffn1_w_down, m_norm_mix_pre, m_norm_mix_post, m_w_in, m_pool_w, m_pool_scale, m_w_pool_o, m_q_a_norm, m_w_q_b, m_kv_a_norm, m_w_kv_b, m_w_mla_o, m_w_out, m_norm_ffn2_pre, m_norm_ffn2_post, m_ffn2_w_gu, m_ffn2_w_down, v_meta_tokens, v_norm_ffn1_pre, v_norm_ffn1_post, v_ffn1_w_gu, v_ffn1_w_down, v_norm_mix_pre, v_norm_mix_post, v_w_in, v_pool_w, v_pool_scale, v_w_pool_o, v_q_a_norm, v_w_q_b, v_kv_a_norm, v_w_kv_b, v_w_mla_o, v_w_out, v_norm_ffn2_pre, v_norm_ffn2_post, v_ffn2_w_gu, v_ffn2_w_down):
    given = dict(locals())
    W = {n: given[n] for n in WEIGHTS}
    M = {n: given['m_' + n] for n in WEIGHTS}
    V = {n: given['v_' + n] for n in WEIGHTS}

    S, D = x.shape[1], x.shape[2]
    NM = meta_tokens.shape[0]
    L = NM + S
    Lp = -(-L // ROW_ALIGN) * ROW_ALIGN
    QL, KVL = w_q_b.shape[1], w_kv_b.shape[1]
    H = w_q_b.shape[2] * N_CHIPS // QK_DIM
    lay = _z_layout(D, QL, KVL)
    gx, gy = lax.axis_index('x'), lax.axis_index('y')
    chip = 2 * gx + gy

    names = list(BIG)
    for d in (W, M, V):
        d['w_in'] = jnp.swapaxes(d['w_in'], 1, 2)
    slab = {n: W[n][0] for n in names}
    kind = dict(BIG, meta_tokens='col')
    slab_shape = {n: tuple(slab[n].shape) for n in names}
    gc = lax.axis_index('c')
    c_arr = jnp.stack([gc]).astype(jnp.int32)
    place_arr = jnp.stack([gc, chip]).astype(jnp.int32)

    full = {n: _cast_place(slab[n], BIG[n], place_arr, BF, f'place_{n}') for n in names}
    full['meta_tokens'] = _cast_place(meta_tokens, 'col', place_arr, F32, 'place_meta_tokens')
    G0 = ['ffn1_w_gu', 'meta_tokens']
    G0B = ['ffn1_w_down']
    G1W = ['w_in']
    G1R = ['pool_w', 'w_pool_o', 'w_q_b', 'w_kv_b']
    G2 = ['w_mla_o', 'w_out']
    G3 = ['ffn2_w_gu', 'ffn2_w_down']

    def gather(phase_fn, group):
        return phase_fn([full[n] for n in group], [kind[n] for n in group])

    def arrived(group, res):
        full.update(zip(group, res))

    arrived(G0, _all_gather([full[n] for n in G0], [kind[n] for n in G0]))
    meta_full = full['meta_tokens']

    pos = jnp.arange(Lp, dtype=F32)
    inv = ROPE_THETA ** (-jnp.arange(0, QK_ROPE, 2, dtype=F32) / QK_ROPE)
    ang = pos[:, None] * inv[None, :]
    ang = jnp.concatenate([ang, ang], axis=-1)
    cos = jnp.pad(jnp.cos(ang), ((0, 0), (0, 128 - QK_ROPE)), constant_values=1.0)
    sin = jnp.pad(jnp.sin(ang), ((0, 0), (0, 128 - QK_ROPE)))

    h0 = jnp.concatenate([meta_full, x[0], jnp.zeros((Lp - L, D), F32)], axis=0)
    tgt = jnp.pad(loss_target[0], ((NM, Lp - L), (0, 0)))

    n1 = _norm_fwd(h0, norm_ffn1_pre, 'ffn1_norm')
    gu1, (res_b, res) = _mm(n1, full['ffn1_w_gu'], 'nn', BF, 'ffn1_gu',
                            [gather(_phase_gather_ici, G0B), gather(_phase_gather_ici, G1R)])
    arrived(G0B, res_b)
    arrived(G1R, res)
    def gather_ici(group, peers):
        return _phase_gather_ici([full[n] for n in group], [kind[n] for n in group], peers)

    (a1,), (res_b, res_w) = _swiglu_fwd(gu1, 'ffn1_act', [gather(_phase_gather_d2d, G0B), gather_ici(G1W, (1,))])
    arrived(G0B, res_b)
    arrived(G1W, res_w)
    f1, (res, res_w) = _mm(a1, full['ffn1_w_down'], 'nn', F32, 'ffn1_down',
                           [gather(_phase_gather_d2d, G1R), gather_ici(G1W, (2, 3))])
    arrived(G1R, res)
    arrived(G1W, res_w)
    (h1, n2), (res_w,) = _post_residual(h0, f1, norm_ffn1_post, 0.5, norm_mix_pre, 'ffn1_res',
                                        [gather(_phase_gather_d2d, G1W)])
    arrived(G1W, res_w)

    w_in_al = _w_in_aligned(full['w_in'], lay, D)
    w_q_pad = jnp.pad(full['w_q_b'].reshape(QL, H, QK_DIM), ((0, 0), (0, 0), (0, HEAD_PAD - QK_DIM))).reshape(
        QL, H * HEAD_PAD)

    z, (res,) = _mm(n2, w_in_al, 'nt', F32, 'mix_in', [gather(_phase_gather_ici, G2)])
    arrived(G2, res)
    d_pool = _pool_fwd(z, 'pool_fwd')
    e_pool, yp = _pool_mix_fwd(d_pool, full['pool_w'], pool_scale, 'pool_mix')
    y_pool = _mm(yp, full['w_pool_o'], 'nn', F32, 'pool_out')
    cqn, ckvn = _lora_norms(z, q_a_norm, kv_a_norm, lay, 'lora_norms')
    q_raw = _mm(cqn, w_q_pad, 'nn', F32, 'mla_q')
    kv = _mm(ckvn, full['w_kv_b'], 'nn', BF, 'mla_kv')
    Q, K = _qk_prep(q_raw, kv, z, cos, sin, lay, H, 'qk_prep')
    (O, lse), (res2, res3) = _flash_fwd(Q, K, kv, H, 'flash_fwd',
                                        [gather(_phase_gather_d2d, G2), gather(_phase_gather_ici, G3)])
    arrived(G2, res2)
    arrived(G3, res3)
    y_mla, (res,) = _mm(O, full['w_mla_o'], 'nn', F32, 'mla_out', [gather(_phase_gather_d2d, G3)])
    arrived(G3, res)
    y = _gate_fwd(z, y_pool, y_mla, lay, 'gate')
    m_mix = _mm(y, full['w_out'], 'nn', F32, 'mix_out')
    (h2, n3), _ = _post_residual(h1, m_mix, norm_mix_post, 1.0, norm_ffn2_pre, 'mix_res')

    gu2 = _mm(n3, full['ffn2_w_gu'], 'nn', BF, 'ffn2_gu')
    (a2,), _ = _swiglu_fwd(gu2, 'ffn2_act')
    f2 = _mm(a2, full['ffn2_w_down'], 'nn', F32, 'ffn2_down')

    G, theirs, sums, halves, reduced = {}, {}, {}, {}, {}
    RA = ['ffn2_w_down', 'ffn2_w_gu']
    RB1 = ['w_out', 'w_pool_o', 'pool_w', 'w_mla_o']
    RB2 = ['w_q_b', 'w_kv_b']
    RB3 = ['w_in']
    RC1 = ['ffn1_w_down']
    RC2 = ['ffn1_w_gu']

    def pair_phase(group):
        return _phase_pair([G[n] for n in group], [BIG[n] for n in group], [slab_shape[n] for n in group])

    def pair_sums(group, res):
        for n, t in zip(group, res):
            sums[n] = _pair_sum(G[n], t, BIG[n], slab_shape[n], c_arr, f'rs_pair_sum_{n}')

    def chip_phase(group):
        return _phase_chip([sums[n] for n in group])

    def chip_sums(group, res):
        for n, ld in zip(group, res):
            halves[n] = _chip_sum(sums[n], ld, BIG[n], slab_shape[n], place_arr, f'rs_chip_sum_{n}')

    def final_phase(group):
        return _phase_final([halves[n] for n in group], [BIG[n] for n in group])

    dh3, df2, G['norm_ffn2_post'], sq = _tail(h2, f2, norm_ffn2_post, 0.5, tgt, NM, S, 'ffn2_tail')

    G['ffn2_w_down'] = _mm(a2, df2, 'tn', BF, 'ffn2_dw_down')
    da2 = _mm(df2, full['ffn2_w_down'], 'nt', BF, 'ffn2_da')
    dgu2 = _swiglu_bwd(da2, gu2, 'ffn2_act_bwd')
    G['ffn2_w_gu'] = _mm(n3, dgu2, 'tn', BF, 'ffn2_dw_gu')
    dn3, (res,) = _mm(dgu2, full['ffn2_w_gu'], 'nt', F32, 'ffn2_dn', [pair_phase(RA)])
    (dh2, G['norm_ffn2_pre'], dm, G['norm_mix_post']), _ = _pre_bwd(
        dh3, dn3, h2, norm_ffn2_pre, 'ffn2_norm_bwd', before=(m_mix, norm_mix_post, 1.0))
    pair_sums(RA, res)

    G['w_out'] = _mm(y, dm, 'tn', BF, 'dw_out')
    dy = _mm(dm, full['w_out'], 'nt', F32, 'mix_out_bwd')
    dy_pool, dy_mla, d_gate = _gate_bwd(dy, z, y_pool, y_mla, lay, 'gate_bwd')
    G['w_pool_o'] = _mm(yp, dy_pool, 'tn', BF, 'dw_pool_o')
    dyp = _mm(dy_pool, full['w_pool_o'], 'nt', F32, 'pool_out_bwd')
    dd, G['pool_scale'], d_pool_w = _pool_mix_bwd(dyp, e_pool, d_pool, full['pool_w'], pool_scale, 'pool_mix_bwd')
    G['pool_w'] = d_pool_w.astype(BF)
    du_pool = _pool_bwd(dd, 'pool_bwd')
    G['w_mla_o'] = _mm(O, dy_mla, 'tn', BF, 'dw_mla_o')
    dO, (res,) = _mm(dy_mla, full['w_mla_o'], 'nt', BF, 'mla_out_bwd', [pair_phase(RB1)])
    pair_sums(RB1, res)
    (dQ, dK, dV), (res, res_b) = _flash_bwd(Q, K, kv, O, dO, lse, H, 'flash_bwd', [chip_phase(RA), chip_phase(RB1)])
    chip_sums(RA, res)
    chip_sums(RB1, res_b)
    dq_raw, dkv, dkr = _qk_prep_bwd(dQ, dK, dV, cos, sin, H, 'qk_prep_bwd')
    d_w_q_pad, (res_b,) = _mm(cqn, dq_raw, 'tn', BF, 'dw_q_b', [final_phase(RB1)])
    reduced.update(zip(RB1, res_b))
    G['w_q_b'] = d_w_q_pad.reshape(QL, H, HEAD_PAD)[:, :, :QK_DIM].reshape(QL, H * QK_DIM)
    dcqn = _mm(dq_raw, w_q_pad, 'nt', F32, 'mla_q_bwd')
    G['w_kv_b'] = _mm(ckvn, dkv, 'tn', BF, 'dw_kv_b')
    dckvn, (res,) = _mm(dkv, full['w_kv_b'], 'nt', F32, 'mla_kv_bwd', [pair_phase(RB2)])
    pair_sums(RB2, res)
    d_lora, G['q_a_norm'], G['kv_a_norm'] = _lora_norms_bwd(dcqn, dckvn, z, q_a_norm, kv_a_norm, lay, 'lora_norms_bwd')
    n0 = POOL_WIDTH + QL + KVL
    dz = jnp.concatenate([du_pool, d_lora, jnp.zeros((Lp, lay['gp'] - n0), BF), d_gate, dkr], axis=1)
    d_w_in_al, (res, res_b) = _mm(dz, n2, 'tn', BF, 'dw_in', [final_phase(RA), chip_phase(RB2)])
    reduced.update(zip(RA, res))
    chip_sums(RB2, res_b)
    G['w_in'] = _w_in_logical(d_w_in_al, lay, D)
    dn2, (res, res_b) = _mm(dz, w_in_al, 'nn', F32, 'mix_in_bwd', [pair_phase(RB3), final_phase(RB2)])
    reduced.update(zip(RB2, res_b))
    (dh1, G['norm_mix_pre'], df1, G['norm_ffn1_post']), _ = _pre_bwd(
        dh2, dn2, h1, norm_mix_pre, 'mix_norm_bwd', before=(f1, norm_ffn1_post, 0.5))
    pair_sums(RB3, res)

    sums_b3 = [sums[n] for n in RB3]
    G['ffn1_w_down'], (land_b,) = _mm(a1, df1, 'tn', BF, 'ffn1_dw_down', [_phase_chip(sums_b3, (1, 2))])
    da1, (res, land_b) = _mm(df1, full['ffn1_w_down'], 'nt', BF, 'ffn1_da',
                             [pair_phase(RC1), _phase_chip(sums_b3, (3,), land_b)])
    chip_sums(RB3, land_b)
    dgu1 = _swiglu_bwd(da1, gu1, 'ffn1_act_bwd')
    pair_sums(RC1, res)
    G['ffn1_w_gu'], (res_c1, res_b) = _mm(n1, dgu1, 'tn', BF, 'ffn1_dw_gu', [chip_phase(RC1), final_phase(RB3)])
    chip_sums(RC1, res_c1)
    reduced.update(zip(RB3, res_b))
    (res,) = _run_phases([pair_phase(RC2)], 'rs_pair_exchange_tail')
    pair_sums(RC2, res)
    dn1, (res_c2, res_c1) = _mm(dgu1, full['ffn1_w_gu'], 'nt', F32, 'ffn1_dn', [chip_phase(RC2), final_phase(RC1)])
    chip_sums(RC2, res_c2)
    reduced.update(zip(RC1, res_c1))
    (dh0, G['norm_ffn1_pre']), (res_c2,) = _pre_bwd(dh1, dn1, h0, norm_ffn1_pre, 'ffn1_norm_bwd', [final_phase(RC2)])
    reduced.update(zip(RC2, res_c2))
    grad_x = dh0[NM:L][None]

    SW = max(D, POOL_WIDTH)

    def widen(a, fill=0.0):
        return jnp.pad(a, ((0, 0), (0, SW - a.shape[1])), constant_values=fill)

    rows = [widen(G[n]) for n in SMALL_VEC] + [widen(dh0[:NM]), widen(sq)]
    n_rows = len(SMALL_VEC) + NM + 1
    pad_rows = -(-n_rows // 8) * 8 - n_rows
    small = _all_reduce_small(jnp.concatenate(rows + [jnp.zeros((pad_rows, SW), F32)], axis=0))
    loss = (0.5 / D) * small[len(SMALL_VEC) + NM, 0]
    for i, n in enumerate(SMALL_VEC):
        reduced[n] = small[i:i + 1, :G[n].shape[1]]
    mw = meta_tokens.shape[1]
    reduced['meta_tokens'] = lax.dynamic_slice(small, (len(SMALL_VEC), chip * mw), (NM, mw))

    grads, deltas, new_m, new_v = {}, {}, {}, {}
    for n in names:
        grads[n], deltas[n], new_m[n], new_v[n] = _adam(W[n], reduced[n][None], M[n], V[n], f'adam_{n}')
    for d in (grads, deltas, new_m, new_v):
        d['w_in'] = jnp.swapaxes(d['w_in'], 1, 2)
    n_vec = len(SMALL_VEC)
    vec_w = jnp.concatenate([widen(W[n]) for n in SMALL_VEC] + [jnp.zeros((16 - n_vec, SW), F32)], axis=0)
    vec_m = jnp.concatenate([widen(M[n]) for n in SMALL_VEC] + [jnp.zeros((16 - n_vec, SW), F32)], axis=0)
    vec_v = jnp.concatenate([widen(V[n], 1.0) for n in SMALL_VEC] + [jnp.ones((16 - n_vec, SW), F32)], axis=0)
    vec_g = jnp.concatenate([small[:n_vec], jnp.zeros((16 - n_vec, SW), F32)], axis=0)
    _, vd, vm, vv = _adam(vec_w, vec_g, vec_m, vec_v, 'adam_vectors')
    for i, n in enumerate(SMALL_VEC):
        wdt = W[n].shape[1]
        grads[n], deltas[n], new_m[n], new_v[n] = reduced[n], vd[i:i + 1, :wdt], vm[i:i + 1, :wdt], vv[i:i + 1, :wdt]
    grads['meta_tokens'], deltas['meta_tokens'], new_m['meta_tokens'], new_v['meta_tokens'] = _adam(
        meta_tokens, reduced['meta_tokens'], m_meta_tokens, v_meta_tokens, 'adam_meta')

    return (loss, grad_x, *[grads[n] for n in WEIGHTS], *[deltas[n] for n in WEIGHTS], *[new_m[n] for n in WEIGHTS],
            *[new_v[n] for n in WEIGHTS])
```

```python
import functools

import jax
import jax.numpy as jnp
import numpy as np
from jax import lax
from jax.experimental import pallas as pl
from jax.experimental.pallas import tpu as pltpu

F32 = jnp.float32
BF = jnp.bfloat16
MESH = pl.DeviceIdType.MESH

EPS = 1e-6
N_CHIPS = 4
POOL_WINDOWS = (2, 4, 8, 16)
POOL_GROUP = 256
POOL_WIDTH = POOL_GROUP * len(POOL_WINDOWS)
QK_NOPE = 128
QK_ROPE = 64
V_DIM = 128
QK_DIM = QK_NOPE + QK_ROPE
HEAD_PAD = 256
ROPE_THETA = 10000.0
SOFTMAX_SCALE = QK_DIM ** -0.5
ADAM_LR = 0.001
ADAM_B1 = 0.9
ADAM_B2 = 0.999
ADAM_EPS = 1e-08
ADAM_WD = 0.01
ADAM_STEP = 10
ROW_ALIGN = 256
VMEM_LIMIT = 56 * 1024 * 1024
MM_TK_ROWS = 4352
MM_TK_COLS = 6400
NEG = -1e30
FLASH_FWD_CHUNK = 17
FLASH_BWD_CHUNK = 8
WEIGHTS = ['meta_tokens', 'norm_ffn1_pre', 'norm_ffn1_post', 'ffn1_w_gu', 'ffn1_w_down', 'norm_mix_pre',
           'norm_mix_post', 'w_in', 'pool_w', 'pool_scale', 'w_pool_o', 'q_a_norm', 'w_q_b', 'kv_a_norm', 'w_kv_b',
           'w_mla_o', 'w_out', 'norm_ffn2_pre', 'norm_ffn2_post', 'ffn2_w_gu', 'ffn2_w_down']
BIG = {'ffn1_w_gu': 'col', 'ffn1_w_down': 'row', 'w_in': 'row', 'pool_w': 'pool', 'w_pool_o': 'col', 'w_q_b': 'col',
       'w_kv_b': 'col', 'w_mla_o': 'row', 'w_out': 'row', 'ffn2_w_gu': 'col', 'ffn2_w_down': 'row'}
SMALL_VEC = ['norm_ffn1_pre', 'norm_ffn1_post', 'norm_mix_pre', 'norm_mix_post', 'norm_ffn2_pre', 'norm_ffn2_post',
             'pool_scale', 'q_a_norm', 'kv_a_norm']


def _div_tile(n, target, align):
    best = None
    for t in range(align, min(n, target) + 1, align):
        if n % t == 0:
            best = t
    return best if best is not None else n


def _cp(sem=None):
    return pltpu.CompilerParams(dimension_semantics=sem, vmem_limit_bytes=VMEM_LIMIT)


def _full_shape(kind, slab):
    if kind == 'col':
        return (slab[0], slab[1] * N_CHIPS)
    if kind == 'row':
        return (slab[0] * N_CHIPS, slab[1])
    return (slab[0], slab[1] * N_CHIPS, slab[2])


def _half_shape(kind, slab):
    if kind == 'pool':
        return (slab[0], slab[1] // 2, slab[2])
    if kind == 'row':
        return (slab[0], slab[1] // 2)
    return (slab[0] // 2, slab[1])


def _half_of_slab(ref, kind, c):
    if kind == 'pool':
        n = ref.shape[1] // 2
        return ref.at[:, pl.ds(c * n, n), :]
    if kind == 'row':
        n = ref.shape[1] // 2
        return ref.at[:, pl.ds(c * n, n)]
    n = ref.shape[0] // 2
    return ref.at[pl.ds(c * n, n), :]


def _piece(ref, kind, k, c):
    if kind == 'col':
        r, w = ref.shape[0] // 2, ref.shape[1] // N_CHIPS
        return ref.at[pl.ds(c * r, r), pl.ds(k * w, w)]
    if kind == 'row':
        r, w = ref.shape[0] // N_CHIPS, ref.shape[1] // 2
        return ref.at[pl.ds(k * r, r), pl.ds(c * w, w)]
    r = ref.shape[1] // (2 * N_CHIPS)
    return ref.at[:, pl.ds((2 * k + c) * r, r), :]


def _place():
    x, y, c = lax.axis_index('x'), lax.axis_index('y'), lax.axis_index('c')
    return x, y, c


def _peer_chip(x, y, j):
    px = 1 - x if (j >> 1) else x
    py = 1 - y if (j & 1) else y
    return px, py


ANY = pl.BlockSpec(memory_space=pl.ANY)


def _all_gather(bufs, kinds):
    n = len(bufs)

    def body(*refs):
        outs = refs[n:2 * n]
        ssem, rsem, fssem, frsem = refs[2 * n:]
        x, y, c = _place()
        me = 2 * x + y
        sib = (x, y, 1 - c)
        sends = []
        for w in range(n):
            for j in (1, 2, 3):
                px, py = _peer_chip(x, y, j)
                mine = _piece(outs[w], kinds[w], me, c)
                sends.append(pltpu.make_async_remote_copy(
                    src_ref=mine, dst_ref=mine, send_sem=ssem.at[w, j - 1], recv_sem=rsem.at[w, j - 1],
                    device_id=(px, py, c), device_id_type=MESH))
        for cp in sends:
            cp.start()
        fwds = []
        for w in range(n):
            for j in (1, 2, 3):
                px, py = _peer_chip(x, y, j)
                got = _piece(outs[w], kinds[w], 2 * px + py, c)
                pltpu.make_async_remote_copy(src_ref=got, dst_ref=got, send_sem=ssem.at[w, j - 1],
                                             recv_sem=rsem.at[w, j - 1], device_id=(px, py, c),
                                             device_id_type=MESH).wait_recv()
                fwd = pltpu.make_async_remote_copy(src_ref=got, dst_ref=got, send_sem=fssem.at[w, j - 1],
                                                   recv_sem=frsem.at[w, j - 1], device_id=sib, device_id_type=MESH)
                fwd.start()
                fwds.append(fwd)
        for w in range(n):
            for j in (1, 2, 3):
                px, py = _peer_chip(x, y, j)
                other = _piece(outs[w], kinds[w], 2 * px + py, 1 - c)
                pltpu.make_async_remote_copy(src_ref=other, dst_ref=other, send_sem=fssem.at[w, j - 1],
                                             recv_sem=frsem.at[w, j - 1], device_id=sib,
                                             device_id_type=MESH).wait_recv()
        for cp in sends + fwds:
            cp.wait_send()

    return pl.pallas_call(
        body, name='all_gather_weights', out_shape=[jax.ShapeDtypeStruct(b.shape, b.dtype) for b in bufs],
        in_specs=[ANY] * n, out_specs=[ANY] * n, input_output_aliases={w: w for w in range(n)},
        scratch_shapes=[pltpu.SemaphoreType.DMA((n, 3)), pltpu.SemaphoreType.DMA((n, 3)),
                        pltpu.SemaphoreType.DMA((n, 3)), pltpu.SemaphoreType.DMA((n, 3))],
    )(*bufs)


def _remote(src, dst, ssem, rsem, k, dev):
    return pltpu.make_async_remote_copy(src_ref=src, dst_ref=dst, send_sem=ssem.at[k], recv_sem=rsem.at[k],
                                        device_id=dev, device_id_type=MESH)


def _phase_gather_ici(bufs, kinds, peers=(1, 2, 3)):
    n, m = len(bufs), len(peers)

    def build(ins, outs, ssem, rsem):
        x, y, c = _place()
        me = 2 * x + y
        ds = []
        for w in range(n):
            for q, j in enumerate(peers):
                px, py = _peer_chip(x, y, j)
                mine = _piece(outs[w], kinds[w], me, c)
                got = _piece(outs[w], kinds[w], 2 * px + py, c)
                k = m * w + q
                ds.append((_remote(mine, mine, ssem, rsem, k, (px, py, c)), _remote(got, got, ssem, rsem, k, (px, py, c))))
        return ds

    return dict(ins=list(bufs), outs=[jax.ShapeDtypeStruct(b.shape, b.dtype) for b in bufs],
                alias={w: w for w in range(n)}, nsem=m * n, build=build)


def _phase_gather_d2d(bufs, kinds):
    n = len(bufs)

    def build(ins, outs, ssem, rsem):
        x, y, c = _place()
        sib = (x, y, 1 - c)
        ds = []
        for w in range(n):
            for j in (1, 2, 3):
                px, py = _peer_chip(x, y, j)
                have = _piece(outs[w], kinds[w], 2 * px + py, c)
                want = _piece(outs[w], kinds[w], 2 * px + py, 1 - c)
                k = 3 * w + j - 1
                ds.append((_remote(have, have, ssem, rsem, k, sib), _remote(want, want, ssem, rsem, k, sib)))
        return ds

    return dict(ins=list(bufs), outs=[jax.ShapeDtypeStruct(b.shape, b.dtype) for b in bufs],
                alias={w: w for w in range(n)}, nsem=3 * n, build=build)


def _phase_pair(grads, kinds, slabs):
    n = len(grads)

    def build(ins, outs, ssem, rsem):
        x, y, c = _place()
        sib = (x, y, 1 - c)
        ds = []
        for w in range(n):
            for k in range(N_CHIPS):
                cp = _remote(_piece(ins[w], kinds[w], k, 1 - c), outs[w].at[k], ssem, rsem, N_CHIPS * w + k, sib)
                ds.append((cp, cp))
        return ds

    return dict(ins=list(grads), alias={}, nsem=N_CHIPS * n, build=build,
                outs=[jax.ShapeDtypeStruct((N_CHIPS,) + _half_shape(k, s), g.dtype)
                      for g, k, s in zip(grads, kinds, slabs)])


def _phase_chip(sums, peers=(1, 2, 3), landing=None):
    n, m = len(sums), len(peers)

    def build(ins, outs, ssem, rsem):
        x, y, c = _place()
        ds = []
        for w in range(n):
            for q, j in enumerate(peers):
                px, py = _peer_chip(x, y, j)
                cp = _remote(ins[w].at[2 * px + py], outs[w].at[j - 1], ssem, rsem, m * w + q, (px, py, c))
                ds.append((cp, cp))
        return ds

    return dict(ins=list(sums) + (list(landing) if landing is not None else []),
                alias={n + w: w for w in range(n)} if landing is not None else {}, nsem=m * n, build=build,
                outs=[jax.ShapeDtypeStruct((3,) + s.shape[1:], s.dtype) for s in sums])


def _phase_final(slabs_half, kinds):
    n = len(slabs_half)

    def build(ins, outs, ssem, rsem):
        x, y, c = _place()
        sib = (x, y, 1 - c)
        ds = []
        for w in range(n):
            mine = _half_of_slab(outs[w], kinds[w], c)
            other = _half_of_slab(outs[w], kinds[w], 1 - c)
            ds.append((_remote(mine, mine, ssem, rsem, w, sib), _remote(other, other, ssem, rsem, w, sib)))
        return ds

    return dict(ins=list(slabs_half), outs=[jax.ShapeDtypeStruct(s.shape, s.dtype) for s in slabs_half],
                alias={w: w for w in range(n)}, nsem=n, build=build)


def _phase_operands(phases, n_main_in, n_main_out):
    ins, outs, alias, sems = [], [], {}, []
    for ph in phases:
        for i, o in ph['alias'].items():
            alias[n_main_in + len(ins) + i] = n_main_out + len(outs) + o
        ins += ph['ins']
        outs += ph['outs']
        sems += [pltpu.SemaphoreType.DMA((ph['nsem'],)), pltpu.SemaphoreType.DMA((ph['nsem'],))]
    return ins, outs, alias, sems


def _phase_copies(phases, in_refs, out_refs, sem_refs):
    ds, a, b = [], 0, 0
    for p, ph in enumerate(phases):
        ds += ph['build'](in_refs[a:a + len(ph['ins'])], out_refs[b:b + len(ph['outs'])], sem_refs[2 * p],
                          sem_refs[2 * p + 1])
        a += len(ph['ins'])
        b += len(ph['outs'])
    return ds


def _phase_start(phases, in_refs, out_refs, sem_refs):
    for send, _ in _phase_copies(phases, in_refs, out_refs, sem_refs):
        send.start()


def _phase_finish(phases, in_refs, out_refs, sem_refs):
    ds = _phase_copies(phases, in_refs, out_refs, sem_refs)
    for _, recv in ds:
        recv.wait_recv()
    for send, _ in ds:
        send.wait_send()


def _phase_results(phases, outs):
    res, b = [], 0
    for ph in phases:
        res.append(list(outs[b:b + len(ph['outs'])]))
        b += len(ph['outs'])
    return res


def _run_phases(phases, name):
    ins, out_shapes, alias, sems = _phase_operands(phases, 0, 0)
    n_in, n_out = len(ins), len(out_shapes)

    def body(*refs):
        in_refs, out_refs, sem_refs = refs[:n_in], refs[n_in:n_in + n_out], refs[n_in + n_out:]
        _phase_start(phases, in_refs, out_refs, sem_refs)
        _phase_finish(phases, in_refs, out_refs, sem_refs)

    outs = pl.pallas_call(body, name=name, out_shape=out_shapes, in_specs=[ANY] * n_in, out_specs=[ANY] * n_out,
                          input_output_aliases=alias, scratch_shapes=sems)(*ins)
    return _phase_results(phases, outs)


def _cast_place(shard, kind, place_arr, dtype, name):
    full = _full_shape(kind, shard.shape)
    if kind == 'pool':
        g, r, w = shard.shape
        grid = (g,)
        src = pl.BlockSpec((None, r, w), lambda i, s: (i, 0, 0))
        dst = pl.BlockSpec((None, r, w), lambda i, s: (i, s[1], 0))
    else:
        r, w = shard.shape
        tr, tc = _tile2(r, w)
        nrb, ncb = r // tr, w // tc
        grid = (nrb, ncb)
        src = pl.BlockSpec((tr, tc), lambda i, j, s: (i, j))
        if kind == 'col':
            dst = pl.BlockSpec((tr, tc), lambda i, j, s: (i, s[1] * ncb + j))
        else:
            dst = pl.BlockSpec((tr, tc), lambda i, j, s: (s[1] * nrb + i, j))

    def body(s_ref, x_ref, o_ref):
        o_ref[...] = x_ref[...].astype(o_ref.dtype)

    return pl.pallas_call(
        body, name=name,
        grid_spec=pltpu.PrefetchScalarGridSpec(num_scalar_prefetch=1, grid=grid, in_specs=[src], out_specs=dst),
        out_shape=jax.ShapeDtypeStruct(full, dtype), compiler_params=_cp(('parallel',) * len(grid)),
    )(place_arr, shard)


def _tile2(r, c):
    tr = _div_tile(r, max(16, (1 << 19) // c), 16)
    if tr == r or tr * c >= (1 << 17):
        return tr, c
    return r, _div_tile(c, max(128, (1 << 19) // r), 128)


def _pair_sum(grad, theirs, kind, slab, c_arr, name):
    hs = _half_shape(kind, slab)
    if kind == 'pool':
        grid = (N_CHIPS, hs[0], 1)
        own = pl.BlockSpec((None, hs[1], hs[2]), lambda k, i, j, s: (i, 2 * k + s[0], 0))
        stk = pl.BlockSpec((None, None, hs[1], hs[2]), lambda k, i, j, s: (k, i, 0, 0))
    else:
        tr, tc = _tile2(*hs)
        nrb, ncb = hs[0] // tr, hs[1] // tc
        grid = (N_CHIPS, nrb, ncb)
        if kind == 'col':
            own = pl.BlockSpec((tr, tc), lambda k, i, j, s: (s[0] * nrb + i, k * ncb + j))
        else:
            own = pl.BlockSpec((tr, tc), lambda k, i, j, s: (k * nrb + i, s[0] * ncb + j))
        stk = pl.BlockSpec((None, tr, tc), lambda k, i, j, s: (k, i, j))

    def body(s_ref, a_ref, b_ref, o_ref):
        o_ref[...] = (a_ref[...].astype(F32) + b_ref[...].astype(F32)).astype(o_ref.dtype)

    return pl.pallas_call(
        body, name=name,
        grid_spec=pltpu.PrefetchScalarGridSpec(num_scalar_prefetch=1, grid=grid, in_specs=[own, stk], out_specs=stk),
        out_shape=jax.ShapeDtypeStruct((N_CHIPS,) + hs, BF), compiler_params=_cp(('parallel',) * 3),
    )(c_arr, grad, theirs)


def _chip_sum(sums, landed, kind, slab, place_arr, name):
    hs = _half_shape(kind, slab)
    if kind == 'pool':
        grid = (hs[0], 1)
        blk = (None, None, hs[1], hs[2])
        mine = pl.BlockSpec(blk, lambda i, j, s: (s[1], i, 0, 0))
        land = [pl.BlockSpec(blk, lambda i, j, s, p=p: (p, i, 0, 0)) for p in range(3)]
        out = pl.BlockSpec((None, hs[1], hs[2]), lambda i, j, s: (i, s[0], 0))
    else:
        tr, tc = _tile2(*hs)
        nrb, ncb = hs[0] // tr, hs[1] // tc
        grid = (nrb, ncb)
        blk = (None, tr, tc)
        mine = pl.BlockSpec(blk, lambda i, j, s: (s[1], i, j))
        land = [pl.BlockSpec(blk, lambda i, j, s, p=p: (p, i, j)) for p in range(3)]
        if kind == 'col':
            out = pl.BlockSpec((tr, tc), lambda i, j, s: (s[0] * nrb + i, j))
        else:
            out = pl.BlockSpec((tr, tc), lambda i, j, s: (i, s[0] * ncb + j))

    def body(s_ref, a_ref, b_ref, c_ref, d_ref, o_ref):
        o_ref[...] = ((a_ref[...].astype(F32) + b_ref[...].astype(F32)) + c_ref[...].astype(F32)) + d_ref[...].astype(F32)

    return pl.pallas_call(
        body, name=name,
        grid_spec=pltpu.PrefetchScalarGridSpec(num_scalar_prefetch=1, grid=grid, in_specs=[mine] + land, out_specs=out),
        out_shape=jax.ShapeDtypeStruct(tuple(slab), F32), compiler_params=_cp(('parallel',) * 2),
    )(place_arr, sums, landed, landed, landed)


def _all_reduce_small(buf):
    rows, cols = buf.shape

    def body(in_ref, out_ref, land, ssem, rsem):
        x, y, c = _place()
        me = 4 * x + 2 * y + c
        land[me] = in_ref[...]
        started = []
        for j in range(1, 8):
            px = 1 - x if (j >> 2) & 1 else x
            py = 1 - y if (j >> 1) & 1 else y
            pc = 1 - c if j & 1 else c
            cp = pltpu.make_async_remote_copy(src_ref=in_ref, dst_ref=land.at[me], send_sem=ssem.at[j - 1],
                                              recv_sem=rsem.at[j - 1], device_id=(px, py, pc), device_id_type=MESH)
            cp.start()
            started.append(cp)
        for j in range(1, 8):
            px = 1 - x if (j >> 2) & 1 else x
            py = 1 - y if (j >> 1) & 1 else y
            pc = 1 - c if j & 1 else c
            slot = land.at[4 * px + 2 * py + pc]
            pltpu.make_async_remote_copy(src_ref=slot, dst_ref=slot, send_sem=ssem.at[j - 1], recv_sem=rsem.at[j - 1],
                                         device_id=(px, py, pc), device_id_type=MESH).wait_recv()
        for cp in started:
            cp.wait_send()
        acc = land[0]
        for d in range(1, 8):
            acc = acc + land[d]
        out_ref[...] = acc

    return pl.pallas_call(
        body, name='all_reduce_small', out_shape=jax.ShapeDtypeStruct((rows, cols), F32),
        in_specs=[pl.BlockSpec(memory_space=pltpu.VMEM)], out_specs=pl.BlockSpec(memory_space=pltpu.VMEM),
        scratch_shapes=[pltpu.VMEM((8, rows, cols), F32), pltpu.SemaphoreType.DMA((7,)), pltpu.SemaphoreType.DMA((7,))],
    )(buf)


def _elementwise(fn, ins, lead_index, out_shape, out_dtypes, name):
    nd = len(out_shape)
    r, cdim = out_shape[-2], out_shape[-1]
    tr, tc = _tile2(r, cdim)
    grid = tuple(out_shape[:-2]) + (r // tr, cdim // tc)
    block = (None,) * (nd - 2) + (tr, tc)

    def spec(lead):
        if lead is None:
            return pl.BlockSpec(block, lambda *g: tuple(g))
        return pl.BlockSpec((None,) + block, lambda *g, lead=lead: (lead,) + tuple(g))

    n_in = len(ins)

    def body(*refs):
        res = fn(*[r_[...] for r_ in refs[:n_in]])
        for o_ref, v in zip(refs[n_in:], res):
            o_ref[...] = v.astype(o_ref.dtype)

    return pl.pallas_call(
        body, name=name, grid=grid, in_specs=[spec(l) for l in lead_index],
        out_specs=[spec(None) for _ in out_dtypes],
        out_shape=[jax.ShapeDtypeStruct(tuple(out_shape), dt) for dt in out_dtypes],
        compiler_params=_cp(('parallel',) * len(grid)),
    )(*ins)


def _adam_fn(w, g, m, v):
    m = ADAM_B1 * m + (1.0 - ADAM_B1) * g
    v = ADAM_B2 * v + (1.0 - ADAM_B2) * (g * g)
    m_hat = m / (1.0 - ADAM_B1 ** ADAM_STEP)
    v_hat = v / (1.0 - ADAM_B2 ** ADAM_STEP)
    delta = -ADAM_LR * (m_hat / (jnp.sqrt(v_hat) + ADAM_EPS) + ADAM_WD * w)
    return g, delta, m, v


def _adam(w, g, m, v, name):
    return _elementwise(_adam_fn, [w, g, m, v], [None] * 4, w.shape, [F32] * 4, name)


_DIMS = {'nn': (((1,), (0,)), ((), ())), 'nt': (((1,), (1,)), ((), ())), 'tn': (((0,), (0,)), ((), ()))}


def _mm(a, b, mode, out_dtype, name, phases=()):
    if mode == 'nn':
        (M, K), N = a.shape, b.shape[1]
    elif mode == 'nt':
        (M, K), N = a.shape, b.shape[0]
    else:
        (K, M), N = a.shape, b.shape[1]
    if mode == 'tn':
        tm, tn, tk = _div_tile(M, 512, 128), _div_tile(N, 1024, 128), _div_tile(K, MM_TK_ROWS, 16)
        if tm < 256:
            tm = _div_tile(M, 1024, 128)
    else:
        tk = _div_tile(K, MM_TK_COLS, 128)
        tm, tn = _div_tile(M, 1088, 16), _div_tile(N, 1024 if tk <= 2816 else 512, 128)
    nk = K // tk
    a_spec = {'nn': pl.BlockSpec((tm, tk), lambda i, j, k: (i, k)), 'nt': pl.BlockSpec((tm, tk), lambda i, j, k: (i, k)),
              'tn': pl.BlockSpec((tk, tm), lambda i, j, k: (k, i))}[mode]
    b_spec = {'nn': pl.BlockSpec((tk, tn), lambda i, j, k: (k, j)), 'nt': pl.BlockSpec((tn, tk), lambda i, j, k: (j, k)),
              'tn': pl.BlockSpec((tk, tn), lambda i, j, k: (k, j))}[mode]
    dims = _DIMS[mode]
    gm, gn = M // tm, N // tn
    extra_in, extra_out, alias, sems = _phase_operands(phases, 2, 1)
    n_ei, n_eo = len(extra_in), len(extra_out)

    def body(*refs):
        a_ref, b_ref, ein = refs[0], refs[1], refs[2:2 + n_ei]
        o_ref, eout = refs[2 + n_ei], refs[3 + n_ei:3 + n_ei + n_eo]
        acc_ref, sem_refs = refs[3 + n_ei + n_eo], refs[4 + n_ei + n_eo:]
        i, j, k = pl.program_id(0), pl.program_id(1), pl.program_id(2)
        if phases:
            @pl.when((i == 0) & (j == 0) & (k == 0))
            def _():
                _phase_start(phases, ein, eout, sem_refs)

        part = lax.dot_general(a_ref[...], b_ref[...], dims, preferred_element_type=F32)
        if nk == 1:
            o_ref[...] = part.astype(o_ref.dtype)
        else:
            @pl.when(k == 0)
            def _():
                acc_ref[...] = part

            @pl.when(k > 0)
            def _():
                acc_ref[...] += part

            @pl.when(k == nk - 1)
            def _():
                o_ref[...] = acc_ref[...].astype(o_ref.dtype)

        if phases:
            @pl.when((i == gm - 1) & (j == gn - 1) & (k == nk - 1))
            def _():
                _phase_finish(phases, ein, eout, sem_refs)

    outs = pl.pallas_call(
        body, name=name, grid=(gm, gn, nk), in_specs=[a_spec, b_spec] + [ANY] * n_ei,
        out_specs=[pl.BlockSpec((tm, tn), lambda i, j, k: (i, j))] + [ANY] * n_eo,
        out_shape=[jax.ShapeDtypeStruct((M, N), out_dtype)] + extra_out, input_output_aliases=alias,
        scratch_shapes=[pltpu.VMEM((tm, tn) if nk > 1 else (8, 128), F32)] + sems,
        compiler_params=_cp(('arbitrary',) * 3 if phases else ('parallel', 'parallel', 'arbitrary')),
    )(a, b, *extra_in)
    if phases:
        return outs[0], _phase_results(phases, outs[1:])
    return outs[0]


def _rb(tm, w, col=0):
    return pl.BlockSpec((tm, w), lambda i, col=col: (i, col))


def _fixed(shape):
    return pl.BlockSpec(shape, lambda i: (0,) * len(shape))


def _rms(x):
    return lax.rsqrt(jnp.mean(x * x, axis=-1, keepdims=True) + EPS)


def _norm_fwd(x, gain, name, width=None, col=0):
    Lp = x.shape[0]
    width = width or x.shape[1]
    tm = _div_tile(Lp, 256, 16)

    def body(x_ref, g_ref, o_ref):
        v = x_ref[...]
        o_ref[...] = (v * _rms(v) * g_ref[...]).astype(o_ref.dtype)

    return pl.pallas_call(
        body, name=name, grid=(Lp // tm,), in_specs=[_rb(tm, width, col), _fixed((1, width))], out_specs=_rb(tm, width),
        out_shape=jax.ShapeDtypeStruct((Lp, width), BF), compiler_params=_cp(('parallel',)),
    )(x, gain)


def _post_residual(h, f, gain, scale, next_gain, name, phases=()):
    Lp, D = h.shape
    tm = _div_tile(Lp, 256, 16)

    def body(h_ref, f_ref, g_ref, ng_ref, o_ref, n_ref):
        v = f_ref[...]
        out = h_ref[...] + scale * (v * _rms(v) * g_ref[...])
        o_ref[...] = out
        n_ref[...] = (out * _rms(out) * ng_ref[...]).astype(n_ref.dtype)

    return _call_carrying(
        body, name, (Lp // tm,), [_rb(tm, D), _rb(tm, D), _fixed((1, D)), _fixed((1, D))], [_rb(tm, D), _rb(tm, D)],
        [jax.ShapeDtypeStruct((Lp, D), F32), jax.ShapeDtypeStruct((Lp, D), BF)], [], (h, f, gain, next_gain), phases)


def _pre_bwd(dres, dn, h, gain, name, phases=(), before=None):
    Lp, D = h.shape
    tm = _div_tile(Lp, 256, 16)

    def accumulate(ref, part):
        @pl.when(pl.program_id(0) == 0)
        def _():
            ref[...] = part

        @pl.when(pl.program_id(0) > 0)
        def _():
            ref[...] += part

    def body(dres_ref, dn_ref, h_ref, g_ref, *rest):
        v = h_ref[...]
        r = _rms(v)
        dy = dn_ref[...]
        w = dy * g_ref[...]
        dh = dres_ref[...] + r * w - v * (r * r * r) * jnp.mean(w * v, axis=-1, keepdims=True)
        if before is None:
            dh_ref, dg_ref = rest
        else:
            f_ref, pg_ref, dh_ref, dg_ref, df_ref, dpg_ref = rest
            fv = f_ref[...]
            fr = _rms(fv)
            fdy = before[2] * dh
            fw = fdy * pg_ref[...]
            df_ref[...] = (fr * fw - fv * (fr * fr * fr) * jnp.mean(fw * fv, axis=-1, keepdims=True)).astype(BF)
            accumulate(dpg_ref, jnp.sum(fdy * fv * fr, axis=0, keepdims=True))
        dh_ref[...] = dh
        accumulate(dg_ref, jnp.sum(dy * v * r, axis=0, keepdims=True))

    in_specs = [_rb(tm, D), _rb(tm, D), _rb(tm, D), _fixed((1, D))]
    out_specs = [_rb(tm, D), _fixed((1, D))]
    out_shape = [jax.ShapeDtypeStruct((Lp, D), F32), jax.ShapeDtypeStruct((1, D), F32)]
    args = (dres, dn, h, gain)
    if before is not None:
        in_specs += [_rb(tm, D), _fixed((1, D))]
        out_specs += [_rb(tm, D), _fixed((1, D))]
        out_shape += [jax.ShapeDtypeStruct((Lp, D), BF), jax.ShapeDtypeStruct((1, D), F32)]
        args += (before[0], before[1])
    return _call_carrying(body, name, (Lp // tm,), in_specs, out_specs, out_shape, [], args, phases)


def _swiglu_fwd(gu, name, phases=()):
    Lp, F2 = gu.shape
    F = F2 // 2
    tm = _div_tile(Lp, 256, 16)

    def body(gu_ref, a_ref):
        g = gu_ref[:, :F].astype(F32)
        u = gu_ref[:, F:].astype(F32)
        a_ref[...] = (g * jax.nn.sigmoid(g) * u).astype(a_ref.dtype)

    return _call_carrying(body, name, (Lp // tm,), [_rb(tm, F2)], [_rb(tm, F)],
                          [jax.ShapeDtypeStruct((Lp, F), BF)], [], (gu,), phases)


def _swiglu_bwd(da, gu, name):
    Lp, F2 = gu.shape
    F = F2 // 2
    tm = _div_tile(Lp, 256, 16)

    def body(da_ref, gu_ref, o_ref):
        g = gu_ref[:, :F].astype(F32)
        u = gu_ref[:, F:].astype(F32)
        da_ = da_ref[...].astype(F32)
        s = jax.nn.sigmoid(g)
        o_ref[:, :F] = (da_ * u * (s * (1.0 + g * (1.0 - s)))).astype(o_ref.dtype)
        o_ref[:, F:] = (da_ * (g * s)).astype(o_ref.dtype)

    return pl.pallas_call(
        body, name=name, grid=(Lp // tm,), in_specs=[_rb(tm, F), _rb(tm, F2)], out_specs=_rb(tm, F2),
        out_shape=jax.ShapeDtypeStruct((Lp, F2), BF), compiler_params=_cp(('parallel',)),
    )(da, gu)


def _tail(h, f, gain, scale, tgt, n_meta, n_real, name):
    Lp, D = h.shape
    tm = _div_tile(Lp, 256, 16)

    def body(h_ref, f_ref, g_ref, t_ref, d_ref, df_ref, dg_ref, l_ref):
        v = f_ref[...]
        r = _rms(v)
        gain_ = g_ref[...]
        row = lax.broadcasted_iota(jnp.int32, (tm, 1), 0) + pl.program_id(0) * tm
        ok = (row >= n_meta) & (row < n_meta + n_real)
        err = jnp.where(ok, h_ref[...] + scale * (v * r * gain_) - t_ref[...], 0.0)
        d_out = err / D
        d_ref[...] = d_out
        dy = scale * d_out
        w = dy * gain_
        df_ref[...] = (r * w - v * (r * r * r) * jnp.mean(w * v, axis=-1, keepdims=True)).astype(df_ref.dtype)
        dg_part = jnp.sum(dy * v * r, axis=0, keepdims=True)
        l_part = jnp.full((1, 128), jnp.sum(err * err), F32)

        @pl.when(pl.program_id(0) == 0)
        def _():
            dg_ref[...] = dg_part
            l_ref[...] = l_part

        @pl.when(pl.program_id(0) > 0)
        def _():
            dg_ref[...] += dg_part
            l_ref[...] += l_part

    return pl.pallas_call(
        body, name=name, grid=(Lp // tm,), in_specs=[_rb(tm, D), _rb(tm, D), _fixed((1, D)), _rb(tm, D)],
        out_specs=[_rb(tm, D), _rb(tm, D), _fixed((1, D)), _fixed((1, 128))],
        out_shape=[jax.ShapeDtypeStruct((Lp, D), F32), jax.ShapeDtypeStruct((Lp, D), BF),
                   jax.ShapeDtypeStruct((1, D), F32), jax.ShapeDtypeStruct((1, 128), F32)],
        compiler_params=_cp(('arbitrary',)),
    )(h, f, gain, tgt)


def _split_bf16(v):
    hi = v.astype(BF)
    return hi, (v - hi.astype(F32)).astype(BF)


def _pool_fwd(z, name):
    Lp = z.shape[0]
    T = _div_tile(Lp, 256, 16)
    G = len(POOL_WINDOWS)

    def body(cur_ref, prev_ref, d_ref):
        i, g = pl.program_id(0), pl.program_id(1)
        w = jnp.left_shift(2, g)
        rr = lax.broadcasted_iota(jnp.int32, (T, T), 0)
        cc = lax.broadcasted_iota(jnp.int32, (T, T), 1)
        b_cur = jnp.where((cc <= rr) & (cc > rr - w), 1.0, 0.0).astype(BF)
        w_prev = jnp.where(i > 0, w, 0)
        b_prev = jnp.where(cc - T > rr - w_prev, 1.0, 0.0).astype(BF)
        u = cur_ref[...]
        s = jnp.zeros((T, POOL_GROUP), F32)
        for part in _split_bf16(u):
            s += jnp.dot(b_cur, part, preferred_element_type=F32)
        for part in _split_bf16(prev_ref[...]):
            s += jnp.dot(b_prev, part, preferred_element_type=F32)
        t = lax.broadcasted_iota(jnp.int32, (T, 1), 0) + i * T
        cnt = jnp.minimum(w, t + 1).astype(F32)
        d_ref[...] = (s / cnt - u).astype(d_ref.dtype)

    return pl.pallas_call(
        body, name=name, grid=(Lp // T, G),
        in_specs=[pl.BlockSpec((T, POOL_GROUP), lambda i, g: (i, g)),
                  pl.BlockSpec((T, POOL_GROUP), lambda i, g: (jnp.maximum(i - 1, 0), g))],
        out_specs=pl.BlockSpec((T, POOL_GROUP), lambda i, g: (i, g)),
        out_shape=jax.ShapeDtypeStruct((Lp, POOL_WIDTH), BF), compiler_params=_cp(('parallel', 'parallel')),
    )(z, z)


def _pool_bwd(dd, name):
    Lp = dd.shape[0]
    T = _div_tile(Lp, 256, 16)
    G = len(POOL_WINDOWS)
    n_t = Lp // T

    def body(cur_ref, next_ref, o_ref):
        i, g = pl.program_id(0), pl.program_id(1)
        w = jnp.left_shift(2, g)
        rr = lax.broadcasted_iota(jnp.int32, (T, T), 0)
        cc = lax.broadcasted_iota(jnp.int32, (T, T), 1)
        b_cur = jnp.where((cc >= rr) & (cc < rr + w), 1.0, 0.0).astype(BF)
        w_next = jnp.where(i < n_t - 1, w, 0)
        b_next = jnp.where(cc + T < rr + w_next, 1.0, 0.0).astype(BF)
        t = lax.broadcasted_iota(jnp.int32, (T, 1), 0) + i * T
        cur = cur_ref[...]
        e_cur = cur / jnp.minimum(w, t + 1).astype(F32)
        e_next = next_ref[...] / jnp.minimum(w, t + T + 1).astype(F32)
        s = jnp.zeros((T, POOL_GROUP), F32)
        for part in _split_bf16(e_cur):
            s += jnp.dot(b_cur, part, preferred_element_type=F32)
        for part in _split_bf16(e_next):
            s += jnp.dot(b_next, part, preferred_element_type=F32)
        o_ref[...] = (s - cur).astype(o_ref.dtype)

    return pl.pallas_call(
        body, name=name, grid=(n_t, G),
        in_specs=[pl.BlockSpec((T, POOL_GROUP), lambda i, g: (i, g)),
                  pl.BlockSpec((T, POOL_GROUP), lambda i, g: (jnp.minimum(i + 1, n_t - 1), g))],
        out_specs=pl.BlockSpec((T, POOL_GROUP), lambda i, g: (i, g)),
        out_shape=jax.ShapeDtypeStruct((Lp, POOL_WIDTH), BF), compiler_params=_cp(('parallel', 'parallel')),
    )(dd, dd)


def _pool_mix_fwd(d, pool_w, scale, name):
    Lp = d.shape[0]
    G = len(POOL_WINDOWS)
    tm = _div_tile(Lp, 1088, 16)

    def body(d_ref, w_ref, s_ref, e_ref, y_ref):
        e = jnp.dot(d_ref[...], w_ref[...], preferred_element_type=F32)
        e_ref[...] = e.astype(e_ref.dtype)
        y_ref[...] = (e * s_ref[...]).astype(y_ref.dtype)

    blk = pl.BlockSpec((tm, POOL_GROUP), lambda g, i: (i, g))
    return pl.pallas_call(
        body, name=name, grid=(G, Lp // tm),
        in_specs=[blk, pl.BlockSpec((None, POOL_GROUP, POOL_GROUP), lambda g, i: (g, 0, 0)),
                  pl.BlockSpec((1, POOL_GROUP), lambda g, i: (0, g))],
        out_specs=[blk, blk], out_shape=[jax.ShapeDtypeStruct((Lp, POOL_WIDTH), BF)] * 2,
        compiler_params=_cp(('parallel', 'parallel')),
    )(d, pool_w, scale)


def _pool_mix_bwd(dyp, e, d, pool_w, scale, name):
    Lp = d.shape[0]
    G = len(POOL_WINDOWS)
    tm = _div_tile(Lp, 1088, 16)

    def body(dy_ref, e_ref, d_ref, w_ref, s_ref, dd_ref, ds_ref, dw_ref):
        i = pl.program_id(1)
        dy = dy_ref[...]
        de = (dy * s_ref[...]).astype(BF)
        dd_ref[...] = lax.dot_general(de, w_ref[...], _DIMS['nt'], preferred_element_type=F32)
        ds_part = jnp.sum(dy * e_ref[...].astype(F32), axis=0, keepdims=True)
        dw_part = lax.dot_general(d_ref[...], de, _DIMS['tn'], preferred_element_type=F32)

        @pl.when(i == 0)
        def _():
            ds_ref[...] = ds_part
            dw_ref[...] = dw_part

        @pl.when(i > 0)
        def _():
            ds_ref[...] += ds_part
            dw_ref[...] += dw_part

    blk = pl.BlockSpec((tm, POOL_GROUP), lambda g, i: (i, g))
    wblk = pl.BlockSpec((None, POOL_GROUP, POOL_GROUP), lambda g, i: (g, 0, 0))
    sblk = pl.BlockSpec((1, POOL_GROUP), lambda g, i: (0, g))
    return pl.pallas_call(
        body, name=name, grid=(G, Lp // tm), in_specs=[blk, blk, blk, wblk, sblk], out_specs=[blk, sblk, wblk],
        out_shape=[jax.ShapeDtypeStruct((Lp, POOL_WIDTH), F32), jax.ShapeDtypeStruct((1, POOL_WIDTH), F32),
                   jax.ShapeDtypeStruct((G, POOL_GROUP, POOL_GROUP), F32)],
        compiler_params=_cp(('parallel', 'arbitrary')),
    )(dyp, e, d, pool_w, scale)


def _rot_half(t):
    lane = lax.broadcasted_iota(jnp.int32, t.shape, 1)
    half = QK_ROPE // 2
    return jnp.where(lane < half, -pltpu.roll(t, 128 - half, 1), pltpu.roll(t, half, 1))


def _lora_norms(z, q_gain, kv_gain, lay, name):
    Lp = z.shape[0]
    QL, KVL = lay['QL'], lay['KVL']
    tm = _div_tile(Lp, 256, 16)

    def body(q_ref, kv_ref, qg_ref, kg_ref, qo_ref, ko_ref):
        a = q_ref[...]
        qo_ref[...] = (a * _rms(a) * qg_ref[...]).astype(BF)
        b = kv_ref[...]
        ko_ref[...] = (b * _rms(b) * kg_ref[...]).astype(BF)

    return pl.pallas_call(
        body, name=name, grid=(Lp // tm,),
        in_specs=[_rb(tm, QL, lay['cq'] // QL), _rb(tm, KVL, lay['ckv'] // KVL), _fixed((1, QL)), _fixed((1, KVL))],
        out_specs=[_rb(tm, QL), _rb(tm, KVL)],
        out_shape=[jax.ShapeDtypeStruct((Lp, QL), BF), jax.ShapeDtypeStruct((Lp, KVL), BF)],
        compiler_params=_cp(('parallel',)),
    )(z, z, q_gain, kv_gain)


def _lora_norms_bwd(dqn, dkn, z, q_gain, kv_gain, lay, name):
    Lp = z.shape[0]
    QL, KVL = lay['QL'], lay['KVL']
    tm = _div_tile(Lp, 256, 16)

    def one(dy, v, gain):
        r = _rms(v)
        w = dy * gain
        return r * w - v * (r * r * r) * jnp.mean(w * v, axis=-1, keepdims=True), jnp.sum(dy * v * r, axis=0, keepdims=True)

    def body(dq_ref, dk_ref, q_ref, kv_ref, qg_ref, kg_ref, o_ref, dqg_ref, dkg_ref):
        da, ga = one(dq_ref[...], q_ref[...], qg_ref[...])
        db, gb = one(dk_ref[...], kv_ref[...], kg_ref[...])
        o_ref[:, :QL] = da.astype(BF)
        o_ref[:, QL:] = db.astype(BF)

        @pl.when(pl.program_id(0) == 0)
        def _():
            dqg_ref[...] = ga
            dkg_ref[...] = gb

        @pl.when(pl.program_id(0) > 0)
        def _():
            dqg_ref[...] += ga
            dkg_ref[...] += gb

    return pl.pallas_call(
        body, name=name, grid=(Lp // tm,),
        in_specs=[_rb(tm, QL), _rb(tm, KVL), _rb(tm, QL, lay['cq'] // QL), _rb(tm, KVL, lay['ckv'] // KVL),
                  _fixed((1, QL)), _fixed((1, KVL))],
        out_specs=[_rb(tm, QL + KVL), _fixed((1, QL)), _fixed((1, KVL))],
        out_shape=[jax.ShapeDtypeStruct((Lp, QL + KVL), BF), jax.ShapeDtypeStruct((1, QL), F32),
                   jax.ShapeDtypeStruct((1, KVL), F32)],
        compiler_params=_cp(('arbitrary',)),
    )(dqn, dkn, z, z, q_gain, kv_gain)


def _qk_prep(q_raw, kv, z, cos, sin, lay, H, name):
    Lp = q_raw.shape[0]
    W = H * HEAD_PAD
    tm = _div_tile(Lp, 256, 16)

    def body(q_ref, kv_ref, kr_ref, c_ref, s_ref, qo_ref, ko_ref):
        c, s = c_ref[...], s_ref[...]

        def rope(t):
            return t * c + _rot_half(t) * s

        kpe = rope(kr_ref[...]).astype(BF)
        for h in range(H):
            b = h * HEAD_PAD
            qo_ref[:, b:b + 128] = (q_ref[:, b:b + 128] * SOFTMAX_SCALE).astype(BF)
            qo_ref[:, b + 128:b + 256] = (rope(q_ref[:, b + 128:b + 256]) * SOFTMAX_SCALE).astype(BF)
            ko_ref[:, b:b + 128] = kv_ref[:, b:b + 128]
            ko_ref[:, b + 128:b + 256] = kpe

    return pl.pallas_call(
        body, name=name, grid=(Lp // tm,),
        in_specs=[_rb(tm, W), _rb(tm, W), _rb(tm, 128, lay['kr'] // 128), _rb(tm, 128), _rb(tm, 128)],
        out_specs=[_rb(tm, W), _rb(tm, W)], out_shape=[jax.ShapeDtypeStruct((Lp, W), BF)] * 2,
        compiler_params=_cp(('parallel',)),
    )(q_raw, kv, z, cos, sin)


def _qk_prep_bwd(dQ, dK, dV, cos, sin, H, name):
    Lp = dQ.shape[0]
    W = H * HEAD_PAD
    tm = _div_tile(Lp, 256, 16)

    def body(dq_ref, dk_ref, dv_ref, c_ref, s_ref, qo_ref, kvo_ref, kro_ref):
        c, s = c_ref[...], s_ref[...]

        def unrope(t):
            return t * c - _rot_half(t * s)

        acc = jnp.zeros((tm, 128), F32)
        for h in range(H):
            b = h * HEAD_PAD
            qo_ref[:, b:b + 128] = (dq_ref[:, b:b + 128] * SOFTMAX_SCALE).astype(BF)
            qo_ref[:, b + 128:b + 256] = (unrope(dq_ref[:, b + 128:b + 256]) * SOFTMAX_SCALE).astype(BF)
            kvo_ref[:, b:b + 128] = dk_ref[:, b:b + 128].astype(BF)
            kvo_ref[:, b + 128:b + 256] = dv_ref[:, h * V_DIM:(h + 1) * V_DIM]
            acc += dk_ref[:, b + 128:b + 256]
        kro_ref[...] = unrope(acc).astype(BF)

    return pl.pallas_call(
        body, name=name, grid=(Lp // tm,),
        in_specs=[_rb(tm, W), _rb(tm, W), _rb(tm, H * V_DIM), _rb(tm, 128), _rb(tm, 128)],
        out_specs=[_rb(tm, W), _rb(tm, W), _rb(tm, 128)],
        out_shape=[jax.ShapeDtypeStruct((Lp, W), BF), jax.ShapeDtypeStruct((Lp, W), BF),
                   jax.ShapeDtypeStruct((Lp, 128), BF)],
        compiler_params=_cp(('parallel',)),
    )(dQ, dK, dV, cos, sin)


def _call_carrying(core, name, grid, in_specs, out_specs, out_shape, scratch, args, phases):
    n_in, n_out, n_scr = len(in_specs), len(out_specs), len(scratch)
    extra_in, extra_out, alias, sems = _phase_operands(phases, n_in, n_out)
    n_ei, n_eo = len(extra_in), len(extra_out)

    def body(*refs):
        ins, ein = refs[:n_in], refs[n_in:n_in + n_ei]
        outs = refs[n_in + n_ei:n_in + n_ei + n_out]
        eout = refs[n_in + n_ei + n_out:n_in + n_ei + n_out + n_eo]
        rest = refs[n_in + n_ei + n_out + n_eo:]
        scr, sem_refs = rest[:n_scr], rest[n_scr:]
        if phases:
            ids = [pl.program_id(d) for d in range(len(grid))]
            first, last = ids[0] == 0, ids[0] == grid[0] - 1
            for d in range(1, len(grid)):
                first, last = first & (ids[d] == 0), last & (ids[d] == grid[d] - 1)

            @pl.when(first)
            def _():
                _phase_start(phases, ein, eout, sem_refs)

        core(*ins, *outs, *scr)
        if phases:
            @pl.when(last)
            def _():
                _phase_finish(phases, ein, eout, sem_refs)

    outs = pl.pallas_call(
        body, name=name, grid=grid, in_specs=list(in_specs) + [ANY] * n_ei, out_specs=list(out_specs) + [ANY] * n_eo,
        out_shape=list(out_shape) + extra_out, input_output_aliases=alias, scratch_shapes=list(scratch) + sems,
        compiler_params=_cp(('arbitrary',) * len(grid)),
    )(*args, *extra_in)
    return list(outs[:n_out]), _phase_results(phases, outs[n_out:])


def _flash_fwd(Q, K, kv, H, name, phases=()):
    Lp = Q.shape[0]
    T = _div_tile(Lp, 256, 16)

    n_t = Lp // T
    FC = min(FLASH_FWD_CHUNK, n_t)
    CH = FC * T

    def body(q_ref, k_ref, v_ref, o_ref, lse_ref, m_s, l_s, acc_s):
        i = pl.program_id(1)
        q = q_ref[...]
        m_s[...] = jnp.full((T, 1), NEG, F32)
        l_s[...] = jnp.zeros((T, 1), F32)
        acc_s[...] = jnp.zeros((T, V_DIM), F32)

        def step(start, width, masked):
            parts = [(start, width, False)] if not masked else (
                ([(start, width - T, False)] if width > T else []) + [(start + (width - T), T, True)])
            scores = []
            for at, wd, diag in parts:
                s = lax.dot_general(q, k_ref[pl.ds(at, wd), :], _DIMS['nt'], preferred_element_type=F32)
                if diag:
                    rr = lax.broadcasted_iota(jnp.int32, (T, T), 0)
                    cc = lax.broadcasted_iota(jnp.int32, (T, T), 1)
                    s = jnp.where(cc <= rr, s, NEG)
                scores.append(s)
            m = m_s[...]
            m_new = m
            for s in scores:
                m_new = jnp.maximum(m_new, jnp.max(s, axis=-1, keepdims=True))
            alpha = jnp.exp(m - m_new)
            l = alpha * l_s[...]
            acc = alpha * acc_s[...]
            for (at, wd, _), s in zip(parts, scores):
                p = jnp.exp(s - m_new)
                l = l + jnp.sum(p, axis=-1, keepdims=True)
                acc = acc + jnp.dot(p.astype(BF), v_ref[pl.ds(at, wd), :], preferred_element_type=F32)
            l_s[...] = l
            acc_s[...] = acc
            m_s[...] = m_new

        n_full = i // FC

        def full(cidx, carry):
            step(pl.multiple_of(cidx * CH, CH), CH, False)
            return carry

        if n_t > FC:
            lax.fori_loop(0, n_full, full, 0)
        for nb in range(1, FC + 1):
            @pl.when(i % FC == nb - 1)
            def _(nb=nb):
                step(pl.multiple_of(n_full * CH, CH), nb * T, True)

        l = l_s[...]
        o_ref[...] = (acc_s[...] / l).astype(o_ref.dtype)
        lse_ref[...] = jnp.broadcast_to(m_s[...] + jnp.log(l), (T, 128))

    return _call_carrying(
        body, name, (H, Lp // T),
        [pl.BlockSpec((T, HEAD_PAD), lambda h, i: (i, h)), pl.BlockSpec((Lp, HEAD_PAD), lambda h, i: (0, h)),
         pl.BlockSpec((Lp, V_DIM), lambda h, i: (0, 2 * h + 1))],
        [pl.BlockSpec((T, V_DIM), lambda h, i: (i, h)), pl.BlockSpec((None, T, 128), lambda h, i: (h, i, 0))],
        [jax.ShapeDtypeStruct((Lp, H * V_DIM), BF), jax.ShapeDtypeStruct((H, Lp, 128), F32)],
        [pltpu.VMEM((T, 1), F32), pltpu.VMEM((T, 1), F32), pltpu.VMEM((T, V_DIM), F32)], (Q, K, kv), phases)


def _flash_bwd(Q, K, kv, O, dO, lse, H, name, phases=()):
    Lp = Q.shape[0]
    T = _div_tile(Lp, 256, 16)
    n_t = Lp // T
    FC = min(FLASH_BWD_CHUNK, n_t)

    def body(q_ref, k_ref, v_ref, o_ref, do_ref, lse_ref, dq_ref, dk_ref, dv_ref, dk_acc, dv_acc):
        j = pl.program_id(1)

        @pl.when(j == 0)
        def _():
            dq_ref[...] = jnp.zeros_like(dq_ref)

        kj, vj = k_ref[...], v_ref[...]
        dk_acc[...] = jnp.zeros_like(dk_acc)
        dv_acc[...] = jnp.zeros_like(dv_acc)

        def step(start, width, masked):
            rows = pl.ds(start, width)
            qi, doi = q_ref[rows, :], do_ref[rows, :]
            delta = jnp.sum(doi.astype(F32) * o_ref[rows, :].astype(F32), axis=-1, keepdims=True)
            s = lax.dot_general(qi, kj, _DIMS['nt'], preferred_element_type=F32)
            p = jnp.exp(s - lse_ref[rows, :][:, :1])
            if masked:
                rr = lax.broadcasted_iota(jnp.int32, (T, T), 0)
                cc = lax.broadcasted_iota(jnp.int32, (T, T), 1)
                first = jnp.where(cc <= rr, p[:T], 0.0)
                p = first if width == T else jnp.concatenate([first, p[T:]], axis=0)
            dp = lax.dot_general(doi, vj, _DIMS['nt'], preferred_element_type=F32)
            ds = (p * (dp - delta)).astype(BF)
            dv_acc[...] += lax.dot_general(p.astype(BF), doi, _DIMS['tn'], preferred_element_type=F32)
            dk_acc[...] += lax.dot_general(ds, qi, _DIMS['tn'], preferred_element_type=F32)
            dq_ref[rows, :] += jnp.dot(ds, kj, preferred_element_type=F32)

        head = (n_t - 1 - j) % FC + 1
        for nb in range(1, FC + 1):
            @pl.when(head == nb)
            def _(nb=nb):
                step(pl.multiple_of(j * T, T), nb * T, True)

        def full(cidx, carry):
            step(pl.multiple_of((j + head + cidx * FC) * T, T), FC * T, False)
            return carry

        if n_t > FC:
            lax.fori_loop(0, (n_t - j - head) // FC, full, 0)
        dk_ref[...] = dk_acc[...]
        dv_ref[...] = dv_acc[...].astype(dv_ref.dtype)

    head_q = pl.BlockSpec((Lp, HEAD_PAD), lambda h, j: (0, h))
    head_v = pl.BlockSpec((Lp, V_DIM), lambda h, j: (0, h))
    return _call_carrying(
        body, name, (H, n_t),
        [head_q, pl.BlockSpec((T, HEAD_PAD), lambda h, j: (j, h)), pl.BlockSpec((T, V_DIM), lambda h, j: (j, 2 * h + 1)),
         head_v, head_v, pl.BlockSpec((None, Lp, 128), lambda h, j: (h, 0, 0))],
        [head_q, pl.BlockSpec((T, HEAD_PAD), lambda h, j: (j, h)), pl.BlockSpec((T, V_DIM), lambda h, j: (j, h))],
        [jax.ShapeDtypeStruct((Lp, H * HEAD_PAD), F32), jax.ShapeDtypeStruct((Lp, H * HEAD_PAD), F32),
         jax.ShapeDtypeStruct((Lp, H * V_DIM), BF)],
        [pltpu.VMEM((T, HEAD_PAD), F32), pltpu.VMEM((T, V_DIM), F32)], (Q, K, kv, O, dO, lse), phases)


def _gate_fwd(z, y_pool, y_mla, lay, name):
    Lp, D = y_pool.shape
    tm = _div_tile(Lp, 256, 16)

    def body(gp_ref, gm_ref, yp_ref, ym_ref, o_ref):
        o_ref[...] = (jax.nn.sigmoid(gp_ref[...]) * yp_ref[...] + jax.nn.sigmoid(gm_ref[...]) * ym_ref[...]).astype(BF)

    return pl.pallas_call(
        body, name=name, grid=(Lp // tm,),
        in_specs=[_rb(tm, D, lay['gp'] // D), _rb(tm, D, lay['gm'] // D), _rb(tm, D), _rb(tm, D)], out_specs=_rb(tm, D),
        out_shape=jax.ShapeDtypeStruct((Lp, D), BF), compiler_params=_cp(('parallel',)),
    )(z, z, y_pool, y_mla)


def _gate_bwd(dy, z, y_pool, y_mla, lay, name):
    Lp, D = y_pool.shape
    tm = _div_tile(Lp, 256, 16)

    def body(dy_ref, gp_ref, gm_ref, yp_ref, ym_ref, dp_ref, dm_ref, dg_ref):
        dy_ = dy_ref[...]
        sp, sm = jax.nn.sigmoid(gp_ref[...]), jax.nn.sigmoid(gm_ref[...])
        dp_ref[...] = (dy_ * sp).astype(BF)
        dm_ref[...] = (dy_ * sm).astype(BF)
        dg_ref[:, :D] = (dy_ * yp_ref[...] * (sp * (1.0 - sp))).astype(BF)
        dg_ref[:, D:] = (dy_ * ym_ref[...] * (sm * (1.0 - sm))).astype(BF)

    return pl.pallas_call(
        body, name=name, grid=(Lp // tm,),
        in_specs=[_rb(tm, D), _rb(tm, D, lay['gp'] // D), _rb(tm, D, lay['gm'] // D), _rb(tm, D), _rb(tm, D)],
        out_specs=[_rb(tm, D), _rb(tm, D), _rb(tm, 2 * D)],
        out_shape=[jax.ShapeDtypeStruct((Lp, D), BF), jax.ShapeDtypeStruct((Lp, D), BF),
                   jax.ShapeDtypeStruct((Lp, 2 * D), BF)],
        compiler_params=_cp(('parallel',)),
    )(dy, z, z, y_pool, y_mla)


def _z_layout(D, QL, KVL):
    cq = POOL_WIDTH
    ckv = cq + QL
    gp = -(-(ckv + KVL) // D) * D
    gm = gp + D
    kr = gm + D
    return dict(QL=QL, KVL=KVL, cq=cq, ckv=ckv, gp=gp, gm=gm, kr=kr, width=kr + 128)


def _w_in_aligned(wt, lay, D):
    n0 = POOL_WIDTH + lay['QL'] + lay['KVL']
    parts = [wt[:n0], jnp.zeros((lay['gp'] - n0, D), wt.dtype), wt[n0 + QK_ROPE:], wt[n0:n0 + QK_ROPE],
             jnp.zeros((128 - QK_ROPE, D), wt.dtype)]
    return jnp.concatenate(parts, axis=0)


def _w_in_logical(wt_al, lay, D):
    n0 = POOL_WIDTH + lay['QL'] + lay['KVL']
    return jnp.concatenate([wt_al[:n0], wt_al[lay['kr']:lay['kr'] + QK_ROPE], wt_al[lay['gp']:lay['gp'] + 2 * D]],
                           axis=0)


def kernel(x, meta_tokens, norm_ffn1_pre, norm_ffn1_post, ffn1_w_gu, ffn1_w_down, norm_mix_pre, norm_mix_post, w_in, pool_w, pool_scale, w_pool_o, q_a_norm, w_q_b, kv_a_norm, w_kv_b, w_mla_o, w_out, norm_ffn2_pre, norm_ffn2_post, ffn2_w_gu, ffn2_w_down, loss_target, m_meta_tokens, m_norm_ffn1_pre, m_norm_ffn1_post, m_ffn1_w_gu, m_ffn1_w_down, m_norm_mix_pre, m_norm_mix_post, m_w_in, m_pool_w, m_pool_scale, m_w_pool_o, m_q_a_norm, m_w_q_b, m_kv_a_norm, m_w_kv_b, m_w_mla_o, m_w_out, m_norm_ffn2_pre, m_norm_ffn2_post, m_ffn2_w_gu, m_ffn2_w_down, v_meta_tokens, v_norm_ffn1_pre, v_norm_ffn1_post, v_ffn1_w_gu, v_ffn1_w_down, v_norm_mix_pre, v_norm_mix_post, v_w_in, v_pool_w, v_pool_scale, v_w_pool_o, v_q_a_norm, v_w_q_b, v_kv_a_norm, v_w_kv_b, v_w_mla_o, v_w_out, v_norm_ffn2_pre, v_norm_ffn2_post, v_ffn2_w_gu, v_ffn2_w_down):
    given = dict(locals())
    W = {n: given[n] for n in WEIGHTS}
    M = {n: given['m_' + n] for n in WEIGHTS}
    V = {n: given['v_' + n] for n in WEIGHTS}

    S, D = x.shape[1], x.shape[2]
    NM = meta_tokens.shape[0]
    L = NM + S
    Lp = -(-L // ROW_ALIGN) * ROW_ALIGN
    QL, KVL = w_q_b.shape[1], w_kv_b.shape[1]
    H = w_q_b.shape[2] * N_CHIPS // QK_DIM
    lay = _z_layout(D, QL, KVL)
    gx, gy = lax.axis_index('x'), lax.axis_index('y')
    chip = 2 * gx + gy

    names = list(BIG)
    for d in (W, M, V):
        d['w_in'] = jnp.swapaxes(d['w_in'], 1, 2)
    slab = {n: W[n][0] for n in names}
    kind = dict(BIG, meta_tokens='col')
    slab_shape = {n: tuple(slab[n].shape) for n in names}
    gc = lax.axis_index('c')
    c_arr = jnp.stack([gc]).astype(jnp.int32)
    place_arr = jnp.stack([gc, chip]).astype(jnp.int32)

    full = {n: _cast_place(slab[n], BIG[n], place_arr, BF, f'place_{n}') for n in names}
    full['meta_tokens'] = _cast_place(meta_tokens, 'col', place_arr, F32, 'place_meta_tokens')
    G0 = ['ffn1_w_gu', 'meta_tokens']
    G0B = ['ffn1_w_down']
    G1W = ['w_in']
    G1R = ['pool_w', 'w_pool_o', 'w_q_b', 'w_kv_b']
    G2 = ['w_mla_o', 'w_out']
    G3 = ['ffn2_w_gu', 'ffn2_w_down']

    def gather(phase_fn, group):
        return phase_fn([full[n] for n in group], [kind[n] for n in group])

    def arrived(group, res):
        full.update(zip(group, res))

    arrived(G0, _all_gather([full[n] for n in G0], [kind[n] for n in G0]))
    meta_full = full['meta_tokens']

    pos = jnp.arange(Lp, dtype=F32)
    inv = ROPE_THETA ** (-jnp.arange(0, QK_ROPE, 2, dtype=F32) / QK_ROPE)
    ang = pos[:, None] * inv[None, :]
    ang = jnp.concatenate([ang, ang], axis=-1)
    cos = jnp.pad(jnp.cos(ang), ((0, 0), (0, 128 - QK_ROPE)), constant_values=1.0)
    sin = jnp.pad(jnp.sin(ang), ((0, 0), (0, 128 - QK_ROPE)))

    h0 = jnp.concatenate([meta_full, x[0], jnp.zeros((Lp - L, D), F32)], axis=0)
    tgt = jnp.pad(loss_target[0], ((NM, Lp - L), (0, 0)))

    n1 = _norm_fwd(h0, norm_ffn1_pre, 'ffn1_norm')
    gu1, (res_b, res) = _mm(n1, full['ffn1_w_gu'], 'nn', BF, 'ffn1_gu',
                            [gather(_phase_gather_ici, G0B), gather(_phase_gather_ici, G1R)])
    arrived(G0B, res_b)
    arrived(G1R, res)
    def gather_ici(group, peers):
        return _phase_gather_ici([full[n] for n in group], [kind[n] for n in group], peers)

    (a1,), (res_b, res_w) = _swiglu_fwd(gu1, 'ffn1_act', [gather(_phase_gather_d2d, G0B), gather_ici(G1W, (1,))])
    arrived(G0B, res_b)
    arrived(G1W, res_w)
    f1, (res, res_w) = _mm(a1, full['ffn1_w_down'], 'nn', F32, 'ffn1_down',
                           [gather(_phase_gather_d2d, G1R), gather_ici(G1W, (2, 3))])
    arrived(G1R, res)
    arrived(G1W, res_w)
    (h1, n2), (res_w,) = _post_residual(h0, f1, norm_ffn1_post, 0.5, norm_mix_pre, 'ffn1_res',
                                        [gather(_phase_gather_d2d, G1W)])
    arrived(G1W, res_w)

    w_in_al = _w_in_aligned(full['w_in'], lay, D)
    w_q_pad = jnp.pad(full['w_q_b'].reshape(QL, H, QK_DIM), ((0, 0), (0, 0), (0, HEAD_PAD - QK_DIM))).reshape(
        QL, H * HEAD_PAD)

    z, (res,) = _mm(n2, w_in_al, 'nt', F32, 'mix_in', [gather(_phase_gather_ici, G2)])
    arrived(G2, res)
    d_pool = _pool_fwd(z, 'pool_fwd')
    e_pool, yp = _pool_mix_fwd(d_pool, full['pool_w'], pool_scale, 'pool_mix')
    y_pool = _mm(yp, full['w_pool_o'], 'nn', F32, 'pool_out')
    cqn, ckvn = _lora_norms(z, q_a_norm, kv_a_norm, lay, 'lora_norms')
    q_raw = _mm(cqn, w_q_pad, 'nn', F32, 'mla_q')
    kv = _mm(ckvn, full['w_kv_b'], 'nn', BF, 'mla_kv')
    Q, K = _qk_prep(q_raw, kv, z, cos, sin, lay, H, 'qk_prep')
    (O, lse), (res2, res3) = _flash_fwd(Q, K, kv, H, 'flash_fwd',
                                        [gather(_phase_gather_d2d, G2), gather(_phase_gather_ici, G3)])
    arrived(G2, res2)
    arrived(G3, res3)
    y_mla, (res,) = _mm(O, full['w_mla_o'], 'nn', F32, 'mla_out', [gather(_phase_gather_d2d, G3)])
    arrived(G3, res)
    y = _gate_fwd(z, y_pool, y_mla, lay, 'gate')
    m_mix = _mm(y, full['w_out'], 'nn', F32, 'mix_out')
    (h2, n3), _ = _post_residual(h1, m_mix, norm_mix_post, 1.0, norm_ffn2_pre, 'mix_res')

    gu2 = _mm(n3, full['ffn2_w_gu'], 'nn', BF, 'ffn2_gu')
    (a2,), _ = _swiglu_fwd(gu2, 'ffn2_act')
    f2 = _mm(a2, full['ffn2_w_down'], 'nn', F32, 'ffn2_down')

    G, theirs, sums, halves, reduced = {}, {}, {}, {}, {}
    RA = ['ffn2_w_down', 'ffn2_w_gu']
    RB1 = ['w_out', 'w_pool_o', 'pool_w', 'w_mla_o']
    RB2 = ['w_q_b', 'w_kv_b']
    RB3 = ['w_in']
    RC1 = ['ffn1_w_down']
    RC2 = ['ffn1_w_gu']

    def pair_phase(group):
        return _phase_pair([G[n] for n in group], [BIG[n] for n in group], [slab_shape[n] for n in group])

    def pair_sums(group, res):
        for n, t in zip(group, res):
            sums[n] = _pair_sum(G[n], t, BIG[n], slab_shape[n], c_arr, f'rs_pair_sum_{n}')

    def chip_phase(group):
        return _phase_chip([sums[n] for n in group])

    def chip_sums(group, res):
        for n, ld in zip(group, res):
            halves[n] = _chip_sum(sums[n], ld, BIG[n], slab_shape[n], place_arr, f'rs_chip_sum_{n}')

    def final_phase(group):
        return _phase_final([halves[n] for n in group], [BIG[n] for n in group])

    dh3, df2, G['norm_ffn2_post'], sq = _tail(h2, f2, norm_ffn2_post, 0.5, tgt, NM, S, 'ffn2_tail')

    G['ffn2_w_down'] = _mm(a2, df2, 'tn', BF, 'ffn2_dw_down')
    da2 = _mm(df2, full['ffn2_w_down'], 'nt', BF, 'ffn2_da')
    dgu2 = _swiglu_bwd(da2, gu2, 'ffn2_act_bwd')
    G['ffn2_w_gu'] = _mm(n3, dgu2, 'tn', BF, 'ffn2_dw_gu')
    dn3, (res,) = _mm(dgu2, full['ffn2_w_gu'], 'nt', F32, 'ffn2_dn', [pair_phase(RA)])
    (dh2, G['norm_ffn2_pre'], dm, G['norm_mix_post']), _ = _pre_bwd(
        dh3, dn3, h2, norm_ffn2_pre, 'ffn2_norm_bwd', before=(m_mix, norm_mix_post, 1.0))
    pair_sums(RA, res)

    G['w_out'] = _mm(y, dm, 'tn', BF, 'dw_out')
    dy = _mm(dm, full['w_out'], 'nt', F32, 'mix_out_bwd')
    dy_pool, dy_mla, d_gate = _gate_bwd(dy, z, y_pool, y_mla, lay, 'gate_bwd')
    G['w_pool_o'] = _mm(yp, dy_pool, 'tn', BF, 'dw_pool_o')
    dyp = _mm(dy_pool, full['w_pool_o'], 'nt', F32, 'pool_out_bwd')
    dd, G['pool_scale'], d_pool_w = _pool_mix_bwd(dyp, e_pool, d_pool, full['pool_w'], pool_scale, 'pool_mix_bwd')
    G['pool_w'] = d_pool_w.astype(BF)
    du_pool = _pool_bwd(dd, 'pool_bwd')
    G['w_mla_o'] = _mm(O, dy_mla, 'tn', BF, 'dw_mla_o')
    dO, (res,) = _mm(dy_mla, full['w_mla_o'], 'nt', BF, 'mla_out_bwd', [pair_phase(RB1)])
    pair_sums(RB1, res)
    (dQ, dK, dV), (res, res_b) = _flash_bwd(Q, K, kv, O, dO, lse, H, 'flash_bwd', [chip_phase(RA), chip_phase(RB1)])
    chip_sums(RA, res)
    chip_sums(RB1, res_b)
    dq_raw, dkv, dkr = _qk_prep_bwd(dQ, dK, dV, cos, sin, H, 'qk_prep_bwd')
    d_w_q_pad, (res_b,) = _mm(cqn, dq_raw, 'tn', BF, 'dw_q_b', [final_phase(RB1)])
    reduced.update(zip(RB1, res_b))
    G['w_q_b'] = d_w_q_pad.reshape(QL, H, HEAD_PAD)[:, :, :QK_DIM].reshape(QL, H * QK_DIM)
    dcqn = _mm(dq_raw, w_q_pad, 'nt', F32, 'mla_q_bwd')
    G['w_kv_b'] = _mm(ckvn, dkv, 'tn', BF, 'dw_kv_b')
    dckvn, (res,) = _mm(dkv, full['w_kv_b'], 'nt', F32, 'mla_kv_bwd', [pair_phase(RB2)])
    pair_sums(RB2, res)
    d_lora, G['q_a_norm'], G['kv_a_norm'] = _lora_norms_bwd(dcqn, dckvn, z, q_a_norm, kv_a_norm, lay, 'lora_norms_bwd')
    n0 = POOL_WIDTH + QL + KVL
    dz = jnp.concatenate([du_pool, d_lora, jnp.zeros((Lp, lay['gp'] - n0), BF), d_gate, dkr], axis=1)
    d_w_in_al, (res, res_b) = _mm(dz, n2, 'tn', BF, 'dw_in', [final_phase(RA), chip_phase(RB2)])
    reduced.update(zip(RA, res))
    chip_sums(RB2, res_b)
    G['w_in'] = _w_in_logical(d_w_in_al, lay, D)
    dn2, (res, res_b) = _mm(dz, w_in_al, 'nn', F32, 'mix_in_bwd', [pair_phase(RB3), final_phase(RB2)])
    reduced.update(zip(RB2, res_b))
    (dh1, G['norm_mix_pre'], df1, G['norm_ffn1_post']), _ = _pre_bwd(
        dh2, dn2, h1, norm_mix_pre, 'mix_norm_bwd', before=(f1, norm_ffn1_post, 0.5))
    pair_sums(RB3, res)

    sums_b3 = [sums[n] for n in RB3]
    G['ffn1_w_down'], (land_b,) = _mm(a1, df1, 'tn', BF, 'ffn1_dw_down', [_phase_chip(sums_b3, (1, 2))])
    da1, (res, land_b) = _mm(df1, full['ffn1_w_down'], 'nt', BF, 'ffn1_da',
                             [pair_phase(RC1), _phase_chip(sums_b3, (3,), land_b)])
    chip_sums(RB3, land_b)
    dgu1 = _swiglu_bwd(da1, gu1, 'ffn1_act_bwd')
    pair_sums(RC1, res)
    G['ffn1_w_gu'], (res_c1, res_b) = _mm(n1, dgu1, 'tn', BF, 'ffn1_dw_gu', [chip_phase(RC1), final_phase(RB3)])
    chip_sums(RC1, res_c1)
    reduced.update(zip(RB3, res_b))
    (res,) = _run_phases([pair_phase(RC2)], 'rs_pair_exchange_tail')
    pair_sums(RC2, res)
    dn1, (res_c2, res_c1) = _mm(dgu1, full['ffn1_w_gu'], 'nt', F32, 'ffn1_dn', [chip_phase(RC2), final_phase(RC1)])
    chip_sums(RC2, res_c2)
    reduced.update(zip(RC1, res_c1))
    (dh0, G['norm_ffn1_pre']), (res_c2,) = _pre_bwd(dh1, dn1, h0, norm_ffn1_pre, 'ffn1_norm_bwd', [final_phase(RC2)])
    reduced.update(zip(RC2, res_c2))
    grad_x = dh0[NM:L][None]

    SW = max(D, POOL_WIDTH)

    def widen(a, fill=0.0):
        return jnp.pad(a, ((0, 0), (0, SW - a.shape[1])), constant_values=fill)

    rows = [widen(G[n]) for n in SMALL_VEC] + [widen(dh0[:NM]), widen(sq)]
    n_rows = len(SMALL_VEC) + NM + 1
    pad_rows = -(-n_rows // 8) * 8 - n_rows
    small = _all_reduce_small(jnp.concatenate(rows + [jnp.zeros((pad_rows, SW), F32)], axis=0))
    loss = (0.5 / D) * small[len(SMALL_VEC) + NM, 0]
    for i, n in enumerate(SMALL_VEC):
        reduced[n] = small[i:i + 1, :G[n].shape[1]]
    mw = meta_tokens.shape[1]
    reduced['meta_tokens'] = lax.dynamic_slice(small, (len(SMALL_VEC), chip * mw), (NM, mw))

    grads, deltas, new_m, new_v = {}, {}, {}, {}
    for n in names:
        grads[n], deltas[n], new_m[n], new_v[n] = _adam(W[n], reduced[n][None], M[n], V[n], f'adam_{n}')
    for d in (grads, deltas, new_m, new_v):
        d['w_in'] = jnp.swapaxes(d['w_in'], 1, 2)
    n_vec = len(SMALL_VEC)
    vec_w = jnp.concatenate([widen(W[n]) for n in SMALL_VEC] + [jnp.zeros((16 - n_vec, SW), F32)], axis=0)
    vec_m = jnp.concatenate([widen(M[n]) for n in SMALL_VEC] + [jnp.zeros((16 - n_vec, SW), F32)], axis=0)
    vec_v = jnp.concatenate([widen(V[n], 1.0) for n in SMALL_VEC] + [jnp.ones((16 - n_vec, SW), F32)], axis=0)
    vec_g = jnp.concatenate([small[:n_vec], jnp.zeros((16 - n_vec, SW), F32)], axis=0)
    _, vd, vm, vv = _adam(vec_w, vec_g, vec_m, vec_v, 'adam_vectors')
    for i, n in enumerate(SMALL_VEC):
        wdt = W[n].shape[1]
        grads[n], deltas[n], new_m[n], new_v[n] = reduced[n], vd[i:i + 1, :wdt], vm[i:i + 1, :wdt], vv[i:i + 1, :wdt]
    grads['meta_tokens'], deltas['meta_tokens'], new_m['meta_tokens'], new_v['meta_tokens'] = _adam(
        meta_tokens, reduced['meta_tokens'], m_meta_tokens, v_meta_tokens, 'adam_meta')

    return (loss, grad_x, *[grads[n] for n in WEIGHTS], *[deltas[n] for n in WEIGHTS], *[new_m[n] for n in WEIGHTS],
            *[new_v[n] for n in WEIGHTS])
```

```python
import functools

import jax
import jax.numpy as jnp
import numpy as np
from jax import lax
from jax.experimental import pallas as pl
from jax.experimental.pallas import tpu as pltpu

F32 = jnp.float32
BF = jnp.bfloat16
MESH = pl.DeviceIdType.MESH

EPS = 1e-6
N_CHIPS = 4
POOL_WINDOWS = (2, 4, 8, 16)
POOL_GROUP = 256
POOL_WIDTH = POOL_GROUP * len(POOL_WINDOWS)
QK_NOPE = 128
QK_ROPE = 64
V_DIM = 128
QK_DIM = QK_NOPE + QK_ROPE
HEAD_PAD = 256
ROPE_THETA = 10000.0
SOFTMAX_SCALE = QK_DIM ** -0.5
ADAM_LR = 0.001
ADAM_B1 = 0.9
ADAM_B2 = 0.999
ADAM_EPS = 1e-08
ADAM_WD = 0.01
ADAM_STEP = 10
ROW_ALIGN = 256
VMEM_LIMIT = 56 * 1024 * 1024
MM_TK_ROWS = 4352
MM_TK_COLS = 6400
NEG = -1e30
FLASH_FWD_CHUNK = 17
FLASH_BWD_CHUNK = 8
WEIGHTS = ['meta_tokens', 'norm_ffn1_pre', 'norm_ffn1_post', 'ffn1_w_gu', 'ffn1_w_down', 'norm_mix_pre',
           'norm_mix_post', 'w_in', 'pool_w', 'pool_scale', 'w_pool_o', 'q_a_norm', 'w_q_b', 'kv_a_norm', 'w_kv_b',
           'w_mla_o', 'w_out', 'norm_ffn2_pre', 'norm_ffn2_post', 'ffn2_w_gu', 'ffn2_w_down']
BIG = {'ffn1_w_gu': 'col', 'ffn1_w_down': 'row', 'w_in': 'row', 'pool_w': 'pool', 'w_pool_o': 'col', 'w_q_b': 'col',
       'w_kv_b': 'col', 'w_mla_o': 'row', 'w_out': 'row', 'ffn2_w_gu': 'col', 'ffn2_w_down': 'row'}
SMALL_VEC = ['norm_ffn1_pre', 'norm_ffn1_post', 'norm_mix_pre', 'norm_mix_post', 'norm_ffn2_pre', 'norm_ffn2_post',
             'pool_scale', 'q_a_norm', 'kv_a_norm']


def _div_tile(n, target, align):
    best = None
    for t in range(align, min(n, target) + 1, align):
        if n % t == 0:
            best = t
    return best if best is not None else n


def _cp(sem=None):
    return pltpu.CompilerParams(dimension_semantics=sem, vmem_limit_bytes=VMEM_LIMIT)


def _full_shape(kind, slab):
    if kind == 'col':
        return (slab[0], slab[1] * N_CHIPS)
    if kind == 'row':
        return (slab[0] * N_CHIPS, slab[1])
    return (slab[0], slab[1] * N_CHIPS, slab[2])


def _half_shape(kind, slab):
    if kind == 'pool':
        return (slab[0], slab[1] // 2, slab[2])
    if kind == 'row':
        return (slab[0], slab[1] // 2)
    return (slab[0] // 2, slab[1])


def _half_of_slab(ref, kind, c):
    if kind == 'pool':
        n = ref.shape[1] // 2
        return ref.at[:, pl.ds(c * n, n), :]
    if kind == 'row':
        n = ref.shape[1] // 2
        return ref.at[:, pl.ds(c * n, n)]
    n = ref.shape[0] // 2
    return ref.at[pl.ds(c * n, n), :]


def _piece(ref, kind, k, c):
    if kind == 'col':
        r, w = ref.shape[0] // 2, ref.shape[1] // N_CHIPS
        return ref.at[pl.ds(c * r, r), pl.ds(k * w, w)]
    if kind == 'row':
        r, w = ref.shape[0] // N_CHIPS, ref.shape[1] // 2
        return ref.at[pl.ds(k * r, r), pl.ds(c * w, w)]
    r = ref.shape[1] // (2 * N_CHIPS)
    return ref.at[:, pl.ds((2 * k + c) * r, r), :]


def _place():
    x, y, c = lax.axis_index('x'), lax.axis_index('y'), lax.axis_index('c')
    return x, y, c


def _peer_chip(x, y, j):
    px = 1 - x if (j >> 1) else x
    py = 1 - y if (j & 1) else y
    return px, py


ANY = pl.BlockSpec(memory_space=pl.ANY)


def _all_gather(bufs, kinds):
    n = len(bufs)

    def body(*refs):
        outs = refs[n:2 * n]
        ssem, rsem, fssem, frsem = refs[2 * n:]
        x, y, c = _place()
        me = 2 * x + y
        sib = (x, y, 1 - c)
        sends = []
        for w in range(n):
            for j in (1, 2, 3):
                px, py = _peer_chip(x, y, j)
                mine = _piece(outs[w], kinds[w], me, c)
                sends.append(pltpu.make_async_remote_copy(
                    src_ref=mine, dst_ref=mine, send_sem=ssem.at[w, j - 1], recv_sem=rsem.at[w, j - 1],
                    device_id=(px, py, c), device_id_type=MESH))
        for cp in sends:
            cp.start()
        fwds = []
        for w in range(n):
            for j in (1, 2, 3):
                px, py = _peer_chip(x, y, j)
                got = _piece(outs[w], kinds[w], 2 * px + py, c)
                pltpu.make_async_remote_copy(src_ref=got, dst_ref=got, send_sem=ssem.at[w, j - 1],
                                             recv_sem=rsem.at[w, j - 1], device_id=(px, py, c),
                                             device_id_type=MESH).wait_recv()
                fwd = pltpu.make_async_remote_copy(src_ref=got, dst_ref=got, send_sem=fssem.at[w, j - 1],
                                                   recv_sem=frsem.at[w, j - 1], device_id=sib, device_id_type=MESH)
                fwd.start()
                fwds.append(fwd)
        for w in range(n):
            for j in (1, 2, 3):
                px, py = _peer_chip(x, y, j)
                other = _piece(outs[w], kinds[w], 2 * px + py, 1 - c)
                pltpu.make_async_remote_copy(src_ref=other, dst_ref=other, send_sem=fssem.at[w, j - 1],
                                             recv_sem=frsem.at[w, j - 1], device_id=sib,
                                             device_id_type=MESH).wait_recv()
        for cp in sends + fwds:
            cp.wait_send()

    return pl.pallas_call(
        body, name='all_gather_weights', out_shape=[jax.ShapeDtypeStruct(b.shape, b.dtype) for b in bufs],
        in_specs=[ANY] * n, out_specs=[ANY] * n, input_output_aliases={w: w for w in range(n)},
        scratch_shapes=[pltpu.SemaphoreType.DMA((n, 3)), pltpu.SemaphoreType.DMA((n, 3)),
                        pltpu.SemaphoreType.DMA((n, 3)), pltpu.SemaphoreType.DMA((n, 3))],
    )(*bufs)


def _remote(src, dst, ssem, rsem, k, dev):
    return pltpu.make_async_remote_copy(src_ref=src, dst_ref=dst, send_sem=ssem.at[k], recv_sem=rsem.at[k],
                                        device_id=dev, device_id_type=MESH)


def _phase_gather_ici(bufs, kinds, peers=(1, 2, 3)):
    n, m = len(bufs), len(peers)

    def build(ins, outs, ssem, rsem):
        x, y, c = _place()
        me = 2 * x + y
        ds = []
        for w in range(n):
            for q, j in enumerate(peers):
                px, py = _peer_chip(x, y, j)
                mine = _piece(outs[w], kinds[w], me, c)
                got = _piece(outs[w], kinds[w], 2 * px + py, c)
                k = m * w + q
                ds.append((_remote(mine, mine, ssem, rsem, k, (px, py, c)), _remote(got, got, ssem, rsem, k, (px, py, c))))
        return ds

    return dict(ins=list(bufs), outs=[jax.ShapeDtypeStruct(b.shape, b.dtype) for b in bufs],
                alias={w: w for w in range(n)}, nsem=m * n, build=build)


def _phase_gather_d2d(bufs, kinds):
    n = len(bufs)

    def build(ins, outs, ssem, rsem):
        x, y, c = _place()
        sib = (x, y, 1 - c)
        ds = []
        for w in range(n):
            for j in (1, 2, 3):
                px, py = _peer_chip(x, y, j)
                have = _piece(outs[w], kinds[w], 2 * px + py, c)
                want = _piece(outs[w], kinds[w], 2 * px + py, 1 - c)
                k = 3 * w + j - 1
                ds.append((_remote(have, have, ssem, rsem, k, sib), _remote(want, want, ssem, rsem, k, sib)))
        return ds

    return dict(ins=list(bufs), outs=[jax.ShapeDtypeStruct(b.shape, b.dtype) for b in bufs],
                alias={w: w for w in range(n)}, nsem=3 * n, build=build)


def _phase_pair(grads, kinds, slabs):
    n = len(grads)

    def build(ins, outs, ssem, rsem):
        x, y, c = _place()
        sib = (x, y, 1 - c)
        ds = []
        for w in range(n):
            for k in range(N_CHIPS):
                cp = _remote(_piece(ins[w], kinds[w], k, 1 - c), outs[w].at[k], ssem, rsem, N_CHIPS * w + k, sib)
                ds.append((cp, cp))
        return ds

    return dict(ins=list(grads), alias={}, nsem=N_CHIPS * n, build=build,
                outs=[jax.ShapeDtypeStruct((N_CHIPS,) + _half_shape(k, s), g.dtype)
                      for g, k, s in zip(grads, kinds, slabs)])


def _phase_chip(sums, peers=(1, 2, 3), landing=None):
    n, m = len(sums), len(peers)

    def build(ins, outs, ssem, rsem):
        x, y, c = _place()
        ds = []
        for w in range(n):
            for q, j in enumerate(peers):
                px, py = _peer_chip(x, y, j)
                cp = _remote(ins[w].at[2 * px + py], outs[w].at[j - 1], ssem, rsem, m * w + q, (px, py, c))
                ds.append((cp, cp))
        return ds

    return dict(ins=list(sums) + (list(landing) if landing is not None else []),
                alias={n + w: w for w in range(n)} if landing is not None else {}, nsem=m * n, build=build,
                outs=[jax.ShapeDtypeStruct((3,) + s.shape[1:], s.dtype) for s in sums])


def _phase_final(slabs_half, kinds):
    n = len(slabs_half)

    def build(ins, outs, ssem, rsem):
        x, y, c = _place()
        sib = (x, y, 1 - c)
        ds = []
        for w in range(n):
            mine = _half_of_slab(outs[w], kinds[w], c)
            other = _half_of_slab(outs[w], kinds[w], 1 - c)
            ds.append((_remote(mine, mine, ssem, rsem, w, sib), _remote(other, other, ssem, rsem, w, sib)))
        return ds

    return dict(ins=list(slabs_half), outs=[jax.ShapeDtypeStruct(s.shape, s.dtype) for s in slabs_half],
                alias={w: w for w in range(n)}, nsem=n, build=build)


def _phase_operands(phases, n_main_in, n_main_out):
    ins, outs, alias, sems = [], [], {}, []
    for ph in phases:
        for i, o in ph['alias'].items():
            alias[n_main_in + len(ins) + i] = n_main_out + len(outs) + o
        ins += ph['ins']
        outs += ph['outs']
        sems += [pltpu.SemaphoreType.DMA((ph['nsem'],)), pltpu.SemaphoreType.DMA((ph['nsem'],))]
    return ins, outs, alias, sems


def _phase_copies(phases, in_refs, out_refs, sem_refs):
    ds, a, b = [], 0, 0
    for p, ph in enumerate(phases):
        ds += ph['build'](in_refs[a:a + len(ph['ins'])], out_refs[b:b + len(ph['outs'])], sem_refs[2 * p],
                          sem_refs[2 * p + 1])
        a += len(ph['ins'])
        b += len(ph['outs'])
    return ds


def _phase_start(phases, in_refs, out_refs, sem_refs):
    for send, _ in _phase_copies(phases, in_refs, out_refs, sem_refs):
        send.start()


def _phase_finish(phases, in_refs, out_refs, sem_refs):
    ds = _phase_copies(phases, in_refs, out_refs, sem_refs)
    for _, recv in ds:
        recv.wait_recv()
    for send, _ in ds:
        send.wait_send()


def _phase_results(phases, outs):
    res, b = [], 0
    for ph in phases:
        res.append(list(outs[b:b + len(ph['outs'])]))
        b += len(ph['outs'])
    return res


def _run_phases(phases, name):
    ins, out_shapes, alias, sems = _phase_operands(phases, 0, 0)
    n_in, n_out = len(ins), len(out_shapes)

    def body(*refs):
        in_refs, out_refs, sem_refs = refs[:n_in], refs[n_in:n_in + n_out], refs[n_in + n_out:]
        _phase_start(phases, in_refs, out_refs, sem_refs)
        _phase_finish(phases, in_refs, out_refs, sem_refs)

    outs = pl.pallas_call(body, name=name, out_shape=out_shapes, in_specs=[ANY] * n_in, out_specs=[ANY] * n_out,
                          input_output_aliases=alias, scratch_shapes=sems)(*ins)
    return _phase_results(phases, outs)


def _cast_place(shard, kind, place_arr, dtype, name):
    full = _full_shape(kind, shard.shape)
    if kind == 'pool':
        g, r, w = shard.shape
        grid = (g,)
        src = pl.BlockSpec((None, r, w), lambda i, s: (i, 0, 0))
        dst = pl.BlockSpec((None, r, w), lambda i, s: (i, s[1], 0))
    else:
        r, w = shard.shape
        tr, tc = _tile2(r, w)
        nrb, ncb = r // tr, w // tc
        grid = (nrb, ncb)
        src = pl.BlockSpec((tr, tc), lambda i, j, s: (i, j))
        if kind == 'col':
            dst = pl.BlockSpec((tr, tc), lambda i, j, s: (i, s[1] * ncb + j))
        else:
            dst = pl.BlockSpec((tr, tc), lambda i, j, s: (s[1] * nrb + i, j))

    def body(s_ref, x_ref, o_ref):
        o_ref[...] = x_ref[...].astype(o_ref.dtype)

    return pl.pallas_call(
        body, name=name,
        grid_spec=pltpu.PrefetchScalarGridSpec(num_scalar_prefetch=1, grid=grid, in_specs=[src], out_specs=dst),
        out_shape=jax.ShapeDtypeStruct(full, dtype), compiler_params=_cp(('parallel',) * len(grid)),
    )(place_arr, shard)


def _tile2(r, c):
    tr = _div_tile(r, max(16, (1 << 19) // c), 16)
    if tr == r or tr * c >= (1 << 17):
        return tr, c
    return r, _div_tile(c, max(128, (1 << 19) // r), 128)


def _pair_sum(grad, theirs, kind, slab, c_arr, name):
    hs = _half_shape(kind, slab)
    if kind == 'pool':
        grid = (N_CHIPS, hs[0], 1)
        own = pl.BlockSpec((None, hs[1], hs[2]), lambda k, i, j, s: (i, 2 * k + s[0], 0))
        stk = pl.BlockSpec((None, None, hs[1], hs[2]), lambda k, i, j, s: (k, i, 0, 0))
    else:
        tr, tc = _tile2(*hs)
        nrb, ncb = hs[0] // tr, hs[1] // tc
        grid = (N_CHIPS, nrb, ncb)
        if kind == 'col':
            own = pl.BlockSpec((tr, tc), lambda k, i, j, s: (s[0] * nrb + i, k * ncb + j))
        else:
            own = pl.BlockSpec((tr, tc), lambda k, i, j, s: (k * nrb + i, s[0] * ncb + j))
        stk = pl.BlockSpec((None, tr, tc), lambda k, i, j, s: (k, i, j))

    def body(s_ref, a_ref, b_ref, o_ref):
        o_ref[...] = (a_ref[...].astype(F32) + b_ref[...].astype(F32)).astype(o_ref.dtype)

    return pl.pallas_call(
        body, name=name,
        grid_spec=pltpu.PrefetchScalarGridSpec(num_scalar_prefetch=1, grid=grid, in_specs=[own, stk], out_specs=stk),
        out_shape=jax.ShapeDtypeStruct((N_CHIPS,) + hs, BF), compiler_params=_cp(('parallel',) * 3),
    )(c_arr, grad, theirs)


def _chip_sum(sums, landed, kind, slab, place_arr, name):
    hs = _half_shape(kind, slab)
    if kind == 'pool':
        grid = (hs[0], 1)
        blk = (None, None, hs[1], hs[2])
        mine = pl.BlockSpec(blk, lambda i, j, s: (s[1], i, 0, 0))
        land = [pl.BlockSpec(blk, lambda i, j, s, p=p: (p, i, 0, 0)) for p in range(3)]
        out = pl.BlockSpec((None, hs[1], hs[2]), lambda i, j, s: (i, s[0], 0))
    else:
        tr, tc = _tile2(*hs)
        nrb, ncb = hs[0] // tr, hs[1] // tc
        grid = (nrb, ncb)
        blk = (None, tr, tc)
        mine = pl.BlockSpec(blk, lambda i, j, s: (s[1], i, j))
        land = [pl.BlockSpec(blk, lambda i, j, s, p=p: (p, i, j)) for p in range(3)]
        if kind == 'col':
            out = pl.BlockSpec((tr, tc), lambda i, j, s: (s[0] * nrb + i, j))
        else:
            out = pl.BlockSpec((tr, tc), lambda i, j, s: (i, s[0] * ncb + j))

    def body(s_ref, a_ref, b_ref, c_ref, d_ref, o_ref):
        o_ref[...] = ((a_ref[...].astype(F32) + b_ref[...].astype(F32)) + c_ref[...].astype(F32)) + d_ref[...].astype(F32)

    return pl.pallas_call(
        body, name=name,
        grid_spec=pltpu.PrefetchScalarGridSpec(num_scalar_prefetch=1, grid=grid, in_specs=[mine] + land, out_specs=out),
        out_shape=jax.ShapeDtypeStruct(tuple(slab), F32), compiler_params=_cp(('parallel',) * 2),
    )(place_arr, sums, landed, landed, landed)


def _all_reduce_small(buf):
    rows, cols = buf.shape

    def body(in_ref, out_ref, land, ssem, rsem):
        x, y, c = _place()
        me = 4 * x + 2 * y + c
        land[me] = in_ref[...]
        started = []
        for j in range(1, 8):
            px = 1 - x if (j >> 2) & 1 else x
            py = 1 - y if (j >> 1) & 1 else y
            pc = 1 - c if j & 1 else c
            cp = pltpu.make_async_remote_copy(src_ref=in_ref, dst_ref=land.at[me], send_sem=ssem.at[j - 1],
                                              recv_sem=rsem.at[j - 1], device_id=(px, py, pc), device_id_type=MESH)
            cp.start()
            started.append(cp)
        for j in range(1, 8):
            px = 1 - x if (j >> 2) & 1 else x
            py = 1 - y if (j >> 1) & 1 else y
            pc = 1 - c if j & 1 else c
            slot = land.at[4 * px + 2 * py + pc]
            pltpu.make_async_remote_copy(src_ref=slot, dst_ref=slot, send_sem=ssem.at[j - 1], recv_sem=rsem.at[j - 1],
                                         device_id=(px, py, pc), device_id_type=MESH).wait_recv()
        for cp in started:
            cp.wait_send()
        acc = land[0]
        for d in range(1, 8):
            acc = acc + land[d]
        out_ref[...] = acc

    return pl.pallas_call(
        body, name='all_reduce_small', out_shape=jax.ShapeDtypeStruct((rows, cols), F32),
        in_specs=[pl.BlockSpec(memory_space=pltpu.VMEM)], out_specs=pl.BlockSpec(memory_space=pltpu.VMEM),
        scratch_shapes=[pltpu.VMEM((8, rows, cols), F32), pltpu.SemaphoreType.DMA((7,)), pltpu.SemaphoreType.DMA((7,))],
    )(buf)


def _elementwise(fn, ins, lead_index, out_shape, out_dtypes, name):
    nd = len(out_shape)
    r, cdim = out_shape[-2], out_shape[-1]
    tr, tc = _tile2(r, cdim)
    grid = tuple(out_shape[:-2]) + (r // tr, cdim // tc)
    block = (None,) * (nd - 2) + (tr, tc)

    def spec(lead):
        if lead is None:
            return pl.BlockSpec(block, lambda *g: tuple(g))
        return pl.BlockSpec((None,) + block, lambda *g, lead=lead: (lead,) + tuple(g))

    n_in = len(ins)

    def body(*refs):
        res = fn(*[r_[...] for r_ in refs[:n_in]])
        for o_ref, v in zip(refs[n_in:], res):
            o_ref[...] = v.astype(o_ref.dtype)

    return pl.pallas_call(
        body, name=name, grid=grid, in_specs=[spec(l) for l in lead_index],
        out_specs=[spec(None) for _ in out_dtypes],
        out_shape=[jax.ShapeDtypeStruct(tuple(out_shape), dt) for dt in out_dtypes],
        compiler_params=_cp(('parallel',) * len(grid)),
    )(*ins)


def _adam_fn(w, g, m, v):
    m = ADAM_B1 * m + (1.0 - ADAM_B1) * g
    v = ADAM_B2 * v + (1.0 - ADAM_B2) * (g * g)
    m_hat = m / (1.0 - ADAM_B1 ** ADAM_STEP)
    v_hat = v / (1.0 - ADAM_B2 ** ADAM_STEP)
    delta = -ADAM_LR * (m_hat / (jnp.sqrt(v_hat) + ADAM_EPS) + ADAM_WD * w)
    return g, delta, m, v


def _adam(w, g, m, v, name):
    return _elementwise(_adam_fn, [w, g, m, v], [None] * 4, w.shape, [F32] * 4, name)


_DIMS = {'nn': (((1,), (0,)), ((), ())), 'nt': (((1,), (1,)), ((), ())), 'tn': (((0,), (0,)), ((), ()))}


def _mm(a, b, mode, out_dtype, name, phases=()):
    if mode == 'nn':
        (M, K), N = a.shape, b.shape[1]
    elif mode == 'nt':
        (M, K), N = a.shape, b.shape[0]
    else:
        (K, M), N = a.shape, b.shape[1]
    if mode == 'tn':
        tm, tn, tk = _div_tile(M, 512, 128), _div_tile(N, 1024, 128), _div_tile(K, MM_TK_ROWS, 16)
        if tm < 256:
            tm = _div_tile(M, 1024, 128)
    else:
        tk = _div_tile(K, MM_TK_COLS, 128)
        tm, tn = _div_tile(M, 1088, 16), _div_tile(N, 1024 if tk <= 2816 else 512, 128)
    nk = K // tk
    a_spec = {'nn': pl.BlockSpec((tm, tk), lambda i, j, k: (i, k)), 'nt': pl.BlockSpec((tm, tk), lambda i, j, k: (i, k)),
              'tn': pl.BlockSpec((tk, tm), lambda i, j, k: (k, i))}[mode]
    b_spec = {'nn': pl.BlockSpec((tk, tn), lambda i, j, k: (k, j)), 'nt': pl.BlockSpec((tn, tk), lambda i, j, k: (j, k)),
              'tn': pl.BlockSpec((tk, tn), lambda i, j, k: (k, j))}[mode]
    dims = _DIMS[mode]
    gm, gn = M // tm, N // tn
    extra_in, extra_out, alias, sems = _phase_operands(phases, 2, 1)
    n_ei, n_eo = len(extra_in), len(extra_out)

    def body(*refs):
        a_ref, b_ref, ein = refs[0], refs[1], refs[2:2 + n_ei]
        o_ref, eout = refs[2 + n_ei], refs[3 + n_ei:3 + n_ei + n_eo]
        acc_ref, sem_refs = refs[3 + n_ei + n_eo], refs[4 + n_ei + n_eo:]
        i, j, k = pl.program_id(0), pl.program_id(1), pl.program_id(2)
        if phases:
            @pl.when((i == 0) & (j == 0) & (k == 0))
            def _():
                _phase_start(phases, ein, eout, sem_refs)

        part = lax.dot_general(a_ref[...], b_ref[...], dims, preferred_element_type=F32)
        if nk == 1:
            o_ref[...] = part.astype(o_ref.dtype)
        else:
            @pl.when(k == 0)
            def _():
                acc_ref[...] = part

            @pl.when(k > 0)
            def _():
                acc_ref[...] += part

            @pl.when(k == nk - 1)
            def _():
                o_ref[...] = acc_ref[...].astype(o_ref.dtype)

        if phases:
            @pl.when((i == gm - 1) & (j == gn - 1) & (k == nk - 1))
            def _():
                _phase_finish(phases, ein, eout, sem_refs)

    outs = pl.pallas_call(
        body, name=name, grid=(gm, gn, nk), in_specs=[a_spec, b_spec] + [ANY] * n_ei,
        out_specs=[pl.BlockSpec((tm, tn), lambda i, j, k: (i, j))] + [ANY] * n_eo,
        out_shape=[jax.ShapeDtypeStruct((M, N), out_dtype)] + extra_out, input_output_aliases=alias,
        scratch_shapes=[pltpu.VMEM((tm, tn) if nk > 1 else (8, 128), F32)] + sems,
        compiler_params=_cp(('arbitrary',) * 3 if phases else ('parallel', 'parallel', 'arbitrary')),
    )(a, b, *extra_in)
    if phases:
        return outs[0], _phase_results(phases, outs[1:])
    return outs[0]


def _rb(tm, w, col=0):
    return pl.BlockSpec((tm, w), lambda i, col=col: (i, col))


def _fixed(shape):
    return pl.BlockSpec(shape, lambda i: (0,) * len(shape))


def _rms(x):
    return lax.rsqrt(jnp.mean(x * x, axis=-1, keepdims=True) + EPS)


def _norm_fwd(x, gain, name, width=None, col=0):
    Lp = x.shape[0]
    width = width or x.shape[1]
    tm = _div_tile(Lp, 256, 16)

    def body(x_ref, g_ref, o_ref):
        v = x_ref[...]
        o_ref[...] = (v * _rms(v) * g_ref[...]).astype(o_ref.dtype)

    return pl.pallas_call(
        body, name=name, grid=(Lp // tm,), in_specs=[_rb(tm, width, col), _fixed((1, width))], out_specs=_rb(tm, width),
        out_shape=jax.ShapeDtypeStruct((Lp, width), BF), compiler_params=_cp(('parallel',)),
    )(x, gain)


def _post_residual(h, f, gain, scale, next_gain, name, phases=()):
    Lp, D = h.shape
    tm = _div_tile(Lp, 256, 16)

    def body(h_ref, f_ref, g_ref, ng_ref, o_ref, n_ref):
        v = f_ref[...]
        out = h_ref[...] + scale * (v * _rms(v) * g_ref[...])
        o_ref[...] = out
        n_ref[...] = (out * _rms(out) * ng_ref[...]).astype(n_ref.dtype)

    return _call_carrying(
        body, name, (Lp // tm,), [_rb(tm, D), _rb(tm, D), _fixed((1, D)), _fixed((1, D))], [_rb(tm, D), _rb(tm, D)],
        [jax.ShapeDtypeStruct((Lp, D), F32), jax.ShapeDtypeStruct((Lp, D), BF)], [], (h, f, gain, next_gain), phases)


def _pre_bwd(dres, dn, h, gain, name, phases=(), before=None):
    Lp, D = h.shape
    tm = _div_tile(Lp, 256, 16)

    def accumulate(ref, part):
        @pl.when(pl.program_id(0) == 0)
        def _():
            ref[...] = part

        @pl.when(pl.program_id(0) > 0)
        def _():
            ref[...] += part

    def body(dres_ref, dn_ref, h_ref, g_ref, *rest):
        v = h_ref[...]
        r = _rms(v)
        dy = dn_ref[...]
        w = dy * g_ref[...]
        dh = dres_ref[...] + r * w - v * (r * r * r) * jnp.mean(w * v, axis=-1, keepdims=True)
        if before is None:
            dh_ref, dg_ref = rest
        else:
            f_ref, pg_ref, dh_ref, dg_ref, df_ref, dpg_ref = rest
            fv = f_ref[...]
            fr = _rms(fv)
            fdy = before[2] * dh
            fw = fdy * pg_ref[...]
            df_ref[...] = (fr * fw - fv * (fr * fr * fr) * jnp.mean(fw * fv, axis=-1, keepdims=True)).astype(BF)
            accumulate(dpg_ref, jnp.sum(fdy * fv * fr, axis=0, keepdims=True))
        dh_ref[...] = dh
        accumulate(dg_ref, jnp.sum(dy * v * r, axis=0, keepdims=True))

    in_specs = [_rb(tm, D), _rb(tm, D), _rb(tm, D), _fixed((1, D))]
    out_specs = [_rb(tm, D), _fixed((1, D))]
    out_shape = [jax.ShapeDtypeStruct((Lp, D), F32), jax.ShapeDtypeStruct((1, D), F32)]
    args = (dres, dn, h, gain)
    if before is not None:
        in_specs += [_rb(tm, D), _fixed((1, D))]
        out_specs += [_rb(tm, D), _fixed((1, D))]
        out_shape += [jax.ShapeDtypeStruct((Lp, D), BF), jax.ShapeDtypeStruct((1, D), F32)]
        args += (before[0], before[1])
    return _call_carrying(body, name, (Lp // tm,), in_specs, out_specs, out_shape, [], args, phases)


def _swiglu_fwd(gu, name, phases=()):
    Lp, F2 = gu.shape
    F = F2 // 2
    tm = _div_tile(Lp, 256, 16)

    def body(gu_ref, a_ref):
        g = gu_ref[:, :F].astype(F32)
        u = gu_ref[:, F:].astype(F32)
        a_ref[...] = (g * jax.nn.sigmoid(g) * u).astype(a_ref.dtype)

    return _call_carrying(body, name, (Lp // tm,), [_rb(tm, F2)], [_rb(tm, F)],
                          [jax.ShapeDtypeStruct((Lp, F), BF)], [], (gu,), phases)


def _swiglu_bwd(da, gu, name):
    Lp, F2 = gu.shape
    F = F2 // 2
    tm = _div_tile(Lp, 256, 16)

    def body(da_ref, gu_ref, o_ref):
        g = gu_ref[:, :F].astype(F32)
        u = gu_ref[:, F:].astype(F32)
        da_ = da_ref[...].astype(F32)
        s = jax.nn.sigmoid(g)
        o_ref[:, :F] = (da_ * u * (s * (1.0 + g * (1.0 - s)))).astype(o_ref.dtype)
        o_ref[:, F:] = (da_ * (g * s)).astype(o_ref.dtype)

    return pl.pallas_call(
        body, name=name, grid=(Lp // tm,), in_specs=[_rb(tm, F), _rb(tm, F2)], out_specs=_rb(tm, F2),
        out_shape=jax.ShapeDtypeStruct((Lp, F2), BF), compiler_params=_cp(('parallel',)),
    )(da, gu)


def _tail(h, f, gain, scale, tgt, n_meta, n_real, name):
    Lp, D = h.shape
    tm = _div_tile(Lp, 256, 16)

    def body(h_ref, f_ref, g_ref, t_ref, d_ref, df_ref, dg_ref, l_ref):
        v = f_ref[...]
        r = _rms(v)
        gain_ = g_ref[...]
        row = lax.broadcasted_iota(jnp.int32, (tm, 1), 0) + pl.program_id(0) * tm
        ok = (row >= n_meta) & (row < n_meta + n_real)
        err = jnp.where(ok, h_ref[...] + scale * (v * r * gain_) - t_ref[...], 0.0)
        d_out = err / D
        d_ref[...] = d_out
        dy = scale * d_out
        w = dy * gain_
        df_ref[...] = (r * w - v * (r * r * r) * jnp.mean(w * v, axis=-1, keepdims=True)).astype(df_ref.dtype)
        dg_part = jnp.sum(dy * v * r, axis=0, keepdims=True)
        l_part = jnp.full((1, 128), jnp.sum(err * err), F32)

        @pl.when(pl.program_id(0) == 0)
        def _():
            dg_ref[...] = dg_part
            l_ref[...] = l_part

        @pl.when(pl.program_id(0) > 0)
        def _():
            dg_ref[...] += dg_part
            l_ref[...] += l_part

    return pl.pallas_call(
        body, name=name, grid=(Lp // tm,), in_specs=[_rb(tm, D), _rb(tm, D), _fixed((1, D)), _rb(tm, D)],
        out_specs=[_rb(tm, D), _rb(tm, D), _fixed((1, D)), _fixed((1, 128))],
        out_shape=[jax.ShapeDtypeStruct((Lp, D), F32), jax.ShapeDtypeStruct((Lp, D), BF),
                   jax.ShapeDtypeStruct((1, D), F32), jax.ShapeDtypeStruct((1, 128), F32)],
        compiler_params=_cp(('arbitrary',)),
    )(h, f, gain, tgt)


def _split_bf16(v):
    hi = v.astype(BF)
    return hi, (v - hi.astype(F32)).astype(BF)


def _pool_fwd(z, name):
    Lp = z.shape[0]
    T = _div_tile(Lp, 256, 16)
    G = len(POOL_WINDOWS)

    def body(cur_ref, prev_ref, d_ref):
        i, g = pl.program_id(0), pl.program_id(1)
        w = jnp.left_shift(2, g)
        rr = lax.broadcasted_iota(jnp.int32, (T, T), 0)
        cc = lax.broadcasted_iota(jnp.int32, (T, T), 1)
        b_cur = jnp.where((cc <= rr) & (cc > rr - w), 1.0, 0.0).astype(BF)
        w_prev = jnp.where(i > 0, w, 0)
        b_prev = jnp.where(cc - T > rr - w_prev, 1.0, 0.0).astype(BF)
        u = cur_ref[...]
        s = jnp.zeros((T, POOL_GROUP), F32)
        for part in _split_bf16(u):
            s += jnp.dot(b_cur, part, preferred_element_type=F32)
        for part in _split_bf16(prev_ref[...]):
            s += jnp.dot(b_prev, part, preferred_element_type=F32)
        t = lax.broadcasted_iota(jnp.int32, (T, 1), 0) + i * T
        cnt = jnp.minimum(w, t + 1).astype(F32)
        d_ref[...] = (s / cnt - u).astype(d_ref.dtype)

    return pl.pallas_call(
        body, name=name, grid=(Lp // T, G),
        in_specs=[pl.BlockSpec((T, POOL_GROUP), lambda i, g: (i, g)),
                  pl.BlockSpec((T, POOL_GROUP), lambda i, g: (jnp.maximum(i - 1, 0), g))],
        out_specs=pl.BlockSpec((T, POOL_GROUP), lambda i, g: (i, g)),
        out_shape=jax.ShapeDtypeStruct((Lp, POOL_WIDTH), BF), compiler_params=_cp(('parallel', 'parallel')),
    )(z, z)


def _pool_bwd(dd, name):
    Lp = dd.shape[0]
    T = _div_tile(Lp, 256, 16)
    G = len(POOL_WINDOWS)
    n_t = Lp // T

    def body(cur_ref, next_ref, o_ref):
        i, g = pl.program_id(0), pl.program_id(1)
        w = jnp.left_shift(2, g)
        rr = lax.broadcasted_iota(jnp.int32, (T, T), 0)
        cc = lax.broadcasted_iota(jnp.int32, (T, T), 1)
        b_cur = jnp.where((cc >= rr) & (cc < rr + w), 1.0, 0.0).astype(BF)
        w_next = jnp.where(i < n_t - 1, w, 0)
        b_next = jnp.where(cc + T < rr + w_next, 1.0, 0.0).astype(BF)
        t = lax.broadcasted_iota(jnp.int32, (T, 1), 0) + i * T
        cur = cur_ref[...]
        e_cur = cur / jnp.minimum(w, t + 1).astype(F32)
        e_next = next_ref[...] / jnp.minimum(w, t + T + 1).astype(F32)
        s = jnp.zeros((T, POOL_GROUP), F32)
        for part in _split_bf16(e_cur):
            s += jnp.dot(b_cur, part, preferred_element_type=F32)
        for part in _split_bf16(e_next):
            s += jnp.dot(b_next, part, preferred_element_type=F32)
        o_ref[...] = (s - cur).astype(o_ref.dtype)

    return pl.pallas_call(
        body, name=name, grid=(n_t, G),
        in_specs=[pl.BlockSpec((T, POOL_GROUP), lambda i, g: (i, g)),
                  pl.BlockSpec((T, POOL_GROUP), lambda i, g: (jnp.minimum(i + 1, n_t - 1), g))],
        out_specs=pl.BlockSpec((T, POOL_GROUP), lambda i, g: (i, g)),
        out_shape=jax.ShapeDtypeStruct((Lp, POOL_WIDTH), BF), compiler_params=_cp(('parallel', 'parallel')),
    )(dd, dd)


def _pool_mix_fwd(d, pool_w, scale, name):
    Lp = d.shape[0]
    G = len(POOL_WINDOWS)
    tm = _div_tile(Lp, 1088, 16)

    def body(d_ref, w_ref, s_ref, e_ref, y_ref):
        e = jnp.dot(d_ref[...], w_ref[...], preferred_element_type=F32)
        e_ref[...] = e.astype(e_ref.dtype)
        y_ref[...] = (e * s_ref[...]).astype(y_ref.dtype)

    blk = pl.BlockSpec((tm, POOL_GROUP), lambda g, i: (i, g))
    return pl.pallas_call(
        body, name=name, grid=(G, Lp // tm),
        in_specs=[blk, pl.BlockSpec((None, POOL_GROUP, POOL_GROUP), lambda g, i: (g, 0, 0)),
                  pl.BlockSpec((1, POOL_GROUP), lambda g, i: (0, g))],
        out_specs=[blk, blk], out_shape=[jax.ShapeDtypeStruct((Lp, POOL_WIDTH), BF)] * 2,
        compiler_params=_cp(('parallel', 'parallel')),
    )(d, pool_w, scale)


def _pool_mix_bwd(dyp, e, d, pool_w, scale, name):
    Lp = d.shape[0]
    G = len(POOL_WINDOWS)
    tm = _div_tile(Lp, 1088, 16)

    def body(dy_ref, e_ref, d_ref, w_ref, s_ref, dd_ref, ds_ref, dw_ref):
        i = pl.program_id(1)
        dy = dy_ref[...]
        de = (dy * s_ref[...]).astype(BF)
        dd_ref[...] = lax.dot_general(de, w_ref[...], _DIMS['nt'], preferred_element_type=F32)
        ds_part = jnp.sum(dy * e_ref[...].astype(F32), axis=0, keepdims=True)
        dw_part = lax.dot_general(d_ref[...], de, _DIMS['tn'], preferred_element_type=F32)

        @pl.when(i == 0)
        def _():
            ds_ref[...] = ds_part
            dw_ref[...] = dw_part

        @pl.when(i > 0)
        def _():
            ds_ref[...] += ds_part
            dw_ref[...] += dw_part

    blk = pl.BlockSpec((tm, POOL_GROUP), lambda g, i: (i, g))
    wblk = pl.BlockSpec((None, POOL_GROUP, POOL_GROUP), lambda g, i: (g, 0, 0))
    sblk = pl.BlockSpec((1, POOL_GROUP), lambda g, i: (0, g))
    return pl.pallas_call(
        body, name=name, grid=(G, Lp // tm), in_specs=[blk, blk, blk, wblk, sblk], out_specs=[blk, sblk, wblk],
        out_shape=[jax.ShapeDtypeStruct((Lp, POOL_WIDTH), F32), jax.ShapeDtypeStruct((1, POOL_WIDTH), F32),
                   jax.ShapeDtypeStruct((G, POOL_GROUP, POOL_GROUP), F32)],
        compiler_params=_cp(('parallel', 'arbitrary')),
    )(dyp, e, d, pool_w, scale)


def _rot_half(t):
    lane = lax.broadcasted_iota(jnp.int32, t.shape, 1)
    half = QK_ROPE // 2
    return jnp.where(lane < half, -pltpu.roll(t, 128 - half, 1), pltpu.roll(t, half, 1))


def _lora_norms(z, q_gain, kv_gain, lay, name):
    Lp = z.shape[0]
    QL, KVL = lay['QL'], lay['KVL']
    tm = _div_tile(Lp, 256, 16)

    def body(q_ref, kv_ref, qg_ref, kg_ref, qo_ref, ko_ref):
        a = q_ref[...]
        qo_ref[...] = (a * _rms(a) * qg_ref[...]).astype(BF)
        b = kv_ref[...]
        ko_ref[...] = (b * _rms(b) * kg_ref[...]).astype(BF)

    return pl.pallas_call(
        body, name=name, grid=(Lp // tm,),
        in_specs=[_rb(tm, QL, lay['cq'] // QL), _rb(tm, KVL, lay['ckv'] // KVL), _fixed((1, QL)), _fixed((1, KVL))],
        out_specs=[_rb(tm, QL), _rb(tm, KVL)],
        out_shape=[jax.ShapeDtypeStruct((Lp, QL), BF), jax.ShapeDtypeStruct((Lp, KVL), BF)],
        compiler_params=_cp(('parallel',)),
    )(z, z, q_gain, kv_gain)


def _lora_norms_bwd(dqn, dkn, z, q_gain, kv_gain, lay, name):
    Lp = z.shape[0]
    QL, KVL = lay['QL'], lay['KVL']
    tm = _div_tile(Lp, 256, 16)

    def one(dy, v, gain):
        r = _rms(v)
        w = dy * gain
        return r * w - v * (r * r * r) * jnp.mean(w * v, axis=-1, keepdims=True), jnp.sum(dy * v * r, axis=0, keepdims=True)

    def body(dq_ref, dk_ref, q_ref, kv_ref, qg_ref, kg_ref, o_ref, dqg_ref, dkg_ref):
        da, ga = one(dq_ref[...], q_ref[...], qg_ref[...])
        db, gb = one(dk_ref[...], kv_ref[...], kg_ref[...])
        o_ref[:, :QL] = da.astype(BF)
        o_ref[:, QL:] = db.astype(BF)

        @pl.when(pl.program_id(0) == 0)
        def _():
            dqg_ref[...] = ga
            dkg_ref[...] = gb

        @pl.when(pl.program_id(0) > 0)
        def _():
            dqg_ref[...] += ga
            dkg_ref[...] += gb

    return pl.pallas_call(
        body, name=name, grid=(Lp // tm,),
        in_specs=[_rb(tm, QL), _rb(tm, KVL), _rb(tm, QL, lay['cq'] // QL), _rb(tm, KVL, lay['ckv'] // KVL),
                  _fixed((1, QL)), _fixed((1, KVL))],
        out_specs=[_rb(tm, QL + KVL), _fixed((1, QL)), _fixed((1, KVL))],
        out_shape=[jax.ShapeDtypeStruct((Lp, QL + KVL), BF), jax.ShapeDtypeStruct((1, QL), F32),
                   jax.ShapeDtypeStruct((1, KVL), F32)],
        compiler_params=_cp(('arbitrary',)),
    )(dqn, dkn, z, z, q_gain, kv_gain)


def _qk_prep(q_raw, kv, z, cos, sin, lay, H, name):
    Lp = q_raw.shape[0]
    W = H * HEAD_PAD
    tm = _div_tile(Lp, 256, 16)

    def body(q_ref, kv_ref, kr_ref, c_ref, s_ref, qo_ref, ko_ref):
        c, s = c_ref[...], s_ref[...]

        def rope(t):
            return t * c + _rot_half(t) * s

        kpe = rope(kr_ref[...]).astype(BF)
        for h in range(H):
            b = h * HEAD_PAD
            qo_ref[:, b:b + 128] = (q_ref[:, b:b + 128] * SOFTMAX_SCALE).astype(BF)
            qo_ref[:, b + 128:b + 256] = (rope(q_ref[:, b + 128:b + 256]) * SOFTMAX_SCALE).astype(BF)
            ko_ref[:, b:b + 128] = kv_ref[:, b:b + 128]
            ko_ref[:, b + 128:b + 256] = kpe

    return pl.pallas_call(
        body, name=name, grid=(Lp // tm,),
        in_specs=[_rb(tm, W), _rb(tm, W), _rb(tm, 128, lay['kr'] // 128), _rb(tm, 128), _rb(tm, 128)],
        out_specs=[_rb(tm, W), _rb(tm, W)], out_shape=[jax.ShapeDtypeStruct((Lp, W), BF)] * 2,
        compiler_params=_cp(('parallel',)),
    )(q_raw, kv, z, cos, sin)


def _qk_prep_bwd(dQ, dK, dV, cos, sin, H, name):
    Lp = dQ.shape[0]
    W = H * HEAD_PAD
    tm = _div_tile(Lp, 256, 16)

    def body(dq_ref, dk_ref, dv_ref, c_ref, s_ref, qo_ref, kvo_ref, kro_ref):
        c, s = c_ref[...], s_ref[...]

        def unrope(t):
            return t * c - _rot_half(t * s)

        acc = jnp.zeros((tm, 128), F32)
        for h in range(H):
            b = h * HEAD_PAD
            qo_ref[:, b:b + 128] = (dq_ref[:, b:b + 128] * SOFTMAX_SCALE).astype(BF)
            qo_ref[:, b + 128:b + 256] = (unrope(dq_ref[:, b + 128:b + 256]) * SOFTMAX_SCALE).astype(BF)
            kvo_ref[:, b:b + 128] = dk_ref[:, b:b + 128].astype(BF)
            kvo_ref[:, b + 128:b + 256] = dv_ref[:, h * V_DIM:(h + 1) * V_DIM]
            acc += dk_ref[:, b + 128:b + 256]
        kro_ref[...] = unrope(acc).astype(BF)

    return pl.pallas_call(
        body, name=name, grid=(Lp // tm,),
        in_specs=[_rb(tm, W), _rb(tm, W), _rb(tm, H * V_DIM), _rb(tm, 128), _rb(tm, 128)],
        out_specs=[_rb(tm, W), _rb(tm, W), _rb(tm, 128)],
        out_shape=[jax.ShapeDtypeStruct((Lp, W), BF), jax.ShapeDtypeStruct((Lp, W), BF),
                   jax.ShapeDtypeStruct((Lp, 128), BF)],
        compiler_params=_cp(('parallel',)),
    )(dQ, dK, dV, cos, sin)


def _call_carrying(core, name, grid, in_specs, out_specs, out_shape, scratch, args, phases):
    n_in, n_out, n_scr = len(in_specs), len(out_specs), len(scratch)
    extra_in, extra_out, alias, sems = _phase_operands(phases, n_in, n_out)
    n_ei, n_eo = len(extra_in), len(extra_out)

    def body(*refs):
        ins, ein = refs[:n_in], refs[n_in:n_in + n_ei]
        outs = refs[n_in + n_ei:n_in + n_ei + n_out]
        eout = refs[n_in + n_ei + n_out:n_in + n_ei + n_out + n_eo]
        rest = refs[n_in + n_ei + n_out + n_eo:]
        scr, sem_refs = rest[:n_scr], rest[n_scr:]
        if phases:
            ids = [pl.program_id(d) for d in range(len(grid))]
            first, last = ids[0] == 0, ids[0] == grid[0] - 1
            for d in range(1, len(grid)):
                first, last = first & (ids[d] == 0), last & (ids[d] == grid[d] - 1)

            @pl.when(first)
            def _():
                _phase_start(phases, ein, eout, sem_refs)

        core(*ins, *outs, *scr)
        if phases:
            @pl.when(last)
            def _():
                _phase_finish(phases, ein, eout, sem_refs)

    outs = pl.pallas_call(
        body, name=name, grid=grid, in_specs=list(in_specs) + [ANY] * n_ei, out_specs=list(out_specs) + [ANY] * n_eo,
        out_shape=list(out_shape) + extra_out, input_output_aliases=alias, scratch_shapes=list(scratch) + sems,
        compiler_params=_cp(('arbitrary',) * len(grid)),
    )(*args, *extra_in)
    return list(outs[:n_out]), _phase_results(phases, outs[n_out:])


def _flash_fwd(Q, K, kv, H, name, phases=()):
    Lp = Q.shape[0]
    T = _div_tile(Lp, 256, 16)

    n_t = Lp // T
    FC = min(FLASH_FWD_CHUNK, n_t)
    CH = FC * T

    def body(q_ref, k_ref, v_ref, o_ref, lse_ref, m_s, l_s, acc_s):
        i = pl.program_id(1)
        q = q_ref[...]
        m_s[...] = jnp.full((T, 1), NEG, F32)
        l_s[...] = jnp.zeros((T, 1), F32)
        acc_s[...] = jnp.zeros((T, V_DIM), F32)

        def step(start, width, masked):
            parts = [(start, width, False)] if not masked else (
                ([(start, width - T, False)] if width > T else []) + [(start + (width - T), T, True)])
            scores = []
            for at, wd, diag in parts:
                s = lax.dot_general(q, k_ref[pl.ds(at, wd), :], _DIMS['nt'], preferred_element_type=F32)
                if diag:
                    rr = lax.broadcasted_iota(jnp.int32, (T, T), 0)
                    cc = lax.broadcasted_iota(jnp.int32, (T, T), 1)
                    s = jnp.where(cc <= rr, s, NEG)
                scores.append(s)
            m = m_s[...]
            m_new = m
            for s in scores:
                m_new = jnp.maximum(m_new, jnp.max(s, axis=-1, keepdims=True))
            alpha = jnp.exp(m - m_new)
            l = alpha * l_s[...]
            acc = alpha * acc_s[...]
            for (at, wd, _), s in zip(parts, scores):
                p = jnp.exp(s - m_new)
                l = l + jnp.sum(p, axis=-1, keepdims=True)
                acc = acc + jnp.dot(p.astype(BF), v_ref[pl.ds(at, wd), :], preferred_element_type=F32)
            l_s[...] = l
            acc_s[...] = acc
            m_s[...] = m_new

        n_full = i // FC

        def full(cidx, carry):
            step(pl.multiple_of(cidx * CH, CH), CH, False)
            return carry

        if n_t > FC:
            lax.fori_loop(0, n_full, full, 0)
        for nb in range(1, FC + 1):
            @pl.when(i % FC == nb - 1)
            def _(nb=nb):
                step(pl.multiple_of(n_full * CH, CH), nb * T, True)

        l = l_s[...]
        o_ref[...] = (acc_s[...] / l).astype(o_ref.dtype)
        lse_ref[...] = jnp.broadcast_to(m_s[...] + jnp.log(l), (T, 128))

    return _call_carrying(
        body, name, (H, Lp // T),
        [pl.BlockSpec((T, HEAD_PAD), lambda h, i: (i, h)), pl.BlockSpec((Lp, HEAD_PAD), lambda h, i: (0, h)),
         pl.BlockSpec((Lp, V_DIM), lambda h, i: (0, 2 * h + 1))],
        [pl.BlockSpec((T, V_DIM), lambda h, i: (i, h)), pl.BlockSpec((None, T, 128), lambda h, i: (h, i, 0))],
        [jax.ShapeDtypeStruct((Lp, H * V_DIM), BF), jax.ShapeDtypeStruct((H, Lp, 128), F32)],
        [pltpu.VMEM((T, 1), F32), pltpu.VMEM((T, 1), F32), pltpu.VMEM((T, V_DIM), F32)], (Q, K, kv), phases)


def _flash_bwd(Q, K, kv, O, dO, lse, H, name, phases=()):
    Lp = Q.shape[0]
    T = _div_tile(Lp, 256, 16)
    n_t = Lp // T
    FC = min(FLASH_BWD_CHUNK, n_t)

    def body(q_ref, k_ref, v_ref, o_ref, do_ref, lse_ref, dq_ref, dk_ref, dv_ref, dk_acc, dv_acc):
        j = pl.program_id(1)

        @pl.when(j == 0)
        def _():
            dq_ref[...] = jnp.zeros_like(dq_ref)

        kj, vj = k_ref[...], v_ref[...]
        dk_acc[...] = jnp.zeros_like(dk_acc)
        dv_acc[...] = jnp.zeros_like(dv_acc)

        def step(start, width, masked):
            rows = pl.ds(start, width)
            qi, doi = q_ref[rows, :], do_ref[rows, :]
            delta = jnp.sum(doi.astype(F32) * o_ref[rows, :].astype(F32), axis=-1, keepdims=True)
            s = lax.dot_general(qi, kj, _DIMS['nt'], preferred_element_type=F32)
            p = jnp.exp(s - lse_ref[rows, :][:, :1])
            if masked:
                rr = lax.broadcasted_iota(jnp.int32, (width, T), 0) + start
                cc = lax.broadcasted_iota(jnp.int32, (width, T), 1) + j * T
                p = jnp.where(cc <= rr, p, 0.0)
            dp = lax.dot_general(doi, vj, _DIMS['nt'], preferred_element_type=F32)
            ds = (p * (dp - delta)).astype(BF)
            dv_acc[...] += lax.dot_general(p.astype(BF), doi, _DIMS['tn'], preferred_element_type=F32)
            dk_acc[...] += lax.dot_general(ds, qi, _DIMS['tn'], preferred_element_type=F32)
            dq_ref[rows, :] += jnp.dot(ds, kj, preferred_element_type=F32)

        head = (n_t - 1 - j) % FC + 1
        for nb in range(1, FC + 1):
            @pl.when(head == nb)
            def _(nb=nb):
                step(pl.multiple_of(j * T, T), nb * T, True)

        def full(cidx, carry):
            step(pl.multiple_of((j + head + cidx * FC) * T, T), FC * T, False)
            return carry

        if n_t > FC:
            lax.fori_loop(0, (n_t - j - head) // FC, full, 0)
        dk_ref[...] = dk_acc[...]
        dv_ref[...] = dv_acc[...].astype(dv_ref.dtype)

    head_q = pl.BlockSpec((Lp, HEAD_PAD), lambda h, j: (0, h))
    head_v = pl.BlockSpec((Lp, V_DIM), lambda h, j: (0, h))
    return _call_carrying(
        body, name, (H, n_t),
        [head_q, pl.BlockSpec((T, HEAD_PAD), lambda h, j: (j, h)), pl.BlockSpec((T, V_DIM), lambda h, j: (j, 2 * h + 1)),
         head_v, head_v, pl.BlockSpec((None, Lp, 128), lambda h, j: (h, 0, 0))],
        [head_q, pl.BlockSpec((T, HEAD_PAD), lambda h, j: (j, h)), pl.BlockSpec((T, V_DIM), lambda h, j: (j, h))],
        [jax.ShapeDtypeStruct((Lp, H * HEAD_PAD), F32), jax.ShapeDtypeStruct((Lp, H * HEAD_PAD), F32),
         jax.ShapeDtypeStruct((Lp, H * V_DIM), BF)],
        [pltpu.VMEM((T, HEAD_PAD), F32), pltpu.VMEM((T, V_DIM), F32)], (Q, K, kv, O, dO, lse), phases)


def _gate_fwd(z, y_pool, y_mla, lay, name):
    Lp, D = y_pool.shape
    tm = _div_tile(Lp, 256, 16)

    def body(gp_ref, gm_ref, yp_ref, ym_ref, o_ref):
        o_ref[...] = (jax.nn.sigmoid(gp_ref[...]) * yp_ref[...] + jax.nn.sigmoid(gm_ref[...]) * ym_ref[...]).astype(BF)

    return pl.pallas_call(
        body, name=name, grid=(Lp // tm,),
        in_specs=[_rb(tm, D, lay['gp'] // D), _rb(tm, D, lay['gm'] // D), _rb(tm, D), _rb(tm, D)], out_specs=_rb(tm, D),
        out_shape=jax.ShapeDtypeStruct((Lp, D), BF), compiler_params=_cp(('parallel',)),
    )(z, z, y_pool, y_mla)


def _gate_bwd(dy, z, y_pool, y_mla, lay, name):
    Lp, D = y_pool.shape
    tm = _div_tile(Lp, 256, 16)

    def body(dy_ref, gp_ref, gm_ref, yp_ref, ym_ref, dp_ref, dm_ref, dg_ref):
        dy_ = dy_ref[...]
        sp, sm = jax.nn.sigmoid(gp_ref[...]), jax.nn.sigmoid(gm_ref[...])
        dp_ref[...] = (dy_ * sp).astype(BF)
        dm_ref[...] = (dy_ * sm).astype(BF)
        dg_ref[:, :D] = (dy_ * yp_ref[...] * (sp * (1.0 - sp))).astype(BF)
        dg_ref[:, D:] = (dy_ * ym_ref[...] * (sm * (1.0 - sm))).astype(BF)

    return pl.pallas_call(
        body, name=name, grid=(Lp // tm,),
        in_specs=[_rb(tm, D), _rb(tm, D, lay['gp'] // D), _rb(tm, D, lay['gm'] // D), _rb(tm, D), _rb(tm, D)],
        out_specs=[_rb(tm, D), _rb(tm, D), _rb(tm, 2 * D)],
        out_shape=[jax.ShapeDtypeStruct((Lp, D), BF), jax.ShapeDtypeStruct((Lp, D), BF),
                   jax.ShapeDtypeStruct((Lp, 2 * D), BF)],
        compiler_params=_cp(('parallel',)),
    )(dy, z, z, y_pool, y_mla)


def _z_layout(D, QL, KVL):
    cq = POOL_WIDTH
    ckv = cq + QL
    gp = -(-(ckv + KVL) // D) * D
    gm = gp + D
    kr = gm + D
    return dict(QL=QL, KVL=KVL, cq=cq, ckv=ckv, gp=gp, gm=gm, kr=kr, width=kr + 128)


def _w_in_aligned(wt, lay, D):
    n0 = POOL_WIDTH + lay['QL'] + lay['KVL']
    parts = [wt[:n0], jnp.zeros((lay['gp'] - n0, D), wt.dtype), wt[n0 + QK_ROPE:], wt[n0:n0 + QK_ROPE],
             jnp.zeros((128 - QK_ROPE, D), wt.dtype)]
    return jnp.concatenate(parts, axis=0)


def _w_in_logical(wt_al, lay, D):
    n0 = POOL_WIDTH + lay['QL'] + lay['KVL']
    return jnp.concatenate([wt_al[:n0], wt_al[lay['kr']:lay['kr'] + QK_ROPE], wt_al[lay['gp']:lay['gp'] + 2 * D]],
                           axis=0)


def kernel(x, meta_tokens, norm_ffn1_pre, norm_ffn1_post, ffn1_w_gu, ffn1_w_down, norm_mix_pre, norm_mix_post, w_in, pool_w, pool_scale, w_pool_o, q_a_norm, w_q_b, kv_a_norm, w_kv_b, w_mla_o, w_out, norm_ffn2_pre, norm_ffn2_post, ffn2_w_gu, ffn2_w_down, loss_target, m_meta_tokens, m_norm_ffn1_pre, m_norm_ffn1_post, m_ffn1_w_gu, m_ffn1_w_down, m_norm_mix_pre, m_norm_mix_post, m_w_in, m_pool_w, m_pool_scale, m_w_pool_o, m_q_a_norm, m_w_q_b, m_kv_a_norm, m_w_kv_b, m_w_mla_o, m_w_out, m_norm_ffn2_pre, m_norm_ffn2_post, m_ffn2_w_gu, m_ffn2_w_down, v_meta_tokens, v_norm_ffn1_pre, v_norm_ffn1_post, v_ffn1_w_gu, v_ffn1_w_down, v_norm_mix_pre, v_norm_mix_post, v_w_in, v_pool_w, v_pool_scale, v_w_pool_o, v_q_a_norm, v_w_q_b, v_kv_a_norm, v_w_kv_b, v_w_mla_o, v_w_out, v_norm_ffn2_pre, v_norm_ffn2_post, v_ffn2_w_gu, v_ffn2_w_down):
    given = dict(locals())
    W = {n: given[n] for n in WEIGHTS}
    M = {n: given['m_' + n] for n in WEIGHTS}
    V = {n: given['v_' + n] for n in WEIGHTS}

    S, D = x.shape[1], x.shape[2]
    NM = meta_tokens.shape[0]
    L = NM + S
    Lp = -(-L // ROW_ALIGN) * ROW_ALIGN
    QL, KVL = w_q_b.shape[1], w_kv_b.shape[1]
    H = w_q_b.shape[2] * N_CHIPS // QK_DIM
    lay = _z_layout(D, QL, KVL)
    gx, gy = lax.axis_index('x'), lax.axis_index('y')
    chip = 2 * gx + gy

    names = list(BIG)
    for d in (W, M, V):
        d['w_in'] = jnp.swapaxes(d['w_in'], 1, 2)
    slab = {n: W[n][0] for n in names}
    kind = dict(BIG, meta_tokens='col')
    slab_shape = {n: tuple(slab[n].shape) for n in names}
    gc = lax.axis_index('c')
    c_arr = jnp.stack([gc]).astype(jnp.int32)
    place_arr = jnp.stack([gc, chip]).astype(jnp.int32)

    full = {n: _cast_place(slab[n], BIG[n], place_arr, BF, f'place_{n}') for n in names}
    full['meta_tokens'] = _cast_place(meta_tokens, 'col', place_arr, F32, 'place_meta_tokens')
    G0 = ['ffn1_w_gu', 'meta_tokens']
    G0B = ['ffn1_w_down']
    G1W = ['w_in']
    G1R = ['pool_w', 'w_pool_o', 'w_q_b', 'w_kv_b']
    G2 = ['w_mla_o', 'w_out']
    G3 = ['ffn2_w_gu', 'ffn2_w_down']

    def gather(phase_fn, group):
        return phase_fn([full[n] for n in group], [kind[n] for n in group])

    def arrived(group, res):
        full.update(zip(group, res))

    arrived(G0, _all_gather([full[n] for n in G0], [kind[n] for n in G0]))
    meta_full = full['meta_tokens']

    pos = jnp.arange(Lp, dtype=F32)
    inv = ROPE_THETA ** (-jnp.arange(0, QK_ROPE, 2, dtype=F32) / QK_ROPE)
    ang = pos[:, None] * inv[None, :]
    ang = jnp.concatenate([ang, ang], axis=-1)
    cos = jnp.pad(jnp.cos(ang), ((0, 0), (0, 128 - QK_ROPE)), constant_values=1.0)
    sin = jnp.pad(jnp.sin(ang), ((0, 0), (0, 128 - QK_ROPE)))

    h0 = jnp.concatenate([meta_full, x[0], jnp.zeros((Lp - L, D), F32)], axis=0)
    tgt = jnp.pad(loss_target[0], ((NM, Lp - L), (0, 0)))

    n1 = _norm_fwd(h0, norm_ffn1_pre, 'ffn1_norm')
    gu1, (res_b, res) = _mm(n1, full['ffn1_w_gu'], 'nn', BF, 'ffn1_gu',
                            [gather(_phase_gather_ici, G0B), gather(_phase_gather_ici, G1R)])
    arrived(G0B, res_b)
    arrived(G1R, res)
    def gather_ici(group, peers):
        return _phase_gather_ici([full[n] for n in group], [kind[n] for n in group], peers)

    (a1,), (res_b,) = _swiglu_fwd(gu1, 'ffn1_act', [gather(_phase_gather_d2d, G0B)])
    arrived(G0B, res_b)
    f1, (res, res_w) = _mm(a1, full['ffn1_w_down'], 'nn', F32, 'ffn1_down',
                           [gather(_phase_gather_d2d, G1R), gather_ici(G1W, (1, 2, 3))])
    arrived(G1R, res)
    arrived(G1W, res_w)
    (h1, n2), (res_w,) = _post_residual(h0, f1, norm_ffn1_post, 0.5, norm_mix_pre, 'ffn1_res',
                                        [gather(_phase_gather_d2d, G1W)])
    arrived(G1W, res_w)

    w_in_al = _w_in_aligned(full['w_in'], lay, D)
    w_q_pad = jnp.pad(full['w_q_b'].reshape(QL, H, QK_DIM), ((0, 0), (0, 0), (0, HEAD_PAD - QK_DIM))).reshape(
        QL, H * HEAD_PAD)

    z, (res,) = _mm(n2, w_in_al, 'nt', F32, 'mix_in', [gather(_phase_gather_ici, G2)])
    arrived(G2, res)
    d_pool = _pool_fwd(z, 'pool_fwd')
    e_pool, yp = _pool_mix_fwd(d_pool, full['pool_w'], pool_scale, 'pool_mix')
    y_pool = _mm(yp, full['w_pool_o'], 'nn', F32, 'pool_out')
    cqn, ckvn = _lora_norms(z, q_a_norm, kv_a_norm, lay, 'lora_norms')
    q_raw = _mm(cqn, w_q_pad, 'nn', F32, 'mla_q')
    kv = _mm(ckvn, full['w_kv_b'], 'nn', BF, 'mla_kv')
    Q, K = _qk_prep(q_raw, kv, z, cos, sin, lay, H, 'qk_prep')
    (O, lse), (res2, res3) = _flash_fwd(Q, K, kv, H, 'flash_fwd',
                                        [gather(_phase_gather_d2d, G2), gather(_phase_gather_ici, G3)])
    arrived(G2, res2)
    arrived(G3, res3)
    y_mla, (res,) = _mm(O, full['w_mla_o'], 'nn', F32, 'mla_out', [gather(_phase_gather_d2d, G3)])
    arrived(G3, res)
    y = _gate_fwd(z, y_pool, y_mla, lay, 'gate')
    m_mix = _mm(y, full['w_out'], 'nn', F32, 'mix_out')
    (h2, n3), _ = _post_residual(h1, m_mix, norm_mix_post, 1.0, norm_ffn2_pre, 'mix_res')

    gu2 = _mm(n3, full['ffn2_w_gu'], 'nn', BF, 'ffn2_gu')
    (a2,), _ = _swiglu_fwd(gu2, 'ffn2_act')
    f2 = _mm(a2, full['ffn2_w_down'], 'nn', F32, 'ffn2_down')

    G, theirs, sums, halves, reduced = {}, {}, {}, {}, {}
    RA = ['ffn2_w_down', 'ffn2_w_gu']
    RB1 = ['w_out', 'w_pool_o', 'pool_w', 'w_mla_o']
    RB2 = ['w_q_b', 'w_kv_b']
    RB3 = ['w_in']
    RC1 = ['ffn1_w_down']
    RC2 = ['ffn1_w_gu']

    def pair_phase(group):
        return _phase_pair([G[n] for n in group], [BIG[n] for n in group], [slab_shape[n] for n in group])

    def pair_sums(group, res):
        for n, t in zip(group, res):
            sums[n] = _pair_sum(G[n], t, BIG[n], slab_shape[n], c_arr, f'rs_pair_sum_{n}')

    def chip_phase(group):
        return _phase_chip([sums[n] for n in group])

    def chip_sums(group, res):
        for n, ld in zip(group, res):
            halves[n] = _chip_sum(sums[n], ld, BIG[n], slab_shape[n], place_arr, f'rs_chip_sum_{n}')

    def final_phase(group):
        return _phase_final([halves[n] for n in group], [BIG[n] for n in group])

    dh3, df2, G['norm_ffn2_post'], sq = _tail(h2, f2, norm_ffn2_post, 0.5, tgt, NM, S, 'ffn2_tail')

    G['ffn2_w_down'] = _mm(a2, df2, 'tn', BF, 'ffn2_dw_down')
    da2 = _mm(df2, full['ffn2_w_down'], 'nt', BF, 'ffn2_da')
    dgu2 = _swiglu_bwd(da2, gu2, 'ffn2_act_bwd')
    G['ffn2_w_gu'] = _mm(n3, dgu2, 'tn', BF, 'ffn2_dw_gu')
    dn3, (res,) = _mm(dgu2, full['ffn2_w_gu'], 'nt', F32, 'ffn2_dn', [pair_phase(RA)])
    (dh2, G['norm_ffn2_pre'], dm, G['norm_mix_post']), _ = _pre_bwd(
        dh3, dn3, h2, norm_ffn2_pre, 'ffn2_norm_bwd', before=(m_mix, norm_mix_post, 1.0))
    pair_sums(RA, res)

    G['w_out'] = _mm(y, dm, 'tn', BF, 'dw_out')
    dy = _mm(dm, full['w_out'], 'nt', F32, 'mix_out_bwd')
    dy_pool, dy_mla, d_gate = _gate_bwd(dy, z, y_pool, y_mla, lay, 'gate_bwd')
    G['w_pool_o'] = _mm(yp, dy_pool, 'tn', BF, 'dw_pool_o')
    dyp = _mm(dy_pool, full['w_pool_o'], 'nt', F32, 'pool_out_bwd')
    dd, G['pool_scale'], d_pool_w = _pool_mix_bwd(dyp, e_pool, d_pool, full['pool_w'], pool_scale, 'pool_mix_bwd')
    G['pool_w'] = d_pool_w.astype(BF)
    du_pool = _pool_bwd(dd, 'pool_bwd')
    G['w_mla_o'] = _mm(O, dy_mla, 'tn', BF, 'dw_mla_o')
    dO, (res,) = _mm(dy_mla, full['w_mla_o'], 'nt', BF, 'mla_out_bwd', [pair_phase(RB1)])
    pair_sums(RB1, res)
    (dQ, dK, dV), (res, res_b) = _flash_bwd(Q, K, kv, O, dO, lse, H, 'flash_bwd', [chip_phase(RA), chip_phase(RB1)])
    chip_sums(RA, res)
    chip_sums(RB1, res_b)
    dq_raw, dkv, dkr = _qk_prep_bwd(dQ, dK, dV, cos, sin, H, 'qk_prep_bwd')
    d_w_q_pad, (res_b,) = _mm(cqn, dq_raw, 'tn', BF, 'dw_q_b', [final_phase(RB1)])
    reduced.update(zip(RB1, res_b))
    G['w_q_b'] = d_w_q_pad.reshape(QL, H, HEAD_PAD)[:, :, :QK_DIM].reshape(QL, H * QK_DIM)
    dcqn = _mm(dq_raw, w_q_pad, 'nt', F32, 'mla_q_bwd')
    G['w_kv_b'] = _mm(ckvn, dkv, 'tn', BF, 'dw_kv_b')
    dckvn, (res,) = _mm(dkv, full['w_kv_b'], 'nt', F32, 'mla_kv_bwd', [pair_phase(RB2)])
    pair_sums(RB2, res)
    d_lora, G['q_a_norm'], G['kv_a_norm'] = _lora_norms_bwd(dcqn, dckvn, z, q_a_norm, kv_a_norm, lay, 'lora_norms_bwd')
    n0 = POOL_WIDTH + QL + KVL
    dz = jnp.concatenate([du_pool, d_lora, jnp.zeros((Lp, lay['gp'] - n0), BF), d_gate, dkr], axis=1)
    d_w_in_al, (res, res_b) = _mm(dz, n2, 'tn', BF, 'dw_in', [final_phase(RA), chip_phase(RB2)])
    reduced.update(zip(RA, res))
    chip_sums(RB2, res_b)
    G['w_in'] = _w_in_logical(d_w_in_al, lay, D)
    dn2, (res, res_b) = _mm(dz, w_in_al, 'nn', F32, 'mix_in_bwd', [pair_phase(RB3), final_phase(RB2)])
    reduced.update(zip(RB2, res_b))
    (dh1, G['norm_mix_pre'], df1, G['norm_ffn1_post']), _ = _pre_bwd(
        dh2, dn2, h1, norm_mix_pre, 'mix_norm_bwd', before=(f1, norm_ffn1_post, 0.5))
    pair_sums(RB3, res)

    sums_b3 = [sums[n] for n in RB3]
    G['ffn1_w_down'], (land_b,) = _mm(a1, df1, 'tn', BF, 'ffn1_dw_down', [_phase_chip(sums_b3, (1, 2))])
    da1, (res, land_b) = _mm(df1, full['ffn1_w_down'], 'nt', BF, 'ffn1_da',
                             [pair_phase(RC1), _phase_chip(sums_b3, (3,), land_b)])
    chip_sums(RB3, land_b)
    dgu1 = _swiglu_bwd(da1, gu1, 'ffn1_act_bwd')
    pair_sums(RC1, res)
    G['ffn1_w_gu'], (res_c1, res_b) = _mm(n1, dgu1, 'tn', BF, 'ffn1_dw_gu', [chip_phase(RC1), final_phase(RB3)])
    chip_sums(RC1, res_c1)
    reduced.update(zip(RB3, res_b))
    (res,) = _run_phases([pair_phase(RC2)], 'rs_pair_exchange_tail')
    pair_sums(RC2, res)
    dn1, (res_c2, res_c1) = _mm(dgu1, full['ffn1_w_gu'], 'nt', F32, 'ffn1_dn', [chip_phase(RC2), final_phase(RC1)])
    chip_sums(RC2, res_c2)
    reduced.update(zip(RC1, res_c1))
    (dh0, G['norm_ffn1_pre']), (res_c2,) = _pre_bwd(dh1, dn1, h0, norm_ffn1_pre, 'ffn1_norm_bwd', [final_phase(RC2)])
    reduced.update(zip(RC2, res_c2))
    grad_x = dh0[NM:L][None]

    SW = max(D, POOL_WIDTH)

    def widen(a, fill=0.0):
        return jnp.pad(a, ((0, 0), (0, SW - a.shape[1])), constant_values=fill)

    rows = [widen(G[n]) for n in SMALL_VEC] + [widen(dh0[:NM]), widen(sq)]
    n_rows = len(SMALL_VEC) + NM + 1
    pad_rows = -(-n_rows // 8) * 8 - n_rows
    small = _all_reduce_small(jnp.concatenate(rows + [jnp.zeros((pad_rows, SW), F32)], axis=0))
    loss = (0.5 / D) * small[len(SMALL_VEC) + NM, 0]
    for i, n in enumerate(SMALL_VEC):
        reduced[n] = small[i:i + 1, :G[n].shape[1]]
    mw = meta_tokens.shape[1]
    reduced['meta_tokens'] = lax.dynamic_slice(small, (len(SMALL_VEC), chip * mw), (NM, mw))

    grads, deltas, new_m, new_v = {}, {}, {}, {}
    for n in names:
        grads[n], deltas[n], new_m[n], new_v[n] = _adam(W[n], reduced[n][None], M[n], V[n], f'adam_{n}')
    for d in (grads, deltas, new_m, new_v):
        d['w_in'] = jnp.swapaxes(d['w_in'], 1, 2)
    n_vec = len(SMALL_VEC)
    vec_w = jnp.concatenate([widen(W[n]) for n in SMALL_VEC] + [jnp.zeros((16 - n_vec, SW), F32)], axis=0)
    vec_m = jnp.concatenate([widen(M[n]) for n in SMALL_VEC] + [jnp.zeros((16 - n_vec, SW), F32)], axis=0)
    vec_v = jnp.concatenate([widen(V[n], 1.0) for n in SMALL_VEC] + [jnp.ones((16 - n_vec, SW), F32)], axis=0)
    vec_g = jnp.concatenate([small[:n_vec], jnp.zeros((16 - n_vec, SW), F32)], axis=0)
    _, vd, vm, vv = _adam(vec_w, vec_g, vec_m, vec_v, 'adam_vectors')
    for i, n in enumerate(SMALL_VEC):
        wdt = W[n].shape[1]
        grads[n], deltas[n], new_m[n], new_v[n] = reduced[n], vd[i:i + 1, :wdt], vm[i:i + 1, :wdt], vv[i:i + 1, :wdt]
    grads['meta_tokens'], deltas['meta_tokens'], new_m['meta_tokens'], new_v['meta_tokens'] = _adam(
        meta_tokens, reduced['meta_tokens'], m_meta_tokens, v_meta_tokens, 'adam_meta')

    return (loss, grad_x, *[grads[n] for n in WEIGHTS], *[deltas[n] for n in WEIGHTS], *[new_m[n] for n in WEIGHTS],
            *[new_v[n] for n in WEIGHTS])
```

```python
import functools

import jax
import jax.numpy as jnp
import numpy as np
from jax import lax
from jax.experimental import pallas as pl
from jax.experimental.pallas import tpu as pltpu

F32 = jnp.float32
BF = jnp.bfloat16
MESH = pl.DeviceIdType.MESH

EPS = 1e-6
N_CHIPS = 4
POOL_WINDOWS = (2, 4, 8, 16)
POOL_GROUP = 256
POOL_WIDTH = POOL_GROUP * len(POOL_WINDOWS)
QK_NOPE = 128
QK_ROPE = 64
V_DIM = 128
QK_DIM = QK_NOPE + QK_ROPE
HEAD_PAD = 256
ROPE_THETA = 10000.0
SOFTMAX_SCALE = QK_DIM ** -0.5
ADAM_LR = 0.001
ADAM_B1 = 0.9
ADAM_B2 = 0.999
ADAM_EPS = 1e-08
ADAM_WD = 0.01
ADAM_STEP = 10
ROW_ALIGN = 256
VMEM_LIMIT = 56 * 1024 * 1024
MM_TK_ROWS = 4352
MM_TK_COLS = 6400
NEG = -1e30
FLASH_FWD_CHUNK = 17
FLASH_BWD_CHUNK = 8
WEIGHTS = ['meta_tokens', 'norm_ffn1_pre', 'norm_ffn1_post', 'ffn1_w_gu', 'ffn1_w_down', 'norm_mix_pre',
           'norm_mix_post', 'w_in', 'pool_w', 'pool_scale', 'w_pool_o', 'q_a_norm', 'w_q_b', 'kv_a_norm', 'w_kv_b',
           'w_mla_o', 'w_out', 'norm_ffn2_pre', 'norm_ffn2_post', 'ffn2_w_gu', 'ffn2_w_down']
BIG = {'ffn1_w_gu': 'col', 'ffn1_w_down': 'row', 'w_in': 'row', 'pool_w': 'pool', 'w_pool_o': 'col', 'w_q_b': 'col',
       'w_kv_b': 'col', 'w_mla_o': 'row', 'w_out': 'row', 'ffn2_w_gu': 'col', 'ffn2_w_down': 'row'}
SMALL_VEC = ['norm_ffn1_pre', 'norm_ffn1_post', 'norm_mix_pre', 'norm_mix_post', 'norm_ffn2_pre', 'norm_ffn2_post',
             'pool_scale', 'q_a_norm', 'kv_a_norm']


def _div_tile(n, target, align):
    best = None
    for t in range(align, min(n, target) + 1, align):
        if n % t == 0:
            best = t
    return best if best is not None else n


def _cp(sem=None):
    return pltpu.CompilerParams(dimension_semantics=sem, vmem_limit_bytes=VMEM_LIMIT)


def _full_shape(kind, slab):
    if kind == 'col':
        return (slab[0], slab[1] * N_CHIPS)
    if kind == 'row':
        return (slab[0] * N_CHIPS, slab[1])
    return (slab[0], slab[1] * N_CHIPS, slab[2])


def _half_shape(kind, slab):
    if kind == 'pool':
        return (slab[0], slab[1] // 2, slab[2])
    if kind == 'row':
        return (slab[0], slab[1] // 2)
    return (slab[0] // 2, slab[1])


def _half_of_slab(ref, kind, c):
    if kind == 'pool':
        n = ref.shape[1] // 2
        return ref.at[:, pl.ds(c * n, n), :]
    if kind == 'row':
        n = ref.shape[1] // 2
        return ref.at[:, pl.ds(c * n, n)]
    n = ref.shape[0] // 2
    return ref.at[pl.ds(c * n, n), :]


def _piece(ref, kind, k, c):
    if kind == 'col':
        r, w = ref.shape[0] // 2, ref.shape[1] // N_CHIPS
        return ref.at[pl.ds(c * r, r), pl.ds(k * w, w)]
    if kind == 'row':
        r, w = ref.shape[0] // N_CHIPS, ref.shape[1] // 2
        return ref.at[pl.ds(k * r, r), pl.ds(c * w, w)]
    r = ref.shape[1] // (2 * N_CHIPS)
    return ref.at[:, pl.ds((2 * k + c) * r, r), :]


def _place():
    x, y, c = lax.axis_index('x'), lax.axis_index('y'), lax.axis_index('c')
    return x, y, c


def _peer_chip(x, y, j):
    px = 1 - x if (j >> 1) else x
    py = 1 - y if (j & 1) else y
    return px, py


ANY = pl.BlockSpec(memory_space=pl.ANY)


def _all_gather(bufs, kinds):
    n = len(bufs)

    def body(*refs):
        outs = refs[n:2 * n]
        ssem, rsem, fssem, frsem = refs[2 * n:]
        x, y, c = _place()
        me = 2 * x + y
        sib = (x, y, 1 - c)
        sends = []
        for w in range(n):
            for j in (1, 2, 3):
                px, py = _peer_chip(x, y, j)
                mine = _piece(outs[w], kinds[w], me, c)
                sends.append(pltpu.make_async_remote_copy(
                    src_ref=mine, dst_ref=mine, send_sem=ssem.at[w, j - 1], recv_sem=rsem.at[w, j - 1],
                    device_id=(px, py, c), device_id_type=MESH))
        for cp in sends:
            cp.start()
        fwds = []
        for w in range(n):
            for j in (1, 2, 3):
                px, py = _peer_chip(x, y, j)
                got = _piece(outs[w], kinds[w], 2 * px + py, c)
                pltpu.make_async_remote_copy(src_ref=got, dst_ref=got, send_sem=ssem.at[w, j - 1],
                                             recv_sem=rsem.at[w, j - 1], device_id=(px, py, c),
                                             device_id_type=MESH).wait_recv()
                fwd = pltpu.make_async_remote_copy(src_ref=got, dst_ref=got, send_sem=fssem.at[w, j - 1],
                                                   recv_sem=frsem.at[w, j - 1], device_id=sib, device_id_type=MESH)
                fwd.start()
                fwds.append(fwd)
        for w in range(n):
            for j in (1, 2, 3):
                px, py = _peer_chip(x, y, j)
                other = _piece(outs[w], kinds[w], 2 * px + py, 1 - c)
                pltpu.make_async_remote_copy(src_ref=other, dst_ref=other, send_sem=fssem.at[w, j - 1],
                                             recv_sem=frsem.at[w, j - 1], device_id=sib,
                                             device_id_type=MESH).wait_recv()
        for cp in sends + fwds:
            cp.wait_send()

    return pl.pallas_call(
        body, name='all_gather_weights', out_shape=[jax.ShapeDtypeStruct(b.shape, b.dtype) for b in bufs],
        in_specs=[ANY] * n, out_specs=[ANY] * n, input_output_aliases={w: w for w in range(n)},
        scratch_shapes=[pltpu.SemaphoreType.DMA((n, 3)), pltpu.SemaphoreType.DMA((n, 3)),
                        pltpu.SemaphoreType.DMA((n, 3)), pltpu.SemaphoreType.DMA((n, 3))],
    )(*bufs)


def _remote(src, dst, ssem, rsem, k, dev):
    return pltpu.make_async_remote_copy(src_ref=src, dst_ref=dst, send_sem=ssem.at[k], recv_sem=rsem.at[k],
                                        device_id=dev, device_id_type=MESH)


def _phase_gather_ici(bufs, kinds, peers=(1, 2, 3)):
    n, m = len(bufs), len(peers)

    def build(ins, outs, ssem, rsem):
        x, y, c = _place()
        me = 2 * x + y
        ds = []
        for w in range(n):
            for q, j in enumerate(peers):
                px, py = _peer_chip(x, y, j)
                mine = _piece(outs[w], kinds[w], me, c)
                got = _piece(outs[w], kinds[w], 2 * px + py, c)
                k = m * w + q
                ds.append((_remote(mine, mine, ssem, rsem, k, (px, py, c)), _remote(got, got, ssem, rsem, k, (px, py, c))))
        return ds

    return dict(ins=list(bufs), outs=[jax.ShapeDtypeStruct(b.shape, b.dtype) for b in bufs],
                alias={w: w for w in range(n)}, nsem=m * n, build=build)


def _phase_gather_d2d(bufs, kinds):
    n = len(bufs)

    def build(ins, outs, ssem, rsem):
        x, y, c = _place()
        sib = (x, y, 1 - c)
        ds = []
        for w in range(n):
            for j in (1, 2, 3):
                px, py = _peer_chip(x, y, j)
                have = _piece(outs[w], kinds[w], 2 * px + py, c)
                want = _piece(outs[w], kinds[w], 2 * px + py, 1 - c)
                k = 3 * w + j - 1
                ds.append((_remote(have, have, ssem, rsem, k, sib), _remote(want, want, ssem, rsem, k, sib)))
        return ds

    return dict(ins=list(bufs), outs=[jax.ShapeDtypeStruct(b.shape, b.dtype) for b in bufs],
                alias={w: w for w in range(n)}, nsem=3 * n, build=build)


def _phase_pair(grads, kinds, slabs):
    n = len(grads)

    def build(ins, outs, ssem, rsem):
        x, y, c = _place()
        sib = (x, y, 1 - c)
        ds = []
        for w in range(n):
            for k in range(N_CHIPS):
                cp = _remote(_piece(ins[w], kinds[w], k, 1 - c), outs[w].at[k], ssem, rsem, N_CHIPS * w + k, sib)
                ds.append((cp, cp))
        return ds

    return dict(ins=list(grads), alias={}, nsem=N_CHIPS * n, build=build,
                outs=[jax.ShapeDtypeStruct((N_CHIPS,) + _half_shape(k, s), g.dtype)
                      for g, k, s in zip(grads, kinds, slabs)])


def _phase_chip(sums, peers=(1, 2, 3), landing=None):
    n, m = len(sums), len(peers)

    def build(ins, outs, ssem, rsem):
        x, y, c = _place()
        ds = []
        for w in range(n):
            for q, j in enumerate(peers):
                px, py = _peer_chip(x, y, j)
                cp = _remote(ins[w].at[2 * px + py], outs[w].at[j - 1], ssem, rsem, m * w + q, (px, py, c))
                ds.append((cp, cp))
        return ds

    return dict(ins=list(sums) + (list(landing) if landing is not None else []),
                alias={n + w: w for w in range(n)} if landing is not None else {}, nsem=m * n, build=build,
                outs=[jax.ShapeDtypeStruct((3,) + s.shape[1:], s.dtype) for s in sums])


def _phase_final(slabs_half, kinds):
    n = len(slabs_half)

    def build(ins, outs, ssem, rsem):
        x, y, c = _place()
        sib = (x, y, 1 - c)
        ds = []
        for w in range(n):
            mine = _half_of_slab(outs[w], kinds[w], c)
            other = _half_of_slab(outs[w], kinds[w], 1 - c)
            ds.append((_remote(mine, mine, ssem, rsem, w, sib), _remote(other, other, ssem, rsem, w, sib)))
        return ds

    return dict(ins=list(slabs_half), outs=[jax.ShapeDtypeStruct(s.shape, s.dtype) for s in slabs_half],
                alias={w: w for w in range(n)}, nsem=n, build=build)


def _phase_operands(phases, n_main_in, n_main_out):
    ins, outs, alias, sems = [], [], {}, []
    for ph in phases:
        for i, o in ph['alias'].items():
            alias[n_main_in + len(ins) + i] = n_main_out + len(outs) + o
        ins += ph['ins']
        outs += ph['outs']
        sems += [pltpu.SemaphoreType.DMA((ph['nsem'],)), pltpu.SemaphoreType.DMA((ph['nsem'],))]
    return ins, outs, alias, sems


def _phase_copies(phases, in_refs, out_refs, sem_refs):
    ds, a, b = [], 0, 0
    for p, ph in enumerate(phases):
        ds += ph['build'](in_refs[a:a + len(ph['ins'])], out_refs[b:b + len(ph['outs'])], sem_refs[2 * p],
                          sem_refs[2 * p + 1])
        a += len(ph['ins'])
        b += len(ph['outs'])
    return ds


def _phase_start(phases, in_refs, out_refs, sem_refs):
    for send, _ in _phase_copies(phases, in_refs, out_refs, sem_refs):
        send.start()


def _phase_finish(phases, in_refs, out_refs, sem_refs):
    ds = _phase_copies(phases, in_refs, out_refs, sem_refs)
    for _, recv in ds:
        recv.wait_recv()
    for send, _ in ds:
        send.wait_send()


def _phase_results(phases, outs):
    res, b = [], 0
    for ph in phases:
        res.append(list(outs[b:b + len(ph['outs'])]))
        b += len(ph['outs'])
    return res


def _run_phases(phases, name):
    ins, out_shapes, alias, sems = _phase_operands(phases, 0, 0)
    n_in, n_out = len(ins), len(out_shapes)

    def body(*refs):
        in_refs, out_refs, sem_refs = refs[:n_in], refs[n_in:n_in + n_out], refs[n_in + n_out:]
        _phase_start(phases, in_refs, out_refs, sem_refs)
        _phase_finish(phases, in_refs, out_refs, sem_refs)

    outs = pl.pallas_call(body, name=name, out_shape=out_shapes, in_specs=[ANY] * n_in, out_specs=[ANY] * n_out,
                          input_output_aliases=alias, scratch_shapes=sems)(*ins)
    return _phase_results(phases, outs)


def _cast_place(shard, kind, place_arr, dtype, name):
    full = _full_shape(kind, shard.shape)
    if kind == 'pool':
        g, r, w = shard.shape
        grid = (g,)
        src = pl.BlockSpec((None, r, w), lambda i, s: (i, 0, 0))
        dst = pl.BlockSpec((None, r, w), lambda i, s: (i, s[1], 0))
    else:
        r, w = shard.shape
        tr, tc = _tile2(r, w)
        nrb, ncb = r // tr, w // tc
        grid = (nrb, ncb)
        src = pl.BlockSpec((tr, tc), lambda i, j, s: (i, j))
        if kind == 'col':
            dst = pl.BlockSpec((tr, tc), lambda i, j, s: (i, s[1] * ncb + j))
        else:
            dst = pl.BlockSpec((tr, tc), lambda i, j, s: (s[1] * nrb + i, j))

    def body(s_ref, x_ref, o_ref):
        o_ref[...] = x_ref[...].astype(o_ref.dtype)

    return pl.pallas_call(
        body, name=name,
        grid_spec=pltpu.PrefetchScalarGridSpec(num_scalar_prefetch=1, grid=grid, in_specs=[src], out_specs=dst),
        out_shape=jax.ShapeDtypeStruct(full, dtype), compiler_params=_cp(('parallel',) * len(grid)),
    )(place_arr, shard)


def _tile2(r, c):
    tr = _div_tile(r, max(16, (1 << 19) // c), 16)
    if tr == r or tr * c >= (1 << 17):
        return tr, c
    return r, _div_tile(c, max(128, (1 << 19) // r), 128)


def _pair_sum(grad, theirs, kind, slab, c_arr, name):
    hs = _half_shape(kind, slab)
    if kind == 'pool':
        grid = (N_CHIPS, hs[0], 1)
        own = pl.BlockSpec((None, hs[1], hs[2]), lambda k, i, j, s: (i, 2 * k + s[0], 0))
        stk = pl.BlockSpec((None, None, hs[1], hs[2]), lambda k, i, j, s: (k, i, 0, 0))
    else:
        tr, tc = _tile2(*hs)
        nrb, ncb = hs[0] // tr, hs[1] // tc
        grid = (N_CHIPS, nrb, ncb)
        if kind == 'col':
            own = pl.BlockSpec((tr, tc), lambda k, i, j, s: (s[0] * nrb + i, k * ncb + j))
        else:
            own = pl.BlockSpec((tr, tc), lambda k, i, j, s: (k * nrb + i, s[0] * ncb + j))
        stk = pl.BlockSpec((None, tr, tc), lambda k, i, j, s: (k, i, j))

    def body(s_ref, a_ref, b_ref, o_ref):
        o_ref[...] = (a_ref[...].astype(F32) + b_ref[...].astype(F32)).astype(o_ref.dtype)

    return pl.pallas_call(
        body, name=name,
        grid_spec=pltpu.PrefetchScalarGridSpec(num_scalar_prefetch=1, grid=grid, in_specs=[own, stk], out_specs=stk),
        out_shape=jax.ShapeDtypeStruct((N_CHIPS,) + hs, BF), compiler_params=_cp(('parallel',) * 3),
    )(c_arr, grad, theirs)


def _chip_sum(sums, landed, kind, slab, place_arr, name):
    hs = _half_shape(kind, slab)
    if kind == 'pool':
        grid = (hs[0], 1)
        blk = (None, None, hs[1], hs[2])
        mine = pl.BlockSpec(blk, lambda i, j, s: (s[1], i, 0, 0))
        land = [pl.BlockSpec(blk, lambda i, j, s, p=p: (p, i, 0, 0)) for p in range(3)]
        out = pl.BlockSpec((None, hs[1], hs[2]), lambda i, j, s: (i, s[0], 0))
    else:
        tr, tc = _tile2(*hs)
        nrb, ncb = hs[0] // tr, hs[1] // tc
        grid = (nrb, ncb)
        blk = (None, tr, tc)
        mine = pl.BlockSpec(blk, lambda i, j, s: (s[1], i, j))
        land = [pl.BlockSpec(blk, lambda i, j, s, p=p: (p, i, j)) for p in range(3)]
        if kind == 'col':
            out = pl.BlockSpec((tr, tc), lambda i, j, s: (s[0] * nrb + i, j))
        else:
            out = pl.BlockSpec((tr, tc), lambda i, j, s: (i, s[0] * ncb + j))

    def body(s_ref, a_ref, b_ref, c_ref, d_ref, o_ref):
        o_ref[...] = ((a_ref[...].astype(F32) + b_ref[...].astype(F32)) + c_ref[...].astype(F32)) + d_ref[...].astype(F32)

    return pl.pallas_call(
        body, name=name,
        grid_spec=pltpu.PrefetchScalarGridSpec(num_scalar_prefetch=1, grid=grid, in_specs=[mine] + land, out_specs=out),
        out_shape=jax.ShapeDtypeStruct(tuple(slab), F32), compiler_params=_cp(('parallel',) * 2),
    )(place_arr, sums, landed, landed, landed)


def _all_reduce_small(buf):
    rows, cols = buf.shape

    def body(in_ref, out_ref, land, ssem, rsem):
        x, y, c = _place()
        me = 4 * x + 2 * y + c
        land[me] = in_ref[...]
        started = []
        for j in range(1, 8):
            px = 1 - x if (j >> 2) & 1 else x
            py = 1 - y if (j >> 1) & 1 else y
            pc = 1 - c if j & 1 else c
            cp = pltpu.make_async_remote_copy(src_ref=in_ref, dst_ref=land.at[me], send_sem=ssem.at[j - 1],
                                              recv_sem=rsem.at[j - 1], device_id=(px, py, pc), device_id_type=MESH)
            cp.start()
            started.append(cp)
        for j in range(1, 8):
            px = 1 - x if (j >> 2) & 1 else x
            py = 1 - y if (j >> 1) & 1 else y
            pc = 1 - c if j & 1 else c
            slot = land.at[4 * px + 2 * py + pc]
            pltpu.make_async_remote_copy(src_ref=slot, dst_ref=slot, send_sem=ssem.at[j - 1], recv_sem=rsem.at[j - 1],
                                         device_id=(px, py, pc), device_id_type=MESH).wait_recv()
        for cp in started:
            cp.wait_send()
        acc = land[0]
        for d in range(1, 8):
            acc = acc + land[d]
        out_ref[...] = acc

    return pl.pallas_call(
        body, name='all_reduce_small', out_shape=jax.ShapeDtypeStruct((rows, cols), F32),
        in_specs=[pl.BlockSpec(memory_space=pltpu.VMEM)], out_specs=pl.BlockSpec(memory_space=pltpu.VMEM),
        scratch_shapes=[pltpu.VMEM((8, rows, cols), F32), pltpu.SemaphoreType.DMA((7,)), pltpu.SemaphoreType.DMA((7,))],
    )(buf)


def _elementwise(fn, ins, lead_index, out_shape, out_dtypes, name):
    nd = len(out_shape)
    r, cdim = out_shape[-2], out_shape[-1]
    tr, tc = _tile2(r, cdim)
    grid = tuple(out_shape[:-2]) + (r // tr, cdim // tc)
    block = (None,) * (nd - 2) + (tr, tc)

    def spec(lead):
        if lead is None:
            return pl.BlockSpec(block, lambda *g: tuple(g))
        return pl.BlockSpec((None,) + block, lambda *g, lead=lead: (lead,) + tuple(g))

    n_in = len(ins)

    def body(*refs):
        res = fn(*[r_[...] for r_ in refs[:n_in]])
        for o_ref, v in zip(refs[n_in:], res):
            o_ref[...] = v.astype(o_ref.dtype)

    return pl.pallas_call(
        body, name=name, grid=grid, in_specs=[spec(l) for l in lead_index],
        out_specs=[spec(None) for _ in out_dtypes],
        out_shape=[jax.ShapeDtypeStruct(tuple(out_shape), dt) for dt in out_dtypes],
        compiler_params=_cp(('parallel',) * len(grid)),
    )(*ins)


def _adam_fn(w, g, m, v):
    m = ADAM_B1 * m + (1.0 - ADAM_B1) * g
    v = ADAM_B2 * v + (1.0 - ADAM_B2) * (g * g)
    m_hat = m / (1.0 - ADAM_B1 ** ADAM_STEP)
    v_hat = v / (1.0 - ADAM_B2 ** ADAM_STEP)
    delta = -ADAM_LR * (m_hat / (jnp.sqrt(v_hat) + ADAM_EPS) + ADAM_WD * w)
    return g, delta, m, v


def _adam(w, g, m, v, name):
    return _elementwise(_adam_fn, [w, g, m, v], [None] * 4, w.shape, [F32] * 4, name)


_DIMS = {'nn': (((1,), (0,)), ((), ())), 'nt': (((1,), (1,)), ((), ())), 'tn': (((0,), (0,)), ((), ()))}


def _mm(a, b, mode, out_dtype, name, phases=()):
    if mode == 'nn':
        (M, K), N = a.shape, b.shape[1]
    elif mode == 'nt':
        (M, K), N = a.shape, b.shape[0]
    else:
        (K, M), N = a.shape, b.shape[1]
    if mode == 'tn':
        tm, tn, tk = _div_tile(M, 512, 128), _div_tile(N, 1024, 128), _div_tile(K, MM_TK_ROWS, 16)
        if tm < 256:
            tm = _div_tile(M, 1024, 128)
    else:
        tk = _div_tile(K, MM_TK_COLS, 128)
        tm, tn = _div_tile(M, 1088, 16), _div_tile(N, 1024 if tk <= 2816 else 512, 128)
    nk = K // tk
    a_spec = {'nn': pl.BlockSpec((tm, tk), lambda i, j, k: (i, k)), 'nt': pl.BlockSpec((tm, tk), lambda i, j, k: (i, k)),
              'tn': pl.BlockSpec((tk, tm), lambda i, j, k: (k, i))}[mode]
    b_spec = {'nn': pl.BlockSpec((tk, tn), lambda i, j, k: (k, j)), 'nt': pl.BlockSpec((tn, tk), lambda i, j, k: (j, k)),
              'tn': pl.BlockSpec((tk, tn), lambda i, j, k: (k, j))}[mode]
    dims = _DIMS[mode]
    gm, gn = M // tm, N // tn
    extra_in, extra_out, alias, sems = _phase_operands(phases, 2, 1)
    n_ei, n_eo = len(extra_in), len(extra_out)

    def body(*refs):
        a_ref, b_ref, ein = refs[0], refs[1], refs[2:2 + n_ei]
        o_ref, eout = refs[2 + n_ei], refs[3 + n_ei:3 + n_ei + n_eo]
        acc_ref, sem_refs = refs[3 + n_ei + n_eo], refs[4 + n_ei + n_eo:]
        i, j, k = pl.program_id(0), pl.program_id(1), pl.program_id(2)
        if phases:
            @pl.when((i == 0) & (j == 0) & (k == 0))
            def _():
                _phase_start(phases, ein, eout, sem_refs)

        part = lax.dot_general(a_ref[...], b_ref[...], dims, preferred_element_type=F32)
        if nk == 1:
            o_ref[...] = part.astype(o_ref.dtype)
        else:
            @pl.when(k == 0)
            def _():
                acc_ref[...] = part

            @pl.when(k > 0)
            def _():
                acc_ref[...] += part

            @pl.when(k == nk - 1)
            def _():
                o_ref[...] = acc_ref[...].astype(o_ref.dtype)

        if phases:
            @pl.when((i == gm - 1) & (j == gn - 1) & (k == nk - 1))
            def _():
                _phase_finish(phases, ein, eout, sem_refs)

    outs = pl.pallas_call(
        body, name=name, grid=(gm, gn, nk), in_specs=[a_spec, b_spec] + [ANY] * n_ei,
        out_specs=[pl.BlockSpec((tm, tn), lambda i, j, k: (i, j))] + [ANY] * n_eo,
        out_shape=[jax.ShapeDtypeStruct((M, N), out_dtype)] + extra_out, input_output_aliases=alias,
        scratch_shapes=[pltpu.VMEM((tm, tn) if nk > 1 else (8, 128), F32)] + sems,
        compiler_params=_cp(('arbitrary',) * 3 if phases else ('parallel', 'parallel', 'arbitrary')),
    )(a, b, *extra_in)
    if phases:
        return outs[0], _phase_results(phases, outs[1:])
    return outs[0]


def _rb(tm, w, col=0):
    return pl.BlockSpec((tm, w), lambda i, col=col: (i, col))


def _fixed(shape):
    return pl.BlockSpec(shape, lambda i: (0,) * len(shape))


def _rms(x):
    return lax.rsqrt(jnp.mean(x * x, axis=-1, keepdims=True) + EPS)


def _norm_fwd(x, gain, name, width=None, col=0):
    Lp = x.shape[0]
    width = width or x.shape[1]
    tm = _div_tile(Lp, 256, 16)

    def body(x_ref, g_ref, o_ref):
        v = x_ref[...]
        o_ref[...] = (v * _rms(v) * g_ref[...]).astype(o_ref.dtype)

    return pl.pallas_call(
        body, name=name, grid=(Lp // tm,), in_specs=[_rb(tm, width, col), _fixed((1, width))], out_specs=_rb(tm, width),
        out_shape=jax.ShapeDtypeStruct((Lp, width), BF), compiler_params=_cp(('parallel',)),
    )(x, gain)


def _post_residual(h, f, gain, scale, next_gain, name, phases=()):
    Lp, D = h.shape
    tm = _div_tile(Lp, 256, 16)

    def body(h_ref, f_ref, g_ref, ng_ref, o_ref, n_ref):
        v = f_ref[...]
        out = h_ref[...] + scale * (v * _rms(v) * g_ref[...])
        o_ref[...] = out
        n_ref[...] = (out * _rms(out) * ng_ref[...]).astype(n_ref.dtype)

    return _call_carrying(
        body, name, (Lp // tm,), [_rb(tm, D), _rb(tm, D), _fixed((1, D)), _fixed((1, D))], [_rb(tm, D), _rb(tm, D)],
        [jax.ShapeDtypeStruct((Lp, D), F32), jax.ShapeDtypeStruct((Lp, D), BF)], [], (h, f, gain, next_gain), phases)


def _pre_bwd(dres, dn, h, gain, name, phases=(), before=None):
    Lp, D = h.shape
    tm = _div_tile(Lp, 256, 16)

    def accumulate(ref, part):
        @pl.when(pl.program_id(0) == 0)
        def _():
            ref[...] = part

        @pl.when(pl.program_id(0) > 0)
        def _():
            ref[...] += part

    def body(dres_ref, dn_ref, h_ref, g_ref, *rest):
        v = h_ref[...]
        r = _rms(v)
        dy = dn_ref[...]
        w = dy * g_ref[...]
        dh = dres_ref[...] + r * w - v * (r * r * r) * jnp.mean(w * v, axis=-1, keepdims=True)
        if before is None:
            dh_ref, dg_ref = rest
        else:
            f_ref, pg_ref, dh_ref, dg_ref, df_ref, dpg_ref = rest
            fv = f_ref[...]
            fr = _rms(fv)
            fdy = before[2] * dh
            fw = fdy * pg_ref[...]
            df_ref[...] = (fr * fw - fv * (fr * fr * fr) * jnp.mean(fw * fv, axis=-1, keepdims=True)).astype(BF)
            accumulate(dpg_ref, jnp.sum(fdy * fv * fr, axis=0, keepdims=True))
        dh_ref[...] = dh
        accumulate(dg_ref, jnp.sum(dy * v * r, axis=0, keepdims=True))

    in_specs = [_rb(tm, D), _rb(tm, D), _rb(tm, D), _fixed((1, D))]
    out_specs = [_rb(tm, D), _fixed((1, D))]
    out_shape = [jax.ShapeDtypeStruct((Lp, D), F32), jax.ShapeDtypeStruct((1, D), F32)]
    args = (dres, dn, h, gain)
    if before is not None:
        in_specs += [_rb(tm, D), _fixed((1, D))]
        out_specs += [_rb(tm, D), _fixed((1, D))]
        out_shape += [jax.ShapeDtypeStruct((Lp, D), BF), jax.ShapeDtypeStruct((1, D), F32)]
        args += (before[0], before[1])
    return _call_carrying(body, name, (Lp // tm,), in_specs, out_specs, out_shape, [], args, phases)


def _swiglu_fwd(gu, name, phases=()):
    Lp, F2 = gu.shape
    F = F2 // 2
    tm = _div_tile(Lp, 256, 16)

    def body(gu_ref, a_ref):
        g = gu_ref[:, :F].astype(F32)
        u = gu_ref[:, F:].astype(F32)
        a_ref[...] = (g * jax.nn.sigmoid(g) * u).astype(a_ref.dtype)

    return _call_carrying(body, name, (Lp // tm,), [_rb(tm, F2)], [_rb(tm, F)],
                          [jax.ShapeDtypeStruct((Lp, F), BF)], [], (gu,), phases)


def _swiglu_bwd(da, gu, name):
    Lp, F2 = gu.shape
    F = F2 // 2
    tm = _div_tile(Lp, 256, 16)

    def body(da_ref, gu_ref, o_ref):
        g = gu_ref[:, :F].astype(F32)
        u = gu_ref[:, F:].astype(F32)
        da_ = da_ref[...].astype(F32)
        s = jax.nn.sigmoid(g)
        o_ref[:, :F] = (da_ * u * (s * (1.0 + g * (1.0 - s)))).astype(o_ref.dtype)
        o_ref[:, F:] = (da_ * (g * s)).astype(o_ref.dtype)

    return pl.pallas_call(
        body, name=name, grid=(Lp // tm,), in_specs=[_rb(tm, F), _rb(tm, F2)], out_specs=_rb(tm, F2),
        out_shape=jax.ShapeDtypeStruct((Lp, F2), BF), compiler_params=_cp(('parallel',)),
    )(da, gu)


def _tail(h, f, gain, scale, tgt, n_meta, n_real, name):
    Lp, D = h.shape
    tm = _div_tile(Lp, 256, 16)

    def body(h_ref, f_ref, g_ref, t_ref, d_ref, df_ref, dg_ref, l_ref):
        v = f_ref[...]
        r = _rms(v)
        gain_ = g_ref[...]
        row = lax.broadcasted_iota(jnp.int32, (tm, 1), 0) + pl.program_id(0) * tm
        ok = (row >= n_meta) & (row < n_meta + n_real)
        err = jnp.where(ok, h_ref[...] + scale * (v * r * gain_) - t_ref[...], 0.0)
        d_out = err / D
        d_ref[...] = d_out
        dy = scale * d_out
        w = dy * gain_
        df_ref[...] = (r * w - v * (r * r * r) * jnp.mean(w * v, axis=-1, keepdims=True)).astype(df_ref.dtype)
        dg_part = jnp.sum(dy * v * r, axis=0, keepdims=True)
        l_part = jnp.full((1, 128), jnp.sum(err * err), F32)

        @pl.when(pl.program_id(0) == 0)
        def _():
            dg_ref[...] = dg_part
            l_ref[...] = l_part

        @pl.when(pl.program_id(0) > 0)
        def _():
            dg_ref[...] += dg_part
            l_ref[...] += l_part

    return pl.pallas_call(
        body, name=name, grid=(Lp // tm,), in_specs=[_rb(tm, D), _rb(tm, D), _fixed((1, D)), _rb(tm, D)],
        out_specs=[_rb(tm, D), _rb(tm, D), _fixed((1, D)), _fixed((1, 128))],
        out_shape=[jax.ShapeDtypeStruct((Lp, D), F32), jax.ShapeDtypeStruct((Lp, D), BF),
                   jax.ShapeDtypeStruct((1, D), F32), jax.ShapeDtypeStruct((1, 128), F32)],
        compiler_params=_cp(('arbitrary',)),
    )(h, f, gain, tgt)


def _split_bf16(v):
    hi = v.astype(BF)
    return hi, (v - hi.astype(F32)).astype(BF)


def _pool_fwd(z, name):
    Lp = z.shape[0]
    T = _div_tile(Lp, 256, 16)
    G = len(POOL_WINDOWS)

    def body(cur_ref, prev_ref, d_ref):
        i, g = pl.program_id(0), pl.program_id(1)
        w = jnp.left_shift(2, g)
        rr = lax.broadcasted_iota(jnp.int32, (T, T), 0)
        cc = lax.broadcasted_iota(jnp.int32, (T, T), 1)
        b_cur = jnp.where((cc <= rr) & (cc > rr - w), 1.0, 0.0).astype(BF)
        w_prev = jnp.where(i > 0, w, 0)
        b_prev = jnp.where(cc - T > rr - w_prev, 1.0, 0.0).astype(BF)
        u = cur_ref[...]
        s = jnp.zeros((T, POOL_GROUP), F32)
        for part in _split_bf16(u):
            s += jnp.dot(b_cur, part, preferred_element_type=F32)
        for part in _split_bf16(prev_ref[...]):
            s += jnp.dot(b_prev, part, preferred_element_type=F32)
        t = lax.broadcasted_iota(jnp.int32, (T, 1), 0) + i * T
        cnt = jnp.minimum(w, t + 1).astype(F32)
        d_ref[...] = (s / cnt - u).astype(d_ref.dtype)

    return pl.pallas_call(
        body, name=name, grid=(Lp // T, G),
        in_specs=[pl.BlockSpec((T, POOL_GROUP), lambda i, g: (i, g)),
                  pl.BlockSpec((T, POOL_GROUP), lambda i, g: (jnp.maximum(i - 1, 0), g))],
        out_specs=pl.BlockSpec((T, POOL_GROUP), lambda i, g: (i, g)),
        out_shape=jax.ShapeDtypeStruct((Lp, POOL_WIDTH), BF), compiler_params=_cp(('parallel', 'parallel')),
    )(z, z)


def _pool_bwd(dd, name):
    Lp = dd.shape[0]
    T = _div_tile(Lp, 256, 16)
    G = len(POOL_WINDOWS)
    n_t = Lp // T

    def body(cur_ref, next_ref, o_ref):
        i, g = pl.program_id(0), pl.program_id(1)
        w = jnp.left_shift(2, g)
        rr = lax.broadcasted_iota(jnp.int32, (T, T), 0)
        cc = lax.broadcasted_iota(jnp.int32, (T, T), 1)
        b_cur = jnp.where((cc >= rr) & (cc < rr + w), 1.0, 0.0).astype(BF)
        w_next = jnp.where(i < n_t - 1, w, 0)
        b_next = jnp.where(cc + T < rr + w_next, 1.0, 0.0).astype(BF)
        t = lax.broadcasted_iota(jnp.int32, (T, 1), 0) + i * T
        cur = cur_ref[...]
        e_cur = cur / jnp.minimum(w, t + 1).astype(F32)
        e_next = next_ref[...] / jnp.minimum(w, t + T + 1).astype(F32)
        s = jnp.zeros((T, POOL_GROUP), F32)
        for part in _split_bf16(e_cur):
            s += jnp.dot(b_cur, part, preferred_element_type=F32)
        for part in _split_bf16(e_next):
            s += jnp.dot(b_next, part, preferred_element_type=F32)
        o_ref[...] = (s - cur).astype(o_ref.dtype)

    return pl.pallas_call(
        body, name=name, grid=(n_t, G),
        in_specs=[pl.BlockSpec((T, POOL_GROUP), lambda i, g: (i, g)),
                  pl.BlockSpec((T, POOL_GROUP), lambda i, g: (jnp.minimum(i + 1, n_t - 1), g))],
        out_specs=pl.BlockSpec((T, POOL_GROUP), lambda i, g: (i, g)),
        out_shape=jax.ShapeDtypeStruct((Lp, POOL_WIDTH), BF), compiler_params=_cp(('parallel', 'parallel')),
    )(dd, dd)


def _pool_mix_fwd(d, pool_w, scale, name):
    Lp = d.shape[0]
    G = len(POOL_WINDOWS)
    tm = _div_tile(Lp, 1088, 16)

    def body(d_ref, w_ref, s_ref, e_ref, y_ref):
        e = jnp.dot(d_ref[...], w_ref[...], preferred_element_type=F32)
        e_ref[...] = e.astype(e_ref.dtype)
        y_ref[...] = (e * s_ref[...]).astype(y_ref.dtype)

    blk = pl.BlockSpec((tm, POOL_GROUP), lambda g, i: (i, g))
    return pl.pallas_call(
        body, name=name, grid=(G, Lp // tm),
        in_specs=[blk, pl.BlockSpec((None, POOL_GROUP, POOL_GROUP), lambda g, i: (g, 0, 0)),
                  pl.BlockSpec((1, POOL_GROUP), lambda g, i: (0, g))],
        out_specs=[blk, blk], out_shape=[jax.ShapeDtypeStruct((Lp, POOL_WIDTH), BF)] * 2,
        compiler_params=_cp(('parallel', 'parallel')),
    )(d, pool_w, scale)


def _pool_mix_bwd(dyp, e, d, pool_w, scale, name):
    Lp = d.shape[0]
    G = len(POOL_WINDOWS)
    tm = _div_tile(Lp, 1088, 16)

    def body(dy_ref, e_ref, d_ref, w_ref, s_ref, dd_ref, ds_ref, dw_ref):
        i = pl.program_id(1)
        dy = dy_ref[...]
        de = (dy * s_ref[...]).astype(BF)
        dd_ref[...] = lax.dot_general(de, w_ref[...], _DIMS['nt'], preferred_element_type=F32)
        ds_part = jnp.sum(dy * e_ref[...].astype(F32), axis=0, keepdims=True)
        dw_part = lax.dot_general(d_ref[...], de, _DIMS['tn'], preferred_element_type=F32)

        @pl.when(i == 0)
        def _():
            ds_ref[...] = ds_part
            dw_ref[...] = dw_part

        @pl.when(i > 0)
        def _():
            ds_ref[...] += ds_part
            dw_ref[...] += dw_part

    blk = pl.BlockSpec((tm, POOL_GROUP), lambda g, i: (i, g))
    wblk = pl.BlockSpec((None, POOL_GROUP, POOL_GROUP), lambda g, i: (g, 0, 0))
    sblk = pl.BlockSpec((1, POOL_GROUP), lambda g, i: (0, g))
    return pl.pallas_call(
        body, name=name, grid=(G, Lp // tm), in_specs=[blk, blk, blk, wblk, sblk], out_specs=[blk, sblk, wblk],
        out_shape=[jax.ShapeDtypeStruct((Lp, POOL_WIDTH), F32), jax.ShapeDtypeStruct((1, POOL_WIDTH), F32),
                   jax.ShapeDtypeStruct((G, POOL_GROUP, POOL_GROUP), F32)],
        compiler_params=_cp(('parallel', 'arbitrary')),
    )(dyp, e, d, pool_w, scale)


def _rot_half(t):
    lane = lax.broadcasted_iota(jnp.int32, t.shape, 1)
    half = QK_ROPE // 2
    return jnp.where(lane < half, -pltpu.roll(t, 128 - half, 1), pltpu.roll(t, half, 1))


def _lora_norms(z, q_gain, kv_gain, lay, name):
    Lp = z.shape[0]
    QL, KVL = lay['QL'], lay['KVL']
    tm = _div_tile(Lp, 256, 16)

    def body(q_ref, kv_ref, qg_ref, kg_ref, qo_ref, ko_ref):
        a = q_ref[...]
        qo_ref[...] = (a * _rms(a) * qg_ref[...]).astype(BF)
        b = kv_ref[...]
        ko_ref[...] = (b * _rms(b) * kg_ref[...]).astype(BF)

    return pl.pallas_call(
        body, name=name, grid=(Lp // tm,),
        in_specs=[_rb(tm, QL, lay['cq'] // QL), _rb(tm, KVL, lay['ckv'] // KVL), _fixed((1, QL)), _fixed((1, KVL))],
        out_specs=[_rb(tm, QL), _rb(tm, KVL)],
        out_shape=[jax.ShapeDtypeStruct((Lp, QL), BF), jax.ShapeDtypeStruct((Lp, KVL), BF)],
        compiler_params=_cp(('parallel',)),
    )(z, z, q_gain, kv_gain)


def _lora_norms_bwd(dqn, dkn, z, q_gain, kv_gain, lay, name):
    Lp = z.shape[0]
    QL, KVL = lay['QL'], lay['KVL']
    tm = _div_tile(Lp, 256, 16)

    def one(dy, v, gain):
        r = _rms(v)
        w = dy * gain
        return r * w - v * (r * r * r) * jnp.mean(w * v, axis=-1, keepdims=True), jnp.sum(dy * v * r, axis=0, keepdims=True)

    def body(dq_ref, dk_ref, q_ref, kv_ref, qg_ref, kg_ref, o_ref, dqg_ref, dkg_ref):
        da, ga = one(dq_ref[...], q_ref[...], qg_ref[...])
        db, gb = one(dk_ref[...], kv_ref[...], kg_ref[...])
        o_ref[:, :QL] = da.astype(BF)
        o_ref[:, QL:] = db.astype(BF)

        @pl.when(pl.program_id(0) == 0)
        def _():
            dqg_ref[...] = ga
            dkg_ref[...] = gb

        @pl.when(pl.program_id(0) > 0)
        def _():
            dqg_ref[...] += ga
            dkg_ref[...] += gb

    return pl.pallas_call(
        body, name=name, grid=(Lp // tm,),
        in_specs=[_rb(tm, QL), _rb(tm, KVL), _rb(tm, QL, lay['cq'] // QL), _rb(tm, KVL, lay['ckv'] // KVL),
                  _fixed((1, QL)), _fixed((1, KVL))],
        out_specs=[_rb(tm, QL + KVL), _fixed((1, QL)), _fixed((1, KVL))],
        out_shape=[jax.ShapeDtypeStruct((Lp, QL + KVL), BF), jax.ShapeDtypeStruct((1, QL), F32),
                   jax.ShapeDtypeStruct((1, KVL), F32)],
        compiler_params=_cp(('arbitrary',)),
    )(dqn, dkn, z, z, q_gain, kv_gain)


def _qk_prep(q_raw, kv, z, cos, sin, lay, H, name):
    Lp = q_raw.shape[0]
    W = H * HEAD_PAD
    tm = _div_tile(Lp, 256, 16)

    def body(q_ref, kv_ref, kr_ref, c_ref, s_ref, qo_ref, ko_ref):
        c, s = c_ref[...], s_ref[...]

        def rope(t):
            return t * c + _rot_half(t) * s

        kpe = rope(kr_ref[...]).astype(BF)
        for h in range(H):
            b = h * HEAD_PAD
            qo_ref[:, b:b + 128] = (q_ref[:, b:b + 128].astype(F32) * SOFTMAX_SCALE).astype(BF)
            qo_ref[:, b + 128:b + 256] = (rope(q_ref[:, b + 128:b + 256].astype(F32)) * SOFTMAX_SCALE).astype(BF)
            ko_ref[:, b:b + 128] = kv_ref[:, b:b + 128]
            ko_ref[:, b + 128:b + 256] = kpe

    return pl.pallas_call(
        body, name=name, grid=(Lp // tm,),
        in_specs=[_rb(tm, W), _rb(tm, W), _rb(tm, 128, lay['kr'] // 128), _rb(tm, 128), _rb(tm, 128)],
        out_specs=[_rb(tm, W), _rb(tm, W)], out_shape=[jax.ShapeDtypeStruct((Lp, W), BF)] * 2,
        compiler_params=_cp(('parallel',)),
    )(q_raw, kv, z, cos, sin)


def _qk_prep_bwd(dQ, dK, dV, cos, sin, H, name):
    Lp = dQ.shape[0]
    W = H * HEAD_PAD
    tm = _div_tile(Lp, 256, 16)

    def body(dq_ref, dk_ref, dv_ref, c_ref, s_ref, qo_ref, kvo_ref, kro_ref):
        c, s = c_ref[...], s_ref[...]

        def unrope(t):
            return t * c - _rot_half(t * s)

        acc = jnp.zeros((tm, 128), F32)
        for h in range(H):
            b = h * HEAD_PAD
            qo_ref[:, b:b + 128] = (dq_ref[:, b:b + 128] * SOFTMAX_SCALE).astype(BF)
            qo_ref[:, b + 128:b + 256] = (unrope(dq_ref[:, b + 128:b + 256]) * SOFTMAX_SCALE).astype(BF)
            kvo_ref[:, b:b + 128] = dk_ref[:, b:b + 128].astype(BF)
            kvo_ref[:, b + 128:b + 256] = dv_ref[:, h * V_DIM:(h + 1) * V_DIM]
            acc += dk_ref[:, b + 128:b + 256]
        kro_ref[...] = unrope(acc).astype(BF)

    return pl.pallas_call(
        body, name=name, grid=(Lp // tm,),
        in_specs=[_rb(tm, W), _rb(tm, W), _rb(tm, H * V_DIM), _rb(tm, 128), _rb(tm, 128)],
        out_specs=[_rb(tm, W), _rb(tm, W), _rb(tm, 128)],
        out_shape=[jax.ShapeDtypeStruct((Lp, W), BF), jax.ShapeDtypeStruct((Lp, W), BF),
                   jax.ShapeDtypeStruct((Lp, 128), BF)],
        compiler_params=_cp(('parallel',)),
    )(dQ, dK, dV, cos, sin)


def _call_carrying(core, name, grid, in_specs, out_specs, out_shape, scratch, args, phases):
    n_in, n_out, n_scr = len(in_specs), len(out_specs), len(scratch)
    extra_in, extra_out, alias, sems = _phase_operands(phases, n_in, n_out)
    n_ei, n_eo = len(extra_in), len(extra_out)

    def body(*refs):
        ins, ein = refs[:n_in], refs[n_in:n_in + n_ei]
        outs = refs[n_in + n_ei:n_in + n_ei + n_out]
        eout = refs[n_in + n_ei + n_out:n_in + n_ei + n_out + n_eo]
        rest = refs[n_in + n_ei + n_out + n_eo:]
        scr, sem_refs = rest[:n_scr], rest[n_scr:]
        if phases:
            ids = [pl.program_id(d) for d in range(len(grid))]
            first, last = ids[0] == 0, ids[0] == grid[0] - 1
            for d in range(1, len(grid)):
                first, last = first & (ids[d] == 0), last & (ids[d] == grid[d] - 1)

            @pl.when(first)
            def _():
                _phase_start(phases, ein, eout, sem_refs)

        core(*ins, *outs, *scr)
        if phases:
            @pl.when(last)
            def _():
                _phase_finish(phases, ein, eout, sem_refs)

    outs = pl.pallas_call(
        body, name=name, grid=grid, in_specs=list(in_specs) + [ANY] * n_ei, out_specs=list(out_specs) + [ANY] * n_eo,
        out_shape=list(out_shape) + extra_out, input_output_aliases=alias, scratch_shapes=list(scratch) + sems,
        compiler_params=_cp(('arbitrary',) * len(grid)),
    )(*args, *extra_in)
    return list(outs[:n_out]), _phase_results(phases, outs[n_out:])


def _flash_fwd(Q, K, kv, H, name, phases=()):
    Lp = Q.shape[0]
    T = _div_tile(Lp, 256, 16)

    n_t = Lp // T
    FC = min(FLASH_FWD_CHUNK, n_t)
    CH = FC * T

    def body(q_ref, k_ref, v_ref, o_ref, lse_ref, m_s, l_s, acc_s):
        i = pl.program_id(1)
        q = q_ref[...]
        m_s[...] = jnp.full((T, 1), NEG, F32)
        l_s[...] = jnp.zeros((T, 1), F32)
        acc_s[...] = jnp.zeros((T, V_DIM), F32)

        def step(start, width, masked):
            parts = [(start, width, False)] if not masked else (
                ([(start, width - T, False)] if width > T else []) + [(start + (width - T), T, True)])
            scores = []
            for at, wd, diag in parts:
                s = lax.dot_general(q, k_ref[pl.ds(at, wd), :], _DIMS['nt'], preferred_element_type=F32)
                if diag:
                    rr = lax.broadcasted_iota(jnp.int32, (T, T), 0)
                    cc = lax.broadcasted_iota(jnp.int32, (T, T), 1)
                    s = jnp.where(cc <= rr, s, NEG)
                scores.append(s)
            m = m_s[...]
            m_new = m
            for s in scores:
                m_new = jnp.maximum(m_new, jnp.max(s, axis=-1, keepdims=True))
            alpha = jnp.exp(m - m_new)
            l = alpha * l_s[...]
            acc = alpha * acc_s[...]
            for (at, wd, _), s in zip(parts, scores):
                p = jnp.exp(s - m_new)
                l = l + jnp.sum(p, axis=-1, keepdims=True)
                acc = acc + jnp.dot(p.astype(BF), v_ref[pl.ds(at, wd), :], preferred_element_type=F32)
            l_s[...] = l
            acc_s[...] = acc
            m_s[...] = m_new

        n_full = i // FC

        def full(cidx, carry):
            step(pl.multiple_of(cidx * CH, CH), CH, False)
            return carry

        if n_t > FC:
            lax.fori_loop(0, n_full, full, 0)
        for nb in range(1, FC + 1):
            @pl.when(i % FC == nb - 1)
            def _(nb=nb):
                step(pl.multiple_of(n_full * CH, CH), nb * T, True)

        l = l_s[...]
        o_ref[...] = (acc_s[...] / l).astype(o_ref.dtype)
        lse_ref[...] = jnp.broadcast_to(m_s[...] + jnp.log(l), (T, 128))

    return _call_carrying(
        body, name, (H, Lp // T),
        [pl.BlockSpec((T, HEAD_PAD), lambda h, i: (i, h)), pl.BlockSpec((Lp, HEAD_PAD), lambda h, i: (0, h)),
         pl.BlockSpec((Lp, V_DIM), lambda h, i: (0, 2 * h + 1))],
        [pl.BlockSpec((T, V_DIM), lambda h, i: (i, h)), pl.BlockSpec((None, T, 128), lambda h, i: (h, i, 0))],
        [jax.ShapeDtypeStruct((Lp, H * V_DIM), BF), jax.ShapeDtypeStruct((H, Lp, 128), F32)],
        [pltpu.VMEM((T, 1), F32), pltpu.VMEM((T, 1), F32), pltpu.VMEM((T, V_DIM), F32)], (Q, K, kv), phases)


def _flash_bwd(Q, K, kv, O, dO, lse, H, name, phases=()):
    Lp = Q.shape[0]
    T = _div_tile(Lp, 256, 16)
    n_t = Lp // T
    FC = min(FLASH_BWD_CHUNK, n_t)

    def body(q_ref, k_ref, v_ref, o_ref, do_ref, lse_ref, dq_ref, dk_ref, dv_ref, dk_acc, dv_acc):
        j = pl.program_id(1)

        @pl.when(j == 0)
        def _():
            dq_ref[...] = jnp.zeros_like(dq_ref)

        kj, vj = k_ref[...], v_ref[...]
        dk_acc[...] = jnp.zeros_like(dk_acc)
        dv_acc[...] = jnp.zeros_like(dv_acc)

        def step(start, width, masked):
            rows = pl.ds(start, width)
            qi, doi = q_ref[rows, :], do_ref[rows, :]
            delta = jnp.sum(doi.astype(F32) * o_ref[rows, :].astype(F32), axis=-1, keepdims=True)
            s = lax.dot_general(qi, kj, _DIMS['nt'], preferred_element_type=F32)
            p = jnp.exp(s - lse_ref[rows, :][:, :1])
            if masked:
                rr = lax.broadcasted_iota(jnp.int32, (width, T), 0) + start
                cc = lax.broadcasted_iota(jnp.int32, (width, T), 1) + j * T
                p = jnp.where(cc <= rr, p, 0.0)
            dp = lax.dot_general(doi, vj, _DIMS['nt'], preferred_element_type=F32)
            ds = (p * (dp - delta)).astype(BF)
            dv_acc[...] += lax.dot_general(p.astype(BF), doi, _DIMS['tn'], preferred_element_type=F32)
            dk_acc[...] += lax.dot_general(ds, qi, _DIMS['tn'], preferred_element_type=F32)
            dq_ref[rows, :] += jnp.dot(ds, kj, preferred_element_type=F32)

        head = (n_t - 1 - j) % FC + 1
        for nb in range(1, FC + 1):
            @pl.when(head == nb)
            def _(nb=nb):
                step(pl.multiple_of(j * T, T), nb * T, True)

        def full(cidx, carry):
            step(pl.multiple_of((j + head + cidx * FC) * T, T), FC * T, False)
            return carry

        if n_t > FC:
            lax.fori_loop(0, (n_t - j - head) // FC, full, 0)
        dk_ref[...] = dk_acc[...]
        dv_ref[...] = dv_acc[...].astype(dv_ref.dtype)

    head_q = pl.BlockSpec((Lp, HEAD_PAD), lambda h, j: (0, h))
    head_v = pl.BlockSpec((Lp, V_DIM), lambda h, j: (0, h))
    return _call_carrying(
        body, name, (H, n_t),
        [head_q, pl.BlockSpec((T, HEAD_PAD), lambda h, j: (j, h)), pl.BlockSpec((T, V_DIM), lambda h, j: (j, 2 * h + 1)),
         head_v, head_v, pl.BlockSpec((None, Lp, 128), lambda h, j: (h, 0, 0))],
        [head_q, pl.BlockSpec((T, HEAD_PAD), lambda h, j: (j, h)), pl.BlockSpec((T, V_DIM), lambda h, j: (j, h))],
        [jax.ShapeDtypeStruct((Lp, H * HEAD_PAD), F32), jax.ShapeDtypeStruct((Lp, H * HEAD_PAD), F32),
         jax.ShapeDtypeStruct((Lp, H * V_DIM), BF)],
        [pltpu.VMEM((T, HEAD_PAD), F32), pltpu.VMEM((T, V_DIM), F32)], (Q, K, kv, O, dO, lse), phases)


def _gate_fwd(z, y_pool, y_mla, lay, name):
    Lp, D = y_pool.shape
    tm = _div_tile(Lp, 256, 16)

    def body(gp_ref, gm_ref, yp_ref, ym_ref, o_ref):
        o_ref[...] = (jax.nn.sigmoid(gp_ref[...]) * yp_ref[...] + jax.nn.sigmoid(gm_ref[...]) * ym_ref[...]).astype(BF)

    return pl.pallas_call(
        body, name=name, grid=(Lp // tm,),
        in_specs=[_rb(tm, D, lay['gp'] // D), _rb(tm, D, lay['gm'] // D), _rb(tm, D), _rb(tm, D)], out_specs=_rb(tm, D),
        out_shape=jax.ShapeDtypeStruct((Lp, D), BF), compiler_params=_cp(('parallel',)),
    )(z, z, y_pool, y_mla)


def _gate_bwd(dy, z, y_pool, y_mla, lay, name):
    Lp, D = y_pool.shape
    tm = _div_tile(Lp, 256, 16)

    def body(dy_ref, gp_ref, gm_ref, yp_ref, ym_ref, dp_ref, dm_ref, dg_ref):
        dy_ = dy_ref[...]
        sp, sm = jax.nn.sigmoid(gp_ref[...]), jax.nn.sigmoid(gm_ref[...])
        dp_ref[...] = (dy_ * sp).astype(BF)
        dm_ref[...] = (dy_ * sm).astype(BF)
        dg_ref[:, :D] = (dy_ * yp_ref[...] * (sp * (1.0 - sp))).astype(BF)
        dg_ref[:, D:] = (dy_ * ym_ref[...] * (sm * (1.0 - sm))).astype(BF)

    return pl.pallas_call(
        body, name=name, grid=(Lp // tm,),
        in_specs=[_rb(tm, D), _rb(tm, D, lay['gp'] // D), _rb(tm, D, lay['gm'] // D), _rb(tm, D), _rb(tm, D)],
        out_specs=[_rb(tm, D), _rb(tm, D), _rb(tm, 2 * D)],
        out_shape=[jax.ShapeDtypeStruct((Lp, D), BF), jax.ShapeDtypeStruct((Lp, D), BF),
                   jax.ShapeDtypeStruct((Lp, 2 * D), BF)],
        compiler_params=_cp(('parallel',)),
    )(dy, z, z, y_pool, y_mla)


def _z_layout(D, QL, KVL):
    cq = POOL_WIDTH
    ckv = cq + QL
    gp = -(-(ckv + KVL) // D) * D
    gm = gp + D
    kr = gm + D
    return dict(QL=QL, KVL=KVL, cq=cq, ckv=ckv, gp=gp, gm=gm, kr=kr, width=kr + 128)


def _w_in_aligned(wt, lay, D):
    n0 = POOL_WIDTH + lay['QL'] + lay['KVL']
    parts = [wt[:n0], jnp.zeros((lay['gp'] - n0, D), wt.dtype), wt[n0 + QK_ROPE:], wt[n0:n0 + QK_ROPE],
             jnp.zeros((128 - QK_ROPE, D), wt.dtype)]
    return jnp.concatenate(parts, axis=0)


def _w_in_logical(wt_al, lay, D):
    n0 = POOL_WIDTH + lay['QL'] + lay['KVL']
    return jnp.concatenate([wt_al[:n0], wt_al[lay['kr']:lay['kr'] + QK_ROPE], wt_al[lay['gp']:lay['gp'] + 2 * D]],
                           axis=0)


def kernel(x, meta_tokens, norm_ffn1_pre, norm_ffn1_post, ffn1_w_gu, ffn1_w_down, norm_mix_pre, norm_mix_post, w_in, pool_w, pool_scale, w_pool_o, q_a_norm, w_q_b, kv_a_norm, w_kv_b, w_mla_o, w_out, norm_ffn2_pre, norm_ffn2_post, ffn2_w_gu, ffn2_w_down, loss_target, m_meta_tokens, m_norm_ffn1_pre, m_norm_ffn1_post, m_ffn1_w_gu, m_ffn1_w_down, m_norm_mix_pre, m_norm_mix_post, m_w_in, m_pool_w, m_pool_scale, m_w_pool_o, m_q_a_norm, m_w_q_b, m_kv_a_norm, m_w_kv_b, m_w_mla_o, m_w_out, m_norm_ffn2_pre, m_norm_ffn2_post, m_ffn2_w_gu, m_ffn2_w_down, v_meta_tokens, v_norm_ffn1_pre, v_norm_ffn1_post, v_ffn1_w_gu, v_ffn1_w_down, v_norm_mix_pre, v_norm_mix_post, v_w_in, v_pool_w, v_pool_scale, v_w_pool_o, v_q_a_norm, v_w_q_b, v_kv_a_norm, v_w_kv_b, v_w_mla_o, v_w_out, v_norm_ffn2_pre, v_norm_ffn2_post, v_ffn2_w_gu, v_ffn2_w_down):
    given = dict(locals())
    W = {n: given[n] for n in WEIGHTS}
    M = {n: given['m_' + n] for n in WEIGHTS}
    V = {n: given['v_' + n] for n in WEIGHTS}

    S, D = x.shape[1], x.shape[2]
    NM = meta_tokens.shape[0]
    L = NM + S
    Lp = -(-L // ROW_ALIGN) * ROW_ALIGN
    QL, KVL = w_q_b.shape[1], w_kv_b.shape[1]
    H = w_q_b.shape[2] * N_CHIPS // QK_DIM
    lay = _z_layout(D, QL, KVL)
    gx, gy = lax.axis_index('x'), lax.axis_index('y')
    chip = 2 * gx + gy

    names = list(BIG)
    for d in (W, M, V):
        d['w_in'] = jnp.swapaxes(d['w_in'], 1, 2)
    slab = {n: W[n][0] for n in names}
    kind = dict(BIG, meta_tokens='col')
    slab_shape = {n: tuple(slab[n].shape) for n in names}
    gc = lax.axis_index('c')
    c_arr = jnp.stack([gc]).astype(jnp.int32)
    place_arr = jnp.stack([gc, chip]).astype(jnp.int32)

    full = {n: _cast_place(slab[n], BIG[n], place_arr, BF, f'place_{n}') for n in names}
    full['meta_tokens'] = _cast_place(meta_tokens, 'col', place_arr, F32, 'place_meta_tokens')
    G0 = ['ffn1_w_gu', 'meta_tokens']
    G0B = ['ffn1_w_down']
    G1W = ['w_in']
    G1R = ['pool_w', 'w_pool_o', 'w_q_b', 'w_kv_b']
    G2 = ['w_mla_o', 'w_out']
    G3 = ['ffn2_w_gu', 'ffn2_w_down']

    def gather(phase_fn, group):
        return phase_fn([full[n] for n in group], [kind[n] for n in group])

    def arrived(group, res):
        full.update(zip(group, res))

    arrived(G0, _all_gather([full[n] for n in G0], [kind[n] for n in G0]))
    meta_full = full['meta_tokens']

    pos = jnp.arange(Lp, dtype=F32)
    inv = ROPE_THETA ** (-jnp.arange(0, QK_ROPE, 2, dtype=F32) / QK_ROPE)
    ang = pos[:, None] * inv[None, :]
    ang = jnp.concatenate([ang, ang], axis=-1)
    cos = jnp.pad(jnp.cos(ang), ((0, 0), (0, 128 - QK_ROPE)), constant_values=1.0)
    sin = jnp.pad(jnp.sin(ang), ((0, 0), (0, 128 - QK_ROPE)))

    h0 = jnp.concatenate([meta_full, x[0], jnp.zeros((Lp - L, D), F32)], axis=0)
    tgt = jnp.pad(loss_target[0], ((NM, Lp - L), (0, 0)))

    n1 = _norm_fwd(h0, norm_ffn1_pre, 'ffn1_norm')
    gu1, (res_b, res) = _mm(n1, full['ffn1_w_gu'], 'nn', BF, 'ffn1_gu',
                            [gather(_phase_gather_ici, G0B), gather(_phase_gather_ici, G1R)])
    arrived(G0B, res_b)
    arrived(G1R, res)
    def gather_ici(group, peers):
        return _phase_gather_ici([full[n] for n in group], [kind[n] for n in group], peers)

    (a1,), (res_b, res_w) = _swiglu_fwd(gu1, 'ffn1_act', [gather(_phase_gather_d2d, G0B), gather_ici(G1W, (1,))])
    arrived(G0B, res_b)
    arrived(G1W, res_w)
    f1, (res, res_w) = _mm(a1, full['ffn1_w_down'], 'nn', F32, 'ffn1_down',
                           [gather(_phase_gather_d2d, G1R), gather_ici(G1W, (2, 3))])
    arrived(G1R, res)
    arrived(G1W, res_w)
    (h1, n2), (res_w,) = _post_residual(h0, f1, norm_ffn1_post, 0.5, norm_mix_pre, 'ffn1_res',
                                        [gather(_phase_gather_d2d, G1W)])
    arrived(G1W, res_w)

    w_in_al = _w_in_aligned(full['w_in'], lay, D)
    w_q_pad = jnp.pad(full['w_q_b'].reshape(QL, H, QK_DIM), ((0, 0), (0, 0), (0, HEAD_PAD - QK_DIM))).reshape(
        QL, H * HEAD_PAD)

    z, (res,) = _mm(n2, w_in_al, 'nt', F32, 'mix_in', [gather(_phase_gather_ici, G2)])
    arrived(G2, res)
    d_pool = _pool_fwd(z, 'pool_fwd')
    e_pool, yp = _pool_mix_fwd(d_pool, full['pool_w'], pool_scale, 'pool_mix')
    y_pool = _mm(yp, full['w_pool_o'], 'nn', F32, 'pool_out')
    cqn, ckvn = _lora_norms(z, q_a_norm, kv_a_norm, lay, 'lora_norms')
    q_raw = _mm(cqn, w_q_pad, 'nn', BF, 'mla_q')
    kv = _mm(ckvn, full['w_kv_b'], 'nn', BF, 'mla_kv')
    Q, K = _qk_prep(q_raw, kv, z, cos, sin, lay, H, 'qk_prep')
    (O, lse), (res2, res3) = _flash_fwd(Q, K, kv, H, 'flash_fwd',
                                        [gather(_phase_gather_d2d, G2), gather(_phase_gather_ici, G3)])
    arrived(G2, res2)
    arrived(G3, res3)
    y_mla, (res,) = _mm(O, full['w_mla_o'], 'nn', F32, 'mla_out', [gather(_phase_gather_d2d, G3)])
    arrived(G3, res)
    y = _gate_fwd(z, y_pool, y_mla, lay, 'gate')
    m_mix = _mm(y, full['w_out'], 'nn', F32, 'mix_out')
    (h2, n3), _ = _post_residual(h1, m_mix, norm_mix_post, 1.0, norm_ffn2_pre, 'mix_res')

    gu2 = _mm(n3, full['ffn2_w_gu'], 'nn', BF, 'ffn2_gu')
    (a2,), _ = _swiglu_fwd(gu2, 'ffn2_act')
    f2 = _mm(a2, full['ffn2_w_down'], 'nn', F32, 'ffn2_down')

    G, theirs, sums, halves, reduced = {}, {}, {}, {}, {}
    RA = ['ffn2_w_down', 'ffn2_w_gu']
    RB1 = ['w_out', 'w_pool_o', 'pool_w', 'w_mla_o']
    RB2 = ['w_q_b', 'w_kv_b']
    RB3 = ['w_in']
    RC1 = ['ffn1_w_down']
    RC2 = ['ffn1_w_gu']

    def pair_phase(group):
        return _phase_pair([G[n] for n in group], [BIG[n] for n in group], [slab_shape[n] for n in group])

    def pair_sums(group, res):
        for n, t in zip(group, res):
            sums[n] = _pair_sum(G[n], t, BIG[n], slab_shape[n], c_arr, f'rs_pair_sum_{n}')

    def chip_phase(group):
        return _phase_chip([sums[n] for n in group])

    def chip_sums(group, res):
        for n, ld in zip(group, res):
            halves[n] = _chip_sum(sums[n], ld, BIG[n], slab_shape[n], place_arr, f'rs_chip_sum_{n}')

    def final_phase(group):
        return _phase_final([halves[n] for n in group], [BIG[n] for n in group])

    dh3, df2, G['norm_ffn2_post'], sq = _tail(h2, f2, norm_ffn2_post, 0.5, tgt, NM, S, 'ffn2_tail')

    G['ffn2_w_down'] = _mm(a2, df2, 'tn', BF, 'ffn2_dw_down')
    da2 = _mm(df2, full['ffn2_w_down'], 'nt', BF, 'ffn2_da')
    dgu2 = _swiglu_bwd(da2, gu2, 'ffn2_act_bwd')
    G['ffn2_w_gu'] = _mm(n3, dgu2, 'tn', BF, 'ffn2_dw_gu')
    dn3, (res,) = _mm(dgu2, full['ffn2_w_gu'], 'nt', F32, 'ffn2_dn', [pair_phase(RA)])
    (dh2, G['norm_ffn2_pre'], dm, G['norm_mix_post']), _ = _pre_bwd(
        dh3, dn3, h2, norm_ffn2_pre, 'ffn2_norm_bwd', before=(m_mix, norm_mix_post, 1.0))
    pair_sums(RA, res)

    G['w_out'] = _mm(y, dm, 'tn', BF, 'dw_out')
    dy = _mm(dm, full['w_out'], 'nt', F32, 'mix_out_bwd')
    dy_pool, dy_mla, d_gate = _gate_bwd(dy, z, y_pool, y_mla, lay, 'gate_bwd')
    G['w_pool_o'] = _mm(yp, dy_pool, 'tn', BF, 'dw_pool_o')
    dyp = _mm(dy_pool, full['w_pool_o'], 'nt', F32, 'pool_out_bwd')
    dd, G['pool_scale'], d_pool_w = _pool_mix_bwd(dyp, e_pool, d_pool, full['pool_w'], pool_scale, 'pool_mix_bwd')
    G['pool_w'] = d_pool_w.astype(BF)
    du_pool = _pool_bwd(dd, 'pool_bwd')
    G['w_mla_o'] = _mm(O, dy_mla, 'tn', BF, 'dw_mla_o')
    dO, (res,) = _mm(dy_mla, full['w_mla_o'], 'nt', BF, 'mla_out_bwd', [pair_phase(RB1)])
    pair_sums(RB1, res)
    (dQ, dK, dV), (res, res_b) = _flash_bwd(Q, K, kv, O, dO, lse, H, 'flash_bwd', [chip_phase(RA), chip_phase(RB1)])
    chip_sums(RA, res)
    chip_sums(RB1, res_b)
    dq_raw, dkv, dkr = _qk_prep_bwd(dQ, dK, dV, cos, sin, H, 'qk_prep_bwd')
    d_w_q_pad, (res_b,) = _mm(cqn, dq_raw, 'tn', BF, 'dw_q_b', [final_phase(RB1)])
    reduced.update(zip(RB1, res_b))
    G['w_q_b'] = d_w_q_pad.reshape(QL, H, HEAD_PAD)[:, :, :QK_DIM].reshape(QL, H * QK_DIM)
    dcqn = _mm(dq_raw, w_q_pad, 'nt', F32, 'mla_q_bwd')
    G['w_kv_b'] = _mm(ckvn, dkv, 'tn', BF, 'dw_kv_b')
    dckvn, (res,) = _mm(dkv, full['w_kv_b'], 'nt', F32, 'mla_kv_bwd', [pair_phase(RB2)])
    pair_sums(RB2, res)
    d_lora, G['q_a_norm'], G['kv_a_norm'] = _lora_norms_bwd(dcqn, dckvn, z, q_a_norm, kv_a_norm, lay, 'lora_norms_bwd')
    n0 = POOL_WIDTH + QL + KVL
    dz = jnp.concatenate([du_pool, d_lora, jnp.zeros((Lp, lay['gp'] - n0), BF), d_gate, dkr], axis=1)
    d_w_in_al, (res, res_b) = _mm(dz, n2, 'tn', BF, 'dw_in', [final_phase(RA), chip_phase(RB2)])
    reduced.update(zip(RA, res))
    chip_sums(RB2, res_b)
    G['w_in'] = _w_in_logical(d_w_in_al, lay, D)
    dn2, (res, res_b) = _mm(dz, w_in_al, 'nn', F32, 'mix_in_bwd', [pair_phase(RB3), final_phase(RB2)])
    reduced.update(zip(RB2, res_b))
    (dh1, G['norm_mix_pre'], df1, G['norm_ffn1_post']), _ = _pre_bwd(
        dh2, dn2, h1, norm_mix_pre, 'mix_norm_bwd', before=(f1, norm_ffn1_post, 0.5))
    pair_sums(RB3, res)

    sums_b3 = [sums[n] for n in RB3]
    G['ffn1_w_down'], (land_b,) = _mm(a1, df1, 'tn', BF, 'ffn1_dw_down', [_phase_chip(sums_b3, (1, 2))])
    da1, (res, land_b) = _mm(df1, full['ffn1_w_down'], 'nt', BF, 'ffn1_da',
                             [pair_phase(RC1), _phase_chip(sums_b3, (3,), land_b)])
    chip_sums(RB3, land_b)
    dgu1 = _swiglu_bwd(da1, gu1, 'ffn1_act_bwd')
    pair_sums(RC1, res)
    G['ffn1_w_gu'], (res_c1, res_b) = _mm(n1, dgu1, 'tn', BF, 'ffn1_dw_gu', [chip_phase(RC1), final_phase(RB3)])
    chip_sums(RC1, res_c1)
    reduced.update(zip(RB3, res_b))
    (res,) = _run_phases([pair_phase(RC2)], 'rs_pair_exchange_tail')
    pair_sums(RC2, res)
    dn1, (res_c2, res_c1) = _mm(dgu1, full['ffn1_w_gu'], 'nt', F32, 'ffn1_dn', [chip_phase(RC2), final_phase(RC1)])
    chip_sums(RC2, res_c2)
    reduced.update(zip(RC1, res_c1))
    (dh0, G['norm_ffn1_pre']), (res_c2,) = _pre_bwd(dh1, dn1, h0, norm_ffn1_pre, 'ffn1_norm_bwd', [final_phase(RC2)])
    reduced.update(zip(RC2, res_c2))
    grad_x = dh0[NM:L][None]

    SW = max(D, POOL_WIDTH)

    def widen(a, fill=0.0):
        return jnp.pad(a, ((0, 0), (0, SW - a.shape[1])), constant_values=fill)

    rows = [widen(G[n]) for n in SMALL_VEC] + [widen(dh0[:NM]), widen(sq)]
    n_rows = len(SMALL_VEC) + NM + 1
    pad_rows = -(-n_rows // 8) * 8 - n_rows
    small = _all_reduce_small(jnp.concatenate(rows + [jnp.zeros((pad_rows, SW), F32)], axis=0))
    loss = (0.5 / D) * small[len(SMALL_VEC) + NM, 0]
    for i, n in enumerate(SMALL_VEC):
        reduced[n] = small[i:i + 1, :G[n].shape[1]]
    mw = meta_tokens.shape[1]
    reduced['meta_tokens'] = lax.dynamic_slice(small, (len(SMALL_VEC), chip * mw), (NM, mw))

    grads, deltas, new_m, new_v = {}, {}, {}, {}
    for n in names:
        grads[n], deltas[n], new_m[n], new_v[n] = _adam(W[n], reduced[n][None], M[n], V[n], f'adam_{n}')
    for d in (grads, deltas, new_m, new_v):
        d['w_in'] = jnp.swapaxes(d['w_in'], 1, 2)
    n_vec = len(SMALL_VEC)
    vec_w = jnp.concatenate([widen(W[n]) for n in SMALL_VEC] + [jnp.zeros((16 - n_vec, SW), F32)], axis=0)
    vec_m = jnp.concatenate([widen(M[n]) for n in SMALL_VEC] + [jnp.zeros((16 - n_vec, SW), F32)], axis=0)
    vec_v = jnp.concatenate([widen(V[n], 1.0) for n in SMALL_VEC] + [jnp.ones((16 - n_vec, SW), F32)], axis=0)
    vec_g = jnp.concatenate([small[:n_vec], jnp.zeros((16 - n_vec, SW), F32)], axis=0)
    _, vd, vm, vv = _adam(vec_w, vec_g, vec_m, vec_v, 'adam_vectors')
    for i, n in enumerate(SMALL_VEC):
        wdt = W[n].shape[1]
        grads[n], deltas[n], new_m[n], new_v[n] = reduced[n], vd[i:i + 1, :wdt], vm[i:i + 1, :wdt], vv[i:i + 1, :wdt]
    grads['meta_tokens'], deltas['meta_tokens'], new_m['meta_tokens'], new_v['meta_tokens'] = _adam(
        meta_tokens, reduced['meta_tokens'], m_meta_tokens, v_meta_tokens, 'adam_meta')

    return (loss, grad_x, *[grads[n] for n in WEIGHTS], *[deltas[n] for n in WEIGHTS], *[new_m[n] for n in WEIGHTS],
            *[new_v[n] for n in WEIGHTS])
```
